```python
import jax, jax.numpy as jnp
from jax import lax
import numpy as np

D_MODEL = 1024
BATCH = 4
SEQ = 4096
DEPTH = 1

CHUNK = 64
Q_BLOCK = 128
MLA_HEADS = 8
QK_NOPE_DIM = 64
QK_ROPE_DIM = 32
V_HEAD_DIM = 64
Q_LORA_RANK = 384
KV_LORA_RANK = 256
ROPE_THETA = 10000.0
MLA_WIDTH = MLA_HEADS * V_HEAD_DIM
CONV_DIM = D_MODEL - MLA_WIDTH
CONV_WIDTH = 3
IN_PROJ_DIM = Q_LORA_RANK + KV_LORA_RANK + QK_ROPE_DIM + 3 * CONV_DIM
N_EXPERTS = 256
TOP_K = 8
N_EXPERT_GROUPS = 8
TOPK_GROUPS = 4
EXPERT_DIM = 256
SHARED_DIM = 256
ROUTED_SCALE = 2.5
EXPERT_BLOCK = 128
EPS = 1e-6

kernel_name = "hybrid_mla_shortconv_moe_block"


def rms_norm(x, g):
    xf = x.astype(jnp.float32)
    y = xf * lax.rsqrt(jnp.mean(xf * xf, axis=-1, keepdims=True) + EPS)
    return (y * g.astype(jnp.float32)).astype(x.dtype)


def rope_tables(positions):
    inv = 1.0 / (ROPE_THETA ** (jnp.arange(0, QK_ROPE_DIM, 2, dtype=jnp.float32) / QK_ROPE_DIM))
    ang = positions.astype(jnp.float32)[..., None] * inv
    return jnp.cos(ang), jnp.sin(ang)


def apply_rope(x, cos, sin):
    xf = x.astype(jnp.float32)
    x1, x2 = jnp.split(xf, 2, axis=-1)
    return jnp.concatenate([x1 * cos - x2 * sin, x1 * sin + x2 * cos], axis=-1).astype(x.dtype)


def chunk_causal_attention(q, k, v):
    B, S, H, Dqk = q.shape
    Dv = v.shape[-1]
    nq = S // Q_BLOCK
    scale = Dqk ** -0.5
    qb = q.reshape(B, nq, Q_BLOCK, H, Dqk).transpose(1, 0, 3, 2, 4)
    kh = k.transpose(0, 2, 1, 3)
    vh = v.transpose(0, 2, 1, 3)
    k_chunk = jnp.arange(S) // CHUNK

    def block(args):
        q_blk, bi = args
        s = jnp.einsum('bhqd,bhkd->bhqk', q_blk, kh).astype(jnp.float32) * scale
        q_chunk = (bi * Q_BLOCK + jnp.arange(Q_BLOCK)) // CHUNK
        s = jnp.where(k_chunk[None, :] <= q_chunk[:, None], s, -jnp.inf)
        p = jax.nn.softmax(s, axis=-1).astype(vh.dtype)
        return jnp.einsum('bhqk,bhkd->bhqd', p, vh)

    o = lax.map(block, (qb, jnp.arange(nq)))
    return o.transpose(1, 0, 3, 2, 4).reshape(B, S, H * Dv)


def token_mixers(h, positions, w_in, g_q_lat, w_uq, g_kv_lat, w_ukv, w_conv,
                 g_attn_out, g_conv_out, w_out):
    B, S, _ = h.shape
    z = h @ w_in
    cuts = [Q_LORA_RANK,
            Q_LORA_RANK + KV_LORA_RANK,
            Q_LORA_RANK + KV_LORA_RANK + QK_ROPE_DIM,
            Q_LORA_RANK + KV_LORA_RANK + QK_ROPE_DIM + CONV_DIM,
            Q_LORA_RANK + KV_LORA_RANK + QK_ROPE_DIM + 2 * CONV_DIM]
    c_q, c_kv, k_rope, gate_b, gate_c, xv = jnp.split(z, cuts, axis=-1)

    q = (rms_norm(c_q, g_q_lat) @ w_uq).reshape(B, S, MLA_HEADS, QK_NOPE_DIM + QK_ROPE_DIM)
    q_nope, q_rope = q[..., :QK_NOPE_DIM], q[..., QK_NOPE_DIM:]
    kv = (rms_norm(c_kv, g_kv_lat) @ w_ukv).reshape(B, S, MLA_HEADS, QK_NOPE_DIM + V_HEAD_DIM)
    k_nope, v = kv[..., :QK_NOPE_DIM], kv[..., QK_NOPE_DIM:]
    cos, sin = rope_tables(positions)
    q_rope = apply_rope(q_rope, cos[:, :, None, :], sin[:, :, None, :])
    k_rope = apply_rope(k_rope, cos, sin)
    q_full = jnp.concatenate([q_nope, q_rope], axis=-1)
    k_full = jnp.concatenate(
        [k_nope, jnp.broadcast_to(k_rope[:, :, None, :], (B, S, MLA_HEADS, QK_ROPE_DIM))], axis=-1)
    attn = chunk_causal_attention(q_full, k_full, v)

    u = gate_c * xv
    conv = lax.conv_general_dilated(
        u, w_conv[:, None, :].astype(u.dtype), window_strides=(1,),
        padding=[(CONV_WIDTH - 1, 0)], dimension_numbers=('NWC', 'WIO', 'NWC'),
        feature_group_count=CONV_DIM)
    y_conv = gate_b * conv

    mixed = jnp.concatenate([rms_norm(attn, g_attn_out), rms_norm(y_conv, g_conv_out)], axis=-1)
    return mixed @ w_out


def swiglu(h, wg, wu, wd):
    return (jax.nn.silu(h @ wg) * (h @ wu)) @ wd


def moe(h, w_router, b_router, w_gate, w_up, w_down, w_sh_gate, w_sh_up, w_sh_down):
    T, D = h.shape
    scores = jax.nn.sigmoid(h.astype(jnp.float32) @ w_router.astype(jnp.float32))
    sel = (scores + b_router.astype(jnp.float32)).reshape(T, N_EXPERT_GROUPS, N_EXPERTS // N_EXPERT_GROUPS)
    grp_score = lax.top_k(sel, 2)[0].sum(-1)
    _, top_g = lax.top_k(grp_score, TOPK_GROUPS)
    gmask = jax.nn.one_hot(top_g, N_EXPERT_GROUPS, dtype=jnp.float32).sum(1) > 0
    sel = jnp.where(gmask[:, :, None], sel, -jnp.inf).reshape(T, N_EXPERTS)
    _, idx = lax.top_k(sel, TOP_K)
    wts = jnp.take_along_axis(scores, idx, axis=1)
    wts = wts / jnp.sum(wts, axis=-1, keepdims=True) * ROUTED_SCALE

    M = T * TOP_K
    NB = (M + N_EXPERTS * (EXPERT_BLOCK - 1) + EXPERT_BLOCK - 1) // EXPERT_BLOCK
    P = NB * EXPERT_BLOCK
    flat_e = idx.reshape(-1)
    flat_tok = jnp.repeat(jnp.arange(T, dtype=jnp.int32), TOP_K)
    flat_w = wts.reshape(-1)
    order = jnp.argsort(flat_e)
    sorted_e = flat_e[order]
    counts = jnp.bincount(flat_e, length=N_EXPERTS)
    padded = ((counts + EXPERT_BLOCK - 1) // EXPERT_BLOCK) * EXPERT_BLOCK
    pad_end = jnp.cumsum(padded)
    pad_start = pad_end - padded
    start = jnp.cumsum(counts) - counts
    dest = pad_start[sorted_e] + (jnp.arange(M) - start[sorted_e])
    buf_tok = jnp.zeros((P,), jnp.int32).at[dest].set(flat_tok[order])
    buf_w = jnp.zeros((P,), jnp.float32).at[dest].set(flat_w[order])
    blk_exp = jnp.minimum(
        jnp.searchsorted(pad_end, jnp.arange(NB) * EXPERT_BLOCK, side='right'), N_EXPERTS - 1)

    def expert_block(args):
        tok, e, wt = args
        xb = h[tok]
        y = swiglu(xb, w_gate[e], w_up[e], w_down[e])
        return y * wt[:, None].astype(y.dtype)

    ys = lax.map(expert_block, (buf_tok.reshape(NB, EXPERT_BLOCK), blk_exp,
                                buf_w.reshape(NB, EXPERT_BLOCK)))
    routed = jax.ops.segment_sum(ys.reshape(P, D), buf_tok, num_segments=T)
    return routed + swiglu(h, w_sh_gate, w_sh_up, w_sh_down)


def setup_inputs(seed: int = 0) -> dict:
    key = jax.random.key(seed)
    ks = jax.random.split(key, 32)
    f32 = jnp.float32

    def nrm(k, shape, scale):
        return jax.random.normal(k, shape, f32) * scale

    def gain(k, n):
        return 1.0 + 0.05 * jax.random.normal(k, (DEPTH, n), f32)

    D = D_MODEL
    offsets = jax.random.randint(ks[2], (BATCH, 1), 0, 4096, dtype=jnp.int32)
    positions = offsets + jnp.arange(SEQ, dtype=jnp.int32)[None, :]
    return {
        "x": nrm(ks[0], (BATCH, SEQ, D), 1.0),
        "c": nrm(ks[1], (BATCH, D), 1.0),
        "positions": positions,
        "w_ada": nrm(ks[3], (DEPTH, D, 6 * D), 0.5 * D ** -0.5),
        "b_ada": nrm(ks[4], (DEPTH, 6 * D), 0.01),
        "g_pre_mix": gain(ks[5], D),
        "w_in": nrm(ks[6], (DEPTH, D, IN_PROJ_DIM), D ** -0.5),
        "g_q_lat": gain(ks[7], Q_LORA_RANK),
        "w_uq": nrm(ks[8], (DEPTH, Q_LORA_RANK, MLA_HEADS * (QK_NOPE_DIM + QK_ROPE_DIM)), Q_LORA_RANK ** -0.5),
        "g_kv_lat": gain(ks[9], KV_LORA_RANK),
        "w_ukv": nrm(ks[10], (DEPTH, KV_LORA_RANK, MLA_HEADS * (QK_NOPE_DIM + V_HEAD_DIM)), KV_LORA_RANK ** -0.5),
        "w_conv": nrm(ks[11], (DEPTH, CONV_WIDTH, CONV_DIM), CONV_WIDTH ** -0.5),
        "g_attn_out": gain(ks[12], MLA_WIDTH),
        "g_conv_out": gain(ks[13], CONV_DIM),
        "w_out": nrm(ks[14], (DEPTH, D, D), D ** -0.5),
        "g_post_mix": gain(ks[15], D),
        "g_pre_ffn": gain(ks[16], D),
        "w_router": nrm(ks[17], (DEPTH, D, N_EXPERTS), D ** -0.5),
        "b_router": nrm(ks[18], (DEPTH, N_EXPERTS), 0.01),
        "w_gate": nrm(ks[19], (DEPTH, N_EXPERTS, D, EXPERT_DIM), D ** -0.5),
        "w_up": nrm(ks[20], (DEPTH, N_EXPERTS, D, EXPERT_DIM), D ** -0.5),
        "w_down": nrm(ks[21], (DEPTH, N_EXPERTS, EXPERT_DIM, D), EXPERT_DIM ** -0.5),
        "w_sh_gate": nrm(ks[22], (DEPTH, D, SHARED_DIM), D ** -0.5),
        "w_sh_up": nrm(ks[23], (DEPTH, D, SHARED_DIM), D ** -0.5),
        "w_sh_down": nrm(ks[24], (DEPTH, SHARED_DIM, D), SHARED_DIM ** -0.5),
        "g_post_ffn": gain(ks[25], D),
    }


def reference(x, c, positions, w_ada, b_ada, g_pre_mix, w_in, g_q_lat, w_uq, g_kv_lat, w_ukv,
              w_conv, g_attn_out, g_conv_out, w_out, g_post_mix, g_pre_ffn, w_router, b_router,
              w_gate, w_up, w_down, w_sh_gate, w_sh_up, w_sh_down, g_post_ffn):
    B, S, D = x.shape
    for l in range(DEPTH):
        mod = jax.nn.silu(c) @ w_ada[l] + b_ada[l]
        sh1, sc1, g1, sh2, sc2, g2 = [m[:, None, :] for m in jnp.split(mod, 6, axis=-1)]
        h = rms_norm(x, g_pre_mix[l]) * (1.0 + sc1) + sh1
        mix = token_mixers(h, positions, w_in[l], g_q_lat[l], w_uq[l], g_kv_lat[l], w_ukv[l],
                           w_conv[l], g_attn_out[l], g_conv_out[l], w_out[l])
        x = x + g1 * rms_norm(mix, g_post_mix[l])
        h = rms_norm(x, g_pre_ffn[l]) * (1.0 + sc2) + sh2
        f = moe(h.reshape(B * S, D), w_router[l], b_router[l], w_gate[l], w_up[l], w_down[l],
                w_sh_gate[l], w_sh_up[l], w_sh_down[l]).reshape(B, S, D)
        x = x + g2 * rms_norm(f, g_post_ffn[l])
    return x
```

```python
import functools

import jax
import jax.numpy as jnp
import numpy as np
from jax import lax
from jax.experimental import pallas as pl
from jax.experimental.pallas import tpu as pltpu

F32 = jnp.float32
BF16 = jnp.bfloat16
I32 = jnp.int32
U32 = jnp.uint32

CHUNK = 64
MLA_HEADS = 8
QK_NOPE_DIM = 64
QK_ROPE_DIM = 32
V_HEAD_DIM = 64
Q_LORA_RANK = 384
KV_LORA_RANK = 256
ROPE_THETA = 10000.0
CONV_WIDTH = 3
N_EXPERTS = 256
TOP_K = 8
N_EXPERT_GROUPS = 8
TOPK_GROUPS = 4
EXPERT_DIM = 256
ROUTED_SCALE = 2.5
EPS = 1e-6

LANES = 128
SUBLANES = 8
HEAD_PAD = LANES
VMEM_LIMIT_BYTES = 56 * 1024 * 1024

TM_IN = 256
TQ_ATTN = 512
TM_OUT = 256
TM_DEST = 512
TM_DISPATCH = 512
BM_EXPERT = 256
TM_COMBINE = 256

NEG_INF = float("-inf")


def _rms(x, g):
    return x * lax.rsqrt(jnp.mean(x * x, axis=-1, keepdims=True) + EPS) * g


_HI_MASK = np.uint32(0xFFFF0000)
_ROW_WORDS = 512
_ROW_SLABS = _ROW_WORDS // LANES


def _pack_row_words(lo, hi):
    lo_w = lax.bitcast_convert_type(lo.astype(BF16).astype(F32), U32) >> 16
    hi_w = lax.bitcast_convert_type(hi.astype(BF16).astype(F32), U32) & _HI_MASK
    return lo_w | hi_w


def _unpack_row_words(w):
    return (lax.bitcast_convert_type(w << 16, F32), lax.bitcast_convert_type(w & _HI_MASK, F32))


def _params(sem):
    return pltpu.CompilerParams(dimension_semantics=sem, vmem_limit_bytes=VMEM_LIMIT_BYTES)


def _ada_kernel(c_ref, w_ref, b_ref, o_ref):
    c = c_ref[...]
    s = c * jax.nn.sigmoid(c)
    o_ref[...] = jnp.dot(s, w_ref[...], preferred_element_type=F32,
                         precision=lax.Precision.HIGHEST) + b_ref[...]


def _ada(c_pad, w, b):
    rows, d = c_pad.shape
    n = w.shape[1]
    tn = 1536
    return pl.pallas_call(
        _ada_kernel,
        grid=(n // tn,),
        in_specs=[pl.BlockSpec((rows, d), lambda j: (0, 0)),
                  pl.BlockSpec((d, tn), lambda j: (0, j)),
                  pl.BlockSpec((1, tn), lambda j: (0, j))],
        out_specs=pl.BlockSpec((rows, tn), lambda j: (0, j)),
        out_shape=jax.ShapeDtypeStruct((rows, n), F32),
        compiler_params=_params(("arbitrary",)),
        name="ada",
    )(c_pad, w, b)


_CQ0, _CQ1 = 0, Q_LORA_RANK
_CKV0, _CKV1 = _CQ1, _CQ1 + KV_LORA_RANK
_KR0, _KR1 = _CKV1, _CKV1 + 2 * HEAD_PAD
_CONV_DIM = 512
_GB0 = _KR1
_GC0 = _GB0 + _CONV_DIM
_XV0 = _GC0 + _CONV_DIM
_WIN_COLS = _XV0 + _CONV_DIM
_QW = MLA_HEADS * HEAD_PAD


def _mix_in_kernel(tiles_per_batch, x_ref, mod_ref, gpre_ref, win_ref, gq_ref, wuq_ref, gkv_ref,
                   wukv_ref, vone_ref, wconv_ref, gconv_ref, cos_ref, sin_ref,
                   q_ref, k_ref, v_ref, yc_ref, h_scr, u_scr):
    i = pl.program_id(0)
    tm = x_ref.shape[0]
    sh1 = mod_ref[0, 0:1, :]
    sc1 = mod_ref[0, 1:2, :]
    h = _rms(x_ref[...], gpre_ref[...]) * (1.0 + sc1) + sh1
    h_scr[...] = h.astype(BF16)
    cos = cos_ref[...]
    sin = sin_ref[...]

    cq = jnp.dot(h_scr[...], win_ref[:, _CQ0:_CQ1], preferred_element_type=F32)
    cqn = _rms(cq, gq_ref[...]).astype(BF16)
    qq = jnp.dot(cqn, wuq_ref[...], preferred_element_type=F32)
    for hd in range(MLA_HEADS):
        lo = hd * HEAD_PAD
        qh = qq[:, lo:lo + HEAD_PAD] * cos + qq[:, _QW + lo:_QW + lo + HEAD_PAD] * sin
        q_ref[:, lo:lo + HEAD_PAD] = qh.astype(BF16)

    ckv = jnp.dot(h_scr[...], win_ref[:, _CKV0:_CKV1], preferred_element_type=F32)
    ckvn = _rms(ckv, gkv_ref[...]).astype(BF16)
    kv = jnp.dot(ckvn, wukv_ref[...], preferred_element_type=F32)
    krr = jnp.dot(h_scr[...], win_ref[:, _KR0:_KR1], preferred_element_type=F32)
    kr = krr[:, 0:HEAD_PAD] * cos + krr[:, HEAD_PAD:2 * HEAD_PAD] * sin
    vone = vone_ref[...]
    for hd in range(MLA_HEADS):
        lo = hd * HEAD_PAD
        k_ref[:, lo:lo + HEAD_PAD] = (kv[:, lo:lo + HEAD_PAD] + kr).astype(BF16)
        v_ref[:, lo:lo + HEAD_PAD] = (kv[:, _QW + lo:_QW + lo + HEAD_PAD] + vone).astype(BF16)

    gb = jnp.dot(h_scr[...], win_ref[:, _GB0:_GC0], preferred_element_type=F32)
    gc = jnp.dot(h_scr[...], win_ref[:, _GC0:_XV0], preferred_element_type=F32)
    xv = jnp.dot(h_scr[...], win_ref[:, _XV0:_WIN_COLS], preferred_element_type=F32)
    u = gc * xv
    prev = u_scr[tm:tm + SUBLANES, :]
    first = (i % tiles_per_batch) == 0
    u_scr[0:SUBLANES, :] = jnp.where(first, jnp.zeros_like(prev), prev)
    u_scr[SUBLANES:tm + SUBLANES, :] = u
    um1 = u_scr[SUBLANES - 1:tm + SUBLANES - 1, :]
    um2 = u_scr[SUBLANES - 2:tm + SUBLANES - 2, :]
    conv = wconv_ref[0:1, :] * um2 + wconv_ref[1:2, :] * um1 + wconv_ref[2:3, :] * u
    yc_ref[...] = _rms(gb * conv, gconv_ref[...]).astype(BF16)


def _mix_in(x2, mod3, gpre, win_p, gq, wuq_p, gkv, wukv_p, vone, wconv, gconv, cos_t, sin_t, seq):
    t, d = x2.shape
    tm = min(TM_IN, seq)
    tpb = seq // tm
    full = lambda a: pl.BlockSpec(a.shape, lambda i: (0,) * a.ndim)
    row = lambda w: pl.BlockSpec((tm, w), lambda i: (i, 0))
    return pl.pallas_call(
        functools.partial(_mix_in_kernel, tpb),
        grid=(t // tm,),
        in_specs=[row(d),
                  pl.BlockSpec((1, 6, d), lambda i: (i // tpb, 0, 0)),
                  full(gpre), full(win_p), full(gq), full(wuq_p), full(gkv), full(wukv_p),
                  full(vone), full(wconv), full(gconv), row(HEAD_PAD), row(HEAD_PAD)],
        out_specs=[row(_QW), row(_QW), row(_QW), row(_CONV_DIM)],
        out_shape=[jax.ShapeDtypeStruct((t, _QW), BF16), jax.ShapeDtypeStruct((t, _QW), BF16),
                   jax.ShapeDtypeStruct((t, _QW), BF16), jax.ShapeDtypeStruct((t, _CONV_DIM), BF16)],
        scratch_shapes=[pltpu.VMEM((tm, d), BF16), pltpu.VMEM((tm + SUBLANES, _CONV_DIM), F32)],
        compiler_params=_params(("arbitrary",)),
        name="mix_in",
    )(x2, mod3, gpre, win_p, gq, wuq_p, gkv, wukv_p, vone, wconv, gconv, cos_t, sin_t)


_HEADS_PER_STEP = 2


def _attn_kernel(q_ref, k_ref, v_ref, o_ref, s_scr, m_scr, acc_scr):
    qi = pl.program_id(2)
    tq = q_ref.shape[0]
    tk = tq
    lane_groups = tk // LANES

    def tile_max(s):
        m = s[:, 0:LANES]
        for g in range(1, lane_groups):
            m = jnp.maximum(m, s[:, g * LANES:(g + 1) * LANES])
        return m

    for hh in range(_HEADS_PER_STEP):
        lo = hh * HEAD_PAD
        q = q_ref[:, lo:lo + HEAD_PAD]

        def scores(kv):
            off = pl.multiple_of(kv * tk, tk)
            kt = k_ref[pl.ds(off, tk), lo:lo + HEAD_PAD]
            return lax.dot_general(q, kt, (((1,), (1,)), ((), ())), preferred_element_type=F32)

        m_scr[...] = jnp.full(m_scr.shape, NEG_INF, F32)

        def pass1(kv, carry):
            s = scores(kv)
            s_scr[kv] = s
            m_scr[...] = jnp.maximum(m_scr[...], tile_max(s))
            return carry

        lax.fori_loop(0, qi, pass1, 0)
        s = scores(qi)
        rc = lax.broadcasted_iota(I32, (tq, tk), 0) // CHUNK
        cc = lax.broadcasted_iota(I32, (tq, tk), 1) // CHUNK
        s = jnp.where(cc <= rc, s, NEG_INF)
        s_scr[qi] = s
        m_row = jnp.max(jnp.maximum(m_scr[...], tile_max(s)), axis=1, keepdims=True)

        acc_scr[...] = jnp.zeros(acc_scr.shape, F32)

        def pass2(kv, carry):
            off = pl.multiple_of(kv * tk, tk)
            p = jnp.exp(s_scr[kv] - m_row).astype(BF16)
            vt = v_ref[pl.ds(off, tk), lo:lo + HEAD_PAD]
            acc_scr[...] += jnp.dot(p, vt, preferred_element_type=F32)
            return carry

        lax.fori_loop(0, qi + 1, pass2, 0)
        acc = acc_scr[...]
        o = acc[:, 0:V_HEAD_DIM] / acc[:, V_HEAD_DIM:V_HEAD_DIM + 1]
        o_ref[:, hh * V_HEAD_DIM:(hh + 1) * V_HEAD_DIM] = o.astype(BF16)


def _attention(q, k, v, batch, seq):
    t = q.shape[0]
    tq = min(TQ_ATTN, seq)
    nq = seq // tq
    hw = _HEADS_PER_STEP * HEAD_PAD
    ow = _HEADS_PER_STEP * V_HEAD_DIM
    return pl.pallas_call(
        _attn_kernel,
        grid=(batch, MLA_HEADS // _HEADS_PER_STEP, nq),
        in_specs=[pl.BlockSpec((tq, hw), lambda b, j, i: (b * nq + i, j)),
                  pl.BlockSpec((seq, hw), lambda b, j, i: (b, j)),
                  pl.BlockSpec((seq, hw), lambda b, j, i: (b, j))],
        out_specs=pl.BlockSpec((tq, ow), lambda b, j, i: (b * nq + i, j)),
        out_shape=jax.ShapeDtypeStruct((t, MLA_HEADS * V_HEAD_DIM), BF16),
        scratch_shapes=[pltpu.VMEM((nq, tq, tq), F32), pltpu.VMEM((tq, LANES), F32),
                        pltpu.VMEM((tq, HEAD_PAD), F32)],
        compiler_params=_params(("arbitrary", "arbitrary", "arbitrary")),
        name="attn",
    )(q, k, v)


_GROUP_SIZE = N_EXPERTS // N_EXPERT_GROUPS
_BIG = 1.0e9


def _mix_out_kernel(attn_ref, yc_ref, x_ref, mod_ref, gattn_ref, wout_ref, gpost_ref, gpre2_ref,
                    wrt_ref, br_ref, x1_ref, h2_ref, h2p_ref, idx_ref, wts_ref, rank_ref, cnt_ref,
                    carry_scr):
    i = pl.program_id(0)
    tm = x_ref.shape[0]
    half = attn_ref.shape[1]

    @pl.when(i == 0)
    def _():
        carry_scr[...] = jnp.zeros(carry_scr.shape, F32)

    an = _rms(attn_ref[...].astype(F32), gattn_ref[...]).astype(BF16)
    mix = (jnp.dot(an, wout_ref[0:half, :], preferred_element_type=F32)
           + jnp.dot(yc_ref[...], wout_ref[half:, :], preferred_element_type=F32))
    g1 = mod_ref[0, 2:3, :]
    sh2 = mod_ref[0, 3:4, :]
    sc2 = mod_ref[0, 4:5, :]
    x1 = x_ref[...] + g1 * _rms(mix, gpost_ref[...])
    x1_ref[...] = x1
    h2 = _rms(x1, gpre2_ref[...]) * (1.0 + sc2) + sh2
    h2_ref[...] = h2.astype(BF16)
    words = _pack_row_words(h2[:, 0:_ROW_WORDS], h2[:, _ROW_WORDS:])
    for sl in range(_ROW_SLABS):
        h2p_ref[:, sl, :] = words[:, sl * LANES:(sl + 1) * LANES]

    logits = lax.dot_general(wrt_ref[...], h2, (((1,), (1,)), ((), ())),
                             preferred_element_type=F32, precision=lax.Precision.HIGHEST)
    scores = jax.nn.sigmoid(logits)
    sel = scores + br_ref[...]
    row = lax.broadcasted_iota(I32, (N_EXPERTS, tm), 0).astype(F32)

    gscore = []
    rw = lax.broadcasted_iota(I32, (_GROUP_SIZE, tm), 0).astype(F32)
    for g in range(N_EXPERT_GROUPS):
        blk = sel[g * _GROUP_SIZE:(g + 1) * _GROUP_SIZE, :]
        m1 = jnp.max(blk, axis=0, keepdims=True)
        i1 = jnp.min(jnp.where(blk == m1, rw, _BIG), axis=0, keepdims=True)
        m2 = jnp.max(jnp.where(rw == i1, NEG_INF, blk), axis=0, keepdims=True)
        gscore.append(m1 + m2)

    gkeep = [jnp.zeros((1, tm), F32) for _ in range(N_EXPERT_GROUPS)]
    for _ in range(TOPK_GROUPS):
        mg = functools.reduce(jnp.maximum, gscore)
        ig = functools.reduce(jnp.minimum, [jnp.where(gscore[g] == mg, float(g), _BIG)
                                            for g in range(N_EXPERT_GROUPS)])
        for g in range(N_EXPERT_GROUPS):
            hit = ig == float(g)
            gkeep[g] = jnp.where(hit, 1.0, gkeep[g])
            gscore[g] = jnp.where(hit, NEG_INF, gscore[g])
    cur = jnp.concatenate(
        [jnp.where(gkeep[g] > 0.0, sel[g * _GROUP_SIZE:(g + 1) * _GROUP_SIZE, :], NEG_INF)
         for g in range(N_EXPERT_GROUPS)], axis=0)

    krow = lax.broadcasted_iota(I32, (TOP_K, tm), 0)
    idx_rows = []
    idx_f = jnp.zeros((TOP_K, tm), F32)
    sc_k = jnp.zeros((TOP_K, tm), F32)
    sc_sum = jnp.zeros((1, tm), F32)
    onehot = jnp.zeros((N_EXPERTS, tm), F32)
    for k in range(TOP_K):
        m = jnp.max(cur, axis=0, keepdims=True)
        ik = jnp.min(jnp.where(cur == m, row, _BIG), axis=0, keepdims=True)
        hit = row == ik
        sk = jnp.sum(jnp.where(hit, scores, 0.0), axis=0, keepdims=True)
        cur = jnp.where(hit, NEG_INF, cur)
        onehot = jnp.where(hit, 1.0, onehot)
        idx_rows.append(ik)
        idx_f = jnp.where(krow == k, ik, idx_f)
        sc_k = jnp.where(krow == k, sk, sc_k)
        sc_sum = sc_sum + sk
    wts_ref[...] = sc_k / sc_sum * ROUTED_SCALE
    idx_ref[...] = idx_f.astype(I32)

    tri = (lax.broadcasted_iota(I32, (tm, tm), 0) < lax.broadcasted_iota(I32, (tm, tm), 1))
    excl = jnp.dot(onehot.astype(BF16), tri.astype(BF16), preferred_element_type=F32)
    rank_e = carry_scr[:, 0:1] + excl
    rank_k = jnp.zeros((TOP_K, tm), F32)
    for k in range(TOP_K):
        hit = row == idx_rows[k]
        rk = jnp.sum(jnp.where(hit, rank_e, 0.0), axis=0, keepdims=True)
        rank_k = jnp.where(krow == k, rk, rank_k)
    rank_ref[...] = rank_k.astype(I32)
    carry_scr[...] = carry_scr[...] + jnp.sum(onehot, axis=1, keepdims=True)
    cnt_ref[...] = carry_scr[...]


def _mix_out(attn, yc, x2, mod3, gattn, wout, gpost, gpre2, wrt, br, seq):
    t, d = x2.shape
    tm = min(TM_OUT, seq)
    tpb = seq // tm
    full = lambda a: pl.BlockSpec(a.shape, lambda i: (0,) * a.ndim)
    row = lambda w: pl.BlockSpec((tm, w), lambda i: (i, 0))
    col = pl.BlockSpec((TOP_K, tm), lambda i: (0, i))
    return pl.pallas_call(
        _mix_out_kernel,
        grid=(t // tm,),
        in_specs=[row(attn.shape[1]), row(yc.shape[1]), row(d),
                  pl.BlockSpec((1, 6, d), lambda i: (i // tpb, 0, 0)),
                  full(gattn), full(wout), full(gpost), full(gpre2), full(wrt), full(br)],
        out_specs=[row(d), row(d), pl.BlockSpec((tm, _ROW_SLABS, LANES), lambda i: (i, 0, 0)), col, col, col,
                   pl.BlockSpec((N_EXPERTS, LANES), lambda i: (0, 0))],
        out_shape=[jax.ShapeDtypeStruct((t, d), F32), jax.ShapeDtypeStruct((t, d), BF16),
                   jax.ShapeDtypeStruct((t, _ROW_SLABS, LANES), U32),
                   jax.ShapeDtypeStruct((TOP_K, t), I32), jax.ShapeDtypeStruct((TOP_K, t), F32),
                   jax.ShapeDtypeStruct((TOP_K, t), I32),
                   jax.ShapeDtypeStruct((N_EXPERTS, LANES), F32)],
        scratch_shapes=[pltpu.VMEM((N_EXPERTS, LANES), F32)],
        compiler_params=_params(("arbitrary",)),
        name="mix_out",
    )(attn, yc, x2, mod3, gattn, wout, gpost, gpre2, wrt, br)


def _dest_kernel(idx_ref, rank_ref, pstart_ref, dest_ref):
    tm = idx_ref.shape[1]
    row = lax.broadcasted_iota(I32, (N_EXPERTS, tm), 0)
    krow = lax.broadcasted_iota(I32, (TOP_K, tm), 0)
    pstart = pstart_ref[...]
    idx = idx_ref[...]
    out = jnp.zeros((TOP_K, tm), F32)
    for k in range(TOP_K):
        hit = row == idx[k:k + 1, :]
        base = jnp.sum(jnp.where(hit, pstart, 0.0), axis=0, keepdims=True)
        out = jnp.where(krow == k, base, out)
    dest_ref[...] = out.astype(I32) + rank_ref[...]


def _dest(idx, rank, pstart):
    t = idx.shape[1]
    tm = min(TM_DEST, t)
    col = pl.BlockSpec((TOP_K, tm), lambda i: (0, i))
    return pl.pallas_call(
        _dest_kernel,
        grid=(t // tm,),
        in_specs=[col, col, pl.BlockSpec((N_EXPERTS, 1), lambda i: (0, 0))],
        out_specs=col,
        out_shape=jax.ShapeDtypeStruct((TOP_K, t), I32),
        compiler_params=_params(("arbitrary",)),
        name="dest",
    )(idx, rank, pstart)


def _dispatch_kernel(dest_ref, h2_hbm, xs_in_hbm, xs_hbm, sem):
    del xs_in_hbm
    i = pl.program_id(0)
    tm = dest_ref.shape[1]

    def row_copy(t, k):
        return pltpu.make_async_copy(h2_hbm.at[i * tm + t], xs_hbm.at[dest_ref[k, t]], sem)

    def issue(t, carry):
        for k in range(TOP_K):
            row_copy(t, k).start()
        return carry

    lax.fori_loop(0, tm, issue, 0)

    def drain(t, carry):
        for k in range(TOP_K):
            row_copy(t, k).wait()
        return carry

    lax.fori_loop(0, tm, drain, 0)


def _dispatch(dest, h2_rows, xs_zero):
    t = h2_rows.shape[0]
    tm = min(TM_DISPATCH, t)
    return pl.pallas_call(
        _dispatch_kernel,
        grid=(t // tm,),
        in_specs=[pl.BlockSpec((TOP_K, tm), lambda i: (0, i), memory_space=pltpu.SMEM),
                  pl.BlockSpec(memory_space=pl.ANY),
                  pl.BlockSpec(memory_space=pl.ANY)],
        out_specs=pl.BlockSpec(memory_space=pl.ANY),
        out_shape=jax.ShapeDtypeStruct(xs_zero.shape, xs_zero.dtype),
        scratch_shapes=[pltpu.SemaphoreType.DMA],
        input_output_aliases={2: 0},
        compiler_params=_params(("arbitrary",)),
        name="dispatch",
    )(dest, h2_rows, xs_zero)


def _expert_kernel(blk_exp_ref, nused_ref, xs_ref, wg_ref, wu_ref, wd_ref, ys_ref, wgu_scr, wd_scr, x_scr):
    i = pl.program_id(0)
    prev = blk_exp_ref[jnp.maximum(i - 1, 0)]
    changed = jnp.logical_or(i == 0, blk_exp_ref[i] != prev)

    @pl.when(changed)
    def _():
        wgu_scr[:, 0:EXPERT_DIM] = wg_ref[0].astype(BF16)
        wgu_scr[:, EXPERT_DIM:2 * EXPERT_DIM] = wu_ref[0].astype(BF16)
        wd_scr[...] = wd_ref[0].astype(BF16)

    @pl.when(i < nused_ref[0])
    def _():
        for sl in range(_ROW_SLABS):
            lo, hi = _unpack_row_words(xs_ref[:, sl, :])
            x_scr[:, sl * LANES:(sl + 1) * LANES] = lo.astype(BF16)
            x_scr[:, _ROW_WORDS + sl * LANES:_ROW_WORDS + (sl + 1) * LANES] = hi.astype(BF16)
        gu = jnp.dot(x_scr[...], wgu_scr[...], preferred_element_type=F32)
        g = gu[:, 0:EXPERT_DIM]
        a = (g * jax.nn.sigmoid(g) * gu[:, EXPERT_DIM:2 * EXPERT_DIM]).astype(BF16)
        y = jnp.dot(a, wd_scr[...], preferred_element_type=F32)
        words = _pack_row_words(y[:, 0:_ROW_WORDS], y[:, _ROW_WORDS:])
        for sl in range(_ROW_SLABS):
            ys_ref[:, sl, :] = words[:, sl * LANES:(sl + 1) * LANES]


def _experts(blk_exp, nused, xs, w_gate, w_up, w_down):
    p = xs.shape[0]
    d = w_gate.shape[1]
    nb = p // BM_EXPERT
    blk = lambda i, be, nu: (jnp.minimum(i, nu[0] - 1), 0, 0)
    wsel = lambda i, be, nu: (be[i], 0, 0)
    return pl.pallas_call(
        _expert_kernel,
        grid_spec=pltpu.PrefetchScalarGridSpec(
            num_scalar_prefetch=2,
            grid=(nb,),
            in_specs=[pl.BlockSpec((BM_EXPERT, _ROW_SLABS, LANES), blk),
                      pl.BlockSpec((1, d, EXPERT_DIM), wsel),
                      pl.BlockSpec((1, d, EXPERT_DIM), wsel),
                      pl.BlockSpec((1, EXPERT_DIM, d), wsel)],
            out_specs=pl.BlockSpec((BM_EXPERT, _ROW_SLABS, LANES), blk),
            scratch_shapes=[pltpu.VMEM((d, 2 * EXPERT_DIM), BF16), pltpu.VMEM((EXPERT_DIM, d), BF16),
                            pltpu.VMEM((BM_EXPERT, d), BF16)]),
        out_shape=jax.ShapeDtypeStruct((p, _ROW_SLABS, LANES), U32),
        compiler_params=_params(("arbitrary",)),
        name="experts",
    )(blk_exp, nused, xs, w_gate, w_up, w_down)


def _combine_kernel(dest_ref, wts_ref, h2_ref, x1_ref, mod_ref, wsg_ref, wsu_ref, wsd_ref, gpost_ref,
                    ys_hbm, o_ref, yg_scr, sem):
    tm = x1_ref.shape[0]

    def row_copy(t, k):
        return pltpu.make_async_copy(ys_hbm.at[dest_ref[k, t]], yg_scr.at[k, t], sem)

    def issue(t, carry):
        for k in range(TOP_K):
            row_copy(t, k).start()
        return carry

    lax.fori_loop(0, tm, issue, 0)

    h2 = h2_ref[...]
    g = jnp.dot(h2, wsg_ref[...], preferred_element_type=F32)
    u = jnp.dot(h2, wsu_ref[...], preferred_element_type=F32)
    f = jnp.dot((g * jax.nn.sigmoid(g) * u).astype(BF16), wsd_ref[...], preferred_element_type=F32)

    def drain(t, carry):
        for k in range(TOP_K):
            row_copy(t, k).wait()
        return carry

    lax.fori_loop(0, tm, drain, 0)

    wts = wts_ref[...]
    los = [f[:, sl * LANES:(sl + 1) * LANES] for sl in range(_ROW_SLABS)]
    his = [f[:, _ROW_WORDS + sl * LANES:_ROW_WORDS + (sl + 1) * LANES] for sl in range(_ROW_SLABS)]
    for k in range(TOP_K):
        wk = wts[:, k:k + 1]
        for sl in range(_ROW_SLABS):
            lo, hi = _unpack_row_words(yg_scr[k, :, sl, :])
            los[sl] = los[sl] + wk * lo
            his[sl] = his[sl] + wk * hi
    f = jnp.concatenate(los + his, axis=1)
    g2 = mod_ref[0, 5:6, :]
    o_ref[...] = x1_ref[...] + g2 * _rms(f, gpost_ref[...])


def _combine(dest, wts_t, h2, x1, mod3, wsg, wsu, wsd, gpost, ys_rows, seq):
    t, d = x1.shape
    tm = min(TM_COMBINE, seq)
    tpb = seq // tm
    full = lambda a: pl.BlockSpec(a.shape, lambda i: (0,) * a.ndim)
    row = lambda w: pl.BlockSpec((tm, w), lambda i: (i, 0))
    return pl.pallas_call(
        _combine_kernel,
        grid=(t // tm,),
        in_specs=[pl.BlockSpec((TOP_K, tm), lambda i: (0, i), memory_space=pltpu.SMEM),
                  row(TOP_K), row(d), row(d),
                  pl.BlockSpec((1, 6, d), lambda i: (i // tpb, 0, 0)),
                  full(wsg), full(wsu), full(wsd), full(gpost),
                  pl.BlockSpec(memory_space=pl.ANY)],
        out_specs=row(d),
        out_shape=jax.ShapeDtypeStruct((t, d), F32),
        scratch_shapes=[pltpu.VMEM((TOP_K, tm, _ROW_SLABS, LANES), U32), pltpu.SemaphoreType.DMA],
        compiler_params=_params(("arbitrary",)),
        name="combine",
    )(dest, wts_t, h2, x1, mod3, wsg, wsu, wsd, gpost, ys_rows)


def _pack_weights(w_in, w_uq, w_ukv):
    d = w_in.shape[0]
    half = QK_ROPE_DIM // 2
    z = lambda n, c: jnp.zeros((n, c), F32)
    o = Q_LORA_RANK + KV_LORA_RANK
    kr = w_in[:, o:o + QK_ROPE_DIM]
    kr_grp = jnp.concatenate([z(d, QK_NOPE_DIM), kr, z(d, HEAD_PAD - QK_NOPE_DIM - QK_ROPE_DIM)], axis=1)
    kr_rot = jnp.concatenate([z(d, QK_NOPE_DIM), -kr[:, half:], kr[:, :half],
                              z(d, HEAD_PAD - QK_NOPE_DIM - QK_ROPE_DIM)], axis=1)
    win_p = jnp.concatenate([w_in[:, :o], kr_grp, kr_rot, w_in[:, o + QK_ROPE_DIM:]], axis=1)

    scale = float(QK_NOPE_DIM + QK_ROPE_DIM) ** -0.5
    r = Q_LORA_RANK
    qd = QK_NOPE_DIM + QK_ROPE_DIM
    q_grp, q_rot = [], []
    for h in range(MLA_HEADS):
        nope = w_uq[:, h * qd:h * qd + QK_NOPE_DIM]
        rope = w_uq[:, h * qd + QK_NOPE_DIM:(h + 1) * qd]
        pad = z(r, HEAD_PAD - qd)
        q_grp.append(jnp.concatenate([nope, rope, pad], axis=1))
        q_rot.append(jnp.concatenate([z(r, QK_NOPE_DIM), -rope[:, half:], rope[:, :half], pad], axis=1))
    wuq_p = jnp.concatenate(q_grp + q_rot, axis=1) * scale

    c = KV_LORA_RANK
    kd = QK_NOPE_DIM + V_HEAD_DIM
    k_grp, v_grp = [], []
    for h in range(MLA_HEADS):
        k_grp.append(jnp.concatenate([w_ukv[:, h * kd:h * kd + QK_NOPE_DIM], z(c, HEAD_PAD - QK_NOPE_DIM)], axis=1))
        v_grp.append(jnp.concatenate([w_ukv[:, h * kd + QK_NOPE_DIM:(h + 1) * kd], z(c, HEAD_PAD - V_HEAD_DIM)], axis=1))
    wukv_p = jnp.concatenate(k_grp + v_grp, axis=1)
    return win_p.astype(BF16), wuq_p.astype(BF16), wukv_p.astype(BF16)


def _rope_tables(positions):
    inv = 1.0 / (ROPE_THETA ** (jnp.arange(0, QK_ROPE_DIM, 2, dtype=F32) / QK_ROPE_DIM))
    ang = positions.astype(F32).reshape(-1)[:, None] * inv
    t = ang.shape[0]
    cos, sin = jnp.cos(ang), jnp.sin(ang)
    tail = HEAD_PAD - QK_NOPE_DIM - QK_ROPE_DIM
    cos_t = jnp.concatenate([jnp.ones((t, QK_NOPE_DIM), F32), cos, cos, jnp.ones((t, tail), F32)], axis=1)
    sin_t = jnp.concatenate([jnp.zeros((t, QK_NOPE_DIM), F32), sin, sin, jnp.zeros((t, tail), F32)], axis=1)
    return cos_t, sin_t


def _layer(x2, c, cos_t, sin_t, batch, seq, w_ada, b_ada, g_pre_mix, w_in, g_q_lat, w_uq, g_kv_lat, w_ukv,
           w_conv, g_attn_out, g_conv_out, w_out, g_post_mix, g_pre_ffn, w_router, b_router,
           w_gate, w_up, w_down, w_sh_gate, w_sh_up, w_sh_down, g_post_ffn):
    t, d = x2.shape
    r1 = lambda a: a.reshape(1, -1)

    c_pad = jnp.zeros((SUBLANES, d), F32).at[:batch].set(c)
    mod = _ada(c_pad, w_ada, r1(b_ada))[:batch]
    mod3 = mod.reshape(batch, 6, d)

    win_p, wuq_p, wukv_p = _pack_weights(w_in, w_uq, w_ukv)
    vone = jnp.zeros((1, HEAD_PAD), F32).at[0, V_HEAD_DIM].set(1.0)
    q, k, v, yc = _mix_in(x2, mod3, r1(g_pre_mix), win_p, r1(g_q_lat), wuq_p, r1(g_kv_lat), wukv_p,
                          vone, w_conv, r1(g_conv_out), cos_t, sin_t, seq)
    attn = _attention(q, k, v, batch, seq)
    x1, h2, h2p, idx, wts, rank, cnt = _mix_out(
        attn, yc, x2, mod3, r1(g_attn_out), w_out.astype(BF16), r1(g_post_mix), r1(g_pre_ffn),
        w_router.T, b_router.reshape(-1, 1), seq)

    counts = cnt[:, 0].astype(I32)
    padded = ((counts + BM_EXPERT - 1) // BM_EXPERT) * BM_EXPERT
    pad_end = jnp.cumsum(padded)
    pad_start = pad_end - padded
    m = t * TOP_K
    nb = (m + N_EXPERTS * (BM_EXPERT - 1)) // BM_EXPERT
    nused = pad_end[-1] // BM_EXPERT
    blk_exp = jnp.minimum(jnp.searchsorted(pad_end, jnp.arange(nb, dtype=I32) * BM_EXPERT, side="right"),
                          N_EXPERTS - 1).astype(I32)
    blk_exp = jnp.where(jnp.arange(nb) < nused, blk_exp, blk_exp[jnp.maximum(nused - 1, 0)])

    dest = _dest(idx, rank, pad_start.astype(F32).reshape(-1, 1))
    xs = _dispatch(dest, h2p, jnp.zeros((nb * BM_EXPERT, _ROW_SLABS, LANES), U32))
    ys = _experts(blk_exp, nused.reshape(1).astype(I32), xs, w_gate, w_up, w_down)
    return _combine(dest, wts.T, h2, x1, mod3, w_sh_gate.astype(BF16), w_sh_up.astype(BF16),
                    w_sh_down.astype(BF16), r1(g_post_ffn), ys, seq)


def kernel(x, c, positions, w_ada, b_ada, g_pre_mix, w_in, g_q_lat, w_uq, g_kv_lat, w_ukv, w_conv, g_attn_out, g_conv_out, w_out, g_post_mix, g_pre_ffn, w_router, b_router, w_gate, w_up, w_down, w_sh_gate, w_sh_up, w_sh_down, g_post_ffn):
    batch, seq, d = x.shape
    cos_t, sin_t = _rope_tables(positions)
    x2 = x.reshape(batch * seq, d)
    for l in range(w_ada.shape[0]):
        x2 = _layer(x2, c, cos_t, sin_t, batch, seq, w_ada[l], b_ada[l], g_pre_mix[l], w_in[l], g_q_lat[l],
                    w_uq[l], g_kv_lat[l], w_ukv[l], w_conv[l], g_attn_out[l], g_conv_out[l], w_out[l],
                    g_post_mix[l], g_pre_ffn[l], w_router[l], b_router[l], w_gate[l], w_up[l], w_down[l],
                    w_sh_gate[l], w_sh_up[l], w_sh_down[l], g_post_ffn[l])
    return x2.reshape(batch, seq, d)
```

```python
import functools

import jax
import jax.numpy as jnp
import numpy as np
from jax import lax
from jax.experimental import pallas as pl
from jax.experimental.pallas import tpu as pltpu

F32 = jnp.float32
BF16 = jnp.bfloat16
I32 = jnp.int32
U32 = jnp.uint32

CHUNK = 64
MLA_HEADS = 8
QK_NOPE_DIM = 64
QK_ROPE_DIM = 32
V_HEAD_DIM = 64
Q_LORA_RANK = 384
KV_LORA_RANK = 256
ROPE_THETA = 10000.0
CONV_WIDTH = 3
N_EXPERTS = 256
TOP_K = 8
N_EXPERT_GROUPS = 8
TOPK_GROUPS = 4
EXPERT_DIM = 256
ROUTED_SCALE = 2.5
EPS = 1e-6

LANES = 128
SUBLANES = 8
HEAD_PAD = LANES
VMEM_LIMIT_BYTES = 56 * 1024 * 1024

TM_IN = 256
TQ_ATTN = 512
TM_OUT = 256
TM_DEST = 512
TM_DISPATCH = 512
BM_EXPERT = 256
TM_COMBINE = 256

NEG_INF = float("-inf")


def _rms(x, g):
    return x * lax.rsqrt(jnp.mean(x * x, axis=-1, keepdims=True) + EPS) * g


_HI_MASK = np.uint32(0xFFFF0000)
_ROW_WORDS = 512
_ROW_SLABS = _ROW_WORDS // LANES


def _pack_row_words(lo, hi):
    lo_w = lax.bitcast_convert_type(lo.astype(BF16).astype(F32), U32) >> 16
    hi_w = lax.bitcast_convert_type(hi.astype(BF16).astype(F32), U32) & _HI_MASK
    return lo_w | hi_w


def _unpack_row_words(w):
    return (lax.bitcast_convert_type(w << 16, F32), lax.bitcast_convert_type(w & _HI_MASK, F32))


def _params(sem):
    return pltpu.CompilerParams(dimension_semantics=sem, vmem_limit_bytes=VMEM_LIMIT_BYTES)


def _ada_kernel(c_ref, w_ref, b_ref, o_ref):
    c = c_ref[...]
    s = c * jax.nn.sigmoid(c)
    o_ref[...] = jnp.dot(s, w_ref[...], preferred_element_type=F32,
                         precision=lax.Precision.HIGHEST) + b_ref[...]


def _ada(c_pad, w, b):
    rows, d = c_pad.shape
    n = w.shape[1]
    tn = 1536
    return pl.pallas_call(
        _ada_kernel,
        grid=(n // tn,),
        in_specs=[pl.BlockSpec((rows, d), lambda j: (0, 0)),
                  pl.BlockSpec((d, tn), lambda j: (0, j)),
                  pl.BlockSpec((1, tn), lambda j: (0, j))],
        out_specs=pl.BlockSpec((rows, tn), lambda j: (0, j)),
        out_shape=jax.ShapeDtypeStruct((rows, n), F32),
        compiler_params=_params(("arbitrary",)),
        name="ada",
    )(c_pad, w, b)


_CQ0, _CQ1 = 0, Q_LORA_RANK
_CKV0, _CKV1 = _CQ1, _CQ1 + KV_LORA_RANK
_KR0, _KR1 = _CKV1, _CKV1 + 2 * HEAD_PAD
_CONV_DIM = 512
_GB0 = _KR1
_GC0 = _GB0 + _CONV_DIM
_XV0 = _GC0 + _CONV_DIM
_WIN_COLS = _XV0 + _CONV_DIM
_QW = MLA_HEADS * HEAD_PAD


def _mix_in_kernel(tiles_per_batch, x_ref, mod_ref, gpre_ref, win_ref, gq_ref, wuq_ref, gkv_ref,
                   wukv_ref, vone_ref, wconv_ref, gconv_ref, cos_ref, sin_ref,
                   q_ref, k_ref, v_ref, yc_ref, h_scr, u_scr):
    i = pl.program_id(0)
    tm = x_ref.shape[0]
    sh1 = mod_ref[0, 0:1, :]
    sc1 = mod_ref[0, 1:2, :]
    h = _rms(x_ref[...], gpre_ref[...]) * (1.0 + sc1) + sh1
    h_scr[...] = h.astype(BF16)
    cos = cos_ref[...]
    sin = sin_ref[...]

    cq = jnp.dot(h_scr[...], win_ref[:, _CQ0:_CQ1], preferred_element_type=F32)
    cqn = _rms(cq, gq_ref[...]).astype(BF16)
    qq = jnp.dot(cqn, wuq_ref[...], preferred_element_type=F32)
    for hd in range(MLA_HEADS):
        lo = hd * HEAD_PAD
        qh = qq[:, lo:lo + HEAD_PAD] * cos + qq[:, _QW + lo:_QW + lo + HEAD_PAD] * sin
        q_ref[:, lo:lo + HEAD_PAD] = qh.astype(BF16)

    ckv = jnp.dot(h_scr[...], win_ref[:, _CKV0:_CKV1], preferred_element_type=F32)
    ckvn = _rms(ckv, gkv_ref[...]).astype(BF16)
    kv = jnp.dot(ckvn, wukv_ref[...], preferred_element_type=F32)
    krr = jnp.dot(h_scr[...], win_ref[:, _KR0:_KR1], preferred_element_type=F32)
    kr = krr[:, 0:HEAD_PAD] * cos + krr[:, HEAD_PAD:2 * HEAD_PAD] * sin
    vone = vone_ref[...]
    for hd in range(MLA_HEADS):
        lo = hd * HEAD_PAD
        k_ref[:, lo:lo + HEAD_PAD] = (kv[:, lo:lo + HEAD_PAD] + kr).astype(BF16)
        v_ref[:, lo:lo + HEAD_PAD] = (kv[:, _QW + lo:_QW + lo + HEAD_PAD] + vone).astype(BF16)

    gb = jnp.dot(h_scr[...], win_ref[:, _GB0:_GC0], preferred_element_type=F32)
    gc = jnp.dot(h_scr[...], win_ref[:, _GC0:_XV0], preferred_element_type=F32)
    xv = jnp.dot(h_scr[...], win_ref[:, _XV0:_WIN_COLS], preferred_element_type=F32)
    u = gc * xv
    prev = u_scr[tm:tm + SUBLANES, :]
    first = (i % tiles_per_batch) == 0
    u_scr[0:SUBLANES, :] = jnp.where(first, jnp.zeros_like(prev), prev)
    u_scr[SUBLANES:tm + SUBLANES, :] = u
    um1 = u_scr[SUBLANES - 1:tm + SUBLANES - 1, :]
    um2 = u_scr[SUBLANES - 2:tm + SUBLANES - 2, :]
    conv = wconv_ref[0:1, :] * um2 + wconv_ref[1:2, :] * um1 + wconv_ref[2:3, :] * u
    yc_ref[...] = _rms(gb * conv, gconv_ref[...]).astype(BF16)


def _mix_in(x2, mod3, gpre, win_p, gq, wuq_p, gkv, wukv_p, vone, wconv, gconv, cos_t, sin_t, seq):
    t, d = x2.shape
    tm = min(TM_IN, seq)
    tpb = seq // tm
    full = lambda a: pl.BlockSpec(a.shape, lambda i: (0,) * a.ndim)
    row = lambda w: pl.BlockSpec((tm, w), lambda i: (i, 0))
    return pl.pallas_call(
        functools.partial(_mix_in_kernel, tpb),
        grid=(t // tm,),
        in_specs=[row(d),
                  pl.BlockSpec((1, 6, d), lambda i: (i // tpb, 0, 0)),
                  full(gpre), full(win_p), full(gq), full(wuq_p), full(gkv), full(wukv_p),
                  full(vone), full(wconv), full(gconv), row(HEAD_PAD), row(HEAD_PAD)],
        out_specs=[row(_QW), row(_QW), row(_QW), row(_CONV_DIM)],
        out_shape=[jax.ShapeDtypeStruct((t, _QW), BF16), jax.ShapeDtypeStruct((t, _QW), BF16),
                   jax.ShapeDtypeStruct((t, _QW), BF16), jax.ShapeDtypeStruct((t, _CONV_DIM), BF16)],
        scratch_shapes=[pltpu.VMEM((tm, d), BF16), pltpu.VMEM((tm + SUBLANES, _CONV_DIM), F32)],
        compiler_params=_params(("arbitrary",)),
        name="mix_in",
    )(x2, mod3, gpre, win_p, gq, wuq_p, gkv, wukv_p, vone, wconv, gconv, cos_t, sin_t)


_HEADS_PER_STEP = 2


def _attn_kernel(q_ref, k_ref, v_ref, o_ref, s_scr, m_scr, acc_scr):
    qi = pl.program_id(2)
    tq = q_ref.shape[0]
    tk = tq
    lane_groups = tk // LANES

    def tile_max(s):
        m = s[:, 0:LANES]
        for g in range(1, lane_groups):
            m = jnp.maximum(m, s[:, g * LANES:(g + 1) * LANES])
        return m

    for hh in range(_HEADS_PER_STEP):
        lo = hh * HEAD_PAD
        q = q_ref[:, lo:lo + HEAD_PAD]

        def scores(kv):
            off = pl.multiple_of(kv * tk, tk)
            kt = k_ref[pl.ds(off, tk), lo:lo + HEAD_PAD]
            return lax.dot_general(q, kt, (((1,), (1,)), ((), ())), preferred_element_type=F32)

        m_scr[...] = jnp.full(m_scr.shape, NEG_INF, F32)

        def pass1(kv, carry):
            s = scores(kv)
            s_scr[kv] = s
            m_scr[...] = jnp.maximum(m_scr[...], tile_max(s))
            return carry

        lax.fori_loop(0, qi, pass1, 0)
        s = scores(qi)
        rc = lax.broadcasted_iota(I32, (tq, tk), 0) // CHUNK
        cc = lax.broadcasted_iota(I32, (tq, tk), 1) // CHUNK
        s = jnp.where(cc <= rc, s, NEG_INF)
        s_scr[qi] = s
        m_row = jnp.max(jnp.maximum(m_scr[...], tile_max(s)), axis=1, keepdims=True)

        acc_scr[...] = jnp.zeros(acc_scr.shape, F32)

        def pass2(kv, carry):
            off = pl.multiple_of(kv * tk, tk)
            p = jnp.exp(s_scr[kv] - m_row).astype(BF16)
            vt = v_ref[pl.ds(off, tk), lo:lo + HEAD_PAD]
            acc_scr[...] += jnp.dot(p, vt, preferred_element_type=F32)
            return carry

        lax.fori_loop(0, qi + 1, pass2, 0)
        acc = acc_scr[...]
        o = acc[:, 0:V_HEAD_DIM] / acc[:, V_HEAD_DIM:V_HEAD_DIM + 1]
        o_ref[:, hh * V_HEAD_DIM:(hh + 1) * V_HEAD_DIM] = o.astype(BF16)


def _attention(q, k, v, batch, seq):
    t = q.shape[0]
    tq = min(TQ_ATTN, seq)
    nq = seq // tq
    hw = _HEADS_PER_STEP * HEAD_PAD
    ow = _HEADS_PER_STEP * V_HEAD_DIM
    return pl.pallas_call(
        _attn_kernel,
        grid=(batch, MLA_HEADS // _HEADS_PER_STEP, nq),
        in_specs=[pl.BlockSpec((tq, hw), lambda b, j, i: (b * nq + i, j)),
                  pl.BlockSpec((seq, hw), lambda b, j, i: (b, j)),
                  pl.BlockSpec((seq, hw), lambda b, j, i: (b, j))],
        out_specs=pl.BlockSpec((tq, ow), lambda b, j, i: (b * nq + i, j)),
        out_shape=jax.ShapeDtypeStruct((t, MLA_HEADS * V_HEAD_DIM), BF16),
        scratch_shapes=[pltpu.VMEM((nq, tq, tq), F32), pltpu.VMEM((tq, LANES), F32),
                        pltpu.VMEM((tq, HEAD_PAD), F32)],
        compiler_params=_params(("arbitrary", "arbitrary", "arbitrary")),
        name="attn",
    )(q, k, v)


_GROUP_SIZE = N_EXPERTS // N_EXPERT_GROUPS
_BIG = 1.0e9


def _mix_out_kernel(attn_ref, yc_ref, x_ref, mod_ref, gattn_ref, wout_ref, gpost_ref, gpre2_ref,
                    wrt_ref, br_ref, x1_ref, h2_ref, h2p_ref, idx_ref, wts_ref, rank_ref, cnt_ref,
                    carry_scr):
    i = pl.program_id(0)
    tm = x_ref.shape[0]
    half = attn_ref.shape[1]

    @pl.when(i == 0)
    def _():
        carry_scr[...] = jnp.zeros(carry_scr.shape, F32)

    an = _rms(attn_ref[...].astype(F32), gattn_ref[...]).astype(BF16)
    mix = (jnp.dot(an, wout_ref[0:half, :], preferred_element_type=F32)
           + jnp.dot(yc_ref[...], wout_ref[half:, :], preferred_element_type=F32))
    g1 = mod_ref[0, 2:3, :]
    sh2 = mod_ref[0, 3:4, :]
    sc2 = mod_ref[0, 4:5, :]
    x1 = x_ref[...] + g1 * _rms(mix, gpost_ref[...])
    x1_ref[...] = x1
    h2 = _rms(x1, gpre2_ref[...]) * (1.0 + sc2) + sh2
    h2_ref[...] = h2.astype(BF16)
    words = _pack_row_words(h2[:, 0:_ROW_WORDS], h2[:, _ROW_WORDS:])
    for sl in range(_ROW_SLABS):
        h2p_ref[:, sl, :] = words[:, sl * LANES:(sl + 1) * LANES]

    logits = lax.dot_general(wrt_ref[...], h2, (((1,), (1,)), ((), ())),
                             preferred_element_type=F32, precision=lax.Precision.HIGHEST)
    scores = jax.nn.sigmoid(logits)
    sel = scores + br_ref[...]
    row = lax.broadcasted_iota(I32, (N_EXPERTS, tm), 0).astype(F32)

    gscore = []
    rw = lax.broadcasted_iota(I32, (_GROUP_SIZE, tm), 0).astype(F32)
    for g in range(N_EXPERT_GROUPS):
        blk = sel[g * _GROUP_SIZE:(g + 1) * _GROUP_SIZE, :]
        m1 = jnp.max(blk, axis=0, keepdims=True)
        i1 = jnp.min(jnp.where(blk == m1, rw, _BIG), axis=0, keepdims=True)
        m2 = jnp.max(jnp.where(rw == i1, NEG_INF, blk), axis=0, keepdims=True)
        gscore.append(m1 + m2)

    gkeep = [jnp.zeros((1, tm), F32) for _ in range(N_EXPERT_GROUPS)]
    for _ in range(TOPK_GROUPS):
        mg = functools.reduce(jnp.maximum, gscore)
        ig = functools.reduce(jnp.minimum, [jnp.where(gscore[g] == mg, float(g), _BIG)
                                            for g in range(N_EXPERT_GROUPS)])
        for g in range(N_EXPERT_GROUPS):
            hit = ig == float(g)
            gkeep[g] = jnp.where(hit, 1.0, gkeep[g])
            gscore[g] = jnp.where(hit, NEG_INF, gscore[g])
    cur = jnp.concatenate(
        [jnp.where(gkeep[g] > 0.0, sel[g * _GROUP_SIZE:(g + 1) * _GROUP_SIZE, :], NEG_INF)
         for g in range(N_EXPERT_GROUPS)], axis=0)

    krow = lax.broadcasted_iota(I32, (TOP_K, tm), 0)
    idx_rows = []
    idx_f = jnp.zeros((TOP_K, tm), F32)
    sc_k = jnp.zeros((TOP_K, tm), F32)
    sc_sum = jnp.zeros((1, tm), F32)
    onehot = jnp.zeros((N_EXPERTS, tm), F32)
    for k in range(TOP_K):
        m = jnp.max(cur, axis=0, keepdims=True)
        ik = jnp.min(jnp.where(cur == m, row, _BIG), axis=0, keepdims=True)
        hit = row == ik
        sk = jnp.sum(jnp.where(hit, scores, 0.0), axis=0, keepdims=True)
        cur = jnp.where(hit, NEG_INF, cur)
        onehot = jnp.where(hit, 1.0, onehot)
        idx_rows.append(ik)
        idx_f = jnp.where(krow == k, ik, idx_f)
        sc_k = jnp.where(krow == k, sk, sc_k)
        sc_sum = sc_sum + sk
    wts_ref[...] = sc_k / sc_sum * ROUTED_SCALE
    idx_ref[...] = idx_f.astype(I32)

    tri = (lax.broadcasted_iota(I32, (tm, tm), 0) < lax.broadcasted_iota(I32, (tm, tm), 1))
    excl = jnp.dot(onehot.astype(BF16), tri.astype(BF16), preferred_element_type=F32)
    rank_e = carry_scr[:, 0:1] + excl
    rank_k = jnp.zeros((TOP_K, tm), F32)
    for k in range(TOP_K):
        hit = row == idx_rows[k]
        rk = jnp.sum(jnp.where(hit, rank_e, 0.0), axis=0, keepdims=True)
        rank_k = jnp.where(krow == k, rk, rank_k)
    rank_ref[...] = rank_k.astype(I32)
    carry_scr[...] = carry_scr[...] + jnp.sum(onehot, axis=1, keepdims=True)
    cnt_ref[...] = carry_scr[...]


def _mix_out(attn, yc, x2, mod3, gattn, wout, gpost, gpre2, wrt, br, seq):
    t, d = x2.shape
    tm = min(TM_OUT, seq)
    tpb = seq // tm
    full = lambda a: pl.BlockSpec(a.shape, lambda i: (0,) * a.ndim)
    row = lambda w: pl.BlockSpec((tm, w), lambda i: (i, 0))
    col = pl.BlockSpec((TOP_K, tm), lambda i: (0, i))
    return pl.pallas_call(
        _mix_out_kernel,
        grid=(t // tm,),
        in_specs=[row(attn.shape[1]), row(yc.shape[1]), row(d),
                  pl.BlockSpec((1, 6, d), lambda i: (i // tpb, 0, 0)),
                  full(gattn), full(wout), full(gpost), full(gpre2), full(wrt), full(br)],
        out_specs=[row(d), row(d), pl.BlockSpec((tm, _ROW_SLABS, LANES), lambda i: (i, 0, 0)), col, col, col,
                   pl.BlockSpec((N_EXPERTS, LANES), lambda i: (0, 0))],
        out_shape=[jax.ShapeDtypeStruct((t, d), F32), jax.ShapeDtypeStruct((t, d), BF16),
                   jax.ShapeDtypeStruct((t, _ROW_SLABS, LANES), U32),
                   jax.ShapeDtypeStruct((TOP_K, t), I32), jax.ShapeDtypeStruct((TOP_K, t), F32),
                   jax.ShapeDtypeStruct((TOP_K, t), I32),
                   jax.ShapeDtypeStruct((N_EXPERTS, LANES), F32)],
        scratch_shapes=[pltpu.VMEM((N_EXPERTS, LANES), F32)],
        compiler_params=_params(("arbitrary",)),
        name="mix_out",
    )(attn, yc, x2, mod3, gattn, wout, gpost, gpre2, wrt, br)


def _dest_kernel(idx_ref, rank_ref, pstart_ref, dest_ref):
    tm = idx_ref.shape[1]
    row = lax.broadcasted_iota(I32, (N_EXPERTS, tm), 0)
    krow = lax.broadcasted_iota(I32, (TOP_K, tm), 0)
    pstart = pstart_ref[...]
    idx = idx_ref[...]
    out = jnp.zeros((TOP_K, tm), F32)
    for k in range(TOP_K):
        hit = row == idx[k:k + 1, :]
        base = jnp.sum(jnp.where(hit, pstart, 0.0), axis=0, keepdims=True)
        out = jnp.where(krow == k, base, out)
    dest_ref[...] = out.astype(I32) + rank_ref[...]


def _dest(idx, rank, pstart):
    t = idx.shape[1]
    tm = min(TM_DEST, t)
    col = pl.BlockSpec((TOP_K, tm), lambda i: (0, i))
    return pl.pallas_call(
        _dest_kernel,
        grid=(t // tm,),
        in_specs=[col, col, pl.BlockSpec((N_EXPERTS, 1), lambda i: (0, 0))],
        out_specs=col,
        out_shape=jax.ShapeDtypeStruct((TOP_K, t), I32),
        compiler_params=_params(("arbitrary",)),
        name="dest",
    )(idx, rank, pstart)


def _dispatch_kernel(dest_ref, h2_ref, xs_in_hbm, xs_hbm, sem):
    del xs_in_hbm
    tm = dest_ref.shape[1]

    def row_copy(t, k):
        return pltpu.make_async_copy(h2_ref.at[t], xs_hbm.at[dest_ref[k, t]], sem)

    def issue(t, carry):
        for k in range(TOP_K):
            row_copy(t, k).start()
        return carry

    lax.fori_loop(0, tm, issue, 0)

    def drain(t, carry):
        for k in range(TOP_K):
            row_copy(t, k).wait()
        return carry

    lax.fori_loop(0, tm, drain, 0)


def _dispatch(dest, h2_rows, xs_zero):
    t = h2_rows.shape[0]
    tm = min(TM_DISPATCH, t)
    return pl.pallas_call(
        _dispatch_kernel,
        grid=(t // tm,),
        in_specs=[pl.BlockSpec((TOP_K, tm), lambda i: (0, i), memory_space=pltpu.SMEM),
                  pl.BlockSpec((tm, _ROW_SLABS, LANES), lambda i: (i, 0, 0)),
                  pl.BlockSpec(memory_space=pl.ANY)],
        out_specs=pl.BlockSpec(memory_space=pl.ANY),
        out_shape=jax.ShapeDtypeStruct(xs_zero.shape, xs_zero.dtype),
        scratch_shapes=[pltpu.SemaphoreType.DMA],
        input_output_aliases={2: 0},
        compiler_params=_params(("arbitrary",)),
        name="dispatch",
    )(dest, h2_rows, xs_zero)


def _expert_kernel(blk_exp_ref, nused_ref, xs_ref, wg_ref, wu_ref, wd_ref, ys_ref, wgu_scr, wd_scr, x_scr):
    i = pl.program_id(0)
    prev = blk_exp_ref[jnp.maximum(i - 1, 0)]
    changed = jnp.logical_or(i == 0, blk_exp_ref[i] != prev)

    @pl.when(changed)
    def _():
        wgu_scr[:, 0:EXPERT_DIM] = wg_ref[0].astype(BF16)
        wgu_scr[:, EXPERT_DIM:2 * EXPERT_DIM] = wu_ref[0].astype(BF16)
        wd_scr[...] = wd_ref[0].astype(BF16)

    @pl.when(i < nused_ref[0])
    def _():
        for sl in range(_ROW_SLABS):
            lo, hi = _unpack_row_words(xs_ref[:, sl, :])
            x_scr[:, sl * LANES:(sl + 1) * LANES] = lo.astype(BF16)
            x_scr[:, _ROW_WORDS + sl * LANES:_ROW_WORDS + (sl + 1) * LANES] = hi.astype(BF16)
        gu = jnp.dot(x_scr[...], wgu_scr[...], preferred_element_type=F32)
        g = gu[:, 0:EXPERT_DIM]
        a = (g * jax.nn.sigmoid(g) * gu[:, EXPERT_DIM:2 * EXPERT_DIM]).astype(BF16)
        y = jnp.dot(a, wd_scr[...], preferred_element_type=F32)
        words = _pack_row_words(y[:, 0:_ROW_WORDS], y[:, _ROW_WORDS:])
        for sl in range(_ROW_SLABS):
            ys_ref[:, sl, :] = words[:, sl * LANES:(sl + 1) * LANES]


def _experts(blk_exp, nused, xs, w_gate, w_up, w_down):
    p = xs.shape[0]
    d = w_gate.shape[1]
    nb = p // BM_EXPERT
    blk = lambda i, be, nu: (jnp.minimum(i, nu[0] - 1), 0, 0)
    wsel = lambda i, be, nu: (be[i], 0, 0)
    return pl.pallas_call(
        _expert_kernel,
        grid_spec=pltpu.PrefetchScalarGridSpec(
            num_scalar_prefetch=2,
            grid=(nb,),
            in_specs=[pl.BlockSpec((BM_EXPERT, _ROW_SLABS, LANES), blk),
                      pl.BlockSpec((1, d, EXPERT_DIM), wsel),
                      pl.BlockSpec((1, d, EXPERT_DIM), wsel),
                      pl.BlockSpec((1, EXPERT_DIM, d), wsel)],
            out_specs=pl.BlockSpec((BM_EXPERT, _ROW_SLABS, LANES), blk),
            scratch_shapes=[pltpu.VMEM((d, 2 * EXPERT_DIM), BF16), pltpu.VMEM((EXPERT_DIM, d), BF16),
                            pltpu.VMEM((BM_EXPERT, d), BF16)]),
        out_shape=jax.ShapeDtypeStruct((p, _ROW_SLABS, LANES), U32),
        compiler_params=_params(("arbitrary",)),
        name="experts",
    )(blk_exp, nused, xs, w_gate, w_up, w_down)


def _combine_kernel(dest_ref, wts_ref, h2_ref, x1_ref, mod_ref, wsg_ref, wsu_ref, wsd_ref, gpost_ref,
                    ys_hbm, o_ref, yg_scr, sem):
    tm = x1_ref.shape[0]

    def row_copy(t, k):
        return pltpu.make_async_copy(ys_hbm.at[dest_ref[k, t]], yg_scr.at[k, t], sem)

    def issue(t, carry):
        for k in range(TOP_K):
            row_copy(t, k).start()
        return carry

    lax.fori_loop(0, tm, issue, 0)

    h2 = h2_ref[...]
    g = jnp.dot(h2, wsg_ref[...], preferred_element_type=F32)
    u = jnp.dot(h2, wsu_ref[...], preferred_element_type=F32)
    f = jnp.dot((g * jax.nn.sigmoid(g) * u).astype(BF16), wsd_ref[...], preferred_element_type=F32)

    def drain(t, carry):
        for k in range(TOP_K):
            row_copy(t, k).wait()
        return carry

    lax.fori_loop(0, tm, drain, 0)

    wts = wts_ref[...]
    los = [f[:, sl * LANES:(sl + 1) * LANES] for sl in range(_ROW_SLABS)]
    his = [f[:, _ROW_WORDS + sl * LANES:_ROW_WORDS + (sl + 1) * LANES] for sl in range(_ROW_SLABS)]
    for k in range(TOP_K):
        wk = wts[:, k:k + 1]
        for sl in range(_ROW_SLABS):
            lo, hi = _unpack_row_words(yg_scr[k, :, sl, :])
            los[sl] = los[sl] + wk * lo
            his[sl] = his[sl] + wk * hi
    f = jnp.concatenate(los + his, axis=1)
    g2 = mod_ref[0, 5:6, :]
    o_ref[...] = x1_ref[...] + g2 * _rms(f, gpost_ref[...])


def _combine(dest, wts_t, h2, x1, mod3, wsg, wsu, wsd, gpost, ys_rows, seq):
    t, d = x1.shape
    tm = min(TM_COMBINE, seq)
    tpb = seq // tm
    full = lambda a: pl.BlockSpec(a.shape, lambda i: (0,) * a.ndim)
    row = lambda w: pl.BlockSpec((tm, w), lambda i: (i, 0))
    return pl.pallas_call(
        _combine_kernel,
        grid=(t // tm,),
        in_specs=[pl.BlockSpec((TOP_K, tm), lambda i: (0, i), memory_space=pltpu.SMEM),
                  row(TOP_K), row(d), row(d),
                  pl.BlockSpec((1, 6, d), lambda i: (i // tpb, 0, 0)),
                  full(wsg), full(wsu), full(wsd), full(gpost),
                  pl.BlockSpec(memory_space=pl.ANY)],
        out_specs=row(d),
        out_shape=jax.ShapeDtypeStruct((t, d), F32),
        scratch_shapes=[pltpu.VMEM((TOP_K, tm, _ROW_SLABS, LANES), U32), pltpu.SemaphoreType.DMA],
        compiler_params=_params(("arbitrary",)),
        name="combine",
    )(dest, wts_t, h2, x1, mod3, wsg, wsu, wsd, gpost, ys_rows)


def _pack_weights(w_in, w_uq, w_ukv):
    d = w_in.shape[0]
    half = QK_ROPE_DIM // 2
    z = lambda n, c: jnp.zeros((n, c), F32)
    o = Q_LORA_RANK + KV_LORA_RANK
    kr = w_in[:, o:o + QK_ROPE_DIM]
    kr_grp = jnp.concatenate([z(d, QK_NOPE_DIM), kr, z(d, HEAD_PAD - QK_NOPE_DIM - QK_ROPE_DIM)], axis=1)
    kr_rot = jnp.concatenate([z(d, QK_NOPE_DIM), -kr[:, half:], kr[:, :half],
                              z(d, HEAD_PAD - QK_NOPE_DIM - QK_ROPE_DIM)], axis=1)
    win_p = jnp.concatenate([w_in[:, :o], kr_grp, kr_rot, w_in[:, o + QK_ROPE_DIM:]], axis=1)

    scale = float(QK_NOPE_DIM + QK_ROPE_DIM) ** -0.5
    r = Q_LORA_RANK
    qd = QK_NOPE_DIM + QK_ROPE_DIM
    q_grp, q_rot = [], []
    for h in range(MLA_HEADS):
        nope = w_uq[:, h * qd:h * qd + QK_NOPE_DIM]
        rope = w_uq[:, h * qd + QK_NOPE_DIM:(h + 1) * qd]
        pad = z(r, HEAD_PAD - qd)
        q_grp.append(jnp.concatenate([nope, rope, pad], axis=1))
        q_rot.append(jnp.concatenate([z(r, QK_NOPE_DIM), -rope[:, half:], rope[:, :half], pad], axis=1))
    wuq_p = jnp.concatenate(q_grp + q_rot, axis=1) * scale

    c = KV_LORA_RANK
    kd = QK_NOPE_DIM + V_HEAD_DIM
    k_grp, v_grp = [], []
    for h in range(MLA_HEADS):
        k_grp.append(jnp.concatenate([w_ukv[:, h * kd:h * kd + QK_NOPE_DIM], z(c, HEAD_PAD - QK_NOPE_DIM)], axis=1))
        v_grp.append(jnp.concatenate([w_ukv[:, h * kd + QK_NOPE_DIM:(h + 1) * kd], z(c, HEAD_PAD - V_HEAD_DIM)], axis=1))
    wukv_p = jnp.concatenate(k_grp + v_grp, axis=1)
    return win_p.astype(BF16), wuq_p.astype(BF16), wukv_p.astype(BF16)


def _rope_tables(positions):
    inv = 1.0 / (ROPE_THETA ** (jnp.arange(0, QK_ROPE_DIM, 2, dtype=F32) / QK_ROPE_DIM))
    ang = positions.astype(F32).reshape(-1)[:, None] * inv
    t = ang.shape[0]
    cos, sin = jnp.cos(ang), jnp.sin(ang)
    tail = HEAD_PAD - QK_NOPE_DIM - QK_ROPE_DIM
    cos_t = jnp.concatenate([jnp.ones((t, QK_NOPE_DIM), F32), cos, cos, jnp.ones((t, tail), F32)], axis=1)
    sin_t = jnp.concatenate([jnp.zeros((t, QK_NOPE_DIM), F32), sin, sin, jnp.zeros((t, tail), F32)], axis=1)
    return cos_t, sin_t


def _layer(x2, c, cos_t, sin_t, batch, seq, w_ada, b_ada, g_pre_mix, w_in, g_q_lat, w_uq, g_kv_lat, w_ukv,
           w_conv, g_attn_out, g_conv_out, w_out, g_post_mix, g_pre_ffn, w_router, b_router,
           w_gate, w_up, w_down, w_sh_gate, w_sh_up, w_sh_down, g_post_ffn):
    t, d = x2.shape
    r1 = lambda a: a.reshape(1, -1)

    c_pad = jnp.zeros((SUBLANES, d), F32).at[:batch].set(c)
    mod = _ada(c_pad, w_ada, r1(b_ada))[:batch]
    mod3 = mod.reshape(batch, 6, d)

    win_p, wuq_p, wukv_p = _pack_weights(w_in, w_uq, w_ukv)
    vone = jnp.zeros((1, HEAD_PAD), F32).at[0, V_HEAD_DIM].set(1.0)
    q, k, v, yc = _mix_in(x2, mod3, r1(g_pre_mix), win_p, r1(g_q_lat), wuq_p, r1(g_kv_lat), wukv_p,
                          vone, w_conv, r1(g_conv_out), cos_t, sin_t, seq)
    attn = _attention(q, k, v, batch, seq)
    x1, h2, h2p, idx, wts, rank, cnt = _mix_out(
        attn, yc, x2, mod3, r1(g_attn_out), w_out.astype(BF16), r1(g_post_mix), r1(g_pre_ffn),
        w_router.T, b_router.reshape(-1, 1), seq)

    counts = cnt[:, 0].astype(I32)
    padded = ((counts + BM_EXPERT - 1) // BM_EXPERT) * BM_EXPERT
    pad_end = jnp.cumsum(padded)
    pad_start = pad_end - padded
    m = t * TOP_K
    nb = (m + N_EXPERTS * (BM_EXPERT - 1)) // BM_EXPERT
    nused = pad_end[-1] // BM_EXPERT
    starts = jnp.minimum(jnp.arange(nb, dtype=I32), nused - 1) * BM_EXPERT
    blk_exp = jnp.minimum(jnp.sum((pad_end[None, :] <= starts[:, None]).astype(I32), axis=1), N_EXPERTS - 1)

    dest = _dest(idx, rank, pad_start.astype(F32).reshape(-1, 1))
    xs = _dispatch(dest, h2p, jnp.zeros((nb * BM_EXPERT, _ROW_SLABS, LANES), U32))
    ys = _experts(blk_exp, nused.reshape(1).astype(I32), xs, w_gate, w_up, w_down)
    return _combine(dest, wts.T, h2, x1, mod3, w_sh_gate.astype(BF16), w_sh_up.astype(BF16),
                    w_sh_down.astype(BF16), r1(g_post_ffn), ys, seq)


def kernel(x, c, positions, w_ada, b_ada, g_pre_mix, w_in, g_q_lat, w_uq, g_kv_lat, w_ukv, w_conv, g_attn_out, g_conv_out, w_out, g_post_mix, g_pre_ffn, w_router, b_router, w_gate, w_up, w_down, w_sh_gate, w_sh_up, w_sh_down, g_post_ffn):
    batch, seq, d = x.shape
    cos_t, sin_t = _rope_tables(positions)
    x2 = x.reshape(batch * seq, d)
    for l in range(w_ada.shape[0]):
        x2 = _layer(x2, c, cos_t, sin_t, batch, seq, w_ada[l], b_ada[l], g_pre_mix[l], w_in[l], g_q_lat[l],
                    w_uq[l], g_kv_lat[l], w_ukv[l], w_conv[l], g_attn_out[l], g_conv_out[l], w_out[l],
                    g_post_mix[l], g_pre_ffn[l], w_router[l], b_router[l], w_gate[l], w_up[l], w_down[l],
                    w_sh_gate[l], w_sh_up[l], w_sh_down[l], g_post_ffn[l])
    return x2.reshape(batch, seq, d)
```

```python
import functools

import jax
import jax.numpy as jnp
import numpy as np
from jax import lax
from jax.experimental import pallas as pl
from jax.experimental.pallas import tpu as pltpu

F32 = jnp.float32
BF16 = jnp.bfloat16
I32 = jnp.int32
U32 = jnp.uint32

CHUNK = 64
MLA_HEADS = 8
QK_NOPE_DIM = 64
QK_ROPE_DIM = 32
V_HEAD_DIM = 64
Q_LORA_RANK = 384
KV_LORA_RANK = 256
ROPE_THETA = 10000.0
CONV_WIDTH = 3
N_EXPERTS = 256
TOP_K = 8
N_EXPERT_GROUPS = 8
TOPK_GROUPS = 4
EXPERT_DIM = 256
ROUTED_SCALE = 2.5
EPS = 1e-6

LANES = 128
SUBLANES = 8
HEAD_PAD = LANES
VMEM_LIMIT_BYTES = 56 * 1024 * 1024

TM_IN = 256
TQ_ATTN = 512
TM_OUT = 256
TM_DEST = 512
TM_DISPATCH = 512
BM_EXPERT = 256
TM_COMBINE = 256

NEG_INF = float("-inf")


def _rms(x, g):
    return x * lax.rsqrt(jnp.mean(x * x, axis=-1, keepdims=True) + EPS) * g


_HI_MASK = np.uint32(0xFFFF0000)
_ROW_WORDS = 512
_ROW_SLABS = _ROW_WORDS // LANES


def _pack_row_words(lo, hi):
    lo_w = lax.bitcast_convert_type(lo.astype(BF16).astype(F32), U32) >> 16
    hi_w = lax.bitcast_convert_type(hi.astype(BF16).astype(F32), U32) & _HI_MASK
    return lo_w | hi_w


def _unpack_row_words(w):
    return (lax.bitcast_convert_type(w << 16, F32), lax.bitcast_convert_type(w & _HI_MASK, F32))


def _params(sem):
    return pltpu.CompilerParams(dimension_semantics=sem, vmem_limit_bytes=VMEM_LIMIT_BYTES)


def _ada_kernel(c_ref, w_ref, b_ref, o_ref):
    c = c_ref[...]
    s = c * jax.nn.sigmoid(c)
    o_ref[...] = jnp.dot(s, w_ref[...], preferred_element_type=F32,
                         precision=lax.Precision.HIGHEST) + b_ref[...]


def _ada(c_pad, w, b):
    rows, d = c_pad.shape
    n = w.shape[1]
    tn = 1536
    return pl.pallas_call(
        _ada_kernel,
        grid=(n // tn,),
        in_specs=[pl.BlockSpec((rows, d), lambda j: (0, 0)),
                  pl.BlockSpec((d, tn), lambda j: (0, j)),
                  pl.BlockSpec((1, tn), lambda j: (0, j))],
        out_specs=pl.BlockSpec((rows, tn), lambda j: (0, j)),
        out_shape=jax.ShapeDtypeStruct((rows, n), F32),
        compiler_params=_params(("arbitrary",)),
        name="ada",
    )(c_pad, w, b)


_CQ0, _CQ1 = 0, Q_LORA_RANK
_CKV0, _CKV1 = _CQ1, _CQ1 + KV_LORA_RANK
_KR0, _KR1 = _CKV1, _CKV1 + 2 * HEAD_PAD
_CONV_DIM = 512
_GB0 = _KR1
_GC0 = _GB0 + _CONV_DIM
_XV0 = _GC0 + _CONV_DIM
_WIN_COLS = _XV0 + _CONV_DIM
_QW = MLA_HEADS * HEAD_PAD


def _mix_in_kernel(tiles_per_batch, x_ref, mod_ref, gpre_ref, win_ref, gq_ref, wuq_ref, gkv_ref,
                   wukv_ref, vone_ref, wconv_ref, gconv_ref, cos_ref, sin_ref,
                   q_ref, k_ref, v_ref, yc_ref, h_scr, u_scr):
    i = pl.program_id(0)
    tm = x_ref.shape[0]
    sh1 = mod_ref[0, 0:1, :]
    sc1 = mod_ref[0, 1:2, :]
    h = _rms(x_ref[...], gpre_ref[...]) * (1.0 + sc1) + sh1
    h_scr[...] = h.astype(BF16)
    cos = cos_ref[...]
    sin = sin_ref[...]

    cq = jnp.dot(h_scr[...], win_ref[:, _CQ0:_CQ1], preferred_element_type=F32)
    cqn = _rms(cq, gq_ref[...]).astype(BF16)
    qq = jnp.dot(cqn, wuq_ref[...], preferred_element_type=F32)
    for hd in range(MLA_HEADS):
        lo = hd * HEAD_PAD
        qh = qq[:, lo:lo + HEAD_PAD] * cos + qq[:, _QW + lo:_QW + lo + HEAD_PAD] * sin
        q_ref[:, lo:lo + HEAD_PAD] = qh.astype(BF16)

    ckv = jnp.dot(h_scr[...], win_ref[:, _CKV0:_CKV1], preferred_element_type=F32)
    ckvn = _rms(ckv, gkv_ref[...]).astype(BF16)
    kv = jnp.dot(ckvn, wukv_ref[...], preferred_element_type=F32)
    krr = jnp.dot(h_scr[...], win_ref[:, _KR0:_KR1], preferred_element_type=F32)
    kr = krr[:, 0:HEAD_PAD] * cos + krr[:, HEAD_PAD:2 * HEAD_PAD] * sin
    vone = vone_ref[...]
    for hd in range(MLA_HEADS):
        lo = hd * HEAD_PAD
        k_ref[:, lo:lo + HEAD_PAD] = (kv[:, lo:lo + HEAD_PAD] + kr).astype(BF16)
        v_ref[:, lo:lo + HEAD_PAD] = (kv[:, _QW + lo:_QW + lo + HEAD_PAD] + vone).astype(BF16)

    gb = jnp.dot(h_scr[...], win_ref[:, _GB0:_GC0], preferred_element_type=F32)
    gc = jnp.dot(h_scr[...], win_ref[:, _GC0:_XV0], preferred_element_type=F32)
    xv = jnp.dot(h_scr[...], win_ref[:, _XV0:_WIN_COLS], preferred_element_type=F32)
    u = gc * xv
    prev = u_scr[tm:tm + SUBLANES, :]
    first = (i % tiles_per_batch) == 0
    u_scr[0:SUBLANES, :] = jnp.where(first, jnp.zeros_like(prev), prev)
    u_scr[SUBLANES:tm + SUBLANES, :] = u
    um1 = u_scr[SUBLANES - 1:tm + SUBLANES - 1, :]
    um2 = u_scr[SUBLANES - 2:tm + SUBLANES - 2, :]
    conv = wconv_ref[0:1, :] * um2 + wconv_ref[1:2, :] * um1 + wconv_ref[2:3, :] * u
    yc_ref[...] = _rms(gb * conv, gconv_ref[...]).astype(BF16)


def _mix_in(x2, mod3, gpre, win_p, gq, wuq_p, gkv, wukv_p, vone, wconv, gconv, cos_t, sin_t, seq):
    t, d = x2.shape
    tm = min(TM_IN, seq)
    tpb = seq // tm
    full = lambda a: pl.BlockSpec(a.shape, lambda i: (0,) * a.ndim)
    row = lambda w: pl.BlockSpec((tm, w), lambda i: (i, 0))
    return pl.pallas_call(
        functools.partial(_mix_in_kernel, tpb),
        grid=(t // tm,),
        in_specs=[row(d),
                  pl.BlockSpec((1, 6, d), lambda i: (i // tpb, 0, 0)),
                  full(gpre), full(win_p), full(gq), full(wuq_p), full(gkv), full(wukv_p),
                  full(vone), full(wconv), full(gconv), row(HEAD_PAD), row(HEAD_PAD)],
        out_specs=[row(_QW), row(_QW), row(_QW), row(_CONV_DIM)],
        out_shape=[jax.ShapeDtypeStruct((t, _QW), BF16), jax.ShapeDtypeStruct((t, _QW), BF16),
                   jax.ShapeDtypeStruct((t, _QW), BF16), jax.ShapeDtypeStruct((t, _CONV_DIM), BF16)],
        scratch_shapes=[pltpu.VMEM((tm, d), BF16), pltpu.VMEM((tm + SUBLANES, _CONV_DIM), F32)],
        compiler_params=_params(("arbitrary",)),
        name="mix_in",
    )(x2, mod3, gpre, win_p, gq, wuq_p, gkv, wukv_p, vone, wconv, gconv, cos_t, sin_t)


_HEADS_PER_STEP = 2


def _attn_kernel(q_ref, k_ref, v_ref, o_ref, s_scr, m_scr, acc_scr):
    qi = pl.program_id(2)
    tq = q_ref.shape[0]
    tk = tq
    lane_groups = tk // LANES

    def tile_max(s):
        m = s[:, 0:LANES]
        for g in range(1, lane_groups):
            m = jnp.maximum(m, s[:, g * LANES:(g + 1) * LANES])
        return m

    for hh in range(_HEADS_PER_STEP):
        lo = hh * HEAD_PAD
        q = q_ref[:, lo:lo + HEAD_PAD]

        def scores(kv):
            off = pl.multiple_of(kv * tk, tk)
            kt = k_ref[pl.ds(off, tk), lo:lo + HEAD_PAD]
            return lax.dot_general(q, kt, (((1,), (1,)), ((), ())), preferred_element_type=F32)

        m_scr[...] = jnp.full(m_scr.shape, NEG_INF, F32)

        def pass1(kv, carry):
            s = scores(kv)
            s_scr[kv] = s
            m_scr[...] = jnp.maximum(m_scr[...], tile_max(s))
            return carry

        lax.fori_loop(0, qi, pass1, 0)
        s = scores(qi)
        rc = lax.broadcasted_iota(I32, (tq, tk), 0) // CHUNK
        cc = lax.broadcasted_iota(I32, (tq, tk), 1) // CHUNK
        s = jnp.where(cc <= rc, s, NEG_INF)
        s_scr[qi] = s
        m_row = jnp.max(jnp.maximum(m_scr[...], tile_max(s)), axis=1, keepdims=True)

        acc_scr[...] = jnp.zeros(acc_scr.shape, F32)

        def pass2(kv, carry):
            off = pl.multiple_of(kv * tk, tk)
            p = jnp.exp(s_scr[kv] - m_row).astype(BF16)
            vt = v_ref[pl.ds(off, tk), lo:lo + HEAD_PAD]
            acc_scr[...] += jnp.dot(p, vt, preferred_element_type=F32)
            return carry

        lax.fori_loop(0, qi + 1, pass2, 0)
        acc = acc_scr[...]
        o = acc[:, 0:V_HEAD_DIM] / acc[:, V_HEAD_DIM:V_HEAD_DIM + 1]
        o_ref[:, hh * V_HEAD_DIM:(hh + 1) * V_HEAD_DIM] = o.astype(BF16)


def _attention(q, k, v, batch, seq):
    t = q.shape[0]
    tq = min(TQ_ATTN, seq)
    nq = seq // tq
    hw = _HEADS_PER_STEP * HEAD_PAD
    ow = _HEADS_PER_STEP * V_HEAD_DIM
    return pl.pallas_call(
        _attn_kernel,
        grid=(batch, MLA_HEADS // _HEADS_PER_STEP, nq),
        in_specs=[pl.BlockSpec((tq, hw), lambda b, j, i: (b * nq + i, j)),
                  pl.BlockSpec((seq, hw), lambda b, j, i: (b, j)),
                  pl.BlockSpec((seq, hw), lambda b, j, i: (b, j))],
        out_specs=pl.BlockSpec((tq, ow), lambda b, j, i: (b * nq + i, j)),
        out_shape=jax.ShapeDtypeStruct((t, MLA_HEADS * V_HEAD_DIM), BF16),
        scratch_shapes=[pltpu.VMEM((nq, tq, tq), F32), pltpu.VMEM((tq, LANES), F32),
                        pltpu.VMEM((tq, HEAD_PAD), F32)],
        compiler_params=_params(("arbitrary", "arbitrary", "arbitrary")),
        name="attn",
    )(q, k, v)


_GROUP_SIZE = N_EXPERTS // N_EXPERT_GROUPS
_BIG = 1.0e9


def _mix_out_kernel(attn_ref, yc_ref, x_ref, mod_ref, gattn_ref, wout_ref, gpost_ref, gpre2_ref,
                    wrt_ref, br_ref, x1_ref, h2_ref, h2p_ref, idx_ref, wts_ref, rank_ref, cnt_ref,
                    carry_scr):
    i = pl.program_id(0)
    tm = x_ref.shape[0]
    half = attn_ref.shape[1]

    @pl.when(i == 0)
    def _():
        carry_scr[...] = jnp.zeros(carry_scr.shape, F32)

    an = _rms(attn_ref[...].astype(F32), gattn_ref[...]).astype(BF16)
    mix = (jnp.dot(an, wout_ref[0:half, :], preferred_element_type=F32)
           + jnp.dot(yc_ref[...], wout_ref[half:, :], preferred_element_type=F32))
    g1 = mod_ref[0, 2:3, :]
    sh2 = mod_ref[0, 3:4, :]
    sc2 = mod_ref[0, 4:5, :]
    x1 = x_ref[...] + g1 * _rms(mix, gpost_ref[...])
    x1_ref[...] = x1
    h2 = _rms(x1, gpre2_ref[...]) * (1.0 + sc2) + sh2
    h2_ref[...] = h2.astype(BF16)
    words = _pack_row_words(h2[:, 0:_ROW_WORDS], h2[:, _ROW_WORDS:])
    h2p_ref[...] = words

    logits = lax.dot_general(wrt_ref[...], h2, (((1,), (1,)), ((), ())),
                             preferred_element_type=F32, precision=lax.Precision.HIGHEST)
    scores = jax.nn.sigmoid(logits)
    sel = scores + br_ref[...]
    row = lax.broadcasted_iota(I32, (N_EXPERTS, tm), 0).astype(F32)

    gscore = []
    rw = lax.broadcasted_iota(I32, (_GROUP_SIZE, tm), 0).astype(F32)
    for g in range(N_EXPERT_GROUPS):
        blk = sel[g * _GROUP_SIZE:(g + 1) * _GROUP_SIZE, :]
        m1 = jnp.max(blk, axis=0, keepdims=True)
        i1 = jnp.min(jnp.where(blk == m1, rw, _BIG), axis=0, keepdims=True)
        m2 = jnp.max(jnp.where(rw == i1, NEG_INF, blk), axis=0, keepdims=True)
        gscore.append(m1 + m2)

    gkeep = [jnp.zeros((1, tm), F32) for _ in range(N_EXPERT_GROUPS)]
    for _ in range(TOPK_GROUPS):
        mg = functools.reduce(jnp.maximum, gscore)
        ig = functools.reduce(jnp.minimum, [jnp.where(gscore[g] == mg, float(g), _BIG)
                                            for g in range(N_EXPERT_GROUPS)])
        for g in range(N_EXPERT_GROUPS):
            hit = ig == float(g)
            gkeep[g] = jnp.where(hit, 1.0, gkeep[g])
            gscore[g] = jnp.where(hit, NEG_INF, gscore[g])
    cur = jnp.concatenate(
        [jnp.where(gkeep[g] > 0.0, sel[g * _GROUP_SIZE:(g + 1) * _GROUP_SIZE, :], NEG_INF)
         for g in range(N_EXPERT_GROUPS)], axis=0)

    krow = lax.broadcasted_iota(I32, (TOP_K, tm), 0)
    idx_rows = []
    idx_f = jnp.zeros((TOP_K, tm), F32)
    sc_k = jnp.zeros((TOP_K, tm), F32)
    sc_sum = jnp.zeros((1, tm), F32)
    onehot = jnp.zeros((N_EXPERTS, tm), F32)
    for k in range(TOP_K):
        m = jnp.max(cur, axis=0, keepdims=True)
        ik = jnp.min(jnp.where(cur == m, row, _BIG), axis=0, keepdims=True)
        hit = row == ik
        sk = jnp.sum(jnp.where(hit, scores, 0.0), axis=0, keepdims=True)
        cur = jnp.where(hit, NEG_INF, cur)
        onehot = jnp.where(hit, 1.0, onehot)
        idx_rows.append(ik)
        idx_f = jnp.where(krow == k, ik, idx_f)
        sc_k = jnp.where(krow == k, sk, sc_k)
        sc_sum = sc_sum + sk
    wts_ref[...] = sc_k / sc_sum * ROUTED_SCALE
    idx_ref[...] = idx_f.astype(I32)

    tri = (lax.broadcasted_iota(I32, (tm, tm), 0) < lax.broadcasted_iota(I32, (tm, tm), 1))
    excl = jnp.dot(onehot.astype(BF16), tri.astype(BF16), preferred_element_type=F32)
    rank_e = carry_scr[:, 0:1] + excl
    rank_k = jnp.zeros((TOP_K, tm), F32)
    for k in range(TOP_K):
        hit = row == idx_rows[k]
        rk = jnp.sum(jnp.where(hit, rank_e, 0.0), axis=0, keepdims=True)
        rank_k = jnp.where(krow == k, rk, rank_k)
    rank_ref[...] = rank_k.astype(I32)
    carry_scr[...] = carry_scr[...] + jnp.sum(onehot, axis=1, keepdims=True)
    cnt_ref[...] = carry_scr[...]


def _mix_out(attn, yc, x2, mod3, gattn, wout, gpost, gpre2, wrt, br, seq):
    t, d = x2.shape
    tm = min(TM_OUT, seq)
    tpb = seq // tm
    full = lambda a: pl.BlockSpec(a.shape, lambda i: (0,) * a.ndim)
    row = lambda w: pl.BlockSpec((tm, w), lambda i: (i, 0))
    col = pl.BlockSpec((TOP_K, tm), lambda i: (0, i))
    return pl.pallas_call(
        _mix_out_kernel,
        grid=(t // tm,),
        in_specs=[row(attn.shape[1]), row(yc.shape[1]), row(d),
                  pl.BlockSpec((1, 6, d), lambda i: (i // tpb, 0, 0)),
                  full(gattn), full(wout), full(gpost), full(gpre2), full(wrt), full(br)],
        out_specs=[row(d), row(d), row(_ROW_WORDS), col, col, col,
                   pl.BlockSpec((N_EXPERTS, LANES), lambda i: (0, 0))],
        out_shape=[jax.ShapeDtypeStruct((t, d), F32), jax.ShapeDtypeStruct((t, d), BF16),
                   jax.ShapeDtypeStruct((t, _ROW_WORDS), U32),
                   jax.ShapeDtypeStruct((TOP_K, t), I32), jax.ShapeDtypeStruct((TOP_K, t), F32),
                   jax.ShapeDtypeStruct((TOP_K, t), I32),
                   jax.ShapeDtypeStruct((N_EXPERTS, LANES), F32)],
        scratch_shapes=[pltpu.VMEM((N_EXPERTS, LANES), F32)],
        compiler_params=_params(("arbitrary",)),
        name="mix_out",
    )(attn, yc, x2, mod3, gattn, wout, gpost, gpre2, wrt, br)


def _dest_kernel(idx_ref, rank_ref, pstart_ref, dest_ref):
    tm = idx_ref.shape[1]
    row = lax.broadcasted_iota(I32, (N_EXPERTS, tm), 0)
    krow = lax.broadcasted_iota(I32, (TOP_K, tm), 0)
    pstart = pstart_ref[...]
    idx = idx_ref[...]
    out = jnp.zeros((TOP_K, tm), F32)
    for k in range(TOP_K):
        hit = row == idx[k:k + 1, :]
        base = jnp.sum(jnp.where(hit, pstart, 0.0), axis=0, keepdims=True)
        out = jnp.where(krow == k, base, out)
    dest_ref[...] = out.astype(I32) + rank_ref[...]


def _dest(idx, rank, pstart):
    t = idx.shape[1]
    tm = min(TM_DEST, t)
    col = pl.BlockSpec((TOP_K, tm), lambda i: (0, i))
    return pl.pallas_call(
        _dest_kernel,
        grid=(t // tm,),
        in_specs=[col, col, pl.BlockSpec((N_EXPERTS, 1), lambda i: (0, 0))],
        out_specs=col,
        out_shape=jax.ShapeDtypeStruct((TOP_K, t), I32),
        compiler_params=_params(("arbitrary",)),
        name="dest",
    )(idx, rank, pstart)


def _dispatch_kernel(dest_ref, h2_ref, xs_in_hbm, xs_hbm, sem):
    del xs_in_hbm
    tm = dest_ref.shape[1]

    def row_copy(t, k):
        return pltpu.make_async_copy(h2_ref.at[pl.ds(t, 1), :], xs_hbm.at[pl.ds(dest_ref[k, t], 1), :], sem)

    def issue(t, carry):
        for k in range(TOP_K):
            row_copy(t, k).start()
        return carry

    lax.fori_loop(0, tm, issue, 0)

    def drain(t, carry):
        for k in range(TOP_K):
            row_copy(t, k).wait()
        return carry

    lax.fori_loop(0, tm, drain, 0)


def _dispatch(dest, h2_rows, xs_zero):
    t = h2_rows.shape[0]
    tm = min(TM_DISPATCH, t)
    return pl.pallas_call(
        _dispatch_kernel,
        grid=(t // tm,),
        in_specs=[pl.BlockSpec((TOP_K, tm), lambda i: (0, i), memory_space=pltpu.SMEM),
                  pl.BlockSpec((tm, _ROW_WORDS), lambda i: (i, 0)),
                  pl.BlockSpec(memory_space=pl.ANY)],
        out_specs=pl.BlockSpec(memory_space=pl.ANY),
        out_shape=jax.ShapeDtypeStruct(xs_zero.shape, xs_zero.dtype),
        scratch_shapes=[pltpu.SemaphoreType.DMA],
        input_output_aliases={2: 0},
        compiler_params=_params(("arbitrary",)),
        name="dispatch",
    )(dest, h2_rows, xs_zero)


def _expert_kernel(blk_exp_ref, nused_ref, xs_ref, wg_ref, wu_ref, wd_ref, ys_ref, wgu_scr, wd_scr, x_scr):
    i = pl.program_id(0)
    prev = blk_exp_ref[jnp.maximum(i - 1, 0)]
    changed = jnp.logical_or(i == 0, blk_exp_ref[i] != prev)

    @pl.when(changed)
    def _():
        wgu_scr[:, 0:EXPERT_DIM] = wg_ref[0].astype(BF16)
        wgu_scr[:, EXPERT_DIM:2 * EXPERT_DIM] = wu_ref[0].astype(BF16)
        wd_scr[...] = wd_ref[0].astype(BF16)

    @pl.when(i < nused_ref[0])
    def _():
        lo, hi = _unpack_row_words(xs_ref[...])
        x_scr[:, 0:_ROW_WORDS] = lo.astype(BF16)
        x_scr[:, _ROW_WORDS:] = hi.astype(BF16)
        gu = jnp.dot(x_scr[...], wgu_scr[...], preferred_element_type=F32)
        g = gu[:, 0:EXPERT_DIM]
        a = (g * jax.nn.sigmoid(g) * gu[:, EXPERT_DIM:2 * EXPERT_DIM]).astype(BF16)
        y = jnp.dot(a, wd_scr[...], preferred_element_type=F32)
        ys_ref[...] = _pack_row_words(y[:, 0:_ROW_WORDS], y[:, _ROW_WORDS:])


def _experts(blk_exp, nused, xs, w_gate, w_up, w_down):
    p = xs.shape[0]
    d = w_gate.shape[1]
    nb = p // BM_EXPERT
    blk = lambda i, be, nu: (jnp.minimum(i, nu[0] - 1), 0)
    wsel = lambda i, be, nu: (be[i], 0, 0)
    return pl.pallas_call(
        _expert_kernel,
        grid_spec=pltpu.PrefetchScalarGridSpec(
            num_scalar_prefetch=2,
            grid=(nb,),
            in_specs=[pl.BlockSpec((BM_EXPERT, _ROW_WORDS), blk),
                      pl.BlockSpec((1, d, EXPERT_DIM), wsel),
                      pl.BlockSpec((1, d, EXPERT_DIM), wsel),
                      pl.BlockSpec((1, EXPERT_DIM, d), wsel)],
            out_specs=pl.BlockSpec((BM_EXPERT, _ROW_WORDS), blk),
            scratch_shapes=[pltpu.VMEM((d, 2 * EXPERT_DIM), BF16), pltpu.VMEM((EXPERT_DIM, d), BF16),
                            pltpu.VMEM((BM_EXPERT, d), BF16)]),
        out_shape=jax.ShapeDtypeStruct((p, _ROW_WORDS), U32),
        compiler_params=_params(("arbitrary",)),
        name="experts",
    )(blk_exp, nused, xs, w_gate, w_up, w_down)


def _combine_kernel(dest_ref, wts_ref, h2_ref, x1_ref, mod_ref, wsg_ref, wsu_ref, wsd_ref, gpost_ref,
                    ys_hbm, o_ref, yg_scr, sem):
    tm = x1_ref.shape[0]

    def row_copy(t, k):
        return pltpu.make_async_copy(ys_hbm.at[pl.ds(dest_ref[k, t], 1), :], yg_scr.at[k, pl.ds(t, 1), :], sem)

    def issue(t, carry):
        for k in range(TOP_K):
            row_copy(t, k).start()
        return carry

    lax.fori_loop(0, tm, issue, 0)

    h2 = h2_ref[...]
    g = jnp.dot(h2, wsg_ref[...], preferred_element_type=F32)
    u = jnp.dot(h2, wsu_ref[...], preferred_element_type=F32)
    f = jnp.dot((g * jax.nn.sigmoid(g) * u).astype(BF16), wsd_ref[...], preferred_element_type=F32)

    def drain(t, carry):
        for k in range(TOP_K):
            row_copy(t, k).wait()
        return carry

    lax.fori_loop(0, tm, drain, 0)

    wts = wts_ref[...]
    los = [f[:, sl * LANES:(sl + 1) * LANES] for sl in range(_ROW_SLABS)]
    his = [f[:, _ROW_WORDS + sl * LANES:_ROW_WORDS + (sl + 1) * LANES] for sl in range(_ROW_SLABS)]
    for k in range(TOP_K):
        wk = wts[:, k:k + 1]
        for sl in range(_ROW_SLABS):
            lo, hi = _unpack_row_words(yg_scr[k, :, sl * LANES:(sl + 1) * LANES])
            los[sl] = los[sl] + wk * lo
            his[sl] = his[sl] + wk * hi
    f = jnp.concatenate(los + his, axis=1)
    g2 = mod_ref[0, 5:6, :]
    o_ref[...] = x1_ref[...] + g2 * _rms(f, gpost_ref[...])


def _combine(dest, wts_t, h2, x1, mod3, wsg, wsu, wsd, gpost, ys_rows, seq):
    t, d = x1.shape
    tm = min(TM_COMBINE, seq)
    tpb = seq // tm
    full = lambda a: pl.BlockSpec(a.shape, lambda i: (0,) * a.ndim)
    row = lambda w: pl.BlockSpec((tm, w), lambda i: (i, 0))
    return pl.pallas_call(
        _combine_kernel,
        grid=(t // tm,),
        in_specs=[pl.BlockSpec((TOP_K, tm), lambda i: (0, i), memory_space=pltpu.SMEM),
                  row(TOP_K), row(d), row(d),
                  pl.BlockSpec((1, 6, d), lambda i: (i // tpb, 0, 0)),
                  full(wsg), full(wsu), full(wsd), full(gpost),
                  pl.BlockSpec(memory_space=pl.ANY)],
        out_specs=row(d),
        out_shape=jax.ShapeDtypeStruct((t, d), F32),
        scratch_shapes=[pltpu.VMEM((TOP_K, tm, _ROW_WORDS), U32), pltpu.SemaphoreType.DMA],
        compiler_params=_params(("arbitrary",)),
        name="combine",
    )(dest, wts_t, h2, x1, mod3, wsg, wsu, wsd, gpost, ys_rows)


def _pack_weights(w_in, w_uq, w_ukv):
    d = w_in.shape[0]
    half = QK_ROPE_DIM // 2
    z = lambda n, c: jnp.zeros((n, c), F32)
    o = Q_LORA_RANK + KV_LORA_RANK
    kr = w_in[:, o:o + QK_ROPE_DIM]
    kr_grp = jnp.concatenate([z(d, QK_NOPE_DIM), kr, z(d, HEAD_PAD - QK_NOPE_DIM - QK_ROPE_DIM)], axis=1)
    kr_rot = jnp.concatenate([z(d, QK_NOPE_DIM), -kr[:, half:], kr[:, :half],
                              z(d, HEAD_PAD - QK_NOPE_DIM - QK_ROPE_DIM)], axis=1)
    win_p = jnp.concatenate([w_in[:, :o], kr_grp, kr_rot, w_in[:, o + QK_ROPE_DIM:]], axis=1)

    scale = float(QK_NOPE_DIM + QK_ROPE_DIM) ** -0.5
    r = Q_LORA_RANK
    qd = QK_NOPE_DIM + QK_ROPE_DIM
    q_grp, q_rot = [], []
    for h in range(MLA_HEADS):
        nope = w_uq[:, h * qd:h * qd + QK_NOPE_DIM]
        rope = w_uq[:, h * qd + QK_NOPE_DIM:(h + 1) * qd]
        pad = z(r, HEAD_PAD - qd)
        q_grp.append(jnp.concatenate([nope, rope, pad], axis=1))
        q_rot.append(jnp.concatenate([z(r, QK_NOPE_DIM), -rope[:, half:], rope[:, :half], pad], axis=1))
    wuq_p = jnp.concatenate(q_grp + q_rot, axis=1) * scale

    c = KV_LORA_RANK
    kd = QK_NOPE_DIM + V_HEAD_DIM
    k_grp, v_grp = [], []
    for h in range(MLA_HEADS):
        k_grp.append(jnp.concatenate([w_ukv[:, h * kd:h * kd + QK_NOPE_DIM], z(c, HEAD_PAD - QK_NOPE_DIM)], axis=1))
        v_grp.append(jnp.concatenate([w_ukv[:, h * kd + QK_NOPE_DIM:(h + 1) * kd], z(c, HEAD_PAD - V_HEAD_DIM)], axis=1))
    wukv_p = jnp.concatenate(k_grp + v_grp, axis=1)
    return win_p.astype(BF16), wuq_p.astype(BF16), wukv_p.astype(BF16)


def _rope_tables(positions):
    inv = 1.0 / (ROPE_THETA ** (jnp.arange(0, QK_ROPE_DIM, 2, dtype=F32) / QK_ROPE_DIM))
    ang = positions.astype(F32).reshape(-1)[:, None] * inv
    t = ang.shape[0]
    cos, sin = jnp.cos(ang), jnp.sin(ang)
    tail = HEAD_PAD - QK_NOPE_DIM - QK_ROPE_DIM
    cos_t = jnp.concatenate([jnp.ones((t, QK_NOPE_DIM), F32), cos, cos, jnp.ones((t, tail), F32)], axis=1)
    sin_t = jnp.concatenate([jnp.zeros((t, QK_NOPE_DIM), F32), sin, sin, jnp.zeros((t, tail), F32)], axis=1)
    return cos_t, sin_t


def _layer(x2, c, cos_t, sin_t, batch, seq, w_ada, b_ada, g_pre_mix, w_in, g_q_lat, w_uq, g_kv_lat, w_ukv,
           w_conv, g_attn_out, g_conv_out, w_out, g_post_mix, g_pre_ffn, w_router, b_router,
           w_gate, w_up, w_down, w_sh_gate, w_sh_up, w_sh_down, g_post_ffn):
    t, d = x2.shape
    r1 = lambda a: a.reshape(1, -1)

    c_pad = jnp.zeros((SUBLANES, d), F32).at[:batch].set(c)
    mod = _ada(c_pad, w_ada, r1(b_ada))[:batch]
    mod3 = mod.reshape(batch, 6, d)

    win_p, wuq_p, wukv_p = _pack_weights(w_in, w_uq, w_ukv)
    vone = jnp.zeros((1, HEAD_PAD), F32).at[0, V_HEAD_DIM].set(1.0)
    q, k, v, yc = _mix_in(x2, mod3, r1(g_pre_mix), win_p, r1(g_q_lat), wuq_p, r1(g_kv_lat), wukv_p,
                          vone, w_conv, r1(g_conv_out), cos_t, sin_t, seq)
    attn = _attention(q, k, v, batch, seq)
    x1, h2, h2p, idx, wts, rank, cnt = _mix_out(
        attn, yc, x2, mod3, r1(g_attn_out), w_out.astype(BF16), r1(g_post_mix), r1(g_pre_ffn),
        w_router.T, b_router.reshape(-1, 1), seq)

    counts = cnt[:, 0].astype(I32)
    padded = ((counts + BM_EXPERT - 1) // BM_EXPERT) * BM_EXPERT
    pad_end = jnp.cumsum(padded)
    pad_start = pad_end - padded
    m = t * TOP_K
    nb = (m + N_EXPERTS * (BM_EXPERT - 1)) // BM_EXPERT
    nused = pad_end[-1] // BM_EXPERT
    starts = jnp.minimum(jnp.arange(nb, dtype=I32), nused - 1) * BM_EXPERT
    blk_exp = jnp.minimum(jnp.sum((pad_end[None, :] <= starts[:, None]).astype(I32), axis=1), N_EXPERTS - 1)

    dest = _dest(idx, rank, pad_start.astype(F32).reshape(-1, 1))
    xs = _dispatch(dest, h2p, jnp.zeros((nb * BM_EXPERT, _ROW_WORDS), U32))
    ys = _experts(blk_exp, nused.reshape(1).astype(I32), xs, w_gate, w_up, w_down)
    return _combine(dest, wts.T, h2, x1, mod3, w_sh_gate.astype(BF16), w_sh_up.astype(BF16),
                    w_sh_down.astype(BF16), r1(g_post_ffn), ys, seq)


def kernel(x, c, positions, w_ada, b_ada, g_pre_mix, w_in, g_q_lat, w_uq, g_kv_lat, w_ukv, w_conv, g_attn_out, g_conv_out, w_out, g_post_mix, g_pre_ffn, w_router, b_router, w_gate, w_up, w_down, w_sh_gate, w_sh_up, w_sh_down, g_post_ffn):
    batch, seq, d = x.shape
    cos_t, sin_t = _rope_tables(positions)
    x2 = x.reshape(batch * seq, d)
    for l in range(w_ada.shape[0]):
        x2 = _layer(x2, c, cos_t, sin_t, batch, seq, w_ada[l], b_ada[l], g_pre_mix[l], w_in[l], g_q_lat[l],
                    w_uq[l], g_kv_lat[l], w_ukv[l], w_conv[l], g_attn_out[l], g_conv_out[l], w_out[l],
                    g_post_mix[l], g_pre_ffn[l], w_router[l], b_router[l], w_gate[l], w_up[l], w_down[l],
                    w_sh_gate[l], w_sh_up[l], w_sh_down[l], g_post_ffn[l])
    return x2.reshape(batch, seq, d)
```

```python
import functools

import jax
import jax.numpy as jnp
import numpy as np
from jax import lax
from jax.experimental import pallas as pl
from jax.experimental.pallas import tpu as pltpu

F32 = jnp.float32
BF16 = jnp.bfloat16
I32 = jnp.int32
U32 = jnp.uint32

CHUNK = 64
MLA_HEADS = 8
QK_NOPE_DIM = 64
QK_ROPE_DIM = 32
V_HEAD_DIM = 64
Q_LORA_RANK = 384
KV_LORA_RANK = 256
ROPE_THETA = 10000.0
CONV_WIDTH = 3
N_EXPERTS = 256
TOP_K = 8
N_EXPERT_GROUPS = 8
TOPK_GROUPS = 4
EXPERT_DIM = 256
ROUTED_SCALE = 2.5
EPS = 1e-6

LANES = 128
SUBLANES = 8
HEAD_PAD = LANES
VMEM_LIMIT_BYTES = 56 * 1024 * 1024

TM_IN = 256
TQ_ATTN = 512
TM_OUT = 256
TM_DEST = 512
TM_DISPATCH = 512
BM_EXPERT = 256
TM_COMBINE = 256

NEG_INF = float("-inf")


def _rms(x, g):
    return x * lax.rsqrt(jnp.mean(x * x, axis=-1, keepdims=True) + EPS) * g


_HI_MASK = np.uint32(0xFFFF0000)
_ROW_WORDS = 512
_ROW_SLABS = _ROW_WORDS // LANES


def _pack_row_words(lo, hi):
    lo_w = lax.bitcast_convert_type(lo.astype(BF16).astype(F32), U32) >> 16
    hi_w = lax.bitcast_convert_type(hi.astype(BF16).astype(F32), U32) & _HI_MASK
    return lo_w | hi_w


def _unpack_row_words(w):
    return (lax.bitcast_convert_type(w << 16, F32), lax.bitcast_convert_type(w & _HI_MASK, F32))


def _params(sem):
    return pltpu.CompilerParams(dimension_semantics=sem, vmem_limit_bytes=VMEM_LIMIT_BYTES)


def _ada_kernel(c_ref, w_ref, b_ref, o_ref):
    c = c_ref[...]
    s = c * jax.nn.sigmoid(c)
    o_ref[...] = jnp.dot(s, w_ref[...], preferred_element_type=F32,
                         precision=lax.Precision.HIGHEST) + b_ref[...]


def _ada(c_pad, w, b):
    rows, d = c_pad.shape
    n = w.shape[1]
    tn = 1536
    return pl.pallas_call(
        _ada_kernel,
        grid=(n // tn,),
        in_specs=[pl.BlockSpec((rows, d), lambda j: (0, 0)),
                  pl.BlockSpec((d, tn), lambda j: (0, j)),
                  pl.BlockSpec((1, tn), lambda j: (0, j))],
        out_specs=pl.BlockSpec((rows, tn), lambda j: (0, j)),
        out_shape=jax.ShapeDtypeStruct((rows, n), F32),
        compiler_params=_params(("arbitrary",)),
        name="ada",
    )(c_pad, w, b)


_CQ0, _CQ1 = 0, Q_LORA_RANK
_CKV0, _CKV1 = _CQ1, _CQ1 + KV_LORA_RANK
_KR0, _KR1 = _CKV1, _CKV1 + 2 * HEAD_PAD
_CONV_DIM = 512
_GB0 = _KR1
_GC0 = _GB0 + _CONV_DIM
_XV0 = _GC0 + _CONV_DIM
_WIN_COLS = _XV0 + _CONV_DIM
_QW = MLA_HEADS * HEAD_PAD


def _mix_in_kernel(tiles_per_batch, x_ref, mod_ref, gpre_ref, win_ref, gq_ref, wuq_ref, gkv_ref,
                   wukv_ref, vone_ref, wconv_ref, gconv_ref, cos_ref, sin_ref,
                   q_ref, k_ref, v_ref, yc_ref, h_scr, u_scr):
    i = pl.program_id(0)
    tm = x_ref.shape[0]
    sh1 = mod_ref[0, 0:1, :]
    sc1 = mod_ref[0, 1:2, :]
    h = _rms(x_ref[...], gpre_ref[...]) * (1.0 + sc1) + sh1
    h_scr[...] = h.astype(BF16)
    cos = cos_ref[...]
    sin = sin_ref[...]

    cq = jnp.dot(h_scr[...], win_ref[:, _CQ0:_CQ1], preferred_element_type=F32)
    cqn = _rms(cq, gq_ref[...]).astype(BF16)
    qq = jnp.dot(cqn, wuq_ref[...], preferred_element_type=F32)
    for hd in range(MLA_HEADS):
        lo = hd * HEAD_PAD
        qh = qq[:, lo:lo + HEAD_PAD] * cos + qq[:, _QW + lo:_QW + lo + HEAD_PAD] * sin
        q_ref[:, lo:lo + HEAD_PAD] = qh.astype(BF16)

    ckv = jnp.dot(h_scr[...], win_ref[:, _CKV0:_CKV1], preferred_element_type=F32)
    ckvn = _rms(ckv, gkv_ref[...]).astype(BF16)
    kv = jnp.dot(ckvn, wukv_ref[...], preferred_element_type=F32)
    krr = jnp.dot(h_scr[...], win_ref[:, _KR0:_KR1], preferred_element_type=F32)
    kr = krr[:, 0:HEAD_PAD] * cos + krr[:, HEAD_PAD:2 * HEAD_PAD] * sin
    vone = vone_ref[...]
    for hd in range(MLA_HEADS):
        lo = hd * HEAD_PAD
        k_ref[:, lo:lo + HEAD_PAD] = (kv[:, lo:lo + HEAD_PAD] + kr).astype(BF16)
        v_ref[:, lo:lo + HEAD_PAD] = (kv[:, _QW + lo:_QW + lo + HEAD_PAD] + vone).astype(BF16)

    gb = jnp.dot(h_scr[...], win_ref[:, _GB0:_GC0], preferred_element_type=F32)
    gc = jnp.dot(h_scr[...], win_ref[:, _GC0:_XV0], preferred_element_type=F32)
    xv = jnp.dot(h_scr[...], win_ref[:, _XV0:_WIN_COLS], preferred_element_type=F32)
    u = gc * xv
    prev = u_scr[tm:tm + SUBLANES, :]
    first = (i % tiles_per_batch) == 0
    u_scr[0:SUBLANES, :] = jnp.where(first, jnp.zeros_like(prev), prev)
    u_scr[SUBLANES:tm + SUBLANES, :] = u
    um1 = u_scr[SUBLANES - 1:tm + SUBLANES - 1, :]
    um2 = u_scr[SUBLANES - 2:tm + SUBLANES - 2, :]
    conv = wconv_ref[0:1, :] * um2 + wconv_ref[1:2, :] * um1 + wconv_ref[2:3, :] * u
    yc_ref[...] = _rms(gb * conv, gconv_ref[...]).astype(BF16)


def _mix_in(x2, mod3, gpre, win_p, gq, wuq_p, gkv, wukv_p, vone, wconv, gconv, cos_t, sin_t, seq):
    t, d = x2.shape
    tm = min(TM_IN, seq)
    tpb = seq // tm
    full = lambda a: pl.BlockSpec(a.shape, lambda i: (0,) * a.ndim)
    row = lambda w: pl.BlockSpec((tm, w), lambda i: (i, 0))
    return pl.pallas_call(
        functools.partial(_mix_in_kernel, tpb),
        grid=(t // tm,),
        in_specs=[row(d),
                  pl.BlockSpec((1, 6, d), lambda i: (i // tpb, 0, 0)),
                  full(gpre), full(win_p), full(gq), full(wuq_p), full(gkv), full(wukv_p),
                  full(vone), full(wconv), full(gconv), row(HEAD_PAD), row(HEAD_PAD)],
        out_specs=[row(_QW), row(_QW), row(_QW), row(_CONV_DIM)],
        out_shape=[jax.ShapeDtypeStruct((t, _QW), BF16), jax.ShapeDtypeStruct((t, _QW), BF16),
                   jax.ShapeDtypeStruct((t, _QW), BF16), jax.ShapeDtypeStruct((t, _CONV_DIM), BF16)],
        scratch_shapes=[pltpu.VMEM((tm, d), BF16), pltpu.VMEM((tm + SUBLANES, _CONV_DIM), F32)],
        compiler_params=_params(("arbitrary",)),
        name="mix_in",
    )(x2, mod3, gpre, win_p, gq, wuq_p, gkv, wukv_p, vone, wconv, gconv, cos_t, sin_t)


_HEADS_PER_STEP = 2


def _attn_kernel(q_ref, k_ref, v_ref, o_ref, s_scr, m_scr, acc_scr):
    qi = pl.program_id(2)
    tq = q_ref.shape[0]
    tk = tq
    lane_groups = tk // LANES

    def tile_max(s):
        m = s[:, 0:LANES]
        for g in range(1, lane_groups):
            m = jnp.maximum(m, s[:, g * LANES:(g + 1) * LANES])
        return m

    for hh in range(_HEADS_PER_STEP):
        lo = hh * HEAD_PAD
        q = q_ref[:, lo:lo + HEAD_PAD]

        def scores(kv):
            off = pl.multiple_of(kv * tk, tk)
            kt = k_ref[pl.ds(off, tk), lo:lo + HEAD_PAD]
            return lax.dot_general(q, kt, (((1,), (1,)), ((), ())), preferred_element_type=F32)

        m_scr[...] = jnp.full(m_scr.shape, NEG_INF, F32)

        def pass1(kv, carry):
            s = scores(kv)
            s_scr[kv] = s
            m_scr[...] = jnp.maximum(m_scr[...], tile_max(s))
            return carry

        lax.fori_loop(0, qi, pass1, 0)
        s = scores(qi)
        rc = lax.broadcasted_iota(I32, (tq, tk), 0) // CHUNK
        cc = lax.broadcasted_iota(I32, (tq, tk), 1) // CHUNK
        s = jnp.where(cc <= rc, s, NEG_INF)
        s_scr[qi] = s
        m_row = jnp.max(jnp.maximum(m_scr[...], tile_max(s)), axis=1, keepdims=True)

        acc_scr[...] = jnp.zeros(acc_scr.shape, F32)

        def pass2(kv, carry):
            off = pl.multiple_of(kv * tk, tk)
            p = jnp.exp(s_scr[kv] - m_row).astype(BF16)
            vt = v_ref[pl.ds(off, tk), lo:lo + HEAD_PAD]
            acc_scr[...] += jnp.dot(p, vt, preferred_element_type=F32)
            return carry

        lax.fori_loop(0, qi + 1, pass2, 0)
        acc = acc_scr[...]
        o = acc[:, 0:V_HEAD_DIM] / acc[:, V_HEAD_DIM:V_HEAD_DIM + 1]
        o_ref[:, hh * V_HEAD_DIM:(hh + 1) * V_HEAD_DIM] = o.astype(BF16)


def _attention(q, k, v, batch, seq):
    t = q.shape[0]
    tq = min(TQ_ATTN, seq)
    nq = seq // tq
    hw = _HEADS_PER_STEP * HEAD_PAD
    ow = _HEADS_PER_STEP * V_HEAD_DIM
    return pl.pallas_call(
        _attn_kernel,
        grid=(batch, MLA_HEADS // _HEADS_PER_STEP, nq),
        in_specs=[pl.BlockSpec((tq, hw), lambda b, j, i: (b * nq + i, j)),
                  pl.BlockSpec((seq, hw), lambda b, j, i: (b, j)),
                  pl.BlockSpec((seq, hw), lambda b, j, i: (b, j))],
        out_specs=pl.BlockSpec((tq, ow), lambda b, j, i: (b * nq + i, j)),
        out_shape=jax.ShapeDtypeStruct((t, MLA_HEADS * V_HEAD_DIM), BF16),
        scratch_shapes=[pltpu.VMEM((nq, tq, tq), F32), pltpu.VMEM((tq, LANES), F32),
                        pltpu.VMEM((tq, HEAD_PAD), F32)],
        compiler_params=_params(("arbitrary", "arbitrary", "arbitrary")),
        name="attn",
    )(q, k, v)


_GROUP_SIZE = N_EXPERTS // N_EXPERT_GROUPS
_BIG = 1.0e9


def _mix_out_kernel(attn_ref, yc_ref, x_ref, mod_ref, gattn_ref, wout_ref, gpost_ref, gpre2_ref,
                    wrt_ref, br_ref, x1_ref, h2_ref, h2p_ref, idx_ref, wts_ref, rank_ref, cnt_ref,
                    carry_scr):
    i = pl.program_id(0)
    tm = x_ref.shape[0]
    half = attn_ref.shape[1]

    @pl.when(i == 0)
    def _():
        carry_scr[...] = jnp.zeros(carry_scr.shape, F32)

    an = _rms(attn_ref[...].astype(F32), gattn_ref[...]).astype(BF16)
    mix = (jnp.dot(an, wout_ref[0:half, :], preferred_element_type=F32)
           + jnp.dot(yc_ref[...], wout_ref[half:, :], preferred_element_type=F32))
    g1 = mod_ref[0, 2:3, :]
    sh2 = mod_ref[0, 3:4, :]
    sc2 = mod_ref[0, 4:5, :]
    x1 = x_ref[...] + g1 * _rms(mix, gpost_ref[...])
    x1_ref[...] = x1
    h2 = _rms(x1, gpre2_ref[...]) * (1.0 + sc2) + sh2
    h2_ref[...] = h2.astype(BF16)
    words = _pack_row_words(h2[:, 0:_ROW_WORDS], h2[:, _ROW_WORDS:])
    h2p_ref[...] = words

    logits = lax.dot_general(wrt_ref[...], h2, (((1,), (1,)), ((), ())),
                             preferred_element_type=F32, precision=lax.Precision.HIGHEST)
    scores = jax.nn.sigmoid(logits)
    sel = scores + br_ref[...]
    row = lax.broadcasted_iota(I32, (N_EXPERTS, tm), 0).astype(F32)

    gscore = []
    rw = lax.broadcasted_iota(I32, (_GROUP_SIZE, tm), 0).astype(F32)
    for g in range(N_EXPERT_GROUPS):
        blk = sel[g * _GROUP_SIZE:(g + 1) * _GROUP_SIZE, :]
        m1 = jnp.max(blk, axis=0, keepdims=True)
        i1 = jnp.min(jnp.where(blk == m1, rw, _BIG), axis=0, keepdims=True)
        m2 = jnp.max(jnp.where(rw == i1, NEG_INF, blk), axis=0, keepdims=True)
        gscore.append(m1 + m2)

    gkeep = [jnp.zeros((1, tm), F32) for _ in range(N_EXPERT_GROUPS)]
    for _ in range(TOPK_GROUPS):
        mg = functools.reduce(jnp.maximum, gscore)
        ig = functools.reduce(jnp.minimum, [jnp.where(gscore[g] == mg, float(g), _BIG)
                                            for g in range(N_EXPERT_GROUPS)])
        for g in range(N_EXPERT_GROUPS):
            hit = ig == float(g)
            gkeep[g] = jnp.where(hit, 1.0, gkeep[g])
            gscore[g] = jnp.where(hit, NEG_INF, gscore[g])
    cur = jnp.concatenate(
        [jnp.where(gkeep[g] > 0.0, sel[g * _GROUP_SIZE:(g + 1) * _GROUP_SIZE, :], NEG_INF)
         for g in range(N_EXPERT_GROUPS)], axis=0)

    krow = lax.broadcasted_iota(I32, (TOP_K, tm), 0)
    idx_rows = []
    idx_f = jnp.zeros((TOP_K, tm), F32)
    sc_k = jnp.zeros((TOP_K, tm), F32)
    sc_sum = jnp.zeros((1, tm), F32)
    onehot = jnp.zeros((N_EXPERTS, tm), F32)
    for k in range(TOP_K):
        m = jnp.max(cur, axis=0, keepdims=True)
        ik = jnp.min(jnp.where(cur == m, row, _BIG), axis=0, keepdims=True)
        hit = row == ik
        sk = jnp.sum(jnp.where(hit, scores, 0.0), axis=0, keepdims=True)
        cur = jnp.where(hit, NEG_INF, cur)
        onehot = jnp.where(hit, 1.0, onehot)
        idx_rows.append(ik)
        idx_f = jnp.where(krow == k, ik, idx_f)
        sc_k = jnp.where(krow == k, sk, sc_k)
        sc_sum = sc_sum + sk
    wts_ref[...] = sc_k / sc_sum * ROUTED_SCALE
    idx_ref[...] = idx_f.astype(I32)

    tri = (lax.broadcasted_iota(I32, (tm, tm), 0) < lax.broadcasted_iota(I32, (tm, tm), 1))
    excl = jnp.dot(onehot.astype(BF16), tri.astype(BF16), preferred_element_type=F32)
    rank_e = carry_scr[:, 0:1] + excl
    rank_k = jnp.zeros((TOP_K, tm), F32)
    for k in range(TOP_K):
        hit = row == idx_rows[k]
        rk = jnp.sum(jnp.where(hit, rank_e, 0.0), axis=0, keepdims=True)
        rank_k = jnp.where(krow == k, rk, rank_k)
    rank_ref[...] = rank_k.astype(I32)
    carry_scr[...] = carry_scr[...] + jnp.sum(onehot, axis=1, keepdims=True)
    cnt_ref[...] = carry_scr[...]


def _mix_out(attn, yc, x2, mod3, gattn, wout, gpost, gpre2, wrt, br, seq):
    t, d = x2.shape
    tm = min(TM_OUT, seq)
    tpb = seq // tm
    full = lambda a: pl.BlockSpec(a.shape, lambda i: (0,) * a.ndim)
    row = lambda w: pl.BlockSpec((tm, w), lambda i: (i, 0))
    col = pl.BlockSpec((TOP_K, tm), lambda i: (0, i))
    return pl.pallas_call(
        _mix_out_kernel,
        grid=(t // tm,),
        in_specs=[row(attn.shape[1]), row(yc.shape[1]), row(d),
                  pl.BlockSpec((1, 6, d), lambda i: (i // tpb, 0, 0)),
                  full(gattn), full(wout), full(gpost), full(gpre2), full(wrt), full(br)],
        out_specs=[row(d), row(d), row(_ROW_WORDS), col, col, col,
                   pl.BlockSpec((N_EXPERTS, LANES), lambda i: (0, 0))],
        out_shape=[jax.ShapeDtypeStruct((t, d), F32), jax.ShapeDtypeStruct((t, d), BF16),
                   jax.ShapeDtypeStruct((t, _ROW_WORDS), U32),
                   jax.ShapeDtypeStruct((TOP_K, t), I32), jax.ShapeDtypeStruct((TOP_K, t), F32),
                   jax.ShapeDtypeStruct((TOP_K, t), I32),
                   jax.ShapeDtypeStruct((N_EXPERTS, LANES), F32)],
        scratch_shapes=[pltpu.VMEM((N_EXPERTS, LANES), F32)],
        compiler_params=_params(("arbitrary",)),
        name="mix_out",
    )(attn, yc, x2, mod3, gattn, wout, gpost, gpre2, wrt, br)


def _dest_kernel(idx_ref, rank_ref, pstart_ref, dest_ref):
    tm = idx_ref.shape[1]
    row = lax.broadcasted_iota(I32, (N_EXPERTS, tm), 0)
    krow = lax.broadcasted_iota(I32, (TOP_K, tm), 0)
    pstart = pstart_ref[...]
    idx = idx_ref[...]
    out = jnp.zeros((TOP_K, tm), F32)
    for k in range(TOP_K):
        hit = row == idx[k:k + 1, :]
        base = jnp.sum(jnp.where(hit, pstart, 0.0), axis=0, keepdims=True)
        out = jnp.where(krow == k, base, out)
    dest_ref[...] = out.astype(I32) + rank_ref[...]


def _dest(idx, rank, pstart):
    t = idx.shape[1]
    tm = min(TM_DEST, t)
    col = pl.BlockSpec((TOP_K, tm), lambda i: (0, i))
    return pl.pallas_call(
        _dest_kernel,
        grid=(t // tm,),
        in_specs=[col, col, pl.BlockSpec((N_EXPERTS, 1), lambda i: (0, 0))],
        out_specs=col,
        out_shape=jax.ShapeDtypeStruct((TOP_K, t), I32),
        compiler_params=_params(("arbitrary",)),
        name="dest",
    )(idx, rank, pstart)


def _dispatch_kernel(dest_ref, h2_ref, xs_in_hbm, xs_hbm, sem):
    del xs_in_hbm
    tm = dest_ref.shape[1]

    def row_copy(t, k):
        return pltpu.make_async_copy(h2_ref.at[pl.ds(t, 1), :], xs_hbm.at[pl.ds(dest_ref[k, t], 1), :], sem)

    def issue(t, carry):
        for k in range(TOP_K):
            row_copy(t, k).start()
        return carry

    lax.fori_loop(0, tm, issue, 0)

    def drain(t, carry):
        for k in range(TOP_K):
            row_copy(t, k).wait()
        return carry

    lax.fori_loop(0, tm, drain, 0)


def _dispatch(dest, h2_rows, xs_zero):
    t = h2_rows.shape[0]
    tm = min(TM_DISPATCH, t)
    return pl.pallas_call(
        _dispatch_kernel,
        grid=(t // tm,),
        in_specs=[pl.BlockSpec((TOP_K, tm), lambda i: (0, i), memory_space=pltpu.SMEM),
                  pl.BlockSpec((tm, _ROW_WORDS), lambda i: (i, 0)),
                  pl.BlockSpec(memory_space=pl.ANY)],
        out_specs=pl.BlockSpec(memory_space=pl.ANY),
        out_shape=jax.ShapeDtypeStruct(xs_zero.shape, xs_zero.dtype),
        scratch_shapes=[pltpu.SemaphoreType.DMA],
        input_output_aliases={2: 0},
        compiler_params=_params(("arbitrary",)),
        name="dispatch",
    )(dest, h2_rows, xs_zero)


_XS_SLOTS = 4
_YS_SLOTS = 2
_W_SLOTS = 2


def _expert_kernel(first_ref, ord_ref, uexp_ref, meta_ref, xs_hbm, wg_hbm, wu_hbm, wd_hbm, ys_hbm,
                   xs_buf, ys_buf, wg_buf, wu_buf, wd_buf, wgu_scr, wd_scr, x_scr, xs_sem, ys_sem, w_sem):
    i = pl.program_id(0)
    nused = meta_ref[0]
    nexp = meta_ref[1]
    bm = xs_buf.shape[1]

    def xs_copy(b, slot):
        return pltpu.make_async_copy(xs_hbm.at[pl.ds(b * bm, bm), :], xs_buf.at[slot], xs_sem.at[slot])

    def ys_copy(b, slot):
        return pltpu.make_async_copy(ys_buf.at[slot], ys_hbm.at[pl.ds(b * bm, bm), :], ys_sem.at[slot])

    def w_copies(j, slot):
        e = uexp_ref[j]
        return (pltpu.make_async_copy(wg_hbm.at[e], wg_buf.at[slot], w_sem.at[slot, 0]),
                pltpu.make_async_copy(wu_hbm.at[e], wu_buf.at[slot], w_sem.at[slot, 1]),
                pltpu.make_async_copy(wd_hbm.at[e], wd_buf.at[slot], w_sem.at[slot, 2]))

    @pl.when(i == 0)
    def _():
        for s in range(_XS_SLOTS - 1):
            @pl.when(s < nused)
            def _():
                xs_copy(s, s).start()
        for s in range(_W_SLOTS):
            @pl.when(s < nexp)
            def _():
                for cp in w_copies(s, s):
                    cp.start()

    @pl.when(i < nused)
    def _():
        ahead = i + _XS_SLOTS - 1

        @pl.when(ahead < nused)
        def _():
            xs_copy(ahead, ahead % _XS_SLOTS).start()

        @pl.when(first_ref[i] == 1)
        def _():
            j = ord_ref[i]
            ws = j % _W_SLOTS
            for cp in w_copies(j, ws):
                cp.wait()
            wgu_scr[:, 0:EXPERT_DIM] = wg_buf[ws].astype(BF16)
            wgu_scr[:, EXPERT_DIM:2 * EXPERT_DIM] = wu_buf[ws].astype(BF16)
            wd_scr[...] = wd_buf[ws].astype(BF16)

            @pl.when(j + _W_SLOTS < nexp)
            def _():
                for cp in w_copies(j + _W_SLOTS, ws):
                    cp.start()

        slot = i % _XS_SLOTS
        xs_copy(i, slot).wait()
        lo, hi = _unpack_row_words(xs_buf[slot])
        x_scr[:, 0:_ROW_WORDS] = lo.astype(BF16)
        x_scr[:, _ROW_WORDS:] = hi.astype(BF16)
        gu = jnp.dot(x_scr[...], wgu_scr[...], preferred_element_type=F32)
        g = gu[:, 0:EXPERT_DIM]
        a = (g * jax.nn.sigmoid(g) * gu[:, EXPERT_DIM:2 * EXPERT_DIM]).astype(BF16)
        y = jnp.dot(a, wd_scr[...], preferred_element_type=F32)
        oslot = i % _YS_SLOTS

        @pl.when(i >= _YS_SLOTS)
        def _():
            ys_copy(i - _YS_SLOTS, oslot).wait()

        ys_buf[oslot] = _pack_row_words(y[:, 0:_ROW_WORDS], y[:, _ROW_WORDS:])
        ys_copy(i, oslot).start()

        @pl.when(i == nused - 1)
        def _():
            ys_copy(i, oslot).wait()

            @pl.when(i >= 1)
            def _():
                ys_copy(i - 1, (i - 1) % _YS_SLOTS).wait()


def _experts(first, ordinal, uexp, meta, xs, w_gate, w_up, w_down):
    p = xs.shape[0]
    d = w_gate.shape[1]
    nb = p // BM_EXPERT
    anyspec = pl.BlockSpec(memory_space=pl.ANY)
    return pl.pallas_call(
        _expert_kernel,
        grid_spec=pltpu.PrefetchScalarGridSpec(
            num_scalar_prefetch=4,
            grid=(nb,),
            in_specs=[anyspec, anyspec, anyspec, anyspec],
            out_specs=anyspec,
            scratch_shapes=[pltpu.VMEM((_XS_SLOTS, BM_EXPERT, _ROW_WORDS), U32),
                            pltpu.VMEM((_YS_SLOTS, BM_EXPERT, _ROW_WORDS), U32),
                            pltpu.VMEM((_W_SLOTS, d, EXPERT_DIM), F32),
                            pltpu.VMEM((_W_SLOTS, d, EXPERT_DIM), F32),
                            pltpu.VMEM((_W_SLOTS, EXPERT_DIM, d), F32),
                            pltpu.VMEM((d, 2 * EXPERT_DIM), BF16), pltpu.VMEM((EXPERT_DIM, d), BF16),
                            pltpu.VMEM((BM_EXPERT, d), BF16),
                            pltpu.SemaphoreType.DMA((_XS_SLOTS,)), pltpu.SemaphoreType.DMA((_YS_SLOTS,)),
                            pltpu.SemaphoreType.DMA((_W_SLOTS, 3))]),
        out_shape=jax.ShapeDtypeStruct((p, _ROW_WORDS), U32),
        compiler_params=_params(("arbitrary",)),
        name="experts",
    )(first, ordinal, uexp, meta, xs, w_gate, w_up, w_down)


def _combine_kernel(dest_ref, wts_ref, h2_ref, x1_ref, mod_ref, wsg_ref, wsu_ref, wsd_ref, gpost_ref,
                    ys_hbm, o_ref, yg_scr, sem):
    tm = x1_ref.shape[0]

    def row_copy(t, k):
        return pltpu.make_async_copy(ys_hbm.at[pl.ds(dest_ref[k, t], 1), :], yg_scr.at[k, pl.ds(t, 1), :], sem)

    def issue(t, carry):
        for k in range(TOP_K):
            row_copy(t, k).start()
        return carry

    lax.fori_loop(0, tm, issue, 0)

    h2 = h2_ref[...]
    g = jnp.dot(h2, wsg_ref[...], preferred_element_type=F32)
    u = jnp.dot(h2, wsu_ref[...], preferred_element_type=F32)
    f = jnp.dot((g * jax.nn.sigmoid(g) * u).astype(BF16), wsd_ref[...], preferred_element_type=F32)

    def drain(t, carry):
        for k in range(TOP_K):
            row_copy(t, k).wait()
        return carry

    lax.fori_loop(0, tm, drain, 0)

    wts = wts_ref[...]
    los = [f[:, sl * LANES:(sl + 1) * LANES] for sl in range(_ROW_SLABS)]
    his = [f[:, _ROW_WORDS + sl * LANES:_ROW_WORDS + (sl + 1) * LANES] for sl in range(_ROW_SLABS)]
    for k in range(TOP_K):
        wk = wts[:, k:k + 1]
        for sl in range(_ROW_SLABS):
            lo, hi = _unpack_row_words(yg_scr[k, :, sl * LANES:(sl + 1) * LANES])
            los[sl] = los[sl] + wk * lo
            his[sl] = his[sl] + wk * hi
    f = jnp.concatenate(los + his, axis=1)
    g2 = mod_ref[0, 5:6, :]
    o_ref[...] = x1_ref[...] + g2 * _rms(f, gpost_ref[...])


def _combine(dest, wts_t, h2, x1, mod3, wsg, wsu, wsd, gpost, ys_rows, seq):
    t, d = x1.shape
    tm = min(TM_COMBINE, seq)
    tpb = seq // tm
    full = lambda a: pl.BlockSpec(a.shape, lambda i: (0,) * a.ndim)
    row = lambda w: pl.BlockSpec((tm, w), lambda i: (i, 0))
    return pl.pallas_call(
        _combine_kernel,
        grid=(t // tm,),
        in_specs=[pl.BlockSpec((TOP_K, tm), lambda i: (0, i), memory_space=pltpu.SMEM),
                  row(TOP_K), row(d), row(d),
                  pl.BlockSpec((1, 6, d), lambda i: (i // tpb, 0, 0)),
                  full(wsg), full(wsu), full(wsd), full(gpost),
                  pl.BlockSpec(memory_space=pl.ANY)],
        out_specs=row(d),
        out_shape=jax.ShapeDtypeStruct((t, d), F32),
        scratch_shapes=[pltpu.VMEM((TOP_K, tm, _ROW_WORDS), U32), pltpu.SemaphoreType.DMA],
        compiler_params=_params(("arbitrary",)),
        name="combine",
    )(dest, wts_t, h2, x1, mod3, wsg, wsu, wsd, gpost, ys_rows)


def _pack_weights(w_in, w_uq, w_ukv):
    d = w_in.shape[0]
    half = QK_ROPE_DIM // 2
    z = lambda n, c: jnp.zeros((n, c), F32)
    o = Q_LORA_RANK + KV_LORA_RANK
    kr = w_in[:, o:o + QK_ROPE_DIM]
    kr_grp = jnp.concatenate([z(d, QK_NOPE_DIM), kr, z(d, HEAD_PAD - QK_NOPE_DIM - QK_ROPE_DIM)], axis=1)
    kr_rot = jnp.concatenate([z(d, QK_NOPE_DIM), -kr[:, half:], kr[:, :half],
                              z(d, HEAD_PAD - QK_NOPE_DIM - QK_ROPE_DIM)], axis=1)
    win_p = jnp.concatenate([w_in[:, :o], kr_grp, kr_rot, w_in[:, o + QK_ROPE_DIM:]], axis=1)

    scale = float(QK_NOPE_DIM + QK_ROPE_DIM) ** -0.5
    r = Q_LORA_RANK
    qd = QK_NOPE_DIM + QK_ROPE_DIM
    q_grp, q_rot = [], []
    for h in range(MLA_HEADS):
        nope = w_uq[:, h * qd:h * qd + QK_NOPE_DIM]
        rope = w_uq[:, h * qd + QK_NOPE_DIM:(h + 1) * qd]
        pad = z(r, HEAD_PAD - qd)
        q_grp.append(jnp.concatenate([nope, rope, pad], axis=1))
        q_rot.append(jnp.concatenate([z(r, QK_NOPE_DIM), -rope[:, half:], rope[:, :half], pad], axis=1))
    wuq_p = jnp.concatenate(q_grp + q_rot, axis=1) * scale

    c = KV_LORA_RANK
    kd = QK_NOPE_DIM + V_HEAD_DIM
    k_grp, v_grp = [], []
    for h in range(MLA_HEADS):
        k_grp.append(jnp.concatenate([w_ukv[:, h * kd:h * kd + QK_NOPE_DIM], z(c, HEAD_PAD - QK_NOPE_DIM)], axis=1))
        v_grp.append(jnp.concatenate([w_ukv[:, h * kd + QK_NOPE_DIM:(h + 1) * kd], z(c, HEAD_PAD - V_HEAD_DIM)], axis=1))
    wukv_p = jnp.concatenate(k_grp + v_grp, axis=1)
    return win_p.astype(BF16), wuq_p.astype(BF16), wukv_p.astype(BF16)


def _rope_tables(positions):
    inv = 1.0 / (ROPE_THETA ** (jnp.arange(0, QK_ROPE_DIM, 2, dtype=F32) / QK_ROPE_DIM))
    ang = positions.astype(F32).reshape(-1)[:, None] * inv
    t = ang.shape[0]
    cos, sin = jnp.cos(ang), jnp.sin(ang)
    tail = HEAD_PAD - QK_NOPE_DIM - QK_ROPE_DIM
    cos_t = jnp.concatenate([jnp.ones((t, QK_NOPE_DIM), F32), cos, cos, jnp.ones((t, tail), F32)], axis=1)
    sin_t = jnp.concatenate([jnp.zeros((t, QK_NOPE_DIM), F32), sin, sin, jnp.zeros((t, tail), F32)], axis=1)
    return cos_t, sin_t


def _layer(x2, c, cos_t, sin_t, batch, seq, w_ada, b_ada, g_pre_mix, w_in, g_q_lat, w_uq, g_kv_lat, w_ukv,
           w_conv, g_attn_out, g_conv_out, w_out, g_post_mix, g_pre_ffn, w_router, b_router,
           w_gate, w_up, w_down, w_sh_gate, w_sh_up, w_sh_down, g_post_ffn):
    t, d = x2.shape
    r1 = lambda a: a.reshape(1, -1)

    c_pad = jnp.zeros((SUBLANES, d), F32).at[:batch].set(c)
    mod = _ada(c_pad, w_ada, r1(b_ada))[:batch]
    mod3 = mod.reshape(batch, 6, d)

    win_p, wuq_p, wukv_p = _pack_weights(w_in, w_uq, w_ukv)
    vone = jnp.zeros((1, HEAD_PAD), F32).at[0, V_HEAD_DIM].set(1.0)
    q, k, v, yc = _mix_in(x2, mod3, r1(g_pre_mix), win_p, r1(g_q_lat), wuq_p, r1(g_kv_lat), wukv_p,
                          vone, w_conv, r1(g_conv_out), cos_t, sin_t, seq)
    attn = _attention(q, k, v, batch, seq)
    x1, h2, h2p, idx, wts, rank, cnt = _mix_out(
        attn, yc, x2, mod3, r1(g_attn_out), w_out.astype(BF16), r1(g_post_mix), r1(g_pre_ffn),
        w_router.T, b_router.reshape(-1, 1), seq)

    counts = cnt[:, 0].astype(I32)
    padded = ((counts + BM_EXPERT - 1) // BM_EXPERT) * BM_EXPERT
    pad_end = jnp.cumsum(padded)
    pad_start = pad_end - padded
    m = t * TOP_K
    nb = (m + N_EXPERTS * (BM_EXPERT - 1)) // BM_EXPERT
    nused = pad_end[-1] // BM_EXPERT
    bidx = jnp.arange(nb, dtype=I32)
    blk_exp = jnp.sum((pad_end[None, :] <= (bidx * BM_EXPERT)[:, None]).astype(I32), axis=1)
    first = ((bidx < nused) & ((bidx == 0) | (blk_exp != jnp.roll(blk_exp, 1)))).astype(I32)
    ordinal = jnp.maximum(jnp.cumsum(first) - 1, 0).astype(I32)
    seen = jnp.cumsum((counts > 0).astype(I32))
    uexp = jnp.minimum(jnp.sum((seen[None, :] <= jnp.arange(N_EXPERTS, dtype=I32)[:, None]).astype(I32), axis=1),
                       N_EXPERTS - 1).astype(I32)
    meta = jnp.stack([nused, seen[-1]]).astype(I32)

    dest = _dest(idx, rank, pad_start.astype(F32).reshape(-1, 1))
    xs = _dispatch(dest, h2p, jnp.zeros((nb * BM_EXPERT, _ROW_WORDS), U32))
    ys = _experts(first, ordinal, uexp, meta, xs, w_gate, w_up, w_down)
    return _combine(dest, wts.T, h2, x1, mod3, w_sh_gate.astype(BF16), w_sh_up.astype(BF16),
                    w_sh_down.astype(BF16), r1(g_post_ffn), ys, seq)


def kernel(x, c, positions, w_ada, b_ada, g_pre_mix, w_in, g_q_lat, w_uq, g_kv_lat, w_ukv, w_conv, g_attn_out, g_conv_out, w_out, g_post_mix, g_pre_ffn, w_router, b_router, w_gate, w_up, w_down, w_sh_gate, w_sh_up, w_sh_down, g_post_ffn):
    batch, seq, d = x.shape
    cos_t, sin_t = _rope_tables(positions)
    x2 = x.reshape(batch * seq, d)
    for l in range(w_ada.shape[0]):
        x2 = _layer(x2, c, cos_t, sin_t, batch, seq, w_ada[l], b_ada[l], g_pre_mix[l], w_in[l], g_q_lat[l],
                    w_uq[l], g_kv_lat[l], w_ukv[l], w_conv[l], g_attn_out[l], g_conv_out[l], w_out[l],
                    g_post_mix[l], g_pre_ffn[l], w_router[l], b_router[l], w_gate[l], w_up[l], w_down[l],
                    w_sh_gate[l], w_sh_up[l], w_sh_down[l], g_post_ffn[l])
    return x2.reshape(batch, seq, d)
```

```python
import functools

import jax
import jax.numpy as jnp
import numpy as np
from jax import lax
from jax.experimental import pallas as pl
from jax.experimental.pallas import tpu as pltpu

F32 = jnp.float32
BF16 = jnp.bfloat16
I32 = jnp.int32
U32 = jnp.uint32

CHUNK = 64
MLA_HEADS = 8
QK_NOPE_DIM = 64
QK_ROPE_DIM = 32
V_HEAD_DIM = 64
Q_LORA_RANK = 384
KV_LORA_RANK = 256
ROPE_THETA = 10000.0
CONV_WIDTH = 3
N_EXPERTS = 256
TOP_K = 8
N_EXPERT_GROUPS = 8
TOPK_GROUPS = 4
EXPERT_DIM = 256
ROUTED_SCALE = 2.5
EPS = 1e-6

LANES = 128
SUBLANES = 8
HEAD_PAD = LANES
VMEM_LIMIT_BYTES = 56 * 1024 * 1024

TM_IN = 256
TQ_ATTN = 512
TM_OUT = 256
TM_DEST = 512
TM_DISPATCH = 512
BM_EXPERT = 256
TM_COMBINE = 256

NEG_INF = float("-inf")


def _rms(x, g):
    return x * lax.rsqrt(jnp.mean(x * x, axis=-1, keepdims=True) + EPS) * g


_HI_MASK = np.uint32(0xFFFF0000)
_ROW_WORDS = 512
_ROW_SLABS = _ROW_WORDS // LANES


def _pack_row_words(lo, hi):
    lo_w = lax.bitcast_convert_type(lo.astype(BF16).astype(F32), U32) >> 16
    hi_w = lax.bitcast_convert_type(hi.astype(BF16).astype(F32), U32) & _HI_MASK
    return lo_w | hi_w


def _unpack_row_words(w):
    return (lax.bitcast_convert_type(w << 16, F32), lax.bitcast_convert_type(w & _HI_MASK, F32))


def _params(sem):
    return pltpu.CompilerParams(dimension_semantics=sem, vmem_limit_bytes=VMEM_LIMIT_BYTES)


def _ada_kernel(c_ref, w_ref, b_ref, o_ref):
    c = c_ref[...]
    s = c * jax.nn.sigmoid(c)
    o_ref[...] = jnp.dot(s, w_ref[...], preferred_element_type=F32,
                         precision=lax.Precision.HIGHEST) + b_ref[...]


def _ada(c_pad, w, b):
    rows, d = c_pad.shape
    n = w.shape[1]
    tn = 1536
    return pl.pallas_call(
        _ada_kernel,
        grid=(n // tn,),
        in_specs=[pl.BlockSpec((rows, d), lambda j: (0, 0)),
                  pl.BlockSpec((d, tn), lambda j: (0, j)),
                  pl.BlockSpec((1, tn), lambda j: (0, j))],
        out_specs=pl.BlockSpec((rows, tn), lambda j: (0, j)),
        out_shape=jax.ShapeDtypeStruct((rows, n), F32),
        compiler_params=_params(("arbitrary",)),
        name="ada",
    )(c_pad, w, b)


_CQ0, _CQ1 = 0, Q_LORA_RANK
_CKV0, _CKV1 = _CQ1, _CQ1 + KV_LORA_RANK
_KR0, _KR1 = _CKV1, _CKV1 + 2 * HEAD_PAD
_CONV_DIM = 512
_GB0 = _KR1
_GC0 = _GB0 + _CONV_DIM
_XV0 = _GC0 + _CONV_DIM
_WIN_COLS = _XV0 + _CONV_DIM
_QW = MLA_HEADS * HEAD_PAD


def _mix_in_kernel(tiles_per_batch, x_ref, mod_ref, gpre_ref, win_ref, gq_ref, wuq_ref, gkv_ref,
                   wukv_ref, vone_ref, wconv_ref, gconv_ref, cos_ref, sin_ref,
                   q_ref, k_ref, v_ref, yc_ref, h_scr, u_scr):
    i = pl.program_id(0)
    tm = x_ref.shape[0]
    sh1 = mod_ref[0, 0:1, :]
    sc1 = mod_ref[0, 1:2, :]
    h = _rms(x_ref[...], gpre_ref[...]) * (1.0 + sc1) + sh1
    h_scr[...] = h.astype(BF16)
    cos = cos_ref[...]
    sin = sin_ref[...]

    cq = jnp.dot(h_scr[...], win_ref[:, _CQ0:_CQ1], preferred_element_type=F32)
    cqn = _rms(cq, gq_ref[...]).astype(BF16)
    qq = jnp.dot(cqn, wuq_ref[...], preferred_element_type=F32)
    for hd in range(MLA_HEADS):
        lo = hd * HEAD_PAD
        qh = qq[:, lo:lo + HEAD_PAD] * cos + qq[:, _QW + lo:_QW + lo + HEAD_PAD] * sin
        q_ref[:, lo:lo + HEAD_PAD] = qh.astype(BF16)

    ckv = jnp.dot(h_scr[...], win_ref[:, _CKV0:_CKV1], preferred_element_type=F32)
    ckvn = _rms(ckv, gkv_ref[...]).astype(BF16)
    kv = jnp.dot(ckvn, wukv_ref[...], preferred_element_type=F32)
    krr = jnp.dot(h_scr[...], win_ref[:, _KR0:_KR1], preferred_element_type=F32)
    kr = krr[:, 0:HEAD_PAD] * cos + krr[:, HEAD_PAD:2 * HEAD_PAD] * sin
    vone = vone_ref[...]
    for hd in range(MLA_HEADS):
        lo = hd * HEAD_PAD
        k_ref[:, lo:lo + HEAD_PAD] = (kv[:, lo:lo + HEAD_PAD] + kr).astype(BF16)
        v_ref[:, lo:lo + HEAD_PAD] = (kv[:, _QW + lo:_QW + lo + HEAD_PAD] + vone).astype(BF16)

    gb = jnp.dot(h_scr[...], win_ref[:, _GB0:_GC0], preferred_element_type=F32)
    gc = jnp.dot(h_scr[...], win_ref[:, _GC0:_XV0], preferred_element_type=F32)
    xv = jnp.dot(h_scr[...], win_ref[:, _XV0:_WIN_COLS], preferred_element_type=F32)
    u = gc * xv
    prev = u_scr[tm:tm + SUBLANES, :]
    first = (i % tiles_per_batch) == 0
    u_scr[0:SUBLANES, :] = jnp.where(first, jnp.zeros_like(prev), prev)
    u_scr[SUBLANES:tm + SUBLANES, :] = u
    um1 = u_scr[SUBLANES - 1:tm + SUBLANES - 1, :]
    um2 = u_scr[SUBLANES - 2:tm + SUBLANES - 2, :]
    conv = wconv_ref[0:1, :] * um2 + wconv_ref[1:2, :] * um1 + wconv_ref[2:3, :] * u
    yc_ref[...] = _rms(gb * conv, gconv_ref[...]).astype(BF16)


def _mix_in(x2, mod3, gpre, win_p, gq, wuq_p, gkv, wukv_p, vone, wconv, gconv, cos_t, sin_t, seq):
    t, d = x2.shape
    tm = min(TM_IN, seq)
    tpb = seq // tm
    full = lambda a: pl.BlockSpec(a.shape, lambda i: (0,) * a.ndim)
    row = lambda w: pl.BlockSpec((tm, w), lambda i: (i, 0))
    return pl.pallas_call(
        functools.partial(_mix_in_kernel, tpb),
        grid=(t // tm,),
        in_specs=[row(d),
                  pl.BlockSpec((1, 6, d), lambda i: (i // tpb, 0, 0)),
                  full(gpre), full(win_p), full(gq), full(wuq_p), full(gkv), full(wukv_p),
                  full(vone), full(wconv), full(gconv), row(HEAD_PAD), row(HEAD_PAD)],
        out_specs=[row(_QW), row(_QW), row(_QW), row(_CONV_DIM)],
        out_shape=[jax.ShapeDtypeStruct((t, _QW), BF16), jax.ShapeDtypeStruct((t, _QW), BF16),
                   jax.ShapeDtypeStruct((t, _QW), BF16), jax.ShapeDtypeStruct((t, _CONV_DIM), BF16)],
        scratch_shapes=[pltpu.VMEM((tm, d), BF16), pltpu.VMEM((tm + SUBLANES, _CONV_DIM), F32)],
        compiler_params=_params(("arbitrary",)),
        name="mix_in",
    )(x2, mod3, gpre, win_p, gq, wuq_p, gkv, wukv_p, vone, wconv, gconv, cos_t, sin_t)


_HEADS_PER_STEP = 2


def _attn_kernel(q_ref, k_ref, v_ref, o_ref, s_scr, m_scr, acc_scr):
    qi = pl.program_id(2)
    tq = q_ref.shape[0]
    tk = tq
    lane_groups = tk // LANES

    def tile_max(s):
        m = s[:, 0:LANES]
        for g in range(1, lane_groups):
            m = jnp.maximum(m, s[:, g * LANES:(g + 1) * LANES])
        return m

    heads = range(_HEADS_PER_STEP)
    lanes = [slice(hh * HEAD_PAD, (hh + 1) * HEAD_PAD) for hh in heads]

    def scores(hh, kv):
        off = pl.multiple_of(kv * tk, tk)
        return lax.dot_general(q_ref[:, lanes[hh]], k_ref[pl.ds(off, tk), lanes[hh]],
                               (((1,), (1,)), ((), ())), preferred_element_type=F32)

    m_scr[...] = jnp.full(m_scr.shape, NEG_INF, F32)

    def pass1(kv, carry):
        for hh in heads:
            s = scores(hh, kv)
            s_scr[hh, kv] = s
            m_scr[hh] = jnp.maximum(m_scr[hh], tile_max(s))
        return carry

    lax.fori_loop(0, qi, pass1, 0)
    rc = lax.broadcasted_iota(I32, (tq, tk), 0) // CHUNK
    cc = lax.broadcasted_iota(I32, (tq, tk), 1) // CHUNK
    for hh in heads:
        s = jnp.where(cc <= rc, scores(hh, qi), NEG_INF)
        s_scr[hh, qi] = s
        m_row = jnp.max(jnp.maximum(m_scr[hh], tile_max(s)), axis=1, keepdims=True)
        m_scr[hh] = jnp.broadcast_to(m_row, (tq, LANES))

    acc_scr[...] = jnp.zeros(acc_scr.shape, F32)

    def pass2(kv, carry):
        off = pl.multiple_of(kv * tk, tk)
        for hh in heads:
            mb = m_scr[hh]
            p = jnp.concatenate(
                [jnp.exp(s_scr[hh, kv, :, g * LANES:(g + 1) * LANES] - mb) for g in range(lane_groups)],
                axis=1).astype(BF16)
            acc_scr[hh] += jnp.dot(p, v_ref[pl.ds(off, tk), lanes[hh]], preferred_element_type=F32)
        return carry

    lax.fori_loop(0, qi + 1, pass2, 0)
    for hh in heads:
        acc = acc_scr[hh]
        o = acc[:, 0:V_HEAD_DIM] / acc[:, V_HEAD_DIM:V_HEAD_DIM + 1]
        o_ref[:, hh * V_HEAD_DIM:(hh + 1) * V_HEAD_DIM] = o.astype(BF16)


def _attention(q, k, v, batch, seq):
    t = q.shape[0]
    tq = min(TQ_ATTN, seq)
    nq = seq // tq
    hw = _HEADS_PER_STEP * HEAD_PAD
    ow = _HEADS_PER_STEP * V_HEAD_DIM
    return pl.pallas_call(
        _attn_kernel,
        grid=(batch, MLA_HEADS // _HEADS_PER_STEP, nq),
        in_specs=[pl.BlockSpec((tq, hw), lambda b, j, i: (b * nq + i, j)),
                  pl.BlockSpec((seq, hw), lambda b, j, i: (b, j)),
                  pl.BlockSpec((seq, hw), lambda b, j, i: (b, j))],
        out_specs=pl.BlockSpec((tq, ow), lambda b, j, i: (b * nq + i, j)),
        out_shape=jax.ShapeDtypeStruct((t, MLA_HEADS * V_HEAD_DIM), BF16),
        scratch_shapes=[pltpu.VMEM((_HEADS_PER_STEP, nq, tq, tq), F32),
                        pltpu.VMEM((_HEADS_PER_STEP, tq, LANES), F32),
                        pltpu.VMEM((_HEADS_PER_STEP, tq, HEAD_PAD), F32)],
        compiler_params=_params(("arbitrary", "arbitrary", "arbitrary")),
        name="attn",
    )(q, k, v)


_GROUP_SIZE = N_EXPERTS // N_EXPERT_GROUPS
_BIG = 1.0e9


def _mix_out_kernel(attn_ref, yc_ref, x_ref, mod_ref, gattn_ref, wout_ref, gpost_ref, gpre2_ref,
                    wrt_ref, br_ref, x1_ref, h2_ref, h2p_ref, idx_ref, wts_ref, rank_ref, cnt_ref,
                    carry_scr):
    i = pl.program_id(0)
    tm = x_ref.shape[0]
    half = attn_ref.shape[1]

    @pl.when(i == 0)
    def _():
        carry_scr[...] = jnp.zeros(carry_scr.shape, F32)

    an = _rms(attn_ref[...].astype(F32), gattn_ref[...]).astype(BF16)
    mix = (jnp.dot(an, wout_ref[0:half, :], preferred_element_type=F32)
           + jnp.dot(yc_ref[...], wout_ref[half:, :], preferred_element_type=F32))
    g1 = mod_ref[0, 2:3, :]
    sh2 = mod_ref[0, 3:4, :]
    sc2 = mod_ref[0, 4:5, :]
    x1 = x_ref[...] + g1 * _rms(mix, gpost_ref[...])
    x1_ref[...] = x1
    h2 = _rms(x1, gpre2_ref[...]) * (1.0 + sc2) + sh2
    h2_ref[...] = h2.astype(BF16)
    words = _pack_row_words(h2[:, 0:_ROW_WORDS], h2[:, _ROW_WORDS:])
    h2p_ref[...] = words

    logits = lax.dot_general(wrt_ref[...], h2, (((1,), (1,)), ((), ())),
                             preferred_element_type=F32, precision=lax.Precision.HIGHEST)
    scores = jax.nn.sigmoid(logits)
    sel = scores + br_ref[...]
    row = lax.broadcasted_iota(I32, (N_EXPERTS, tm), 0).astype(F32)

    gscore = []
    rw = lax.broadcasted_iota(I32, (_GROUP_SIZE, tm), 0).astype(F32)
    for g in range(N_EXPERT_GROUPS):
        blk = sel[g * _GROUP_SIZE:(g + 1) * _GROUP_SIZE, :]
        m1 = jnp.max(blk, axis=0, keepdims=True)
        i1 = jnp.min(jnp.where(blk == m1, rw, _BIG), axis=0, keepdims=True)
        m2 = jnp.max(jnp.where(rw == i1, NEG_INF, blk), axis=0, keepdims=True)
        gscore.append(m1 + m2)

    gkeep = [jnp.zeros((1, tm), F32) for _ in range(N_EXPERT_GROUPS)]
    for _ in range(TOPK_GROUPS):
        mg = functools.reduce(jnp.maximum, gscore)
        ig = functools.reduce(jnp.minimum, [jnp.where(gscore[g] == mg, float(g), _BIG)
                                            for g in range(N_EXPERT_GROUPS)])
        for g in range(N_EXPERT_GROUPS):
            hit = ig == float(g)
            gkeep[g] = jnp.where(hit, 1.0, gkeep[g])
            gscore[g] = jnp.where(hit, NEG_INF, gscore[g])
    cur = jnp.concatenate(
        [jnp.where(gkeep[g] > 0.0, sel[g * _GROUP_SIZE:(g + 1) * _GROUP_SIZE, :], NEG_INF)
         for g in range(N_EXPERT_GROUPS)], axis=0)

    krow = lax.broadcasted_iota(I32, (TOP_K, tm), 0)
    idx_rows = []
    idx_f = jnp.zeros((TOP_K, tm), F32)
    sc_k = jnp.zeros((TOP_K, tm), F32)
    sc_sum = jnp.zeros((1, tm), F32)
    onehot = jnp.zeros((N_EXPERTS, tm), F32)
    for k in range(TOP_K):
        m = jnp.max(cur, axis=0, keepdims=True)
        ik = jnp.min(jnp.where(cur == m, row, _BIG), axis=0, keepdims=True)
        hit = row == ik
        sk = jnp.sum(jnp.where(hit, scores, 0.0), axis=0, keepdims=True)
        cur = jnp.where(hit, NEG_INF, cur)
        onehot = jnp.where(hit, 1.0, onehot)
        idx_rows.append(ik)
        idx_f = jnp.where(krow == k, ik, idx_f)
        sc_k = jnp.where(krow == k, sk, sc_k)
        sc_sum = sc_sum + sk
    wts_ref[...] = sc_k / sc_sum * ROUTED_SCALE
    idx_ref[...] = idx_f.astype(I32)

    tri = (lax.broadcasted_iota(I32, (tm, tm), 0) < lax.broadcasted_iota(I32, (tm, tm), 1))
    excl = jnp.dot(onehot.astype(BF16), tri.astype(BF16), preferred_element_type=F32)
    rank_e = carry_scr[:, 0:1] + excl
    rank_k = jnp.zeros((TOP_K, tm), F32)
    for k in range(TOP_K):
        hit = row == idx_rows[k]
        rk = jnp.sum(jnp.where(hit, rank_e, 0.0), axis=0, keepdims=True)
        rank_k = jnp.where(krow == k, rk, rank_k)
    rank_ref[...] = rank_k.astype(I32)
    carry_scr[...] = carry_scr[...] + jnp.sum(onehot, axis=1, keepdims=True)
    cnt_ref[...] = carry_scr[...]


def _mix_out(attn, yc, x2, mod3, gattn, wout, gpost, gpre2, wrt, br, seq):
    t, d = x2.shape
    tm = min(TM_OUT, seq)
    tpb = seq // tm
    full = lambda a: pl.BlockSpec(a.shape, lambda i: (0,) * a.ndim)
    row = lambda w: pl.BlockSpec((tm, w), lambda i: (i, 0))
    col = pl.BlockSpec((TOP_K, tm), lambda i: (0, i))
    return pl.pallas_call(
        _mix_out_kernel,
        grid=(t // tm,),
        in_specs=[row(attn.shape[1]), row(yc.shape[1]), row(d),
                  pl.BlockSpec((1, 6, d), lambda i: (i // tpb, 0, 0)),
                  full(gattn), full(wout), full(gpost), full(gpre2), full(wrt), full(br)],
        out_specs=[row(d), row(d), row(_ROW_WORDS), col, col, col,
                   pl.BlockSpec((N_EXPERTS, LANES), lambda i: (0, 0))],
        out_shape=[jax.ShapeDtypeStruct((t, d), F32), jax.ShapeDtypeStruct((t, d), BF16),
                   jax.ShapeDtypeStruct((t, _ROW_WORDS), U32),
                   jax.ShapeDtypeStruct((TOP_K, t), I32), jax.ShapeDtypeStruct((TOP_K, t), F32),
                   jax.ShapeDtypeStruct((TOP_K, t), I32),
                   jax.ShapeDtypeStruct((N_EXPERTS, LANES), F32)],
        scratch_shapes=[pltpu.VMEM((N_EXPERTS, LANES), F32)],
        compiler_params=_params(("arbitrary",)),
        name="mix_out",
    )(attn, yc, x2, mod3, gattn, wout, gpost, gpre2, wrt, br)


def _dest_kernel(idx_ref, rank_ref, pstart_ref, dest_ref):
    tm = idx_ref.shape[1]
    row = lax.broadcasted_iota(I32, (N_EXPERTS, tm), 0)
    krow = lax.broadcasted_iota(I32, (TOP_K, tm), 0)
    pstart = pstart_ref[...]
    idx = idx_ref[...]
    out = jnp.zeros((TOP_K, tm), F32)
    for k in range(TOP_K):
        hit = row == idx[k:k + 1, :]
        base = jnp.sum(jnp.where(hit, pstart, 0.0), axis=0, keepdims=True)
        out = jnp.where(krow == k, base, out)
    dest_ref[...] = out.astype(I32) + rank_ref[...]


def _dest(idx, rank, pstart):
    t = idx.shape[1]
    tm = min(TM_DEST, t)
    col = pl.BlockSpec((TOP_K, tm), lambda i: (0, i))
    return pl.pallas_call(
        _dest_kernel,
        grid=(t // tm,),
        in_specs=[col, col, pl.BlockSpec((N_EXPERTS, 1), lambda i: (0, 0))],
        out_specs=col,
        out_shape=jax.ShapeDtypeStruct((TOP_K, t), I32),
        compiler_params=_params(("arbitrary",)),
        name="dest",
    )(idx, rank, pstart)


_PAD_CHUNKS = tuple(BM_EXPERT >> s for s in range(1, BM_EXPERT.bit_length()))


def _dispatch_kernel(pad_from_ref, pad_n_ref, dest_ref, h2_ref, xs_hbm, zero_scr, sem, pad_sem):
    i = pl.program_id(0)
    tm = dest_ref.shape[1]

    @pl.when(i == 0)
    def _():
        zero_scr[...] = jnp.zeros(zero_scr.shape, zero_scr.dtype)

        def pad_copies(e, act):
            n = pad_n_ref[e]
            base = pad_from_ref[e]

            def single_rows(start, count):
                for j in range(SUBLANES - 1):
                    @pl.when(j < count)
                    def _():
                        act(pltpu.make_async_copy(zero_scr.at[pl.ds(0, 1), :],
                                                  xs_hbm.at[pl.ds(start + j, 1), :], pad_sem))

            head = jnp.minimum(n, (SUBLANES - (base & (SUBLANES - 1))) & (SUBLANES - 1))
            single_rows(base, head)
            rest = n - head
            mid = base + head
            for rows in _PAD_CHUNKS:
                if rows >= SUBLANES:
                    @pl.when((rest & rows) != 0)
                    def _():
                        start = pl.multiple_of(mid + (rest & ~(2 * rows - 1)), SUBLANES)
                        act(pltpu.make_async_copy(zero_scr.at[pl.ds(0, rows), :],
                                                  xs_hbm.at[pl.ds(start, rows), :], pad_sem))
            single_rows(mid + (rest & ~(SUBLANES - 1)), rest & (SUBLANES - 1))

        def issue_pad(e, carry):
            pad_copies(e, lambda cp: cp.start())
            return carry

        def drain_pad(e, carry):
            pad_copies(e, lambda cp: cp.wait())
            return carry

        lax.fori_loop(0, N_EXPERTS, issue_pad, 0)
        lax.fori_loop(0, N_EXPERTS, drain_pad, 0)

    def row_copy(t, k):
        return pltpu.make_async_copy(h2_ref.at[pl.ds(t, 1), :], xs_hbm.at[pl.ds(dest_ref[k, t], 1), :], sem)

    def issue(t, carry):
        for k in range(TOP_K):
            row_copy(t, k).start(priority=k % 2)
        return carry

    lax.fori_loop(0, tm, issue, 0)

    def drain(t, carry):
        for k in range(TOP_K):
            row_copy(t, k).wait()
        return carry

    lax.fori_loop(0, tm, drain, 0)


def _dispatch(pad_from, pad_n, dest, h2_rows, n_rows):
    t = h2_rows.shape[0]
    tm = min(TM_DISPATCH, t)
    return pl.pallas_call(
        _dispatch_kernel,
        grid_spec=pltpu.PrefetchScalarGridSpec(
            num_scalar_prefetch=2,
            grid=(t // tm,),
            in_specs=[pl.BlockSpec((TOP_K, tm), lambda i, pf, pn: (0, i), memory_space=pltpu.SMEM),
                      pl.BlockSpec((tm, _ROW_WORDS), lambda i, pf, pn: (i, 0))],
            out_specs=pl.BlockSpec(memory_space=pl.ANY),
            scratch_shapes=[pltpu.VMEM((BM_EXPERT // 2, _ROW_WORDS), U32),
                            pltpu.SemaphoreType.DMA, pltpu.SemaphoreType.DMA]),
        out_shape=jax.ShapeDtypeStruct((n_rows, _ROW_WORDS), U32),
        compiler_params=_params(("arbitrary",)),
        name="dispatch",
    )(pad_from, pad_n, dest, h2_rows)


_XS_SLOTS = 4
_YS_SLOTS = 2
_W_SLOTS = 2


def _expert_kernel(first_ref, ord_ref, uexp_ref, meta_ref, xs_hbm, wg_hbm, wu_hbm, wd_hbm, ys_hbm,
                   xs_buf, ys_buf, wg_buf, wu_buf, wd_buf, wgu_scr, wd_scr, x_scr, xs_sem, ys_sem, w_sem):
    i = pl.program_id(0)
    nused = meta_ref[0]
    nexp = meta_ref[1]
    bm = xs_buf.shape[1]

    def xs_copy(b, slot):
        return pltpu.make_async_copy(xs_hbm.at[pl.ds(b * bm, bm), :], xs_buf.at[slot], xs_sem.at[slot])

    def ys_copy(b, slot):
        return pltpu.make_async_copy(ys_buf.at[slot], ys_hbm.at[pl.ds(b * bm, bm), :], ys_sem.at[slot])

    def w_copies(j, slot):
        e = uexp_ref[j]
        return (pltpu.make_async_copy(wg_hbm.at[e], wg_buf.at[slot], w_sem.at[slot, 0]),
                pltpu.make_async_copy(wu_hbm.at[e], wu_buf.at[slot], w_sem.at[slot, 1]),
                pltpu.make_async_copy(wd_hbm.at[e], wd_buf.at[slot], w_sem.at[slot, 2]))

    @pl.when(i == 0)
    def _():
        for s in range(_XS_SLOTS - 1):
            @pl.when(s < nused)
            def _():
                xs_copy(s, s).start()
        for s in range(_W_SLOTS):
            @pl.when(s < nexp)
            def _():
                for cp in w_copies(s, s):
                    cp.start()

    @pl.when(i < nused)
    def _():
        ahead = i + _XS_SLOTS - 1

        @pl.when(ahead < nused)
        def _():
            xs_copy(ahead, ahead % _XS_SLOTS).start()

        @pl.when(first_ref[i] == 1)
        def _():
            j = ord_ref[i]
            ws = j % _W_SLOTS
            for cp in w_copies(j, ws):
                cp.wait()
            wgu_scr[:, 0:EXPERT_DIM] = wg_buf[ws].astype(BF16)
            wgu_scr[:, EXPERT_DIM:2 * EXPERT_DIM] = wu_buf[ws].astype(BF16)
            wd_scr[...] = wd_buf[ws].astype(BF16)

            @pl.when(j + _W_SLOTS < nexp)
            def _():
                for cp in w_copies(j + _W_SLOTS, ws):
                    cp.start()

        slot = i % _XS_SLOTS
        xs_copy(i, slot).wait()
        lo, hi = _unpack_row_words(xs_buf[slot])
        x_scr[:, 0:_ROW_WORDS] = lo.astype(BF16)
        x_scr[:, _ROW_WORDS:] = hi.astype(BF16)
        gu = jnp.dot(x_scr[...], wgu_scr[...], preferred_element_type=F32)
        g = gu[:, 0:EXPERT_DIM]
        a = (g * jax.nn.sigmoid(g) * gu[:, EXPERT_DIM:2 * EXPERT_DIM]).astype(BF16)
        y = jnp.dot(a, wd_scr[...], preferred_element_type=F32)
        oslot = i % _YS_SLOTS

        @pl.when(i >= _YS_SLOTS)
        def _():
            ys_copy(i - _YS_SLOTS, oslot).wait()

        ys_buf[oslot] = _pack_row_words(y[:, 0:_ROW_WORDS], y[:, _ROW_WORDS:])
        ys_copy(i, oslot).start()

        @pl.when(i == nused - 1)
        def _():
            ys_copy(i, oslot).wait()

            @pl.when(i >= 1)
            def _():
                ys_copy(i - 1, (i - 1) % _YS_SLOTS).wait()


def _experts(first, ordinal, uexp, meta, xs, w_gate, w_up, w_down):
    p = xs.shape[0]
    d = w_gate.shape[1]
    nb = p // BM_EXPERT
    anyspec = pl.BlockSpec(memory_space=pl.ANY)
    return pl.pallas_call(
        _expert_kernel,
        grid_spec=pltpu.PrefetchScalarGridSpec(
            num_scalar_prefetch=4,
            grid=(nb,),
            in_specs=[anyspec, anyspec, anyspec, anyspec],
            out_specs=anyspec,
            scratch_shapes=[pltpu.VMEM((_XS_SLOTS, BM_EXPERT, _ROW_WORDS), U32),
                            pltpu.VMEM((_YS_SLOTS, BM_EXPERT, _ROW_WORDS), U32),
                            pltpu.VMEM((_W_SLOTS, d, EXPERT_DIM), F32),
                            pltpu.VMEM((_W_SLOTS, d, EXPERT_DIM), F32),
                            pltpu.VMEM((_W_SLOTS, EXPERT_DIM, d), F32),
                            pltpu.VMEM((d, 2 * EXPERT_DIM), BF16), pltpu.VMEM((EXPERT_DIM, d), BF16),
                            pltpu.VMEM((BM_EXPERT, d), BF16),
                            pltpu.SemaphoreType.DMA((_XS_SLOTS,)), pltpu.SemaphoreType.DMA((_YS_SLOTS,)),
                            pltpu.SemaphoreType.DMA((_W_SLOTS, 3))]),
        out_shape=jax.ShapeDtypeStruct((p, _ROW_WORDS), U32),
        compiler_params=_params(("arbitrary",)),
        name="experts",
    )(first, ordinal, uexp, meta, xs, w_gate, w_up, w_down)


def _combine_kernel(dest_ref, wts_ref, h2_ref, x1_ref, mod_ref, wsg_ref, wsu_ref, wsd_ref, gpost_ref,
                    ys_hbm, o_ref, yg_scr, sem):
    tm = x1_ref.shape[0]

    def row_copy(t, k):
        return pltpu.make_async_copy(ys_hbm.at[pl.ds(dest_ref[k, t], 1), :], yg_scr.at[k, pl.ds(t, 1), :], sem)

    def issue(t, carry):
        for k in range(TOP_K):
            row_copy(t, k).start(priority=k % 2)
        return carry

    lax.fori_loop(0, tm, issue, 0)

    h2 = h2_ref[...]
    g = jnp.dot(h2, wsg_ref[...], preferred_element_type=F32)
    u = jnp.dot(h2, wsu_ref[...], preferred_element_type=F32)
    f = jnp.dot((g * jax.nn.sigmoid(g) * u).astype(BF16), wsd_ref[...], preferred_element_type=F32)

    def drain(t, carry):
        for k in range(TOP_K):
            row_copy(t, k).wait()
        return carry

    lax.fori_loop(0, tm, drain, 0)

    wts = wts_ref[...]
    los = [f[:, sl * LANES:(sl + 1) * LANES] for sl in range(_ROW_SLABS)]
    his = [f[:, _ROW_WORDS + sl * LANES:_ROW_WORDS + (sl + 1) * LANES] for sl in range(_ROW_SLABS)]
    for k in range(TOP_K):
        wk = wts[:, k:k + 1]
        for sl in range(_ROW_SLABS):
            lo, hi = _unpack_row_words(yg_scr[k, :, sl * LANES:(sl + 1) * LANES])
            los[sl] = los[sl] + wk * lo
            his[sl] = his[sl] + wk * hi
    f = jnp.concatenate(los + his, axis=1)
    g2 = mod_ref[0, 5:6, :]
    o_ref[...] = x1_ref[...] + g2 * _rms(f, gpost_ref[...])


def _combine(dest, wts_t, h2, x1, mod3, wsg, wsu, wsd, gpost, ys_rows, seq):
    t, d = x1.shape
    tm = min(TM_COMBINE, seq)
    tpb = seq // tm
    full = lambda a: pl.BlockSpec(a.shape, lambda i: (0,) * a.ndim)
    row = lambda w: pl.BlockSpec((tm, w), lambda i: (i, 0))
    return pl.pallas_call(
        _combine_kernel,
        grid=(t // tm,),
        in_specs=[pl.BlockSpec((TOP_K, tm), lambda i: (0, i), memory_space=pltpu.SMEM),
                  row(TOP_K), row(d), row(d),
                  pl.BlockSpec((1, 6, d), lambda i: (i // tpb, 0, 0)),
                  full(wsg), full(wsu), full(wsd), full(gpost),
                  pl.BlockSpec(memory_space=pl.ANY)],
        out_specs=row(d),
        out_shape=jax.ShapeDtypeStruct((t, d), F32),
        scratch_shapes=[pltpu.VMEM((TOP_K, tm, _ROW_WORDS), U32), pltpu.SemaphoreType.DMA],
        compiler_params=_params(("arbitrary",)),
        name="combine",
    )(dest, wts_t, h2, x1, mod3, wsg, wsu, wsd, gpost, ys_rows)


def _pack_weights(w_in, w_uq, w_ukv):
    d = w_in.shape[0]
    half = QK_ROPE_DIM // 2
    z = lambda n, c: jnp.zeros((n, c), F32)
    o = Q_LORA_RANK + KV_LORA_RANK
    kr = w_in[:, o:o + QK_ROPE_DIM]
    kr_grp = jnp.concatenate([z(d, QK_NOPE_DIM), kr, z(d, HEAD_PAD - QK_NOPE_DIM - QK_ROPE_DIM)], axis=1)
    kr_rot = jnp.concatenate([z(d, QK_NOPE_DIM), -kr[:, half:], kr[:, :half],
                              z(d, HEAD_PAD - QK_NOPE_DIM - QK_ROPE_DIM)], axis=1)
    win_p = jnp.concatenate([w_in[:, :o], kr_grp, kr_rot, w_in[:, o + QK_ROPE_DIM:]], axis=1)

    scale = float(QK_NOPE_DIM + QK_ROPE_DIM) ** -0.5
    r = Q_LORA_RANK
    qd = QK_NOPE_DIM + QK_ROPE_DIM
    q_grp, q_rot = [], []
    for h in range(MLA_HEADS):
        nope = w_uq[:, h * qd:h * qd + QK_NOPE_DIM]
        rope = w_uq[:, h * qd + QK_NOPE_DIM:(h + 1) * qd]
        pad = z(r, HEAD_PAD - qd)
        q_grp.append(jnp.concatenate([nope, rope, pad], axis=1))
        q_rot.append(jnp.concatenate([z(r, QK_NOPE_DIM), -rope[:, half:], rope[:, :half], pad], axis=1))
    wuq_p = jnp.concatenate(q_grp + q_rot, axis=1) * scale

    c = KV_LORA_RANK
    kd = QK_NOPE_DIM + V_HEAD_DIM
    k_grp, v_grp = [], []
    for h in range(MLA_HEADS):
        k_grp.append(jnp.concatenate([w_ukv[:, h * kd:h * kd + QK_NOPE_DIM], z(c, HEAD_PAD - QK_NOPE_DIM)], axis=1))
        v_grp.append(jnp.concatenate([w_ukv[:, h * kd + QK_NOPE_DIM:(h + 1) * kd], z(c, HEAD_PAD - V_HEAD_DIM)], axis=1))
    wukv_p = jnp.concatenate(k_grp + v_grp, axis=1)
    return win_p.astype(BF16), wuq_p.astype(BF16), wukv_p.astype(BF16)


def _rope_tables(positions):
    inv = 1.0 / (ROPE_THETA ** (jnp.arange(0, QK_ROPE_DIM, 2, dtype=F32) / QK_ROPE_DIM))
    ang = positions.astype(F32).reshape(-1)[:, None] * inv
    t = ang.shape[0]
    cos, sin = jnp.cos(ang), jnp.sin(ang)
    tail = HEAD_PAD - QK_NOPE_DIM - QK_ROPE_DIM
    cos_t = jnp.concatenate([jnp.ones((t, QK_NOPE_DIM), F32), cos, cos, jnp.ones((t, tail), F32)], axis=1)
    sin_t = jnp.concatenate([jnp.zeros((t, QK_NOPE_DIM), F32), sin, sin, jnp.zeros((t, tail), F32)], axis=1)
    return cos_t, sin_t


def _layer(x2, c, cos_t, sin_t, batch, seq, w_ada, b_ada, g_pre_mix, w_in, g_q_lat, w_uq, g_kv_lat, w_ukv,
           w_conv, g_attn_out, g_conv_out, w_out, g_post_mix, g_pre_ffn, w_router, b_router,
           w_gate, w_up, w_down, w_sh_gate, w_sh_up, w_sh_down, g_post_ffn):
    t, d = x2.shape
    r1 = lambda a: a.reshape(1, -1)

    c_pad = jnp.zeros((SUBLANES, d), F32).at[:batch].set(c)
    mod = _ada(c_pad, w_ada, r1(b_ada))[:batch]
    mod3 = mod.reshape(batch, 6, d)

    win_p, wuq_p, wukv_p = _pack_weights(w_in, w_uq, w_ukv)
    vone = jnp.zeros((1, HEAD_PAD), F32).at[0, V_HEAD_DIM].set(1.0)
    q, k, v, yc = _mix_in(x2, mod3, r1(g_pre_mix), win_p, r1(g_q_lat), wuq_p, r1(g_kv_lat), wukv_p,
                          vone, w_conv, r1(g_conv_out), cos_t, sin_t, seq)
    attn = _attention(q, k, v, batch, seq)
    x1, h2, h2p, idx, wts, rank, cnt = _mix_out(
        attn, yc, x2, mod3, r1(g_attn_out), w_out.astype(BF16), r1(g_post_mix), r1(g_pre_ffn),
        w_router.T, b_router.reshape(-1, 1), seq)

    counts = cnt[:, 0].astype(I32)
    padded = ((counts + BM_EXPERT - 1) // BM_EXPERT) * BM_EXPERT
    pad_end = jnp.cumsum(padded)
    pad_start = pad_end - padded
    m = t * TOP_K
    nb = (m + N_EXPERTS * (BM_EXPERT - 1)) // BM_EXPERT
    nused = pad_end[-1] // BM_EXPERT
    bidx = jnp.arange(nb, dtype=I32)
    blk_exp = jnp.sum((pad_end[None, :] <= (bidx * BM_EXPERT)[:, None]).astype(I32), axis=1)
    first = ((bidx < nused) & ((bidx == 0) | (blk_exp != jnp.roll(blk_exp, 1)))).astype(I32)
    ordinal = jnp.maximum(jnp.cumsum(first) - 1, 0).astype(I32)
    seen = jnp.cumsum((counts > 0).astype(I32))
    uexp = jnp.minimum(jnp.sum((seen[None, :] <= jnp.arange(N_EXPERTS, dtype=I32)[:, None]).astype(I32), axis=1),
                       N_EXPERTS - 1).astype(I32)
    meta = jnp.stack([nused, seen[-1]]).astype(I32)

    dest = _dest(idx, rank, pad_start.astype(F32).reshape(-1, 1))
    xs = _dispatch((pad_start + counts).astype(I32), (padded - counts).astype(I32), dest, h2p, nb * BM_EXPERT)
    ys = _experts(first, ordinal, uexp, meta, xs, w_gate, w_up, w_down)
    return _combine(dest, wts.T, h2, x1, mod3, w_sh_gate.astype(BF16), w_sh_up.astype(BF16),
                    w_sh_down.astype(BF16), r1(g_post_ffn), ys, seq)


def kernel(x, c, positions, w_ada, b_ada, g_pre_mix, w_in, g_q_lat, w_uq, g_kv_lat, w_ukv, w_conv, g_attn_out, g_conv_out, w_out, g_post_mix, g_pre_ffn, w_router, b_router, w_gate, w_up, w_down, w_sh_gate, w_sh_up, w_sh_down, g_post_ffn):
    batch, seq, d = x.shape
    cos_t, sin_t = _rope_tables(positions)
    x2 = x.reshape(batch * seq, d)
    for l in range(w_ada.shape[0]):
        x2 = _layer(x2, c, cos_t, sin_t, batch, seq, w_ada[l], b_ada[l], g_pre_mix[l], w_in[l], g_q_lat[l],
                    w_uq[l], g_kv_lat[l], w_ukv[l], w_conv[l], g_attn_out[l], g_conv_out[l], w_out[l],
                    g_post_mix[l], g_pre_ffn[l], w_router[l], b_router[l], w_gate[l], w_up[l], w_down[l],
                    w_sh_gate[l], w_sh_up[l], w_sh_down[l], g_post_ffn[l])
    return x2.reshape(batch, seq, d)
```

```python
import functools

import jax
import jax.numpy as jnp
import numpy as np
from jax import lax
from jax.experimental import pallas as pl
from jax.experimental.pallas import tpu as pltpu

F32 = jnp.float32
BF16 = jnp.bfloat16
I32 = jnp.int32
U32 = jnp.uint32

CHUNK = 64
MLA_HEADS = 8
QK_NOPE_DIM = 64
QK_ROPE_DIM = 32
V_HEAD_DIM = 64
Q_LORA_RANK = 384
KV_LORA_RANK = 256
ROPE_THETA = 10000.0
CONV_WIDTH = 3
N_EXPERTS = 256
TOP_K = 8
N_EXPERT_GROUPS = 8
TOPK_GROUPS = 4
EXPERT_DIM = 256
ROUTED_SCALE = 2.5
EPS = 1e-6

LANES = 128
SUBLANES = 8
HEAD_PAD = LANES
VMEM_LIMIT_BYTES = 56 * 1024 * 1024

TM_IN = 512
TQ_ATTN = 512
TM_OUT = 512
TM_DEST = 512
TM_DISPATCH = 512
BM_EXPERT = 256
TM_COMBINE = 256

NEG_INF = float("-inf")


def _rms(x, g):
    return x * lax.rsqrt(jnp.mean(x * x, axis=-1, keepdims=True) + EPS) * g


_HI_MASK = np.uint32(0xFFFF0000)
_ROW_WORDS = 512
_ROW_SLABS = _ROW_WORDS // LANES


def _pack_row_words(lo, hi):
    lo_w = lax.bitcast_convert_type(lo.astype(BF16).astype(F32), U32) >> 16
    hi_w = lax.bitcast_convert_type(hi.astype(BF16).astype(F32), U32) & _HI_MASK
    return lo_w | hi_w


def _unpack_row_words(w):
    return (lax.bitcast_convert_type(w << 16, F32), lax.bitcast_convert_type(w & _HI_MASK, F32))


def _params(sem):
    return pltpu.CompilerParams(dimension_semantics=sem, vmem_limit_bytes=VMEM_LIMIT_BYTES)


def _ada_kernel(c_ref, w_ref, b_ref, o_ref):
    c = c_ref[...]
    s = c * jax.nn.sigmoid(c)
    o_ref[...] = jnp.dot(s, w_ref[...], preferred_element_type=F32,
                         precision=lax.Precision.HIGHEST) + b_ref[...]


def _ada(c_pad, w, b):
    rows, d = c_pad.shape
    n = w.shape[1]
    tn = 1536
    return pl.pallas_call(
        _ada_kernel,
        grid=(n // tn,),
        in_specs=[pl.BlockSpec((rows, d), lambda j: (0, 0)),
                  pl.BlockSpec((d, tn), lambda j: (0, j)),
                  pl.BlockSpec((1, tn), lambda j: (0, j))],
        out_specs=pl.BlockSpec((rows, tn), lambda j: (0, j)),
        out_shape=jax.ShapeDtypeStruct((rows, n), F32),
        compiler_params=_params(("arbitrary",)),
        name="ada",
    )(c_pad, w, b)


_CQ0, _CQ1 = 0, Q_LORA_RANK
_CKV0, _CKV1 = _CQ1, _CQ1 + KV_LORA_RANK
_KR0, _KR1 = _CKV1, _CKV1 + 2 * HEAD_PAD
_CONV_DIM = 512
_GB0 = _KR1
_GC0 = _GB0 + _CONV_DIM
_XV0 = _GC0 + _CONV_DIM
_WIN_COLS = _XV0 + _CONV_DIM
_QW = MLA_HEADS * HEAD_PAD


def _mix_in_kernel(tiles_per_batch, x_ref, mod_ref, gpre_ref, win_ref, gq_ref, wuq_ref, gkv_ref,
                   wukv_ref, vone_ref, wconv_ref, gconv_ref, cos_ref, sin_ref,
                   q_ref, k_ref, v_ref, yc_ref, h_scr, u_scr):
    i = pl.program_id(0)
    tm = x_ref.shape[0]
    sh1 = mod_ref[0, 0:1, :]
    sc1 = mod_ref[0, 1:2, :]
    h = _rms(x_ref[...], gpre_ref[...]) * (1.0 + sc1) + sh1
    h_scr[...] = h.astype(BF16)
    cos = cos_ref[...]
    sin = sin_ref[...]

    cq = jnp.dot(h_scr[...], win_ref[:, _CQ0:_CQ1], preferred_element_type=F32)
    cqn = _rms(cq, gq_ref[...]).astype(BF16)
    qq = jnp.dot(cqn, wuq_ref[...], preferred_element_type=F32)
    for hd in range(MLA_HEADS):
        lo = hd * HEAD_PAD
        qh = qq[:, lo:lo + HEAD_PAD] * cos + qq[:, _QW + lo:_QW + lo + HEAD_PAD] * sin
        q_ref[:, lo:lo + HEAD_PAD] = qh.astype(BF16)

    ckv = jnp.dot(h_scr[...], win_ref[:, _CKV0:_CKV1], preferred_element_type=F32)
    ckvn = _rms(ckv, gkv_ref[...]).astype(BF16)
    kv = jnp.dot(ckvn, wukv_ref[...], preferred_element_type=F32)
    krr = jnp.dot(h_scr[...], win_ref[:, _KR0:_KR1], preferred_element_type=F32)
    kr = krr[:, 0:HEAD_PAD] * cos + krr[:, HEAD_PAD:2 * HEAD_PAD] * sin
    vone = vone_ref[...]
    for hd in range(MLA_HEADS):
        lo = hd * HEAD_PAD
        k_ref[:, lo:lo + HEAD_PAD] = (kv[:, lo:lo + HEAD_PAD] + kr).astype(BF16)
        v_ref[:, lo:lo + HEAD_PAD] = (kv[:, _QW + lo:_QW + lo + HEAD_PAD] + vone).astype(BF16)

    gb = jnp.dot(h_scr[...], win_ref[:, _GB0:_GC0], preferred_element_type=F32)
    gc = jnp.dot(h_scr[...], win_ref[:, _GC0:_XV0], preferred_element_type=F32)
    xv = jnp.dot(h_scr[...], win_ref[:, _XV0:_WIN_COLS], preferred_element_type=F32)
    u = gc * xv
    prev = u_scr[tm:tm + SUBLANES, :]
    first = (i % tiles_per_batch) == 0
    u_scr[0:SUBLANES, :] = jnp.where(first, jnp.zeros_like(prev), prev)
    u_scr[SUBLANES:tm + SUBLANES, :] = u
    um1 = u_scr[SUBLANES - 1:tm + SUBLANES - 1, :]
    um2 = u_scr[SUBLANES - 2:tm + SUBLANES - 2, :]
    conv = wconv_ref[0:1, :] * um2 + wconv_ref[1:2, :] * um1 + wconv_ref[2:3, :] * u
    yc_ref[...] = _rms(gb * conv, gconv_ref[...]).astype(BF16)


def _mix_in(x2, mod3, gpre, win_p, gq, wuq_p, gkv, wukv_p, vone, wconv, gconv, cos_t, sin_t, seq):
    t, d = x2.shape
    tm = min(TM_IN, seq)
    tpb = seq // tm
    full = lambda a: pl.BlockSpec(a.shape, lambda i: (0,) * a.ndim)
    row = lambda w: pl.BlockSpec((tm, w), lambda i: (i, 0))
    return pl.pallas_call(
        functools.partial(_mix_in_kernel, tpb),
        grid=(t // tm,),
        in_specs=[row(d),
                  pl.BlockSpec((1, 6, d), lambda i: (i // tpb, 0, 0)),
                  full(gpre), full(win_p), full(gq), full(wuq_p), full(gkv), full(wukv_p),
                  full(vone), full(wconv), full(gconv), row(HEAD_PAD), row(HEAD_PAD)],
        out_specs=[row(_QW), row(_QW), row(_QW), row(_CONV_DIM)],
        out_shape=[jax.ShapeDtypeStruct((t, _QW), BF16), jax.ShapeDtypeStruct((t, _QW), BF16),
                   jax.ShapeDtypeStruct((t, _QW), BF16), jax.ShapeDtypeStruct((t, _CONV_DIM), BF16)],
        scratch_shapes=[pltpu.VMEM((tm, d), BF16), pltpu.VMEM((tm + SUBLANES, _CONV_DIM), F32)],
        compiler_params=_params(("arbitrary",)),
        name="mix_in",
    )(x2, mod3, gpre, win_p, gq, wuq_p, gkv, wukv_p, vone, wconv, gconv, cos_t, sin_t)


_HEADS_PER_STEP = 2


def _attn_kernel(q_ref, k_ref, v_ref, o_ref, s_scr, m_scr, acc_scr):
    qi = pl.program_id(2)
    tq = q_ref.shape[0]
    tk = tq
    lane_groups = tk // LANES

    def tile_max(s):
        m = s[:, 0:LANES]
        for g in range(1, lane_groups):
            m = jnp.maximum(m, s[:, g * LANES:(g + 1) * LANES])
        return m

    heads = range(_HEADS_PER_STEP)
    lanes = [slice(hh * HEAD_PAD, (hh + 1) * HEAD_PAD) for hh in heads]

    def scores(hh, kv):
        off = pl.multiple_of(kv * tk, tk)
        return lax.dot_general(q_ref[:, lanes[hh]], k_ref[pl.ds(off, tk), lanes[hh]],
                               (((1,), (1,)), ((), ())), preferred_element_type=F32)

    m_scr[...] = jnp.full(m_scr.shape, NEG_INF, F32)

    def pass1(kv, carry):
        for hh in heads:
            s = scores(hh, kv)
            s_scr[hh, kv] = s
            m_scr[hh] = jnp.maximum(m_scr[hh], tile_max(s))
        return carry

    lax.fori_loop(0, qi, pass1, 0)
    rc = lax.broadcasted_iota(I32, (tq, tk), 0) // CHUNK
    cc = lax.broadcasted_iota(I32, (tq, tk), 1) // CHUNK
    for hh in heads:
        s = jnp.where(cc <= rc, scores(hh, qi), NEG_INF)
        s_scr[hh, qi] = s
        m_row = jnp.max(jnp.maximum(m_scr[hh], tile_max(s)), axis=1, keepdims=True)
        m_scr[hh] = jnp.broadcast_to(m_row, (tq, LANES))

    acc_scr[...] = jnp.zeros(acc_scr.shape, F32)

    def pass2(kv, carry):
        off = pl.multiple_of(kv * tk, tk)
        for hh in heads:
            mb = m_scr[hh]
            p = jnp.concatenate(
                [jnp.exp2(s_scr[hh, kv, :, g * LANES:(g + 1) * LANES] - mb) for g in range(lane_groups)],
                axis=1).astype(BF16)
            acc_scr[hh] += jnp.dot(p, v_ref[pl.ds(off, tk), lanes[hh]], preferred_element_type=F32)
        return carry

    lax.fori_loop(0, qi + 1, pass2, 0)
    for hh in heads:
        acc = acc_scr[hh]
        o = acc[:, 0:V_HEAD_DIM] / acc[:, V_HEAD_DIM:V_HEAD_DIM + 1]
        o_ref[:, hh * V_HEAD_DIM:(hh + 1) * V_HEAD_DIM] = o.astype(BF16)


def _attention(q, k, v, batch, seq):
    t = q.shape[0]
    tq = min(TQ_ATTN, seq)
    nq = seq // tq
    hw = _HEADS_PER_STEP * HEAD_PAD
    ow = _HEADS_PER_STEP * V_HEAD_DIM
    return pl.pallas_call(
        _attn_kernel,
        grid=(batch, MLA_HEADS // _HEADS_PER_STEP, nq),
        in_specs=[pl.BlockSpec((tq, hw), lambda b, j, i: (b * nq + i, j)),
                  pl.BlockSpec((seq, hw), lambda b, j, i: (b, j)),
                  pl.BlockSpec((seq, hw), lambda b, j, i: (b, j))],
        out_specs=pl.BlockSpec((tq, ow), lambda b, j, i: (b * nq + i, j)),
        out_shape=jax.ShapeDtypeStruct((t, MLA_HEADS * V_HEAD_DIM), BF16),
        scratch_shapes=[pltpu.VMEM((_HEADS_PER_STEP, nq, tq, tq), F32),
                        pltpu.VMEM((_HEADS_PER_STEP, tq, LANES), F32),
                        pltpu.VMEM((_HEADS_PER_STEP, tq, HEAD_PAD), F32)],
        compiler_params=_params(("arbitrary", "arbitrary", "arbitrary")),
        name="attn",
    )(q, k, v)


_GROUP_SIZE = N_EXPERTS // N_EXPERT_GROUPS
_BIG = 1.0e9


def _mix_out_kernel(attn_ref, yc_ref, x_ref, mod_ref, gattn_ref, wout_ref, gpost_ref, gpre2_ref,
                    wrt_ref, br_ref, x1_ref, h2_ref, h2p_ref, idx_ref, wts_ref, rank_ref, cnt_ref,
                    carry_scr):
    i = pl.program_id(0)
    tm = x_ref.shape[0]
    half = attn_ref.shape[1]

    @pl.when(i == 0)
    def _():
        carry_scr[...] = jnp.zeros(carry_scr.shape, F32)

    an = _rms(attn_ref[...].astype(F32), gattn_ref[...]).astype(BF16)
    mix = (jnp.dot(an, wout_ref[0:half, :], preferred_element_type=F32)
           + jnp.dot(yc_ref[...], wout_ref[half:, :], preferred_element_type=F32))
    g1 = mod_ref[0, 2:3, :]
    sh2 = mod_ref[0, 3:4, :]
    sc2 = mod_ref[0, 4:5, :]
    x1 = x_ref[...] + g1 * _rms(mix, gpost_ref[...])
    x1_ref[...] = x1
    h2 = _rms(x1, gpre2_ref[...]) * (1.0 + sc2) + sh2
    h2_ref[...] = h2.astype(BF16)
    words = _pack_row_words(h2[:, 0:_ROW_WORDS], h2[:, _ROW_WORDS:])
    h2p_ref[...] = words

    logits = lax.dot_general(wrt_ref[...], h2, (((1,), (1,)), ((), ())),
                             preferred_element_type=F32, precision=lax.Precision.HIGHEST)
    scores = jax.nn.sigmoid(logits)
    sel = scores + br_ref[...]
    row = lax.broadcasted_iota(I32, (N_EXPERTS, tm), 0).astype(F32)

    gscore = []
    rw = lax.broadcasted_iota(I32, (_GROUP_SIZE, tm), 0).astype(F32)
    for g in range(N_EXPERT_GROUPS):
        blk = sel[g * _GROUP_SIZE:(g + 1) * _GROUP_SIZE, :]
        m1 = jnp.max(blk, axis=0, keepdims=True)
        i1 = jnp.min(jnp.where(blk == m1, rw, _BIG), axis=0, keepdims=True)
        m2 = jnp.max(jnp.where(rw == i1, NEG_INF, blk), axis=0, keepdims=True)
        gscore.append(m1 + m2)

    gkeep = [jnp.zeros((1, tm), F32) for _ in range(N_EXPERT_GROUPS)]
    for _ in range(TOPK_GROUPS):
        mg = functools.reduce(jnp.maximum, gscore)
        ig = functools.reduce(jnp.minimum, [jnp.where(gscore[g] == mg, float(g), _BIG)
                                            for g in range(N_EXPERT_GROUPS)])
        for g in range(N_EXPERT_GROUPS):
            hit = ig == float(g)
            gkeep[g] = jnp.where(hit, 1.0, gkeep[g])
            gscore[g] = jnp.where(hit, NEG_INF, gscore[g])
    cur = jnp.concatenate(
        [jnp.where(gkeep[g] > 0.0, sel[g * _GROUP_SIZE:(g + 1) * _GROUP_SIZE, :], NEG_INF)
         for g in range(N_EXPERT_GROUPS)], axis=0)

    krow = lax.broadcasted_iota(I32, (TOP_K, tm), 0)
    idx_rows = []
    idx_f = jnp.zeros((TOP_K, tm), F32)
    sc_k = jnp.zeros((TOP_K, tm), F32)
    sc_sum = jnp.zeros((1, tm), F32)
    onehot = jnp.zeros((N_EXPERTS, tm), F32)
    for k in range(TOP_K):
        m = jnp.max(cur, axis=0, keepdims=True)
        ik = jnp.min(jnp.where(cur == m, row, _BIG), axis=0, keepdims=True)
        hit = row == ik
        sk = jnp.sum(jnp.where(hit, scores, 0.0), axis=0, keepdims=True)
        cur = jnp.where(hit, NEG_INF, cur)
        onehot = jnp.where(hit, 1.0, onehot)
        idx_rows.append(ik)
        idx_f = jnp.where(krow == k, ik, idx_f)
        sc_k = jnp.where(krow == k, sk, sc_k)
        sc_sum = sc_sum + sk
    wts_ref[...] = sc_k / sc_sum * ROUTED_SCALE
    idx_ref[...] = idx_f.astype(I32)

    tri = (lax.broadcasted_iota(I32, (tm, tm), 0) < lax.broadcasted_iota(I32, (tm, tm), 1))
    excl = jnp.dot(onehot.astype(BF16), tri.astype(BF16), preferred_element_type=F32)
    rank_e = carry_scr[:, 0:1] + excl
    rank_k = jnp.zeros((TOP_K, tm), F32)
    for k in range(TOP_K):
        hit = row == idx_rows[k]
        rk = jnp.sum(jnp.where(hit, rank_e, 0.0), axis=0, keepdims=True)
        rank_k = jnp.where(krow == k, rk, rank_k)
    rank_ref[...] = rank_k.astype(I32)
    carry_scr[...] = carry_scr[...] + jnp.sum(onehot, axis=1, keepdims=True)
    cnt_ref[...] = carry_scr[...]


def _mix_out(attn, yc, x2, mod3, gattn, wout, gpost, gpre2, wrt, br, seq):
    t, d = x2.shape
    tm = min(TM_OUT, seq)
    tpb = seq // tm
    full = lambda a: pl.BlockSpec(a.shape, lambda i: (0,) * a.ndim)
    row = lambda w: pl.BlockSpec((tm, w), lambda i: (i, 0))
    col = pl.BlockSpec((TOP_K, tm), lambda i: (0, i))
    return pl.pallas_call(
        _mix_out_kernel,
        grid=(t // tm,),
        in_specs=[row(attn.shape[1]), row(yc.shape[1]), row(d),
                  pl.BlockSpec((1, 6, d), lambda i: (i // tpb, 0, 0)),
                  full(gattn), full(wout), full(gpost), full(gpre2), full(wrt), full(br)],
        out_specs=[row(d), row(d), row(_ROW_WORDS), col, col, col,
                   pl.BlockSpec((N_EXPERTS, LANES), lambda i: (0, 0))],
        out_shape=[jax.ShapeDtypeStruct((t, d), F32), jax.ShapeDtypeStruct((t, d), BF16),
                   jax.ShapeDtypeStruct((t, _ROW_WORDS), U32),
                   jax.ShapeDtypeStruct((TOP_K, t), I32), jax.ShapeDtypeStruct((TOP_K, t), F32),
                   jax.ShapeDtypeStruct((TOP_K, t), I32),
                   jax.ShapeDtypeStruct((N_EXPERTS, LANES), F32)],
        scratch_shapes=[pltpu.VMEM((N_EXPERTS, LANES), F32)],
        compiler_params=_params(("arbitrary",)),
        name="mix_out",
    )(attn, yc, x2, mod3, gattn, wout, gpost, gpre2, wrt, br)


def _dest_kernel(idx_ref, rank_ref, pstart_ref, dest_ref):
    tm = idx_ref.shape[1]
    row = lax.broadcasted_iota(I32, (N_EXPERTS, tm), 0)
    krow = lax.broadcasted_iota(I32, (TOP_K, tm), 0)
    pstart = pstart_ref[...]
    idx = idx_ref[...]
    out = jnp.zeros((TOP_K, tm), F32)
    for k in range(TOP_K):
        hit = row == idx[k:k + 1, :]
        base = jnp.sum(jnp.where(hit, pstart, 0.0), axis=0, keepdims=True)
        out = jnp.where(krow == k, base, out)
    dest_ref[...] = out.astype(I32) + rank_ref[...]


def _dest(idx, rank, pstart):
    t = idx.shape[1]
    tm = min(TM_DEST, t)
    col = pl.BlockSpec((TOP_K, tm), lambda i: (0, i))
    return pl.pallas_call(
        _dest_kernel,
        grid=(t // tm,),
        in_specs=[col, col, pl.BlockSpec((N_EXPERTS, 1), lambda i: (0, 0))],
        out_specs=col,
        out_shape=jax.ShapeDtypeStruct((TOP_K, t), I32),
        compiler_params=_params(("arbitrary",)),
        name="dest",
    )(idx, rank, pstart)


_PAD_CHUNKS = tuple(BM_EXPERT >> s for s in range(1, BM_EXPERT.bit_length()))


def _dispatch_kernel(pad_from_ref, pad_n_ref, dest_ref, h2_ref, xs_hbm, zero_scr, sem, pad_sem):
    i = pl.program_id(0)
    tm = dest_ref.shape[1]

    @pl.when(i == 0)
    def _():
        zero_scr[...] = jnp.zeros(zero_scr.shape, zero_scr.dtype)

        def pad_copies(e, act):
            n = pad_n_ref[e]
            base = pad_from_ref[e]

            def single_rows(start, count):
                for j in range(SUBLANES - 1):
                    @pl.when(j < count)
                    def _():
                        act(pltpu.make_async_copy(zero_scr.at[pl.ds(0, 1), :],
                                                  xs_hbm.at[pl.ds(start + j, 1), :], pad_sem))

            head = jnp.minimum(n, (SUBLANES - (base & (SUBLANES - 1))) & (SUBLANES - 1))
            single_rows(base, head)
            rest = n - head
            mid = base + head
            for rows in _PAD_CHUNKS:
                if rows >= SUBLANES:
                    @pl.when((rest & rows) != 0)
                    def _():
                        start = pl.multiple_of(mid + (rest & ~(2 * rows - 1)), SUBLANES)
                        act(pltpu.make_async_copy(zero_scr.at[pl.ds(0, rows), :],
                                                  xs_hbm.at[pl.ds(start, rows), :], pad_sem))
            single_rows(mid + (rest & ~(SUBLANES - 1)), rest & (SUBLANES - 1))

        def issue_pad(e, carry):
            pad_copies(e, lambda cp: cp.start())
            return carry

        def drain_pad(e, carry):
            pad_copies(e, lambda cp: cp.wait())
            return carry

        lax.fori_loop(0, N_EXPERTS, issue_pad, 0)
        lax.fori_loop(0, N_EXPERTS, drain_pad, 0)

    def row_copy(t, k):
        return pltpu.make_async_copy(h2_ref.at[pl.ds(t, 1), :], xs_hbm.at[pl.ds(dest_ref[k, t], 1), :], sem)

    def issue(t, carry):
        for k in range(TOP_K):
            row_copy(t, k).start(priority=k % 2)
        return carry

    lax.fori_loop(0, tm, issue, 0)

    def drain(t, carry):
        for k in range(TOP_K):
            row_copy(t, k).wait()
        return carry

    lax.fori_loop(0, tm, drain, 0)


def _dispatch(pad_from, pad_n, dest, h2_rows, n_rows):
    t = h2_rows.shape[0]
    tm = min(TM_DISPATCH, t)
    return pl.pallas_call(
        _dispatch_kernel,
        grid_spec=pltpu.PrefetchScalarGridSpec(
            num_scalar_prefetch=2,
            grid=(t // tm,),
            in_specs=[pl.BlockSpec((TOP_K, tm), lambda i, pf, pn: (0, i), memory_space=pltpu.SMEM),
                      pl.BlockSpec((tm, _ROW_WORDS), lambda i, pf, pn: (i, 0))],
            out_specs=pl.BlockSpec(memory_space=pl.ANY),
            scratch_shapes=[pltpu.VMEM((BM_EXPERT // 2, _ROW_WORDS), U32),
                            pltpu.SemaphoreType.DMA, pltpu.SemaphoreType.DMA]),
        out_shape=jax.ShapeDtypeStruct((n_rows, _ROW_WORDS), U32),
        compiler_params=_params(("arbitrary",)),
        name="dispatch",
    )(pad_from, pad_n, dest, h2_rows)


_XS_SLOTS = 4
_YS_SLOTS = 2
_W_SLOTS = 2


def _expert_kernel(first_ref, ord_ref, uexp_ref, meta_ref, xs_hbm, wg_hbm, wu_hbm, wd_hbm, ys_hbm,
                   xs_buf, ys_buf, wg_buf, wu_buf, wd_buf, wgu_scr, wd_scr, x_scr, xs_sem, ys_sem, w_sem):
    i = pl.program_id(0)
    nused = meta_ref[0]
    nexp = meta_ref[1]
    bm = xs_buf.shape[1]

    def xs_copy(b, slot):
        return pltpu.make_async_copy(xs_hbm.at[pl.ds(b * bm, bm), :], xs_buf.at[slot], xs_sem.at[slot])

    def ys_copy(b, slot):
        return pltpu.make_async_copy(ys_buf.at[slot], ys_hbm.at[pl.ds(b * bm, bm), :], ys_sem.at[slot])

    def w_copies(j, slot):
        e = uexp_ref[j]
        return (pltpu.make_async_copy(wg_hbm.at[e], wg_buf.at[slot], w_sem.at[slot, 0]),
                pltpu.make_async_copy(wu_hbm.at[e], wu_buf.at[slot], w_sem.at[slot, 1]),
                pltpu.make_async_copy(wd_hbm.at[e], wd_buf.at[slot], w_sem.at[slot, 2]))

    @pl.when(i == 0)
    def _():
        for s in range(_XS_SLOTS - 1):
            @pl.when(s < nused)
            def _():
                xs_copy(s, s).start()
        for s in range(_W_SLOTS):
            @pl.when(s < nexp)
            def _():
                for cp in w_copies(s, s):
                    cp.start()

    @pl.when(i < nused)
    def _():
        ahead = i + _XS_SLOTS - 1

        @pl.when(ahead < nused)
        def _():
            xs_copy(ahead, ahead % _XS_SLOTS).start()

        @pl.when(first_ref[i] == 1)
        def _():
            j = ord_ref[i]
            ws = j % _W_SLOTS
            for cp in w_copies(j, ws):
                cp.wait()
            wgu_scr[:, 0:EXPERT_DIM] = wg_buf[ws].astype(BF16)
            wgu_scr[:, EXPERT_DIM:2 * EXPERT_DIM] = wu_buf[ws].astype(BF16)
            wd_scr[...] = wd_buf[ws].astype(BF16)

            @pl.when(j + _W_SLOTS < nexp)
            def _():
                for cp in w_copies(j + _W_SLOTS, ws):
                    cp.start()

        slot = i % _XS_SLOTS
        xs_copy(i, slot).wait()
        lo, hi = _unpack_row_words(xs_buf[slot])
        x_scr[:, 0:_ROW_WORDS] = lo.astype(BF16)
        x_scr[:, _ROW_WORDS:] = hi.astype(BF16)
        gu = jnp.dot(x_scr[...], wgu_scr[...], preferred_element_type=F32)
        g = gu[:, 0:EXPERT_DIM]
        a = (g * jax.nn.sigmoid(g) * gu[:, EXPERT_DIM:2 * EXPERT_DIM]).astype(BF16)
        y = jnp.dot(a, wd_scr[...], preferred_element_type=F32)
        oslot = i % _YS_SLOTS

        @pl.when(i >= _YS_SLOTS)
        def _():
            ys_copy(i - _YS_SLOTS, oslot).wait()

        ys_buf[oslot] = _pack_row_words(y[:, 0:_ROW_WORDS], y[:, _ROW_WORDS:])
        ys_copy(i, oslot).start()

        @pl.when(i == nused - 1)
        def _():
            ys_copy(i, oslot).wait()

            @pl.when(i >= 1)
            def _():
                ys_copy(i - 1, (i - 1) % _YS_SLOTS).wait()


def _experts(first, ordinal, uexp, meta, xs, w_gate, w_up, w_down):
    p = xs.shape[0]
    d = w_gate.shape[1]
    nb = p // BM_EXPERT
    anyspec = pl.BlockSpec(memory_space=pl.ANY)
    return pl.pallas_call(
        _expert_kernel,
        grid_spec=pltpu.PrefetchScalarGridSpec(
            num_scalar_prefetch=4,
            grid=(nb,),
            in_specs=[anyspec, anyspec, anyspec, anyspec],
            out_specs=anyspec,
            scratch_shapes=[pltpu.VMEM((_XS_SLOTS, BM_EXPERT, _ROW_WORDS), U32),
                            pltpu.VMEM((_YS_SLOTS, BM_EXPERT, _ROW_WORDS), U32),
                            pltpu.VMEM((_W_SLOTS, d, EXPERT_DIM), F32),
                            pltpu.VMEM((_W_SLOTS, d, EXPERT_DIM), F32),
                            pltpu.VMEM((_W_SLOTS, EXPERT_DIM, d), F32),
                            pltpu.VMEM((d, 2 * EXPERT_DIM), BF16), pltpu.VMEM((EXPERT_DIM, d), BF16),
                            pltpu.VMEM((BM_EXPERT, d), BF16),
                            pltpu.SemaphoreType.DMA((_XS_SLOTS,)), pltpu.SemaphoreType.DMA((_YS_SLOTS,)),
                            pltpu.SemaphoreType.DMA((_W_SLOTS, 3))]),
        out_shape=jax.ShapeDtypeStruct((p, _ROW_WORDS), U32),
        compiler_params=_params(("arbitrary",)),
        name="experts",
    )(first, ordinal, uexp, meta, xs, w_gate, w_up, w_down)


def _combine_kernel(dest_ref, wts_ref, h2_ref, x1_ref, mod_ref, wsg_ref, wsu_ref, wsd_ref, gpost_ref,
                    ys_hbm, o_ref, yg_scr, sem):
    tm = x1_ref.shape[0]

    def row_copy(t, k):
        return pltpu.make_async_copy(ys_hbm.at[pl.ds(dest_ref[k, t], 1), :], yg_scr.at[k, pl.ds(t, 1), :], sem)

    def issue(t, carry):
        for k in range(TOP_K):
            row_copy(t, k).start(priority=k % 2)
        return carry

    lax.fori_loop(0, tm, issue, 0)

    h2 = h2_ref[...]
    g = jnp.dot(h2, wsg_ref[...], preferred_element_type=F32)
    u = jnp.dot(h2, wsu_ref[...], preferred_element_type=F32)
    f = jnp.dot((g * jax.nn.sigmoid(g) * u).astype(BF16), wsd_ref[...], preferred_element_type=F32)

    def drain(t, carry):
        for k in range(TOP_K):
            row_copy(t, k).wait()
        return carry

    lax.fori_loop(0, tm, drain, 0)

    wts = wts_ref[...]
    los = [f[:, sl * LANES:(sl + 1) * LANES] for sl in range(_ROW_SLABS)]
    his = [f[:, _ROW_WORDS + sl * LANES:_ROW_WORDS + (sl + 1) * LANES] for sl in range(_ROW_SLABS)]
    for k in range(TOP_K):
        wk = wts[:, k:k + 1]
        for sl in range(_ROW_SLABS):
            lo, hi = _unpack_row_words(yg_scr[k, :, sl * LANES:(sl + 1) * LANES])
            los[sl] = los[sl] + wk * lo
            his[sl] = his[sl] + wk * hi
    f = jnp.concatenate(los + his, axis=1)
    g2 = mod_ref[0, 5:6, :]
    o_ref[...] = x1_ref[...] + g2 * _rms(f, gpost_ref[...])


def _combine(dest, wts_t, h2, x1, mod3, wsg, wsu, wsd, gpost, ys_rows, seq):
    t, d = x1.shape
    tm = min(TM_COMBINE, seq)
    tpb = seq // tm
    full = lambda a: pl.BlockSpec(a.shape, lambda i: (0,) * a.ndim)
    row = lambda w: pl.BlockSpec((tm, w), lambda i: (i, 0))
    return pl.pallas_call(
        _combine_kernel,
        grid=(t // tm,),
        in_specs=[pl.BlockSpec((TOP_K, tm), lambda i: (0, i), memory_space=pltpu.SMEM),
                  row(TOP_K), row(d), row(d),
                  pl.BlockSpec((1, 6, d), lambda i: (i // tpb, 0, 0)),
                  full(wsg), full(wsu), full(wsd), full(gpost),
                  pl.BlockSpec(memory_space=pl.ANY)],
        out_specs=row(d),
        out_shape=jax.ShapeDtypeStruct((t, d), F32),
        scratch_shapes=[pltpu.VMEM((TOP_K, tm, _ROW_WORDS), U32), pltpu.SemaphoreType.DMA],
        compiler_params=_params(("arbitrary",)),
        name="combine",
    )(dest, wts_t, h2, x1, mod3, wsg, wsu, wsd, gpost, ys_rows)


def _pack_weights(w_in, w_uq, w_ukv):
    d = w_in.shape[0]
    half = QK_ROPE_DIM // 2
    z = lambda n, c: jnp.zeros((n, c), F32)
    o = Q_LORA_RANK + KV_LORA_RANK
    kr = w_in[:, o:o + QK_ROPE_DIM]
    kr_grp = jnp.concatenate([z(d, QK_NOPE_DIM), kr, z(d, HEAD_PAD - QK_NOPE_DIM - QK_ROPE_DIM)], axis=1)
    kr_rot = jnp.concatenate([z(d, QK_NOPE_DIM), -kr[:, half:], kr[:, :half],
                              z(d, HEAD_PAD - QK_NOPE_DIM - QK_ROPE_DIM)], axis=1)
    win_p = jnp.concatenate([w_in[:, :o], kr_grp, kr_rot, w_in[:, o + QK_ROPE_DIM:]], axis=1)

    scale = float(QK_NOPE_DIM + QK_ROPE_DIM) ** -0.5 * float(np.log2(np.e))
    r = Q_LORA_RANK
    qd = QK_NOPE_DIM + QK_ROPE_DIM
    q_grp, q_rot = [], []
    for h in range(MLA_HEADS):
        nope = w_uq[:, h * qd:h * qd + QK_NOPE_DIM]
        rope = w_uq[:, h * qd + QK_NOPE_DIM:(h + 1) * qd]
        pad = z(r, HEAD_PAD - qd)
        q_grp.append(jnp.concatenate([nope, rope, pad], axis=1))
        q_rot.append(jnp.concatenate([z(r, QK_NOPE_DIM), -rope[:, half:], rope[:, :half], pad], axis=1))
    wuq_p = jnp.concatenate(q_grp + q_rot, axis=1) * scale

    c = KV_LORA_RANK
    kd = QK_NOPE_DIM + V_HEAD_DIM
    k_grp, v_grp = [], []
    for h in range(MLA_HEADS):
        k_grp.append(jnp.concatenate([w_ukv[:, h * kd:h * kd + QK_NOPE_DIM], z(c, HEAD_PAD - QK_NOPE_DIM)], axis=1))
        v_grp.append(jnp.concatenate([w_ukv[:, h * kd + QK_NOPE_DIM:(h + 1) * kd], z(c, HEAD_PAD - V_HEAD_DIM)], axis=1))
    wukv_p = jnp.concatenate(k_grp + v_grp, axis=1)
    return win_p.astype(BF16), wuq_p.astype(BF16), wukv_p.astype(BF16)


def _rope_tables(positions):
    inv = 1.0 / (ROPE_THETA ** (jnp.arange(0, QK_ROPE_DIM, 2, dtype=F32) / QK_ROPE_DIM))
    ang = positions.astype(F32).reshape(-1)[:, None] * inv
    t = ang.shape[0]
    cos, sin = jnp.cos(ang), jnp.sin(ang)
    tail = HEAD_PAD - QK_NOPE_DIM - QK_ROPE_DIM
    cos_t = jnp.concatenate([jnp.ones((t, QK_NOPE_DIM), F32), cos, cos, jnp.ones((t, tail), F32)], axis=1)
    sin_t = jnp.concatenate([jnp.zeros((t, QK_NOPE_DIM), F32), sin, sin, jnp.zeros((t, tail), F32)], axis=1)
    return cos_t, sin_t


def _layer(x2, c, cos_t, sin_t, batch, seq, w_ada, b_ada, g_pre_mix, w_in, g_q_lat, w_uq, g_kv_lat, w_ukv,
           w_conv, g_attn_out, g_conv_out, w_out, g_post_mix, g_pre_ffn, w_router, b_router,
           w_gate, w_up, w_down, w_sh_gate, w_sh_up, w_sh_down, g_post_ffn):
    t, d = x2.shape
    r1 = lambda a: a.reshape(1, -1)

    c_pad = jnp.zeros((SUBLANES, d), F32).at[:batch].set(c)
    mod = _ada(c_pad, w_ada, r1(b_ada))[:batch]
    mod3 = mod.reshape(batch, 6, d)

    win_p, wuq_p, wukv_p = _pack_weights(w_in, w_uq, w_ukv)
    vone = jnp.zeros((1, HEAD_PAD), F32).at[0, V_HEAD_DIM].set(1.0)
    q, k, v, yc = _mix_in(x2, mod3, r1(g_pre_mix), win_p, r1(g_q_lat), wuq_p, r1(g_kv_lat), wukv_p,
                          vone, w_conv, r1(g_conv_out), cos_t, sin_t, seq)
    attn = _attention(q, k, v, batch, seq)
    x1, h2, h2p, idx, wts, rank, cnt = _mix_out(
        attn, yc, x2, mod3, r1(g_attn_out), w_out.astype(BF16), r1(g_post_mix), r1(g_pre_ffn),
        w_router.T, b_router.reshape(-1, 1), seq)

    counts = cnt[:, 0].astype(I32)
    padded = ((counts + BM_EXPERT - 1) // BM_EXPERT) * BM_EXPERT
    pad_end = jnp.cumsum(padded)
    pad_start = pad_end - padded
    m = t * TOP_K
    nb = (m + N_EXPERTS * (BM_EXPERT - 1)) // BM_EXPERT
    nused = pad_end[-1] // BM_EXPERT
    bidx = jnp.arange(nb, dtype=I32)
    blk_exp = jnp.sum((pad_end[None, :] <= (bidx * BM_EXPERT)[:, None]).astype(I32), axis=1)
    first = ((bidx < nused) & ((bidx == 0) | (blk_exp != jnp.roll(blk_exp, 1)))).astype(I32)
    ordinal = jnp.maximum(jnp.cumsum(first) - 1, 0).astype(I32)
    seen = jnp.cumsum((counts > 0).astype(I32))
    uexp = jnp.minimum(jnp.sum((seen[None, :] <= jnp.arange(N_EXPERTS, dtype=I32)[:, None]).astype(I32), axis=1),
                       N_EXPERTS - 1).astype(I32)
    meta = jnp.stack([nused, seen[-1]]).astype(I32)

    dest = _dest(idx, rank, pad_start.astype(F32).reshape(-1, 1))
    xs = _dispatch((pad_start + counts).astype(I32), (padded - counts).astype(I32), dest, h2p, nb * BM_EXPERT)
    ys = _experts(first, ordinal, uexp, meta, xs, w_gate, w_up, w_down)
    return _combine(dest, wts.T, h2, x1, mod3, w_sh_gate.astype(BF16), w_sh_up.astype(BF16),
                    w_sh_down.astype(BF16), r1(g_post_ffn), ys, seq)


def kernel(x, c, positions, w_ada, b_ada, g_pre_mix, w_in, g_q_lat, w_uq, g_kv_lat, w_ukv, w_conv, g_attn_out, g_conv_out, w_out, g_post_mix, g_pre_ffn, w_router, b_router, w_gate, w_up, w_down, w_sh_gate, w_sh_up, w_sh_down, g_post_ffn):
    batch, seq, d = x.shape
    cos_t, sin_t = _rope_tables(positions)
    x2 = x.reshape(batch * seq, d)
    for l in range(w_ada.shape[0]):
        x2 = _layer(x2, c, cos_t, sin_t, batch, seq, w_ada[l], b_ada[l], g_pre_mix[l], w_in[l], g_q_lat[l],
                    w_uq[l], g_kv_lat[l], w_ukv[l], w_conv[l], g_attn_out[l], g_conv_out[l], w_out[l],
                    g_post_mix[l], g_pre_ffn[l], w_router[l], b_router[l], w_gate[l], w_up[l], w_down[l],
                    w_sh_gate[l], w_sh_up[l], w_sh_down[l], g_post_ffn[l])
    return x2.reshape(batch, seq, d)
```

```python
import functools

import jax
import jax.numpy as jnp
import numpy as np
from jax import lax
from jax.experimental import pallas as pl
from jax.experimental.pallas import tpu as pltpu
from jax.experimental.pallas import tpu_sc as plsc

F32 = jnp.float32
BF16 = jnp.bfloat16
I32 = jnp.int32
U32 = jnp.uint32

CHUNK = 64
MLA_HEADS = 8
QK_NOPE_DIM = 64
QK_ROPE_DIM = 32
V_HEAD_DIM = 64
Q_LORA_RANK = 384
KV_LORA_RANK = 256
ROPE_THETA = 10000.0
CONV_WIDTH = 3
N_EXPERTS = 256
TOP_K = 8
N_EXPERT_GROUPS = 8
TOPK_GROUPS = 4
EXPERT_DIM = 256
ROUTED_SCALE = 2.5
EPS = 1e-6

LANES = 128
SUBLANES = 8
HEAD_PAD = LANES
VMEM_LIMIT_BYTES = 56 * 1024 * 1024

TM_IN = 512
TQ_ATTN = 512
TM_OUT = 512
TM_DEST = 512
TM_DISPATCH = 512
BM_EXPERT = 256
TM_COMBINE = 256

NEG_INF = float("-inf")


def _rms(x, g):
    return x * lax.rsqrt(jnp.mean(x * x, axis=-1, keepdims=True) + EPS) * g


_HI_MASK = np.uint32(0xFFFF0000)
_ROW_WORDS = 512
_ROW_SLABS = _ROW_WORDS // LANES


def _pack_row_words(lo, hi):
    lo_w = lax.bitcast_convert_type(lo.astype(BF16).astype(F32), U32) >> 16
    hi_w = lax.bitcast_convert_type(hi.astype(BF16).astype(F32), U32) & _HI_MASK
    return lo_w | hi_w


def _unpack_row_words(w):
    return (lax.bitcast_convert_type(w << 16, F32), lax.bitcast_convert_type(w & _HI_MASK, F32))


def _params(sem):
    return pltpu.CompilerParams(dimension_semantics=sem, vmem_limit_bytes=VMEM_LIMIT_BYTES)


def _ada_kernel(c_ref, w_ref, b_ref, o_ref):
    c = c_ref[...]
    s = c * jax.nn.sigmoid(c)
    o_ref[...] = jnp.dot(s, w_ref[...], preferred_element_type=F32,
                         precision=lax.Precision.HIGHEST) + b_ref[...]


def _ada(c_pad, w, b):
    rows, d = c_pad.shape
    n = w.shape[1]
    tn = 1536
    return pl.pallas_call(
        _ada_kernel,
        grid=(n // tn,),
        in_specs=[pl.BlockSpec((rows, d), lambda j: (0, 0)),
                  pl.BlockSpec((d, tn), lambda j: (0, j)),
                  pl.BlockSpec((1, tn), lambda j: (0, j))],
        out_specs=pl.BlockSpec((rows, tn), lambda j: (0, j)),
        out_shape=jax.ShapeDtypeStruct((rows, n), F32),
        compiler_params=_params(("arbitrary",)),
        name="ada",
    )(c_pad, w, b)


_CQ0, _CQ1 = 0, Q_LORA_RANK
_CKV0, _CKV1 = _CQ1, _CQ1 + KV_LORA_RANK
_KR0, _KR1 = _CKV1, _CKV1 + 2 * HEAD_PAD
_CONV_DIM = 512
_GB0 = _KR1
_GC0 = _GB0 + _CONV_DIM
_XV0 = _GC0 + _CONV_DIM
_WIN_COLS = _XV0 + _CONV_DIM
_QW = MLA_HEADS * HEAD_PAD


def _mix_in_kernel(tiles_per_batch, x_ref, mod_ref, gpre_ref, win_ref, gq_ref, wuq_ref, gkv_ref,
                   wukv_ref, vone_ref, wconv_ref, gconv_ref, cos_ref, sin_ref,
                   q_ref, k_ref, v_ref, yc_ref, h_scr, u_scr):
    i = pl.program_id(0)
    tm = x_ref.shape[0]
    sh1 = mod_ref[0, 0:1, :]
    sc1 = mod_ref[0, 1:2, :]
    h = _rms(x_ref[...], gpre_ref[...]) * (1.0 + sc1) + sh1
    h_scr[...] = h.astype(BF16)
    cos = cos_ref[...]
    sin = sin_ref[...]

    cq = jnp.dot(h_scr[...], win_ref[:, _CQ0:_CQ1], preferred_element_type=F32)
    cqn = _rms(cq, gq_ref[...]).astype(BF16)
    qq = jnp.dot(cqn, wuq_ref[...], preferred_element_type=F32)
    for hd in range(MLA_HEADS):
        lo = hd * HEAD_PAD
        qh = qq[:, lo:lo + HEAD_PAD] * cos + qq[:, _QW + lo:_QW + lo + HEAD_PAD] * sin
        q_ref[:, lo:lo + HEAD_PAD] = qh.astype(BF16)

    ckv = jnp.dot(h_scr[...], win_ref[:, _CKV0:_CKV1], preferred_element_type=F32)
    ckvn = _rms(ckv, gkv_ref[...]).astype(BF16)
    kv = jnp.dot(ckvn, wukv_ref[...], preferred_element_type=F32)
    krr = jnp.dot(h_scr[...], win_ref[:, _KR0:_KR1], preferred_element_type=F32)
    kr = krr[:, 0:HEAD_PAD] * cos + krr[:, HEAD_PAD:2 * HEAD_PAD] * sin
    vone = vone_ref[...]
    for hd in range(MLA_HEADS):
        lo = hd * HEAD_PAD
        k_ref[:, lo:lo + HEAD_PAD] = (kv[:, lo:lo + HEAD_PAD] + kr).astype(BF16)
        v_ref[:, lo:lo + HEAD_PAD] = (kv[:, _QW + lo:_QW + lo + HEAD_PAD] + vone).astype(BF16)

    gb = jnp.dot(h_scr[...], win_ref[:, _GB0:_GC0], preferred_element_type=F32)
    gc = jnp.dot(h_scr[...], win_ref[:, _GC0:_XV0], preferred_element_type=F32)
    xv = jnp.dot(h_scr[...], win_ref[:, _XV0:_WIN_COLS], preferred_element_type=F32)
    u = gc * xv
    prev = u_scr[tm:tm + SUBLANES, :]
    first = (i % tiles_per_batch) == 0
    u_scr[0:SUBLANES, :] = jnp.where(first, jnp.zeros_like(prev), prev)
    u_scr[SUBLANES:tm + SUBLANES, :] = u
    um1 = u_scr[SUBLANES - 1:tm + SUBLANES - 1, :]
    um2 = u_scr[SUBLANES - 2:tm + SUBLANES - 2, :]
    conv = wconv_ref[0:1, :] * um2 + wconv_ref[1:2, :] * um1 + wconv_ref[2:3, :] * u
    yc_ref[...] = _rms(gb * conv, gconv_ref[...]).astype(BF16)


def _mix_in(x2, mod3, gpre, win_p, gq, wuq_p, gkv, wukv_p, vone, wconv, gconv, cos_t, sin_t, seq):
    t, d = x2.shape
    tm = min(TM_IN, seq)
    tpb = seq // tm
    full = lambda a: pl.BlockSpec(a.shape, lambda i: (0,) * a.ndim)
    row = lambda w: pl.BlockSpec((tm, w), lambda i: (i, 0))
    return pl.pallas_call(
        functools.partial(_mix_in_kernel, tpb),
        grid=(t // tm,),
        in_specs=[row(d),
                  pl.BlockSpec((1, 6, d), lambda i: (i // tpb, 0, 0)),
                  full(gpre), full(win_p), full(gq), full(wuq_p), full(gkv), full(wukv_p),
                  full(vone), full(wconv), full(gconv), row(HEAD_PAD), row(HEAD_PAD)],
        out_specs=[row(_QW), row(_QW), row(_QW), row(_CONV_DIM)],
        out_shape=[jax.ShapeDtypeStruct((t, _QW), BF16), jax.ShapeDtypeStruct((t, _QW), BF16),
                   jax.ShapeDtypeStruct((t, _QW), BF16), jax.ShapeDtypeStruct((t, _CONV_DIM), BF16)],
        scratch_shapes=[pltpu.VMEM((tm, d), BF16), pltpu.VMEM((tm + SUBLANES, _CONV_DIM), F32)],
        compiler_params=_params(("arbitrary",)),
        name="mix_in",
    )(x2, mod3, gpre, win_p, gq, wuq_p, gkv, wukv_p, vone, wconv, gconv, cos_t, sin_t)


_HEADS_PER_STEP = 2


def _attn_kernel(q_ref, k_ref, v_ref, o_ref, s_scr, m_scr, acc_scr):
    qi = pl.program_id(2)
    tq = q_ref.shape[0]
    tk = tq
    lane_groups = tk // LANES

    def tile_max(s):
        m = s[:, 0:LANES]
        for g in range(1, lane_groups):
            m = jnp.maximum(m, s[:, g * LANES:(g + 1) * LANES])
        return m

    heads = range(_HEADS_PER_STEP)
    lanes = [slice(hh * HEAD_PAD, (hh + 1) * HEAD_PAD) for hh in heads]

    def scores(hh, kv):
        off = pl.multiple_of(kv * tk, tk)
        return lax.dot_general(q_ref[:, lanes[hh]], k_ref[pl.ds(off, tk), lanes[hh]],
                               (((1,), (1,)), ((), ())), preferred_element_type=F32)

    m_scr[...] = jnp.full(m_scr.shape, NEG_INF, F32)

    def pass1(kv, carry):
        for hh in heads:
            s = scores(hh, kv)
            s_scr[hh, kv] = s
            m_scr[hh] = jnp.maximum(m_scr[hh], tile_max(s))
        return carry

    lax.fori_loop(0, qi, pass1, 0)
    rc = lax.broadcasted_iota(I32, (tq, tk), 0) // CHUNK
    cc = lax.broadcasted_iota(I32, (tq, tk), 1) // CHUNK
    for hh in heads:
        s = jnp.where(cc <= rc, scores(hh, qi), NEG_INF)
        s_scr[hh, qi] = s
        m_row = jnp.max(jnp.maximum(m_scr[hh], tile_max(s)), axis=1, keepdims=True)
        m_scr[hh] = jnp.broadcast_to(m_row, (tq, LANES))

    acc_scr[...] = jnp.zeros(acc_scr.shape, F32)

    def pass2(kv, carry):
        off = pl.multiple_of(kv * tk, tk)
        for hh in heads:
            mb = m_scr[hh]
            p = jnp.concatenate(
                [jnp.exp2(s_scr[hh, kv, :, g * LANES:(g + 1) * LANES] - mb) for g in range(lane_groups)],
                axis=1).astype(BF16)
            acc_scr[hh] += jnp.dot(p, v_ref[pl.ds(off, tk), lanes[hh]], preferred_element_type=F32)
        return carry

    lax.fori_loop(0, qi + 1, pass2, 0)
    for hh in heads:
        acc = acc_scr[hh]
        o = acc[:, 0:V_HEAD_DIM] / acc[:, V_HEAD_DIM:V_HEAD_DIM + 1]
        o_ref[:, hh * V_HEAD_DIM:(hh + 1) * V_HEAD_DIM] = o.astype(BF16)


def _attention(q, k, v, batch, seq):
    t = q.shape[0]
    tq = min(TQ_ATTN, seq)
    nq = seq // tq
    hw = _HEADS_PER_STEP * HEAD_PAD
    ow = _HEADS_PER_STEP * V_HEAD_DIM
    return pl.pallas_call(
        _attn_kernel,
        grid=(batch, MLA_HEADS // _HEADS_PER_STEP, nq),
        in_specs=[pl.BlockSpec((tq, hw), lambda b, j, i: (b * nq + i, j)),
                  pl.BlockSpec((seq, hw), lambda b, j, i: (b, j)),
                  pl.BlockSpec((seq, hw), lambda b, j, i: (b, j))],
        out_specs=pl.BlockSpec((tq, ow), lambda b, j, i: (b * nq + i, j)),
        out_shape=jax.ShapeDtypeStruct((t, MLA_HEADS * V_HEAD_DIM), BF16),
        scratch_shapes=[pltpu.VMEM((_HEADS_PER_STEP, nq, tq, tq), F32),
                        pltpu.VMEM((_HEADS_PER_STEP, tq, LANES), F32),
                        pltpu.VMEM((_HEADS_PER_STEP, tq, HEAD_PAD), F32)],
        compiler_params=_params(("arbitrary", "arbitrary", "arbitrary")),
        name="attn",
    )(q, k, v)


_GROUP_SIZE = N_EXPERTS // N_EXPERT_GROUPS
_BIG = 1.0e9


def _mix_out_kernel(attn_ref, yc_ref, x_ref, mod_ref, gattn_ref, wout_ref, gpost_ref, gpre2_ref,
                    wrt_ref, br_ref, x1_ref, h2_ref, h2p_ref, idx_ref, wts_ref, rank_ref, cnt_ref,
                    carry_scr):
    i = pl.program_id(0)
    tm = x_ref.shape[0]
    half = attn_ref.shape[1]

    @pl.when(i == 0)
    def _():
        carry_scr[...] = jnp.zeros(carry_scr.shape, F32)

    an = _rms(attn_ref[...].astype(F32), gattn_ref[...]).astype(BF16)
    mix = (jnp.dot(an, wout_ref[0:half, :], preferred_element_type=F32)
           + jnp.dot(yc_ref[...], wout_ref[half:, :], preferred_element_type=F32))
    g1 = mod_ref[0, 2:3, :]
    sh2 = mod_ref[0, 3:4, :]
    sc2 = mod_ref[0, 4:5, :]
    x1 = x_ref[...] + g1 * _rms(mix, gpost_ref[...])
    x1_ref[...] = x1
    h2 = _rms(x1, gpre2_ref[...]) * (1.0 + sc2) + sh2
    h2_ref[...] = h2.astype(BF16)
    words = _pack_row_words(h2[:, 0:_ROW_WORDS], h2[:, _ROW_WORDS:])
    h2p_ref[...] = words

    logits = lax.dot_general(wrt_ref[...], h2, (((1,), (1,)), ((), ())),
                             preferred_element_type=F32, precision=lax.Precision.HIGHEST)
    scores = jax.nn.sigmoid(logits)
    sel = scores + br_ref[...]
    row = lax.broadcasted_iota(I32, (N_EXPERTS, tm), 0).astype(F32)

    gscore = []
    rw = lax.broadcasted_iota(I32, (_GROUP_SIZE, tm), 0).astype(F32)
    for g in range(N_EXPERT_GROUPS):
        blk = sel[g * _GROUP_SIZE:(g + 1) * _GROUP_SIZE, :]
        m1 = jnp.max(blk, axis=0, keepdims=True)
        i1 = jnp.min(jnp.where(blk == m1, rw, _BIG), axis=0, keepdims=True)
        m2 = jnp.max(jnp.where(rw == i1, NEG_INF, blk), axis=0, keepdims=True)
        gscore.append(m1 + m2)

    gkeep = [jnp.zeros((1, tm), F32) for _ in range(N_EXPERT_GROUPS)]
    for _ in range(TOPK_GROUPS):
        mg = functools.reduce(jnp.maximum, gscore)
        ig = functools.reduce(jnp.minimum, [jnp.where(gscore[g] == mg, float(g), _BIG)
                                            for g in range(N_EXPERT_GROUPS)])
        for g in range(N_EXPERT_GROUPS):
            hit = ig == float(g)
            gkeep[g] = jnp.where(hit, 1.0, gkeep[g])
            gscore[g] = jnp.where(hit, NEG_INF, gscore[g])
    cur = jnp.concatenate(
        [jnp.where(gkeep[g] > 0.0, sel[g * _GROUP_SIZE:(g + 1) * _GROUP_SIZE, :], NEG_INF)
         for g in range(N_EXPERT_GROUPS)], axis=0)

    krow = lax.broadcasted_iota(I32, (TOP_K, tm), 0)
    idx_rows = []
    idx_f = jnp.zeros((TOP_K, tm), F32)
    sc_k = jnp.zeros((TOP_K, tm), F32)
    sc_sum = jnp.zeros((1, tm), F32)
    onehot = jnp.zeros((N_EXPERTS, tm), F32)
    for k in range(TOP_K):
        m = jnp.max(cur, axis=0, keepdims=True)
        ik = jnp.min(jnp.where(cur == m, row, _BIG), axis=0, keepdims=True)
        hit = row == ik
        sk = jnp.sum(jnp.where(hit, scores, 0.0), axis=0, keepdims=True)
        cur = jnp.where(hit, NEG_INF, cur)
        onehot = jnp.where(hit, 1.0, onehot)
        idx_rows.append(ik)
        idx_f = jnp.where(krow == k, ik, idx_f)
        sc_k = jnp.where(krow == k, sk, sc_k)
        sc_sum = sc_sum + sk
    wts_ref[...] = sc_k / sc_sum * ROUTED_SCALE
    idx_ref[...] = idx_f.astype(I32)

    tri = (lax.broadcasted_iota(I32, (tm, tm), 0) < lax.broadcasted_iota(I32, (tm, tm), 1))
    excl = jnp.dot(onehot.astype(BF16), tri.astype(BF16), preferred_element_type=F32)
    rank_e = carry_scr[:, 0:1] + excl
    rank_k = jnp.zeros((TOP_K, tm), F32)
    for k in range(TOP_K):
        hit = row == idx_rows[k]
        rk = jnp.sum(jnp.where(hit, rank_e, 0.0), axis=0, keepdims=True)
        rank_k = jnp.where(krow == k, rk, rank_k)
    rank_ref[...] = rank_k.astype(I32)
    carry_scr[...] = carry_scr[...] + jnp.sum(onehot, axis=1, keepdims=True)
    cnt_ref[...] = carry_scr[...]


def _mix_out(attn, yc, x2, mod3, gattn, wout, gpost, gpre2, wrt, br, seq):
    t, d = x2.shape
    tm = min(TM_OUT, seq)
    tpb = seq // tm
    full = lambda a: pl.BlockSpec(a.shape, lambda i: (0,) * a.ndim)
    row = lambda w: pl.BlockSpec((tm, w), lambda i: (i, 0))
    col = pl.BlockSpec((TOP_K, tm), lambda i: (0, i))
    return pl.pallas_call(
        _mix_out_kernel,
        grid=(t // tm,),
        in_specs=[row(attn.shape[1]), row(yc.shape[1]), row(d),
                  pl.BlockSpec((1, 6, d), lambda i: (i // tpb, 0, 0)),
                  full(gattn), full(wout), full(gpost), full(gpre2), full(wrt), full(br)],
        out_specs=[row(d), row(d), row(_ROW_WORDS), col, col, col,
                   pl.BlockSpec((N_EXPERTS, LANES), lambda i: (0, 0))],
        out_shape=[jax.ShapeDtypeStruct((t, d), F32), jax.ShapeDtypeStruct((t, d), BF16),
                   jax.ShapeDtypeStruct((t, _ROW_WORDS), U32),
                   jax.ShapeDtypeStruct((TOP_K, t), I32), jax.ShapeDtypeStruct((TOP_K, t), F32),
                   jax.ShapeDtypeStruct((TOP_K, t), I32),
                   jax.ShapeDtypeStruct((N_EXPERTS, LANES), F32)],
        scratch_shapes=[pltpu.VMEM((N_EXPERTS, LANES), F32)],
        compiler_params=_params(("arbitrary",)),
        name="mix_out",
    )(attn, yc, x2, mod3, gattn, wout, gpost, gpre2, wrt, br)


def _dest_kernel(idx_ref, rank_ref, pstart_ref, dest_ref):
    tm = idx_ref.shape[1]
    row = lax.broadcasted_iota(I32, (N_EXPERTS, tm), 0)
    krow = lax.broadcasted_iota(I32, (TOP_K, tm), 0)
    pstart = pstart_ref[...]
    idx = idx_ref[...]
    out = jnp.zeros((TOP_K, tm), F32)
    for k in range(TOP_K):
        hit = row == idx[k:k + 1, :]
        base = jnp.sum(jnp.where(hit, pstart, 0.0), axis=0, keepdims=True)
        out = jnp.where(krow == k, base, out)
    dest_ref[...] = out.astype(I32) + rank_ref[...]


def _dest(idx, rank, pstart):
    t = idx.shape[1]
    tm = min(TM_DEST, t)
    col = pl.BlockSpec((TOP_K, tm), lambda i: (0, i))
    return pl.pallas_call(
        _dest_kernel,
        grid=(t // tm,),
        in_specs=[col, col, pl.BlockSpec((N_EXPERTS, 1), lambda i: (0, 0))],
        out_specs=col,
        out_shape=jax.ShapeDtypeStruct((TOP_K, t), I32),
        compiler_params=_params(("arbitrary",)),
        name="dest",
    )(idx, rank, pstart)


_PAD_CHUNKS = tuple(BM_EXPERT >> s for s in range(1, BM_EXPERT.bit_length()))


def _dispatch_kernel(pad_from_ref, pad_n_ref, dest_ref, h2_ref, xs_hbm, zero_scr, sem, pad_sem):
    i = pl.program_id(0)
    tm = dest_ref.shape[1]

    @pl.when(i == 0)
    def _():
        zero_scr[...] = jnp.zeros(zero_scr.shape, zero_scr.dtype)

        def pad_copies(e, act):
            n = pad_n_ref[e]
            base = pad_from_ref[e]

            def single_rows(start, count):
                for j in range(SUBLANES - 1):
                    @pl.when(j < count)
                    def _():
                        act(pltpu.make_async_copy(zero_scr.at[pl.ds(0, 1), :],
                                                  xs_hbm.at[pl.ds(start + j, 1), :], pad_sem))

            head = jnp.minimum(n, (SUBLANES - (base & (SUBLANES - 1))) & (SUBLANES - 1))
            single_rows(base, head)
            rest = n - head
            mid = base + head
            for rows in _PAD_CHUNKS:
                if rows >= SUBLANES:
                    @pl.when((rest & rows) != 0)
                    def _():
                        start = pl.multiple_of(mid + (rest & ~(2 * rows - 1)), SUBLANES)
                        act(pltpu.make_async_copy(zero_scr.at[pl.ds(0, rows), :],
                                                  xs_hbm.at[pl.ds(start, rows), :], pad_sem))
            single_rows(mid + (rest & ~(SUBLANES - 1)), rest & (SUBLANES - 1))

        def issue_pad(e, carry):
            pad_copies(e, lambda cp: cp.start())
            return carry

        def drain_pad(e, carry):
            pad_copies(e, lambda cp: cp.wait())
            return carry

        lax.fori_loop(0, N_EXPERTS, issue_pad, 0)
        lax.fori_loop(0, N_EXPERTS, drain_pad, 0)

    def row_copy(t, k):
        return pltpu.make_async_copy(h2_ref.at[pl.ds(t, 1), :], xs_hbm.at[pl.ds(dest_ref[k, t], 1), :], sem)

    def issue(t, carry):
        for k in range(TOP_K):
            row_copy(t, k).start(priority=k % 2)
        return carry

    lax.fori_loop(0, tm, issue, 0)

    def drain(t, carry):
        for k in range(TOP_K):
            row_copy(t, k).wait()
        return carry

    lax.fori_loop(0, tm, drain, 0)


def _dispatch(pad_from, pad_n, dest, h2_rows, n_rows):
    t = h2_rows.shape[0]
    tm = min(TM_DISPATCH, t)
    return pl.pallas_call(
        _dispatch_kernel,
        grid_spec=pltpu.PrefetchScalarGridSpec(
            num_scalar_prefetch=2,
            grid=(t // tm,),
            in_specs=[pl.BlockSpec((TOP_K, tm), lambda i, pf, pn: (0, i), memory_space=pltpu.SMEM),
                      pl.BlockSpec((tm, _ROW_WORDS), lambda i, pf, pn: (i, 0))],
            out_specs=pl.BlockSpec(memory_space=pl.ANY),
            scratch_shapes=[pltpu.VMEM((BM_EXPERT // 2, _ROW_WORDS), U32),
                            pltpu.SemaphoreType.DMA, pltpu.SemaphoreType.DMA]),
        out_shape=jax.ShapeDtypeStruct((n_rows, _ROW_WORDS), U32),
        compiler_params=_params(("arbitrary",)),
        name="dispatch",
    )(pad_from, pad_n, dest, h2_rows)


_XS_SLOTS = 4
_YS_SLOTS = 2
_W_SLOTS = 2


def _expert_kernel(first_ref, ord_ref, uexp_ref, meta_ref, xs_hbm, wg_hbm, wu_hbm, wd_hbm, ys_hbm,
                   xs_buf, ys_buf, wg_buf, wu_buf, wd_buf, wgu_scr, wd_scr, x_scr, xs_sem, ys_sem, w_sem):
    i = pl.program_id(0)
    nused = meta_ref[0]
    nexp = meta_ref[1]
    bm = xs_buf.shape[1]

    def xs_copy(b, slot):
        return pltpu.make_async_copy(xs_hbm.at[pl.ds(b * bm, bm), :], xs_buf.at[slot], xs_sem.at[slot])

    def ys_copy(b, slot):
        return pltpu.make_async_copy(ys_buf.at[slot], ys_hbm.at[pl.ds(b * bm, bm), :], ys_sem.at[slot])

    def w_copies(j, slot):
        e = uexp_ref[j]
        return (pltpu.make_async_copy(wg_hbm.at[e], wg_buf.at[slot], w_sem.at[slot, 0]),
                pltpu.make_async_copy(wu_hbm.at[e], wu_buf.at[slot], w_sem.at[slot, 1]),
                pltpu.make_async_copy(wd_hbm.at[e], wd_buf.at[slot], w_sem.at[slot, 2]))

    @pl.when(i == 0)
    def _():
        for s in range(_XS_SLOTS - 1):
            @pl.when(s < nused)
            def _():
                xs_copy(s, s).start()
        for s in range(_W_SLOTS):
            @pl.when(s < nexp)
            def _():
                for cp in w_copies(s, s):
                    cp.start()

    @pl.when(i < nused)
    def _():
        ahead = i + _XS_SLOTS - 1

        @pl.when(ahead < nused)
        def _():
            xs_copy(ahead, ahead % _XS_SLOTS).start()

        @pl.when(first_ref[i] == 1)
        def _():
            j = ord_ref[i]
            ws = j % _W_SLOTS
            for cp in w_copies(j, ws):
                cp.wait()
            wgu_scr[:, 0:EXPERT_DIM] = wg_buf[ws].astype(BF16)
            wgu_scr[:, EXPERT_DIM:2 * EXPERT_DIM] = wu_buf[ws].astype(BF16)
            wd_scr[...] = wd_buf[ws].astype(BF16)

            @pl.when(j + _W_SLOTS < nexp)
            def _():
                for cp in w_copies(j + _W_SLOTS, ws):
                    cp.start()

        slot = i % _XS_SLOTS
        xs_copy(i, slot).wait()
        lo, hi = _unpack_row_words(xs_buf[slot])
        x_scr[:, 0:_ROW_WORDS] = lo.astype(BF16)
        x_scr[:, _ROW_WORDS:] = hi.astype(BF16)
        gu = jnp.dot(x_scr[...], wgu_scr[...], preferred_element_type=F32)
        g = gu[:, 0:EXPERT_DIM]
        a = (g * jax.nn.sigmoid(g) * gu[:, EXPERT_DIM:2 * EXPERT_DIM]).astype(BF16)
        y = jnp.dot(a, wd_scr[...], preferred_element_type=F32)
        oslot = i % _YS_SLOTS

        @pl.when(i >= _YS_SLOTS)
        def _():
            ys_copy(i - _YS_SLOTS, oslot).wait()

        ys_buf[oslot] = _pack_row_words(y[:, 0:_ROW_WORDS], y[:, _ROW_WORDS:])
        ys_copy(i, oslot).start()

        @pl.when(i == nused - 1)
        def _():
            ys_copy(i, oslot).wait()

            @pl.when(i >= 1)
            def _():
                ys_copy(i - 1, (i - 1) % _YS_SLOTS).wait()


def _experts(first, ordinal, uexp, meta, xs, w_gate, w_up, w_down):
    p = xs.shape[0]
    d = w_gate.shape[1]
    nb = p // BM_EXPERT
    anyspec = pl.BlockSpec(memory_space=pl.ANY)
    return pl.pallas_call(
        _expert_kernel,
        grid_spec=pltpu.PrefetchScalarGridSpec(
            num_scalar_prefetch=4,
            grid=(nb,),
            in_specs=[anyspec, anyspec, anyspec, anyspec],
            out_specs=anyspec,
            scratch_shapes=[pltpu.VMEM((_XS_SLOTS, BM_EXPERT, _ROW_WORDS), U32),
                            pltpu.VMEM((_YS_SLOTS, BM_EXPERT, _ROW_WORDS), U32),
                            pltpu.VMEM((_W_SLOTS, d, EXPERT_DIM), F32),
                            pltpu.VMEM((_W_SLOTS, d, EXPERT_DIM), F32),
                            pltpu.VMEM((_W_SLOTS, EXPERT_DIM, d), F32),
                            pltpu.VMEM((d, 2 * EXPERT_DIM), BF16), pltpu.VMEM((EXPERT_DIM, d), BF16),
                            pltpu.VMEM((BM_EXPERT, d), BF16),
                            pltpu.SemaphoreType.DMA((_XS_SLOTS,)), pltpu.SemaphoreType.DMA((_YS_SLOTS,)),
                            pltpu.SemaphoreType.DMA((_W_SLOTS, 3))]),
        out_shape=jax.ShapeDtypeStruct((p, _ROW_WORDS), U32),
        compiler_params=_params(("arbitrary",)),
        name="experts",
    )(first, ordinal, uexp, meta, xs, w_gate, w_up, w_down)


SC_CORES = 2
SC_SUBCORES = 16
SC_GATHER_ROWS = 128


def _gather_rows_sc(table, idx):
    n = idx.shape[0]
    width = table.shape[1]
    workers = SC_CORES * SC_SUBCORES
    per_worker = n // workers
    chunks = per_worker // SC_GATHER_ROWS
    assert per_worker * workers == n and chunks * SC_GATHER_ROWS == per_worker
    mesh = plsc.VectorSubcoreMesh(core_axis_name="c", subcore_axis_name="s")

    @functools.partial(
        pl.kernel, mesh=mesh, out_type=jax.ShapeDtypeStruct((n, width), table.dtype),
        scratch_types=[pltpu.VMEM((SC_GATHER_ROWS,), I32), pltpu.VMEM((SC_GATHER_ROWS, width), table.dtype),
                       pltpu.SemaphoreType.DMA])
    def gather(table_hbm, idx_hbm, out_hbm, idx_v, rows_v, sem):
        base = (lax.axis_index("s") * SC_CORES + lax.axis_index("c")) * per_worker

        @pl.loop(0, chunks)
        def _(c):
            off = base + c * SC_GATHER_ROWS
            pltpu.sync_copy(idx_hbm.at[pl.ds(off, SC_GATHER_ROWS)], idx_v)
            pltpu.async_copy(table_hbm.at[idx_v], rows_v, sem).wait()
            pltpu.sync_copy(rows_v, out_hbm.at[pl.ds(off, SC_GATHER_ROWS)])

    return gather(table, idx)


def _combine_kernel(wts_ref, yg_ref, h2_ref, x1_ref, mod_ref, wsg_ref, wsu_ref, wsd_ref, gpost_ref, o_ref):
    h2 = h2_ref[...]
    g = jnp.dot(h2, wsg_ref[...], preferred_element_type=F32)
    u = jnp.dot(h2, wsu_ref[...], preferred_element_type=F32)
    f = jnp.dot((g * jax.nn.sigmoid(g) * u).astype(BF16), wsd_ref[...], preferred_element_type=F32)

    wts = wts_ref[...]
    los = [f[:, sl * LANES:(sl + 1) * LANES] for sl in range(_ROW_SLABS)]
    his = [f[:, _ROW_WORDS + sl * LANES:_ROW_WORDS + (sl + 1) * LANES] for sl in range(_ROW_SLABS)]
    for k in range(TOP_K):
        wk = wts[:, k:k + 1]
        for sl in range(_ROW_SLABS):
            lo, hi = _unpack_row_words(yg_ref[k, :, sl * LANES:(sl + 1) * LANES])
            los[sl] = los[sl] + wk * lo
            his[sl] = his[sl] + wk * hi
    f = jnp.concatenate(los + his, axis=1)
    g2 = mod_ref[0, 5:6, :]
    o_ref[...] = x1_ref[...] + g2 * _rms(f, gpost_ref[...])


def _combine(wts_t, yg, h2, x1, mod3, wsg, wsu, wsd, gpost, seq):
    t, d = x1.shape
    tm = min(TM_COMBINE, seq)
    tpb = seq // tm
    full = lambda a: pl.BlockSpec(a.shape, lambda i: (0,) * a.ndim)
    row = lambda w: pl.BlockSpec((tm, w), lambda i: (i, 0))
    return pl.pallas_call(
        _combine_kernel,
        grid=(t // tm,),
        in_specs=[row(TOP_K), pl.BlockSpec((TOP_K, tm, _ROW_WORDS), lambda i: (0, i, 0)), row(d), row(d),
                  pl.BlockSpec((1, 6, d), lambda i: (i // tpb, 0, 0)),
                  full(wsg), full(wsu), full(wsd), full(gpost)],
        out_specs=row(d),
        out_shape=jax.ShapeDtypeStruct((t, d), F32),
        compiler_params=_params(("arbitrary",)),
        name="combine",
    )(wts_t, yg, h2, x1, mod3, wsg, wsu, wsd, gpost)


def _pack_weights(w_in, w_uq, w_ukv):
    d = w_in.shape[0]
    half = QK_ROPE_DIM // 2
    z = lambda n, c: jnp.zeros((n, c), F32)
    o = Q_LORA_RANK + KV_LORA_RANK
    kr = w_in[:, o:o + QK_ROPE_DIM]
    kr_grp = jnp.concatenate([z(d, QK_NOPE_DIM), kr, z(d, HEAD_PAD - QK_NOPE_DIM - QK_ROPE_DIM)], axis=1)
    kr_rot = jnp.concatenate([z(d, QK_NOPE_DIM), -kr[:, half:], kr[:, :half],
                              z(d, HEAD_PAD - QK_NOPE_DIM - QK_ROPE_DIM)], axis=1)
    win_p = jnp.concatenate([w_in[:, :o], kr_grp, kr_rot, w_in[:, o + QK_ROPE_DIM:]], axis=1)

    scale = float(QK_NOPE_DIM + QK_ROPE_DIM) ** -0.5 * float(np.log2(np.e))
    r = Q_LORA_RANK
    qd = QK_NOPE_DIM + QK_ROPE_DIM
    q_grp, q_rot = [], []
    for h in range(MLA_HEADS):
        nope = w_uq[:, h * qd:h * qd + QK_NOPE_DIM]
        rope = w_uq[:, h * qd + QK_NOPE_DIM:(h + 1) * qd]
        pad = z(r, HEAD_PAD - qd)
        q_grp.append(jnp.concatenate([nope, rope, pad], axis=1))
        q_rot.append(jnp.concatenate([z(r, QK_NOPE_DIM), -rope[:, half:], rope[:, :half], pad], axis=1))
    wuq_p = jnp.concatenate(q_grp + q_rot, axis=1) * scale

    c = KV_LORA_RANK
    kd = QK_NOPE_DIM + V_HEAD_DIM
    k_grp, v_grp = [], []
    for h in range(MLA_HEADS):
        k_grp.append(jnp.concatenate([w_ukv[:, h * kd:h * kd + QK_NOPE_DIM], z(c, HEAD_PAD - QK_NOPE_DIM)], axis=1))
        v_grp.append(jnp.concatenate([w_ukv[:, h * kd + QK_NOPE_DIM:(h + 1) * kd], z(c, HEAD_PAD - V_HEAD_DIM)], axis=1))
    wukv_p = jnp.concatenate(k_grp + v_grp, axis=1)
    return win_p.astype(BF16), wuq_p.astype(BF16), wukv_p.astype(BF16)


def _rope_tables(positions):
    inv = 1.0 / (ROPE_THETA ** (jnp.arange(0, QK_ROPE_DIM, 2, dtype=F32) / QK_ROPE_DIM))
    ang = positions.astype(F32).reshape(-1)[:, None] * inv
    t = ang.shape[0]
    cos, sin = jnp.cos(ang), jnp.sin(ang)
    tail = HEAD_PAD - QK_NOPE_DIM - QK_ROPE_DIM
    cos_t = jnp.concatenate([jnp.ones((t, QK_NOPE_DIM), F32), cos, cos, jnp.ones((t, tail), F32)], axis=1)
    sin_t = jnp.concatenate([jnp.zeros((t, QK_NOPE_DIM), F32), sin, sin, jnp.zeros((t, tail), F32)], axis=1)
    return cos_t, sin_t


def _layer(x2, c, cos_t, sin_t, batch, seq, w_ada, b_ada, g_pre_mix, w_in, g_q_lat, w_uq, g_kv_lat, w_ukv,
           w_conv, g_attn_out, g_conv_out, w_out, g_post_mix, g_pre_ffn, w_router, b_router,
           w_gate, w_up, w_down, w_sh_gate, w_sh_up, w_sh_down, g_post_ffn):
    t, d = x2.shape
    r1 = lambda a: a.reshape(1, -1)

    c_pad = jnp.zeros((SUBLANES, d), F32).at[:batch].set(c)
    mod = _ada(c_pad, w_ada, r1(b_ada))[:batch]
    mod3 = mod.reshape(batch, 6, d)

    win_p, wuq_p, wukv_p = _pack_weights(w_in, w_uq, w_ukv)
    vone = jnp.zeros((1, HEAD_PAD), F32).at[0, V_HEAD_DIM].set(1.0)
    q, k, v, yc = _mix_in(x2, mod3, r1(g_pre_mix), win_p, r1(g_q_lat), wuq_p, r1(g_kv_lat), wukv_p,
                          vone, w_conv, r1(g_conv_out), cos_t, sin_t, seq)
    attn = _attention(q, k, v, batch, seq)
    x1, h2, h2p, idx, wts, rank, cnt = _mix_out(
        attn, yc, x2, mod3, r1(g_attn_out), w_out.astype(BF16), r1(g_post_mix), r1(g_pre_ffn),
        w_router.T, b_router.reshape(-1, 1), seq)

    counts = cnt[:, 0].astype(I32)
    padded = ((counts + BM_EXPERT - 1) // BM_EXPERT) * BM_EXPERT
    pad_end = jnp.cumsum(padded)
    pad_start = pad_end - padded
    m = t * TOP_K
    nb = (m + N_EXPERTS * (BM_EXPERT - 1)) // BM_EXPERT
    nused = pad_end[-1] // BM_EXPERT
    bidx = jnp.arange(nb, dtype=I32)
    blk_exp = jnp.sum((pad_end[None, :] <= (bidx * BM_EXPERT)[:, None]).astype(I32), axis=1)
    first = ((bidx < nused) & ((bidx == 0) | (blk_exp != jnp.roll(blk_exp, 1)))).astype(I32)
    ordinal = jnp.maximum(jnp.cumsum(first) - 1, 0).astype(I32)
    seen = jnp.cumsum((counts > 0).astype(I32))
    uexp = jnp.minimum(jnp.sum((seen[None, :] <= jnp.arange(N_EXPERTS, dtype=I32)[:, None]).astype(I32), axis=1),
                       N_EXPERTS - 1).astype(I32)
    meta = jnp.stack([nused, seen[-1]]).astype(I32)

    dest = _dest(idx, rank, pad_start.astype(F32).reshape(-1, 1))
    xs = _dispatch((pad_start + counts).astype(I32), (padded - counts).astype(I32), dest, h2p, nb * BM_EXPERT)
    ys = _experts(first, ordinal, uexp, meta, xs, w_gate, w_up, w_down)
    yg = _gather_rows_sc(ys, dest.reshape(-1)).reshape(TOP_K, t, _ROW_WORDS)
    return _combine(wts.T, yg, h2, x1, mod3, w_sh_gate.astype(BF16), w_sh_up.astype(BF16),
                    w_sh_down.astype(BF16), r1(g_post_ffn), seq)


def kernel(x, c, positions, w_ada, b_ada, g_pre_mix, w_in, g_q_lat, w_uq, g_kv_lat, w_ukv, w_conv, g_attn_out, g_conv_out, w_out, g_post_mix, g_pre_ffn, w_router, b_router, w_gate, w_up, w_down, w_sh_gate, w_sh_up, w_sh_down, g_post_ffn):
    batch, seq, d = x.shape
    cos_t, sin_t = _rope_tables(positions)
    x2 = x.reshape(batch * seq, d)
    for l in range(w_ada.shape[0]):
        x2 = _layer(x2, c, cos_t, sin_t, batch, seq, w_ada[l], b_ada[l], g_pre_mix[l], w_in[l], g_q_lat[l],
                    w_uq[l], g_kv_lat[l], w_ukv[l], w_conv[l], g_attn_out[l], g_conv_out[l], w_out[l],
                    g_post_mix[l], g_pre_ffn[l], w_router[l], b_router[l], w_gate[l], w_up[l], w_down[l],
                    w_sh_gate[l], w_sh_up[l], w_sh_down[l], g_post_ffn[l])
    return x2.reshape(batch, seq, d)
```

```python
import functools

import jax
import jax.numpy as jnp
import numpy as np
from jax import lax
from jax.experimental import pallas as pl
from jax.experimental.pallas import tpu as pltpu
from jax.experimental.pallas import tpu_sc as plsc

F32 = jnp.float32
BF16 = jnp.bfloat16
I32 = jnp.int32
U32 = jnp.uint32

CHUNK = 64
MLA_HEADS = 8
QK_NOPE_DIM = 64
QK_ROPE_DIM = 32
V_HEAD_DIM = 64
Q_LORA_RANK = 384
KV_LORA_RANK = 256
ROPE_THETA = 10000.0
CONV_WIDTH = 3
N_EXPERTS = 256
TOP_K = 8
N_EXPERT_GROUPS = 8
TOPK_GROUPS = 4
EXPERT_DIM = 256
ROUTED_SCALE = 2.5
EPS = 1e-6

LANES = 128
SUBLANES = 8
HEAD_PAD = LANES
VMEM_LIMIT_BYTES = 56 * 1024 * 1024

TM_IN = 512
TQ_ATTN = 512
TM_OUT = 512
TM_DEST = 512
TM_DISPATCH = 512
BM_EXPERT = 256
TM_COMBINE = 256

NEG_INF = float("-inf")


def _rms(x, g):
    return x * lax.rsqrt(jnp.mean(x * x, axis=-1, keepdims=True) + EPS) * g


_HI_MASK = np.uint32(0xFFFF0000)
_ROW_WORDS = 512
_ROW_SLABS = _ROW_WORDS // LANES


def _pack_row_words(lo, hi):
    lo_w = lax.bitcast_convert_type(lo.astype(BF16).astype(F32), U32) >> 16
    hi_w = lax.bitcast_convert_type(hi.astype(BF16).astype(F32), U32) & _HI_MASK
    return lo_w | hi_w


def _unpack_row_words(w):
    return (lax.bitcast_convert_type(w << 16, F32), lax.bitcast_convert_type(w & _HI_MASK, F32))


def _params(sem):
    return pltpu.CompilerParams(dimension_semantics=sem, vmem_limit_bytes=VMEM_LIMIT_BYTES)


def _ada_kernel(c_ref, w_ref, b_ref, o_ref):
    c = c_ref[...]
    s = c * jax.nn.sigmoid(c)
    o_ref[...] = jnp.dot(s, w_ref[...], preferred_element_type=F32,
                         precision=lax.Precision.HIGHEST) + b_ref[...]


def _ada(c_pad, w, b):
    rows, d = c_pad.shape
    n = w.shape[1]
    tn = 1536
    return pl.pallas_call(
        _ada_kernel,
        grid=(n // tn,),
        in_specs=[pl.BlockSpec((rows, d), lambda j: (0, 0)),
                  pl.BlockSpec((d, tn), lambda j: (0, j)),
                  pl.BlockSpec((1, tn), lambda j: (0, j))],
        out_specs=pl.BlockSpec((rows, tn), lambda j: (0, j)),
        out_shape=jax.ShapeDtypeStruct((rows, n), F32),
        compiler_params=_params(("arbitrary",)),
        name="ada",
    )(c_pad, w, b)


_CQ0, _CQ1 = 0, Q_LORA_RANK
_CKV0, _CKV1 = _CQ1, _CQ1 + KV_LORA_RANK
_KR0, _KR1 = _CKV1, _CKV1 + 2 * HEAD_PAD
_CONV_DIM = 512
_GB0 = _KR1
_GC0 = _GB0 + _CONV_DIM
_XV0 = _GC0 + _CONV_DIM
_WIN_COLS = _XV0 + _CONV_DIM
_QW = MLA_HEADS * HEAD_PAD


def _mix_in_kernel(tiles_per_batch, x_ref, mod_ref, gpre_ref, win_ref, gq_ref, wuq_ref, gkv_ref,
                   wukv_ref, vone_ref, wconv_ref, gconv_ref, cos_ref, sin_ref,
                   q_ref, k_ref, v_ref, yc_ref, h_scr, u_scr):
    i = pl.program_id(0)
    tm = x_ref.shape[0]
    sh1 = mod_ref[0, 0:1, :]
    sc1 = mod_ref[0, 1:2, :]
    h = _rms(x_ref[...], gpre_ref[...]) * (1.0 + sc1) + sh1
    h_scr[...] = h.astype(BF16)
    cos = cos_ref[...]
    sin = sin_ref[...]

    cq = jnp.dot(h_scr[...], win_ref[:, _CQ0:_CQ1], preferred_element_type=F32)
    cqn = _rms(cq, gq_ref[...]).astype(BF16)
    qq = jnp.dot(cqn, wuq_ref[...], preferred_element_type=F32)
    for hd in range(MLA_HEADS):
        lo = hd * HEAD_PAD
        qh = qq[:, lo:lo + HEAD_PAD] * cos + qq[:, _QW + lo:_QW + lo + HEAD_PAD] * sin
        q_ref[:, lo:lo + HEAD_PAD] = qh.astype(BF16)

    ckv = jnp.dot(h_scr[...], win_ref[:, _CKV0:_CKV1], preferred_element_type=F32)
    ckvn = _rms(ckv, gkv_ref[...]).astype(BF16)
    kv = jnp.dot(ckvn, wukv_ref[...], preferred_element_type=F32)
    krr = jnp.dot(h_scr[...], win_ref[:, _KR0:_KR1], preferred_element_type=F32)
    kr = krr[:, 0:HEAD_PAD] * cos + krr[:, HEAD_PAD:2 * HEAD_PAD] * sin
    vone = vone_ref[...]
    for hd in range(MLA_HEADS):
        lo = hd * HEAD_PAD
        k_ref[:, lo:lo + HEAD_PAD] = (kv[:, lo:lo + HEAD_PAD] + kr).astype(BF16)
        v_ref[:, lo:lo + HEAD_PAD] = (kv[:, _QW + lo:_QW + lo + HEAD_PAD] + vone).astype(BF16)

    gb = jnp.dot(h_scr[...], win_ref[:, _GB0:_GC0], preferred_element_type=F32)
    gc = jnp.dot(h_scr[...], win_ref[:, _GC0:_XV0], preferred_element_type=F32)
    xv = jnp.dot(h_scr[...], win_ref[:, _XV0:_WIN_COLS], preferred_element_type=F32)
    u = gc * xv
    prev = u_scr[tm:tm + SUBLANES, :]
    first = (i % tiles_per_batch) == 0
    u_scr[0:SUBLANES, :] = jnp.where(first, jnp.zeros_like(prev), prev)
    u_scr[SUBLANES:tm + SUBLANES, :] = u
    um1 = u_scr[SUBLANES - 1:tm + SUBLANES - 1, :]
    um2 = u_scr[SUBLANES - 2:tm + SUBLANES - 2, :]
    conv = wconv_ref[0:1, :] * um2 + wconv_ref[1:2, :] * um1 + wconv_ref[2:3, :] * u
    yc_ref[...] = _rms(gb * conv, gconv_ref[...]).astype(BF16)


def _mix_in(x2, mod3, gpre, win_p, gq, wuq_p, gkv, wukv_p, vone, wconv, gconv, cos_t, sin_t, seq):
    t, d = x2.shape
    tm = min(TM_IN, seq)
    tpb = seq // tm
    full = lambda a: pl.BlockSpec(a.shape, lambda i: (0,) * a.ndim)
    row = lambda w: pl.BlockSpec((tm, w), lambda i: (i, 0))
    return pl.pallas_call(
        functools.partial(_mix_in_kernel, tpb),
        grid=(t // tm,),
        in_specs=[row(d),
                  pl.BlockSpec((1, 6, d), lambda i: (i // tpb, 0, 0)),
                  full(gpre), full(win_p), full(gq), full(wuq_p), full(gkv), full(wukv_p),
                  full(vone), full(wconv), full(gconv), row(HEAD_PAD), row(HEAD_PAD)],
        out_specs=[row(_QW), row(_QW), row(_QW), row(_CONV_DIM)],
        out_shape=[jax.ShapeDtypeStruct((t, _QW), BF16), jax.ShapeDtypeStruct((t, _QW), BF16),
                   jax.ShapeDtypeStruct((t, _QW), BF16), jax.ShapeDtypeStruct((t, _CONV_DIM), BF16)],
        scratch_shapes=[pltpu.VMEM((tm, d), BF16), pltpu.VMEM((tm + SUBLANES, _CONV_DIM), F32)],
        compiler_params=_params(("arbitrary",)),
        name="mix_in",
    )(x2, mod3, gpre, win_p, gq, wuq_p, gkv, wukv_p, vone, wconv, gconv, cos_t, sin_t)


_HEADS_PER_STEP = 2


def _attn_kernel(q_ref, k_ref, v_ref, o_ref, s_scr, m_scr, acc_scr):
    qi = pl.program_id(2)
    tq = q_ref.shape[0]
    tk = tq
    lane_groups = tk // LANES

    def tile_max(s):
        m = s[:, 0:LANES]
        for g in range(1, lane_groups):
            m = jnp.maximum(m, s[:, g * LANES:(g + 1) * LANES])
        return m

    heads = range(_HEADS_PER_STEP)
    lanes = [slice(hh * HEAD_PAD, (hh + 1) * HEAD_PAD) for hh in heads]

    def scores(hh, kv):
        off = pl.multiple_of(kv * tk, tk)
        return lax.dot_general(q_ref[:, lanes[hh]], k_ref[pl.ds(off, tk), lanes[hh]],
                               (((1,), (1,)), ((), ())), preferred_element_type=F32)

    m_scr[...] = jnp.full(m_scr.shape, NEG_INF, F32)

    def pass1(kv, carry):
        for hh in heads:
            s = scores(hh, kv)
            s_scr[hh, kv] = s
            m_scr[hh] = jnp.maximum(m_scr[hh], tile_max(s))
        return carry

    lax.fori_loop(0, qi, pass1, 0)
    rc = lax.broadcasted_iota(I32, (tq, tk), 0) // CHUNK
    cc = lax.broadcasted_iota(I32, (tq, tk), 1) // CHUNK
    for hh in heads:
        s = jnp.where(cc <= rc, scores(hh, qi), NEG_INF)
        s_scr[hh, qi] = s
        m_row = jnp.max(jnp.maximum(m_scr[hh], tile_max(s)), axis=1, keepdims=True)
        m_scr[hh] = jnp.broadcast_to(m_row, (tq, LANES))

    acc_scr[...] = jnp.zeros(acc_scr.shape, F32)

    def pass2(kv, carry):
        off = pl.multiple_of(kv * tk, tk)
        for hh in heads:
            mb = m_scr[hh]
            p = jnp.concatenate(
                [jnp.exp2(s_scr[hh, kv, :, g * LANES:(g + 1) * LANES] - mb) for g in range(lane_groups)],
                axis=1).astype(BF16)
            acc_scr[hh] += jnp.dot(p, v_ref[pl.ds(off, tk), lanes[hh]], preferred_element_type=F32)
        return carry

    lax.fori_loop(0, qi + 1, pass2, 0)
    for hh in heads:
        acc = acc_scr[hh]
        o = acc[:, 0:V_HEAD_DIM] / acc[:, V_HEAD_DIM:V_HEAD_DIM + 1]
        o_ref[:, hh * V_HEAD_DIM:(hh + 1) * V_HEAD_DIM] = o.astype(BF16)


def _attention(q, k, v, batch, seq):
    t = q.shape[0]
    tq = min(TQ_ATTN, seq)
    nq = seq // tq
    hw = _HEADS_PER_STEP * HEAD_PAD
    ow = _HEADS_PER_STEP * V_HEAD_DIM
    return pl.pallas_call(
        _attn_kernel,
        grid=(batch, MLA_HEADS // _HEADS_PER_STEP, nq),
        in_specs=[pl.BlockSpec((tq, hw), lambda b, j, i: (b * nq + i, j)),
                  pl.BlockSpec((seq, hw), lambda b, j, i: (b, j)),
                  pl.BlockSpec((seq, hw), lambda b, j, i: (b, j))],
        out_specs=pl.BlockSpec((tq, ow), lambda b, j, i: (b * nq + i, j)),
        out_shape=jax.ShapeDtypeStruct((t, MLA_HEADS * V_HEAD_DIM), BF16),
        scratch_shapes=[pltpu.VMEM((_HEADS_PER_STEP, nq, tq, tq), F32),
                        pltpu.VMEM((_HEADS_PER_STEP, tq, LANES), F32),
                        pltpu.VMEM((_HEADS_PER_STEP, tq, HEAD_PAD), F32)],
        compiler_params=_params(("arbitrary", "arbitrary", "arbitrary")),
        name="attn",
    )(q, k, v)


_GROUP_SIZE = N_EXPERTS // N_EXPERT_GROUPS
_BIG = 1.0e9


def _mix_out_kernel(attn_ref, yc_ref, x_ref, mod_ref, gattn_ref, wout_ref, gpost_ref, gpre2_ref,
                    wrt_ref, br_ref, x1_ref, h2_ref, h2p_ref, idx_ref, wts_ref, rank_ref, cnt_ref,
                    carry_scr):
    i = pl.program_id(0)
    tm = x_ref.shape[0]
    half = attn_ref.shape[1]

    @pl.when(i == 0)
    def _():
        carry_scr[...] = jnp.zeros(carry_scr.shape, F32)

    an = _rms(attn_ref[...].astype(F32), gattn_ref[...]).astype(BF16)
    mix = (jnp.dot(an, wout_ref[0:half, :], preferred_element_type=F32)
           + jnp.dot(yc_ref[...], wout_ref[half:, :], preferred_element_type=F32))
    g1 = mod_ref[0, 2:3, :]
    sh2 = mod_ref[0, 3:4, :]
    sc2 = mod_ref[0, 4:5, :]
    x1 = x_ref[...] + g1 * _rms(mix, gpost_ref[...])
    x1_ref[...] = x1
    h2 = _rms(x1, gpre2_ref[...]) * (1.0 + sc2) + sh2
    h2_ref[...] = h2.astype(BF16)
    words = _pack_row_words(h2[:, 0:_ROW_WORDS], h2[:, _ROW_WORDS:])
    h2p_ref[...] = words

    logits = lax.dot_general(wrt_ref[...], h2, (((1,), (1,)), ((), ())),
                             preferred_element_type=F32, precision=lax.Precision.HIGHEST)
    scores = jax.nn.sigmoid(logits)
    sel = scores + br_ref[...]
    row = lax.broadcasted_iota(I32, (N_EXPERTS, tm), 0).astype(F32)

    gscore = []
    rw = lax.broadcasted_iota(I32, (_GROUP_SIZE, tm), 0).astype(F32)
    for g in range(N_EXPERT_GROUPS):
        blk = sel[g * _GROUP_SIZE:(g + 1) * _GROUP_SIZE, :]
        m1 = jnp.max(blk, axis=0, keepdims=True)
        i1 = jnp.min(jnp.where(blk == m1, rw, _BIG), axis=0, keepdims=True)
        m2 = jnp.max(jnp.where(rw == i1, NEG_INF, blk), axis=0, keepdims=True)
        gscore.append(m1 + m2)

    gkeep = [jnp.zeros((1, tm), F32) for _ in range(N_EXPERT_GROUPS)]
    for _ in range(TOPK_GROUPS):
        mg = functools.reduce(jnp.maximum, gscore)
        ig = functools.reduce(jnp.minimum, [jnp.where(gscore[g] == mg, float(g), _BIG)
                                            for g in range(N_EXPERT_GROUPS)])
        for g in range(N_EXPERT_GROUPS):
            hit = ig == float(g)
            gkeep[g] = jnp.where(hit, 1.0, gkeep[g])
            gscore[g] = jnp.where(hit, NEG_INF, gscore[g])
    cur = jnp.concatenate(
        [jnp.where(gkeep[g] > 0.0, sel[g * _GROUP_SIZE:(g + 1) * _GROUP_SIZE, :], NEG_INF)
         for g in range(N_EXPERT_GROUPS)], axis=0)

    krow = lax.broadcasted_iota(I32, (TOP_K, tm), 0)
    idx_rows = []
    idx_f = jnp.zeros((TOP_K, tm), F32)
    sc_k = jnp.zeros((TOP_K, tm), F32)
    sc_sum = jnp.zeros((1, tm), F32)
    onehot = jnp.zeros((N_EXPERTS, tm), F32)
    for k in range(TOP_K):
        m = jnp.max(cur, axis=0, keepdims=True)
        ik = jnp.min(jnp.where(cur == m, row, _BIG), axis=0, keepdims=True)
        hit = row == ik
        sk = jnp.sum(jnp.where(hit, scores, 0.0), axis=0, keepdims=True)
        cur = jnp.where(hit, NEG_INF, cur)
        onehot = jnp.where(hit, 1.0, onehot)
        idx_rows.append(ik)
        idx_f = jnp.where(krow == k, ik, idx_f)
        sc_k = jnp.where(krow == k, sk, sc_k)
        sc_sum = sc_sum + sk
    wts_ref[...] = sc_k / sc_sum * ROUTED_SCALE
    idx_ref[...] = idx_f.astype(I32)

    tri = (lax.broadcasted_iota(I32, (tm, tm), 0) < lax.broadcasted_iota(I32, (tm, tm), 1))
    excl = jnp.dot(onehot.astype(BF16), tri.astype(BF16), preferred_element_type=F32)
    rank_e = carry_scr[:, 0:1] + excl
    rank_k = jnp.zeros((TOP_K, tm), F32)
    for k in range(TOP_K):
        hit = row == idx_rows[k]
        rk = jnp.sum(jnp.where(hit, rank_e, 0.0), axis=0, keepdims=True)
        rank_k = jnp.where(krow == k, rk, rank_k)
    rank_ref[...] = rank_k.astype(I32)
    carry_scr[...] = carry_scr[...] + jnp.sum(onehot, axis=1, keepdims=True)
    cnt_ref[...] = carry_scr[...]


def _mix_out(attn, yc, x2, mod3, gattn, wout, gpost, gpre2, wrt, br, seq):
    t, d = x2.shape
    tm = min(TM_OUT, seq)
    tpb = seq // tm
    full = lambda a: pl.BlockSpec(a.shape, lambda i: (0,) * a.ndim)
    row = lambda w: pl.BlockSpec((tm, w), lambda i: (i, 0))
    col = pl.BlockSpec((TOP_K, tm), lambda i: (0, i))
    return pl.pallas_call(
        _mix_out_kernel,
        grid=(t // tm,),
        in_specs=[row(attn.shape[1]), row(yc.shape[1]), row(d),
                  pl.BlockSpec((1, 6, d), lambda i: (i // tpb, 0, 0)),
                  full(gattn), full(wout), full(gpost), full(gpre2), full(wrt), full(br)],
        out_specs=[row(d), row(d), row(_ROW_WORDS), col, col, col,
                   pl.BlockSpec((N_EXPERTS, LANES), lambda i: (0, 0))],
        out_shape=[jax.ShapeDtypeStruct((t, d), F32), jax.ShapeDtypeStruct((t, d), BF16),
                   jax.ShapeDtypeStruct((t, _ROW_WORDS), U32),
                   jax.ShapeDtypeStruct((TOP_K, t), I32), jax.ShapeDtypeStruct((TOP_K, t), F32),
                   jax.ShapeDtypeStruct((TOP_K, t), I32),
                   jax.ShapeDtypeStruct((N_EXPERTS, LANES), F32)],
        scratch_shapes=[pltpu.VMEM((N_EXPERTS, LANES), F32)],
        compiler_params=_params(("arbitrary",)),
        name="mix_out",
    )(attn, yc, x2, mod3, gattn, wout, gpost, gpre2, wrt, br)


def _dest_kernel(idx_ref, rank_ref, pstart_ref, dest_ref):
    tm = idx_ref.shape[1]
    row = lax.broadcasted_iota(I32, (N_EXPERTS, tm), 0)
    krow = lax.broadcasted_iota(I32, (TOP_K, tm), 0)
    pstart = pstart_ref[...]
    idx = idx_ref[...]
    out = jnp.zeros((TOP_K, tm), F32)
    for k in range(TOP_K):
        hit = row == idx[k:k + 1, :]
        base = jnp.sum(jnp.where(hit, pstart, 0.0), axis=0, keepdims=True)
        out = jnp.where(krow == k, base, out)
    dest_ref[...] = out.astype(I32) + rank_ref[...]


def _dest(idx, rank, pstart):
    t = idx.shape[1]
    tm = min(TM_DEST, t)
    col = pl.BlockSpec((TOP_K, tm), lambda i: (0, i))
    return pl.pallas_call(
        _dest_kernel,
        grid=(t // tm,),
        in_specs=[col, col, pl.BlockSpec((N_EXPERTS, 1), lambda i: (0, 0))],
        out_specs=col,
        out_shape=jax.ShapeDtypeStruct((TOP_K, t), I32),
        compiler_params=_params(("arbitrary",)),
        name="dest",
    )(idx, rank, pstart)


_PAD_CHUNKS = tuple(BM_EXPERT >> s for s in range(1, BM_EXPERT.bit_length()))


SC_CORES = 2
SC_SUBCORES = 16
SC_ROWS = 128


def _sc_worker_split(n_chunks):
    workers = SC_CORES * SC_SUBCORES
    per_worker = max(1, n_chunks // workers)
    active = n_chunks // per_worker
    assert active * per_worker == n_chunks and active <= workers
    return per_worker, active


def _scatter_rows_sc(rows, idx2d, n_out):
    n, width = rows.shape
    n_chunks = n // SC_ROWS
    per_worker, active = _sc_worker_split(n_chunks)
    mesh = plsc.VectorSubcoreMesh(core_axis_name="c", subcore_axis_name="s")

    @functools.partial(
        pl.kernel, mesh=mesh, out_type=jax.ShapeDtypeStruct((n_out, width), rows.dtype),
        scratch_types=[pltpu.VMEM((1, SC_ROWS), I32), pltpu.VMEM((SC_ROWS, width), rows.dtype)])
    def scatter(rows_hbm, idx_hbm, out_hbm, idx_v, rows_v):
        wid = lax.axis_index("s") * SC_CORES + lax.axis_index("c")

        @pl.when(wid < active)
        def _():
            @pl.loop(0, per_worker)
            def _(c):
                chunk = wid * per_worker + c
                pltpu.sync_copy(rows_hbm.at[pl.ds(chunk * SC_ROWS, SC_ROWS)], rows_v)
                for k in range(TOP_K):
                    pltpu.sync_copy(idx_hbm.at[pl.ds(k * n_chunks + chunk, 1)], idx_v)
                    pltpu.sync_copy(rows_v, out_hbm.at[idx_v.at[0]])

    return scatter(rows, idx2d)


def _padfill_kernel(pad_from_ref, pad_n_ref, xs_in_hbm, xs_hbm, zero_scr, pad_sem):
    del xs_in_hbm
    zero_scr[...] = jnp.zeros(zero_scr.shape, zero_scr.dtype)

    def pad_copies(e, act):
        n = pad_n_ref[e]
        base = pad_from_ref[e]

        def single_rows(start, count):
            for j in range(SUBLANES - 1):
                @pl.when(j < count)
                def _():
                    act(pltpu.make_async_copy(zero_scr.at[pl.ds(0, 1), :],
                                              xs_hbm.at[pl.ds(start + j, 1), :], pad_sem))

        head = jnp.minimum(n, (SUBLANES - (base & (SUBLANES - 1))) & (SUBLANES - 1))
        single_rows(base, head)
        rest = n - head
        mid = base + head
        for rows in _PAD_CHUNKS:
            if rows >= SUBLANES:
                @pl.when((rest & rows) != 0)
                def _():
                    start = pl.multiple_of(mid + (rest & ~(2 * rows - 1)), SUBLANES)
                    act(pltpu.make_async_copy(zero_scr.at[pl.ds(0, rows), :],
                                              xs_hbm.at[pl.ds(start, rows), :], pad_sem))
        single_rows(mid + (rest & ~(SUBLANES - 1)), rest & (SUBLANES - 1))

    def issue_pad(e, carry):
        pad_copies(e, lambda cp: cp.start())
        return carry

    def drain_pad(e, carry):
        pad_copies(e, lambda cp: cp.wait())
        return carry

    lax.fori_loop(0, N_EXPERTS, issue_pad, 0)
    lax.fori_loop(0, N_EXPERTS, drain_pad, 0)


def _padfill(pad_from, pad_n, xs):
    return pl.pallas_call(
        _padfill_kernel,
        grid_spec=pltpu.PrefetchScalarGridSpec(
            num_scalar_prefetch=2,
            grid=(1,),
            in_specs=[pl.BlockSpec(memory_space=pl.ANY)],
            out_specs=pl.BlockSpec(memory_space=pl.ANY),
            scratch_shapes=[pltpu.VMEM((BM_EXPERT // 2, _ROW_WORDS), U32), pltpu.SemaphoreType.DMA]),
        out_shape=jax.ShapeDtypeStruct(xs.shape, xs.dtype),
        input_output_aliases={2: 0},
        compiler_params=_params(("arbitrary",)),
        name="padfill",
    )(pad_from, pad_n, xs)


_XS_SLOTS = 4
_YS_SLOTS = 2
_W_SLOTS = 2


def _expert_kernel(first_ref, ord_ref, uexp_ref, meta_ref, xs_hbm, wg_hbm, wu_hbm, wd_hbm, ys_hbm,
                   xs_buf, ys_buf, wg_buf, wu_buf, wd_buf, wgu_scr, wd_scr, x_scr, xs_sem, ys_sem, w_sem):
    i = pl.program_id(0)
    nused = meta_ref[0]
    nexp = meta_ref[1]
    bm = xs_buf.shape[1]

    def xs_copy(b, slot):
        return pltpu.make_async_copy(xs_hbm.at[pl.ds(b * bm, bm), :], xs_buf.at[slot], xs_sem.at[slot])

    def ys_copy(b, slot):
        return pltpu.make_async_copy(ys_buf.at[slot], ys_hbm.at[pl.ds(b * bm, bm), :], ys_sem.at[slot])

    def w_copies(j, slot):
        e = uexp_ref[j]
        return (pltpu.make_async_copy(wg_hbm.at[e], wg_buf.at[slot], w_sem.at[slot, 0]),
                pltpu.make_async_copy(wu_hbm.at[e], wu_buf.at[slot], w_sem.at[slot, 1]),
                pltpu.make_async_copy(wd_hbm.at[e], wd_buf.at[slot], w_sem.at[slot, 2]))

    @pl.when(i == 0)
    def _():
        for s in range(_XS_SLOTS - 1):
            @pl.when(s < nused)
            def _():
                xs_copy(s, s).start()
        for s in range(_W_SLOTS):
            @pl.when(s < nexp)
            def _():
                for cp in w_copies(s, s):
                    cp.start()

    @pl.when(i < nused)
    def _():
        ahead = i + _XS_SLOTS - 1

        @pl.when(ahead < nused)
        def _():
            xs_copy(ahead, ahead % _XS_SLOTS).start()

        @pl.when(first_ref[i] == 1)
        def _():
            j = ord_ref[i]
            ws = j % _W_SLOTS
            for cp in w_copies(j, ws):
                cp.wait()
            wgu_scr[:, 0:EXPERT_DIM] = wg_buf[ws].astype(BF16)
            wgu_scr[:, EXPERT_DIM:2 * EXPERT_DIM] = wu_buf[ws].astype(BF16)
            wd_scr[...] = wd_buf[ws].astype(BF16)

            @pl.when(j + _W_SLOTS < nexp)
            def _():
                for cp in w_copies(j + _W_SLOTS, ws):
                    cp.start()

        slot = i % _XS_SLOTS
        xs_copy(i, slot).wait()
        lo, hi = _unpack_row_words(xs_buf[slot])
        x_scr[:, 0:_ROW_WORDS] = lo.astype(BF16)
        x_scr[:, _ROW_WORDS:] = hi.astype(BF16)
        gu = jnp.dot(x_scr[...], wgu_scr[...], preferred_element_type=F32)
        g = gu[:, 0:EXPERT_DIM]
        a = (g * jax.nn.sigmoid(g) * gu[:, EXPERT_DIM:2 * EXPERT_DIM]).astype(BF16)
        y = jnp.dot(a, wd_scr[...], preferred_element_type=F32)
        oslot = i % _YS_SLOTS

        @pl.when(i >= _YS_SLOTS)
        def _():
            ys_copy(i - _YS_SLOTS, oslot).wait()

        ys_buf[oslot] = _pack_row_words(y[:, 0:_ROW_WORDS], y[:, _ROW_WORDS:])
        ys_copy(i, oslot).start()

        @pl.when(i == nused - 1)
        def _():
            ys_copy(i, oslot).wait()

            @pl.when(i >= 1)
            def _():
                ys_copy(i - 1, (i - 1) % _YS_SLOTS).wait()


def _experts(first, ordinal, uexp, meta, xs, w_gate, w_up, w_down):
    p = xs.shape[0]
    d = w_gate.shape[1]
    nb = p // BM_EXPERT
    anyspec = pl.BlockSpec(memory_space=pl.ANY)
    return pl.pallas_call(
        _expert_kernel,
        grid_spec=pltpu.PrefetchScalarGridSpec(
            num_scalar_prefetch=4,
            grid=(nb,),
            in_specs=[anyspec, anyspec, anyspec, anyspec],
            out_specs=anyspec,
            scratch_shapes=[pltpu.VMEM((_XS_SLOTS, BM_EXPERT, _ROW_WORDS), U32),
                            pltpu.VMEM((_YS_SLOTS, BM_EXPERT, _ROW_WORDS), U32),
                            pltpu.VMEM((_W_SLOTS, d, EXPERT_DIM), F32),
                            pltpu.VMEM((_W_SLOTS, d, EXPERT_DIM), F32),
                            pltpu.VMEM((_W_SLOTS, EXPERT_DIM, d), F32),
                            pltpu.VMEM((d, 2 * EXPERT_DIM), BF16), pltpu.VMEM((EXPERT_DIM, d), BF16),
                            pltpu.VMEM((BM_EXPERT, d), BF16),
                            pltpu.SemaphoreType.DMA((_XS_SLOTS,)), pltpu.SemaphoreType.DMA((_YS_SLOTS,)),
                            pltpu.SemaphoreType.DMA((_W_SLOTS, 3))]),
        out_shape=jax.ShapeDtypeStruct((p, _ROW_WORDS), U32),
        compiler_params=_params(("arbitrary",)),
        name="experts",
    )(first, ordinal, uexp, meta, xs, w_gate, w_up, w_down)


def _gather_rows_sc(table, idx):
    n = idx.shape[0]
    width = table.shape[1]
    per_worker, active = _sc_worker_split(n // SC_ROWS)
    mesh = plsc.VectorSubcoreMesh(core_axis_name="c", subcore_axis_name="s")

    @functools.partial(
        pl.kernel, mesh=mesh, out_type=jax.ShapeDtypeStruct((n, width), table.dtype),
        scratch_types=[pltpu.VMEM((SC_ROWS,), I32), pltpu.VMEM((SC_ROWS, width), table.dtype),
                       pltpu.SemaphoreType.DMA])
    def gather(table_hbm, idx_hbm, out_hbm, idx_v, rows_v, sem):
        wid = lax.axis_index("s") * SC_CORES + lax.axis_index("c")

        @pl.when(wid < active)
        def _():
            @pl.loop(0, per_worker)
            def _(c):
                off = (wid * per_worker + c) * SC_ROWS
                pltpu.sync_copy(idx_hbm.at[pl.ds(off, SC_ROWS)], idx_v)
                pltpu.async_copy(table_hbm.at[idx_v], rows_v, sem).wait()
                pltpu.sync_copy(rows_v, out_hbm.at[pl.ds(off, SC_ROWS)])

    return gather(table, idx)


def _combine_kernel(wts_ref, yg_ref, h2_ref, x1_ref, mod_ref, wsg_ref, wsu_ref, wsd_ref, gpost_ref, o_ref):
    h2 = h2_ref[...]
    g = jnp.dot(h2, wsg_ref[...], preferred_element_type=F32)
    u = jnp.dot(h2, wsu_ref[...], preferred_element_type=F32)
    f = jnp.dot((g * jax.nn.sigmoid(g) * u).astype(BF16), wsd_ref[...], preferred_element_type=F32)

    wts = wts_ref[...]
    los = [f[:, sl * LANES:(sl + 1) * LANES] for sl in range(_ROW_SLABS)]
    his = [f[:, _ROW_WORDS + sl * LANES:_ROW_WORDS + (sl + 1) * LANES] for sl in range(_ROW_SLABS)]
    for k in range(TOP_K):
        wk = wts[:, k:k + 1]
        for sl in range(_ROW_SLABS):
            lo, hi = _unpack_row_words(yg_ref[k, :, sl * LANES:(sl + 1) * LANES])
            los[sl] = los[sl] + wk * lo
            his[sl] = his[sl] + wk * hi
    f = jnp.concatenate(los + his, axis=1)
    g2 = mod_ref[0, 5:6, :]
    o_ref[...] = x1_ref[...] + g2 * _rms(f, gpost_ref[...])


def _combine(wts_t, yg, h2, x1, mod3, wsg, wsu, wsd, gpost, seq):
    t, d = x1.shape
    tm = min(TM_COMBINE, seq)
    tpb = seq // tm
    full = lambda a: pl.BlockSpec(a.shape, lambda i: (0,) * a.ndim)
    row = lambda w: pl.BlockSpec((tm, w), lambda i: (i, 0))
    return pl.pallas_call(
        _combine_kernel,
        grid=(t // tm,),
        in_specs=[row(TOP_K), pl.BlockSpec((TOP_K, tm, _ROW_WORDS), lambda i: (0, i, 0)), row(d), row(d),
                  pl.BlockSpec((1, 6, d), lambda i: (i // tpb, 0, 0)),
                  full(wsg), full(wsu), full(wsd), full(gpost)],
        out_specs=row(d),
        out_shape=jax.ShapeDtypeStruct((t, d), F32),
        compiler_params=_params(("arbitrary",)),
        name="combine",
    )(wts_t, yg, h2, x1, mod3, wsg, wsu, wsd, gpost)


def _pack_weights(w_in, w_uq, w_ukv):
    d = w_in.shape[0]
    half = QK_ROPE_DIM // 2
    z = lambda n, c: jnp.zeros((n, c), F32)
    o = Q_LORA_RANK + KV_LORA_RANK
    kr = w_in[:, o:o + QK_ROPE_DIM]
    kr_grp = jnp.concatenate([z(d, QK_NOPE_DIM), kr, z(d, HEAD_PAD - QK_NOPE_DIM - QK_ROPE_DIM)], axis=1)
    kr_rot = jnp.concatenate([z(d, QK_NOPE_DIM), -kr[:, half:], kr[:, :half],
                              z(d, HEAD_PAD - QK_NOPE_DIM - QK_ROPE_DIM)], axis=1)
    win_p = jnp.concatenate([w_in[:, :o], kr_grp, kr_rot, w_in[:, o + QK_ROPE_DIM:]], axis=1)

    scale = float(QK_NOPE_DIM + QK_ROPE_DIM) ** -0.5 * float(np.log2(np.e))
    r = Q_LORA_RANK
    qd = QK_NOPE_DIM + QK_ROPE_DIM
    q_grp, q_rot = [], []
    for h in range(MLA_HEADS):
        nope = w_uq[:, h * qd:h * qd + QK_NOPE_DIM]
        rope = w_uq[:, h * qd + QK_NOPE_DIM:(h + 1) * qd]
        pad = z(r, HEAD_PAD - qd)
        q_grp.append(jnp.concatenate([nope, rope, pad], axis=1))
        q_rot.append(jnp.concatenate([z(r, QK_NOPE_DIM), -rope[:, half:], rope[:, :half], pad], axis=1))
    wuq_p = jnp.concatenate(q_grp + q_rot, axis=1) * scale

    c = KV_LORA_RANK
    kd = QK_NOPE_DIM + V_HEAD_DIM
    k_grp, v_grp = [], []
    for h in range(MLA_HEADS):
        k_grp.append(jnp.concatenate([w_ukv[:, h * kd:h * kd + QK_NOPE_DIM], z(c, HEAD_PAD - QK_NOPE_DIM)], axis=1))
        v_grp.append(jnp.concatenate([w_ukv[:, h * kd + QK_NOPE_DIM:(h + 1) * kd], z(c, HEAD_PAD - V_HEAD_DIM)], axis=1))
    wukv_p = jnp.concatenate(k_grp + v_grp, axis=1)
    return win_p.astype(BF16), wuq_p.astype(BF16), wukv_p.astype(BF16)


def _rope_tables(positions):
    inv = 1.0 / (ROPE_THETA ** (jnp.arange(0, QK_ROPE_DIM, 2, dtype=F32) / QK_ROPE_DIM))
    ang = positions.astype(F32).reshape(-1)[:, None] * inv
    t = ang.shape[0]
    cos, sin = jnp.cos(ang), jnp.sin(ang)
    tail = HEAD_PAD - QK_NOPE_DIM - QK_ROPE_DIM
    cos_t = jnp.concatenate([jnp.ones((t, QK_NOPE_DIM), F32), cos, cos, jnp.ones((t, tail), F32)], axis=1)
    sin_t = jnp.concatenate([jnp.zeros((t, QK_NOPE_DIM), F32), sin, sin, jnp.zeros((t, tail), F32)], axis=1)
    return cos_t, sin_t


def _layer(x2, c, cos_t, sin_t, batch, seq, w_ada, b_ada, g_pre_mix, w_in, g_q_lat, w_uq, g_kv_lat, w_ukv,
           w_conv, g_attn_out, g_conv_out, w_out, g_post_mix, g_pre_ffn, w_router, b_router,
           w_gate, w_up, w_down, w_sh_gate, w_sh_up, w_sh_down, g_post_ffn):
    t, d = x2.shape
    r1 = lambda a: a.reshape(1, -1)

    c_pad = jnp.zeros((SUBLANES, d), F32).at[:batch].set(c)
    mod = _ada(c_pad, w_ada, r1(b_ada))[:batch]
    mod3 = mod.reshape(batch, 6, d)

    win_p, wuq_p, wukv_p = _pack_weights(w_in, w_uq, w_ukv)
    vone = jnp.zeros((1, HEAD_PAD), F32).at[0, V_HEAD_DIM].set(1.0)
    q, k, v, yc = _mix_in(x2, mod3, r1(g_pre_mix), win_p, r1(g_q_lat), wuq_p, r1(g_kv_lat), wukv_p,
                          vone, w_conv, r1(g_conv_out), cos_t, sin_t, seq)
    attn = _attention(q, k, v, batch, seq)
    x1, h2, h2p, idx, wts, rank, cnt = _mix_out(
        attn, yc, x2, mod3, r1(g_attn_out), w_out.astype(BF16), r1(g_post_mix), r1(g_pre_ffn),
        w_router.T, b_router.reshape(-1, 1), seq)

    counts = cnt[:, 0].astype(I32)
    padded = ((counts + BM_EXPERT - 1) // BM_EXPERT) * BM_EXPERT
    pad_end = jnp.cumsum(padded)
    pad_start = pad_end - padded
    m = t * TOP_K
    nb = (m + N_EXPERTS * (BM_EXPERT - 1)) // BM_EXPERT
    nused = pad_end[-1] // BM_EXPERT
    bidx = jnp.arange(nb, dtype=I32)
    blk_exp = jnp.sum((pad_end[None, :] <= (bidx * BM_EXPERT)[:, None]).astype(I32), axis=1)
    first = ((bidx < nused) & ((bidx == 0) | (blk_exp != jnp.roll(blk_exp, 1)))).astype(I32)
    ordinal = jnp.maximum(jnp.cumsum(first) - 1, 0).astype(I32)
    seen = jnp.cumsum((counts > 0).astype(I32))
    uexp = jnp.minimum(jnp.sum((seen[None, :] <= jnp.arange(N_EXPERTS, dtype=I32)[:, None]).astype(I32), axis=1),
                       N_EXPERTS - 1).astype(I32)
    meta = jnp.stack([nused, seen[-1]]).astype(I32)

    dest = _dest(idx, rank, pad_start.astype(F32).reshape(-1, 1))
    xs = _scatter_rows_sc(h2p, dest.reshape(-1, SC_ROWS), nb * BM_EXPERT)
    xs = _padfill((pad_start + counts).astype(I32), (padded - counts).astype(I32), xs)
    ys = _experts(first, ordinal, uexp, meta, xs, w_gate, w_up, w_down)
    yg = _gather_rows_sc(ys, dest.reshape(-1)).reshape(TOP_K, t, _ROW_WORDS)
    return _combine(wts.T, yg, h2, x1, mod3, w_sh_gate.astype(BF16), w_sh_up.astype(BF16),
                    w_sh_down.astype(BF16), r1(g_post_ffn), seq)


def kernel(x, c, positions, w_ada, b_ada, g_pre_mix, w_in, g_q_lat, w_uq, g_kv_lat, w_ukv, w_conv, g_attn_out, g_conv_out, w_out, g_post_mix, g_pre_ffn, w_router, b_router, w_gate, w_up, w_down, w_sh_gate, w_sh_up, w_sh_down, g_post_ffn):
    batch, seq, d = x.shape
    cos_t, sin_t = _rope_tables(positions)
    x2 = x.reshape(batch * seq, d)
    for l in range(w_ada.shape[0]):
        x2 = _layer(x2, c, cos_t, sin_t, batch, seq, w_ada[l], b_ada[l], g_pre_mix[l], w_in[l], g_q_lat[l],
                    w_uq[l], g_kv_lat[l], w_ukv[l], w_conv[l], g_attn_out[l], g_conv_out[l], w_out[l],
                    g_post_mix[l], g_pre_ffn[l], w_router[l], b_router[l], w_gate[l], w_up[l], w_down[l],
                    w_sh_gate[l], w_sh_up[l], w_sh_down[l], g_post_ffn[l])
    return x2.reshape(batch, seq, d)
```

```python
import functools

import jax
import jax.numpy as jnp
import numpy as np
from jax import lax
from jax.experimental import pallas as pl
from jax.experimental.pallas import tpu as pltpu
from jax.experimental.pallas import tpu_sc as plsc

F32 = jnp.float32
BF16 = jnp.bfloat16
I32 = jnp.int32
U32 = jnp.uint32

CHUNK = 64
MLA_HEADS = 8
QK_NOPE_DIM = 64
QK_ROPE_DIM = 32
V_HEAD_DIM = 64
Q_LORA_RANK = 384
KV_LORA_RANK = 256
ROPE_THETA = 10000.0
CONV_WIDTH = 3
N_EXPERTS = 256
TOP_K = 8
N_EXPERT_GROUPS = 8
TOPK_GROUPS = 4
EXPERT_DIM = 256
ROUTED_SCALE = 2.5
EPS = 1e-6

LANES = 128
SUBLANES = 8
HEAD_PAD = LANES
VMEM_LIMIT_BYTES = 56 * 1024 * 1024

TM_IN = 512
TQ_ATTN = 512
TM_OUT = 512
TM_DEST = 512
TM_DISPATCH = 512
BM_EXPERT = 256
TM_COMBINE = 256

NEG_INF = float("-inf")


def _rms(x, g):
    return x * lax.rsqrt(jnp.mean(x * x, axis=-1, keepdims=True) + EPS) * g


_HI_MASK = np.uint32(0xFFFF0000)
_ROW_WORDS = 512
_ROW_SLABS = _ROW_WORDS // LANES


def _pack_row_words(lo, hi):
    lo_w = lax.bitcast_convert_type(lo.astype(BF16).astype(F32), U32) >> 16
    hi_w = lax.bitcast_convert_type(hi.astype(BF16).astype(F32), U32) & _HI_MASK
    return lo_w | hi_w


def _unpack_row_words(w):
    return (lax.bitcast_convert_type(w << 16, F32), lax.bitcast_convert_type(w & _HI_MASK, F32))


def _params(sem):
    return pltpu.CompilerParams(dimension_semantics=sem, vmem_limit_bytes=VMEM_LIMIT_BYTES)


def _ada_kernel(c_ref, w_ref, b_ref, o_ref):
    c = c_ref[...]
    s = c * jax.nn.sigmoid(c)
    o_ref[...] = jnp.dot(s, w_ref[...], preferred_element_type=F32,
                         precision=lax.Precision.HIGHEST) + b_ref[...]


def _ada(c_pad, w, b):
    rows, d = c_pad.shape
    n = w.shape[1]
    tn = 1536
    return pl.pallas_call(
        _ada_kernel,
        grid=(n // tn,),
        in_specs=[pl.BlockSpec((rows, d), lambda j: (0, 0)),
                  pl.BlockSpec((d, tn), lambda j: (0, j)),
                  pl.BlockSpec((1, tn), lambda j: (0, j))],
        out_specs=pl.BlockSpec((rows, tn), lambda j: (0, j)),
        out_shape=jax.ShapeDtypeStruct((rows, n), F32),
        compiler_params=_params(("arbitrary",)),
        name="ada",
    )(c_pad, w, b)


_CQ0, _CQ1 = 0, Q_LORA_RANK
_CKV0, _CKV1 = _CQ1, _CQ1 + KV_LORA_RANK
_KR0, _KR1 = _CKV1, _CKV1 + 2 * HEAD_PAD
_CONV_DIM = 512
_GB0 = _KR1
_GC0 = _GB0 + _CONV_DIM
_XV0 = _GC0 + _CONV_DIM
_WIN_COLS = _XV0 + _CONV_DIM
_QW = MLA_HEADS * HEAD_PAD


def _mix_in_kernel(tiles_per_batch, x_ref, mod_ref, gpre_ref, win_ref, gq_ref, wuq_ref, gkv_ref,
                   wukv_ref, vone_ref, wconv_ref, gconv_ref, cos_ref, sin_ref,
                   q_ref, k_ref, v_ref, yc_ref, h_scr, u_scr):
    i = pl.program_id(0)
    tm = x_ref.shape[0]
    sh1 = mod_ref[0, 0:1, :]
    sc1 = mod_ref[0, 1:2, :]
    h = _rms(x_ref[...], gpre_ref[...]) * (1.0 + sc1) + sh1
    h_scr[...] = h.astype(BF16)
    cos = cos_ref[...]
    sin = sin_ref[...]

    cq = jnp.dot(h_scr[...], win_ref[:, _CQ0:_CQ1], preferred_element_type=F32)
    cqn = _rms(cq, gq_ref[...]).astype(BF16)
    qq = jnp.dot(cqn, wuq_ref[...], preferred_element_type=F32)
    for hd in range(MLA_HEADS):
        lo = hd * HEAD_PAD
        qh = qq[:, lo:lo + HEAD_PAD] * cos + qq[:, _QW + lo:_QW + lo + HEAD_PAD] * sin
        q_ref[:, lo:lo + HEAD_PAD] = qh.astype(BF16)

    ckv = jnp.dot(h_scr[...], win_ref[:, _CKV0:_CKV1], preferred_element_type=F32)
    ckvn = _rms(ckv, gkv_ref[...]).astype(BF16)
    kv = jnp.dot(ckvn, wukv_ref[...], preferred_element_type=F32)
    krr = jnp.dot(h_scr[...], win_ref[:, _KR0:_KR1], preferred_element_type=F32)
    kr = krr[:, 0:HEAD_PAD] * cos + krr[:, HEAD_PAD:2 * HEAD_PAD] * sin
    vone = vone_ref[...]
    for hd in range(MLA_HEADS):
        lo = hd * HEAD_PAD
        k_ref[:, lo:lo + HEAD_PAD] = (kv[:, lo:lo + HEAD_PAD] + kr).astype(BF16)
        v_ref[:, lo:lo + HEAD_PAD] = (kv[:, _QW + lo:_QW + lo + HEAD_PAD] + vone).astype(BF16)

    gb = jnp.dot(h_scr[...], win_ref[:, _GB0:_GC0], preferred_element_type=F32)
    gc = jnp.dot(h_scr[...], win_ref[:, _GC0:_XV0], preferred_element_type=F32)
    xv = jnp.dot(h_scr[...], win_ref[:, _XV0:_WIN_COLS], preferred_element_type=F32)
    u = gc * xv
    prev = u_scr[tm:tm + SUBLANES, :]
    first = (i % tiles_per_batch) == 0
    u_scr[0:SUBLANES, :] = jnp.where(first, jnp.zeros_like(prev), prev)
    u_scr[SUBLANES:tm + SUBLANES, :] = u
    um1 = u_scr[SUBLANES - 1:tm + SUBLANES - 1, :]
    um2 = u_scr[SUBLANES - 2:tm + SUBLANES - 2, :]
    conv = wconv_ref[0:1, :] * um2 + wconv_ref[1:2, :] * um1 + wconv_ref[2:3, :] * u
    yc_ref[...] = _rms(gb * conv, gconv_ref[...]).astype(BF16)


def _mix_in(x2, mod3, gpre, win_p, gq, wuq_p, gkv, wukv_p, vone, wconv, gconv, cos_t, sin_t, seq):
    t, d = x2.shape
    tm = min(TM_IN, seq)
    tpb = seq // tm
    full = lambda a: pl.BlockSpec(a.shape, lambda i: (0,) * a.ndim)
    row = lambda w: pl.BlockSpec((tm, w), lambda i: (i, 0))
    return pl.pallas_call(
        functools.partial(_mix_in_kernel, tpb),
        grid=(t // tm,),
        in_specs=[row(d),
                  pl.BlockSpec((1, 6, d), lambda i: (i // tpb, 0, 0)),
                  full(gpre), full(win_p), full(gq), full(wuq_p), full(gkv), full(wukv_p),
                  full(vone), full(wconv), full(gconv), row(HEAD_PAD), row(HEAD_PAD)],
        out_specs=[row(_QW), row(_QW), row(_QW), row(_CONV_DIM)],
        out_shape=[jax.ShapeDtypeStruct((t, _QW), BF16), jax.ShapeDtypeStruct((t, _QW), BF16),
                   jax.ShapeDtypeStruct((t, _QW), BF16), jax.ShapeDtypeStruct((t, _CONV_DIM), BF16)],
        scratch_shapes=[pltpu.VMEM((tm, d), BF16), pltpu.VMEM((tm + SUBLANES, _CONV_DIM), F32)],
        compiler_params=_params(("arbitrary",)),
        name="mix_in",
    )(x2, mod3, gpre, win_p, gq, wuq_p, gkv, wukv_p, vone, wconv, gconv, cos_t, sin_t)


_HEADS_PER_STEP = 2


def _attn_kernel(tq, q_ref, k_ref, v_ref, o_ref, s_scr, mrun_scr, mb_scr, acc_scr):
    tk = tq
    nq = q_ref.shape[0] // tq
    lane_groups = tk // LANES
    heads = range(_HEADS_PER_STEP)
    lanes = [slice(hh * HEAD_PAD, (hh + 1) * HEAD_PAD) for hh in heads]

    def tile_max(s):
        m = s[:, 0:LANES]
        for g in range(1, lane_groups):
            m = jnp.maximum(m, s[:, g * LANES:(g + 1) * LANES])
        return m

    def scores(hh, qi, kv):
        off = pl.multiple_of(kv * tk, tk)
        return lax.dot_general(q_ref[qi * tq:(qi + 1) * tq, lanes[hh]], k_ref[pl.ds(off, tk), lanes[hh]],
                               (((1,), (1,)), ((), ())), preferred_element_type=F32)

    rc = lax.broadcasted_iota(I32, (tq, tk), 0) // CHUNK
    cc = lax.broadcasted_iota(I32, (tq, tk), 1) // CHUNK

    def diagonal(qi):
        for hh in heads:
            s = jnp.where(cc <= rc, scores(hh, qi, qi), NEG_INF)
            s_scr[hh, qi] = s
            m_row = jnp.max(jnp.maximum(mrun_scr[hh], tile_max(s)), axis=1, keepdims=True)
            mb_scr[hh] = jnp.broadcast_to(m_row, (tq, LANES))

    mrun_scr[...] = jnp.full(mrun_scr.shape, NEG_INF, F32)
    diagonal(0)
    for qi in range(nq):
        has_next = qi + 1 < nq
        acc_scr[...] = jnp.zeros(acc_scr.shape, F32)
        if has_next:
            mrun_scr[...] = jnp.full(mrun_scr.shape, NEG_INF, F32)

        def body(kv, carry, qi=qi, has_next=has_next):
            off = pl.multiple_of(kv * tk, tk)
            for hh in heads:
                mb = mb_scr[hh]
                p = jnp.concatenate(
                    [jnp.exp2(s_scr[hh, kv, :, g * LANES:(g + 1) * LANES] - mb) for g in range(lane_groups)],
                    axis=1).astype(BF16)
                acc_scr[hh] += jnp.dot(p, v_ref[pl.ds(off, tk), lanes[hh]], preferred_element_type=F32)
                if has_next:
                    s = scores(hh, qi + 1, kv)
                    s_scr[hh, kv] = s
                    mrun_scr[hh] = jnp.maximum(mrun_scr[hh], tile_max(s))
            return carry

        lax.fori_loop(0, qi + 1, body, 0)
        for hh in heads:
            acc = acc_scr[hh]
            o = acc[:, 0:V_HEAD_DIM] / acc[:, V_HEAD_DIM:V_HEAD_DIM + 1]
            o_ref[qi * tq:(qi + 1) * tq, hh * V_HEAD_DIM:(hh + 1) * V_HEAD_DIM] = o.astype(BF16)
        if has_next:
            diagonal(qi + 1)


def _attention(q, k, v, batch, seq):
    t = q.shape[0]
    tq = min(TQ_ATTN, seq)
    nq = seq // tq
    hw = _HEADS_PER_STEP * HEAD_PAD
    ow = _HEADS_PER_STEP * V_HEAD_DIM
    blk = lambda w: pl.BlockSpec((seq, w), lambda b, j: (b, j))
    return pl.pallas_call(
        functools.partial(_attn_kernel, tq),
        grid=(batch, MLA_HEADS // _HEADS_PER_STEP),
        in_specs=[blk(hw), blk(hw), blk(hw)],
        out_specs=blk(ow),
        out_shape=jax.ShapeDtypeStruct((t, MLA_HEADS * V_HEAD_DIM), BF16),
        scratch_shapes=[pltpu.VMEM((_HEADS_PER_STEP, nq, tq, tq), F32),
                        pltpu.VMEM((_HEADS_PER_STEP, tq, LANES), F32),
                        pltpu.VMEM((_HEADS_PER_STEP, tq, LANES), F32),
                        pltpu.VMEM((_HEADS_PER_STEP, tq, HEAD_PAD), F32)],
        compiler_params=_params(("arbitrary", "arbitrary")),
        name="attn",
    )(q, k, v)


_GROUP_SIZE = N_EXPERTS // N_EXPERT_GROUPS
_BIG = 1.0e9


def _mix_out_kernel(attn_ref, yc_ref, x_ref, mod_ref, gattn_ref, wout_ref, gpost_ref, gpre2_ref,
                    wrt_ref, br_ref, x1_ref, h2_ref, h2p_ref, idx_ref, wts_ref, rank_ref, cnt_ref,
                    carry_scr):
    i = pl.program_id(0)
    tm = x_ref.shape[0]
    half = attn_ref.shape[1]

    @pl.when(i == 0)
    def _():
        carry_scr[...] = jnp.zeros(carry_scr.shape, F32)

    an = _rms(attn_ref[...].astype(F32), gattn_ref[...]).astype(BF16)
    mix = (jnp.dot(an, wout_ref[0:half, :], preferred_element_type=F32)
           + jnp.dot(yc_ref[...], wout_ref[half:, :], preferred_element_type=F32))
    g1 = mod_ref[0, 2:3, :]
    sh2 = mod_ref[0, 3:4, :]
    sc2 = mod_ref[0, 4:5, :]
    x1 = x_ref[...] + g1 * _rms(mix, gpost_ref[...])
    x1_ref[...] = x1
    h2 = _rms(x1, gpre2_ref[...]) * (1.0 + sc2) + sh2
    h2_ref[...] = h2.astype(BF16)
    words = _pack_row_words(h2[:, 0:_ROW_WORDS], h2[:, _ROW_WORDS:])
    h2p_ref[...] = words

    logits = lax.dot_general(wrt_ref[...], h2, (((1,), (1,)), ((), ())),
                             preferred_element_type=F32, precision=lax.Precision.HIGHEST)
    scores = jax.nn.sigmoid(logits)
    sel = scores + br_ref[...]
    row = lax.broadcasted_iota(I32, (N_EXPERTS, tm), 0).astype(F32)

    gscore = []
    rw = lax.broadcasted_iota(I32, (_GROUP_SIZE, tm), 0).astype(F32)
    for g in range(N_EXPERT_GROUPS):
        blk = sel[g * _GROUP_SIZE:(g + 1) * _GROUP_SIZE, :]
        m1 = jnp.max(blk, axis=0, keepdims=True)
        i1 = jnp.min(jnp.where(blk == m1, rw, _BIG), axis=0, keepdims=True)
        m2 = jnp.max(jnp.where(rw == i1, NEG_INF, blk), axis=0, keepdims=True)
        gscore.append(m1 + m2)

    gkeep = [jnp.zeros((1, tm), F32) for _ in range(N_EXPERT_GROUPS)]
    for _ in range(TOPK_GROUPS):
        mg = functools.reduce(jnp.maximum, gscore)
        ig = functools.reduce(jnp.minimum, [jnp.where(gscore[g] == mg, float(g), _BIG)
                                            for g in range(N_EXPERT_GROUPS)])
        for g in range(N_EXPERT_GROUPS):
            hit = ig == float(g)
            gkeep[g] = jnp.where(hit, 1.0, gkeep[g])
            gscore[g] = jnp.where(hit, NEG_INF, gscore[g])
    cur = jnp.concatenate(
        [jnp.where(gkeep[g] > 0.0, sel[g * _GROUP_SIZE:(g + 1) * _GROUP_SIZE, :], NEG_INF)
         for g in range(N_EXPERT_GROUPS)], axis=0)

    krow = lax.broadcasted_iota(I32, (TOP_K, tm), 0)
    idx_rows = []
    idx_f = jnp.zeros((TOP_K, tm), F32)
    sc_k = jnp.zeros((TOP_K, tm), F32)
    sc_sum = jnp.zeros((1, tm), F32)
    onehot = jnp.zeros((N_EXPERTS, tm), F32)
    for k in range(TOP_K):
        m = jnp.max(cur, axis=0, keepdims=True)
        ik = jnp.min(jnp.where(cur == m, row, _BIG), axis=0, keepdims=True)
        hit = row == ik
        sk = jnp.sum(jnp.where(hit, scores, 0.0), axis=0, keepdims=True)
        cur = jnp.where(hit, NEG_INF, cur)
        onehot = jnp.where(hit, 1.0, onehot)
        idx_rows.append(ik)
        idx_f = jnp.where(krow == k, ik, idx_f)
        sc_k = jnp.where(krow == k, sk, sc_k)
        sc_sum = sc_sum + sk
    wts_ref[...] = sc_k / sc_sum * ROUTED_SCALE
    idx_ref[...] = idx_f.astype(I32)

    tri = (lax.broadcasted_iota(I32, (tm, tm), 0) < lax.broadcasted_iota(I32, (tm, tm), 1))
    excl = jnp.dot(onehot.astype(BF16), tri.astype(BF16), preferred_element_type=F32)
    rank_e = carry_scr[:, 0:1] + excl
    rank_k = jnp.zeros((TOP_K, tm), F32)
    for k in range(TOP_K):
        hit = row == idx_rows[k]
        rk = jnp.sum(jnp.where(hit, rank_e, 0.0), axis=0, keepdims=True)
        rank_k = jnp.where(krow == k, rk, rank_k)
    rank_ref[...] = rank_k.astype(I32)
    carry_scr[...] = carry_scr[...] + jnp.sum(onehot, axis=1, keepdims=True)
    cnt_ref[...] = carry_scr[...]


def _mix_out(attn, yc, x2, mod3, gattn, wout, gpost, gpre2, wrt, br, seq):
    t, d = x2.shape
    tm = min(TM_OUT, seq)
    tpb = seq // tm
    full = lambda a: pl.BlockSpec(a.shape, lambda i: (0,) * a.ndim)
    row = lambda w: pl.BlockSpec((tm, w), lambda i: (i, 0))
    col = pl.BlockSpec((TOP_K, tm), lambda i: (0, i))
    return pl.pallas_call(
        _mix_out_kernel,
        grid=(t // tm,),
        in_specs=[row(attn.shape[1]), row(yc.shape[1]), row(d),
                  pl.BlockSpec((1, 6, d), lambda i: (i // tpb, 0, 0)),
                  full(gattn), full(wout), full(gpost), full(gpre2), full(wrt), full(br)],
        out_specs=[row(d), row(d), row(_ROW_WORDS), col, col, col,
                   pl.BlockSpec((N_EXPERTS, LANES), lambda i: (0, 0))],
        out_shape=[jax.ShapeDtypeStruct((t, d), F32), jax.ShapeDtypeStruct((t, d), BF16),
                   jax.ShapeDtypeStruct((t, _ROW_WORDS), U32),
                   jax.ShapeDtypeStruct((TOP_K, t), I32), jax.ShapeDtypeStruct((TOP_K, t), F32),
                   jax.ShapeDtypeStruct((TOP_K, t), I32),
                   jax.ShapeDtypeStruct((N_EXPERTS, LANES), F32)],
        scratch_shapes=[pltpu.VMEM((N_EXPERTS, LANES), F32)],
        compiler_params=_params(("arbitrary",)),
        name="mix_out",
    )(attn, yc, x2, mod3, gattn, wout, gpost, gpre2, wrt, br)


def _dest_kernel(idx_ref, rank_ref, pstart_ref, dest_ref):
    tm = idx_ref.shape[1]
    row = lax.broadcasted_iota(I32, (N_EXPERTS, tm), 0)
    krow = lax.broadcasted_iota(I32, (TOP_K, tm), 0)
    pstart = pstart_ref[...]
    idx = idx_ref[...]
    out = jnp.zeros((TOP_K, tm), F32)
    for k in range(TOP_K):
        hit = row == idx[k:k + 1, :]
        base = jnp.sum(jnp.where(hit, pstart, 0.0), axis=0, keepdims=True)
        out = jnp.where(krow == k, base, out)
    dest_ref[...] = out.astype(I32) + rank_ref[...]


def _dest(idx, rank, pstart):
    t = idx.shape[1]
    tm = min(TM_DEST, t)
    col = pl.BlockSpec((TOP_K, tm), lambda i: (0, i))
    return pl.pallas_call(
        _dest_kernel,
        grid=(t // tm,),
        in_specs=[col, col, pl.BlockSpec((N_EXPERTS, 1), lambda i: (0, 0))],
        out_specs=col,
        out_shape=jax.ShapeDtypeStruct((TOP_K, t), I32),
        compiler_params=_params(("arbitrary",)),
        name="dest",
    )(idx, rank, pstart)


_PAD_CHUNKS = tuple(BM_EXPERT >> s for s in range(1, BM_EXPERT.bit_length()))


SC_CORES = 2
SC_SUBCORES = 16
SC_ROWS = 128


def _sc_worker_split(n_chunks):
    workers = SC_CORES * SC_SUBCORES
    per_worker = max(1, n_chunks // workers)
    active = n_chunks // per_worker
    assert active * per_worker == n_chunks and active <= workers
    return per_worker, active


def _scatter_rows_sc(rows, idx2d, n_out):
    n, width = rows.shape
    n_chunks = n // SC_ROWS
    per_worker, active = _sc_worker_split(n_chunks)
    mesh = plsc.VectorSubcoreMesh(core_axis_name="c", subcore_axis_name="s")

    @functools.partial(
        pl.kernel, mesh=mesh, out_type=jax.ShapeDtypeStruct((n_out, width), rows.dtype),
        scratch_types=[pltpu.VMEM((1, SC_ROWS), I32), pltpu.VMEM((SC_ROWS, width), rows.dtype)])
    def scatter(rows_hbm, idx_hbm, out_hbm, idx_v, rows_v):
        wid = lax.axis_index("s") * SC_CORES + lax.axis_index("c")

        @pl.when(wid < active)
        def _():
            @pl.loop(0, per_worker)
            def _(c):
                chunk = wid * per_worker + c
                pltpu.sync_copy(rows_hbm.at[pl.ds(chunk * SC_ROWS, SC_ROWS)], rows_v)
                for k in range(TOP_K):
                    pltpu.sync_copy(idx_hbm.at[pl.ds(k * n_chunks + chunk, 1)], idx_v)
                    pltpu.sync_copy(rows_v, out_hbm.at[idx_v.at[0]])

    return scatter(rows, idx2d)


def _padfill_kernel(pad_from_ref, pad_n_ref, xs_in_hbm, xs_hbm, zero_scr, pad_sem):
    del xs_in_hbm
    zero_scr[...] = jnp.zeros(zero_scr.shape, zero_scr.dtype)

    def pad_copies(e, act):
        n = pad_n_ref[e]
        base = pad_from_ref[e]

        def single_rows(start, count):
            for j in range(SUBLANES - 1):
                @pl.when(j < count)
                def _():
                    act(pltpu.make_async_copy(zero_scr.at[pl.ds(0, 1), :],
                                              xs_hbm.at[pl.ds(start + j, 1), :], pad_sem))

        head = jnp.minimum(n, (SUBLANES - (base & (SUBLANES - 1))) & (SUBLANES - 1))
        single_rows(base, head)
        rest = n - head
        mid = base + head
        for rows in _PAD_CHUNKS:
            if rows >= SUBLANES:
                @pl.when((rest & rows) != 0)
                def _():
                    start = pl.multiple_of(mid + (rest & ~(2 * rows - 1)), SUBLANES)
                    act(pltpu.make_async_copy(zero_scr.at[pl.ds(0, rows), :],
                                              xs_hbm.at[pl.ds(start, rows), :], pad_sem))
        single_rows(mid + (rest & ~(SUBLANES - 1)), rest & (SUBLANES - 1))

    def issue_pad(e, carry):
        pad_copies(e, lambda cp: cp.start())
        return carry

    def drain_pad(e, carry):
        pad_copies(e, lambda cp: cp.wait())
        return carry

    lax.fori_loop(0, N_EXPERTS, issue_pad, 0)
    lax.fori_loop(0, N_EXPERTS, drain_pad, 0)


def _padfill(pad_from, pad_n, xs):
    return pl.pallas_call(
        _padfill_kernel,
        grid_spec=pltpu.PrefetchScalarGridSpec(
            num_scalar_prefetch=2,
            grid=(1,),
            in_specs=[pl.BlockSpec(memory_space=pl.ANY)],
            out_specs=pl.BlockSpec(memory_space=pl.ANY),
            scratch_shapes=[pltpu.VMEM((BM_EXPERT // 2, _ROW_WORDS), U32), pltpu.SemaphoreType.DMA]),
        out_shape=jax.ShapeDtypeStruct(xs.shape, xs.dtype),
        input_output_aliases={2: 0},
        compiler_params=_params(("arbitrary",)),
        name="padfill",
    )(pad_from, pad_n, xs)


_XS_SLOTS = 4
_YS_SLOTS = 2
_W_SLOTS = 2


def _expert_kernel(first_ref, ord_ref, uexp_ref, meta_ref, xs_hbm, wg_hbm, wu_hbm, wd_hbm, ys_hbm,
                   xs_buf, ys_buf, wg_buf, wu_buf, wd_buf, wgu_scr, wd_scr, x_scr, xs_sem, ys_sem, w_sem):
    i = pl.program_id(0)
    nused = meta_ref[0]
    nexp = meta_ref[1]
    bm = xs_buf.shape[1]

    def xs_copy(b, slot):
        return pltpu.make_async_copy(xs_hbm.at[pl.ds(b * bm, bm), :], xs_buf.at[slot], xs_sem.at[slot])

    def ys_copy(b, slot):
        return pltpu.make_async_copy(ys_buf.at[slot], ys_hbm.at[pl.ds(b * bm, bm), :], ys_sem.at[slot])

    def w_copies(j, slot):
        e = uexp_ref[j]
        return (pltpu.make_async_copy(wg_hbm.at[e], wg_buf.at[slot], w_sem.at[slot, 0]),
                pltpu.make_async_copy(wu_hbm.at[e], wu_buf.at[slot], w_sem.at[slot, 1]),
                pltpu.make_async_copy(wd_hbm.at[e], wd_buf.at[slot], w_sem.at[slot, 2]))

    @pl.when(i == 0)
    def _():
        for s in range(_XS_SLOTS - 1):
            @pl.when(s < nused)
            def _():
                xs_copy(s, s).start()
        for s in range(_W_SLOTS):
            @pl.when(s < nexp)
            def _():
                for cp in w_copies(s, s):
                    cp.start()

    @pl.when(i < nused)
    def _():
        ahead = i + _XS_SLOTS - 1

        @pl.when(ahead < nused)
        def _():
            xs_copy(ahead, ahead % _XS_SLOTS).start()

        @pl.when(first_ref[i] == 1)
        def _():
            j = ord_ref[i]
            ws = j % _W_SLOTS
            for cp in w_copies(j, ws):
                cp.wait()
            wgu_scr[:, 0:EXPERT_DIM] = wg_buf[ws].astype(BF16)
            wgu_scr[:, EXPERT_DIM:2 * EXPERT_DIM] = wu_buf[ws].astype(BF16)
            wd_scr[...] = wd_buf[ws].astype(BF16)

            @pl.when(j + _W_SLOTS < nexp)
            def _():
                for cp in w_copies(j + _W_SLOTS, ws):
                    cp.start()

        slot = i % _XS_SLOTS
        xs_copy(i, slot).wait()
        lo, hi = _unpack_row_words(xs_buf[slot])
        x_scr[:, 0:_ROW_WORDS] = lo.astype(BF16)
        x_scr[:, _ROW_WORDS:] = hi.astype(BF16)
        gu = jnp.dot(x_scr[...], wgu_scr[...], preferred_element_type=F32)
        g = gu[:, 0:EXPERT_DIM]
        a = (g * jax.nn.sigmoid(g) * gu[:, EXPERT_DIM:2 * EXPERT_DIM]).astype(BF16)
        y = jnp.dot(a, wd_scr[...], preferred_element_type=F32)
        oslot = i % _YS_SLOTS

        @pl.when(i >= _YS_SLOTS)
        def _():
            ys_copy(i - _YS_SLOTS, oslot).wait()

        ys_buf[oslot] = _pack_row_words(y[:, 0:_ROW_WORDS], y[:, _ROW_WORDS:])
        ys_copy(i, oslot).start()

        @pl.when(i == nused - 1)
        def _():
            ys_copy(i, oslot).wait()

            @pl.when(i >= 1)
            def _():
                ys_copy(i - 1, (i - 1) % _YS_SLOTS).wait()


def _experts(first, ordinal, uexp, meta, xs, w_gate, w_up, w_down):
    p = xs.shape[0]
    d = w_gate.shape[1]
    nb = p // BM_EXPERT
    anyspec = pl.BlockSpec(memory_space=pl.ANY)
    return pl.pallas_call(
        _expert_kernel,
        grid_spec=pltpu.PrefetchScalarGridSpec(
            num_scalar_prefetch=4,
            grid=(nb,),
            in_specs=[anyspec, anyspec, anyspec, anyspec],
            out_specs=anyspec,
            scratch_shapes=[pltpu.VMEM((_XS_SLOTS, BM_EXPERT, _ROW_WORDS), U32),
                            pltpu.VMEM((_YS_SLOTS, BM_EXPERT, _ROW_WORDS), U32),
                            pltpu.VMEM((_W_SLOTS, d, EXPERT_DIM), F32),
                            pltpu.VMEM((_W_SLOTS, d, EXPERT_DIM), F32),
                            pltpu.VMEM((_W_SLOTS, EXPERT_DIM, d), F32),
                            pltpu.VMEM((d, 2 * EXPERT_DIM), BF16), pltpu.VMEM((EXPERT_DIM, d), BF16),
                            pltpu.VMEM((BM_EXPERT, d), BF16),
                            pltpu.SemaphoreType.DMA((_XS_SLOTS,)), pltpu.SemaphoreType.DMA((_YS_SLOTS,)),
                            pltpu.SemaphoreType.DMA((_W_SLOTS, 3))]),
        out_shape=jax.ShapeDtypeStruct((p, _ROW_WORDS), U32),
        compiler_params=_params(("arbitrary",)),
        name="experts",
    )(first, ordinal, uexp, meta, xs, w_gate, w_up, w_down)


def _gather_rows_sc(table, idx):
    n = idx.shape[0]
    width = table.shape[1]
    per_worker, active = _sc_worker_split(n // SC_ROWS)
    mesh = plsc.VectorSubcoreMesh(core_axis_name="c", subcore_axis_name="s")

    @functools.partial(
        pl.kernel, mesh=mesh, out_type=jax.ShapeDtypeStruct((n, width), table.dtype),
        scratch_types=[pltpu.VMEM((SC_ROWS,), I32), pltpu.VMEM((SC_ROWS, width), table.dtype),
                       pltpu.SemaphoreType.DMA])
    def gather(table_hbm, idx_hbm, out_hbm, idx_v, rows_v, sem):
        wid = lax.axis_index("s") * SC_CORES + lax.axis_index("c")

        @pl.when(wid < active)
        def _():
            @pl.loop(0, per_worker)
            def _(c):
                off = (wid * per_worker + c) * SC_ROWS
                pltpu.sync_copy(idx_hbm.at[pl.ds(off, SC_ROWS)], idx_v)
                pltpu.async_copy(table_hbm.at[idx_v], rows_v, sem).wait()
                pltpu.sync_copy(rows_v, out_hbm.at[pl.ds(off, SC_ROWS)])

    return gather(table, idx)


def _combine_kernel(wts_ref, yg_ref, h2_ref, x1_ref, mod_ref, wsg_ref, wsu_ref, wsd_ref, gpost_ref, o_ref):
    h2 = h2_ref[...]
    g = jnp.dot(h2, wsg_ref[...], preferred_element_type=F32)
    u = jnp.dot(h2, wsu_ref[...], preferred_element_type=F32)
    f = jnp.dot((g * jax.nn.sigmoid(g) * u).astype(BF16), wsd_ref[...], preferred_element_type=F32)

    wts = wts_ref[...]
    los = [f[:, sl * LANES:(sl + 1) * LANES] for sl in range(_ROW_SLABS)]
    his = [f[:, _ROW_WORDS + sl * LANES:_ROW_WORDS + (sl + 1) * LANES] for sl in range(_ROW_SLABS)]
    for k in range(TOP_K):
        wk = wts[:, k:k + 1]
        for sl in range(_ROW_SLABS):
            lo, hi = _unpack_row_words(yg_ref[k, :, sl * LANES:(sl + 1) * LANES])
            los[sl] = los[sl] + wk * lo
            his[sl] = his[sl] + wk * hi
    f = jnp.concatenate(los + his, axis=1)
    g2 = mod_ref[0, 5:6, :]
    o_ref[...] = x1_ref[...] + g2 * _rms(f, gpost_ref[...])


def _combine(wts_t, yg, h2, x1, mod3, wsg, wsu, wsd, gpost, seq):
    t, d = x1.shape
    tm = min(TM_COMBINE, seq)
    tpb = seq // tm
    full = lambda a: pl.BlockSpec(a.shape, lambda i: (0,) * a.ndim)
    row = lambda w: pl.BlockSpec((tm, w), lambda i: (i, 0))
    return pl.pallas_call(
        _combine_kernel,
        grid=(t // tm,),
        in_specs=[row(TOP_K), pl.BlockSpec((TOP_K, tm, _ROW_WORDS), lambda i: (0, i, 0)), row(d), row(d),
                  pl.BlockSpec((1, 6, d), lambda i: (i // tpb, 0, 0)),
                  full(wsg), full(wsu), full(wsd), full(gpost)],
        out_specs=row(d),
        out_shape=jax.ShapeDtypeStruct((t, d), F32),
        compiler_params=_params(("arbitrary",)),
        name="combine",
    )(wts_t, yg, h2, x1, mod3, wsg, wsu, wsd, gpost)


def _pack_weights(w_in, w_uq, w_ukv):
    d = w_in.shape[0]
    half = QK_ROPE_DIM // 2
    z = lambda n, c: jnp.zeros((n, c), F32)
    o = Q_LORA_RANK + KV_LORA_RANK
    kr = w_in[:, o:o + QK_ROPE_DIM]
    kr_grp = jnp.concatenate([z(d, QK_NOPE_DIM), kr, z(d, HEAD_PAD - QK_NOPE_DIM - QK_ROPE_DIM)], axis=1)
    kr_rot = jnp.concatenate([z(d, QK_NOPE_DIM), -kr[:, half:], kr[:, :half],
                              z(d, HEAD_PAD - QK_NOPE_DIM - QK_ROPE_DIM)], axis=1)
    win_p = jnp.concatenate([w_in[:, :o], kr_grp, kr_rot, w_in[:, o + QK_ROPE_DIM:]], axis=1)

    scale = float(QK_NOPE_DIM + QK_ROPE_DIM) ** -0.5 * float(np.log2(np.e))
    r = Q_LORA_RANK
    qd = QK_NOPE_DIM + QK_ROPE_DIM
    q_grp, q_rot = [], []
    for h in range(MLA_HEADS):
        nope = w_uq[:, h * qd:h * qd + QK_NOPE_DIM]
        rope = w_uq[:, h * qd + QK_NOPE_DIM:(h + 1) * qd]
        pad = z(r, HEAD_PAD - qd)
        q_grp.append(jnp.concatenate([nope, rope, pad], axis=1))
        q_rot.append(jnp.concatenate([z(r, QK_NOPE_DIM), -rope[:, half:], rope[:, :half], pad], axis=1))
    wuq_p = jnp.concatenate(q_grp + q_rot, axis=1) * scale

    c = KV_LORA_RANK
    kd = QK_NOPE_DIM + V_HEAD_DIM
    k_grp, v_grp = [], []
    for h in range(MLA_HEADS):
        k_grp.append(jnp.concatenate([w_ukv[:, h * kd:h * kd + QK_NOPE_DIM], z(c, HEAD_PAD - QK_NOPE_DIM)], axis=1))
        v_grp.append(jnp.concatenate([w_ukv[:, h * kd + QK_NOPE_DIM:(h + 1) * kd], z(c, HEAD_PAD - V_HEAD_DIM)], axis=1))
    wukv_p = jnp.concatenate(k_grp + v_grp, axis=1)
    return win_p.astype(BF16), wuq_p.astype(BF16), wukv_p.astype(BF16)


def _rope_tables(positions):
    inv = 1.0 / (ROPE_THETA ** (jnp.arange(0, QK_ROPE_DIM, 2, dtype=F32) / QK_ROPE_DIM))
    ang = positions.astype(F32).reshape(-1)[:, None] * inv
    t = ang.shape[0]
    cos, sin = jnp.cos(ang), jnp.sin(ang)
    tail = HEAD_PAD - QK_NOPE_DIM - QK_ROPE_DIM
    cos_t = jnp.concatenate([jnp.ones((t, QK_NOPE_DIM), F32), cos, cos, jnp.ones((t, tail), F32)], axis=1)
    sin_t = jnp.concatenate([jnp.zeros((t, QK_NOPE_DIM), F32), sin, sin, jnp.zeros((t, tail), F32)], axis=1)
    return cos_t, sin_t


def _layer(x2, c, cos_t, sin_t, batch, seq, w_ada, b_ada, g_pre_mix, w_in, g_q_lat, w_uq, g_kv_lat, w_ukv,
           w_conv, g_attn_out, g_conv_out, w_out, g_post_mix, g_pre_ffn, w_router, b_router,
           w_gate, w_up, w_down, w_sh_gate, w_sh_up, w_sh_down, g_post_ffn):
    t, d = x2.shape
    r1 = lambda a: a.reshape(1, -1)

    c_pad = jnp.zeros((SUBLANES, d), F32).at[:batch].set(c)
    mod = _ada(c_pad, w_ada, r1(b_ada))[:batch]
    mod3 = mod.reshape(batch, 6, d)

    win_p, wuq_p, wukv_p = _pack_weights(w_in, w_uq, w_ukv)
    vone = jnp.zeros((1, HEAD_PAD), F32).at[0, V_HEAD_DIM].set(1.0)
    q, k, v, yc = _mix_in(x2, mod3, r1(g_pre_mix), win_p, r1(g_q_lat), wuq_p, r1(g_kv_lat), wukv_p,
                          vone, w_conv, r1(g_conv_out), cos_t, sin_t, seq)
    attn = _attention(q, k, v, batch, seq)
    x1, h2, h2p, idx, wts, rank, cnt = _mix_out(
        attn, yc, x2, mod3, r1(g_attn_out), w_out.astype(BF16), r1(g_post_mix), r1(g_pre_ffn),
        w_router.T, b_router.reshape(-1, 1), seq)

    counts = cnt[:, 0].astype(I32)
    padded = ((counts + BM_EXPERT - 1) // BM_EXPERT) * BM_EXPERT
    pad_end = jnp.cumsum(padded)
    pad_start = pad_end - padded
    m = t * TOP_K
    nb = (m + N_EXPERTS * (BM_EXPERT - 1)) // BM_EXPERT
    nused = pad_end[-1] // BM_EXPERT
    bidx = jnp.arange(nb, dtype=I32)
    blk_exp = jnp.sum((pad_end[None, :] <= (bidx * BM_EXPERT)[:, None]).astype(I32), axis=1)
    first = ((bidx < nused) & ((bidx == 0) | (blk_exp != jnp.roll(blk_exp, 1)))).astype(I32)
    ordinal = jnp.maximum(jnp.cumsum(first) - 1, 0).astype(I32)
    seen = jnp.cumsum((counts > 0).astype(I32))
    uexp = jnp.minimum(jnp.sum((seen[None, :] <= jnp.arange(N_EXPERTS, dtype=I32)[:, None]).astype(I32), axis=1),
                       N_EXPERTS - 1).astype(I32)
    meta = jnp.stack([nused, seen[-1]]).astype(I32)

    dest = _dest(idx, rank, pad_start.astype(F32).reshape(-1, 1))
    xs = _scatter_rows_sc(h2p, dest.reshape(-1, SC_ROWS), nb * BM_EXPERT)
    xs = _padfill((pad_start + counts).astype(I32), (padded - counts).astype(I32), xs)
    ys = _experts(first, ordinal, uexp, meta, xs, w_gate, w_up, w_down)
    yg = _gather_rows_sc(ys, dest.reshape(-1)).reshape(TOP_K, t, _ROW_WORDS)
    return _combine(wts.T, yg, h2, x1, mod3, w_sh_gate.astype(BF16), w_sh_up.astype(BF16),
                    w_sh_down.astype(BF16), r1(g_post_ffn), seq)


def kernel(x, c, positions, w_ada, b_ada, g_pre_mix, w_in, g_q_lat, w_uq, g_kv_lat, w_ukv, w_conv, g_attn_out, g_conv_out, w_out, g_post_mix, g_pre_ffn, w_router, b_router, w_gate, w_up, w_down, w_sh_gate, w_sh_up, w_sh_down, g_post_ffn):
    batch, seq, d = x.shape
    cos_t, sin_t = _rope_tables(positions)
    x2 = x.reshape(batch * seq, d)
    for l in range(w_ada.shape[0]):
        x2 = _layer(x2, c, cos_t, sin_t, batch, seq, w_ada[l], b_ada[l], g_pre_mix[l], w_in[l], g_q_lat[l],
                    w_uq[l], g_kv_lat[l], w_ukv[l], w_conv[l], g_attn_out[l], g_conv_out[l], w_out[l],
                    g_post_mix[l], g_pre_ffn[l], w_router[l], b_router[l], w_gate[l], w_up[l], w_down[l],
                    w_sh_gate[l], w_sh_up[l], w_sh_down[l], g_post_ffn[l])
    return x2.reshape(batch, seq, d)
```

```python
import functools

import jax
import jax.numpy as jnp
import numpy as np
from jax import lax
from jax.experimental import pallas as pl
from jax.experimental.pallas import tpu as pltpu
from jax.experimental.pallas import tpu_sc as plsc

F32 = jnp.float32
BF16 = jnp.bfloat16
I32 = jnp.int32
U32 = jnp.uint32

CHUNK = 64
MLA_HEADS = 8
QK_NOPE_DIM = 64
QK_ROPE_DIM = 32
V_HEAD_DIM = 64
Q_LORA_RANK = 384
KV_LORA_RANK = 256
ROPE_THETA = 10000.0
CONV_WIDTH = 3
N_EXPERTS = 256
TOP_K = 8
N_EXPERT_GROUPS = 8
TOPK_GROUPS = 4
EXPERT_DIM = 256
ROUTED_SCALE = 2.5
EPS = 1e-6

LANES = 128
SUBLANES = 8
HEAD_PAD = LANES
VMEM_LIMIT_BYTES = 56 * 1024 * 1024

TM_IN = 512
TQ_ATTN = 512
TM_OUT = 512
TM_DEST = 512
TM_DISPATCH = 512
BM_EXPERT = 256
TM_COMBINE = 256

NEG_INF = float("-inf")


def _rms(x, g):
    return x * lax.rsqrt(jnp.mean(x * x, axis=-1, keepdims=True) + EPS) * g


_HI_MASK = np.uint32(0xFFFF0000)
_ROW_WORDS = 512
_ROW_SLABS = _ROW_WORDS // LANES


def _pack_row_words(lo, hi):
    lo_w = lax.bitcast_convert_type(lo.astype(BF16).astype(F32), U32) >> 16
    hi_w = lax.bitcast_convert_type(hi.astype(BF16).astype(F32), U32) & _HI_MASK
    return lo_w | hi_w


def _unpack_row_words(w):
    return (lax.bitcast_convert_type(w << 16, F32), lax.bitcast_convert_type(w & _HI_MASK, F32))


def _params(sem):
    return pltpu.CompilerParams(dimension_semantics=sem, vmem_limit_bytes=VMEM_LIMIT_BYTES)


def _ada_kernel(c_ref, w_ref, b_ref, o_ref):
    c = c_ref[...]
    s = c * jax.nn.sigmoid(c)
    o_ref[...] = jnp.dot(s, w_ref[...], preferred_element_type=F32,
                         precision=lax.Precision.HIGHEST) + b_ref[...]


def _ada(c_pad, w, b):
    rows, d = c_pad.shape
    n = w.shape[1]
    tn = 1536
    return pl.pallas_call(
        _ada_kernel,
        grid=(n // tn,),
        in_specs=[pl.BlockSpec((rows, d), lambda j: (0, 0)),
                  pl.BlockSpec((d, tn), lambda j: (0, j)),
                  pl.BlockSpec((1, tn), lambda j: (0, j))],
        out_specs=pl.BlockSpec((rows, tn), lambda j: (0, j)),
        out_shape=jax.ShapeDtypeStruct((rows, n), F32),
        compiler_params=_params(("arbitrary",)),
        name="ada",
    )(c_pad, w, b)


_CQ0, _CQ1 = 0, Q_LORA_RANK
_CKV0, _CKV1 = _CQ1, _CQ1 + KV_LORA_RANK
_KR0, _KR1 = _CKV1, _CKV1 + 2 * HEAD_PAD
_CONV_DIM = 512
_GB0 = _KR1
_GC0 = _GB0 + _CONV_DIM
_XV0 = _GC0 + _CONV_DIM
_WIN_COLS = _XV0 + _CONV_DIM
_QW = MLA_HEADS * HEAD_PAD


def _mix_in_kernel(tiles_per_batch, x_ref, mod_ref, gpre_ref, win_ref, gq_ref, wuq_ref, gkv_ref,
                   wukv_ref, vone_ref, wconv_ref, gconv_ref, cos_ref, sin_ref,
                   q_ref, k_ref, v_ref, yc_ref, h_scr, u_scr):
    i = pl.program_id(0)
    tm = x_ref.shape[0]
    sh1 = mod_ref[0, 0:1, :]
    sc1 = mod_ref[0, 1:2, :]
    h = _rms(x_ref[...], gpre_ref[...]) * (1.0 + sc1) + sh1
    h_scr[...] = h.astype(BF16)
    cos = cos_ref[...]
    sin = sin_ref[...]

    cq = jnp.dot(h_scr[...], win_ref[:, _CQ0:_CQ1], preferred_element_type=F32)
    cqn = _rms(cq, gq_ref[...]).astype(BF16)
    qq = jnp.dot(cqn, wuq_ref[...], preferred_element_type=F32)
    for hd in range(MLA_HEADS):
        lo = hd * HEAD_PAD
        qh = qq[:, lo:lo + HEAD_PAD] * cos + qq[:, _QW + lo:_QW + lo + HEAD_PAD] * sin
        q_ref[:, lo:lo + HEAD_PAD] = qh.astype(BF16)

    ckv = jnp.dot(h_scr[...], win_ref[:, _CKV0:_CKV1], preferred_element_type=F32)
    ckvn = _rms(ckv, gkv_ref[...]).astype(BF16)
    kv = jnp.dot(ckvn, wukv_ref[...], preferred_element_type=F32)
    krr = jnp.dot(h_scr[...], win_ref[:, _KR0:_KR1], preferred_element_type=F32)
    kr = krr[:, 0:HEAD_PAD] * cos + krr[:, HEAD_PAD:2 * HEAD_PAD] * sin
    vone = vone_ref[...]
    for hd in range(MLA_HEADS):
        lo = hd * HEAD_PAD
        k_ref[:, lo:lo + HEAD_PAD] = (kv[:, lo:lo + HEAD_PAD] + kr).astype(BF16)
        v_ref[:, lo:lo + HEAD_PAD] = (kv[:, _QW + lo:_QW + lo + HEAD_PAD] + vone).astype(BF16)

    gb = jnp.dot(h_scr[...], win_ref[:, _GB0:_GC0], preferred_element_type=F32)
    gc = jnp.dot(h_scr[...], win_ref[:, _GC0:_XV0], preferred_element_type=F32)
    xv = jnp.dot(h_scr[...], win_ref[:, _XV0:_WIN_COLS], preferred_element_type=F32)
    u = gc * xv
    prev = u_scr[tm:tm + SUBLANES, :]
    first = (i % tiles_per_batch) == 0
    u_scr[0:SUBLANES, :] = jnp.where(first, jnp.zeros_like(prev), prev)
    u_scr[SUBLANES:tm + SUBLANES, :] = u
    um1 = u_scr[SUBLANES - 1:tm + SUBLANES - 1, :]
    um2 = u_scr[SUBLANES - 2:tm + SUBLANES - 2, :]
    conv = wconv_ref[0:1, :] * um2 + wconv_ref[1:2, :] * um1 + wconv_ref[2:3, :] * u
    yc_ref[...] = _rms(gb * conv, gconv_ref[...]).astype(BF16)


def _mix_in(x2, mod3, gpre, win_p, gq, wuq_p, gkv, wukv_p, vone, wconv, gconv, cos_t, sin_t, seq):
    t, d = x2.shape
    tm = min(TM_IN, seq)
    tpb = seq // tm
    full = lambda a: pl.BlockSpec(a.shape, lambda i: (0,) * a.ndim)
    row = lambda w: pl.BlockSpec((tm, w), lambda i: (i, 0))
    return pl.pallas_call(
        functools.partial(_mix_in_kernel, tpb),
        grid=(t // tm,),
        in_specs=[row(d),
                  pl.BlockSpec((1, 6, d), lambda i: (i // tpb, 0, 0)),
                  full(gpre), full(win_p), full(gq), full(wuq_p), full(gkv), full(wukv_p),
                  full(vone), full(wconv), full(gconv), row(HEAD_PAD), row(HEAD_PAD)],
        out_specs=[row(_QW), row(_QW), row(_QW), row(_CONV_DIM)],
        out_shape=[jax.ShapeDtypeStruct((t, _QW), BF16), jax.ShapeDtypeStruct((t, _QW), BF16),
                   jax.ShapeDtypeStruct((t, _QW), BF16), jax.ShapeDtypeStruct((t, _CONV_DIM), BF16)],
        scratch_shapes=[pltpu.VMEM((tm, d), BF16), pltpu.VMEM((tm + SUBLANES, _CONV_DIM), F32)],
        compiler_params=_params(("arbitrary",)),
        name="mix_in",
    )(x2, mod3, gpre, win_p, gq, wuq_p, gkv, wukv_p, vone, wconv, gconv, cos_t, sin_t)


_HEADS_PER_STEP = 2


def _attn_kernel(tq, q_ref, k_ref, v_ref, o_ref, s_scr, mrun_scr, mb_scr, acc_scr):
    tk = tq
    nq = q_ref.shape[0] // tq
    lane_groups = tk // LANES
    heads = range(_HEADS_PER_STEP)
    lanes = [slice(hh * HEAD_PAD, (hh + 1) * HEAD_PAD) for hh in heads]

    def tile_max(s):
        m = s[:, 0:LANES]
        for g in range(1, lane_groups):
            m = jnp.maximum(m, s[:, g * LANES:(g + 1) * LANES])
        return m

    def scores(hh, qi, kv):
        off = pl.multiple_of(kv * tk, tk)
        return lax.dot_general(q_ref[qi * tq:(qi + 1) * tq, lanes[hh]], k_ref[pl.ds(off, tk), lanes[hh]],
                               (((1,), (1,)), ((), ())), preferred_element_type=F32)

    rc = lax.broadcasted_iota(I32, (tq, tk), 0) // CHUNK
    cc = lax.broadcasted_iota(I32, (tq, tk), 1) // CHUNK

    def diagonal(qi):
        for hh in heads:
            s = jnp.where(cc <= rc, scores(hh, qi, qi), NEG_INF)
            s_scr[hh, qi] = s
            m_row = jnp.max(jnp.maximum(mrun_scr[hh], tile_max(s)), axis=1, keepdims=True)
            mb_scr[hh] = jnp.broadcast_to(m_row, (tq, LANES))

    mrun_scr[...] = jnp.full(mrun_scr.shape, NEG_INF, F32)
    diagonal(0)
    for qi in range(nq):
        has_next = qi + 1 < nq
        acc_scr[...] = jnp.zeros(acc_scr.shape, F32)
        if has_next:
            mrun_scr[...] = jnp.full(mrun_scr.shape, NEG_INF, F32)

        def body(kv, carry, qi=qi, has_next=has_next):
            off = pl.multiple_of(kv * tk, tk)
            for hh in heads:
                mb = mb_scr[hh]
                p = jnp.concatenate(
                    [jnp.exp2(s_scr[hh, kv, :, g * LANES:(g + 1) * LANES] - mb) for g in range(lane_groups)],
                    axis=1).astype(BF16)
                acc_scr[hh] += jnp.dot(p, v_ref[pl.ds(off, tk), lanes[hh]], preferred_element_type=F32)
                if has_next:
                    s = scores(hh, qi + 1, kv)
                    s_scr[hh, kv] = s
                    mrun_scr[hh] = jnp.maximum(mrun_scr[hh], tile_max(s))
            return carry

        lax.fori_loop(0, qi + 1, body, 0)
        for hh in heads:
            acc = acc_scr[hh]
            o = acc[:, 0:V_HEAD_DIM] / acc[:, V_HEAD_DIM:V_HEAD_DIM + 1]
            o_ref[qi * tq:(qi + 1) * tq, hh * V_HEAD_DIM:(hh + 1) * V_HEAD_DIM] = o.astype(BF16)
        if has_next:
            diagonal(qi + 1)


def _attention(q, k, v, batch, seq):
    t = q.shape[0]
    tq = min(TQ_ATTN, seq)
    nq = seq // tq
    hw = _HEADS_PER_STEP * HEAD_PAD
    ow = _HEADS_PER_STEP * V_HEAD_DIM
    blk = lambda w: pl.BlockSpec((seq, w), lambda b, j: (b, j))
    return pl.pallas_call(
        functools.partial(_attn_kernel, tq),
        grid=(batch, MLA_HEADS // _HEADS_PER_STEP),
        in_specs=[blk(hw), blk(hw), blk(hw)],
        out_specs=blk(ow),
        out_shape=jax.ShapeDtypeStruct((t, MLA_HEADS * V_HEAD_DIM), BF16),
        scratch_shapes=[pltpu.VMEM((_HEADS_PER_STEP, nq, tq, tq), F32),
                        pltpu.VMEM((_HEADS_PER_STEP, tq, LANES), F32),
                        pltpu.VMEM((_HEADS_PER_STEP, tq, LANES), F32),
                        pltpu.VMEM((_HEADS_PER_STEP, tq, HEAD_PAD), F32)],
        compiler_params=_params(("arbitrary", "arbitrary")),
        name="attn",
    )(q, k, v)


_GROUP_SIZE = N_EXPERTS // N_EXPERT_GROUPS
_BIG = 1.0e9


def _mix_out_kernel(attn_ref, yc_ref, x_ref, mod_ref, gattn_ref, wout_ref, gpost_ref, gpre2_ref,
                    wrt_ref, br_ref, x1_ref, h2_ref, h2p_ref, idx_ref, wts_ref, rank_ref, cnt_ref,
                    carry_scr):
    i = pl.program_id(0)
    tm = x_ref.shape[0]
    half = attn_ref.shape[1]

    @pl.when(i == 0)
    def _():
        carry_scr[...] = jnp.zeros(carry_scr.shape, F32)

    an = _rms(attn_ref[...].astype(F32), gattn_ref[...]).astype(BF16)
    mix = (jnp.dot(an, wout_ref[0:half, :], preferred_element_type=F32)
           + jnp.dot(yc_ref[...], wout_ref[half:, :], preferred_element_type=F32))
    g1 = mod_ref[0, 2:3, :]
    sh2 = mod_ref[0, 3:4, :]
    sc2 = mod_ref[0, 4:5, :]
    x1 = x_ref[...] + g1 * _rms(mix, gpost_ref[...])
    x1_ref[...] = x1
    h2 = _rms(x1, gpre2_ref[...]) * (1.0 + sc2) + sh2
    h2_ref[...] = h2.astype(BF16)
    words = _pack_row_words(h2[:, 0:_ROW_WORDS], h2[:, _ROW_WORDS:])
    h2p_ref[...] = words

    logits = lax.dot_general(wrt_ref[...], h2, (((1,), (1,)), ((), ())),
                             preferred_element_type=F32, precision=lax.Precision.HIGHEST)
    scores = jax.nn.sigmoid(logits)
    sel = scores + br_ref[...]
    row = lax.broadcasted_iota(I32, (N_EXPERTS, tm), 0).astype(F32)

    gscore = []
    rw = lax.broadcasted_iota(I32, (_GROUP_SIZE, tm), 0).astype(F32)
    for g in range(N_EXPERT_GROUPS):
        blk = sel[g * _GROUP_SIZE:(g + 1) * _GROUP_SIZE, :]
        m1 = jnp.max(blk, axis=0, keepdims=True)
        i1 = jnp.min(jnp.where(blk == m1, rw, _BIG), axis=0, keepdims=True)
        m2 = jnp.max(jnp.where(rw == i1, NEG_INF, blk), axis=0, keepdims=True)
        gscore.append(m1 + m2)

    gkeep = [jnp.zeros((1, tm), F32) for _ in range(N_EXPERT_GROUPS)]
    for _ in range(TOPK_GROUPS):
        mg = functools.reduce(jnp.maximum, gscore)
        ig = functools.reduce(jnp.minimum, [jnp.where(gscore[g] == mg, float(g), _BIG)
                                            for g in range(N_EXPERT_GROUPS)])
        for g in range(N_EXPERT_GROUPS):
            hit = ig == float(g)
            gkeep[g] = jnp.where(hit, 1.0, gkeep[g])
            gscore[g] = jnp.where(hit, NEG_INF, gscore[g])
    cur = jnp.concatenate(
        [jnp.where(gkeep[g] > 0.0, sel[g * _GROUP_SIZE:(g + 1) * _GROUP_SIZE, :], NEG_INF)
         for g in range(N_EXPERT_GROUPS)], axis=0)

    krow = lax.broadcasted_iota(I32, (TOP_K, tm), 0)
    idx_rows = []
    idx_f = jnp.zeros((TOP_K, tm), F32)
    sc_k = jnp.zeros((TOP_K, tm), F32)
    sc_sum = jnp.zeros((1, tm), F32)
    onehot = jnp.zeros((N_EXPERTS, tm), F32)
    for k in range(TOP_K):
        m = jnp.max(cur, axis=0, keepdims=True)
        ik = jnp.min(jnp.where(cur == m, row, _BIG), axis=0, keepdims=True)
        hit = row == ik
        sk = jnp.sum(jnp.where(hit, scores, 0.0), axis=0, keepdims=True)
        cur = jnp.where(hit, NEG_INF, cur)
        onehot = jnp.where(hit, 1.0, onehot)
        idx_rows.append(ik)
        idx_f = jnp.where(krow == k, ik, idx_f)
        sc_k = jnp.where(krow == k, sk, sc_k)
        sc_sum = sc_sum + sk
    wts_ref[...] = sc_k / sc_sum * ROUTED_SCALE
    idx_ref[...] = idx_f.astype(I32)

    tri = (lax.broadcasted_iota(I32, (tm, tm), 0) < lax.broadcasted_iota(I32, (tm, tm), 1))
    excl = jnp.dot(onehot.astype(BF16), tri.astype(BF16), preferred_element_type=F32)
    rank_e = carry_scr[:, 0:1] + excl
    rank_k = jnp.zeros((TOP_K, tm), F32)
    for k in range(TOP_K):
        hit = row == idx_rows[k]
        rk = jnp.sum(jnp.where(hit, rank_e, 0.0), axis=0, keepdims=True)
        rank_k = jnp.where(krow == k, rk, rank_k)
    rank_ref[...] = rank_k.astype(I32)
    carry_scr[...] = carry_scr[...] + jnp.sum(onehot, axis=1, keepdims=True)
    cnt_ref[...] = carry_scr[...]


def _mix_out(attn, yc, x2, mod3, gattn, wout, gpost, gpre2, wrt, br, seq):
    t, d = x2.shape
    tm = min(TM_OUT, seq)
    tpb = seq // tm
    full = lambda a: pl.BlockSpec(a.shape, lambda i: (0,) * a.ndim)
    row = lambda w: pl.BlockSpec((tm, w), lambda i: (i, 0))
    col = pl.BlockSpec((TOP_K, tm), lambda i: (0, i))
    return pl.pallas_call(
        _mix_out_kernel,
        grid=(t // tm,),
        in_specs=[row(attn.shape[1]), row(yc.shape[1]), row(d),
                  pl.BlockSpec((1, 6, d), lambda i: (i // tpb, 0, 0)),
                  full(gattn), full(wout), full(gpost), full(gpre2), full(wrt), full(br)],
        out_specs=[row(d), row(d), row(_ROW_WORDS), col, col, col,
                   pl.BlockSpec((N_EXPERTS, LANES), lambda i: (0, 0))],
        out_shape=[jax.ShapeDtypeStruct((t, d), F32), jax.ShapeDtypeStruct((t, d), BF16),
                   jax.ShapeDtypeStruct((t, _ROW_WORDS), U32),
                   jax.ShapeDtypeStruct((TOP_K, t), I32), jax.ShapeDtypeStruct((TOP_K, t), F32),
                   jax.ShapeDtypeStruct((TOP_K, t), I32),
                   jax.ShapeDtypeStruct((N_EXPERTS, LANES), F32)],
        scratch_shapes=[pltpu.VMEM((N_EXPERTS, LANES), F32)],
        compiler_params=_params(("arbitrary",)),
        name="mix_out",
    )(attn, yc, x2, mod3, gattn, wout, gpost, gpre2, wrt, br)


def _dest_kernel(idx_ref, rank_ref, pstart_ref, dest_ref):
    tm = idx_ref.shape[1]
    row = lax.broadcasted_iota(I32, (N_EXPERTS, tm), 0)
    krow = lax.broadcasted_iota(I32, (TOP_K, tm), 0)
    pstart = pstart_ref[...]
    idx = idx_ref[...]
    out = jnp.zeros((TOP_K, tm), F32)
    for k in range(TOP_K):
        hit = row == idx[k:k + 1, :]
        base = jnp.sum(jnp.where(hit, pstart, 0.0), axis=0, keepdims=True)
        out = jnp.where(krow == k, base, out)
    dest_ref[...] = out.astype(I32) + rank_ref[...]


def _dest(idx, rank, pstart):
    t = idx.shape[1]
    tm = min(TM_DEST, t)
    col = pl.BlockSpec((TOP_K, tm), lambda i: (0, i))
    return pl.pallas_call(
        _dest_kernel,
        grid=(t // tm,),
        in_specs=[col, col, pl.BlockSpec((N_EXPERTS, 1), lambda i: (0, 0))],
        out_specs=col,
        out_shape=jax.ShapeDtypeStruct((TOP_K, t), I32),
        compiler_params=_params(("arbitrary",)),
        name="dest",
    )(idx, rank, pstart)


_PAD_CHUNKS = tuple(BM_EXPERT >> s for s in range(1, BM_EXPERT.bit_length()))


SC_CORES = 2
SC_SUBCORES = 16
SC_ROWS = 128


def _sc_worker_split(n_chunks):
    workers = SC_CORES * SC_SUBCORES
    per_worker = max(1, n_chunks // workers)
    active = n_chunks // per_worker
    assert active * per_worker == n_chunks and active <= workers
    return per_worker, active


def _scatter_rows_sc(rows, idx2d, n_out):
    n, width = rows.shape
    n_chunks = n // SC_ROWS
    per_worker, active = _sc_worker_split(n_chunks)
    mesh = plsc.VectorSubcoreMesh(core_axis_name="c", subcore_axis_name="s")

    @functools.partial(
        pl.kernel, mesh=mesh, out_type=jax.ShapeDtypeStruct((n_out, width), rows.dtype),
        scratch_types=[pltpu.VMEM((1, SC_ROWS), I32), pltpu.VMEM((SC_ROWS, width), rows.dtype)])
    def scatter(rows_hbm, idx_hbm, out_hbm, idx_v, rows_v):
        wid = lax.axis_index("s") * SC_CORES + lax.axis_index("c")

        @pl.when(wid < active)
        def _():
            @pl.loop(0, per_worker)
            def _(c):
                chunk = wid * per_worker + c
                pltpu.sync_copy(rows_hbm.at[pl.ds(chunk * SC_ROWS, SC_ROWS)], rows_v)
                for k in range(TOP_K):
                    pltpu.sync_copy(idx_hbm.at[pl.ds(k * n_chunks + chunk, 1)], idx_v)
                    pltpu.sync_copy(rows_v, out_hbm.at[idx_v.at[0]])

    return scatter(rows, idx2d)


def _padfill_kernel(pad_from_ref, pad_n_ref, xs_in_hbm, xs_hbm, zero_scr, pad_sem):
    del xs_in_hbm
    zero_scr[...] = jnp.zeros(zero_scr.shape, zero_scr.dtype)

    def pad_copies(e, act):
        n = pad_n_ref[e]
        base = pad_from_ref[e]

        def single_rows(start, count):
            for j in range(SUBLANES - 1):
                @pl.when(j < count)
                def _():
                    act(pltpu.make_async_copy(zero_scr.at[pl.ds(0, 1), :],
                                              xs_hbm.at[pl.ds(start + j, 1), :], pad_sem))

        head = jnp.minimum(n, (SUBLANES - (base & (SUBLANES - 1))) & (SUBLANES - 1))
        single_rows(base, head)
        rest = n - head
        mid = base + head
        for rows in _PAD_CHUNKS:
            if rows >= SUBLANES:
                @pl.when((rest & rows) != 0)
                def _():
                    start = pl.multiple_of(mid + (rest & ~(2 * rows - 1)), SUBLANES)
                    act(pltpu.make_async_copy(zero_scr.at[pl.ds(0, rows), :],
                                              xs_hbm.at[pl.ds(start, rows), :], pad_sem))
        single_rows(mid + (rest & ~(SUBLANES - 1)), rest & (SUBLANES - 1))

    def issue_pad(e, carry):
        pad_copies(e, lambda cp: cp.start())
        return carry

    def drain_pad(e, carry):
        pad_copies(e, lambda cp: cp.wait())
        return carry

    lax.fori_loop(0, N_EXPERTS, issue_pad, 0)
    lax.fori_loop(0, N_EXPERTS, drain_pad, 0)


def _padfill(pad_from, pad_n, xs):
    return pl.pallas_call(
        _padfill_kernel,
        grid_spec=pltpu.PrefetchScalarGridSpec(
            num_scalar_prefetch=2,
            grid=(1,),
            in_specs=[pl.BlockSpec(memory_space=pl.ANY)],
            out_specs=pl.BlockSpec(memory_space=pl.ANY),
            scratch_shapes=[pltpu.VMEM((BM_EXPERT // 2, _ROW_WORDS), U32), pltpu.SemaphoreType.DMA]),
        out_shape=jax.ShapeDtypeStruct(xs.shape, xs.dtype),
        input_output_aliases={2: 0},
        compiler_params=_params(("arbitrary",)),
        name="padfill",
    )(pad_from, pad_n, xs)


_XS_SLOTS = 4
_YS_SLOTS = 2
_W_SLOTS = 4
_W_AHEAD = 2


def _expert_kernel(first_ref, ord_ref, uexp_ref, meta_ref, xs_hbm, wg_hbm, wu_hbm, wd_hbm, ys_hbm,
                   xs_buf, ys_buf, wg_buf, wu_buf, wd_buf, act_scr, xs_sem, ys_sem, w_sem):
    i = pl.program_id(0)
    nused = meta_ref[0]
    nexp = meta_ref[1]
    bm = xs_buf.shape[1]

    def xs_copy(b, slot):
        return pltpu.make_async_copy(xs_hbm.at[pl.ds(b * bm, bm), :], xs_buf.at[slot], xs_sem.at[slot])

    def ys_copy(b, slot):
        return pltpu.make_async_copy(ys_buf.at[slot], ys_hbm.at[pl.ds(b * bm, bm), :], ys_sem.at[slot])

    def w_copies(j, slot):
        e = uexp_ref[j]
        return (pltpu.make_async_copy(wg_hbm.at[e], wg_buf.at[slot], w_sem.at[slot, 0]),
                pltpu.make_async_copy(wu_hbm.at[e], wu_buf.at[slot], w_sem.at[slot, 1]),
                pltpu.make_async_copy(wd_hbm.at[e], wd_buf.at[slot], w_sem.at[slot, 2]))

    @pl.when(i == 0)
    def _():
        for s in range(_XS_SLOTS - 1):
            @pl.when(s < nused)
            def _():
                xs_copy(s, s).start()
        for s in range(_W_AHEAD):
            @pl.when(s < nexp)
            def _():
                for cp in w_copies(s, s):
                    cp.start()

    def fetch(b):
        ahead = b + _XS_SLOTS - 1

        @pl.when(ahead < nused)
        def _():
            xs_copy(ahead, ahead % _XS_SLOTS).start()

        j = ord_ref[b]

        @pl.when(first_ref[b] == 1)
        def _():
            for cp in w_copies(j, j % _W_SLOTS):
                cp.wait()
            nxt = j + _W_AHEAD

            @pl.when(nxt < nexp)
            def _():
                for cp in w_copies(nxt, nxt % _W_SLOTS):
                    cp.start()

        xs_copy(b, b % _XS_SLOTS).wait()

    def gate_up(b):
        ws = ord_ref[b] % _W_SLOTS
        lo, hi = _unpack_row_words(xs_buf[b % _XS_SLOTS])
        g = (jnp.dot(lo, wg_buf[ws, 0:_ROW_WORDS, :], preferred_element_type=F32)
             + jnp.dot(hi, wg_buf[ws, _ROW_WORDS:, :], preferred_element_type=F32))
        u = (jnp.dot(lo, wu_buf[ws, 0:_ROW_WORDS, :], preferred_element_type=F32)
             + jnp.dot(hi, wu_buf[ws, _ROW_WORDS:, :], preferred_element_type=F32))
        return g * jax.nn.sigmoid(g) * u

    def down(b, act):
        y = jnp.dot(act, wd_buf[ord_ref[b] % _W_SLOTS], preferred_element_type=F32)
        oslot = b % _YS_SLOTS
        ys_buf[oslot] = _pack_row_words(y[:, 0:_ROW_WORDS], y[:, _ROW_WORDS:])
        ys_copy(b, oslot).start()

    @pl.when(jnp.logical_and(i >= _YS_SLOTS + 1, i <= nused))
    def _():
        ys_copy(i - 1 - _YS_SLOTS, (i - 1) % _YS_SLOTS).wait()

    @pl.when(i == 0)
    def _():
        fetch(i)
        act_scr[...] = gate_up(i)

    @pl.when(jnp.logical_and(i >= 1, i < nused))
    def _():
        fetch(i)
        prev = act_scr[...]
        act_scr[...] = gate_up(i)
        down(i - 1, prev)

    @pl.when(i == nused)
    def _():
        down(i - 1, act_scr[...])
        ys_copy(i - 1, (i - 1) % _YS_SLOTS).wait()

        @pl.when(i >= 2)
        def _():
            ys_copy(i - 2, (i - 2) % _YS_SLOTS).wait()


def _experts(first, ordinal, uexp, meta, xs, w_gate, w_up, w_down):
    p = xs.shape[0]
    d = w_gate.shape[1]
    nb = p // BM_EXPERT
    anyspec = pl.BlockSpec(memory_space=pl.ANY)
    return pl.pallas_call(
        _expert_kernel,
        grid_spec=pltpu.PrefetchScalarGridSpec(
            num_scalar_prefetch=4,
            grid=(nb + 1,),
            in_specs=[anyspec, anyspec, anyspec, anyspec],
            out_specs=anyspec,
            scratch_shapes=[pltpu.VMEM((_XS_SLOTS, BM_EXPERT, _ROW_WORDS), U32),
                            pltpu.VMEM((_YS_SLOTS, BM_EXPERT, _ROW_WORDS), U32),
                            pltpu.VMEM((_W_SLOTS, d, EXPERT_DIM), F32),
                            pltpu.VMEM((_W_SLOTS, d, EXPERT_DIM), F32),
                            pltpu.VMEM((_W_SLOTS, EXPERT_DIM, d), F32),
                            pltpu.VMEM((BM_EXPERT, EXPERT_DIM), F32),
                            pltpu.SemaphoreType.DMA((_XS_SLOTS,)), pltpu.SemaphoreType.DMA((_YS_SLOTS,)),
                            pltpu.SemaphoreType.DMA((_W_SLOTS, 3))]),
        out_shape=jax.ShapeDtypeStruct((p, _ROW_WORDS), U32),
        compiler_params=_params(("arbitrary",)),
        name="experts",
    )(first, ordinal, uexp, meta, xs, w_gate, w_up, w_down)


def _gather_rows_sc(table, idx):
    n = idx.shape[0]
    width = table.shape[1]
    per_worker, active = _sc_worker_split(n // SC_ROWS)
    mesh = plsc.VectorSubcoreMesh(core_axis_name="c", subcore_axis_name="s")

    @functools.partial(
        pl.kernel, mesh=mesh, out_type=jax.ShapeDtypeStruct((n, width), table.dtype),
        scratch_types=[pltpu.VMEM((SC_ROWS,), I32), pltpu.VMEM((SC_ROWS, width), table.dtype),
                       pltpu.SemaphoreType.DMA])
    def gather(table_hbm, idx_hbm, out_hbm, idx_v, rows_v, sem):
        wid = lax.axis_index("s") * SC_CORES + lax.axis_index("c")

        @pl.when(wid < active)
        def _():
            @pl.loop(0, per_worker)
            def _(c):
                off = (wid * per_worker + c) * SC_ROWS
                pltpu.sync_copy(idx_hbm.at[pl.ds(off, SC_ROWS)], idx_v)
                pltpu.async_copy(table_hbm.at[idx_v], rows_v, sem).wait()
                pltpu.sync_copy(rows_v, out_hbm.at[pl.ds(off, SC_ROWS)])

    return gather(table, idx)


def _combine_kernel(wts_ref, yg_ref, h2_ref, x1_ref, mod_ref, wsg_ref, wsu_ref, wsd_ref, gpost_ref, o_ref):
    h2 = h2_ref[...]
    g = jnp.dot(h2, wsg_ref[...], preferred_element_type=F32)
    u = jnp.dot(h2, wsu_ref[...], preferred_element_type=F32)
    f = jnp.dot((g * jax.nn.sigmoid(g) * u).astype(BF16), wsd_ref[...], preferred_element_type=F32)

    wts = wts_ref[...]
    los = [f[:, sl * LANES:(sl + 1) * LANES] for sl in range(_ROW_SLABS)]
    his = [f[:, _ROW_WORDS + sl * LANES:_ROW_WORDS + (sl + 1) * LANES] for sl in range(_ROW_SLABS)]
    for k in range(TOP_K):
        wk = wts[:, k:k + 1]
        for sl in range(_ROW_SLABS):
            lo, hi = _unpack_row_words(yg_ref[k, :, sl * LANES:(sl + 1) * LANES])
            los[sl] = los[sl] + wk * lo
            his[sl] = his[sl] + wk * hi
    f = jnp.concatenate(los + his, axis=1)
    g2 = mod_ref[0, 5:6, :]
    o_ref[...] = x1_ref[...] + g2 * _rms(f, gpost_ref[...])


def _combine(wts_t, yg, h2, x1, mod3, wsg, wsu, wsd, gpost, seq):
    t, d = x1.shape
    tm = min(TM_COMBINE, seq)
    tpb = seq // tm
    full = lambda a: pl.BlockSpec(a.shape, lambda i: (0,) * a.ndim)
    row = lambda w: pl.BlockSpec((tm, w), lambda i: (i, 0))
    return pl.pallas_call(
        _combine_kernel,
        grid=(t // tm,),
        in_specs=[row(TOP_K), pl.BlockSpec((TOP_K, tm, _ROW_WORDS), lambda i: (0, i, 0)), row(d), row(d),
                  pl.BlockSpec((1, 6, d), lambda i: (i // tpb, 0, 0)),
                  full(wsg), full(wsu), full(wsd), full(gpost)],
        out_specs=row(d),
        out_shape=jax.ShapeDtypeStruct((t, d), F32),
        compiler_params=_params(("arbitrary",)),
        name="combine",
    )(wts_t, yg, h2, x1, mod3, wsg, wsu, wsd, gpost)


def _pack_weights(w_in, w_uq, w_ukv):
    d = w_in.shape[0]
    half = QK_ROPE_DIM // 2
    z = lambda n, c: jnp.zeros((n, c), F32)
    o = Q_LORA_RANK + KV_LORA_RANK
    kr = w_in[:, o:o + QK_ROPE_DIM]
    kr_grp = jnp.concatenate([z(d, QK_NOPE_DIM), kr, z(d, HEAD_PAD - QK_NOPE_DIM - QK_ROPE_DIM)], axis=1)
    kr_rot = jnp.concatenate([z(d, QK_NOPE_DIM), -kr[:, half:], kr[:, :half],
                              z(d, HEAD_PAD - QK_NOPE_DIM - QK_ROPE_DIM)], axis=1)
    win_p = jnp.concatenate([w_in[:, :o], kr_grp, kr_rot, w_in[:, o + QK_ROPE_DIM:]], axis=1)

    scale = float(QK_NOPE_DIM + QK_ROPE_DIM) ** -0.5 * float(np.log2(np.e))
    r = Q_LORA_RANK
    qd = QK_NOPE_DIM + QK_ROPE_DIM
    q_grp, q_rot = [], []
    for h in range(MLA_HEADS):
        nope = w_uq[:, h * qd:h * qd + QK_NOPE_DIM]
        rope = w_uq[:, h * qd + QK_NOPE_DIM:(h + 1) * qd]
        pad = z(r, HEAD_PAD - qd)
        q_grp.append(jnp.concatenate([nope, rope, pad], axis=1))
        q_rot.append(jnp.concatenate([z(r, QK_NOPE_DIM), -rope[:, half:], rope[:, :half], pad], axis=1))
    wuq_p = jnp.concatenate(q_grp + q_rot, axis=1) * scale

    c = KV_LORA_RANK
    kd = QK_NOPE_DIM + V_HEAD_DIM
    k_grp, v_grp = [], []
    for h in range(MLA_HEADS):
        k_grp.append(jnp.concatenate([w_ukv[:, h * kd:h * kd + QK_NOPE_DIM], z(c, HEAD_PAD - QK_NOPE_DIM)], axis=1))
        v_grp.append(jnp.concatenate([w_ukv[:, h * kd + QK_NOPE_DIM:(h + 1) * kd], z(c, HEAD_PAD - V_HEAD_DIM)], axis=1))
    wukv_p = jnp.concatenate(k_grp + v_grp, axis=1)
    return win_p.astype(BF16), wuq_p.astype(BF16), wukv_p.astype(BF16)


def _rope_tables(positions):
    inv = 1.0 / (ROPE_THETA ** (jnp.arange(0, QK_ROPE_DIM, 2, dtype=F32) / QK_ROPE_DIM))
    ang = positions.astype(F32).reshape(-1)[:, None] * inv
    t = ang.shape[0]
    cos, sin = jnp.cos(ang), jnp.sin(ang)
    tail = HEAD_PAD - QK_NOPE_DIM - QK_ROPE_DIM
    cos_t = jnp.concatenate([jnp.ones((t, QK_NOPE_DIM), F32), cos, cos, jnp.ones((t, tail), F32)], axis=1)
    sin_t = jnp.concatenate([jnp.zeros((t, QK_NOPE_DIM), F32), sin, sin, jnp.zeros((t, tail), F32)], axis=1)
    return cos_t, sin_t


def _layer(x2, c, cos_t, sin_t, batch, seq, w_ada, b_ada, g_pre_mix, w_in, g_q_lat, w_uq, g_kv_lat, w_ukv,
           w_conv, g_attn_out, g_conv_out, w_out, g_post_mix, g_pre_ffn, w_router, b_router,
           w_gate, w_up, w_down, w_sh_gate, w_sh_up, w_sh_down, g_post_ffn):
    t, d = x2.shape
    r1 = lambda a: a.reshape(1, -1)

    c_pad = jnp.zeros((SUBLANES, d), F32).at[:batch].set(c)
    mod = _ada(c_pad, w_ada, r1(b_ada))[:batch]
    mod3 = mod.reshape(batch, 6, d)

    win_p, wuq_p, wukv_p = _pack_weights(w_in, w_uq, w_ukv)
    vone = jnp.zeros((1, HEAD_PAD), F32).at[0, V_HEAD_DIM].set(1.0)
    q, k, v, yc = _mix_in(x2, mod3, r1(g_pre_mix), win_p, r1(g_q_lat), wuq_p, r1(g_kv_lat), wukv_p,
                          vone, w_conv, r1(g_conv_out), cos_t, sin_t, seq)
    attn = _attention(q, k, v, batch, seq)
    x1, h2, h2p, idx, wts, rank, cnt = _mix_out(
        attn, yc, x2, mod3, r1(g_attn_out), w_out.astype(BF16), r1(g_post_mix), r1(g_pre_ffn),
        w_router.T, b_router.reshape(-1, 1), seq)

    counts = cnt[:, 0].astype(I32)
    padded = ((counts + BM_EXPERT - 1) // BM_EXPERT) * BM_EXPERT
    pad_end = jnp.cumsum(padded)
    pad_start = pad_end - padded
    m = t * TOP_K
    nb = (m + N_EXPERTS * (BM_EXPERT - 1)) // BM_EXPERT
    nused = pad_end[-1] // BM_EXPERT
    bidx = jnp.arange(nb, dtype=I32)
    blk_exp = jnp.sum((pad_end[None, :] <= (bidx * BM_EXPERT)[:, None]).astype(I32), axis=1)
    first = ((bidx < nused) & ((bidx == 0) | (blk_exp != jnp.roll(blk_exp, 1)))).astype(I32)
    ordinal = jnp.maximum(jnp.cumsum(first) - 1, 0).astype(I32)
    seen = jnp.cumsum((counts > 0).astype(I32))
    uexp = jnp.minimum(jnp.sum((seen[None, :] <= jnp.arange(N_EXPERTS, dtype=I32)[:, None]).astype(I32), axis=1),
                       N_EXPERTS - 1).astype(I32)
    meta = jnp.stack([nused, seen[-1]]).astype(I32)

    dest = _dest(idx, rank, pad_start.astype(F32).reshape(-1, 1))
    xs = _scatter_rows_sc(h2p, dest.reshape(-1, SC_ROWS), nb * BM_EXPERT)
    xs = _padfill((pad_start + counts).astype(I32), (padded - counts).astype(I32), xs)
    ys = _experts(first, ordinal, uexp, meta, xs, w_gate, w_up, w_down)
    yg = _gather_rows_sc(ys, dest.reshape(-1)).reshape(TOP_K, t, _ROW_WORDS)
    return _combine(wts.T, yg, h2, x1, mod3, w_sh_gate.astype(BF16), w_sh_up.astype(BF16),
                    w_sh_down.astype(BF16), r1(g_post_ffn), seq)


def kernel(x, c, positions, w_ada, b_ada, g_pre_mix, w_in, g_q_lat, w_uq, g_kv_lat, w_ukv, w_conv, g_attn_out, g_conv_out, w_out, g_post_mix, g_pre_ffn, w_router, b_router, w_gate, w_up, w_down, w_sh_gate, w_sh_up, w_sh_down, g_post_ffn):
    batch, seq, d = x.shape
    cos_t, sin_t = _rope_tables(positions)
    x2 = x.reshape(batch * seq, d)
    for l in range(w_ada.shape[0]):
        x2 = _layer(x2, c, cos_t, sin_t, batch, seq, w_ada[l], b_ada[l], g_pre_mix[l], w_in[l], g_q_lat[l],
                    w_uq[l], g_kv_lat[l], w_ukv[l], w_conv[l], g_attn_out[l], g_conv_out[l], w_out[l],
                    g_post_mix[l], g_pre_ffn[l], w_router[l], b_router[l], w_gate[l], w_up[l], w_down[l],
                    w_sh_gate[l], w_sh_up[l], w_sh_down[l], g_post_ffn[l])
    return x2.reshape(batch, seq, d)
```

```python
import functools

import jax
import jax.numpy as jnp
import numpy as np
from jax import lax
from jax.experimental import pallas as pl
from jax.experimental.pallas import tpu as pltpu
from jax.experimental.pallas import tpu_sc as plsc

F32 = jnp.float32
BF16 = jnp.bfloat16
I32 = jnp.int32
U32 = jnp.uint32

CHUNK = 64
MLA_HEADS = 8
QK_NOPE_DIM = 64
QK_ROPE_DIM = 32
V_HEAD_DIM = 64
Q_LORA_RANK = 384
KV_LORA_RANK = 256
ROPE_THETA = 10000.0
CONV_WIDTH = 3
N_EXPERTS = 256
TOP_K = 8
N_EXPERT_GROUPS = 8
TOPK_GROUPS = 4
EXPERT_DIM = 256
ROUTED_SCALE = 2.5
EPS = 1e-6

LANES = 128
SUBLANES = 8
HEAD_PAD = LANES
VMEM_LIMIT_BYTES = 56 * 1024 * 1024

TM_IN = 512
TQ_ATTN = 512
TM_OUT = 512
TM_DEST = 512
TM_DISPATCH = 512
BM_EXPERT = 256
TM_COMBINE = 256

NEG_INF = float("-inf")


def _rms(x, g):
    return x * lax.rsqrt(jnp.mean(x * x, axis=-1, keepdims=True) + EPS) * g


_HI_MASK = np.uint32(0xFFFF0000)
_ROW_WORDS = 512
_ROW_SLABS = _ROW_WORDS // LANES


def _pack_row_words(lo, hi):
    lo_w = lax.bitcast_convert_type(lo.astype(BF16).astype(F32), U32) >> 16
    hi_w = lax.bitcast_convert_type(hi.astype(BF16).astype(F32), U32) & _HI_MASK
    return lo_w | hi_w


def _unpack_row_words(w):
    return (lax.bitcast_convert_type(w << 16, F32), lax.bitcast_convert_type(w & _HI_MASK, F32))


def _params(sem):
    return pltpu.CompilerParams(dimension_semantics=sem, vmem_limit_bytes=VMEM_LIMIT_BYTES)


def _ada_kernel(c_ref, w_ref, b_ref, o_ref):
    c = c_ref[...]
    s = c * jax.nn.sigmoid(c)
    o_ref[...] = jnp.dot(s, w_ref[...], preferred_element_type=F32,
                         precision=lax.Precision.HIGHEST) + b_ref[...]


def _ada(c_pad, w, b):
    rows, d = c_pad.shape
    n = w.shape[1]
    tn = 1536
    return pl.pallas_call(
        _ada_kernel,
        grid=(n // tn,),
        in_specs=[pl.BlockSpec((rows, d), lambda j: (0, 0)),
                  pl.BlockSpec((d, tn), lambda j: (0, j)),
                  pl.BlockSpec((1, tn), lambda j: (0, j))],
        out_specs=pl.BlockSpec((rows, tn), lambda j: (0, j)),
        out_shape=jax.ShapeDtypeStruct((rows, n), F32),
        compiler_params=_params(("arbitrary",)),
        name="ada",
    )(c_pad, w, b)


_CQ0, _CQ1 = 0, Q_LORA_RANK
_CKV0, _CKV1 = _CQ1, _CQ1 + KV_LORA_RANK
_KR0, _KR1 = _CKV1, _CKV1 + 2 * HEAD_PAD
_CONV_DIM = 512
_GB0 = _KR1
_GC0 = _GB0 + _CONV_DIM
_XV0 = _GC0 + _CONV_DIM
_WIN_COLS = _XV0 + _CONV_DIM
_QW = MLA_HEADS * HEAD_PAD


def _mix_in_kernel(tiles_per_batch, x_ref, mod_ref, gpre_ref, win_ref, gq_ref, wuq_ref, gkv_ref,
                   wukv_ref, vone_ref, wconv_ref, gconv_ref, cos_ref, sin_ref,
                   q_ref, k_ref, v_ref, yc_ref, h_scr, u_scr):
    i = pl.program_id(0)
    tm = x_ref.shape[0]
    sh1 = mod_ref[0, 0:1, :]
    sc1 = mod_ref[0, 1:2, :]
    h = _rms(x_ref[...], gpre_ref[...]) * (1.0 + sc1) + sh1
    h_scr[...] = h.astype(BF16)
    cos = cos_ref[...]
    sin = sin_ref[...]

    cq = jnp.dot(h_scr[...], win_ref[:, _CQ0:_CQ1], preferred_element_type=F32)
    cqn = _rms(cq, gq_ref[...]).astype(BF16)
    qq = jnp.dot(cqn, wuq_ref[...], preferred_element_type=F32)
    for hd in range(MLA_HEADS):
        lo = hd * HEAD_PAD
        qh = qq[:, lo:lo + HEAD_PAD] * cos + qq[:, _QW + lo:_QW + lo + HEAD_PAD] * sin
        q_ref[:, lo:lo + HEAD_PAD] = qh.astype(BF16)

    ckv = jnp.dot(h_scr[...], win_ref[:, _CKV0:_CKV1], preferred_element_type=F32)
    ckvn = _rms(ckv, gkv_ref[...]).astype(BF16)
    kv = jnp.dot(ckvn, wukv_ref[...], preferred_element_type=F32)
    krr = jnp.dot(h_scr[...], win_ref[:, _KR0:_KR1], preferred_element_type=F32)
    kr = krr[:, 0:HEAD_PAD] * cos + krr[:, HEAD_PAD:2 * HEAD_PAD] * sin
    vone = vone_ref[...]
    for hd in range(MLA_HEADS):
        lo = hd * HEAD_PAD
        k_ref[:, lo:lo + HEAD_PAD] = (kv[:, lo:lo + HEAD_PAD] + kr).astype(BF16)
        v_ref[:, lo:lo + HEAD_PAD] = (kv[:, _QW + lo:_QW + lo + HEAD_PAD] + vone).astype(BF16)

    gb = jnp.dot(h_scr[...], win_ref[:, _GB0:_GC0], preferred_element_type=F32)
    gc = jnp.dot(h_scr[...], win_ref[:, _GC0:_XV0], preferred_element_type=F32)
    xv = jnp.dot(h_scr[...], win_ref[:, _XV0:_WIN_COLS], preferred_element_type=F32)
    u = gc * xv
    prev = u_scr[tm:tm + SUBLANES, :]
    first = (i % tiles_per_batch) == 0
    u_scr[0:SUBLANES, :] = jnp.where(first, jnp.zeros_like(prev), prev)
    u_scr[SUBLANES:tm + SUBLANES, :] = u
    um1 = u_scr[SUBLANES - 1:tm + SUBLANES - 1, :]
    um2 = u_scr[SUBLANES - 2:tm + SUBLANES - 2, :]
    conv = wconv_ref[0:1, :] * um2 + wconv_ref[1:2, :] * um1 + wconv_ref[2:3, :] * u
    yc_ref[...] = _rms(gb * conv, gconv_ref[...]).astype(BF16)


def _mix_in(x2, mod3, gpre, win_p, gq, wuq_p, gkv, wukv_p, vone, wconv, gconv, cos_t, sin_t, seq):
    t, d = x2.shape
    tm = min(TM_IN, seq)
    tpb = seq // tm
    full = lambda a: pl.BlockSpec(a.shape, lambda i: (0,) * a.ndim)
    row = lambda w: pl.BlockSpec((tm, w), lambda i: (i, 0))
    return pl.pallas_call(
        functools.partial(_mix_in_kernel, tpb),
        grid=(t // tm,),
        in_specs=[row(d),
                  pl.BlockSpec((1, 6, d), lambda i: (i // tpb, 0, 0)),
                  full(gpre), full(win_p), full(gq), full(wuq_p), full(gkv), full(wukv_p),
                  full(vone), full(wconv), full(gconv), row(HEAD_PAD), row(HEAD_PAD)],
        out_specs=[row(_QW), row(_QW), row(_QW), row(_CONV_DIM)],
        out_shape=[jax.ShapeDtypeStruct((t, _QW), BF16), jax.ShapeDtypeStruct((t, _QW), BF16),
                   jax.ShapeDtypeStruct((t, _QW), BF16), jax.ShapeDtypeStruct((t, _CONV_DIM), BF16)],
        scratch_shapes=[pltpu.VMEM((tm, d), BF16), pltpu.VMEM((tm + SUBLANES, _CONV_DIM), F32)],
        compiler_params=_params(("arbitrary",)),
        name="mix_in",
    )(x2, mod3, gpre, win_p, gq, wuq_p, gkv, wukv_p, vone, wconv, gconv, cos_t, sin_t)


_HEADS_PER_STEP = 2


def _attn_kernel(tq, q_ref, k_ref, v_ref, o_ref, s_scr, mrun_scr, mb_scr, acc_scr):
    tk = tq
    nq = q_ref.shape[0] // tq
    lane_groups = tk // LANES
    heads = range(_HEADS_PER_STEP)
    lanes = [slice(hh * HEAD_PAD, (hh + 1) * HEAD_PAD) for hh in heads]

    def tile_max(s):
        m = s[:, 0:LANES]
        for g in range(1, lane_groups):
            m = jnp.maximum(m, s[:, g * LANES:(g + 1) * LANES])
        return m

    def scores(hh, qi, kv):
        off = pl.multiple_of(kv * tk, tk)
        return lax.dot_general(q_ref[qi * tq:(qi + 1) * tq, lanes[hh]], k_ref[pl.ds(off, tk), lanes[hh]],
                               (((1,), (1,)), ((), ())), preferred_element_type=F32)

    rc = lax.broadcasted_iota(I32, (tq, tk), 0) // CHUNK
    cc = lax.broadcasted_iota(I32, (tq, tk), 1) // CHUNK

    def diagonal(qi):
        for hh in heads:
            s = jnp.where(cc <= rc, scores(hh, qi, qi), NEG_INF)
            s_scr[hh, qi] = s
            m_row = jnp.max(jnp.maximum(mrun_scr[hh], tile_max(s)), axis=1, keepdims=True)
            mb_scr[hh] = jnp.broadcast_to(m_row, (tq, LANES))

    mrun_scr[...] = jnp.full(mrun_scr.shape, NEG_INF, F32)
    diagonal(0)
    for qi in range(nq):
        has_next = qi + 1 < nq
        acc_scr[...] = jnp.zeros(acc_scr.shape, F32)
        if has_next:
            mrun_scr[...] = jnp.full(mrun_scr.shape, NEG_INF, F32)

        def body(kv, carry, qi=qi, has_next=has_next):
            off = pl.multiple_of(kv * tk, tk)
            for hh in heads:
                mb = mb_scr[hh]
                p = jnp.concatenate(
                    [jnp.exp2(s_scr[hh, kv, :, g * LANES:(g + 1) * LANES] - mb) for g in range(lane_groups)],
                    axis=1).astype(BF16)
                acc_scr[hh] += jnp.dot(p, v_ref[pl.ds(off, tk), lanes[hh]], preferred_element_type=F32)
                if has_next:
                    s = scores(hh, qi + 1, kv)
                    s_scr[hh, kv] = s
                    mrun_scr[hh] = jnp.maximum(mrun_scr[hh], tile_max(s))
            return carry

        lax.fori_loop(0, qi + 1, body, 0)
        for hh in heads:
            acc = acc_scr[hh]
            o = acc[:, 0:V_HEAD_DIM] / acc[:, V_HEAD_DIM:V_HEAD_DIM + 1]
            o_ref[qi * tq:(qi + 1) * tq, hh * V_HEAD_DIM:(hh + 1) * V_HEAD_DIM] = o.astype(BF16)
        if has_next:
            diagonal(qi + 1)


def _attention(q, k, v, batch, seq):
    t = q.shape[0]
    tq = min(TQ_ATTN, seq)
    nq = seq // tq
    hw = _HEADS_PER_STEP * HEAD_PAD
    ow = _HEADS_PER_STEP * V_HEAD_DIM
    blk = lambda w: pl.BlockSpec((seq, w), lambda b, j: (b, j))
    return pl.pallas_call(
        functools.partial(_attn_kernel, tq),
        grid=(batch, MLA_HEADS // _HEADS_PER_STEP),
        in_specs=[blk(hw), blk(hw), blk(hw)],
        out_specs=blk(ow),
        out_shape=jax.ShapeDtypeStruct((t, MLA_HEADS * V_HEAD_DIM), BF16),
        scratch_shapes=[pltpu.VMEM((_HEADS_PER_STEP, nq, tq, tq), F32),
                        pltpu.VMEM((_HEADS_PER_STEP, tq, LANES), F32),
                        pltpu.VMEM((_HEADS_PER_STEP, tq, LANES), F32),
                        pltpu.VMEM((_HEADS_PER_STEP, tq, HEAD_PAD), F32)],
        compiler_params=_params(("arbitrary", "arbitrary")),
        name="attn",
    )(q, k, v)


_GROUP_SIZE = N_EXPERTS // N_EXPERT_GROUPS
_BIG = 1.0e9


def _mix_out_kernel(attn_ref, yc_ref, x_ref, mod_ref, gattn_ref, wout_ref, gpost_ref, gpre2_ref,
                    wrt_ref, br_ref, x1_ref, h2_ref, h2p_ref, idx_ref, wts_ref, rank_ref, cnt_ref,
                    carry_scr):
    i = pl.program_id(0)
    tm = x_ref.shape[0]
    half = attn_ref.shape[1]

    @pl.when(i == 0)
    def _():
        carry_scr[...] = jnp.zeros(carry_scr.shape, F32)

    an = _rms(attn_ref[...].astype(F32), gattn_ref[...]).astype(BF16)
    mix = (jnp.dot(an, wout_ref[0:half, :], preferred_element_type=F32)
           + jnp.dot(yc_ref[...], wout_ref[half:, :], preferred_element_type=F32))
    g1 = mod_ref[0, 2:3, :]
    sh2 = mod_ref[0, 3:4, :]
    sc2 = mod_ref[0, 4:5, :]
    x1 = x_ref[...] + g1 * _rms(mix, gpost_ref[...])
    x1_ref[...] = x1
    h2 = _rms(x1, gpre2_ref[...]) * (1.0 + sc2) + sh2
    h2_ref[...] = h2.astype(BF16)
    words = _pack_row_words(h2[:, 0:_ROW_WORDS], h2[:, _ROW_WORDS:])
    h2p_ref[...] = words

    logits = lax.dot_general(wrt_ref[...], h2, (((1,), (1,)), ((), ())),
                             preferred_element_type=F32, precision=lax.Precision.HIGHEST)
    scores = jax.nn.sigmoid(logits)
    sel = scores + br_ref[...]
    row = lax.broadcasted_iota(I32, (N_EXPERTS, tm), 0).astype(F32)

    gscore = []
    rw = lax.broadcasted_iota(I32, (_GROUP_SIZE, tm), 0).astype(F32)
    for g in range(N_EXPERT_GROUPS):
        blk = sel[g * _GROUP_SIZE:(g + 1) * _GROUP_SIZE, :]
        m1 = jnp.max(blk, axis=0, keepdims=True)
        i1 = jnp.min(jnp.where(blk == m1, rw, _BIG), axis=0, keepdims=True)
        m2 = jnp.max(jnp.where(rw == i1, NEG_INF, blk), axis=0, keepdims=True)
        gscore.append(m1 + m2)

    gkeep = [jnp.zeros((1, tm), F32) for _ in range(N_EXPERT_GROUPS)]
    for _ in range(TOPK_GROUPS):
        mg = functools.reduce(jnp.maximum, gscore)
        ig = functools.reduce(jnp.minimum, [jnp.where(gscore[g] == mg, float(g), _BIG)
                                            for g in range(N_EXPERT_GROUPS)])
        for g in range(N_EXPERT_GROUPS):
            hit = ig == float(g)
            gkeep[g] = jnp.where(hit, 1.0, gkeep[g])
            gscore[g] = jnp.where(hit, NEG_INF, gscore[g])
    cur = jnp.concatenate(
        [jnp.where(gkeep[g] > 0.0, sel[g * _GROUP_SIZE:(g + 1) * _GROUP_SIZE, :], NEG_INF)
         for g in range(N_EXPERT_GROUPS)], axis=0)

    krow = lax.broadcasted_iota(I32, (TOP_K, tm), 0)
    idx_rows = []
    idx_f = jnp.zeros((TOP_K, tm), F32)
    sc_k = jnp.zeros((TOP_K, tm), F32)
    sc_sum = jnp.zeros((1, tm), F32)
    onehot = jnp.zeros((N_EXPERTS, tm), F32)
    for k in range(TOP_K):
        m = jnp.max(cur, axis=0, keepdims=True)
        ik = jnp.min(jnp.where(cur == m, row, _BIG), axis=0, keepdims=True)
        hit = row == ik
        sk = jnp.sum(jnp.where(hit, scores, 0.0), axis=0, keepdims=True)
        cur = jnp.where(hit, NEG_INF, cur)
        onehot = jnp.where(hit, 1.0, onehot)
        idx_rows.append(ik)
        idx_f = jnp.where(krow == k, ik, idx_f)
        sc_k = jnp.where(krow == k, sk, sc_k)
        sc_sum = sc_sum + sk
    wts_ref[...] = sc_k / sc_sum * ROUTED_SCALE
    idx_ref[...] = idx_f.astype(I32)

    tri = (lax.broadcasted_iota(I32, (tm, tm), 0) < lax.broadcasted_iota(I32, (tm, tm), 1))
    excl = jnp.dot(onehot.astype(BF16), tri.astype(BF16), preferred_element_type=F32)
    rank_e = carry_scr[:, 0:1] + excl
    rank_k = jnp.zeros((TOP_K, tm), F32)
    for k in range(TOP_K):
        hit = row == idx_rows[k]
        rk = jnp.sum(jnp.where(hit, rank_e, 0.0), axis=0, keepdims=True)
        rank_k = jnp.where(krow == k, rk, rank_k)
    rank_ref[...] = rank_k.astype(I32)
    carry_scr[...] = carry_scr[...] + jnp.sum(onehot, axis=1, keepdims=True)
    cnt_ref[...] = carry_scr[...]


def _mix_out(attn, yc, x2, mod3, gattn, wout, gpost, gpre2, wrt, br, seq):
    t, d = x2.shape
    tm = min(TM_OUT, seq)
    tpb = seq // tm
    full = lambda a: pl.BlockSpec(a.shape, lambda i: (0,) * a.ndim)
    row = lambda w: pl.BlockSpec((tm, w), lambda i: (i, 0))
    col = pl.BlockSpec((TOP_K, tm), lambda i: (0, i))
    return pl.pallas_call(
        _mix_out_kernel,
        grid=(t // tm,),
        in_specs=[row(attn.shape[1]), row(yc.shape[1]), row(d),
                  pl.BlockSpec((1, 6, d), lambda i: (i // tpb, 0, 0)),
                  full(gattn), full(wout), full(gpost), full(gpre2), full(wrt), full(br)],
        out_specs=[row(d), row(d), row(_ROW_WORDS), col, col, col,
                   pl.BlockSpec((N_EXPERTS, LANES), lambda i: (0, 0))],
        out_shape=[jax.ShapeDtypeStruct((t, d), F32), jax.ShapeDtypeStruct((t, d), BF16),
                   jax.ShapeDtypeStruct((t, _ROW_WORDS), U32),
                   jax.ShapeDtypeStruct((TOP_K, t), I32), jax.ShapeDtypeStruct((TOP_K, t), F32),
                   jax.ShapeDtypeStruct((TOP_K, t), I32),
                   jax.ShapeDtypeStruct((N_EXPERTS, LANES), F32)],
        scratch_shapes=[pltpu.VMEM((N_EXPERTS, LANES), F32)],
        compiler_params=_params(("arbitrary",)),
        name="mix_out",
    )(attn, yc, x2, mod3, gattn, wout, gpost, gpre2, wrt, br)


def _dest_kernel(idx_ref, rank_ref, pstart_ref, dest_ref):
    tm = idx_ref.shape[1]
    row = lax.broadcasted_iota(I32, (N_EXPERTS, tm), 0)
    krow = lax.broadcasted_iota(I32, (TOP_K, tm), 0)
    pstart = pstart_ref[...]
    idx = idx_ref[...]
    out = jnp.zeros((TOP_K, tm), F32)
    for k in range(TOP_K):
        hit = row == idx[k:k + 1, :]
        base = jnp.sum(jnp.where(hit, pstart, 0.0), axis=0, keepdims=True)
        out = jnp.where(krow == k, base, out)
    dest_ref[...] = out.astype(I32) + rank_ref[...]


def _dest(idx, rank, pstart):
    t = idx.shape[1]
    tm = min(TM_DEST, t)
    col = pl.BlockSpec((TOP_K, tm), lambda i: (0, i))
    return pl.pallas_call(
        _dest_kernel,
        grid=(t // tm,),
        in_specs=[col, col, pl.BlockSpec((N_EXPERTS, 1), lambda i: (0, 0))],
        out_specs=col,
        out_shape=jax.ShapeDtypeStruct((TOP_K, t), I32),
        compiler_params=_params(("arbitrary",)),
        name="dest",
    )(idx, rank, pstart)


_PAD_CHUNKS = tuple(BM_EXPERT >> s for s in range(1, BM_EXPERT.bit_length()))


SC_CORES = 2
SC_SUBCORES = 16
SC_ROWS = 128


def _sc_worker_split(n_chunks):
    workers = SC_CORES * SC_SUBCORES
    per_worker = max(1, n_chunks // workers)
    active = n_chunks // per_worker
    assert active * per_worker == n_chunks and active <= workers
    return per_worker, active


def _scatter_rows_sc(rows, idx2d, n_out):
    n, width = rows.shape
    n_chunks = n // SC_ROWS
    per_worker, active = _sc_worker_split(n_chunks)
    mesh = plsc.VectorSubcoreMesh(core_axis_name="c", subcore_axis_name="s")

    @functools.partial(
        pl.kernel, mesh=mesh, out_type=jax.ShapeDtypeStruct((n_out, width), rows.dtype),
        scratch_types=[pltpu.VMEM((1, SC_ROWS), I32), pltpu.VMEM((SC_ROWS, width), rows.dtype)])
    def scatter(rows_hbm, idx_hbm, out_hbm, idx_v, rows_v):
        wid = lax.axis_index("s") * SC_CORES + lax.axis_index("c")

        @pl.when(wid < active)
        def _():
            @pl.loop(0, per_worker)
            def _(c):
                chunk = wid * per_worker + c
                pltpu.sync_copy(rows_hbm.at[pl.ds(chunk * SC_ROWS, SC_ROWS)], rows_v)
                for k in range(TOP_K):
                    pltpu.sync_copy(idx_hbm.at[pl.ds(k * n_chunks + chunk, 1)], idx_v)
                    pltpu.sync_copy(rows_v, out_hbm.at[idx_v.at[0]])

    return scatter(rows, idx2d)


def _padfill_kernel(pad_from_ref, pad_n_ref, xs_in_hbm, xs_hbm, zero_scr, pad_sem):
    del xs_in_hbm
    zero_scr[...] = jnp.zeros(zero_scr.shape, zero_scr.dtype)

    def pad_copies(e, act):
        n = pad_n_ref[e]
        base = pad_from_ref[e]

        def single_rows(start, count):
            for j in range(SUBLANES - 1):
                @pl.when(j < count)
                def _():
                    act(pltpu.make_async_copy(zero_scr.at[pl.ds(0, 1), :],
                                              xs_hbm.at[pl.ds(start + j, 1), :], pad_sem))

        head = jnp.minimum(n, (SUBLANES - (base & (SUBLANES - 1))) & (SUBLANES - 1))
        single_rows(base, head)
        rest = n - head
        mid = base + head
        for rows in _PAD_CHUNKS:
            if rows >= SUBLANES:
                @pl.when((rest & rows) != 0)
                def _():
                    start = pl.multiple_of(mid + (rest & ~(2 * rows - 1)), SUBLANES)
                    act(pltpu.make_async_copy(zero_scr.at[pl.ds(0, rows), :],
                                              xs_hbm.at[pl.ds(start, rows), :], pad_sem))
        single_rows(mid + (rest & ~(SUBLANES - 1)), rest & (SUBLANES - 1))

    def issue_pad(e, carry):
        pad_copies(e, lambda cp: cp.start())
        return carry

    def drain_pad(e, carry):
        pad_copies(e, lambda cp: cp.wait())
        return carry

    lax.fori_loop(0, N_EXPERTS, issue_pad, 0)
    lax.fori_loop(0, N_EXPERTS, drain_pad, 0)


def _padfill(pad_from, pad_n, xs):
    return pl.pallas_call(
        _padfill_kernel,
        grid_spec=pltpu.PrefetchScalarGridSpec(
            num_scalar_prefetch=2,
            grid=(1,),
            in_specs=[pl.BlockSpec(memory_space=pl.ANY)],
            out_specs=pl.BlockSpec(memory_space=pl.ANY),
            scratch_shapes=[pltpu.VMEM((BM_EXPERT // 2, _ROW_WORDS), U32), pltpu.SemaphoreType.DMA]),
        out_shape=jax.ShapeDtypeStruct(xs.shape, xs.dtype),
        input_output_aliases={2: 0},
        compiler_params=_params(("arbitrary",)),
        name="padfill",
    )(pad_from, pad_n, xs)


_XS_SLOTS = 4
_YS_SLOTS = 2
_W_SLOTS = 4
_W_AHEAD = 2


def _expert_kernel(first_ref, ord_ref, uexp_ref, meta_ref, xs_hbm, wg_hbm, wu_hbm, wd_hbm, ys_hbm,
                   xs_buf, ys_buf, wg_buf, wu_buf, wd_buf, act_scr, xs_sem, ys_sem, w_sem):
    i = pl.program_id(0)
    nused = meta_ref[0]
    nexp = meta_ref[1]
    bm = xs_buf.shape[1]

    def xs_copy(b, slot):
        return pltpu.make_async_copy(xs_hbm.at[pl.ds(b * bm, bm), :], xs_buf.at[slot], xs_sem.at[slot])

    def ys_copy(b, slot):
        return pltpu.make_async_copy(ys_buf.at[slot], ys_hbm.at[pl.ds(b * bm, bm), :], ys_sem.at[slot])

    def w_copies(j, slot):
        e = uexp_ref[j]
        return (pltpu.make_async_copy(wg_hbm.at[e], wg_buf.at[slot], w_sem.at[slot, 0]),
                pltpu.make_async_copy(wu_hbm.at[e], wu_buf.at[slot], w_sem.at[slot, 1]),
                pltpu.make_async_copy(wd_hbm.at[e], wd_buf.at[slot], w_sem.at[slot, 2]))

    @pl.when(i == 0)
    def _():
        for s in range(_XS_SLOTS - 1):
            @pl.when(s < nused)
            def _():
                xs_copy(s, s).start()
        for s in range(_W_AHEAD):
            @pl.when(s < nexp)
            def _():
                for cp in w_copies(s, s):
                    cp.start()

    def fetch(b):
        ahead = b + _XS_SLOTS - 1

        @pl.when(ahead < nused)
        def _():
            xs_copy(ahead, ahead % _XS_SLOTS).start()

        j = ord_ref[b]

        @pl.when(first_ref[b] == 1)
        def _():
            for cp in w_copies(j, j % _W_SLOTS):
                cp.wait()
            nxt = j + _W_AHEAD

            @pl.when(nxt < nexp)
            def _():
                for cp in w_copies(nxt, nxt % _W_SLOTS):
                    cp.start()

        xs_copy(b, b % _XS_SLOTS).wait()

    def gate_up(b):
        ws = ord_ref[b] % _W_SLOTS
        lo, hi = _unpack_row_words(xs_buf[b % _XS_SLOTS])
        g = (jnp.dot(lo, wg_buf[ws, 0:_ROW_WORDS, :], preferred_element_type=F32)
             + jnp.dot(hi, wg_buf[ws, _ROW_WORDS:, :], preferred_element_type=F32))
        u = (jnp.dot(lo, wu_buf[ws, 0:_ROW_WORDS, :], preferred_element_type=F32)
             + jnp.dot(hi, wu_buf[ws, _ROW_WORDS:, :], preferred_element_type=F32))
        return g * jax.nn.sigmoid(g) * u

    def down(b, act):
        y = jnp.dot(act, wd_buf[ord_ref[b] % _W_SLOTS], preferred_element_type=F32)
        oslot = b % _YS_SLOTS
        ys_buf[oslot] = _pack_row_words(y[:, 0:_ROW_WORDS], y[:, _ROW_WORDS:])
        ys_copy(b, oslot).start()

    @pl.when(jnp.logical_and(i >= _YS_SLOTS + 1, i <= nused))
    def _():
        ys_copy(i - 1 - _YS_SLOTS, (i - 1) % _YS_SLOTS).wait()

    @pl.when(i == 0)
    def _():
        fetch(i)
        act_scr[...] = gate_up(i)

    @pl.when(jnp.logical_and(i >= 1, i < nused))
    def _():
        fetch(i)
        prev = act_scr[...]
        act_scr[...] = gate_up(i)
        down(i - 1, prev)

    @pl.when(i == nused)
    def _():
        down(i - 1, act_scr[...])
        ys_copy(i - 1, (i - 1) % _YS_SLOTS).wait()

        @pl.when(i >= 2)
        def _():
            ys_copy(i - 2, (i - 2) % _YS_SLOTS).wait()


def _experts(first, ordinal, uexp, meta, xs, w_gate, w_up, w_down):
    p = xs.shape[0]
    d = w_gate.shape[1]
    nb = p // BM_EXPERT
    anyspec = pl.BlockSpec(memory_space=pl.ANY)
    return pl.pallas_call(
        _expert_kernel,
        grid_spec=pltpu.PrefetchScalarGridSpec(
            num_scalar_prefetch=4,
            grid=(nb + 1,),
            in_specs=[anyspec, anyspec, anyspec, anyspec],
            out_specs=anyspec,
            scratch_shapes=[pltpu.VMEM((_XS_SLOTS, BM_EXPERT, _ROW_WORDS), U32),
                            pltpu.VMEM((_YS_SLOTS, BM_EXPERT, _ROW_WORDS), U32),
                            pltpu.VMEM((_W_SLOTS, d, EXPERT_DIM), F32),
                            pltpu.VMEM((_W_SLOTS, d, EXPERT_DIM), F32),
                            pltpu.VMEM((_W_SLOTS, EXPERT_DIM, d), F32),
                            pltpu.VMEM((BM_EXPERT, EXPERT_DIM), F32),
                            pltpu.SemaphoreType.DMA((_XS_SLOTS,)), pltpu.SemaphoreType.DMA((_YS_SLOTS,)),
                            pltpu.SemaphoreType.DMA((_W_SLOTS, 3))]),
        out_shape=jax.ShapeDtypeStruct((p, _ROW_WORDS), U32),
        compiler_params=_params(("arbitrary",)),
        name="experts",
    )(first, ordinal, uexp, meta, xs, w_gate, w_up, w_down)


SC_GATHER_ROWS = 64


def _gather_rows_sc(table, idx):
    n = idx.shape[0]
    width = table.shape[1]
    rows = SC_GATHER_ROWS
    per_worker, active = _sc_worker_split(n // rows)
    assert per_worker % 2 == 0 or per_worker == 1
    mesh = plsc.VectorSubcoreMesh(core_axis_name="c", subcore_axis_name="s")

    @functools.partial(
        pl.kernel, mesh=mesh, out_type=jax.ShapeDtypeStruct((n, width), table.dtype),
        scratch_types=[pltpu.VMEM((per_worker, rows), I32), pltpu.VMEM((2, rows, width), table.dtype),
                       pltpu.SemaphoreType.DMA((2,))])
    def gather(table_hbm, idx_hbm, out_hbm, idx_v, rows_v, sem):
        wid = lax.axis_index("s") * SC_CORES + lax.axis_index("c")

        def fetch(c, b):
            return pltpu.make_async_copy(table_hbm.at[idx_v.at[c]], rows_v.at[b], sem.at[b])

        @pl.when(wid < active)
        def _():
            first = wid * per_worker
            pltpu.sync_copy(idx_hbm.at[pl.ds(first, per_worker)], idx_v)
            fetch(0, 0).start()

            @pl.loop(0, per_worker, step=2)
            def _(c):
                for b in range(min(2, per_worker)):
                    cur = c + b

                    @pl.when(cur + 1 < per_worker)
                    def _():
                        fetch(cur + 1, 1 - b).start()

                    fetch(cur, b).wait()
                    pltpu.sync_copy(rows_v.at[b], out_hbm.at[pl.ds((first + cur) * rows, rows)])

    return gather(table, idx.reshape(n // rows, rows))


def _combine_kernel(wts_ref, yg_ref, h2_ref, x1_ref, mod_ref, wsg_ref, wsu_ref, wsd_ref, gpost_ref, *rest):
    o_ref = rest[-1]
    h2 = h2_ref[...]
    g = jnp.dot(h2, wsg_ref[...], preferred_element_type=F32)
    u = jnp.dot(h2, wsu_ref[...], preferred_element_type=F32)
    f = jnp.dot((g * jax.nn.sigmoid(g) * u).astype(BF16), wsd_ref[...], preferred_element_type=F32)

    wts = wts_ref[...]
    los = [f[:, sl * LANES:(sl + 1) * LANES] for sl in range(_ROW_SLABS)]
    his = [f[:, _ROW_WORDS + sl * LANES:_ROW_WORDS + (sl + 1) * LANES] for sl in range(_ROW_SLABS)]
    for k in range(TOP_K):
        wk = wts[:, k:k + 1]
        for sl in range(_ROW_SLABS):
            lo, hi = _unpack_row_words(yg_ref[k, :, sl * LANES:(sl + 1) * LANES])
            los[sl] = los[sl] + wk * lo
            his[sl] = his[sl] + wk * hi
    f = jnp.concatenate(los + his, axis=1)
    g2 = mod_ref[0, 5:6, :]
    o_ref[...] = x1_ref[...] + g2 * _rms(f, gpost_ref[...])


def _combine(wts_t, yg, h2, x1, mod3, wsg, wsu, wsd, gpost, seq, first_tile, partial_out):
    t, d = x1.shape
    tm = min(TM_COMBINE, seq)
    tpb = seq // tm
    full = lambda a: pl.BlockSpec(a.shape, lambda i: (0,) * a.ndim)
    row = lambda w: pl.BlockSpec((tm, w), lambda i: (i + first_tile, 0))
    args = [wts_t, yg, h2, x1, mod3, wsg, wsu, wsd, gpost]
    in_specs = [row(TOP_K), pl.BlockSpec((TOP_K, tm, _ROW_WORDS), lambda i: (0, i, 0)), row(d), row(d),
                pl.BlockSpec((1, 6, d), lambda i: ((i + first_tile) // tpb, 0, 0)),
                full(wsg), full(wsu), full(wsd), full(gpost)]
    aliases = {}
    if partial_out is not None:
        aliases = {len(args): 0}
        args.append(partial_out)
        in_specs.append(pl.BlockSpec(memory_space=pl.ANY))
    return pl.pallas_call(
        _combine_kernel,
        grid=(yg.shape[1] // tm,),
        in_specs=in_specs,
        out_specs=row(d),
        out_shape=jax.ShapeDtypeStruct((t, d), F32),
        input_output_aliases=aliases,
        compiler_params=_params(("arbitrary",)),
        name="combine",
    )(*args)


def _pack_weights(w_in, w_uq, w_ukv):
    d = w_in.shape[0]
    half = QK_ROPE_DIM // 2
    z = lambda n, c: jnp.zeros((n, c), F32)
    o = Q_LORA_RANK + KV_LORA_RANK
    kr = w_in[:, o:o + QK_ROPE_DIM]
    kr_grp = jnp.concatenate([z(d, QK_NOPE_DIM), kr, z(d, HEAD_PAD - QK_NOPE_DIM - QK_ROPE_DIM)], axis=1)
    kr_rot = jnp.concatenate([z(d, QK_NOPE_DIM), -kr[:, half:], kr[:, :half],
                              z(d, HEAD_PAD - QK_NOPE_DIM - QK_ROPE_DIM)], axis=1)
    win_p = jnp.concatenate([w_in[:, :o], kr_grp, kr_rot, w_in[:, o + QK_ROPE_DIM:]], axis=1)

    scale = float(QK_NOPE_DIM + QK_ROPE_DIM) ** -0.5 * float(np.log2(np.e))
    r = Q_LORA_RANK
    qd = QK_NOPE_DIM + QK_ROPE_DIM
    q_grp, q_rot = [], []
    for h in range(MLA_HEADS):
        nope = w_uq[:, h * qd:h * qd + QK_NOPE_DIM]
        rope = w_uq[:, h * qd + QK_NOPE_DIM:(h + 1) * qd]
        pad = z(r, HEAD_PAD - qd)
        q_grp.append(jnp.concatenate([nope, rope, pad], axis=1))
        q_rot.append(jnp.concatenate([z(r, QK_NOPE_DIM), -rope[:, half:], rope[:, :half], pad], axis=1))
    wuq_p = jnp.concatenate(q_grp + q_rot, axis=1) * scale

    c = KV_LORA_RANK
    kd = QK_NOPE_DIM + V_HEAD_DIM
    k_grp, v_grp = [], []
    for h in range(MLA_HEADS):
        k_grp.append(jnp.concatenate([w_ukv[:, h * kd:h * kd + QK_NOPE_DIM], z(c, HEAD_PAD - QK_NOPE_DIM)], axis=1))
        v_grp.append(jnp.concatenate([w_ukv[:, h * kd + QK_NOPE_DIM:(h + 1) * kd], z(c, HEAD_PAD - V_HEAD_DIM)], axis=1))
    wukv_p = jnp.concatenate(k_grp + v_grp, axis=1)
    return win_p.astype(BF16), wuq_p.astype(BF16), wukv_p.astype(BF16)


def _rope_tables(positions):
    inv = 1.0 / (ROPE_THETA ** (jnp.arange(0, QK_ROPE_DIM, 2, dtype=F32) / QK_ROPE_DIM))
    ang = positions.astype(F32).reshape(-1)[:, None] * inv
    t = ang.shape[0]
    cos, sin = jnp.cos(ang), jnp.sin(ang)
    tail = HEAD_PAD - QK_NOPE_DIM - QK_ROPE_DIM
    cos_t = jnp.concatenate([jnp.ones((t, QK_NOPE_DIM), F32), cos, cos, jnp.ones((t, tail), F32)], axis=1)
    sin_t = jnp.concatenate([jnp.zeros((t, QK_NOPE_DIM), F32), sin, sin, jnp.zeros((t, tail), F32)], axis=1)
    return cos_t, sin_t


def _layer(x2, c, cos_t, sin_t, batch, seq, w_ada, b_ada, g_pre_mix, w_in, g_q_lat, w_uq, g_kv_lat, w_ukv,
           w_conv, g_attn_out, g_conv_out, w_out, g_post_mix, g_pre_ffn, w_router, b_router,
           w_gate, w_up, w_down, w_sh_gate, w_sh_up, w_sh_down, g_post_ffn):
    t, d = x2.shape
    r1 = lambda a: a.reshape(1, -1)

    c_pad = jnp.zeros((SUBLANES, d), F32).at[:batch].set(c)
    mod = _ada(c_pad, w_ada, r1(b_ada))[:batch]
    mod3 = mod.reshape(batch, 6, d)

    win_p, wuq_p, wukv_p = _pack_weights(w_in, w_uq, w_ukv)
    vone = jnp.zeros((1, HEAD_PAD), F32).at[0, V_HEAD_DIM].set(1.0)
    q, k, v, yc = _mix_in(x2, mod3, r1(g_pre_mix), win_p, r1(g_q_lat), wuq_p, r1(g_kv_lat), wukv_p,
                          vone, w_conv, r1(g_conv_out), cos_t, sin_t, seq)
    attn = _attention(q, k, v, batch, seq)
    x1, h2, h2p, idx, wts, rank, cnt = _mix_out(
        attn, yc, x2, mod3, r1(g_attn_out), w_out.astype(BF16), r1(g_post_mix), r1(g_pre_ffn),
        w_router.T, b_router.reshape(-1, 1), seq)

    counts = cnt[:, 0].astype(I32)
    padded = ((counts + BM_EXPERT - 1) // BM_EXPERT) * BM_EXPERT
    pad_end = jnp.cumsum(padded)
    pad_start = pad_end - padded
    m = t * TOP_K
    nb = (m + N_EXPERTS * (BM_EXPERT - 1)) // BM_EXPERT
    nused = pad_end[-1] // BM_EXPERT
    bidx = jnp.arange(nb, dtype=I32)
    blk_exp = jnp.sum((pad_end[None, :] <= (bidx * BM_EXPERT)[:, None]).astype(I32), axis=1)
    first = ((bidx < nused) & ((bidx == 0) | (blk_exp != jnp.roll(blk_exp, 1)))).astype(I32)
    ordinal = jnp.maximum(jnp.cumsum(first) - 1, 0).astype(I32)
    seen = jnp.cumsum((counts > 0).astype(I32))
    uexp = jnp.minimum(jnp.sum((seen[None, :] <= jnp.arange(N_EXPERTS, dtype=I32)[:, None]).astype(I32), axis=1),
                       N_EXPERTS - 1).astype(I32)
    meta = jnp.stack([nused, seen[-1]]).astype(I32)

    dest = _dest(idx, rank, pad_start.astype(F32).reshape(-1, 1))
    xs = _scatter_rows_sc(h2p, dest.reshape(-1, SC_ROWS), nb * BM_EXPERT)
    xs = _padfill((pad_start + counts).astype(I32), (padded - counts).astype(I32), xs)
    ys = _experts(first, ordinal, uexp, meta, xs, w_gate, w_up, w_down)
    parts = 2 if t % (2 * TM_COMBINE * SC_CORES * SC_SUBCORES) == 0 else 1
    tp = t // parts
    wsg, wsu, wsd = w_sh_gate.astype(BF16), w_sh_up.astype(BF16), w_sh_down.astype(BF16)
    out = None
    for part in range(parts):
        idx = dest[:, part * tp:(part + 1) * tp].reshape(-1)
        yg = _gather_rows_sc(ys, idx).reshape(TOP_K, tp, _ROW_WORDS)
        out = _combine(wts.T, yg, h2, x1, mod3, wsg, wsu, wsd, r1(g_post_ffn), seq,
                       part * tp // min(TM_COMBINE, seq), out)
    return out


def kernel(x, c, positions, w_ada, b_ada, g_pre_mix, w_in, g_q_lat, w_uq, g_kv_lat, w_ukv, w_conv, g_attn_out, g_conv_out, w_out, g_post_mix, g_pre_ffn, w_router, b_router, w_gate, w_up, w_down, w_sh_gate, w_sh_up, w_sh_down, g_post_ffn):
    batch, seq, d = x.shape
    cos_t, sin_t = _rope_tables(positions)
    x2 = x.reshape(batch * seq, d)
    for l in range(w_ada.shape[0]):
        x2 = _layer(x2, c, cos_t, sin_t, batch, seq, w_ada[l], b_ada[l], g_pre_mix[l], w_in[l], g_q_lat[l],
                    w_uq[l], g_kv_lat[l], w_ukv[l], w_conv[l], g_attn_out[l], g_conv_out[l], w_out[l],
                    g_post_mix[l], g_pre_ffn[l], w_router[l], b_router[l], w_gate[l], w_up[l], w_down[l],
                    w_sh_gate[l], w_sh_up[l], w_sh_down[l], g_post_ffn[l])
    return x2.reshape(batch, seq, d)
```

```python
import functools

import jax
import jax.numpy as jnp
import numpy as np
from jax import lax
from jax.experimental import pallas as pl
from jax.experimental.pallas import tpu as pltpu
from jax.experimental.pallas import tpu_sc as plsc

F32 = jnp.float32
BF16 = jnp.bfloat16
I32 = jnp.int32
U32 = jnp.uint32

CHUNK = 64
MLA_HEADS = 8
QK_NOPE_DIM = 64
QK_ROPE_DIM = 32
V_HEAD_DIM = 64
Q_LORA_RANK = 384
KV_LORA_RANK = 256
ROPE_THETA = 10000.0
CONV_WIDTH = 3
N_EXPERTS = 256
TOP_K = 8
N_EXPERT_GROUPS = 8
TOPK_GROUPS = 4
EXPERT_DIM = 256
ROUTED_SCALE = 2.5
EPS = 1e-6

LANES = 128
SUBLANES = 8
HEAD_PAD = LANES
VMEM_LIMIT_BYTES = 56 * 1024 * 1024

TM_IN = 512
TQ_ATTN = 512
TM_OUT = 512
TM_DEST = 512
TM_DISPATCH = 512
BM_EXPERT = 256
TM_COMBINE = 256

NEG_INF = float("-inf")


def _rms(x, g):
    return x * lax.rsqrt(jnp.mean(x * x, axis=-1, keepdims=True) + EPS) * g


_HI_MASK = np.uint32(0xFFFF0000)
_ROW_WORDS = 512
_ROW_SLABS = _ROW_WORDS // LANES


def _pack_row_words(lo, hi):
    lo_w = lax.bitcast_convert_type(lo.astype(BF16).astype(F32), U32) >> 16
    hi_w = lax.bitcast_convert_type(hi.astype(BF16).astype(F32), U32) & _HI_MASK
    return lo_w | hi_w


def _unpack_row_words(w):
    return (lax.bitcast_convert_type(w << 16, F32), lax.bitcast_convert_type(w & _HI_MASK, F32))


def _params(sem):
    return pltpu.CompilerParams(dimension_semantics=sem, vmem_limit_bytes=VMEM_LIMIT_BYTES)


def _ada_kernel(c_ref, w_ref, b_ref, o_ref):
    c = c_ref[...]
    s = c * jax.nn.sigmoid(c)
    o_ref[...] = jnp.dot(s, w_ref[...], preferred_element_type=F32,
                         precision=lax.Precision.HIGHEST) + b_ref[...]


def _ada(c_pad, w, b):
    rows, d = c_pad.shape
    n = w.shape[1]
    tn = 1536
    return pl.pallas_call(
        _ada_kernel,
        grid=(n // tn,),
        in_specs=[pl.BlockSpec((rows, d), lambda j: (0, 0)),
                  pl.BlockSpec((d, tn), lambda j: (0, j)),
                  pl.BlockSpec((1, tn), lambda j: (0, j))],
        out_specs=pl.BlockSpec((rows, tn), lambda j: (0, j)),
        out_shape=jax.ShapeDtypeStruct((rows, n), F32),
        compiler_params=_params(("arbitrary",)),
        name="ada",
    )(c_pad, w, b)


_CQ0, _CQ1 = 0, Q_LORA_RANK
_CKV0, _CKV1 = _CQ1, _CQ1 + KV_LORA_RANK
_KR0, _KR1 = _CKV1, _CKV1 + 2 * HEAD_PAD
_CONV_DIM = 512
_GB0 = _KR1
_GC0 = _GB0 + _CONV_DIM
_XV0 = _GC0 + _CONV_DIM
_WIN_COLS = _XV0 + _CONV_DIM
_QW = MLA_HEADS * HEAD_PAD


def _mix_in_kernel(tiles_per_batch, x_ref, mod_ref, gpre_ref, win_ref, gq_ref, wuq_ref, gkv_ref,
                   wukv_ref, vone_ref, wconv_ref, gconv_ref, pos_ref, freq_ref,
                   q_ref, k_ref, v_ref, yc_ref, h_scr, u_scr):
    i = pl.program_id(0)
    tm = x_ref.shape[0]
    sh1 = mod_ref[0, 0:1, :]
    sc1 = mod_ref[0, 1:2, :]
    h = _rms(x_ref[...], gpre_ref[...]) * (1.0 + sc1) + sh1
    h_scr[...] = h.astype(BF16)
    ang = pos_ref[...] * freq_ref[...]
    cos = jnp.cos(ang)
    sin = jnp.sin(ang)

    cq = jnp.dot(h_scr[...], win_ref[:, _CQ0:_CQ1], preferred_element_type=F32)
    cqn = _rms(cq, gq_ref[...]).astype(BF16)
    qq = jnp.dot(cqn, wuq_ref[...], preferred_element_type=F32)
    for hd in range(MLA_HEADS):
        lo = hd * HEAD_PAD
        qh = qq[:, lo:lo + HEAD_PAD] * cos + qq[:, _QW + lo:_QW + lo + HEAD_PAD] * sin
        q_ref[:, lo:lo + HEAD_PAD] = qh.astype(BF16)

    ckv = jnp.dot(h_scr[...], win_ref[:, _CKV0:_CKV1], preferred_element_type=F32)
    ckvn = _rms(ckv, gkv_ref[...]).astype(BF16)
    kv = jnp.dot(ckvn, wukv_ref[...], preferred_element_type=F32)
    krr = jnp.dot(h_scr[...], win_ref[:, _KR0:_KR1], preferred_element_type=F32)
    kr = krr[:, 0:HEAD_PAD] * cos + krr[:, HEAD_PAD:2 * HEAD_PAD] * sin
    vone = vone_ref[...]
    for hd in range(MLA_HEADS):
        lo = hd * HEAD_PAD
        k_ref[:, lo:lo + HEAD_PAD] = (kv[:, lo:lo + HEAD_PAD] + kr).astype(BF16)
        v_ref[:, lo:lo + HEAD_PAD] = (kv[:, _QW + lo:_QW + lo + HEAD_PAD] + vone).astype(BF16)

    gb = jnp.dot(h_scr[...], win_ref[:, _GB0:_GC0], preferred_element_type=F32)
    gc = jnp.dot(h_scr[...], win_ref[:, _GC0:_XV0], preferred_element_type=F32)
    xv = jnp.dot(h_scr[...], win_ref[:, _XV0:_WIN_COLS], preferred_element_type=F32)
    u = gc * xv
    prev = u_scr[tm:tm + SUBLANES, :]
    first = (i % tiles_per_batch) == 0
    u_scr[0:SUBLANES, :] = jnp.where(first, jnp.zeros_like(prev), prev)
    u_scr[SUBLANES:tm + SUBLANES, :] = u
    um1 = u_scr[SUBLANES - 1:tm + SUBLANES - 1, :]
    um2 = u_scr[SUBLANES - 2:tm + SUBLANES - 2, :]
    conv = wconv_ref[0:1, :] * um2 + wconv_ref[1:2, :] * um1 + wconv_ref[2:3, :] * u
    yc_ref[...] = _rms(gb * conv, gconv_ref[...]).astype(BF16)


def _mix_in(x2, mod3, gpre, win_p, gq, wuq_p, gkv, wukv_p, vone, wconv, gconv, pos, freq, seq):
    t, d = x2.shape
    tm = min(TM_IN, seq)
    tpb = seq // tm
    full = lambda a: pl.BlockSpec(a.shape, lambda i: (0,) * a.ndim)
    row = lambda w: pl.BlockSpec((tm, w), lambda i: (i, 0))
    return pl.pallas_call(
        functools.partial(_mix_in_kernel, tpb),
        grid=(t // tm,),
        in_specs=[row(d),
                  pl.BlockSpec((1, 6, d), lambda i: (i // tpb, 0, 0)),
                  full(gpre), full(win_p), full(gq), full(wuq_p), full(gkv), full(wukv_p),
                  full(vone), full(wconv), full(gconv), row(1), full(freq)],
        out_specs=[row(_QW), row(_QW), row(_QW), row(_CONV_DIM)],
        out_shape=[jax.ShapeDtypeStruct((t, _QW), BF16), jax.ShapeDtypeStruct((t, _QW), BF16),
                   jax.ShapeDtypeStruct((t, _QW), BF16), jax.ShapeDtypeStruct((t, _CONV_DIM), BF16)],
        scratch_shapes=[pltpu.VMEM((tm, d), BF16), pltpu.VMEM((tm + SUBLANES, _CONV_DIM), F32)],
        compiler_params=_params(("arbitrary",)),
        name="mix_in",
    )(x2, mod3, gpre, win_p, gq, wuq_p, gkv, wukv_p, vone, wconv, gconv, pos, freq)


_HEADS_PER_STEP = 2


def _attn_kernel(tq, q_ref, k_ref, v_ref, o_ref, s_scr, mrun_scr, mb_scr, acc_scr):
    tk = tq
    nq = q_ref.shape[0] // tq
    lane_groups = tk // LANES
    heads = range(_HEADS_PER_STEP)
    lanes = [slice(hh * HEAD_PAD, (hh + 1) * HEAD_PAD) for hh in heads]

    def tile_max(s):
        m = s[:, 0:LANES]
        for g in range(1, lane_groups):
            m = jnp.maximum(m, s[:, g * LANES:(g + 1) * LANES])
        return m

    def scores(hh, qi, kv):
        off = pl.multiple_of(kv * tk, tk)
        return lax.dot_general(q_ref[qi * tq:(qi + 1) * tq, lanes[hh]], k_ref[pl.ds(off, tk), lanes[hh]],
                               (((1,), (1,)), ((), ())), preferred_element_type=F32)

    rc = lax.broadcasted_iota(I32, (tq, tk), 0) // CHUNK
    cc = lax.broadcasted_iota(I32, (tq, tk), 1) // CHUNK

    def diagonal(qi):
        for hh in heads:
            s = jnp.where(cc <= rc, scores(hh, qi, qi), NEG_INF)
            s_scr[hh, qi] = s
            m_row = jnp.max(jnp.maximum(mrun_scr[hh], tile_max(s)), axis=1, keepdims=True)
            mb_scr[hh] = jnp.broadcast_to(m_row, (tq, LANES))

    mrun_scr[...] = jnp.full(mrun_scr.shape, NEG_INF, F32)
    diagonal(0)
    for qi in range(nq):
        has_next = qi + 1 < nq
        acc_scr[...] = jnp.zeros(acc_scr.shape, F32)
        if has_next:
            mrun_scr[...] = jnp.full(mrun_scr.shape, NEG_INF, F32)

        def body(kv, carry, qi=qi, has_next=has_next):
            off = pl.multiple_of(kv * tk, tk)
            for hh in heads:
                mb = mb_scr[hh]
                p = jnp.concatenate(
                    [jnp.exp2(s_scr[hh, kv, :, g * LANES:(g + 1) * LANES] - mb) for g in range(lane_groups)],
                    axis=1).astype(BF16)
                acc_scr[hh] += jnp.dot(p, v_ref[pl.ds(off, tk), lanes[hh]], preferred_element_type=F32)
                if has_next:
                    s = scores(hh, qi + 1, kv)
                    s_scr[hh, kv] = s
                    mrun_scr[hh] = jnp.maximum(mrun_scr[hh], tile_max(s))
            return carry

        lax.fori_loop(0, qi + 1, body, 0)
        for hh in heads:
            acc = acc_scr[hh]
            o = acc[:, 0:V_HEAD_DIM] / acc[:, V_HEAD_DIM:V_HEAD_DIM + 1]
            o_ref[qi * tq:(qi + 1) * tq, hh * V_HEAD_DIM:(hh + 1) * V_HEAD_DIM] = o.astype(BF16)
        if has_next:
            diagonal(qi + 1)


def _attention(q, k, v, batch, seq):
    t = q.shape[0]
    tq = min(TQ_ATTN, seq)
    nq = seq // tq
    hw = _HEADS_PER_STEP * HEAD_PAD
    ow = _HEADS_PER_STEP * V_HEAD_DIM
    blk = lambda w: pl.BlockSpec((seq, w), lambda b, j: (b, j))
    return pl.pallas_call(
        functools.partial(_attn_kernel, tq),
        grid=(batch, MLA_HEADS // _HEADS_PER_STEP),
        in_specs=[blk(hw), blk(hw), blk(hw)],
        out_specs=blk(ow),
        out_shape=jax.ShapeDtypeStruct((t, MLA_HEADS * V_HEAD_DIM), BF16),
        scratch_shapes=[pltpu.VMEM((_HEADS_PER_STEP, nq, tq, tq), F32),
                        pltpu.VMEM((_HEADS_PER_STEP, tq, LANES), F32),
                        pltpu.VMEM((_HEADS_PER_STEP, tq, LANES), F32),
                        pltpu.VMEM((_HEADS_PER_STEP, tq, HEAD_PAD), F32)],
        compiler_params=_params(("arbitrary", "arbitrary")),
        name="attn",
    )(q, k, v)


_GROUP_SIZE = N_EXPERTS // N_EXPERT_GROUPS
_BIG = 1.0e9


def _mix_out_kernel(attn_ref, yc_ref, x_ref, mod_ref, gattn_ref, wout_ref, gpost_ref, gpre2_ref,
                    wrt_ref, br_ref, x1_ref, h2_ref, h2p_ref, idx_ref, wts_ref, rank_ref, cnt_ref,
                    carry_scr):
    i = pl.program_id(0)
    tm = x_ref.shape[0]
    half = attn_ref.shape[1]

    @pl.when(i == 0)
    def _():
        carry_scr[...] = jnp.zeros(carry_scr.shape, F32)

    an = _rms(attn_ref[...].astype(F32), gattn_ref[...]).astype(BF16)
    mix = (jnp.dot(an, wout_ref[0:half, :], preferred_element_type=F32)
           + jnp.dot(yc_ref[...], wout_ref[half:, :], preferred_element_type=F32))
    g1 = mod_ref[0, 2:3, :]
    sh2 = mod_ref[0, 3:4, :]
    sc2 = mod_ref[0, 4:5, :]
    x1 = x_ref[...] + g1 * _rms(mix, gpost_ref[...])
    x1_ref[...] = x1
    h2 = _rms(x1, gpre2_ref[...]) * (1.0 + sc2) + sh2
    h2_ref[...] = h2.astype(BF16)
    words = _pack_row_words(h2[:, 0:_ROW_WORDS], h2[:, _ROW_WORDS:])
    h2p_ref[...] = words

    logits = lax.dot_general(wrt_ref[...], h2, (((1,), (1,)), ((), ())),
                             preferred_element_type=F32, precision=lax.Precision.HIGHEST)
    scores = jax.nn.sigmoid(logits)
    sel = scores + br_ref[...]
    row = lax.broadcasted_iota(I32, (N_EXPERTS, tm), 0).astype(F32)

    gscore = []
    rw = lax.broadcasted_iota(I32, (_GROUP_SIZE, tm), 0).astype(F32)
    for g in range(N_EXPERT_GROUPS):
        blk = sel[g * _GROUP_SIZE:(g + 1) * _GROUP_SIZE, :]
        m1 = jnp.max(blk, axis=0, keepdims=True)
        i1 = jnp.min(jnp.where(blk == m1, rw, _BIG), axis=0, keepdims=True)
        m2 = jnp.max(jnp.where(rw == i1, NEG_INF, blk), axis=0, keepdims=True)
        gscore.append(m1 + m2)

    gkeep = [jnp.zeros((1, tm), F32) for _ in range(N_EXPERT_GROUPS)]
    for _ in range(TOPK_GROUPS):
        mg = functools.reduce(jnp.maximum, gscore)
        ig = functools.reduce(jnp.minimum, [jnp.where(gscore[g] == mg, float(g), _BIG)
                                            for g in range(N_EXPERT_GROUPS)])
        for g in range(N_EXPERT_GROUPS):
            hit = ig == float(g)
            gkeep[g] = jnp.where(hit, 1.0, gkeep[g])
            gscore[g] = jnp.where(hit, NEG_INF, gscore[g])
    n_slabs = N_EXPERTS // SUBLANES
    slabs_per_group = _GROUP_SIZE // SUBLANES
    sub = lax.broadcasted_iota(I32, (SUBLANES, tm), 0).astype(F32)
    first_rows = [jnp.where(gkeep[j // slabs_per_group] > 0.0, sel[j * SUBLANES:(j + 1) * SUBLANES, :], NEG_INF)
                  for j in range(n_slabs)]
    cur_rows = list(first_rows)
    krow = lax.broadcasted_iota(I32, (TOP_K, tm), 0)
    idx_rows = []
    idx_f = jnp.zeros((TOP_K, tm), F32)
    sc_k = jnp.zeros((TOP_K, tm), F32)
    sc_sum = jnp.zeros((1, tm), F32)
    prev = None
    for k in range(TOP_K):
        best = jnp.full((SUBLANES, tm), NEG_INF, F32)
        best_slab = jnp.zeros((SUBLANES, tm), F32)
        best_score = jnp.zeros((SUBLANES, tm), F32)
        for j in range(n_slabs):
            if prev is not None:
                cur_rows[j] = jnp.where(sub == prev - float(j * SUBLANES), NEG_INF, cur_rows[j])
            better = cur_rows[j] > best
            best = jnp.where(better, cur_rows[j], best)
            best_slab = jnp.where(better, float(j), best_slab)
            best_score = jnp.where(better, scores[j * SUBLANES:(j + 1) * SUBLANES, :], best_score)
        best_idx = best_slab * float(SUBLANES) + sub
        m = jnp.max(best, axis=0, keepdims=True)
        ik = jnp.min(jnp.where(best == m, best_idx, _BIG), axis=0, keepdims=True)
        sk = jnp.sum(jnp.where(best_idx == ik, best_score, 0.0), axis=0, keepdims=True)
        prev = ik
        idx_rows.append(ik)
        idx_f = jnp.where(krow == k, ik, idx_f)
        sc_k = jnp.where(krow == k, sk, sc_k)
        sc_sum = sc_sum + sk
    wts_ref[...] = sc_k / sc_sum * ROUTED_SCALE
    idx_ref[...] = idx_f.astype(I32)
    onehot = jnp.concatenate(
        [jnp.where(jnp.where(sub == prev - float(j * SUBLANES), NEG_INF, cur_rows[j]) != first_rows[j], 1.0, 0.0)
         for j in range(n_slabs)], axis=0)

    tri = (lax.broadcasted_iota(I32, (tm, tm), 0) < lax.broadcasted_iota(I32, (tm, tm), 1))
    excl = jnp.dot(onehot.astype(BF16), tri.astype(BF16), preferred_element_type=F32)
    rank_e = carry_scr[:, 0:1] + excl
    rank_k = jnp.zeros((TOP_K, tm), F32)
    for k in range(TOP_K):
        hit = row == idx_rows[k]
        rk = jnp.sum(jnp.where(hit, rank_e, 0.0), axis=0, keepdims=True)
        rank_k = jnp.where(krow == k, rk, rank_k)
    rank_ref[...] = rank_k.astype(I32)
    carry_scr[...] = carry_scr[...] + jnp.sum(onehot, axis=1, keepdims=True)
    cnt_ref[...] = carry_scr[...]


def _mix_out(attn, yc, x2, mod3, gattn, wout, gpost, gpre2, wrt, br, seq):
    t, d = x2.shape
    tm = min(TM_OUT, seq)
    tpb = seq // tm
    full = lambda a: pl.BlockSpec(a.shape, lambda i: (0,) * a.ndim)
    row = lambda w: pl.BlockSpec((tm, w), lambda i: (i, 0))
    col = pl.BlockSpec((TOP_K, tm), lambda i: (0, i))
    return pl.pallas_call(
        _mix_out_kernel,
        grid=(t // tm,),
        in_specs=[row(attn.shape[1]), row(yc.shape[1]), row(d),
                  pl.BlockSpec((1, 6, d), lambda i: (i // tpb, 0, 0)),
                  full(gattn), full(wout), full(gpost), full(gpre2), full(wrt), full(br)],
        out_specs=[row(d), row(d), row(_ROW_WORDS), col, col, col,
                   pl.BlockSpec((N_EXPERTS, LANES), lambda i: (0, 0))],
        out_shape=[jax.ShapeDtypeStruct((t, d), F32), jax.ShapeDtypeStruct((t, d), BF16),
                   jax.ShapeDtypeStruct((t, _ROW_WORDS), U32),
                   jax.ShapeDtypeStruct((TOP_K, t), I32), jax.ShapeDtypeStruct((TOP_K, t), F32),
                   jax.ShapeDtypeStruct((TOP_K, t), I32),
                   jax.ShapeDtypeStruct((N_EXPERTS, LANES), F32)],
        scratch_shapes=[pltpu.VMEM((N_EXPERTS, LANES), F32)],
        compiler_params=_params(("arbitrary",)),
        name="mix_out",
    )(attn, yc, x2, mod3, gattn, wout, gpost, gpre2, wrt, br)


def _dest_kernel(idx_ref, rank_ref, pstart_ref, dest_ref):
    tm = idx_ref.shape[1]
    row = lax.broadcasted_iota(I32, (N_EXPERTS, tm), 0)
    krow = lax.broadcasted_iota(I32, (TOP_K, tm), 0)
    pstart = pstart_ref[...]
    idx = idx_ref[...]
    out = jnp.zeros((TOP_K, tm), F32)
    for k in range(TOP_K):
        hit = row == idx[k:k + 1, :]
        base = jnp.sum(jnp.where(hit, pstart, 0.0), axis=0, keepdims=True)
        out = jnp.where(krow == k, base, out)
    dest_ref[...] = out.astype(I32) + rank_ref[...]


def _dest(idx, rank, pstart):
    t = idx.shape[1]
    tm = min(TM_DEST, t)
    col = pl.BlockSpec((TOP_K, tm), lambda i: (0, i))
    return pl.pallas_call(
        _dest_kernel,
        grid=(t // tm,),
        in_specs=[col, col, pl.BlockSpec((N_EXPERTS, 1), lambda i: (0, 0))],
        out_specs=col,
        out_shape=jax.ShapeDtypeStruct((TOP_K, t), I32),
        compiler_params=_params(("arbitrary",)),
        name="dest",
    )(idx, rank, pstart)


_PAD_CHUNKS = tuple(BM_EXPERT >> s for s in range(1, BM_EXPERT.bit_length()))


SC_CORES = 2
SC_SUBCORES = 16
SC_ROWS = 128


def _sc_worker_split(n_chunks):
    workers = SC_CORES * SC_SUBCORES
    per_worker = max(1, n_chunks // workers)
    active = n_chunks // per_worker
    assert active * per_worker == n_chunks and active <= workers
    return per_worker, active


def _scatter_rows_sc(rows, idx2d, n_out):
    n, width = rows.shape
    n_chunks = n // SC_ROWS
    per_worker, active = _sc_worker_split(n_chunks)
    mesh = plsc.VectorSubcoreMesh(core_axis_name="c", subcore_axis_name="s")

    @functools.partial(
        pl.kernel, mesh=mesh, out_type=jax.ShapeDtypeStruct((n_out, width), rows.dtype),
        scratch_types=[pltpu.VMEM((1, SC_ROWS), I32), pltpu.VMEM((SC_ROWS, width), rows.dtype)])
    def scatter(rows_hbm, idx_hbm, out_hbm, idx_v, rows_v):
        wid = lax.axis_index("s") * SC_CORES + lax.axis_index("c")

        @pl.when(wid < active)
        def _():
            @pl.loop(0, per_worker)
            def _(c):
                chunk = wid * per_worker + c
                pltpu.sync_copy(rows_hbm.at[pl.ds(chunk * SC_ROWS, SC_ROWS)], rows_v)
                for k in range(TOP_K):
                    pltpu.sync_copy(idx_hbm.at[pl.ds(k * n_chunks + chunk, 1)], idx_v)
                    pltpu.sync_copy(rows_v, out_hbm.at[idx_v.at[0]])

    return scatter(rows, idx2d)


def _padfill_kernel(pad_from_ref, pad_n_ref, xs_in_hbm, xs_hbm, zero_scr, pad_sem):
    del xs_in_hbm
    zero_scr[...] = jnp.zeros(zero_scr.shape, zero_scr.dtype)

    def pad_copies(e, act):
        n = pad_n_ref[e]
        base = pad_from_ref[e]

        def single_rows(start, count):
            for j in range(SUBLANES - 1):
                @pl.when(j < count)
                def _():
                    act(pltpu.make_async_copy(zero_scr.at[pl.ds(0, 1), :],
                                              xs_hbm.at[pl.ds(start + j, 1), :], pad_sem))

        head = jnp.minimum(n, (SUBLANES - (base & (SUBLANES - 1))) & (SUBLANES - 1))
        single_rows(base, head)
        rest = n - head
        mid = base + head
        for rows in _PAD_CHUNKS:
            if rows >= SUBLANES:
                @pl.when((rest & rows) != 0)
                def _():
                    start = pl.multiple_of(mid + (rest & ~(2 * rows - 1)), SUBLANES)
                    act(pltpu.make_async_copy(zero_scr.at[pl.ds(0, rows), :],
                                              xs_hbm.at[pl.ds(start, rows), :], pad_sem))
        single_rows(mid + (rest & ~(SUBLANES - 1)), rest & (SUBLANES - 1))

    def issue_pad(e, carry):
        pad_copies(e, lambda cp: cp.start())
        return carry

    def drain_pad(e, carry):
        pad_copies(e, lambda cp: cp.wait())
        return carry

    lax.fori_loop(0, N_EXPERTS, issue_pad, 0)
    lax.fori_loop(0, N_EXPERTS, drain_pad, 0)


def _padfill(pad_from, pad_n, xs):
    return pl.pallas_call(
        _padfill_kernel,
        grid_spec=pltpu.PrefetchScalarGridSpec(
            num_scalar_prefetch=2,
            grid=(1,),
            in_specs=[pl.BlockSpec(memory_space=pl.ANY)],
            out_specs=pl.BlockSpec(memory_space=pl.ANY),
            scratch_shapes=[pltpu.VMEM((BM_EXPERT // 2, _ROW_WORDS), U32), pltpu.SemaphoreType.DMA]),
        out_shape=jax.ShapeDtypeStruct(xs.shape, xs.dtype),
        input_output_aliases={2: 0},
        compiler_params=_params(("arbitrary",)),
        name="padfill",
    )(pad_from, pad_n, xs)


_XS_SLOTS = 4
_YS_SLOTS = 2
_W_SLOTS = 4
_W_AHEAD = 2


def _expert_kernel(first_ref, ord_ref, uexp_ref, meta_ref, xs_hbm, wg_hbm, wu_hbm, wd_hbm, ys_hbm,
                   xs_buf, ys_buf, wg_buf, wu_buf, wd_buf, act_scr, xs_sem, ys_sem, w_sem):
    i = pl.program_id(0)
    nused = meta_ref[0]
    nexp = meta_ref[1]
    bm = xs_buf.shape[1]

    def xs_copy(b, slot):
        return pltpu.make_async_copy(xs_hbm.at[pl.ds(b * bm, bm), :], xs_buf.at[slot], xs_sem.at[slot])

    def ys_copy(b, slot):
        return pltpu.make_async_copy(ys_buf.at[slot], ys_hbm.at[pl.ds(b * bm, bm), :], ys_sem.at[slot])

    def w_copies(j, slot):
        e = uexp_ref[j]
        return (pltpu.make_async_copy(wg_hbm.at[e], wg_buf.at[slot], w_sem.at[slot, 0]),
                pltpu.make_async_copy(wu_hbm.at[e], wu_buf.at[slot], w_sem.at[slot, 1]),
                pltpu.make_async_copy(wd_hbm.at[e], wd_buf.at[slot], w_sem.at[slot, 2]))

    @pl.when(i == 0)
    def _():
        for s in range(_XS_SLOTS - 1):
            @pl.when(s < nused)
            def _():
                xs_copy(s, s).start()
        for s in range(_W_AHEAD):
            @pl.when(s < nexp)
            def _():
                for cp in w_copies(s, s):
                    cp.start()

    def fetch(b):
        ahead = b + _XS_SLOTS - 1

        @pl.when(ahead < nused)
        def _():
            xs_copy(ahead, ahead % _XS_SLOTS).start()

        j = ord_ref[b]

        @pl.when(first_ref[b] == 1)
        def _():
            for cp in w_copies(j, j % _W_SLOTS):
                cp.wait()
            nxt = j + _W_AHEAD

            @pl.when(nxt < nexp)
            def _():
                for cp in w_copies(nxt, nxt % _W_SLOTS):
                    cp.start()

        xs_copy(b, b % _XS_SLOTS).wait()

    def gate_up(b):
        ws = ord_ref[b] % _W_SLOTS
        lo, hi = _unpack_row_words(xs_buf[b % _XS_SLOTS])
        g = (jnp.dot(lo, wg_buf[ws, 0:_ROW_WORDS, :], preferred_element_type=F32)
             + jnp.dot(hi, wg_buf[ws, _ROW_WORDS:, :], preferred_element_type=F32))
        u = (jnp.dot(lo, wu_buf[ws, 0:_ROW_WORDS, :], preferred_element_type=F32)
             + jnp.dot(hi, wu_buf[ws, _ROW_WORDS:, :], preferred_element_type=F32))
        return g * jax.nn.sigmoid(g) * u

    def down(b, act):
        y = jnp.dot(act, wd_buf[ord_ref[b] % _W_SLOTS], preferred_element_type=F32)
        oslot = b % _YS_SLOTS
        ys_buf[oslot] = _pack_row_words(y[:, 0:_ROW_WORDS], y[:, _ROW_WORDS:])
        ys_copy(b, oslot).start()

    @pl.when(jnp.logical_and(i >= _YS_SLOTS + 1, i <= nused))
    def _():
        ys_copy(i - 1 - _YS_SLOTS, (i - 1) % _YS_SLOTS).wait()

    @pl.when(i == 0)
    def _():
        fetch(i)
        act_scr[...] = gate_up(i)

    @pl.when(jnp.logical_and(i >= 1, i < nused))
    def _():
        fetch(i)
        prev = act_scr[...]
        act_scr[...] = gate_up(i)
        down(i - 1, prev)

    @pl.when(i == nused)
    def _():
        down(i - 1, act_scr[...])
        ys_copy(i - 1, (i - 1) % _YS_SLOTS).wait()

        @pl.when(i >= 2)
        def _():
            ys_copy(i - 2, (i - 2) % _YS_SLOTS).wait()


def _experts(first, ordinal, uexp, meta, xs, w_gate, w_up, w_down):
    p = xs.shape[0]
    d = w_gate.shape[1]
    nb = p // BM_EXPERT
    anyspec = pl.BlockSpec(memory_space=pl.ANY)
    return pl.pallas_call(
        _expert_kernel,
        grid_spec=pltpu.PrefetchScalarGridSpec(
            num_scalar_prefetch=4,
            grid=(nb + 1,),
            in_specs=[anyspec, anyspec, anyspec, anyspec],
            out_specs=anyspec,
            scratch_shapes=[pltpu.VMEM((_XS_SLOTS, BM_EXPERT, _ROW_WORDS), U32),
                            pltpu.VMEM((_YS_SLOTS, BM_EXPERT, _ROW_WORDS), U32),
                            pltpu.VMEM((_W_SLOTS, d, EXPERT_DIM), F32),
                            pltpu.VMEM((_W_SLOTS, d, EXPERT_DIM), F32),
                            pltpu.VMEM((_W_SLOTS, EXPERT_DIM, d), F32),
                            pltpu.VMEM((BM_EXPERT, EXPERT_DIM), F32),
                            pltpu.SemaphoreType.DMA((_XS_SLOTS,)), pltpu.SemaphoreType.DMA((_YS_SLOTS,)),
                            pltpu.SemaphoreType.DMA((_W_SLOTS, 3))]),
        out_shape=jax.ShapeDtypeStruct((p, _ROW_WORDS), U32),
        compiler_params=_params(("arbitrary",)),
        name="experts",
    )(first, ordinal, uexp, meta, xs, w_gate, w_up, w_down)


SC_GATHER_ROWS = 64


def _gather_rows_sc(table, idx):
    n = idx.shape[0]
    width = table.shape[1]
    rows = SC_GATHER_ROWS
    per_worker, active = _sc_worker_split(n // rows)
    assert per_worker % 2 == 0 or per_worker == 1
    mesh = plsc.VectorSubcoreMesh(core_axis_name="c", subcore_axis_name="s")

    @functools.partial(
        pl.kernel, mesh=mesh, out_type=jax.ShapeDtypeStruct((n, width), table.dtype),
        scratch_types=[pltpu.VMEM((per_worker, rows), I32), pltpu.VMEM((2, rows, width), table.dtype),
                       pltpu.SemaphoreType.DMA((2,))])
    def gather(table_hbm, idx_hbm, out_hbm, idx_v, rows_v, sem):
        wid = lax.axis_index("s") * SC_CORES + lax.axis_index("c")

        def fetch(c, b):
            return pltpu.make_async_copy(table_hbm.at[idx_v.at[c]], rows_v.at[b], sem.at[b])

        @pl.when(wid < active)
        def _():
            first = wid * per_worker
            pltpu.sync_copy(idx_hbm.at[pl.ds(first, per_worker)], idx_v)
            fetch(0, 0).start()

            @pl.loop(0, per_worker, step=2)
            def _(c):
                for b in range(min(2, per_worker)):
                    cur = c + b

                    @pl.when(cur + 1 < per_worker)
                    def _():
                        fetch(cur + 1, 1 - b).start()

                    fetch(cur, b).wait()
                    pltpu.sync_copy(rows_v.at[b], out_hbm.at[pl.ds((first + cur) * rows, rows)])

    return gather(table, idx.reshape(n // rows, rows))


def _combine_kernel(wts_ref, yg_ref, h2_ref, x1_ref, mod_ref, wsg_ref, wsu_ref, wsd_ref, gpost_ref, *rest):
    o_ref = rest[-1]
    h2 = h2_ref[...]
    g = jnp.dot(h2, wsg_ref[...], preferred_element_type=F32)
    u = jnp.dot(h2, wsu_ref[...], preferred_element_type=F32)
    f = jnp.dot((g * jax.nn.sigmoid(g) * u).astype(BF16), wsd_ref[...], preferred_element_type=F32)

    wts = wts_ref[...]
    los = [f[:, sl * LANES:(sl + 1) * LANES] for sl in range(_ROW_SLABS)]
    his = [f[:, _ROW_WORDS + sl * LANES:_ROW_WORDS + (sl + 1) * LANES] for sl in range(_ROW_SLABS)]
    for k in range(TOP_K):
        wk = wts[:, k:k + 1]
        for sl in range(_ROW_SLABS):
            lo, hi = _unpack_row_words(yg_ref[k, :, sl * LANES:(sl + 1) * LANES])
            los[sl] = los[sl] + wk * lo
            his[sl] = his[sl] + wk * hi
    f = jnp.concatenate(los + his, axis=1)
    g2 = mod_ref[0, 5:6, :]
    o_ref[...] = x1_ref[...] + g2 * _rms(f, gpost_ref[...])


def _combine(wts_t, yg, h2, x1, mod3, wsg, wsu, wsd, gpost, seq, first_tile, partial_out):
    t, d = x1.shape
    tm = min(TM_COMBINE, seq)
    tpb = seq // tm
    full = lambda a: pl.BlockSpec(a.shape, lambda i: (0,) * a.ndim)
    row = lambda w: pl.BlockSpec((tm, w), lambda i: (i + first_tile, 0))
    args = [wts_t, yg, h2, x1, mod3, wsg, wsu, wsd, gpost]
    in_specs = [row(TOP_K), pl.BlockSpec((TOP_K, tm, _ROW_WORDS), lambda i: (0, i, 0)), row(d), row(d),
                pl.BlockSpec((1, 6, d), lambda i: ((i + first_tile) // tpb, 0, 0)),
                full(wsg), full(wsu), full(wsd), full(gpost)]
    aliases = {}
    if partial_out is not None:
        aliases = {len(args): 0}
        args.append(partial_out)
        in_specs.append(pl.BlockSpec(memory_space=pl.ANY))
    return pl.pallas_call(
        _combine_kernel,
        grid=(yg.shape[1] // tm,),
        in_specs=in_specs,
        out_specs=row(d),
        out_shape=jax.ShapeDtypeStruct((t, d), F32),
        input_output_aliases=aliases,
        compiler_params=_params(("arbitrary",)),
        name="combine",
    )(*args)


def _pack_weights(w_in, w_uq, w_ukv):
    d = w_in.shape[0]
    half = QK_ROPE_DIM // 2
    z = lambda n, c: jnp.zeros((n, c), F32)
    o = Q_LORA_RANK + KV_LORA_RANK
    kr = w_in[:, o:o + QK_ROPE_DIM]
    kr_grp = jnp.concatenate([z(d, QK_NOPE_DIM), kr, z(d, HEAD_PAD - QK_NOPE_DIM - QK_ROPE_DIM)], axis=1)
    kr_rot = jnp.concatenate([z(d, QK_NOPE_DIM), -kr[:, half:], kr[:, :half],
                              z(d, HEAD_PAD - QK_NOPE_DIM - QK_ROPE_DIM)], axis=1)
    win_p = jnp.concatenate([w_in[:, :o], kr_grp, kr_rot, w_in[:, o + QK_ROPE_DIM:]], axis=1)

    scale = float(QK_NOPE_DIM + QK_ROPE_DIM) ** -0.5 * float(np.log2(np.e))
    r = Q_LORA_RANK
    qd = QK_NOPE_DIM + QK_ROPE_DIM
    q_grp, q_rot = [], []
    for h in range(MLA_HEADS):
        nope = w_uq[:, h * qd:h * qd + QK_NOPE_DIM]
        rope = w_uq[:, h * qd + QK_NOPE_DIM:(h + 1) * qd]
        pad = z(r, HEAD_PAD - qd)
        q_grp.append(jnp.concatenate([nope, rope, pad], axis=1))
        q_rot.append(jnp.concatenate([z(r, QK_NOPE_DIM), -rope[:, half:], rope[:, :half], pad], axis=1))
    wuq_p = jnp.concatenate(q_grp + q_rot, axis=1) * scale

    c = KV_LORA_RANK
    kd = QK_NOPE_DIM + V_HEAD_DIM
    k_grp, v_grp = [], []
    for h in range(MLA_HEADS):
        k_grp.append(jnp.concatenate([w_ukv[:, h * kd:h * kd + QK_NOPE_DIM], z(c, HEAD_PAD - QK_NOPE_DIM)], axis=1))
        v_grp.append(jnp.concatenate([w_ukv[:, h * kd + QK_NOPE_DIM:(h + 1) * kd], z(c, HEAD_PAD - V_HEAD_DIM)], axis=1))
    wukv_p = jnp.concatenate(k_grp + v_grp, axis=1)
    return win_p.astype(BF16), wuq_p.astype(BF16), wukv_p.astype(BF16)


def _rope_inputs(positions):
    inv = 1.0 / (ROPE_THETA ** (jnp.arange(0, QK_ROPE_DIM, 2, dtype=F32) / QK_ROPE_DIM))
    tail = HEAD_PAD - QK_NOPE_DIM - QK_ROPE_DIM
    freq = jnp.concatenate([jnp.zeros((QK_NOPE_DIM,), F32), inv, inv, jnp.zeros((tail,), F32)])
    return positions.astype(F32).reshape(-1, 1), freq.reshape(1, HEAD_PAD)


def _layer(x2, c, pos, freq, batch, seq, w_ada, b_ada, g_pre_mix, w_in, g_q_lat, w_uq, g_kv_lat, w_ukv,
           w_conv, g_attn_out, g_conv_out, w_out, g_post_mix, g_pre_ffn, w_router, b_router,
           w_gate, w_up, w_down, w_sh_gate, w_sh_up, w_sh_down, g_post_ffn):
    t, d = x2.shape
    r1 = lambda a: a.reshape(1, -1)

    c_pad = jnp.zeros((SUBLANES, d), F32).at[:batch].set(c)
    mod = _ada(c_pad, w_ada, r1(b_ada))[:batch]
    mod3 = mod.reshape(batch, 6, d)

    win_p, wuq_p, wukv_p = _pack_weights(w_in, w_uq, w_ukv)
    vone = jnp.zeros((1, HEAD_PAD), F32).at[0, V_HEAD_DIM].set(1.0)
    q, k, v, yc = _mix_in(x2, mod3, r1(g_pre_mix), win_p, r1(g_q_lat), wuq_p, r1(g_kv_lat), wukv_p,
                          vone, w_conv, r1(g_conv_out), pos, freq, seq)
    attn = _attention(q, k, v, batch, seq)
    x1, h2, h2p, idx, wts, rank, cnt = _mix_out(
        attn, yc, x2, mod3, r1(g_attn_out), w_out.astype(BF16), r1(g_post_mix), r1(g_pre_ffn),
        w_router.T, b_router.reshape(-1, 1), seq)

    counts = cnt[:, 0].astype(I32)
    padded = ((counts + BM_EXPERT - 1) // BM_EXPERT) * BM_EXPERT
    pad_end = jnp.cumsum(padded)
    pad_start = pad_end - padded
    m = t * TOP_K
    nb = (m + N_EXPERTS * (BM_EXPERT - 1)) // BM_EXPERT
    nused = pad_end[-1] // BM_EXPERT
    bidx = jnp.arange(nb, dtype=I32)
    blk_exp = jnp.sum((pad_end[None, :] <= (bidx * BM_EXPERT)[:, None]).astype(I32), axis=1)
    first = ((bidx < nused) & ((bidx == 0) | (blk_exp != jnp.roll(blk_exp, 1)))).astype(I32)
    ordinal = jnp.maximum(jnp.cumsum(first) - 1, 0).astype(I32)
    seen = jnp.cumsum((counts > 0).astype(I32))
    uexp = jnp.minimum(jnp.sum((seen[None, :] <= jnp.arange(N_EXPERTS, dtype=I32)[:, None]).astype(I32), axis=1),
                       N_EXPERTS - 1).astype(I32)
    meta = jnp.stack([nused, seen[-1]]).astype(I32)

    dest = _dest(idx, rank, pad_start.astype(F32).reshape(-1, 1))
    xs = _scatter_rows_sc(h2p, dest.reshape(-1, SC_ROWS), nb * BM_EXPERT)
    xs = _padfill((pad_start + counts).astype(I32), (padded - counts).astype(I32), xs)
    ys = _experts(first, ordinal, uexp, meta, xs, w_gate, w_up, w_down)
    parts = 2 if t % (2 * TM_COMBINE * SC_CORES * SC_SUBCORES) == 0 else 1
    tp = t // parts
    wsg, wsu, wsd = w_sh_gate.astype(BF16), w_sh_up.astype(BF16), w_sh_down.astype(BF16)
    out = None
    for part in range(parts):
        idx = dest[:, part * tp:(part + 1) * tp].reshape(-1)
        yg = _gather_rows_sc(ys, idx).reshape(TOP_K, tp, _ROW_WORDS)
        out = _combine(wts.T, yg, h2, x1, mod3, wsg, wsu, wsd, r1(g_post_ffn), seq,
                       part * tp // min(TM_COMBINE, seq), out)
    return out


def kernel(x, c, positions, w_ada, b_ada, g_pre_mix, w_in, g_q_lat, w_uq, g_kv_lat, w_ukv, w_conv, g_attn_out, g_conv_out, w_out, g_post_mix, g_pre_ffn, w_router, b_router, w_gate, w_up, w_down, w_sh_gate, w_sh_up, w_sh_down, g_post_ffn):
    batch, seq, d = x.shape
    pos, freq = _rope_inputs(positions)
    x2 = x.reshape(batch * seq, d)
    for l in range(w_ada.shape[0]):
        x2 = _layer(x2, c, pos, freq, batch, seq, w_ada[l], b_ada[l], g_pre_mix[l], w_in[l], g_q_lat[l],
                    w_uq[l], g_kv_lat[l], w_ukv[l], w_conv[l], g_attn_out[l], g_conv_out[l], w_out[l],
                    g_post_mix[l], g_pre_ffn[l], w_router[l], b_router[l], w_gate[l], w_up[l], w_down[l],
                    w_sh_gate[l], w_sh_up[l], w_sh_down[l], g_post_ffn[l])
    return x2.reshape(batch, seq, d)
```

```python
import functools

import jax
import jax.numpy as jnp
import numpy as np
from jax import lax
from jax.experimental import pallas as pl
from jax.experimental.pallas import tpu as pltpu
from jax.experimental.pallas import tpu_sc as plsc

F32 = jnp.float32
BF16 = jnp.bfloat16
I32 = jnp.int32
U32 = jnp.uint32

CHUNK = 64
MLA_HEADS = 8
QK_NOPE_DIM = 64
QK_ROPE_DIM = 32
V_HEAD_DIM = 64
Q_LORA_RANK = 384
KV_LORA_RANK = 256
ROPE_THETA = 10000.0
CONV_WIDTH = 3
N_EXPERTS = 256
TOP_K = 8
N_EXPERT_GROUPS = 8
TOPK_GROUPS = 4
EXPERT_DIM = 256
ROUTED_SCALE = 2.5
EPS = 1e-6

LANES = 128
SUBLANES = 8
HEAD_PAD = LANES
VMEM_LIMIT_BYTES = 56 * 1024 * 1024

TM_IN = 512
TQ_ATTN = 512
TM_OUT = 512
TM_DEST = 512
TM_DISPATCH = 512
BM_EXPERT = 256
TM_COMBINE = 256

NEG_INF = float("-inf")


def _rms(x, g):
    return x * lax.rsqrt(jnp.mean(x * x, axis=-1, keepdims=True) + EPS) * g


_HI_MASK = np.uint32(0xFFFF0000)
_ROW_WORDS = 512
_ROW_SLABS = _ROW_WORDS // LANES


def _pack_row_words(lo, hi):
    lo_w = lax.bitcast_convert_type(lo.astype(BF16).astype(F32), U32) >> 16
    hi_w = lax.bitcast_convert_type(hi.astype(BF16).astype(F32), U32) & _HI_MASK
    return lo_w | hi_w


def _unpack_row_words(w):
    return (lax.bitcast_convert_type(w << 16, F32), lax.bitcast_convert_type(w & _HI_MASK, F32))


def _params(sem):
    return pltpu.CompilerParams(dimension_semantics=sem, vmem_limit_bytes=VMEM_LIMIT_BYTES)


def _ada_kernel(c_ref, w_ref, b_ref, o_ref):
    c = c_ref[...]
    s = c * jax.nn.sigmoid(c)
    o_ref[...] = jnp.dot(s, w_ref[...], preferred_element_type=F32,
                         precision=lax.Precision.HIGHEST) + b_ref[...]


def _ada(c_pad, w, b):
    rows, d = c_pad.shape
    n = w.shape[1]
    tn = 1536
    return pl.pallas_call(
        _ada_kernel,
        grid=(n // tn,),
        in_specs=[pl.BlockSpec((rows, d), lambda j: (0, 0)),
                  pl.BlockSpec((d, tn), lambda j: (0, j)),
                  pl.BlockSpec((1, tn), lambda j: (0, j))],
        out_specs=pl.BlockSpec((rows, tn), lambda j: (0, j)),
        out_shape=jax.ShapeDtypeStruct((rows, n), F32),
        compiler_params=_params(("arbitrary",)),
        name="ada",
    )(c_pad, w, b)


_CQ0, _CQ1 = 0, Q_LORA_RANK
_CKV0, _CKV1 = _CQ1, _CQ1 + KV_LORA_RANK
_KR0, _KR1 = _CKV1, _CKV1 + 2 * HEAD_PAD
_CONV_DIM = 512
_GB0 = _KR1
_GC0 = _GB0 + _CONV_DIM
_XV0 = _GC0 + _CONV_DIM
_WIN_COLS = _XV0 + _CONV_DIM
_QW = MLA_HEADS * HEAD_PAD


def _mix_in_kernel(tiles_per_batch, x_ref, mod_ref, gpre_ref, win_ref, gq_ref, wuq_ref, gkv_ref,
                   wukv_ref, vone_ref, wconv_ref, gconv_ref, pos_ref, freq_ref,
                   q_ref, k_ref, v_ref, yc_ref, h_scr, u_scr):
    i = pl.program_id(0)
    tm = x_ref.shape[0]
    sh1 = mod_ref[0, 0:1, :]
    sc1 = mod_ref[0, 1:2, :]
    h = _rms(x_ref[...], gpre_ref[...]) * (1.0 + sc1) + sh1
    h_scr[...] = h.astype(BF16)
    ang = pos_ref[...] * freq_ref[...]
    cos = jnp.cos(ang)
    sin = jnp.sin(ang)

    cq = jnp.dot(h_scr[...], win_ref[:, _CQ0:_CQ1], preferred_element_type=F32)
    cqn = _rms(cq, gq_ref[...]).astype(BF16)
    qq = jnp.dot(cqn, wuq_ref[...], preferred_element_type=F32)
    for hd in range(MLA_HEADS):
        lo = hd * HEAD_PAD
        qh = qq[:, lo:lo + HEAD_PAD] * cos + qq[:, _QW + lo:_QW + lo + HEAD_PAD] * sin
        q_ref[:, lo:lo + HEAD_PAD] = qh.astype(BF16)

    ckv = jnp.dot(h_scr[...], win_ref[:, _CKV0:_CKV1], preferred_element_type=F32)
    ckvn = _rms(ckv, gkv_ref[...]).astype(BF16)
    kv = jnp.dot(ckvn, wukv_ref[...], preferred_element_type=F32)
    krr = jnp.dot(h_scr[...], win_ref[:, _KR0:_KR1], preferred_element_type=F32)
    kr = krr[:, 0:HEAD_PAD] * cos + krr[:, HEAD_PAD:2 * HEAD_PAD] * sin
    vone = vone_ref[...]
    for hd in range(MLA_HEADS):
        lo = hd * HEAD_PAD
        k_ref[:, lo:lo + HEAD_PAD] = (kv[:, lo:lo + HEAD_PAD] + kr).astype(BF16)
        v_ref[:, lo:lo + HEAD_PAD] = (kv[:, _QW + lo:_QW + lo + HEAD_PAD] + vone).astype(BF16)

    gb = jnp.dot(h_scr[...], win_ref[:, _GB0:_GC0], preferred_element_type=F32)
    gc = jnp.dot(h_scr[...], win_ref[:, _GC0:_XV0], preferred_element_type=F32)
    xv = jnp.dot(h_scr[...], win_ref[:, _XV0:_WIN_COLS], preferred_element_type=F32)
    u = gc * xv
    prev = u_scr[tm:tm + SUBLANES, :]
    first = (i % tiles_per_batch) == 0
    u_scr[0:SUBLANES, :] = jnp.where(first, jnp.zeros_like(prev), prev)
    u_scr[SUBLANES:tm + SUBLANES, :] = u
    um1 = u_scr[SUBLANES - 1:tm + SUBLANES - 1, :]
    um2 = u_scr[SUBLANES - 2:tm + SUBLANES - 2, :]
    conv = wconv_ref[0:1, :] * um2 + wconv_ref[1:2, :] * um1 + wconv_ref[2:3, :] * u
    yc_ref[...] = _rms(gb * conv, gconv_ref[...]).astype(BF16)


def _mix_in(x2, mod3, gpre, win_p, gq, wuq_p, gkv, wukv_p, vone, wconv, gconv, pos, freq, seq):
    t, d = x2.shape
    tm = min(TM_IN, seq)
    tpb = seq // tm
    full = lambda a: pl.BlockSpec(a.shape, lambda i: (0,) * a.ndim)
    row = lambda w: pl.BlockSpec((tm, w), lambda i: (i, 0))
    return pl.pallas_call(
        functools.partial(_mix_in_kernel, tpb),
        grid=(t // tm,),
        in_specs=[row(d),
                  pl.BlockSpec((1, 6, d), lambda i: (i // tpb, 0, 0)),
                  full(gpre), full(win_p), full(gq), full(wuq_p), full(gkv), full(wukv_p),
                  full(vone), full(wconv), full(gconv), row(1), full(freq)],
        out_specs=[row(_QW), row(_QW), row(_QW), row(_CONV_DIM)],
        out_shape=[jax.ShapeDtypeStruct((t, _QW), BF16), jax.ShapeDtypeStruct((t, _QW), BF16),
                   jax.ShapeDtypeStruct((t, _QW), BF16), jax.ShapeDtypeStruct((t, _CONV_DIM), BF16)],
        scratch_shapes=[pltpu.VMEM((tm, d), BF16), pltpu.VMEM((tm + SUBLANES, _CONV_DIM), F32)],
        compiler_params=_params(("arbitrary",)),
        name="mix_in",
    )(x2, mod3, gpre, win_p, gq, wuq_p, gkv, wukv_p, vone, wconv, gconv, pos, freq)


_HEADS_PER_STEP = 2


def _attn_kernel(tq, q_ref, k_ref, v_ref, o_ref, s_scr, mrun_scr, mb_scr, acc_scr):
    tk = tq
    nq = q_ref.shape[0] // tq
    lane_groups = tk // LANES
    heads = range(_HEADS_PER_STEP)
    lanes = [slice(hh * HEAD_PAD, (hh + 1) * HEAD_PAD) for hh in heads]

    def tile_max(s):
        m = s[:, 0:LANES]
        for g in range(1, lane_groups):
            m = jnp.maximum(m, s[:, g * LANES:(g + 1) * LANES])
        return m

    def scores(hh, qi, kv):
        off = pl.multiple_of(kv * tk, tk)
        return lax.dot_general(q_ref[qi * tq:(qi + 1) * tq, lanes[hh]], k_ref[pl.ds(off, tk), lanes[hh]],
                               (((1,), (1,)), ((), ())), preferred_element_type=F32)

    rc = lax.broadcasted_iota(I32, (tq, tk), 0) // CHUNK
    cc = lax.broadcasted_iota(I32, (tq, tk), 1) // CHUNK

    def diagonal(qi):
        for hh in heads:
            s = jnp.where(cc <= rc, scores(hh, qi, qi), NEG_INF)
            s_scr[hh, qi] = s
            m_row = jnp.max(jnp.maximum(mrun_scr[hh], tile_max(s)), axis=1, keepdims=True)
            mb_scr[hh] = jnp.broadcast_to(m_row, (tq, LANES))

    mrun_scr[...] = jnp.full(mrun_scr.shape, NEG_INF, F32)
    diagonal(0)
    for qi in range(nq):
        has_next = qi + 1 < nq
        acc_scr[...] = jnp.zeros(acc_scr.shape, F32)
        if has_next:
            mrun_scr[...] = jnp.full(mrun_scr.shape, NEG_INF, F32)

        def body(kv, carry, qi=qi, has_next=has_next):
            off = pl.multiple_of(kv * tk, tk)
            for hh in heads:
                mb = mb_scr[hh]
                p = jnp.concatenate(
                    [jnp.exp2(s_scr[hh, kv, :, g * LANES:(g + 1) * LANES] - mb) for g in range(lane_groups)],
                    axis=1).astype(BF16)
                acc_scr[hh] += jnp.dot(p, v_ref[pl.ds(off, tk), lanes[hh]], preferred_element_type=F32)
                if has_next:
                    s = scores(hh, qi + 1, kv)
                    s_scr[hh, kv] = s
                    mrun_scr[hh] = jnp.maximum(mrun_scr[hh], tile_max(s))
            return carry

        lax.fori_loop(0, qi + 1, body, 0, unroll=4)
        for hh in heads:
            acc = acc_scr[hh]
            o = acc[:, 0:V_HEAD_DIM] / acc[:, V_HEAD_DIM:V_HEAD_DIM + 1]
            o_ref[qi * tq:(qi + 1) * tq, hh * V_HEAD_DIM:(hh + 1) * V_HEAD_DIM] = o.astype(BF16)
        if has_next:
            diagonal(qi + 1)


def _attention(q, k, v, batch, seq):
    t = q.shape[0]
    tq = min(TQ_ATTN, seq)
    nq = seq // tq
    hw = _HEADS_PER_STEP * HEAD_PAD
    ow = _HEADS_PER_STEP * V_HEAD_DIM
    blk = lambda w: pl.BlockSpec((seq, w), lambda b, j: (b, j))
    return pl.pallas_call(
        functools.partial(_attn_kernel, tq),
        grid=(batch, MLA_HEADS // _HEADS_PER_STEP),
        in_specs=[blk(hw), blk(hw), blk(hw)],
        out_specs=blk(ow),
        out_shape=jax.ShapeDtypeStruct((t, MLA_HEADS * V_HEAD_DIM), BF16),
        scratch_shapes=[pltpu.VMEM((_HEADS_PER_STEP, nq, tq, tq), F32),
                        pltpu.VMEM((_HEADS_PER_STEP, tq, LANES), F32),
                        pltpu.VMEM((_HEADS_PER_STEP, tq, LANES), F32),
                        pltpu.VMEM((_HEADS_PER_STEP, tq, HEAD_PAD), F32)],
        compiler_params=_params(("arbitrary", "arbitrary")),
        name="attn",
    )(q, k, v)


_GROUP_SIZE = N_EXPERTS // N_EXPERT_GROUPS
_BIG = 1.0e9


def _mix_out_kernel(attn_ref, yc_ref, x_ref, mod_ref, gattn_ref, wout_ref, gpost_ref, gpre2_ref,
                    wrt_ref, br_ref, x1_ref, h2_ref, h2p_ref, idx_ref, wts_ref, rank_ref, cnt_ref,
                    carry_scr):
    i = pl.program_id(0)
    tm = x_ref.shape[0]
    half = attn_ref.shape[1]

    @pl.when(i == 0)
    def _():
        carry_scr[...] = jnp.zeros(carry_scr.shape, F32)

    an = _rms(attn_ref[...].astype(F32), gattn_ref[...]).astype(BF16)
    mix = (jnp.dot(an, wout_ref[0:half, :], preferred_element_type=F32)
           + jnp.dot(yc_ref[...], wout_ref[half:, :], preferred_element_type=F32))
    g1 = mod_ref[0, 2:3, :]
    sh2 = mod_ref[0, 3:4, :]
    sc2 = mod_ref[0, 4:5, :]
    x1 = x_ref[...] + g1 * _rms(mix, gpost_ref[...])
    x1_ref[...] = x1
    h2 = _rms(x1, gpre2_ref[...]) * (1.0 + sc2) + sh2
    h2_ref[...] = h2.astype(BF16)
    words = _pack_row_words(h2[:, 0:_ROW_WORDS], h2[:, _ROW_WORDS:])
    h2p_ref[...] = words

    logits = lax.dot_general(wrt_ref[...], h2, (((1,), (1,)), ((), ())),
                             preferred_element_type=F32, precision=lax.Precision.HIGHEST)
    scores = jax.nn.sigmoid(logits)
    sel = scores + br_ref[...]
    row = lax.broadcasted_iota(I32, (N_EXPERTS, tm), 0).astype(F32)

    gscore = []
    rw = lax.broadcasted_iota(I32, (_GROUP_SIZE, tm), 0).astype(F32)
    for g in range(N_EXPERT_GROUPS):
        blk = sel[g * _GROUP_SIZE:(g + 1) * _GROUP_SIZE, :]
        m1 = jnp.max(blk, axis=0, keepdims=True)
        i1 = jnp.min(jnp.where(blk == m1, rw, _BIG), axis=0, keepdims=True)
        m2 = jnp.max(jnp.where(rw == i1, NEG_INF, blk), axis=0, keepdims=True)
        gscore.append(m1 + m2)

    gkeep = [jnp.zeros((1, tm), F32) for _ in range(N_EXPERT_GROUPS)]
    for _ in range(TOPK_GROUPS):
        mg = functools.reduce(jnp.maximum, gscore)
        ig = functools.reduce(jnp.minimum, [jnp.where(gscore[g] == mg, float(g), _BIG)
                                            for g in range(N_EXPERT_GROUPS)])
        for g in range(N_EXPERT_GROUPS):
            hit = ig == float(g)
            gkeep[g] = jnp.where(hit, 1.0, gkeep[g])
            gscore[g] = jnp.where(hit, NEG_INF, gscore[g])
    n_slabs = N_EXPERTS // SUBLANES
    slabs_per_group = _GROUP_SIZE // SUBLANES
    sub = lax.broadcasted_iota(I32, (SUBLANES, tm), 0).astype(F32)
    first_rows = [jnp.where(gkeep[j // slabs_per_group] > 0.0, sel[j * SUBLANES:(j + 1) * SUBLANES, :], NEG_INF)
                  for j in range(n_slabs)]
    cur_rows = list(first_rows)
    krow = lax.broadcasted_iota(I32, (TOP_K, tm), 0)
    idx_rows = []
    idx_f = jnp.zeros((TOP_K, tm), F32)
    sc_k = jnp.zeros((TOP_K, tm), F32)
    sc_sum = jnp.zeros((1, tm), F32)
    prev = None
    for k in range(TOP_K):
        best = jnp.full((SUBLANES, tm), NEG_INF, F32)
        best_slab = jnp.zeros((SUBLANES, tm), F32)
        best_score = jnp.zeros((SUBLANES, tm), F32)
        for j in range(n_slabs):
            if prev is not None:
                cur_rows[j] = jnp.where(sub == prev - float(j * SUBLANES), NEG_INF, cur_rows[j])
            better = cur_rows[j] > best
            best = jnp.where(better, cur_rows[j], best)
            best_slab = jnp.where(better, float(j), best_slab)
            best_score = jnp.where(better, scores[j * SUBLANES:(j + 1) * SUBLANES, :], best_score)
        best_idx = best_slab * float(SUBLANES) + sub
        m = jnp.max(best, axis=0, keepdims=True)
        ik = jnp.min(jnp.where(best == m, best_idx, _BIG), axis=0, keepdims=True)
        sk = jnp.sum(jnp.where(best_idx == ik, best_score, 0.0), axis=0, keepdims=True)
        prev = ik
        idx_rows.append(ik)
        idx_f = jnp.where(krow == k, ik, idx_f)
        sc_k = jnp.where(krow == k, sk, sc_k)
        sc_sum = sc_sum + sk
    wts_ref[...] = sc_k / sc_sum * ROUTED_SCALE
    idx_ref[...] = idx_f.astype(I32)
    onehot = jnp.concatenate(
        [jnp.where(jnp.where(sub == prev - float(j * SUBLANES), NEG_INF, cur_rows[j]) != first_rows[j], 1.0, 0.0)
         for j in range(n_slabs)], axis=0)

    tri = (lax.broadcasted_iota(I32, (tm, tm), 0) < lax.broadcasted_iota(I32, (tm, tm), 1))
    excl = jnp.dot(onehot.astype(BF16), tri.astype(BF16), preferred_element_type=F32)
    rank_e = carry_scr[:, 0:1] + excl
    rank_k = jnp.zeros((TOP_K, tm), F32)
    for k in range(TOP_K):
        hit = row == idx_rows[k]
        rk = jnp.sum(jnp.where(hit, rank_e, 0.0), axis=0, keepdims=True)
        rank_k = jnp.where(krow == k, rk, rank_k)
    rank_ref[...] = rank_k.astype(I32)
    carry_scr[...] = carry_scr[...] + jnp.sum(onehot, axis=1, keepdims=True)
    cnt_ref[...] = carry_scr[...]


def _mix_out(attn, yc, x2, mod3, gattn, wout, gpost, gpre2, wrt, br, seq):
    t, d = x2.shape
    tm = min(TM_OUT, seq)
    tpb = seq // tm
    full = lambda a: pl.BlockSpec(a.shape, lambda i: (0,) * a.ndim)
    row = lambda w: pl.BlockSpec((tm, w), lambda i: (i, 0))
    col = pl.BlockSpec((TOP_K, tm), lambda i: (0, i))
    return pl.pallas_call(
        _mix_out_kernel,
        grid=(t // tm,),
        in_specs=[row(attn.shape[1]), row(yc.shape[1]), row(d),
                  pl.BlockSpec((1, 6, d), lambda i: (i // tpb, 0, 0)),
                  full(gattn), full(wout), full(gpost), full(gpre2), full(wrt), full(br)],
        out_specs=[row(d), row(d), row(_ROW_WORDS), col, col, col,
                   pl.BlockSpec((N_EXPERTS, LANES), lambda i: (0, 0))],
        out_shape=[jax.ShapeDtypeStruct((t, d), F32), jax.ShapeDtypeStruct((t, d), BF16),
                   jax.ShapeDtypeStruct((t, _ROW_WORDS), U32),
                   jax.ShapeDtypeStruct((TOP_K, t), I32), jax.ShapeDtypeStruct((TOP_K, t), F32),
                   jax.ShapeDtypeStruct((TOP_K, t), I32),
                   jax.ShapeDtypeStruct((N_EXPERTS, LANES), F32)],
        scratch_shapes=[pltpu.VMEM((N_EXPERTS, LANES), F32)],
        compiler_params=_params(("arbitrary",)),
        name="mix_out",
    )(attn, yc, x2, mod3, gattn, wout, gpost, gpre2, wrt, br)


def _dest_kernel(idx_ref, rank_ref, pstart_ref, dest_ref):
    tm = idx_ref.shape[1]
    row = lax.broadcasted_iota(I32, (N_EXPERTS, tm), 0)
    krow = lax.broadcasted_iota(I32, (TOP_K, tm), 0)
    pstart = pstart_ref[...]
    idx = idx_ref[...]
    out = jnp.zeros((TOP_K, tm), F32)
    for k in range(TOP_K):
        hit = row == idx[k:k + 1, :]
        base = jnp.sum(jnp.where(hit, pstart, 0.0), axis=0, keepdims=True)
        out = jnp.where(krow == k, base, out)
    dest_ref[...] = out.astype(I32) + rank_ref[...]


def _dest(idx, rank, pstart):
    t = idx.shape[1]
    tm = min(TM_DEST, t)
    col = pl.BlockSpec((TOP_K, tm), lambda i: (0, i))
    return pl.pallas_call(
        _dest_kernel,
        grid=(t // tm,),
        in_specs=[col, col, pl.BlockSpec((N_EXPERTS, 1), lambda i: (0, 0))],
        out_specs=col,
        out_shape=jax.ShapeDtypeStruct((TOP_K, t), I32),
        compiler_params=_params(("arbitrary",)),
        name="dest",
    )(idx, rank, pstart)


_PAD_CHUNKS = tuple(BM_EXPERT >> s for s in range(1, BM_EXPERT.bit_length()))


SC_CORES = 2
SC_SUBCORES = 16
SC_ROWS = 128


def _sc_worker_split(n_chunks):
    workers = SC_CORES * SC_SUBCORES
    per_worker = max(1, n_chunks // workers)
    active = n_chunks // per_worker
    assert active * per_worker == n_chunks and active <= workers
    return per_worker, active


def _scatter_rows_sc(rows, idx3d, n_out):
    n, width = rows.shape
    n_chunks = n // SC_ROWS
    per_worker, active = _sc_worker_split(n_chunks)
    mesh = plsc.VectorSubcoreMesh(core_axis_name="c", subcore_axis_name="s")

    @functools.partial(
        pl.kernel, mesh=mesh, out_type=jax.ShapeDtypeStruct((n_out, width), rows.dtype),
        scratch_types=[pltpu.VMEM((TOP_K, SC_ROWS), I32), pltpu.VMEM((SC_ROWS, width), rows.dtype),
                       pltpu.SemaphoreType.DMA])
    def scatter(rows_hbm, idx_hbm, out_hbm, idx_v, rows_v, sem):
        wid = lax.axis_index("s") * SC_CORES + lax.axis_index("c")

        @pl.when(wid < active)
        def _():
            @pl.loop(0, per_worker)
            def _(c):
                chunk = wid * per_worker + c
                pltpu.sync_copy(rows_hbm.at[pl.ds(chunk * SC_ROWS, SC_ROWS)], rows_v)
                pltpu.sync_copy(idx_hbm.at[chunk], idx_v)
                copies = [pltpu.make_async_copy(rows_v, out_hbm.at[idx_v.at[k]], sem) for k in range(TOP_K)]
                for cp in copies:
                    cp.start()
                for cp in copies:
                    cp.wait()

    return scatter(rows, idx3d)


def _padfill_kernel(pad_from_ref, pad_n_ref, xs_in_hbm, xs_hbm, zero_scr, pad_sem):
    del xs_in_hbm
    zero_scr[...] = jnp.zeros(zero_scr.shape, zero_scr.dtype)

    def pad_copies(e, act):
        n = pad_n_ref[e]
        base = pad_from_ref[e]

        def single_rows(start, count):
            for j in range(SUBLANES - 1):
                @pl.when(j < count)
                def _():
                    act(pltpu.make_async_copy(zero_scr.at[pl.ds(0, 1), :],
                                              xs_hbm.at[pl.ds(start + j, 1), :], pad_sem))

        head = jnp.minimum(n, (SUBLANES - (base & (SUBLANES - 1))) & (SUBLANES - 1))
        single_rows(base, head)
        rest = n - head
        mid = base + head
        for rows in _PAD_CHUNKS:
            if rows >= SUBLANES:
                @pl.when((rest & rows) != 0)
                def _():
                    start = pl.multiple_of(mid + (rest & ~(2 * rows - 1)), SUBLANES)
                    act(pltpu.make_async_copy(zero_scr.at[pl.ds(0, rows), :],
                                              xs_hbm.at[pl.ds(start, rows), :], pad_sem))
        single_rows(mid + (rest & ~(SUBLANES - 1)), rest & (SUBLANES - 1))

    def issue_pad(e, carry):
        pad_copies(e, lambda cp: cp.start())
        return carry

    def drain_pad(e, carry):
        pad_copies(e, lambda cp: cp.wait())
        return carry

    lax.fori_loop(0, N_EXPERTS, issue_pad, 0)
    lax.fori_loop(0, N_EXPERTS, drain_pad, 0)


def _padfill(pad_from, pad_n, xs):
    return pl.pallas_call(
        _padfill_kernel,
        grid_spec=pltpu.PrefetchScalarGridSpec(
            num_scalar_prefetch=2,
            grid=(1,),
            in_specs=[pl.BlockSpec(memory_space=pl.ANY)],
            out_specs=pl.BlockSpec(memory_space=pl.ANY),
            scratch_shapes=[pltpu.VMEM((BM_EXPERT // 2, _ROW_WORDS), U32), pltpu.SemaphoreType.DMA]),
        out_shape=jax.ShapeDtypeStruct(xs.shape, xs.dtype),
        input_output_aliases={2: 0},
        compiler_params=_params(("arbitrary",)),
        name="padfill",
    )(pad_from, pad_n, xs)


_XS_SLOTS = 4
_YS_SLOTS = 2
_W_SLOTS = 4
_W_AHEAD = 2


def _expert_kernel(first_ref, ord_ref, uexp_ref, meta_ref, xs_hbm, wg_hbm, wu_hbm, wd_hbm, ys_hbm,
                   xs_buf, ys_buf, wg_buf, wu_buf, wd_buf, act_scr, xs_sem, ys_sem, w_sem):
    i = pl.program_id(0)
    nused = meta_ref[0]
    nexp = meta_ref[1]
    bm = xs_buf.shape[1]

    def xs_copy(b, slot):
        return pltpu.make_async_copy(xs_hbm.at[pl.ds(b * bm, bm), :], xs_buf.at[slot], xs_sem.at[slot])

    def ys_copy(b, slot):
        return pltpu.make_async_copy(ys_buf.at[slot], ys_hbm.at[pl.ds(b * bm, bm), :], ys_sem.at[slot])

    def w_copies(j, slot):
        e = uexp_ref[j]
        return (pltpu.make_async_copy(wg_hbm.at[e], wg_buf.at[slot], w_sem.at[slot, 0]),
                pltpu.make_async_copy(wu_hbm.at[e], wu_buf.at[slot], w_sem.at[slot, 1]),
                pltpu.make_async_copy(wd_hbm.at[e], wd_buf.at[slot], w_sem.at[slot, 2]))

    @pl.when(i == 0)
    def _():
        for s in range(_XS_SLOTS - 1):
            @pl.when(s < nused)
            def _():
                xs_copy(s, s).start()
        for s in range(_W_AHEAD):
            @pl.when(s < nexp)
            def _():
                for cp in w_copies(s, s):
                    cp.start()

    def fetch(b):
        ahead = b + _XS_SLOTS - 1

        @pl.when(ahead < nused)
        def _():
            xs_copy(ahead, ahead % _XS_SLOTS).start()

        j = ord_ref[b]

        @pl.when(first_ref[b] == 1)
        def _():
            for cp in w_copies(j, j % _W_SLOTS):
                cp.wait()
            nxt = j + _W_AHEAD

            @pl.when(nxt < nexp)
            def _():
                for cp in w_copies(nxt, nxt % _W_SLOTS):
                    cp.start()

        xs_copy(b, b % _XS_SLOTS).wait()

    def gate_up(b):
        ws = ord_ref[b] % _W_SLOTS
        lo, hi = _unpack_row_words(xs_buf[b % _XS_SLOTS])
        g = (jnp.dot(lo, wg_buf[ws, 0:_ROW_WORDS, :], preferred_element_type=F32)
             + jnp.dot(hi, wg_buf[ws, _ROW_WORDS:, :], preferred_element_type=F32))
        u = (jnp.dot(lo, wu_buf[ws, 0:_ROW_WORDS, :], preferred_element_type=F32)
             + jnp.dot(hi, wu_buf[ws, _ROW_WORDS:, :], preferred_element_type=F32))
        return g * jax.nn.sigmoid(g) * u

    def down(b, act):
        y = jnp.dot(act, wd_buf[ord_ref[b] % _W_SLOTS], preferred_element_type=F32)
        oslot = b % _YS_SLOTS
        ys_buf[oslot] = _pack_row_words(y[:, 0:_ROW_WORDS], y[:, _ROW_WORDS:])
        ys_copy(b, oslot).start()

    @pl.when(jnp.logical_and(i >= _YS_SLOTS + 1, i <= nused))
    def _():
        ys_copy(i - 1 - _YS_SLOTS, (i - 1) % _YS_SLOTS).wait()

    @pl.when(i == 0)
    def _():
        fetch(i)
        act_scr[...] = gate_up(i)

    @pl.when(jnp.logical_and(i >= 1, i < nused))
    def _():
        fetch(i)
        prev = act_scr[...]
        act_scr[...] = gate_up(i)
        down(i - 1, prev)

    @pl.when(i == nused)
    def _():
        down(i - 1, act_scr[...])
        ys_copy(i - 1, (i - 1) % _YS_SLOTS).wait()

        @pl.when(i >= 2)
        def _():
            ys_copy(i - 2, (i - 2) % _YS_SLOTS).wait()


def _experts(first, ordinal, uexp, meta, xs, w_gate, w_up, w_down):
    p = xs.shape[0]
    d = w_gate.shape[1]
    nb = p // BM_EXPERT
    anyspec = pl.BlockSpec(memory_space=pl.ANY)
    return pl.pallas_call(
        _expert_kernel,
        grid_spec=pltpu.PrefetchScalarGridSpec(
            num_scalar_prefetch=4,
            grid=(nb + 1,),
            in_specs=[anyspec, anyspec, anyspec, anyspec],
            out_specs=anyspec,
            scratch_shapes=[pltpu.VMEM((_XS_SLOTS, BM_EXPERT, _ROW_WORDS), U32),
                            pltpu.VMEM((_YS_SLOTS, BM_EXPERT, _ROW_WORDS), U32),
                            pltpu.VMEM((_W_SLOTS, d, EXPERT_DIM), F32),
                            pltpu.VMEM((_W_SLOTS, d, EXPERT_DIM), F32),
                            pltpu.VMEM((_W_SLOTS, EXPERT_DIM, d), F32),
                            pltpu.VMEM((BM_EXPERT, EXPERT_DIM), F32),
                            pltpu.SemaphoreType.DMA((_XS_SLOTS,)), pltpu.SemaphoreType.DMA((_YS_SLOTS,)),
                            pltpu.SemaphoreType.DMA((_W_SLOTS, 3))]),
        out_shape=jax.ShapeDtypeStruct((p, _ROW_WORDS), U32),
        compiler_params=_params(("arbitrary",)),
        name="experts",
    )(first, ordinal, uexp, meta, xs, w_gate, w_up, w_down)


SC_GATHER_ROWS = 64


def _gather_rows_sc(table, idx):
    n = idx.shape[0]
    width = table.shape[1]
    rows = SC_GATHER_ROWS
    per_worker, active = _sc_worker_split(n // rows)
    assert per_worker % 2 == 0 or per_worker == 1
    mesh = plsc.VectorSubcoreMesh(core_axis_name="c", subcore_axis_name="s")

    @functools.partial(
        pl.kernel, mesh=mesh, out_type=jax.ShapeDtypeStruct((n, width), table.dtype),
        scratch_types=[pltpu.VMEM((per_worker, rows), I32), pltpu.VMEM((2, rows, width), table.dtype),
                       pltpu.SemaphoreType.DMA((2,))])
    def gather(table_hbm, idx_hbm, out_hbm, idx_v, rows_v, sem):
        wid = lax.axis_index("s") * SC_CORES + lax.axis_index("c")

        def fetch(c, b):
            return pltpu.make_async_copy(table_hbm.at[idx_v.at[c]], rows_v.at[b], sem.at[b])

        @pl.when(wid < active)
        def _():
            first = wid * per_worker
            pltpu.sync_copy(idx_hbm.at[pl.ds(first, per_worker)], idx_v)
            fetch(0, 0).start()

            @pl.loop(0, per_worker, step=2)
            def _(c):
                for b in range(min(2, per_worker)):
                    cur = c + b

                    @pl.when(cur + 1 < per_worker)
                    def _():
                        fetch(cur + 1, 1 - b).start()

                    fetch(cur, b).wait()
                    pltpu.sync_copy(rows_v.at[b], out_hbm.at[pl.ds((first + cur) * rows, rows)])

    return gather(table, idx.reshape(n // rows, rows))


def _combine_kernel(wts_ref, yg_ref, h2_ref, x1_ref, mod_ref, wsg_ref, wsu_ref, wsd_ref, gpost_ref, *rest):
    o_ref = rest[-1]
    h2 = h2_ref[...]
    g = jnp.dot(h2, wsg_ref[...], preferred_element_type=F32)
    u = jnp.dot(h2, wsu_ref[...], preferred_element_type=F32)
    f = jnp.dot((g * jax.nn.sigmoid(g) * u).astype(BF16), wsd_ref[...], preferred_element_type=F32)

    wts = wts_ref[...]
    los = [f[:, sl * LANES:(sl + 1) * LANES] for sl in range(_ROW_SLABS)]
    his = [f[:, _ROW_WORDS + sl * LANES:_ROW_WORDS + (sl + 1) * LANES] for sl in range(_ROW_SLABS)]
    for k in range(TOP_K):
        wk = wts[:, k:k + 1]
        for sl in range(_ROW_SLABS):
            lo, hi = _unpack_row_words(yg_ref[k, :, sl * LANES:(sl + 1) * LANES])
            los[sl] = los[sl] + wk * lo
            his[sl] = his[sl] + wk * hi
    f = jnp.concatenate(los + his, axis=1)
    g2 = mod_ref[0, 5:6, :]
    o_ref[...] = x1_ref[...] + g2 * _rms(f, gpost_ref[...])


def _combine(wts_t, yg, h2, x1, mod3, wsg, wsu, wsd, gpost, seq, first_tile, partial_out):
    t, d = x1.shape
    tm = min(TM_COMBINE, seq)
    tpb = seq // tm
    full = lambda a: pl.BlockSpec(a.shape, lambda i: (0,) * a.ndim)
    row = lambda w: pl.BlockSpec((tm, w), lambda i: (i + first_tile, 0))
    args = [wts_t, yg, h2, x1, mod3, wsg, wsu, wsd, gpost]
    in_specs = [row(TOP_K), pl.BlockSpec((TOP_K, tm, _ROW_WORDS), lambda i: (0, i, 0)), row(d), row(d),
                pl.BlockSpec((1, 6, d), lambda i: ((i + first_tile) // tpb, 0, 0)),
                full(wsg), full(wsu), full(wsd), full(gpost)]
    aliases = {}
    if partial_out is not None:
        aliases = {len(args): 0}
        args.append(partial_out)
        in_specs.append(pl.BlockSpec(memory_space=pl.ANY))
    return pl.pallas_call(
        _combine_kernel,
        grid=(yg.shape[1] // tm,),
        in_specs=in_specs,
        out_specs=row(d),
        out_shape=jax.ShapeDtypeStruct((t, d), F32),
        input_output_aliases=aliases,
        compiler_params=_params(("arbitrary",)),
        name="combine",
    )(*args)


def _pack_weights(w_in, w_uq, w_ukv):
    d = w_in.shape[0]
    half = QK_ROPE_DIM // 2
    z = lambda n, c: jnp.zeros((n, c), F32)
    o = Q_LORA_RANK + KV_LORA_RANK
    kr = w_in[:, o:o + QK_ROPE_DIM]
    kr_grp = jnp.concatenate([z(d, QK_NOPE_DIM), kr, z(d, HEAD_PAD - QK_NOPE_DIM - QK_ROPE_DIM)], axis=1)
    kr_rot = jnp.concatenate([z(d, QK_NOPE_DIM), -kr[:, half:], kr[:, :half],
                              z(d, HEAD_PAD - QK_NOPE_DIM - QK_ROPE_DIM)], axis=1)
    win_p = jnp.concatenate([w_in[:, :o], kr_grp, kr_rot, w_in[:, o + QK_ROPE_DIM:]], axis=1)

    scale = float(QK_NOPE_DIM + QK_ROPE_DIM) ** -0.5 * float(np.log2(np.e))
    r = Q_LORA_RANK
    qd = QK_NOPE_DIM + QK_ROPE_DIM
    q_grp, q_rot = [], []
    for h in range(MLA_HEADS):
        nope = w_uq[:, h * qd:h * qd + QK_NOPE_DIM]
        rope = w_uq[:, h * qd + QK_NOPE_DIM:(h + 1) * qd]
        pad = z(r, HEAD_PAD - qd)
        q_grp.append(jnp.concatenate([nope, rope, pad], axis=1))
        q_rot.append(jnp.concatenate([z(r, QK_NOPE_DIM), -rope[:, half:], rope[:, :half], pad], axis=1))
    wuq_p = jnp.concatenate(q_grp + q_rot, axis=1) * scale

    c = KV_LORA_RANK
    kd = QK_NOPE_DIM + V_HEAD_DIM
    k_grp, v_grp = [], []
    for h in range(MLA_HEADS):
        k_grp.append(jnp.concatenate([w_ukv[:, h * kd:h * kd + QK_NOPE_DIM], z(c, HEAD_PAD - QK_NOPE_DIM)], axis=1))
        v_grp.append(jnp.concatenate([w_ukv[:, h * kd + QK_NOPE_DIM:(h + 1) * kd], z(c, HEAD_PAD - V_HEAD_DIM)], axis=1))
    wukv_p = jnp.concatenate(k_grp + v_grp, axis=1)
    return win_p.astype(BF16), wuq_p.astype(BF16), wukv_p.astype(BF16)


def _rope_inputs(positions):
    inv = 1.0 / (ROPE_THETA ** (jnp.arange(0, QK_ROPE_DIM, 2, dtype=F32) / QK_ROPE_DIM))
    tail = HEAD_PAD - QK_NOPE_DIM - QK_ROPE_DIM
    freq = jnp.concatenate([jnp.zeros((QK_NOPE_DIM,), F32), inv, inv, jnp.zeros((tail,), F32)])
    return positions.astype(F32).reshape(-1, 1), freq.reshape(1, HEAD_PAD)


def _layer(x2, c, pos, freq, batch, seq, w_ada, b_ada, g_pre_mix, w_in, g_q_lat, w_uq, g_kv_lat, w_ukv,
           w_conv, g_attn_out, g_conv_out, w_out, g_post_mix, g_pre_ffn, w_router, b_router,
           w_gate, w_up, w_down, w_sh_gate, w_sh_up, w_sh_down, g_post_ffn):
    t, d = x2.shape
    r1 = lambda a: a.reshape(1, -1)

    c_pad = jnp.zeros((SUBLANES, d), F32).at[:batch].set(c)
    mod = _ada(c_pad, w_ada, r1(b_ada))[:batch]
    mod3 = mod.reshape(batch, 6, d)

    win_p, wuq_p, wukv_p = _pack_weights(w_in, w_uq, w_ukv)
    vone = jnp.zeros((1, HEAD_PAD), F32).at[0, V_HEAD_DIM].set(1.0)
    q, k, v, yc = _mix_in(x2, mod3, r1(g_pre_mix), win_p, r1(g_q_lat), wuq_p, r1(g_kv_lat), wukv_p,
                          vone, w_conv, r1(g_conv_out), pos, freq, seq)
    attn = _attention(q, k, v, batch, seq)
    x1, h2, h2p, idx, wts, rank, cnt = _mix_out(
        attn, yc, x2, mod3, r1(g_attn_out), w_out.astype(BF16), r1(g_post_mix), r1(g_pre_ffn),
        w_router.T, b_router.reshape(-1, 1), seq)

    counts = cnt[:, 0].astype(I32)
    padded = ((counts + BM_EXPERT - 1) // BM_EXPERT) * BM_EXPERT
    pad_end = jnp.cumsum(padded)
    pad_start = pad_end - padded
    m = t * TOP_K
    nb = (m + N_EXPERTS * (BM_EXPERT - 1)) // BM_EXPERT
    nused = pad_end[-1] // BM_EXPERT
    bidx = jnp.arange(nb, dtype=I32)
    blk_exp = jnp.sum((pad_end[None, :] <= (bidx * BM_EXPERT)[:, None]).astype(I32), axis=1)
    first = ((bidx < nused) & ((bidx == 0) | (blk_exp != jnp.roll(blk_exp, 1)))).astype(I32)
    ordinal = jnp.maximum(jnp.cumsum(first) - 1, 0).astype(I32)
    seen = jnp.cumsum((counts > 0).astype(I32))
    uexp = jnp.minimum(jnp.sum((seen[None, :] <= jnp.arange(N_EXPERTS, dtype=I32)[:, None]).astype(I32), axis=1),
                       N_EXPERTS - 1).astype(I32)
    meta = jnp.stack([nused, seen[-1]]).astype(I32)

    dest = _dest(idx, rank, pad_start.astype(F32).reshape(-1, 1))
    xs = _scatter_rows_sc(h2p, dest.reshape(TOP_K, -1, SC_ROWS).transpose(1, 0, 2), nb * BM_EXPERT)
    xs = _padfill((pad_start + counts).astype(I32), (padded - counts).astype(I32), xs)
    ys = _experts(first, ordinal, uexp, meta, xs, w_gate, w_up, w_down)
    parts = 2 if t % (2 * TM_COMBINE * SC_CORES * SC_SUBCORES) == 0 else 1
    tp = t // parts
    wsg, wsu, wsd = w_sh_gate.astype(BF16), w_sh_up.astype(BF16), w_sh_down.astype(BF16)
    out = None
    for part in range(parts):
        idx = dest[:, part * tp:(part + 1) * tp].reshape(-1)
        yg = _gather_rows_sc(ys, idx).reshape(TOP_K, tp, _ROW_WORDS)
        out = _combine(wts.T, yg, h2, x1, mod3, wsg, wsu, wsd, r1(g_post_ffn), seq,
                       part * tp // min(TM_COMBINE, seq), out)
    return out


def kernel(x, c, positions, w_ada, b_ada, g_pre_mix, w_in, g_q_lat, w_uq, g_kv_lat, w_ukv, w_conv, g_attn_out, g_conv_out, w_out, g_post_mix, g_pre_ffn, w_router, b_router, w_gate, w_up, w_down, w_sh_gate, w_sh_up, w_sh_down, g_post_ffn):
    batch, seq, d = x.shape
    pos, freq = _rope_inputs(positions)
    x2 = x.reshape(batch * seq, d)
    for l in range(w_ada.shape[0]):
        x2 = _layer(x2, c, pos, freq, batch, seq, w_ada[l], b_ada[l], g_pre_mix[l], w_in[l], g_q_lat[l],
                    w_uq[l], g_kv_lat[l], w_ukv[l], w_conv[l], g_attn_out[l], g_conv_out[l], w_out[l],
                    g_post_mix[l], g_pre_ffn[l], w_router[l], b_router[l], w_gate[l], w_up[l], w_down[l],
                    w_sh_gate[l], w_sh_up[l], w_sh_down[l], g_post_ffn[l])
    return x2.reshape(batch, seq, d)
```

```python
import functools

import jax
import jax.numpy as jnp
import numpy as np
from jax import lax
from jax.experimental import pallas as pl
from jax.experimental.pallas import tpu as pltpu
from jax.experimental.pallas import tpu_sc as plsc

F32 = jnp.float32
BF16 = jnp.bfloat16
I32 = jnp.int32
U32 = jnp.uint32

CHUNK = 64
MLA_HEADS = 8
QK_NOPE_DIM = 64
QK_ROPE_DIM = 32
V_HEAD_DIM = 64
Q_LORA_RANK = 384
KV_LORA_RANK = 256
ROPE_THETA = 10000.0
CONV_WIDTH = 3
N_EXPERTS = 256
TOP_K = 8
N_EXPERT_GROUPS = 8
TOPK_GROUPS = 4
EXPERT_DIM = 256
ROUTED_SCALE = 2.5
EPS = 1e-6

LANES = 128
SUBLANES = 8
HEAD_PAD = LANES
VMEM_LIMIT_BYTES = 56 * 1024 * 1024

TM_IN = 512
TQ_ATTN = 512
TM_OUT = 512
TM_DEST = 512
TM_DISPATCH = 512
BM_EXPERT = 256
TM_COMBINE = 256

NEG_INF = float("-inf")


def _rms(x, g):
    return x * lax.rsqrt(jnp.mean(x * x, axis=-1, keepdims=True) + EPS) * g


_HI_MASK = np.uint32(0xFFFF0000)
_ROW_WORDS = 512
_ROW_SLABS = _ROW_WORDS // LANES


def _pack_row_words(lo, hi):
    lo_w = lax.bitcast_convert_type(lo.astype(BF16).astype(F32), U32) >> 16
    hi_w = lax.bitcast_convert_type(hi.astype(BF16).astype(F32), U32) & _HI_MASK
    return lo_w | hi_w


def _unpack_row_words(w):
    return (lax.bitcast_convert_type(w << 16, F32), lax.bitcast_convert_type(w & _HI_MASK, F32))


def _params(sem):
    return pltpu.CompilerParams(dimension_semantics=sem, vmem_limit_bytes=VMEM_LIMIT_BYTES)


def _ada_kernel(c_ref, w_ref, b_ref, o_ref):
    c = c_ref[...]
    s = c * jax.nn.sigmoid(c)
    o_ref[...] = jnp.dot(s, w_ref[...], preferred_element_type=F32,
                         precision=lax.Precision.HIGHEST) + b_ref[...]


def _ada(c_pad, w, b):
    rows, d = c_pad.shape
    n = w.shape[1]
    tn = 1536
    return pl.pallas_call(
        _ada_kernel,
        grid=(n // tn,),
        in_specs=[pl.BlockSpec((rows, d), lambda j: (0, 0)),
                  pl.BlockSpec((d, tn), lambda j: (0, j)),
                  pl.BlockSpec((1, tn), lambda j: (0, j))],
        out_specs=pl.BlockSpec((rows, tn), lambda j: (0, j)),
        out_shape=jax.ShapeDtypeStruct((rows, n), F32),
        compiler_params=_params(("arbitrary",)),
        name="ada",
    )(c_pad, w, b)


_CQ0, _CQ1 = 0, Q_LORA_RANK
_CKV0, _CKV1 = _CQ1, _CQ1 + KV_LORA_RANK
_KR0, _KR1 = _CKV1, _CKV1 + 2 * HEAD_PAD
_CONV_DIM = 512
_GB0 = _KR1
_GC0 = _GB0 + _CONV_DIM
_XV0 = _GC0 + _CONV_DIM
_WIN_COLS = _XV0 + _CONV_DIM
_QW = MLA_HEADS * HEAD_PAD
_ROPE_PACK = LANES // QK_ROPE_DIM


def _mix_in_kernel(tiles_per_batch, x_ref, mod_ref, gpre_ref, win_ref, gq_ref, wuq_ref, gkv_ref,
                   wukv_ref, vone_ref, wconv_ref, gconv_ref, pos_ref, freq_ref,
                   q_ref, k_ref, v_ref, yc_ref, h_scr, u_scr, cos_scr, sin_scr):
    i = pl.program_id(0)
    tm = x_ref.shape[0]
    sh1 = mod_ref[0, 0:1, :]
    sc1 = mod_ref[0, 1:2, :]
    h = _rms(x_ref[...], gpre_ref[...]) * (1.0 + sc1) + sh1
    h_scr[...] = h.astype(BF16)
    ang4 = pos_ref[...] * freq_ref[...]
    cos4 = jnp.cos(ang4)
    sin4 = jnp.sin(ang4)
    lane = lax.broadcasted_iota(I32, ang4.shape, 1)
    on_rope = jnp.logical_and(lane >= QK_NOPE_DIM, lane < QK_NOPE_DIM + QK_ROPE_DIM)
    for g in range(_ROPE_PACK):
        shift = (QK_NOPE_DIM - QK_ROPE_DIM * g) % LANES
        cg = cos4 if shift == 0 else pltpu.roll(cos4, shift, axis=1)
        sg = sin4 if shift == 0 else pltpu.roll(sin4, shift, axis=1)
        cos_scr[pl.ds(g, tm // _ROPE_PACK, stride=_ROPE_PACK), :] = jnp.where(on_rope, cg, 1.0)
        sin_scr[pl.ds(g, tm // _ROPE_PACK, stride=_ROPE_PACK), :] = jnp.where(on_rope, sg, 0.0)
    cos = cos_scr[...]
    sin = sin_scr[...]

    cq = jnp.dot(h_scr[...], win_ref[:, _CQ0:_CQ1], preferred_element_type=F32)
    cqn = _rms(cq, gq_ref[...]).astype(BF16)
    qq = jnp.dot(cqn, wuq_ref[...], preferred_element_type=F32)
    for hd in range(MLA_HEADS):
        lo = hd * HEAD_PAD
        qh = qq[:, lo:lo + HEAD_PAD] * cos + qq[:, _QW + lo:_QW + lo + HEAD_PAD] * sin
        q_ref[:, lo:lo + HEAD_PAD] = qh.astype(BF16)

    ckv = jnp.dot(h_scr[...], win_ref[:, _CKV0:_CKV1], preferred_element_type=F32)
    ckvn = _rms(ckv, gkv_ref[...]).astype(BF16)
    kv = jnp.dot(ckvn, wukv_ref[...], preferred_element_type=F32)
    krr = jnp.dot(h_scr[...], win_ref[:, _KR0:_KR1], preferred_element_type=F32)
    kr = krr[:, 0:HEAD_PAD] * cos + krr[:, HEAD_PAD:2 * HEAD_PAD] * sin
    vone = vone_ref[...]
    for hd in range(MLA_HEADS):
        lo = hd * HEAD_PAD
        k_ref[:, lo:lo + HEAD_PAD] = (kv[:, lo:lo + HEAD_PAD] + kr).astype(BF16)
        v_ref[:, lo:lo + HEAD_PAD] = (kv[:, _QW + lo:_QW + lo + HEAD_PAD] + vone).astype(BF16)

    gb = jnp.dot(h_scr[...], win_ref[:, _GB0:_GC0], preferred_element_type=F32)
    gc = jnp.dot(h_scr[...], win_ref[:, _GC0:_XV0], preferred_element_type=F32)
    xv = jnp.dot(h_scr[...], win_ref[:, _XV0:_WIN_COLS], preferred_element_type=F32)
    u = gc * xv
    prev = u_scr[tm:tm + SUBLANES, :]
    first = (i % tiles_per_batch) == 0
    u_scr[0:SUBLANES, :] = jnp.where(first, jnp.zeros_like(prev), prev)
    u_scr[SUBLANES:tm + SUBLANES, :] = u
    um1 = u_scr[SUBLANES - 1:tm + SUBLANES - 1, :]
    um2 = u_scr[SUBLANES - 2:tm + SUBLANES - 2, :]
    conv = wconv_ref[0:1, :] * um2 + wconv_ref[1:2, :] * um1 + wconv_ref[2:3, :] * u
    yc_ref[...] = _rms(gb * conv, gconv_ref[...]).astype(BF16)


def _mix_in(x2, mod3, gpre, win_p, gq, wuq_p, gkv, wukv_p, vone, wconv, gconv, pos, freq, seq):
    t, d = x2.shape
    tm = min(TM_IN, seq)
    tpb = seq // tm
    full = lambda a: pl.BlockSpec(a.shape, lambda i: (0,) * a.ndim)
    row = lambda w: pl.BlockSpec((tm, w), lambda i: (i, 0))
    return pl.pallas_call(
        functools.partial(_mix_in_kernel, tpb),
        grid=(t // tm,),
        in_specs=[row(d),
                  pl.BlockSpec((1, 6, d), lambda i: (i // tpb, 0, 0)),
                  full(gpre), full(win_p), full(gq), full(wuq_p), full(gkv), full(wukv_p),
                  full(vone), full(wconv), full(gconv),
                  pl.BlockSpec((tm // _ROPE_PACK, LANES), lambda i: (i, 0)), full(freq)],
        out_specs=[row(_QW), row(_QW), row(_QW), row(_CONV_DIM)],
        out_shape=[jax.ShapeDtypeStruct((t, _QW), BF16), jax.ShapeDtypeStruct((t, _QW), BF16),
                   jax.ShapeDtypeStruct((t, _QW), BF16), jax.ShapeDtypeStruct((t, _CONV_DIM), BF16)],
        scratch_shapes=[pltpu.VMEM((tm, d), BF16), pltpu.VMEM((tm + SUBLANES, _CONV_DIM), F32),
                        pltpu.VMEM((tm, HEAD_PAD), F32), pltpu.VMEM((tm, HEAD_PAD), F32)],
        compiler_params=_params(("arbitrary",)),
        name="mix_in",
    )(x2, mod3, gpre, win_p, gq, wuq_p, gkv, wukv_p, vone, wconv, gconv, pos, freq)


_HEADS_PER_STEP = 2


def _attn_kernel(tq, q_ref, k_ref, v_ref, o_ref, s_scr, mrun_scr, mb_scr, acc_scr):
    tk = tq
    nq = q_ref.shape[0] // tq
    lane_groups = tk // LANES
    heads = range(_HEADS_PER_STEP)
    lanes = [slice(hh * HEAD_PAD, (hh + 1) * HEAD_PAD) for hh in heads]

    def tile_max(s):
        m = s[:, 0:LANES]
        for g in range(1, lane_groups):
            m = jnp.maximum(m, s[:, g * LANES:(g + 1) * LANES])
        return m

    def scores(hh, qi, kv):
        off = pl.multiple_of(kv * tk, tk)
        return lax.dot_general(q_ref[qi * tq:(qi + 1) * tq, lanes[hh]], k_ref[pl.ds(off, tk), lanes[hh]],
                               (((1,), (1,)), ((), ())), preferred_element_type=F32)

    rc = lax.broadcasted_iota(I32, (tq, tk), 0) // CHUNK
    cc = lax.broadcasted_iota(I32, (tq, tk), 1) // CHUNK

    def diagonal(qi):
        for hh in heads:
            s = jnp.where(cc <= rc, scores(hh, qi, qi), NEG_INF)
            s_scr[hh, qi] = s
            m_row = jnp.max(jnp.maximum(mrun_scr[hh], tile_max(s)), axis=1, keepdims=True)
            mb_scr[hh] = jnp.broadcast_to(m_row, (tq, LANES))

    mrun_scr[...] = jnp.full(mrun_scr.shape, NEG_INF, F32)
    diagonal(0)
    for qi in range(nq):
        has_next = qi + 1 < nq
        acc_scr[...] = jnp.zeros(acc_scr.shape, F32)
        if has_next:
            mrun_scr[...] = jnp.full(mrun_scr.shape, NEG_INF, F32)

        def body(kv, carry, qi=qi, has_next=has_next):
            off = pl.multiple_of(kv * tk, tk)
            for hh in heads:
                mb = mb_scr[hh]
                p = jnp.concatenate(
                    [jnp.exp2(s_scr[hh, kv, :, g * LANES:(g + 1) * LANES] - mb) for g in range(lane_groups)],
                    axis=1).astype(BF16)
                acc_scr[hh] += jnp.dot(p, v_ref[pl.ds(off, tk), lanes[hh]], preferred_element_type=F32)
                if has_next:
                    s = scores(hh, qi + 1, kv)
                    s_scr[hh, kv] = s
                    mrun_scr[hh] = jnp.maximum(mrun_scr[hh], tile_max(s))
            return carry

        lax.fori_loop(0, qi + 1, body, 0, unroll=4)
        for hh in heads:
            acc = acc_scr[hh]
            o = acc[:, 0:V_HEAD_DIM] / acc[:, V_HEAD_DIM:V_HEAD_DIM + 1]
            o_ref[qi * tq:(qi + 1) * tq, hh * V_HEAD_DIM:(hh + 1) * V_HEAD_DIM] = o.astype(BF16)
        if has_next:
            diagonal(qi + 1)


def _attention(q, k, v, batch, seq):
    t = q.shape[0]
    tq = min(TQ_ATTN, seq)
    nq = seq // tq
    hw = _HEADS_PER_STEP * HEAD_PAD
    ow = _HEADS_PER_STEP * V_HEAD_DIM
    blk = lambda w: pl.BlockSpec((seq, w), lambda b, j: (b, j))
    return pl.pallas_call(
        functools.partial(_attn_kernel, tq),
        grid=(batch, MLA_HEADS // _HEADS_PER_STEP),
        in_specs=[blk(hw), blk(hw), blk(hw)],
        out_specs=blk(ow),
        out_shape=jax.ShapeDtypeStruct((t, MLA_HEADS * V_HEAD_DIM), BF16),
        scratch_shapes=[pltpu.VMEM((_HEADS_PER_STEP, nq, tq, tq), F32),
                        pltpu.VMEM((_HEADS_PER_STEP, tq, LANES), F32),
                        pltpu.VMEM((_HEADS_PER_STEP, tq, LANES), F32),
                        pltpu.VMEM((_HEADS_PER_STEP, tq, HEAD_PAD), F32)],
        compiler_params=_params(("arbitrary", "arbitrary")),
        name="attn",
    )(q, k, v)


_GROUP_SIZE = N_EXPERTS // N_EXPERT_GROUPS
_BIG = 1.0e9


def _mix_out_kernel(attn_ref, yc_ref, x_ref, mod_ref, gattn_ref, wout_ref, gpost_ref, gpre2_ref,
                    wrt_ref, br_ref, x1_ref, h2_ref, h2p_ref, idx_ref, wts_ref, rank_ref, cnt_ref,
                    carry_scr):
    i = pl.program_id(0)
    tm = x_ref.shape[0]
    half = attn_ref.shape[1]

    @pl.when(i == 0)
    def _():
        carry_scr[...] = jnp.zeros(carry_scr.shape, F32)

    an = _rms(attn_ref[...].astype(F32), gattn_ref[...]).astype(BF16)
    mix = (jnp.dot(an, wout_ref[0:half, :], preferred_element_type=F32)
           + jnp.dot(yc_ref[...], wout_ref[half:, :], preferred_element_type=F32))
    g1 = mod_ref[0, 2:3, :]
    sh2 = mod_ref[0, 3:4, :]
    sc2 = mod_ref[0, 4:5, :]
    x1 = x_ref[...] + g1 * _rms(mix, gpost_ref[...])
    x1_ref[...] = x1
    h2 = _rms(x1, gpre2_ref[...]) * (1.0 + sc2) + sh2
    h2_ref[...] = h2.astype(BF16)
    words = _pack_row_words(h2[:, 0:_ROW_WORDS], h2[:, _ROW_WORDS:])
    h2p_ref[...] = words

    logits = lax.dot_general(wrt_ref[...], h2, (((1,), (1,)), ((), ())),
                             preferred_element_type=F32, precision=lax.Precision.HIGHEST)
    scores = jax.nn.sigmoid(logits)
    sel = scores + br_ref[...]
    row = lax.broadcasted_iota(I32, (N_EXPERTS, tm), 0).astype(F32)

    gscore = []
    rw = lax.broadcasted_iota(I32, (_GROUP_SIZE, tm), 0).astype(F32)
    for g in range(N_EXPERT_GROUPS):
        blk = sel[g * _GROUP_SIZE:(g + 1) * _GROUP_SIZE, :]
        m1 = jnp.max(blk, axis=0, keepdims=True)
        i1 = jnp.min(jnp.where(blk == m1, rw, _BIG), axis=0, keepdims=True)
        m2 = jnp.max(jnp.where(rw == i1, NEG_INF, blk), axis=0, keepdims=True)
        gscore.append(m1 + m2)

    gkeep = [jnp.zeros((1, tm), F32) for _ in range(N_EXPERT_GROUPS)]
    for _ in range(TOPK_GROUPS):
        mg = functools.reduce(jnp.maximum, gscore)
        ig = functools.reduce(jnp.minimum, [jnp.where(gscore[g] == mg, float(g), _BIG)
                                            for g in range(N_EXPERT_GROUPS)])
        for g in range(N_EXPERT_GROUPS):
            hit = ig == float(g)
            gkeep[g] = jnp.where(hit, 1.0, gkeep[g])
            gscore[g] = jnp.where(hit, NEG_INF, gscore[g])
    n_slabs = N_EXPERTS // SUBLANES
    slabs_per_group = _GROUP_SIZE // SUBLANES
    sub = lax.broadcasted_iota(I32, (SUBLANES, tm), 0).astype(F32)
    first_rows = [jnp.where(gkeep[j // slabs_per_group] > 0.0, sel[j * SUBLANES:(j + 1) * SUBLANES, :], NEG_INF)
                  for j in range(n_slabs)]
    cur_rows = list(first_rows)
    krow = lax.broadcasted_iota(I32, (TOP_K, tm), 0)
    idx_rows = []
    idx_f = jnp.zeros((TOP_K, tm), F32)
    sc_k = jnp.zeros((TOP_K, tm), F32)
    sc_sum = jnp.zeros((1, tm), F32)
    prev = None
    for k in range(TOP_K):
        best = jnp.full((SUBLANES, tm), NEG_INF, F32)
        best_slab = jnp.zeros((SUBLANES, tm), F32)
        best_score = jnp.zeros((SUBLANES, tm), F32)
        for j in range(n_slabs):
            if prev is not None:
                cur_rows[j] = jnp.where(sub == prev - float(j * SUBLANES), NEG_INF, cur_rows[j])
            better = cur_rows[j] > best
            best = jnp.where(better, cur_rows[j], best)
            best_slab = jnp.where(better, float(j), best_slab)
            best_score = jnp.where(better, scores[j * SUBLANES:(j + 1) * SUBLANES, :], best_score)
        best_idx = best_slab * float(SUBLANES) + sub
        m = jnp.max(best, axis=0, keepdims=True)
        ik = jnp.min(jnp.where(best == m, best_idx, _BIG), axis=0, keepdims=True)
        sk = jnp.sum(jnp.where(best_idx == ik, best_score, 0.0), axis=0, keepdims=True)
        prev = ik
        idx_rows.append(ik)
        idx_f = jnp.where(krow == k, ik, idx_f)
        sc_k = jnp.where(krow == k, sk, sc_k)
        sc_sum = sc_sum + sk
    wts_ref[...] = sc_k / sc_sum * ROUTED_SCALE
    idx_ref[...] = idx_f.astype(I32)
    onehot = jnp.concatenate(
        [jnp.where(jnp.where(sub == prev - float(j * SUBLANES), NEG_INF, cur_rows[j]) != first_rows[j], 1.0, 0.0)
         for j in range(n_slabs)], axis=0)

    tri = (lax.broadcasted_iota(I32, (tm, tm), 0) < lax.broadcasted_iota(I32, (tm, tm), 1))
    excl = jnp.dot(onehot.astype(BF16), tri.astype(BF16), preferred_element_type=F32)
    rank_e = carry_scr[:, 0:1] + excl
    rank_k = jnp.zeros((TOP_K, tm), F32)
    for k in range(TOP_K):
        hit = row == idx_rows[k]
        rk = jnp.sum(jnp.where(hit, rank_e, 0.0), axis=0, keepdims=True)
        rank_k = jnp.where(krow == k, rk, rank_k)
    rank_ref[...] = rank_k.astype(I32)
    carry_scr[...] = carry_scr[...] + jnp.sum(onehot, axis=1, keepdims=True)
    cnt_ref[...] = carry_scr[...]


def _mix_out(attn, yc, x2, mod3, gattn, wout, gpost, gpre2, wrt, br, seq):
    t, d = x2.shape
    tm = min(TM_OUT, seq)
    tpb = seq // tm
    full = lambda a: pl.BlockSpec(a.shape, lambda i: (0,) * a.ndim)
    row = lambda w: pl.BlockSpec((tm, w), lambda i: (i, 0))
    col = pl.BlockSpec((TOP_K, tm), lambda i: (0, i))
    return pl.pallas_call(
        _mix_out_kernel,
        grid=(t // tm,),
        in_specs=[row(attn.shape[1]), row(yc.shape[1]), row(d),
                  pl.BlockSpec((1, 6, d), lambda i: (i // tpb, 0, 0)),
                  full(gattn), full(wout), full(gpost), full(gpre2), full(wrt), full(br)],
        out_specs=[row(d), row(d), row(_ROW_WORDS), col, col, col,
                   pl.BlockSpec((N_EXPERTS, LANES), lambda i: (0, 0))],
        out_shape=[jax.ShapeDtypeStruct((t, d), F32), jax.ShapeDtypeStruct((t, d), BF16),
                   jax.ShapeDtypeStruct((t, _ROW_WORDS), U32),
                   jax.ShapeDtypeStruct((TOP_K, t), I32), jax.ShapeDtypeStruct((TOP_K, t), F32),
                   jax.ShapeDtypeStruct((TOP_K, t), I32),
                   jax.ShapeDtypeStruct((N_EXPERTS, LANES), F32)],
        scratch_shapes=[pltpu.VMEM((N_EXPERTS, LANES), F32)],
        compiler_params=_params(("arbitrary",)),
        name="mix_out",
    )(attn, yc, x2, mod3, gattn, wout, gpost, gpre2, wrt, br)


def _dest_kernel(idx_ref, rank_ref, pstart_ref, dest_ref):
    tm = idx_ref.shape[1]
    row = lax.broadcasted_iota(I32, (N_EXPERTS, tm), 0)
    krow = lax.broadcasted_iota(I32, (TOP_K, tm), 0)
    pstart = pstart_ref[...]
    idx = idx_ref[...]
    out = jnp.zeros((TOP_K, tm), F32)
    for k in range(TOP_K):
        hit = row == idx[k:k + 1, :]
        base = jnp.sum(jnp.where(hit, pstart, 0.0), axis=0, keepdims=True)
        out = jnp.where(krow == k, base, out)
    dest_ref[...] = out.astype(I32) + rank_ref[...]


def _dest(idx, rank, pstart):
    t = idx.shape[1]
    tm = min(TM_DEST, t)
    col = pl.BlockSpec((TOP_K, tm), lambda i: (0, i))
    return pl.pallas_call(
        _dest_kernel,
        grid=(t // tm,),
        in_specs=[col, col, pl.BlockSpec((N_EXPERTS, 1), lambda i: (0, 0))],
        out_specs=col,
        out_shape=jax.ShapeDtypeStruct((TOP_K, t), I32),
        compiler_params=_params(("arbitrary",)),
        name="dest",
    )(idx, rank, pstart)


_PAD_CHUNKS = tuple(BM_EXPERT >> s for s in range(1, BM_EXPERT.bit_length()))


SC_CORES = 2
SC_SUBCORES = 16
SC_ROWS = 128


def _sc_worker_split(n_chunks):
    workers = SC_CORES * SC_SUBCORES
    per_worker = max(1, n_chunks // workers)
    active = n_chunks // per_worker
    assert active * per_worker == n_chunks and active <= workers
    return per_worker, active


def _scatter_rows_sc(rows, idx3d, n_out):
    n, width = rows.shape
    n_chunks = n // SC_ROWS
    per_worker, active = _sc_worker_split(n_chunks)
    mesh = plsc.VectorSubcoreMesh(core_axis_name="c", subcore_axis_name="s")

    @functools.partial(
        pl.kernel, mesh=mesh, out_type=jax.ShapeDtypeStruct((n_out, width), rows.dtype),
        scratch_types=[pltpu.VMEM((TOP_K, SC_ROWS), I32), pltpu.VMEM((SC_ROWS, width), rows.dtype),
                       pltpu.SemaphoreType.DMA])
    def scatter(rows_hbm, idx_hbm, out_hbm, idx_v, rows_v, sem):
        wid = lax.axis_index("s") * SC_CORES + lax.axis_index("c")

        @pl.when(wid < active)
        def _():
            @pl.loop(0, per_worker)
            def _(c):
                chunk = wid * per_worker + c
                pltpu.sync_copy(rows_hbm.at[pl.ds(chunk * SC_ROWS, SC_ROWS)], rows_v)
                pltpu.sync_copy(idx_hbm.at[chunk], idx_v)
                copies = [pltpu.make_async_copy(rows_v, out_hbm.at[idx_v.at[k]], sem) for k in range(TOP_K)]
                for cp in copies:
                    cp.start()
                for cp in copies:
                    cp.wait()

    return scatter(rows, idx3d)


def _padfill_kernel(pad_from_ref, pad_n_ref, xs_in_hbm, xs_hbm, zero_scr, pad_sem):
    del xs_in_hbm
    zero_scr[...] = jnp.zeros(zero_scr.shape, zero_scr.dtype)

    def pad_copies(e, act):
        n = pad_n_ref[e]
        base = pad_from_ref[e]

        def single_rows(start, count):
            for j in range(SUBLANES - 1):
                @pl.when(j < count)
                def _():
                    act(pltpu.make_async_copy(zero_scr.at[pl.ds(0, 1), :],
                                              xs_hbm.at[pl.ds(start + j, 1), :], pad_sem))

        head = jnp.minimum(n, (SUBLANES - (base & (SUBLANES - 1))) & (SUBLANES - 1))
        single_rows(base, head)
        rest = n - head
        mid = base + head
        for rows in _PAD_CHUNKS:
            if rows >= SUBLANES:
                @pl.when((rest & rows) != 0)
                def _():
                    start = pl.multiple_of(mid + (rest & ~(2 * rows - 1)), SUBLANES)
                    act(pltpu.make_async_copy(zero_scr.at[pl.ds(0, rows), :],
                                              xs_hbm.at[pl.ds(start, rows), :], pad_sem))
        single_rows(mid + (rest & ~(SUBLANES - 1)), rest & (SUBLANES - 1))

    def issue_pad(e, carry):
        pad_copies(e, lambda cp: cp.start())
        return carry

    def drain_pad(e, carry):
        pad_copies(e, lambda cp: cp.wait())
        return carry

    lax.fori_loop(0, N_EXPERTS, issue_pad, 0)
    lax.fori_loop(0, N_EXPERTS, drain_pad, 0)


def _padfill(pad_from, pad_n, xs):
    return pl.pallas_call(
        _padfill_kernel,
        grid_spec=pltpu.PrefetchScalarGridSpec(
            num_scalar_prefetch=2,
            grid=(1,),
            in_specs=[pl.BlockSpec(memory_space=pl.ANY)],
            out_specs=pl.BlockSpec(memory_space=pl.ANY),
            scratch_shapes=[pltpu.VMEM((BM_EXPERT // 2, _ROW_WORDS), U32), pltpu.SemaphoreType.DMA]),
        out_shape=jax.ShapeDtypeStruct(xs.shape, xs.dtype),
        input_output_aliases={2: 0},
        compiler_params=_params(("arbitrary",)),
        name="padfill",
    )(pad_from, pad_n, xs)


_XS_SLOTS = 4
_YS_SLOTS = 2
_W_SLOTS = 4
_W_AHEAD = 2


def _expert_kernel(first_ref, ord_ref, uexp_ref, meta_ref, xs_hbm, wg_hbm, wu_hbm, wd_hbm, ys_hbm,
                   xs_buf, ys_buf, wg_buf, wu_buf, wd_buf, act_scr, xs_sem, ys_sem, w_sem):
    i = pl.program_id(0)
    nused = meta_ref[0]
    nexp = meta_ref[1]
    bm = xs_buf.shape[1]

    def xs_copy(b, slot):
        return pltpu.make_async_copy(xs_hbm.at[pl.ds(b * bm, bm), :], xs_buf.at[slot], xs_sem.at[slot])

    def ys_copy(b, slot):
        return pltpu.make_async_copy(ys_buf.at[slot], ys_hbm.at[pl.ds(b * bm, bm), :], ys_sem.at[slot])

    def w_copies(j, slot):
        e = uexp_ref[j]
        return (pltpu.make_async_copy(wg_hbm.at[e], wg_buf.at[slot], w_sem.at[slot, 0]),
                pltpu.make_async_copy(wu_hbm.at[e], wu_buf.at[slot], w_sem.at[slot, 1]),
                pltpu.make_async_copy(wd_hbm.at[e], wd_buf.at[slot], w_sem.at[slot, 2]))

    @pl.when(i == 0)
    def _():
        for s in range(_XS_SLOTS - 1):
            @pl.when(s < nused)
            def _():
                xs_copy(s, s).start()
        for s in range(_W_AHEAD):
            @pl.when(s < nexp)
            def _():
                for cp in w_copies(s, s):
                    cp.start()

    def fetch(b):
        ahead = b + _XS_SLOTS - 1

        @pl.when(ahead < nused)
        def _():
            xs_copy(ahead, ahead % _XS_SLOTS).start()

        j = ord_ref[b]

        @pl.when(first_ref[b] == 1)
        def _():
            for cp in w_copies(j, j % _W_SLOTS):
                cp.wait()
            nxt = j + _W_AHEAD

            @pl.when(nxt < nexp)
            def _():
                for cp in w_copies(nxt, nxt % _W_SLOTS):
                    cp.start()

        xs_copy(b, b % _XS_SLOTS).wait()

    def gate_up(b):
        ws = ord_ref[b] % _W_SLOTS
        lo, hi = _unpack_row_words(xs_buf[b % _XS_SLOTS])
        g = (jnp.dot(lo, wg_buf[ws, 0:_ROW_WORDS, :], preferred_element_type=F32)
             + jnp.dot(hi, wg_buf[ws, _ROW_WORDS:, :], preferred_element_type=F32))
        u = (jnp.dot(lo, wu_buf[ws, 0:_ROW_WORDS, :], preferred_element_type=F32)
             + jnp.dot(hi, wu_buf[ws, _ROW_WORDS:, :], preferred_element_type=F32))
        return g * jax.nn.sigmoid(g) * u

    def down(b, act):
        y = jnp.dot(act, wd_buf[ord_ref[b] % _W_SLOTS], preferred_element_type=F32)
        oslot = b % _YS_SLOTS
        ys_buf[oslot] = _pack_row_words(y[:, 0:_ROW_WORDS], y[:, _ROW_WORDS:])
        ys_copy(b, oslot).start()

    @pl.when(jnp.logical_and(i >= _YS_SLOTS + 1, i <= nused))
    def _():
        ys_copy(i - 1 - _YS_SLOTS, (i - 1) % _YS_SLOTS).wait()

    @pl.when(i == 0)
    def _():
        fetch(i)
        act_scr[...] = gate_up(i)

    @pl.when(jnp.logical_and(i >= 1, i < nused))
    def _():
        fetch(i)
        prev = act_scr[...]
        act_scr[...] = gate_up(i)
        down(i - 1, prev)

    @pl.when(i == nused)
    def _():
        down(i - 1, act_scr[...])
        ys_copy(i - 1, (i - 1) % _YS_SLOTS).wait()

        @pl.when(i >= 2)
        def _():
            ys_copy(i - 2, (i - 2) % _YS_SLOTS).wait()


def _experts(first, ordinal, uexp, meta, xs, w_gate, w_up, w_down):
    p = xs.shape[0]
    d = w_gate.shape[1]
    nb = p // BM_EXPERT
    anyspec = pl.BlockSpec(memory_space=pl.ANY)
    return pl.pallas_call(
        _expert_kernel,
        grid_spec=pltpu.PrefetchScalarGridSpec(
            num_scalar_prefetch=4,
            grid=(nb + 1,),
            in_specs=[anyspec, anyspec, anyspec, anyspec],
            out_specs=anyspec,
            scratch_shapes=[pltpu.VMEM((_XS_SLOTS, BM_EXPERT, _ROW_WORDS), U32),
                            pltpu.VMEM((_YS_SLOTS, BM_EXPERT, _ROW_WORDS), U32),
                            pltpu.VMEM((_W_SLOTS, d, EXPERT_DIM), F32),
                            pltpu.VMEM((_W_SLOTS, d, EXPERT_DIM), F32),
                            pltpu.VMEM((_W_SLOTS, EXPERT_DIM, d), F32),
                            pltpu.VMEM((BM_EXPERT, EXPERT_DIM), F32),
                            pltpu.SemaphoreType.DMA((_XS_SLOTS,)), pltpu.SemaphoreType.DMA((_YS_SLOTS,)),
                            pltpu.SemaphoreType.DMA((_W_SLOTS, 3))]),
        out_shape=jax.ShapeDtypeStruct((p, _ROW_WORDS), U32),
        compiler_params=_params(("arbitrary",)),
        name="experts",
    )(first, ordinal, uexp, meta, xs, w_gate, w_up, w_down)


SC_GATHER_ROWS = 64


def _gather_rows_sc(table, idx):
    n = idx.shape[0]
    width = table.shape[1]
    rows = SC_GATHER_ROWS
    per_worker, active = _sc_worker_split(n // rows)
    assert per_worker % 2 == 0 or per_worker == 1
    mesh = plsc.VectorSubcoreMesh(core_axis_name="c", subcore_axis_name="s")

    @functools.partial(
        pl.kernel, mesh=mesh, out_type=jax.ShapeDtypeStruct((n, width), table.dtype),
        scratch_types=[pltpu.VMEM((per_worker, rows), I32), pltpu.VMEM((2, rows, width), table.dtype),
                       pltpu.SemaphoreType.DMA((2,))])
    def gather(table_hbm, idx_hbm, out_hbm, idx_v, rows_v, sem):
        wid = lax.axis_index("s") * SC_CORES + lax.axis_index("c")

        def fetch(c, b):
            return pltpu.make_async_copy(table_hbm.at[idx_v.at[c]], rows_v.at[b], sem.at[b])

        @pl.when(wid < active)
        def _():
            first = wid * per_worker
            pltpu.sync_copy(idx_hbm.at[pl.ds(first, per_worker)], idx_v)
            fetch(0, 0).start()

            @pl.loop(0, per_worker, step=2)
            def _(c):
                for b in range(min(2, per_worker)):
                    cur = c + b

                    @pl.when(cur + 1 < per_worker)
                    def _():
                        fetch(cur + 1, 1 - b).start()

                    fetch(cur, b).wait()
                    pltpu.sync_copy(rows_v.at[b], out_hbm.at[pl.ds((first + cur) * rows, rows)])

    return gather(table, idx.reshape(n // rows, rows))


def _combine_kernel(wts_ref, yg_ref, h2_ref, x1_ref, mod_ref, wsg_ref, wsu_ref, wsd_ref, gpost_ref, *rest):
    o_ref = rest[-1]
    h2 = h2_ref[...]
    g = jnp.dot(h2, wsg_ref[...], preferred_element_type=F32)
    u = jnp.dot(h2, wsu_ref[...], preferred_element_type=F32)
    f = jnp.dot((g * jax.nn.sigmoid(g) * u).astype(BF16), wsd_ref[...], preferred_element_type=F32)

    wts = wts_ref[...]
    los = [f[:, sl * LANES:(sl + 1) * LANES] for sl in range(_ROW_SLABS)]
    his = [f[:, _ROW_WORDS + sl * LANES:_ROW_WORDS + (sl + 1) * LANES] for sl in range(_ROW_SLABS)]
    for k in range(TOP_K):
        wk = wts[:, k:k + 1]
        for sl in range(_ROW_SLABS):
            lo, hi = _unpack_row_words(yg_ref[k, :, sl * LANES:(sl + 1) * LANES])
            los[sl] = los[sl] + wk * lo
            his[sl] = his[sl] + wk * hi
    f = jnp.concatenate(los + his, axis=1)
    g2 = mod_ref[0, 5:6, :]
    o_ref[...] = x1_ref[...] + g2 * _rms(f, gpost_ref[...])


def _combine(wts_t, yg, h2, x1, mod3, wsg, wsu, wsd, gpost, seq, first_tile, partial_out):
    t, d = x1.shape
    tm = min(TM_COMBINE, seq)
    tpb = seq // tm
    full = lambda a: pl.BlockSpec(a.shape, lambda i: (0,) * a.ndim)
    row = lambda w: pl.BlockSpec((tm, w), lambda i: (i + first_tile, 0))
    args = [wts_t, yg, h2, x1, mod3, wsg, wsu, wsd, gpost]
    in_specs = [row(TOP_K), pl.BlockSpec((TOP_K, tm, _ROW_WORDS), lambda i: (0, i, 0)), row(d), row(d),
                pl.BlockSpec((1, 6, d), lambda i: ((i + first_tile) // tpb, 0, 0)),
                full(wsg), full(wsu), full(wsd), full(gpost)]
    aliases = {}
    if partial_out is not None:
        aliases = {len(args): 0}
        args.append(partial_out)
        in_specs.append(pl.BlockSpec(memory_space=pl.ANY))
    return pl.pallas_call(
        _combine_kernel,
        grid=(yg.shape[1] // tm,),
        in_specs=in_specs,
        out_specs=row(d),
        out_shape=jax.ShapeDtypeStruct((t, d), F32),
        input_output_aliases=aliases,
        compiler_params=_params(("arbitrary",)),
        name="combine",
    )(*args)


def _pack_weights(w_in, w_uq, w_ukv):
    d = w_in.shape[0]
    half = QK_ROPE_DIM // 2
    z = lambda n, c: jnp.zeros((n, c), F32)
    o = Q_LORA_RANK + KV_LORA_RANK
    kr = w_in[:, o:o + QK_ROPE_DIM]
    kr_grp = jnp.concatenate([z(d, QK_NOPE_DIM), kr, z(d, HEAD_PAD - QK_NOPE_DIM - QK_ROPE_DIM)], axis=1)
    kr_rot = jnp.concatenate([z(d, QK_NOPE_DIM), -kr[:, half:], kr[:, :half],
                              z(d, HEAD_PAD - QK_NOPE_DIM - QK_ROPE_DIM)], axis=1)
    win_p = jnp.concatenate([w_in[:, :o], kr_grp, kr_rot, w_in[:, o + QK_ROPE_DIM:]], axis=1)

    scale = float(QK_NOPE_DIM + QK_ROPE_DIM) ** -0.5 * float(np.log2(np.e))
    r = Q_LORA_RANK
    qd = QK_NOPE_DIM + QK_ROPE_DIM
    q_grp, q_rot = [], []
    for h in range(MLA_HEADS):
        nope = w_uq[:, h * qd:h * qd + QK_NOPE_DIM]
        rope = w_uq[:, h * qd + QK_NOPE_DIM:(h + 1) * qd]
        pad = z(r, HEAD_PAD - qd)
        q_grp.append(jnp.concatenate([nope, rope, pad], axis=1))
        q_rot.append(jnp.concatenate([z(r, QK_NOPE_DIM), -rope[:, half:], rope[:, :half], pad], axis=1))
    wuq_p = jnp.concatenate(q_grp + q_rot, axis=1) * scale

    c = KV_LORA_RANK
    kd = QK_NOPE_DIM + V_HEAD_DIM
    k_grp, v_grp = [], []
    for h in range(MLA_HEADS):
        k_grp.append(jnp.concatenate([w_ukv[:, h * kd:h * kd + QK_NOPE_DIM], z(c, HEAD_PAD - QK_NOPE_DIM)], axis=1))
        v_grp.append(jnp.concatenate([w_ukv[:, h * kd + QK_NOPE_DIM:(h + 1) * kd], z(c, HEAD_PAD - V_HEAD_DIM)], axis=1))
    wukv_p = jnp.concatenate(k_grp + v_grp, axis=1)
    return win_p.astype(BF16), wuq_p.astype(BF16), wukv_p.astype(BF16)


def _rope_inputs(positions):
    inv = 1.0 / (ROPE_THETA ** (jnp.arange(0, QK_ROPE_DIM, 2, dtype=F32) / QK_ROPE_DIM))
    freq = jnp.tile(inv, LANES // inv.shape[0]).reshape(1, LANES)
    pos = jnp.repeat(positions.astype(F32).reshape(-1, _ROPE_PACK), QK_ROPE_DIM, axis=1)
    return pos, freq


def _layer(x2, c, pos, freq, batch, seq, w_ada, b_ada, g_pre_mix, w_in, g_q_lat, w_uq, g_kv_lat, w_ukv,
           w_conv, g_attn_out, g_conv_out, w_out, g_post_mix, g_pre_ffn, w_router, b_router,
           w_gate, w_up, w_down, w_sh_gate, w_sh_up, w_sh_down, g_post_ffn):
    t, d = x2.shape
    r1 = lambda a: a.reshape(1, -1)

    c_pad = jnp.zeros((SUBLANES, d), F32).at[:batch].set(c)
    mod = _ada(c_pad, w_ada, r1(b_ada))[:batch]
    mod3 = mod.reshape(batch, 6, d)

    win_p, wuq_p, wukv_p = _pack_weights(w_in, w_uq, w_ukv)
    vone = jnp.zeros((1, HEAD_PAD), F32).at[0, V_HEAD_DIM].set(1.0)
    q, k, v, yc = _mix_in(x2, mod3, r1(g_pre_mix), win_p, r1(g_q_lat), wuq_p, r1(g_kv_lat), wukv_p,
                          vone, w_conv, r1(g_conv_out), pos, freq, seq)
    attn = _attention(q, k, v, batch, seq)
    x1, h2, h2p, idx, wts, rank, cnt = _mix_out(
        attn, yc, x2, mod3, r1(g_attn_out), w_out.astype(BF16), r1(g_post_mix), r1(g_pre_ffn),
        w_router.T, b_router.reshape(-1, 1), seq)

    counts = cnt[:, 0].astype(I32)
    padded = ((counts + BM_EXPERT - 1) // BM_EXPERT) * BM_EXPERT
    pad_end = jnp.cumsum(padded)
    pad_start = pad_end - padded
    m = t * TOP_K
    nb = (m + N_EXPERTS * (BM_EXPERT - 1)) // BM_EXPERT
    nused = pad_end[-1] // BM_EXPERT
    bidx = jnp.arange(nb, dtype=I32)
    blk_exp = jnp.sum((pad_end[None, :] <= (bidx * BM_EXPERT)[:, None]).astype(I32), axis=1)
    first = ((bidx < nused) & ((bidx == 0) | (blk_exp != jnp.roll(blk_exp, 1)))).astype(I32)
    ordinal = jnp.maximum(jnp.cumsum(first) - 1, 0).astype(I32)
    seen = jnp.cumsum((counts > 0).astype(I32))
    uexp = jnp.minimum(jnp.sum((seen[None, :] <= jnp.arange(N_EXPERTS, dtype=I32)[:, None]).astype(I32), axis=1),
                       N_EXPERTS - 1).astype(I32)
    meta = jnp.stack([nused, seen[-1]]).astype(I32)

    dest = _dest(idx, rank, pad_start.astype(F32).reshape(-1, 1))
    xs = _scatter_rows_sc(h2p, dest.reshape(TOP_K, -1, SC_ROWS).transpose(1, 0, 2), nb * BM_EXPERT)
    xs = _padfill((pad_start + counts).astype(I32), (padded - counts).astype(I32), xs)
    ys = _experts(first, ordinal, uexp, meta, xs, w_gate, w_up, w_down)
    parts = 2 if t % (2 * TM_COMBINE * SC_CORES * SC_SUBCORES) == 0 else 1
    tp = t // parts
    wsg, wsu, wsd = w_sh_gate.astype(BF16), w_sh_up.astype(BF16), w_sh_down.astype(BF16)
    out = None
    for part in range(parts):
        idx = dest[:, part * tp:(part + 1) * tp].reshape(-1)
        yg = _gather_rows_sc(ys, idx).reshape(TOP_K, tp, _ROW_WORDS)
        out = _combine(wts.T, yg, h2, x1, mod3, wsg, wsu, wsd, r1(g_post_ffn), seq,
                       part * tp // min(TM_COMBINE, seq), out)
    return out


def kernel(x, c, positions, w_ada, b_ada, g_pre_mix, w_in, g_q_lat, w_uq, g_kv_lat, w_ukv, w_conv, g_attn_out, g_conv_out, w_out, g_post_mix, g_pre_ffn, w_router, b_router, w_gate, w_up, w_down, w_sh_gate, w_sh_up, w_sh_down, g_post_ffn):
    batch, seq, d = x.shape
    pos, freq = _rope_inputs(positions)
    x2 = x.reshape(batch * seq, d)
    for l in range(w_ada.shape[0]):
        x2 = _layer(x2, c, pos, freq, batch, seq, w_ada[l], b_ada[l], g_pre_mix[l], w_in[l], g_q_lat[l],
                    w_uq[l], g_kv_lat[l], w_ukv[l], w_conv[l], g_attn_out[l], g_conv_out[l], w_out[l],
                    g_post_mix[l], g_pre_ffn[l], w_router[l], b_router[l], w_gate[l], w_up[l], w_down[l],
                    w_sh_gate[l], w_sh_up[l], w_sh_down[l], g_post_ffn[l])
    return x2.reshape(batch, seq, d)
```

```python
import functools

import jax
import jax.numpy as jnp
import numpy as np
from jax import lax
from jax.experimental import pallas as pl
from jax.experimental.pallas import tpu as pltpu
from jax.experimental.pallas import tpu_sc as plsc

F32 = jnp.float32
BF16 = jnp.bfloat16
I32 = jnp.int32
U32 = jnp.uint32

CHUNK = 64
MLA_HEADS = 8
QK_NOPE_DIM = 64
QK_ROPE_DIM = 32
V_HEAD_DIM = 64
Q_LORA_RANK = 384
KV_LORA_RANK = 256
ROPE_THETA = 10000.0
CONV_WIDTH = 3
N_EXPERTS = 256
TOP_K = 8
N_EXPERT_GROUPS = 8
TOPK_GROUPS = 4
EXPERT_DIM = 256
ROUTED_SCALE = 2.5
EPS = 1e-6

LANES = 128
SUBLANES = 8
HEAD_PAD = LANES
VMEM_LIMIT_BYTES = 56 * 1024 * 1024

TM_IN = 512
TQ_ATTN = 512
TM_OUT = 512
TM_DEST = 512
BM_EXPERT = 256
TM_COMBINE = 256

NEG_INF = float("-inf")


def _rms(x, g):
    return x * lax.rsqrt(jnp.mean(x * x, axis=-1, keepdims=True) + EPS) * g


_HI_MASK = np.uint32(0xFFFF0000)
_ROW_WORDS = 512
_ROW_SLABS = _ROW_WORDS // LANES


def _pack_row_words(lo, hi):
    lo_w = lax.bitcast_convert_type(lo.astype(BF16).astype(F32), U32) >> 16
    hi_w = lax.bitcast_convert_type(hi.astype(BF16).astype(F32), U32) & _HI_MASK
    return lo_w | hi_w


def _unpack_row_words(w):
    return (lax.bitcast_convert_type(w << 16, F32), lax.bitcast_convert_type(w & _HI_MASK, F32))


def _params(sem):
    return pltpu.CompilerParams(dimension_semantics=sem, vmem_limit_bytes=VMEM_LIMIT_BYTES)


def _ada_kernel(c_ref, w_ref, b_ref, o_ref):
    c = c_ref[...]
    s = c * jax.nn.sigmoid(c)
    o_ref[...] = jnp.dot(s, w_ref[...], preferred_element_type=F32,
                         precision=lax.Precision.HIGHEST) + b_ref[...]


def _ada(c_pad, w, b):
    rows, d = c_pad.shape
    n = w.shape[1]
    tn = 1536
    return pl.pallas_call(
        _ada_kernel,
        grid=(n // tn,),
        in_specs=[pl.BlockSpec((rows, d), lambda j: (0, 0)),
                  pl.BlockSpec((d, tn), lambda j: (0, j)),
                  pl.BlockSpec((1, tn), lambda j: (0, j))],
        out_specs=pl.BlockSpec((rows, tn), lambda j: (0, j)),
        out_shape=jax.ShapeDtypeStruct((rows, n), F32),
        compiler_params=_params(("arbitrary",)),
        name="ada",
    )(c_pad, w, b)


_CQ0, _CQ1 = 0, Q_LORA_RANK
_CKV0, _CKV1 = _CQ1, _CQ1 + KV_LORA_RANK
_KR0, _KR1 = _CKV1, _CKV1 + 2 * HEAD_PAD
_CONV_DIM = 512
_GB0 = _KR1
_GC0 = _GB0 + _CONV_DIM
_XV0 = _GC0 + _CONV_DIM
_WIN_COLS = _XV0 + _CONV_DIM
_QW = MLA_HEADS * HEAD_PAD
_ROPE_PACK = LANES // QK_ROPE_DIM


def _mix_in_kernel(tiles_per_batch, x_ref, mod_ref, gpre_ref, win_ref, gq_ref, wuq_ref, gkv_ref,
                   wukv_ref, vone_ref, wconv_ref, gconv_ref, pos_ref, freq_ref,
                   q_ref, k_ref, v_ref, yc_ref, h_scr, u_scr, cos_scr, sin_scr):
    i = pl.program_id(0)
    tm = x_ref.shape[0]
    sh1 = mod_ref[0, 0:1, :]
    sc1 = mod_ref[0, 1:2, :]
    h = _rms(x_ref[...], gpre_ref[...]) * (1.0 + sc1) + sh1
    h_scr[...] = h.astype(BF16)
    ang4 = pos_ref[...] * freq_ref[...]
    cos4 = jnp.cos(ang4)
    sin4 = jnp.sin(ang4)
    lane = lax.broadcasted_iota(I32, ang4.shape, 1)
    on_rope = jnp.logical_and(lane >= QK_NOPE_DIM, lane < QK_NOPE_DIM + QK_ROPE_DIM)
    for g in range(_ROPE_PACK):
        shift = (QK_NOPE_DIM - QK_ROPE_DIM * g) % LANES
        cg = cos4 if shift == 0 else pltpu.roll(cos4, shift, axis=1)
        sg = sin4 if shift == 0 else pltpu.roll(sin4, shift, axis=1)
        cos_scr[pl.ds(g, tm // _ROPE_PACK, stride=_ROPE_PACK), :] = jnp.where(on_rope, cg, 1.0)
        sin_scr[pl.ds(g, tm // _ROPE_PACK, stride=_ROPE_PACK), :] = jnp.where(on_rope, sg, 0.0)
    cos = cos_scr[...]
    sin = sin_scr[...]

    cq = jnp.dot(h_scr[...], win_ref[:, _CQ0:_CQ1], preferred_element_type=F32)
    cqn = _rms(cq, gq_ref[...]).astype(BF16)
    qq = jnp.dot(cqn, wuq_ref[...], preferred_element_type=F32)
    for hd in range(MLA_HEADS):
        lo = hd * HEAD_PAD
        qh = qq[:, lo:lo + HEAD_PAD] * cos + qq[:, _QW + lo:_QW + lo + HEAD_PAD] * sin
        q_ref[:, lo:lo + HEAD_PAD] = qh.astype(BF16)

    ckv = jnp.dot(h_scr[...], win_ref[:, _CKV0:_CKV1], preferred_element_type=F32)
    ckvn = _rms(ckv, gkv_ref[...]).astype(BF16)
    kv = jnp.dot(ckvn, wukv_ref[...], preferred_element_type=F32)
    krr = jnp.dot(h_scr[...], win_ref[:, _KR0:_KR1], preferred_element_type=F32)
    kr = krr[:, 0:HEAD_PAD] * cos + krr[:, HEAD_PAD:2 * HEAD_PAD] * sin
    vone = vone_ref[...]
    for hd in range(MLA_HEADS):
        lo = hd * HEAD_PAD
        k_ref[:, lo:lo + HEAD_PAD] = (kv[:, lo:lo + HEAD_PAD] + kr).astype(BF16)
        v_ref[:, lo:lo + HEAD_PAD] = (kv[:, _QW + lo:_QW + lo + HEAD_PAD] + vone).astype(BF16)

    gb = jnp.dot(h_scr[...], win_ref[:, _GB0:_GC0], preferred_element_type=F32)
    gc = jnp.dot(h_scr[...], win_ref[:, _GC0:_XV0], preferred_element_type=F32)
    xv = jnp.dot(h_scr[...], win_ref[:, _XV0:_WIN_COLS], preferred_element_type=F32)
    u = gc * xv
    prev = u_scr[tm:tm + SUBLANES, :]
    first = (i % tiles_per_batch) == 0
    u_scr[0:SUBLANES, :] = jnp.where(first, jnp.zeros_like(prev), prev)
    u_scr[SUBLANES:tm + SUBLANES, :] = u
    um1 = u_scr[SUBLANES - 1:tm + SUBLANES - 1, :]
    um2 = u_scr[SUBLANES - 2:tm + SUBLANES - 2, :]
    conv = wconv_ref[0:1, :] * um2 + wconv_ref[1:2, :] * um1 + wconv_ref[2:3, :] * u
    yc_ref[...] = _rms(gb * conv, gconv_ref[...]).astype(BF16)


def _mix_in(x2, mod3, gpre, win_p, gq, wuq_p, gkv, wukv_p, vone, wconv, gconv, pos, freq, seq):
    t, d = x2.shape
    tm = min(TM_IN, seq)
    tpb = seq // tm
    full = lambda a: pl.BlockSpec(a.shape, lambda i: (0,) * a.ndim)
    row = lambda w: pl.BlockSpec((tm, w), lambda i: (i, 0))
    return pl.pallas_call(
        functools.partial(_mix_in_kernel, tpb),
        grid=(t // tm,),
        in_specs=[row(d),
                  pl.BlockSpec((1, 6, d), lambda i: (i // tpb, 0, 0)),
                  full(gpre), full(win_p), full(gq), full(wuq_p), full(gkv), full(wukv_p),
                  full(vone), full(wconv), full(gconv),
                  pl.BlockSpec((tm // _ROPE_PACK, LANES), lambda i: (i, 0)), full(freq)],
        out_specs=[row(_QW), row(_QW), row(_QW), row(_CONV_DIM)],
        out_shape=[jax.ShapeDtypeStruct((t, _QW), BF16), jax.ShapeDtypeStruct((t, _QW), BF16),
                   jax.ShapeDtypeStruct((t, _QW), BF16), jax.ShapeDtypeStruct((t, _CONV_DIM), BF16)],
        scratch_shapes=[pltpu.VMEM((tm, d), BF16), pltpu.VMEM((tm + SUBLANES, _CONV_DIM), F32),
                        pltpu.VMEM((tm, HEAD_PAD), F32), pltpu.VMEM((tm, HEAD_PAD), F32)],
        compiler_params=_params(("arbitrary",)),
        name="mix_in",
    )(x2, mod3, gpre, win_p, gq, wuq_p, gkv, wukv_p, vone, wconv, gconv, pos, freq)


_HEADS_PER_STEP = 2


def _attn_kernel(tq, q_ref, k_ref, v_ref, o_ref, s_scr, mrun_scr, mb_scr, acc_scr):
    tk = tq
    nq = q_ref.shape[0] // tq
    lane_groups = tk // LANES
    heads = range(_HEADS_PER_STEP)
    lanes = [slice(hh * HEAD_PAD, (hh + 1) * HEAD_PAD) for hh in heads]

    def tile_max(s):
        m = s[:, 0:LANES]
        for g in range(1, lane_groups):
            m = jnp.maximum(m, s[:, g * LANES:(g + 1) * LANES])
        return m

    def scores(hh, qi, kv):
        off = pl.multiple_of(kv * tk, tk)
        return lax.dot_general(q_ref[qi * tq:(qi + 1) * tq, lanes[hh]], k_ref[pl.ds(off, tk), lanes[hh]],
                               (((1,), (1,)), ((), ())), preferred_element_type=F32)

    rc = lax.broadcasted_iota(I32, (tq, tk), 0) // CHUNK
    cc = lax.broadcasted_iota(I32, (tq, tk), 1) // CHUNK

    def diagonal(qi):
        for hh in heads:
            s = jnp.where(cc <= rc, scores(hh, qi, qi), NEG_INF)
            s_scr[hh, qi] = s
            m_row = jnp.max(jnp.maximum(mrun_scr[hh], tile_max(s)), axis=1, keepdims=True)
            mb_scr[hh] = jnp.broadcast_to(m_row, (tq, LANES))

    mrun_scr[...] = jnp.full(mrun_scr.shape, NEG_INF, F32)
    diagonal(0)
    for qi in range(nq):
        has_next = qi + 1 < nq
        acc_scr[...] = jnp.zeros(acc_scr.shape, F32)
        if has_next:
            mrun_scr[...] = jnp.full(mrun_scr.shape, NEG_INF, F32)

        def body(kv, carry, qi=qi, has_next=has_next):
            off = pl.multiple_of(kv * tk, tk)
            for hh in heads:
                mb = mb_scr[hh]
                p = jnp.concatenate(
                    [jnp.exp2(s_scr[hh, kv, :, g * LANES:(g + 1) * LANES] - mb) for g in range(lane_groups)],
                    axis=1).astype(BF16)
                acc_scr[hh] += jnp.dot(p, v_ref[pl.ds(off, tk), lanes[hh]], preferred_element_type=F32)
                if has_next:
                    s = scores(hh, qi + 1, kv)
                    s_scr[hh, kv] = s
                    mrun_scr[hh] = jnp.maximum(mrun_scr[hh], tile_max(s))
            return carry

        lax.fori_loop(0, qi + 1, body, 0, unroll=4)
        for hh in heads:
            acc = acc_scr[hh]
            o = acc[:, 0:V_HEAD_DIM] / acc[:, V_HEAD_DIM:V_HEAD_DIM + 1]
            o_ref[qi * tq:(qi + 1) * tq, hh * V_HEAD_DIM:(hh + 1) * V_HEAD_DIM] = o.astype(BF16)
        if has_next:
            diagonal(qi + 1)


def _attention(q, k, v, batch, seq):
    t = q.shape[0]
    tq = min(TQ_ATTN, seq)
    nq = seq // tq
    hw = _HEADS_PER_STEP * HEAD_PAD
    ow = _HEADS_PER_STEP * V_HEAD_DIM
    blk = lambda w: pl.BlockSpec((seq, w), lambda b, j: (b, j))
    return pl.pallas_call(
        functools.partial(_attn_kernel, tq),
        grid=(batch, MLA_HEADS // _HEADS_PER_STEP),
        in_specs=[blk(hw), blk(hw), blk(hw)],
        out_specs=blk(ow),
        out_shape=jax.ShapeDtypeStruct((t, MLA_HEADS * V_HEAD_DIM), BF16),
        scratch_shapes=[pltpu.VMEM((_HEADS_PER_STEP, nq, tq, tq), F32),
                        pltpu.VMEM((_HEADS_PER_STEP, tq, LANES), F32),
                        pltpu.VMEM((_HEADS_PER_STEP, tq, LANES), F32),
                        pltpu.VMEM((_HEADS_PER_STEP, tq, HEAD_PAD), F32)],
        compiler_params=_params(("arbitrary", "arbitrary")),
        name="attn",
    )(q, k, v)


_GROUP_SIZE = N_EXPERTS // N_EXPERT_GROUPS
_BIG = 1.0e9


def _mix_out_kernel(attn_ref, yc_ref, x_ref, mod_ref, gattn_ref, wout_ref, gpost_ref, gpre2_ref,
                    wrt_ref, br_ref, x1_ref, h2_ref, h2p_ref, idx_ref, wts_ref, rank_ref, cnt_ref,
                    carry_scr, scores_scr, sel_scr):
    i = pl.program_id(0)
    n_tiles = pl.num_programs(0) - 1

    @pl.when(i == 0)
    def _():
        carry_scr[...] = jnp.zeros(carry_scr.shape, F32)
        scores, sel = _project_tile(attn_ref, yc_ref, x_ref, mod_ref, gattn_ref, wout_ref, gpost_ref,
                                    gpre2_ref, wrt_ref, br_ref, x1_ref, h2_ref, h2p_ref)
        scores_scr[0] = scores
        sel_scr[0] = sel

    @pl.when(jnp.logical_and(i >= 1, i < n_tiles))
    def _():
        prev_scores = scores_scr[(i - 1) % 2]
        prev_sel = sel_scr[(i - 1) % 2]
        scores, sel = _project_tile(attn_ref, yc_ref, x_ref, mod_ref, gattn_ref, wout_ref, gpost_ref,
                                    gpre2_ref, wrt_ref, br_ref, x1_ref, h2_ref, h2p_ref)
        scores_scr[i % 2] = scores
        sel_scr[i % 2] = sel
        _route_tile(prev_scores, prev_sel, idx_ref, wts_ref, rank_ref, cnt_ref, carry_scr)

    @pl.when(i == n_tiles)
    def _():
        _route_tile(scores_scr[(i - 1) % 2], sel_scr[(i - 1) % 2], idx_ref, wts_ref, rank_ref, cnt_ref,
                    carry_scr)


def _project_tile(attn_ref, yc_ref, x_ref, mod_ref, gattn_ref, wout_ref, gpost_ref, gpre2_ref,
                  wrt_ref, br_ref, x1_ref, h2_ref, h2p_ref):
    half = attn_ref.shape[1]
    an = _rms(attn_ref[...].astype(F32), gattn_ref[...]).astype(BF16)
    mix = (jnp.dot(an, wout_ref[0:half, :], preferred_element_type=F32)
           + jnp.dot(yc_ref[...], wout_ref[half:, :], preferred_element_type=F32))
    g1 = mod_ref[0, 2:3, :]
    sh2 = mod_ref[0, 3:4, :]
    sc2 = mod_ref[0, 4:5, :]
    x1 = x_ref[...] + g1 * _rms(mix, gpost_ref[...])
    x1_ref[...] = x1
    h2 = _rms(x1, gpre2_ref[...]) * (1.0 + sc2) + sh2
    h2_ref[...] = h2.astype(BF16)
    h2p_ref[...] = _pack_row_words(h2[:, 0:_ROW_WORDS], h2[:, _ROW_WORDS:])

    logits = lax.dot_general(wrt_ref[...], h2, (((1,), (1,)), ((), ())),
                             preferred_element_type=F32, precision=lax.Precision.HIGHEST)
    scores = jax.nn.sigmoid(logits)
    return scores, scores + br_ref[...]


def _route_tile(scores, sel, idx_ref, wts_ref, rank_ref, cnt_ref, carry_scr):
    tm = scores.shape[1]
    row = lax.broadcasted_iota(I32, (N_EXPERTS, tm), 0).astype(F32)

    gscore = []
    rw = lax.broadcasted_iota(I32, (_GROUP_SIZE, tm), 0).astype(F32)
    for g in range(N_EXPERT_GROUPS):
        blk = sel[g * _GROUP_SIZE:(g + 1) * _GROUP_SIZE, :]
        m1 = jnp.max(blk, axis=0, keepdims=True)
        i1 = jnp.min(jnp.where(blk == m1, rw, _BIG), axis=0, keepdims=True)
        m2 = jnp.max(jnp.where(rw == i1, NEG_INF, blk), axis=0, keepdims=True)
        gscore.append(m1 + m2)

    gkeep = [jnp.zeros((1, tm), F32) for _ in range(N_EXPERT_GROUPS)]
    for _ in range(TOPK_GROUPS):
        mg = functools.reduce(jnp.maximum, gscore)
        ig = functools.reduce(jnp.minimum, [jnp.where(gscore[g] == mg, float(g), _BIG)
                                            for g in range(N_EXPERT_GROUPS)])
        for g in range(N_EXPERT_GROUPS):
            hit = ig == float(g)
            gkeep[g] = jnp.where(hit, 1.0, gkeep[g])
            gscore[g] = jnp.where(hit, NEG_INF, gscore[g])
    n_slabs = N_EXPERTS // SUBLANES
    slabs_per_group = _GROUP_SIZE // SUBLANES
    sub = lax.broadcasted_iota(I32, (SUBLANES, tm), 0).astype(F32)
    first_rows = [jnp.where(gkeep[j // slabs_per_group] > 0.0, sel[j * SUBLANES:(j + 1) * SUBLANES, :], NEG_INF)
                  for j in range(n_slabs)]
    cur_rows = list(first_rows)
    krow = lax.broadcasted_iota(I32, (TOP_K, tm), 0)
    idx_rows = []
    idx_f = jnp.zeros((TOP_K, tm), F32)
    sc_k = jnp.zeros((TOP_K, tm), F32)
    sc_sum = jnp.zeros((1, tm), F32)
    prev = None
    for k in range(TOP_K):
        best = jnp.full((SUBLANES, tm), NEG_INF, F32)
        best_slab = jnp.zeros((SUBLANES, tm), F32)
        best_score = jnp.zeros((SUBLANES, tm), F32)
        for j in range(n_slabs):
            if prev is not None:
                cur_rows[j] = jnp.where(sub == prev - float(j * SUBLANES), NEG_INF, cur_rows[j])
            better = cur_rows[j] > best
            best = jnp.where(better, cur_rows[j], best)
            best_slab = jnp.where(better, float(j), best_slab)
            best_score = jnp.where(better, scores[j * SUBLANES:(j + 1) * SUBLANES, :], best_score)
        best_idx = best_slab * float(SUBLANES) + sub
        m = jnp.max(best, axis=0, keepdims=True)
        ik = jnp.min(jnp.where(best == m, best_idx, _BIG), axis=0, keepdims=True)
        sk = jnp.sum(jnp.where(best_idx == ik, best_score, 0.0), axis=0, keepdims=True)
        prev = ik
        idx_rows.append(ik)
        idx_f = jnp.where(krow == k, ik, idx_f)
        sc_k = jnp.where(krow == k, sk, sc_k)
        sc_sum = sc_sum + sk
    wts_ref[...] = sc_k / sc_sum * ROUTED_SCALE
    idx_ref[...] = idx_f.astype(I32)
    onehot = jnp.concatenate(
        [jnp.where(jnp.where(sub == prev - float(j * SUBLANES), NEG_INF, cur_rows[j]) != first_rows[j], 1.0, 0.0)
         for j in range(n_slabs)], axis=0)

    tri = (lax.broadcasted_iota(I32, (tm, tm), 0) < lax.broadcasted_iota(I32, (tm, tm), 1))
    excl = jnp.dot(onehot.astype(BF16), tri.astype(BF16), preferred_element_type=F32)
    rank_e = carry_scr[:, 0:1] + excl
    rank_k = jnp.zeros((TOP_K, tm), F32)
    for k in range(TOP_K):
        hit = row == idx_rows[k]
        rk = jnp.sum(jnp.where(hit, rank_e, 0.0), axis=0, keepdims=True)
        rank_k = jnp.where(krow == k, rk, rank_k)
    rank_ref[...] = rank_k.astype(I32)
    carry_scr[...] = carry_scr[...] + jnp.sum(onehot, axis=1, keepdims=True)
    cnt_ref[...] = carry_scr[...]


def _mix_out(attn, yc, x2, mod3, gattn, wout, gpost, gpre2, wrt, br, seq):
    t, d = x2.shape
    tm = min(TM_OUT, seq)
    tpb = seq // tm
    n_tiles = t // tm
    last = n_tiles - 1
    full = lambda a: pl.BlockSpec(a.shape, lambda i: (0,) * a.ndim)
    row = lambda w: pl.BlockSpec((tm, w), lambda i: (jnp.minimum(i, last), 0))
    col = pl.BlockSpec((TOP_K, tm), lambda i: (0, jnp.maximum(i - 1, 0)))
    return pl.pallas_call(
        _mix_out_kernel,
        grid=(n_tiles + 1,),
        in_specs=[row(attn.shape[1]), row(yc.shape[1]), row(d),
                  pl.BlockSpec((1, 6, d), lambda i: (jnp.minimum(i, last) // tpb, 0, 0)),
                  full(gattn), full(wout), full(gpost), full(gpre2), full(wrt), full(br)],
        out_specs=[row(d), row(d), row(_ROW_WORDS), col, col, col,
                   pl.BlockSpec((N_EXPERTS, LANES), lambda i: (0, 0))],
        out_shape=[jax.ShapeDtypeStruct((t, d), F32), jax.ShapeDtypeStruct((t, d), BF16),
                   jax.ShapeDtypeStruct((t, _ROW_WORDS), U32),
                   jax.ShapeDtypeStruct((TOP_K, t), I32), jax.ShapeDtypeStruct((TOP_K, t), F32),
                   jax.ShapeDtypeStruct((TOP_K, t), I32),
                   jax.ShapeDtypeStruct((N_EXPERTS, LANES), F32)],
        scratch_shapes=[pltpu.VMEM((N_EXPERTS, LANES), F32), pltpu.VMEM((2, N_EXPERTS, tm), F32),
                        pltpu.VMEM((2, N_EXPERTS, tm), F32)],
        compiler_params=_params(("arbitrary",)),
        name="mix_out",
    )(attn, yc, x2, mod3, gattn, wout, gpost, gpre2, wrt, br)


def _dest_kernel(idx_ref, rank_ref, pstart_ref, dest_ref):
    tm = idx_ref.shape[1]
    row = lax.broadcasted_iota(I32, (N_EXPERTS, tm), 0)
    krow = lax.broadcasted_iota(I32, (TOP_K, tm), 0)
    pstart = pstart_ref[...]
    idx = idx_ref[...]
    out = jnp.zeros((TOP_K, tm), F32)
    for k in range(TOP_K):
        hit = row == idx[k:k + 1, :]
        base = jnp.sum(jnp.where(hit, pstart, 0.0), axis=0, keepdims=True)
        out = jnp.where(krow == k, base, out)
    dest_ref[...] = out.astype(I32) + rank_ref[...]


def _dest(idx, rank, pstart):
    t = idx.shape[1]
    tm = min(TM_DEST, t)
    col = pl.BlockSpec((TOP_K, tm), lambda i: (0, i))
    return pl.pallas_call(
        _dest_kernel,
        grid=(t // tm,),
        in_specs=[col, col, pl.BlockSpec((N_EXPERTS, 1), lambda i: (0, 0))],
        out_specs=col,
        out_shape=jax.ShapeDtypeStruct((TOP_K, t), I32),
        compiler_params=_params(("arbitrary",)),
        name="dest",
    )(idx, rank, pstart)


_PAD_CHUNKS = tuple(BM_EXPERT >> s for s in range(1, BM_EXPERT.bit_length()))


SC_CORES = 2
SC_SUBCORES = 16
SC_ROWS = 128


def _sc_worker_split(n_chunks):
    workers = SC_CORES * SC_SUBCORES
    per_worker = max(1, n_chunks // workers)
    active = n_chunks // per_worker
    assert active * per_worker == n_chunks and active <= workers
    return per_worker, active


def _scatter_rows_sc(rows, idx3d, n_out):
    n, width = rows.shape
    n_chunks = n // SC_ROWS
    per_worker, active = _sc_worker_split(n_chunks)
    mesh = plsc.VectorSubcoreMesh(core_axis_name="c", subcore_axis_name="s")

    @functools.partial(
        pl.kernel, mesh=mesh, out_type=jax.ShapeDtypeStruct((n_out, width), rows.dtype),
        scratch_types=[pltpu.VMEM((TOP_K, SC_ROWS), I32), pltpu.VMEM((SC_ROWS, width), rows.dtype),
                       pltpu.SemaphoreType.DMA])
    def scatter(rows_hbm, idx_hbm, out_hbm, idx_v, rows_v, sem):
        wid = lax.axis_index("s") * SC_CORES + lax.axis_index("c")

        @pl.when(wid < active)
        def _():
            @pl.loop(0, per_worker)
            def _(c):
                chunk = wid * per_worker + c
                pltpu.sync_copy(rows_hbm.at[pl.ds(chunk * SC_ROWS, SC_ROWS)], rows_v)
                pltpu.sync_copy(idx_hbm.at[chunk], idx_v)
                copies = [pltpu.make_async_copy(rows_v, out_hbm.at[idx_v.at[k]], sem) for k in range(TOP_K)]
                for cp in copies:
                    cp.start()
                for cp in copies:
                    cp.wait()

    return scatter(rows, idx3d)


def _padfill_kernel(pad_from_ref, pad_n_ref, xs_in_hbm, xs_hbm, zero_scr, pad_sem):
    del xs_in_hbm
    zero_scr[...] = jnp.zeros(zero_scr.shape, zero_scr.dtype)

    def pad_copies(e, act):
        n = pad_n_ref[e]
        base = pad_from_ref[e]

        def single_rows(start, count):
            for j in range(SUBLANES - 1):
                @pl.when(j < count)
                def _():
                    act(pltpu.make_async_copy(zero_scr.at[pl.ds(0, 1), :],
                                              xs_hbm.at[pl.ds(start + j, 1), :], pad_sem))

        head = jnp.minimum(n, (SUBLANES - (base & (SUBLANES - 1))) & (SUBLANES - 1))
        single_rows(base, head)
        rest = n - head
        mid = base + head
        for rows in _PAD_CHUNKS:
            if rows >= SUBLANES:
                @pl.when((rest & rows) != 0)
                def _():
                    start = pl.multiple_of(mid + (rest & ~(2 * rows - 1)), SUBLANES)
                    act(pltpu.make_async_copy(zero_scr.at[pl.ds(0, rows), :],
                                              xs_hbm.at[pl.ds(start, rows), :], pad_sem))
        single_rows(mid + (rest & ~(SUBLANES - 1)), rest & (SUBLANES - 1))

    def issue_pad(e, carry):
        pad_copies(e, lambda cp: cp.start())
        return carry

    def drain_pad(e, carry):
        pad_copies(e, lambda cp: cp.wait())
        return carry

    lax.fori_loop(0, N_EXPERTS, issue_pad, 0)
    lax.fori_loop(0, N_EXPERTS, drain_pad, 0)


def _padfill(pad_from, pad_n, xs):
    return pl.pallas_call(
        _padfill_kernel,
        grid_spec=pltpu.PrefetchScalarGridSpec(
            num_scalar_prefetch=2,
            grid=(1,),
            in_specs=[pl.BlockSpec(memory_space=pl.ANY)],
            out_specs=pl.BlockSpec(memory_space=pl.ANY),
            scratch_shapes=[pltpu.VMEM((BM_EXPERT // 2, _ROW_WORDS), U32), pltpu.SemaphoreType.DMA]),
        out_shape=jax.ShapeDtypeStruct(xs.shape, xs.dtype),
        input_output_aliases={2: 0},
        compiler_params=_params(("arbitrary",)),
        name="padfill",
    )(pad_from, pad_n, xs)


_XS_SLOTS = 4
_YS_SLOTS = 2
_W_SLOTS = 4
_W_AHEAD = 2


def _expert_kernel(first_ref, ord_ref, uexp_ref, meta_ref, xs_hbm, wg_hbm, wu_hbm, wd_hbm, ys_hbm,
                   xs_buf, ys_buf, wg_buf, wu_buf, wd_buf, act_scr, xs_sem, ys_sem, w_sem):
    i = pl.program_id(0)
    nused = meta_ref[0]
    nexp = meta_ref[1]
    bm = xs_buf.shape[1]

    def xs_copy(b, slot):
        return pltpu.make_async_copy(xs_hbm.at[pl.ds(b * bm, bm), :], xs_buf.at[slot], xs_sem.at[slot])

    def ys_copy(b, slot):
        return pltpu.make_async_copy(ys_buf.at[slot], ys_hbm.at[pl.ds(b * bm, bm), :], ys_sem.at[slot])

    def w_copies(j, slot):
        e = uexp_ref[j]
        return (pltpu.make_async_copy(wg_hbm.at[e], wg_buf.at[slot], w_sem.at[slot, 0]),
                pltpu.make_async_copy(wu_hbm.at[e], wu_buf.at[slot], w_sem.at[slot, 1]),
                pltpu.make_async_copy(wd_hbm.at[e], wd_buf.at[slot], w_sem.at[slot, 2]))

    @pl.when(i == 0)
    def _():
        for s in range(_XS_SLOTS - 1):
            @pl.when(s < nused)
            def _():
                xs_copy(s, s).start()
        for s in range(_W_AHEAD):
            @pl.when(s < nexp)
            def _():
                for cp in w_copies(s, s):
                    cp.start()

    def fetch(b):
        ahead = b + _XS_SLOTS - 1

        @pl.when(ahead < nused)
        def _():
            xs_copy(ahead, ahead % _XS_SLOTS).start()

        j = ord_ref[b]

        @pl.when(first_ref[b] == 1)
        def _():
            for cp in w_copies(j, j % _W_SLOTS):
                cp.wait()
            nxt = j + _W_AHEAD

            @pl.when(nxt < nexp)
            def _():
                for cp in w_copies(nxt, nxt % _W_SLOTS):
                    cp.start()

        xs_copy(b, b % _XS_SLOTS).wait()

    def gate_up(b):
        ws = ord_ref[b] % _W_SLOTS
        lo, hi = _unpack_row_words(xs_buf[b % _XS_SLOTS])
        g = (jnp.dot(lo, wg_buf[ws, 0:_ROW_WORDS, :], preferred_element_type=F32)
             + jnp.dot(hi, wg_buf[ws, _ROW_WORDS:, :], preferred_element_type=F32))
        u = (jnp.dot(lo, wu_buf[ws, 0:_ROW_WORDS, :], preferred_element_type=F32)
             + jnp.dot(hi, wu_buf[ws, _ROW_WORDS:, :], preferred_element_type=F32))
        return g * jax.nn.sigmoid(g) * u

    def down(b, act):
        y = jnp.dot(act, wd_buf[ord_ref[b] % _W_SLOTS], preferred_element_type=F32)
        oslot = b % _YS_SLOTS
        ys_buf[oslot] = _pack_row_words(y[:, 0:_ROW_WORDS], y[:, _ROW_WORDS:])
        ys_copy(b, oslot).start()

    @pl.when(jnp.logical_and(i >= _YS_SLOTS + 1, i <= nused))
    def _():
        ys_copy(i - 1 - _YS_SLOTS, (i - 1) % _YS_SLOTS).wait()

    @pl.when(i == 0)
    def _():
        fetch(i)
        act_scr[...] = gate_up(i)

    @pl.when(jnp.logical_and(i >= 1, i < nused))
    def _():
        fetch(i)
        prev = act_scr[...]
        act_scr[...] = gate_up(i)
        down(i - 1, prev)

    @pl.when(i == nused)
    def _():
        down(i - 1, act_scr[...])
        ys_copy(i - 1, (i - 1) % _YS_SLOTS).wait()

        @pl.when(i >= 2)
        def _():
            ys_copy(i - 2, (i - 2) % _YS_SLOTS).wait()


def _experts(first, ordinal, uexp, meta, xs, w_gate, w_up, w_down):
    p = xs.shape[0]
    d = w_gate.shape[1]
    nb = p // BM_EXPERT
    anyspec = pl.BlockSpec(memory_space=pl.ANY)
    return pl.pallas_call(
        _expert_kernel,
        grid_spec=pltpu.PrefetchScalarGridSpec(
            num_scalar_prefetch=4,
            grid=(nb + 1,),
            in_specs=[anyspec, anyspec, anyspec, anyspec],
            out_specs=anyspec,
            scratch_shapes=[pltpu.VMEM((_XS_SLOTS, BM_EXPERT, _ROW_WORDS), U32),
                            pltpu.VMEM((_YS_SLOTS, BM_EXPERT, _ROW_WORDS), U32),
                            pltpu.VMEM((_W_SLOTS, d, EXPERT_DIM), F32),
                            pltpu.VMEM((_W_SLOTS, d, EXPERT_DIM), F32),
                            pltpu.VMEM((_W_SLOTS, EXPERT_DIM, d), F32),
                            pltpu.VMEM((BM_EXPERT, EXPERT_DIM), F32),
                            pltpu.SemaphoreType.DMA((_XS_SLOTS,)), pltpu.SemaphoreType.DMA((_YS_SLOTS,)),
                            pltpu.SemaphoreType.DMA((_W_SLOTS, 3))]),
        out_shape=jax.ShapeDtypeStruct((p, _ROW_WORDS), U32),
        compiler_params=_params(("arbitrary",)),
        name="experts",
    )(first, ordinal, uexp, meta, xs, w_gate, w_up, w_down)


SC_GATHER_ROWS = 64


def _gather_rows_sc(table, idx):
    n = idx.shape[0]
    width = table.shape[1]
    rows = SC_GATHER_ROWS
    per_worker, active = _sc_worker_split(n // rows)
    assert per_worker % 2 == 0 or per_worker == 1
    mesh = plsc.VectorSubcoreMesh(core_axis_name="c", subcore_axis_name="s")

    @functools.partial(
        pl.kernel, mesh=mesh, out_type=jax.ShapeDtypeStruct((n, width), table.dtype),
        scratch_types=[pltpu.VMEM((per_worker, rows), I32), pltpu.VMEM((2, rows, width), table.dtype),
                       pltpu.SemaphoreType.DMA((2,))])
    def gather(table_hbm, idx_hbm, out_hbm, idx_v, rows_v, sem):
        wid = lax.axis_index("s") * SC_CORES + lax.axis_index("c")

        def fetch(c, b):
            return pltpu.make_async_copy(table_hbm.at[idx_v.at[c]], rows_v.at[b], sem.at[b])

        @pl.when(wid < active)
        def _():
            first = wid * per_worker
            pltpu.sync_copy(idx_hbm.at[pl.ds(first, per_worker)], idx_v)
            fetch(0, 0).start()

            @pl.loop(0, per_worker, step=2)
            def _(c):
                for b in range(min(2, per_worker)):
                    cur = c + b

                    @pl.when(cur + 1 < per_worker)
                    def _():
                        fetch(cur + 1, 1 - b).start()

                    fetch(cur, b).wait()
                    pltpu.sync_copy(rows_v.at[b], out_hbm.at[pl.ds((first + cur) * rows, rows)])

    return gather(table, idx.reshape(n // rows, rows))


def _combine_kernel(wts_ref, yg_ref, h2_ref, x1_ref, mod_ref, wsg_ref, wsu_ref, wsd_ref, gpost_ref, *rest):
    o_ref = rest[-1]
    h2 = h2_ref[...]
    g = jnp.dot(h2, wsg_ref[...], preferred_element_type=F32)
    u = jnp.dot(h2, wsu_ref[...], preferred_element_type=F32)
    f = jnp.dot((g * jax.nn.sigmoid(g) * u).astype(BF16), wsd_ref[...], preferred_element_type=F32)

    wts = wts_ref[...]
    los = [f[:, sl * LANES:(sl + 1) * LANES] for sl in range(_ROW_SLABS)]
    his = [f[:, _ROW_WORDS + sl * LANES:_ROW_WORDS + (sl + 1) * LANES] for sl in range(_ROW_SLABS)]
    for k in range(TOP_K):
        wk = wts[:, k:k + 1]
        for sl in range(_ROW_SLABS):
            lo, hi = _unpack_row_words(yg_ref[k, :, sl * LANES:(sl + 1) * LANES])
            los[sl] = los[sl] + wk * lo
            his[sl] = his[sl] + wk * hi
    f = jnp.concatenate(los + his, axis=1)
    g2 = mod_ref[0, 5:6, :]
    o_ref[...] = x1_ref[...] + g2 * _rms(f, gpost_ref[...])


def _combine(wts_t, yg, h2, x1, mod3, wsg, wsu, wsd, gpost, seq, first_tile, partial_out):
    t, d = x1.shape
    tm = min(TM_COMBINE, seq)
    tpb = seq // tm
    full = lambda a: pl.BlockSpec(a.shape, lambda i: (0,) * a.ndim)
    row = lambda w: pl.BlockSpec((tm, w), lambda i: (i + first_tile, 0))
    args = [wts_t, yg, h2, x1, mod3, wsg, wsu, wsd, gpost]
    in_specs = [row(TOP_K), pl.BlockSpec((TOP_K, tm, _ROW_WORDS), lambda i: (0, i, 0)), row(d), row(d),
                pl.BlockSpec((1, 6, d), lambda i: ((i + first_tile) // tpb, 0, 0)),
                full(wsg), full(wsu), full(wsd), full(gpost)]
    aliases = {}
    if partial_out is not None:
        aliases = {len(args): 0}
        args.append(partial_out)
        in_specs.append(pl.BlockSpec(memory_space=pl.ANY))
    return pl.pallas_call(
        _combine_kernel,
        grid=(yg.shape[1] // tm,),
        in_specs=in_specs,
        out_specs=row(d),
        out_shape=jax.ShapeDtypeStruct((t, d), F32),
        input_output_aliases=aliases,
        compiler_params=_params(("arbitrary",)),
        name="combine",
    )(*args)


def _pack_weights(w_in, w_uq, w_ukv):
    d = w_in.shape[0]
    half = QK_ROPE_DIM // 2
    z = lambda n, c: jnp.zeros((n, c), F32)
    o = Q_LORA_RANK + KV_LORA_RANK
    kr = w_in[:, o:o + QK_ROPE_DIM]
    kr_grp = jnp.concatenate([z(d, QK_NOPE_DIM), kr, z(d, HEAD_PAD - QK_NOPE_DIM - QK_ROPE_DIM)], axis=1)
    kr_rot = jnp.concatenate([z(d, QK_NOPE_DIM), -kr[:, half:], kr[:, :half],
                              z(d, HEAD_PAD - QK_NOPE_DIM - QK_ROPE_DIM)], axis=1)
    win_p = jnp.concatenate([w_in[:, :o], kr_grp, kr_rot, w_in[:, o + QK_ROPE_DIM:]], axis=1)

    scale = float(QK_NOPE_DIM + QK_ROPE_DIM) ** -0.5 * float(np.log2(np.e))
    r = Q_LORA_RANK
    qd = QK_NOPE_DIM + QK_ROPE_DIM
    q_grp, q_rot = [], []
    for h in range(MLA_HEADS):
        nope = w_uq[:, h * qd:h * qd + QK_NOPE_DIM]
        rope = w_uq[:, h * qd + QK_NOPE_DIM:(h + 1) * qd]
        pad = z(r, HEAD_PAD - qd)
        q_grp.append(jnp.concatenate([nope, rope, pad], axis=1))
        q_rot.append(jnp.concatenate([z(r, QK_NOPE_DIM), -rope[:, half:], rope[:, :half], pad], axis=1))
    wuq_p = jnp.concatenate(q_grp + q_rot, axis=1) * scale

    c = KV_LORA_RANK
    kd = QK_NOPE_DIM + V_HEAD_DIM
    k_grp, v_grp = [], []
    for h in range(MLA_HEADS):
        k_grp.append(jnp.concatenate([w_ukv[:, h * kd:h * kd + QK_NOPE_DIM], z(c, HEAD_PAD - QK_NOPE_DIM)], axis=1))
        v_grp.append(jnp.concatenate([w_ukv[:, h * kd + QK_NOPE_DIM:(h + 1) * kd], z(c, HEAD_PAD - V_HEAD_DIM)], axis=1))
    wukv_p = jnp.concatenate(k_grp + v_grp, axis=1)
    return win_p.astype(BF16), wuq_p.astype(BF16), wukv_p.astype(BF16)


def _rope_inputs(positions):
    inv = 1.0 / (ROPE_THETA ** (jnp.arange(0, QK_ROPE_DIM, 2, dtype=F32) / QK_ROPE_DIM))
    freq = jnp.tile(inv, LANES // inv.shape[0]).reshape(1, LANES)
    pos = jnp.repeat(positions.astype(F32).reshape(-1, _ROPE_PACK), QK_ROPE_DIM, axis=1)
    return pos, freq


def _layer(x2, c, pos, freq, batch, seq, w_ada, b_ada, g_pre_mix, w_in, g_q_lat, w_uq, g_kv_lat, w_ukv,
           w_conv, g_attn_out, g_conv_out, w_out, g_post_mix, g_pre_ffn, w_router, b_router,
           w_gate, w_up, w_down, w_sh_gate, w_sh_up, w_sh_down, g_post_ffn):
    t, d = x2.shape
    r1 = lambda a: a.reshape(1, -1)

    c_pad = jnp.zeros((SUBLANES, d), F32).at[:batch].set(c)
    mod = _ada(c_pad, w_ada, r1(b_ada))[:batch]
    mod3 = mod.reshape(batch, 6, d)

    win_p, wuq_p, wukv_p = _pack_weights(w_in, w_uq, w_ukv)
    vone = jnp.zeros((1, HEAD_PAD), F32).at[0, V_HEAD_DIM].set(1.0)
    q, k, v, yc = _mix_in(x2, mod3, r1(g_pre_mix), win_p, r1(g_q_lat), wuq_p, r1(g_kv_lat), wukv_p,
                          vone, w_conv, r1(g_conv_out), pos, freq, seq)
    attn = _attention(q, k, v, batch, seq)
    x1, h2, h2p, idx, wts, rank, cnt = _mix_out(
        attn, yc, x2, mod3, r1(g_attn_out), w_out.astype(BF16), r1(g_post_mix), r1(g_pre_ffn),
        w_router.T, b_router.reshape(-1, 1), seq)

    counts = cnt[:, 0].astype(I32)
    padded = ((counts + BM_EXPERT - 1) // BM_EXPERT) * BM_EXPERT
    pad_end = jnp.cumsum(padded)
    pad_start = pad_end - padded
    m = t * TOP_K
    nb = (m + N_EXPERTS * (BM_EXPERT - 1)) // BM_EXPERT
    nused = pad_end[-1] // BM_EXPERT
    bidx = jnp.arange(nb, dtype=I32)
    blk_exp = jnp.sum((pad_end[None, :] <= (bidx * BM_EXPERT)[:, None]).astype(I32), axis=1)
    first = ((bidx < nused) & ((bidx == 0) | (blk_exp != jnp.roll(blk_exp, 1)))).astype(I32)
    ordinal = jnp.maximum(jnp.cumsum(first) - 1, 0).astype(I32)
    seen = jnp.cumsum((counts > 0).astype(I32))
    uexp = jnp.minimum(jnp.sum((seen[None, :] <= jnp.arange(N_EXPERTS, dtype=I32)[:, None]).astype(I32), axis=1),
                       N_EXPERTS - 1).astype(I32)
    meta = jnp.stack([nused, seen[-1]]).astype(I32)

    dest = _dest(idx, rank, pad_start.astype(F32).reshape(-1, 1))
    xs = _scatter_rows_sc(h2p, dest.reshape(TOP_K, -1, SC_ROWS).transpose(1, 0, 2), nb * BM_EXPERT)
    xs = _padfill((pad_start + counts).astype(I32), (padded - counts).astype(I32), xs)
    ys = _experts(first, ordinal, uexp, meta, xs, w_gate, w_up, w_down)
    parts = 2 if t % (2 * TM_COMBINE * SC_CORES * SC_SUBCORES) == 0 else 1
    tp = t // parts
    wsg, wsu, wsd = w_sh_gate.astype(BF16), w_sh_up.astype(BF16), w_sh_down.astype(BF16)
    out = None
    for part in range(parts):
        idx = dest[:, part * tp:(part + 1) * tp].reshape(-1)
        yg = _gather_rows_sc(ys, idx).reshape(TOP_K, tp, _ROW_WORDS)
        out = _combine(wts.T, yg, h2, x1, mod3, wsg, wsu, wsd, r1(g_post_ffn), seq,
                       part * tp // min(TM_COMBINE, seq), out)
    return out


def kernel(x, c, positions, w_ada, b_ada, g_pre_mix, w_in, g_q_lat, w_uq, g_kv_lat, w_ukv, w_conv, g_attn_out, g_conv_out, w_out, g_post_mix, g_pre_ffn, w_router, b_router, w_gate, w_up, w_down, w_sh_gate, w_sh_up, w_sh_down, g_post_ffn):
    batch, seq, d = x.shape
    pos, freq = _rope_inputs(positions)
    x2 = x.reshape(batch * seq, d)
    for l in range(w_ada.shape[0]):
        x2 = _layer(x2, c, pos, freq, batch, seq, w_ada[l], b_ada[l], g_pre_mix[l], w_in[l], g_q_lat[l],
                    w_uq[l], g_kv_lat[l], w_ukv[l], w_conv[l], g_attn_out[l], g_conv_out[l], w_out[l],
                    g_post_mix[l], g_pre_ffn[l], w_router[l], b_router[l], w_gate[l], w_up[l], w_down[l],
                    w_sh_gate[l], w_sh_up[l], w_sh_down[l], g_post_ffn[l])
    return x2.reshape(batch, seq, d)
```

```python
import functools

import jax
import jax.numpy as jnp
import numpy as np
from jax import lax
from jax.experimental import pallas as pl
from jax.experimental.pallas import tpu as pltpu
from jax.experimental.pallas import tpu_sc as plsc

F32 = jnp.float32
BF16 = jnp.bfloat16
I32 = jnp.int32
U32 = jnp.uint32

CHUNK = 64
MLA_HEADS = 8
QK_NOPE_DIM = 64
QK_ROPE_DIM = 32
V_HEAD_DIM = 64
Q_LORA_RANK = 384
KV_LORA_RANK = 256
ROPE_THETA = 10000.0
CONV_WIDTH = 3
N_EXPERTS = 256
TOP_K = 8
N_EXPERT_GROUPS = 8
TOPK_GROUPS = 4
EXPERT_DIM = 256
ROUTED_SCALE = 2.5
EPS = 1e-6

LANES = 128
SUBLANES = 8
HEAD_PAD = LANES
VMEM_LIMIT_BYTES = 56 * 1024 * 1024

TM_IN = 512
TQ_ATTN = 512
TM_OUT = 512
TM_DEST = 512
BM_EXPERT = 256
TM_COMBINE = 512

NEG_INF = float("-inf")


def _rms(x, g):
    return x * lax.rsqrt(jnp.mean(x * x, axis=-1, keepdims=True) + EPS) * g


_HI_MASK = np.uint32(0xFFFF0000)
_ROW_WORDS = 512
_ROW_SLABS = _ROW_WORDS // LANES


def _pack_row_words(lo, hi):
    lo_w = lax.bitcast_convert_type(lo.astype(BF16).astype(F32), U32) >> 16
    hi_w = lax.bitcast_convert_type(hi.astype(BF16).astype(F32), U32) & _HI_MASK
    return lo_w | hi_w


def _unpack_row_words(w):
    return (lax.bitcast_convert_type(w << 16, F32), lax.bitcast_convert_type(w & _HI_MASK, F32))


def _params(sem):
    return pltpu.CompilerParams(dimension_semantics=sem, vmem_limit_bytes=VMEM_LIMIT_BYTES)


def _ada_kernel(c_ref, w_ref, b_ref, o_ref):
    c = c_ref[...]
    s = c * jax.nn.sigmoid(c)
    o_ref[...] = jnp.dot(s, w_ref[...], preferred_element_type=F32,
                         precision=lax.Precision.HIGHEST) + b_ref[...]


def _ada(c_pad, w, b):
    rows, d = c_pad.shape
    n = w.shape[1]
    tn = 1536
    return pl.pallas_call(
        _ada_kernel,
        grid=(n // tn,),
        in_specs=[pl.BlockSpec((rows, d), lambda j: (0, 0)),
                  pl.BlockSpec((d, tn), lambda j: (0, j)),
                  pl.BlockSpec((1, tn), lambda j: (0, j))],
        out_specs=pl.BlockSpec((rows, tn), lambda j: (0, j)),
        out_shape=jax.ShapeDtypeStruct((rows, n), F32),
        compiler_params=_params(("arbitrary",)),
        name="ada",
    )(c_pad, w, b)


_CQ0, _CQ1 = 0, Q_LORA_RANK
_CKV0, _CKV1 = _CQ1, _CQ1 + KV_LORA_RANK
_KR0, _KR1 = _CKV1, _CKV1 + 2 * HEAD_PAD
_CONV_DIM = 512
_GB0 = _KR1
_GC0 = _GB0 + _CONV_DIM
_XV0 = _GC0 + _CONV_DIM
_WIN_COLS = _XV0 + _CONV_DIM
_QW = MLA_HEADS * HEAD_PAD
_ROPE_PACK = LANES // QK_ROPE_DIM


def _mix_in_kernel(tiles_per_batch, x_ref, mod_ref, gpre_ref, win_ref, gq_ref, wuq_ref, gkv_ref,
                   wukv_ref, vone_ref, wconv_ref, gconv_ref, pos_ref, freq_ref,
                   q_ref, k_ref, v_ref, yc_ref, h_scr, u_scr, cos_scr, sin_scr):
    i = pl.program_id(0)
    tm = x_ref.shape[0]
    sh1 = mod_ref[0, 0:1, :]
    sc1 = mod_ref[0, 1:2, :]
    h = _rms(x_ref[...], gpre_ref[...]) * (1.0 + sc1) + sh1
    h_scr[...] = h.astype(BF16)
    ang4 = pos_ref[...] * freq_ref[...]
    cos4 = jnp.cos(ang4)
    sin4 = jnp.sin(ang4)
    lane = lax.broadcasted_iota(I32, ang4.shape, 1)
    on_rope = jnp.logical_and(lane >= QK_NOPE_DIM, lane < QK_NOPE_DIM + QK_ROPE_DIM)
    for g in range(_ROPE_PACK):
        shift = (QK_NOPE_DIM - QK_ROPE_DIM * g) % LANES
        cg = cos4 if shift == 0 else pltpu.roll(cos4, shift, axis=1)
        sg = sin4 if shift == 0 else pltpu.roll(sin4, shift, axis=1)
        cos_scr[pl.ds(g, tm // _ROPE_PACK, stride=_ROPE_PACK), :] = jnp.where(on_rope, cg, 1.0)
        sin_scr[pl.ds(g, tm // _ROPE_PACK, stride=_ROPE_PACK), :] = jnp.where(on_rope, sg, 0.0)
    cos = cos_scr[...]
    sin = sin_scr[...]

    cq = jnp.dot(h_scr[...], win_ref[:, _CQ0:_CQ1], preferred_element_type=F32)
    cqn = _rms(cq, gq_ref[...]).astype(BF16)
    qq = jnp.dot(cqn, wuq_ref[...], preferred_element_type=F32)
    for hd in range(MLA_HEADS):
        lo = hd * HEAD_PAD
        qh = qq[:, lo:lo + HEAD_PAD] * cos + qq[:, _QW + lo:_QW + lo + HEAD_PAD] * sin
        q_ref[:, lo:lo + HEAD_PAD] = qh.astype(BF16)

    ckv = jnp.dot(h_scr[...], win_ref[:, _CKV0:_CKV1], preferred_element_type=F32)
    ckvn = _rms(ckv, gkv_ref[...]).astype(BF16)
    kv = jnp.dot(ckvn, wukv_ref[...], preferred_element_type=F32)
    krr = jnp.dot(h_scr[...], win_ref[:, _KR0:_KR1], preferred_element_type=F32)
    kr = krr[:, 0:HEAD_PAD] * cos + krr[:, HEAD_PAD:2 * HEAD_PAD] * sin
    vone = vone_ref[...]
    for hd in range(MLA_HEADS):
        lo = hd * HEAD_PAD
        k_ref[:, lo:lo + HEAD_PAD] = (kv[:, lo:lo + HEAD_PAD] + kr).astype(BF16)
        v_ref[:, lo:lo + HEAD_PAD] = (kv[:, _QW + lo:_QW + lo + HEAD_PAD] + vone).astype(BF16)

    gb = jnp.dot(h_scr[...], win_ref[:, _GB0:_GC0], preferred_element_type=F32)
    gc = jnp.dot(h_scr[...], win_ref[:, _GC0:_XV0], preferred_element_type=F32)
    xv = jnp.dot(h_scr[...], win_ref[:, _XV0:_WIN_COLS], preferred_element_type=F32)
    u = gc * xv
    prev = u_scr[tm:tm + SUBLANES, :]
    first = (i % tiles_per_batch) == 0
    u_scr[0:SUBLANES, :] = jnp.where(first, jnp.zeros_like(prev), prev)
    u_scr[SUBLANES:tm + SUBLANES, :] = u
    um1 = u_scr[SUBLANES - 1:tm + SUBLANES - 1, :]
    um2 = u_scr[SUBLANES - 2:tm + SUBLANES - 2, :]
    assert wconv_ref.shape[0] == CONV_WIDTH
    conv = wconv_ref[0:1, :] * um2 + wconv_ref[1:2, :] * um1 + wconv_ref[2:3, :] * u
    yc_ref[...] = _rms(gb * conv, gconv_ref[...]).astype(BF16)


def _mix_in(x2, mod3, gpre, win_p, gq, wuq_p, gkv, wukv_p, vone, wconv, gconv, pos, freq, seq):
    t, d = x2.shape
    tm = min(TM_IN, seq)
    tpb = seq // tm
    full = lambda a: pl.BlockSpec(a.shape, lambda i: (0,) * a.ndim)
    row = lambda w: pl.BlockSpec((tm, w), lambda i: (i, 0))
    return pl.pallas_call(
        functools.partial(_mix_in_kernel, tpb),
        grid=(t // tm,),
        in_specs=[row(d),
                  pl.BlockSpec((1, 6, d), lambda i: (i // tpb, 0, 0)),
                  full(gpre), full(win_p), full(gq), full(wuq_p), full(gkv), full(wukv_p),
                  full(vone), full(wconv), full(gconv),
                  pl.BlockSpec((tm // _ROPE_PACK, LANES), lambda i: (i, 0)), full(freq)],
        out_specs=[row(_QW), row(_QW), row(_QW), row(_CONV_DIM)],
        out_shape=[jax.ShapeDtypeStruct((t, _QW), BF16), jax.ShapeDtypeStruct((t, _QW), BF16),
                   jax.ShapeDtypeStruct((t, _QW), BF16), jax.ShapeDtypeStruct((t, _CONV_DIM), BF16)],
        scratch_shapes=[pltpu.VMEM((tm, d), BF16), pltpu.VMEM((tm + SUBLANES, _CONV_DIM), F32),
                        pltpu.VMEM((tm, HEAD_PAD), F32), pltpu.VMEM((tm, HEAD_PAD), F32)],
        compiler_params=_params(("arbitrary",)),
        name="mix_in",
    )(x2, mod3, gpre, win_p, gq, wuq_p, gkv, wukv_p, vone, wconv, gconv, pos, freq)


_HEADS_PER_STEP = 2


def _attn_kernel(tq, q_ref, k_ref, v_ref, o_ref, s_scr, mrun_scr, mb_scr, acc_scr):
    tk = tq
    nq = q_ref.shape[0] // tq
    lane_groups = tk // LANES
    heads = range(_HEADS_PER_STEP)
    lanes = [slice(hh * HEAD_PAD, (hh + 1) * HEAD_PAD) for hh in heads]

    def tile_max(s):
        m = s[:, 0:LANES]
        for g in range(1, lane_groups):
            m = jnp.maximum(m, s[:, g * LANES:(g + 1) * LANES])
        return m

    def scores(hh, qi, kv):
        off = pl.multiple_of(kv * tk, tk)
        return lax.dot_general(q_ref[qi * tq:(qi + 1) * tq, lanes[hh]], k_ref[pl.ds(off, tk), lanes[hh]],
                               (((1,), (1,)), ((), ())), preferred_element_type=F32)

    rc = lax.broadcasted_iota(I32, (tq, tk), 0) // CHUNK
    cc = lax.broadcasted_iota(I32, (tq, tk), 1) // CHUNK

    def diagonal(qi):
        for hh in heads:
            s = jnp.where(cc <= rc, scores(hh, qi, qi), NEG_INF)
            s_scr[hh, qi] = s
            m_row = jnp.max(jnp.maximum(mrun_scr[hh], tile_max(s)), axis=1, keepdims=True)
            mb_scr[hh] = jnp.broadcast_to(m_row, (tq, LANES))

    mrun_scr[...] = jnp.full(mrun_scr.shape, NEG_INF, F32)
    diagonal(0)
    for qi in range(nq):
        has_next = qi + 1 < nq
        acc_scr[...] = jnp.zeros(acc_scr.shape, F32)
        if has_next:
            mrun_scr[...] = jnp.full(mrun_scr.shape, NEG_INF, F32)

        def body(kv, carry, qi=qi, has_next=has_next):
            off = pl.multiple_of(kv * tk, tk)
            for hh in heads:
                mb = mb_scr[hh]
                p = jnp.concatenate(
                    [jnp.exp2(s_scr[hh, kv, :, g * LANES:(g + 1) * LANES] - mb) for g in range(lane_groups)],
                    axis=1).astype(BF16)
                acc_scr[hh] += jnp.dot(p, v_ref[pl.ds(off, tk), lanes[hh]], preferred_element_type=F32)
                if has_next:
                    s = scores(hh, qi + 1, kv)
                    s_scr[hh, kv] = s
                    mrun_scr[hh] = jnp.maximum(mrun_scr[hh], tile_max(s))
            return carry

        lax.fori_loop(0, qi + 1, body, 0, unroll=4)
        for hh in heads:
            acc = acc_scr[hh]
            o = acc[:, 0:V_HEAD_DIM] / acc[:, V_HEAD_DIM:V_HEAD_DIM + 1]
            o_ref[qi * tq:(qi + 1) * tq, hh * V_HEAD_DIM:(hh + 1) * V_HEAD_DIM] = o.astype(BF16)
        if has_next:
            diagonal(qi + 1)


def _attention(q, k, v, batch, seq):
    t = q.shape[0]
    tq = min(TQ_ATTN, seq)
    nq = seq // tq
    hw = _HEADS_PER_STEP * HEAD_PAD
    ow = _HEADS_PER_STEP * V_HEAD_DIM
    blk = lambda w: pl.BlockSpec((seq, w), lambda b, j: (b, j))
    return pl.pallas_call(
        functools.partial(_attn_kernel, tq),
        grid=(batch, MLA_HEADS // _HEADS_PER_STEP),
        in_specs=[blk(hw), blk(hw), blk(hw)],
        out_specs=blk(ow),
        out_shape=jax.ShapeDtypeStruct((t, MLA_HEADS * V_HEAD_DIM), BF16),
        scratch_shapes=[pltpu.VMEM((_HEADS_PER_STEP, nq, tq, tq), F32),
                        pltpu.VMEM((_HEADS_PER_STEP, tq, LANES), F32),
                        pltpu.VMEM((_HEADS_PER_STEP, tq, LANES), F32),
                        pltpu.VMEM((_HEADS_PER_STEP, tq, HEAD_PAD), F32)],
        compiler_params=_params(("arbitrary", "arbitrary")),
        name="attn",
    )(q, k, v)


_GROUP_SIZE = N_EXPERTS // N_EXPERT_GROUPS
_BIG = 1.0e9


def _mix_out_kernel(attn_ref, yc_ref, x_ref, mod_ref, gattn_ref, wout_ref, gpost_ref, gpre2_ref,
                    wrt_ref, br_ref, x1_ref, h2_ref, h2p_ref, idx_ref, wts_ref, rank_ref, cnt_ref,
                    carry_scr, scores_scr, sel_scr):
    i = pl.program_id(0)
    n_tiles = pl.num_programs(0) - 1

    @pl.when(i == 0)
    def _():
        carry_scr[...] = jnp.zeros(carry_scr.shape, F32)
        scores, sel = _project_tile(attn_ref, yc_ref, x_ref, mod_ref, gattn_ref, wout_ref, gpost_ref,
                                    gpre2_ref, wrt_ref, br_ref, x1_ref, h2_ref, h2p_ref)
        scores_scr[0] = scores
        sel_scr[0] = sel

    @pl.when(jnp.logical_and(i >= 1, i < n_tiles))
    def _():
        prev_scores = scores_scr[(i - 1) % 2]
        prev_sel = sel_scr[(i - 1) % 2]
        scores, sel = _project_tile(attn_ref, yc_ref, x_ref, mod_ref, gattn_ref, wout_ref, gpost_ref,
                                    gpre2_ref, wrt_ref, br_ref, x1_ref, h2_ref, h2p_ref)
        scores_scr[i % 2] = scores
        sel_scr[i % 2] = sel
        _route_tile(prev_scores, prev_sel, idx_ref, wts_ref, rank_ref, cnt_ref, carry_scr)

    @pl.when(i == n_tiles)
    def _():
        _route_tile(scores_scr[(i - 1) % 2], sel_scr[(i - 1) % 2], idx_ref, wts_ref, rank_ref, cnt_ref,
                    carry_scr)


def _project_tile(attn_ref, yc_ref, x_ref, mod_ref, gattn_ref, wout_ref, gpost_ref, gpre2_ref,
                  wrt_ref, br_ref, x1_ref, h2_ref, h2p_ref):
    half = attn_ref.shape[1]
    an = _rms(attn_ref[...].astype(F32), gattn_ref[...]).astype(BF16)
    mix = (jnp.dot(an, wout_ref[0:half, :], preferred_element_type=F32)
           + jnp.dot(yc_ref[...], wout_ref[half:, :], preferred_element_type=F32))
    g1 = mod_ref[0, 2:3, :]
    sh2 = mod_ref[0, 3:4, :]
    sc2 = mod_ref[0, 4:5, :]
    x1 = x_ref[...] + g1 * _rms(mix, gpost_ref[...])
    x1_ref[...] = x1
    h2 = _rms(x1, gpre2_ref[...]) * (1.0 + sc2) + sh2
    h2_ref[...] = h2.astype(BF16)
    h2p_ref[...] = _pack_row_words(h2[:, 0:_ROW_WORDS], h2[:, _ROW_WORDS:])

    logits = lax.dot_general(wrt_ref[...], h2, (((1,), (1,)), ((), ())),
                             preferred_element_type=F32, precision=lax.Precision.HIGHEST)
    scores = jax.nn.sigmoid(logits)
    return scores, scores + br_ref[...]


def _route_tile(scores, sel, idx_ref, wts_ref, rank_ref, cnt_ref, carry_scr):
    tm = scores.shape[1]
    row = lax.broadcasted_iota(I32, (N_EXPERTS, tm), 0).astype(F32)

    gscore = []
    rw = lax.broadcasted_iota(I32, (_GROUP_SIZE, tm), 0).astype(F32)
    for g in range(N_EXPERT_GROUPS):
        blk = sel[g * _GROUP_SIZE:(g + 1) * _GROUP_SIZE, :]
        m1 = jnp.max(blk, axis=0, keepdims=True)
        i1 = jnp.min(jnp.where(blk == m1, rw, _BIG), axis=0, keepdims=True)
        m2 = jnp.max(jnp.where(rw == i1, NEG_INF, blk), axis=0, keepdims=True)
        gscore.append(m1 + m2)

    gkeep = [jnp.zeros((1, tm), F32) for _ in range(N_EXPERT_GROUPS)]
    for _ in range(TOPK_GROUPS):
        mg = functools.reduce(jnp.maximum, gscore)
        ig = functools.reduce(jnp.minimum, [jnp.where(gscore[g] == mg, float(g), _BIG)
                                            for g in range(N_EXPERT_GROUPS)])
        for g in range(N_EXPERT_GROUPS):
            hit = ig == float(g)
            gkeep[g] = jnp.where(hit, 1.0, gkeep[g])
            gscore[g] = jnp.where(hit, NEG_INF, gscore[g])
    n_slabs = N_EXPERTS // SUBLANES
    slabs_per_group = _GROUP_SIZE // SUBLANES
    sub = lax.broadcasted_iota(I32, (SUBLANES, tm), 0).astype(F32)
    first_rows = [jnp.where(gkeep[j // slabs_per_group] > 0.0, sel[j * SUBLANES:(j + 1) * SUBLANES, :], NEG_INF)
                  for j in range(n_slabs)]
    cur_rows = list(first_rows)
    krow = lax.broadcasted_iota(I32, (TOP_K, tm), 0)
    idx_rows = []
    idx_f = jnp.zeros((TOP_K, tm), F32)
    sc_k = jnp.zeros((TOP_K, tm), F32)
    sc_sum = jnp.zeros((1, tm), F32)
    prev = None
    for k in range(TOP_K):
        best = jnp.full((SUBLANES, tm), NEG_INF, F32)
        best_slab = jnp.zeros((SUBLANES, tm), F32)
        best_score = jnp.zeros((SUBLANES, tm), F32)
        for j in range(n_slabs):
            if prev is not None:
                cur_rows[j] = jnp.where(sub == prev - float(j * SUBLANES), NEG_INF, cur_rows[j])
            better = cur_rows[j] > best
            best = jnp.where(better, cur_rows[j], best)
            best_slab = jnp.where(better, float(j), best_slab)
            best_score = jnp.where(better, scores[j * SUBLANES:(j + 1) * SUBLANES, :], best_score)
        best_idx = best_slab * float(SUBLANES) + sub
        m = jnp.max(best, axis=0, keepdims=True)
        ik = jnp.min(jnp.where(best == m, best_idx, _BIG), axis=0, keepdims=True)
        sk = jnp.sum(jnp.where(best_idx == ik, best_score, 0.0), axis=0, keepdims=True)
        prev = ik
        idx_rows.append(ik)
        idx_f = jnp.where(krow == k, ik, idx_f)
        sc_k = jnp.where(krow == k, sk, sc_k)
        sc_sum = sc_sum + sk
    wts_ref[...] = sc_k / sc_sum * ROUTED_SCALE
    idx_ref[...] = idx_f.astype(I32)
    onehot = jnp.concatenate(
        [jnp.where(jnp.where(sub == prev - float(j * SUBLANES), NEG_INF, cur_rows[j]) != first_rows[j], 1.0, 0.0)
         for j in range(n_slabs)], axis=0)

    tri = (lax.broadcasted_iota(I32, (tm, tm), 0) < lax.broadcasted_iota(I32, (tm, tm), 1))
    excl = jnp.dot(onehot.astype(BF16), tri.astype(BF16), preferred_element_type=F32)
    rank_e = carry_scr[:, 0:1] + excl
    rank_k = jnp.zeros((TOP_K, tm), F32)
    for k in range(TOP_K):
        hit = row == idx_rows[k]
        rk = jnp.sum(jnp.where(hit, rank_e, 0.0), axis=0, keepdims=True)
        rank_k = jnp.where(krow == k, rk, rank_k)
    rank_ref[...] = rank_k.astype(I32)
    carry_scr[...] = carry_scr[...] + jnp.sum(onehot, axis=1, keepdims=True)
    cnt_ref[...] = carry_scr[...]


def _mix_out(attn, yc, x2, mod3, gattn, wout, gpost, gpre2, wrt, br, seq):
    t, d = x2.shape
    tm = min(TM_OUT, seq)
    tpb = seq // tm
    n_tiles = t // tm
    last = n_tiles - 1
    full = lambda a: pl.BlockSpec(a.shape, lambda i: (0,) * a.ndim)
    row = lambda w: pl.BlockSpec((tm, w), lambda i: (jnp.minimum(i, last), 0))
    col = pl.BlockSpec((TOP_K, tm), lambda i: (0, jnp.maximum(i - 1, 0)))
    return pl.pallas_call(
        _mix_out_kernel,
        grid=(n_tiles + 1,),
        in_specs=[row(attn.shape[1]), row(yc.shape[1]), row(d),
                  pl.BlockSpec((1, 6, d), lambda i: (jnp.minimum(i, last) // tpb, 0, 0)),
                  full(gattn), full(wout), full(gpost), full(gpre2), full(wrt), full(br)],
        out_specs=[row(d), row(d), row(_ROW_WORDS), col, col, col,
                   pl.BlockSpec((N_EXPERTS, LANES), lambda i: (0, 0))],
        out_shape=[jax.ShapeDtypeStruct((t, d), F32), jax.ShapeDtypeStruct((t, d), BF16),
                   jax.ShapeDtypeStruct((t, _ROW_WORDS), U32),
                   jax.ShapeDtypeStruct((TOP_K, t), I32), jax.ShapeDtypeStruct((TOP_K, t), F32),
                   jax.ShapeDtypeStruct((TOP_K, t), I32),
                   jax.ShapeDtypeStruct((N_EXPERTS, LANES), F32)],
        scratch_shapes=[pltpu.VMEM((N_EXPERTS, LANES), F32), pltpu.VMEM((2, N_EXPERTS, tm), F32),
                        pltpu.VMEM((2, N_EXPERTS, tm), F32)],
        compiler_params=_params(("arbitrary",)),
        name="mix_out",
    )(attn, yc, x2, mod3, gattn, wout, gpost, gpre2, wrt, br)


def _dest_kernel(idx_ref, rank_ref, pstart_ref, dest_ref):
    tm = idx_ref.shape[1]
    row = lax.broadcasted_iota(I32, (N_EXPERTS, tm), 0)
    krow = lax.broadcasted_iota(I32, (TOP_K, tm), 0)
    pstart = pstart_ref[...]
    idx = idx_ref[...]
    out = jnp.zeros((TOP_K, tm), F32)
    for k in range(TOP_K):
        hit = row == idx[k:k + 1, :]
        base = jnp.sum(jnp.where(hit, pstart, 0.0), axis=0, keepdims=True)
        out = jnp.where(krow == k, base, out)
    dest_ref[...] = out.astype(I32) + rank_ref[...]


def _dest(idx, rank, pstart):
    t = idx.shape[1]
    tm = min(TM_DEST, t)
    col = pl.BlockSpec((TOP_K, tm), lambda i: (0, i))
    return pl.pallas_call(
        _dest_kernel,
        grid=(t // tm,),
        in_specs=[col, col, pl.BlockSpec((N_EXPERTS, 1), lambda i: (0, 0))],
        out_specs=col,
        out_shape=jax.ShapeDtypeStruct((TOP_K, t), I32),
        compiler_params=_params(("arbitrary",)),
        name="dest",
    )(idx, rank, pstart)


_PAD_CHUNKS = tuple(BM_EXPERT >> s for s in range(1, BM_EXPERT.bit_length()))


SC_CORES = 2
SC_SUBCORES = 16
SC_ROWS = 128


def _sc_worker_split(n_chunks):
    workers = SC_CORES * SC_SUBCORES
    per_worker = max(1, n_chunks // workers)
    active = n_chunks // per_worker
    assert active * per_worker == n_chunks and active <= workers
    return per_worker, active


def _scatter_rows_sc(rows, idx3d, n_out):
    n, width = rows.shape
    n_chunks = n // SC_ROWS
    per_worker, active = _sc_worker_split(n_chunks)
    mesh = plsc.VectorSubcoreMesh(core_axis_name="c", subcore_axis_name="s")

    @functools.partial(
        pl.kernel, mesh=mesh, out_type=jax.ShapeDtypeStruct((n_out, width), rows.dtype),
        scratch_types=[pltpu.VMEM((TOP_K, SC_ROWS), I32), pltpu.VMEM((SC_ROWS, width), rows.dtype),
                       pltpu.SemaphoreType.DMA])
    def scatter(rows_hbm, idx_hbm, out_hbm, idx_v, rows_v, sem):
        wid = lax.axis_index("s") * SC_CORES + lax.axis_index("c")

        @pl.when(wid < active)
        def _():
            @pl.loop(0, per_worker)
            def _(c):
                chunk = wid * per_worker + c
                pltpu.sync_copy(rows_hbm.at[pl.ds(chunk * SC_ROWS, SC_ROWS)], rows_v)
                pltpu.sync_copy(idx_hbm.at[chunk], idx_v)
                copies = [pltpu.make_async_copy(rows_v, out_hbm.at[idx_v.at[k]], sem) for k in range(TOP_K)]
                for cp in copies:
                    cp.start()
                for cp in copies:
                    cp.wait()

    return scatter(rows, idx3d)


def _padfill_kernel(pad_from_ref, pad_n_ref, xs_in_hbm, xs_hbm, zero_scr, pad_sem):
    del xs_in_hbm
    zero_scr[...] = jnp.zeros(zero_scr.shape, zero_scr.dtype)

    def pad_copies(e, act):
        n = pad_n_ref[e]
        base = pad_from_ref[e]

        def single_rows(start, count):
            for j in range(SUBLANES - 1):
                @pl.when(j < count)
                def _():
                    act(pltpu.make_async_copy(zero_scr.at[pl.ds(0, 1), :],
                                              xs_hbm.at[pl.ds(start + j, 1), :], pad_sem))

        head = jnp.minimum(n, (SUBLANES - (base & (SUBLANES - 1))) & (SUBLANES - 1))
        single_rows(base, head)
        rest = n - head
        mid = base + head
        for rows in _PAD_CHUNKS:
            if rows >= SUBLANES:
                @pl.when((rest & rows) != 0)
                def _():
                    start = pl.multiple_of(mid + (rest & ~(2 * rows - 1)), SUBLANES)
                    act(pltpu.make_async_copy(zero_scr.at[pl.ds(0, rows), :],
                                              xs_hbm.at[pl.ds(start, rows), :], pad_sem))
        single_rows(mid + (rest & ~(SUBLANES - 1)), rest & (SUBLANES - 1))

    def issue_pad(e, carry):
        pad_copies(e, lambda cp: cp.start())
        return carry

    def drain_pad(e, carry):
        pad_copies(e, lambda cp: cp.wait())
        return carry

    lax.fori_loop(0, N_EXPERTS, issue_pad, 0)
    lax.fori_loop(0, N_EXPERTS, drain_pad, 0)


def _padfill(pad_from, pad_n, xs):
    return pl.pallas_call(
        _padfill_kernel,
        grid_spec=pltpu.PrefetchScalarGridSpec(
            num_scalar_prefetch=2,
            grid=(1,),
            in_specs=[pl.BlockSpec(memory_space=pl.ANY)],
            out_specs=pl.BlockSpec(memory_space=pl.ANY),
            scratch_shapes=[pltpu.VMEM((BM_EXPERT // 2, _ROW_WORDS), U32), pltpu.SemaphoreType.DMA]),
        out_shape=jax.ShapeDtypeStruct(xs.shape, xs.dtype),
        input_output_aliases={2: 0},
        compiler_params=_params(("arbitrary",)),
        name="padfill",
    )(pad_from, pad_n, xs)


_XS_SLOTS = 4
_YS_SLOTS = 2
_W_SLOTS = 4
_W_AHEAD = 2


def _expert_kernel(first_ref, ord_ref, uexp_ref, meta_ref, xs_hbm, wg_hbm, wu_hbm, wd_hbm, ys_hbm,
                   xs_buf, ys_buf, wg_buf, wu_buf, wd_buf, act_scr, xs_sem, ys_sem, w_sem):
    i = pl.program_id(0)
    nused = meta_ref[0]
    nexp = meta_ref[1]
    bm = xs_buf.shape[1]

    def xs_copy(b, slot):
        return pltpu.make_async_copy(xs_hbm.at[pl.ds(b * bm, bm), :], xs_buf.at[slot], xs_sem.at[slot])

    def ys_copy(b, slot):
        return pltpu.make_async_copy(ys_buf.at[slot], ys_hbm.at[pl.ds(b * bm, bm), :], ys_sem.at[slot])

    def w_copies(j, slot):
        e = uexp_ref[j]
        return (pltpu.make_async_copy(wg_hbm.at[e], wg_buf.at[slot], w_sem.at[slot, 0]),
                pltpu.make_async_copy(wu_hbm.at[e], wu_buf.at[slot], w_sem.at[slot, 1]),
                pltpu.make_async_copy(wd_hbm.at[e], wd_buf.at[slot], w_sem.at[slot, 2]))

    @pl.when(i == 0)
    def _():
        for s in range(_XS_SLOTS - 1):
            @pl.when(s < nused)
            def _():
                xs_copy(s, s).start()
        for s in range(_W_AHEAD):
            @pl.when(s < nexp)
            def _():
                for cp in w_copies(s, s):
                    cp.start()

    def fetch(b):
        ahead = b + _XS_SLOTS - 1

        @pl.when(ahead < nused)
        def _():
            xs_copy(ahead, ahead % _XS_SLOTS).start()

        j = ord_ref[b]

        @pl.when(first_ref[b] == 1)
        def _():
            for cp in w_copies(j, j % _W_SLOTS):
                cp.wait()
            nxt = j + _W_AHEAD

            @pl.when(nxt < nexp)
            def _():
                for cp in w_copies(nxt, nxt % _W_SLOTS):
                    cp.start()

        xs_copy(b, b % _XS_SLOTS).wait()

    def gate_up(b):
        ws = ord_ref[b] % _W_SLOTS
        lo, hi = _unpack_row_words(xs_buf[b % _XS_SLOTS])
        g = (jnp.dot(lo, wg_buf[ws, 0:_ROW_WORDS, :], preferred_element_type=F32)
             + jnp.dot(hi, wg_buf[ws, _ROW_WORDS:, :], preferred_element_type=F32))
        u = (jnp.dot(lo, wu_buf[ws, 0:_ROW_WORDS, :], preferred_element_type=F32)
             + jnp.dot(hi, wu_buf[ws, _ROW_WORDS:, :], preferred_element_type=F32))
        return g * jax.nn.sigmoid(g) * u

    def down(b, act):
        y = jnp.dot(act, wd_buf[ord_ref[b] % _W_SLOTS], preferred_element_type=F32)
        oslot = b % _YS_SLOTS
        ys_buf[oslot] = _pack_row_words(y[:, 0:_ROW_WORDS], y[:, _ROW_WORDS:])
        ys_copy(b, oslot).start()

    @pl.when(jnp.logical_and(i >= _YS_SLOTS + 1, i <= nused))
    def _():
        ys_copy(i - 1 - _YS_SLOTS, (i - 1) % _YS_SLOTS).wait()

    @pl.when(i == 0)
    def _():
        fetch(i)
        act_scr[...] = gate_up(i)

    @pl.when(jnp.logical_and(i >= 1, i < nused))
    def _():
        fetch(i)
        prev = act_scr[...]
        act_scr[...] = gate_up(i)
        down(i - 1, prev)

    @pl.when(i == nused)
    def _():
        down(i - 1, act_scr[...])
        ys_copy(i - 1, (i - 1) % _YS_SLOTS).wait()

        @pl.when(i >= 2)
        def _():
            ys_copy(i - 2, (i - 2) % _YS_SLOTS).wait()


def _experts(first, ordinal, uexp, meta, xs, w_gate, w_up, w_down):
    p = xs.shape[0]
    d = w_gate.shape[1]
    nb = p // BM_EXPERT
    anyspec = pl.BlockSpec(memory_space=pl.ANY)
    return pl.pallas_call(
        _expert_kernel,
        grid_spec=pltpu.PrefetchScalarGridSpec(
            num_scalar_prefetch=4,
            grid=(nb + 1,),
            in_specs=[anyspec, anyspec, anyspec, anyspec],
            out_specs=anyspec,
            scratch_shapes=[pltpu.VMEM((_XS_SLOTS, BM_EXPERT, _ROW_WORDS), U32),
                            pltpu.VMEM((_YS_SLOTS, BM_EXPERT, _ROW_WORDS), U32),
                            pltpu.VMEM((_W_SLOTS, d, EXPERT_DIM), F32),
                            pltpu.VMEM((_W_SLOTS, d, EXPERT_DIM), F32),
                            pltpu.VMEM((_W_SLOTS, EXPERT_DIM, d), F32),
                            pltpu.VMEM((BM_EXPERT, EXPERT_DIM), F32),
                            pltpu.SemaphoreType.DMA((_XS_SLOTS,)), pltpu.SemaphoreType.DMA((_YS_SLOTS,)),
                            pltpu.SemaphoreType.DMA((_W_SLOTS, 3))]),
        out_shape=jax.ShapeDtypeStruct((p, _ROW_WORDS), U32),
        compiler_params=_params(("arbitrary",)),
        name="experts",
    )(first, ordinal, uexp, meta, xs, w_gate, w_up, w_down)


SC_GATHER_ROWS = 64


def _gather_rows_sc(table, idx):
    n = idx.shape[0]
    width = table.shape[1]
    rows = SC_GATHER_ROWS
    per_worker, active = _sc_worker_split(n // rows)
    assert per_worker % 2 == 0 or per_worker == 1
    mesh = plsc.VectorSubcoreMesh(core_axis_name="c", subcore_axis_name="s")

    @functools.partial(
        pl.kernel, mesh=mesh, out_type=jax.ShapeDtypeStruct((n, width), table.dtype),
        scratch_types=[pltpu.VMEM((per_worker, rows), I32), pltpu.VMEM((2, rows, width), table.dtype),
                       pltpu.SemaphoreType.DMA((2,))])
    def gather(table_hbm, idx_hbm, out_hbm, idx_v, rows_v, sem):
        wid = lax.axis_index("s") * SC_CORES + lax.axis_index("c")

        def fetch(c, b):
            return pltpu.make_async_copy(table_hbm.at[idx_v.at[c]], rows_v.at[b], sem.at[b])

        @pl.when(wid < active)
        def _():
            first = wid * per_worker
            pltpu.sync_copy(idx_hbm.at[pl.ds(first, per_worker)], idx_v)
            fetch(0, 0).start()

            @pl.loop(0, per_worker, step=2)
            def _(c):
                for b in range(min(2, per_worker)):
                    cur = c + b

                    @pl.when(cur + 1 < per_worker)
                    def _():
                        fetch(cur + 1, 1 - b).start()

                    fetch(cur, b).wait()
                    pltpu.sync_copy(rows_v.at[b], out_hbm.at[pl.ds((first + cur) * rows, rows)])

    return gather(table, idx.reshape(n // rows, rows))


def _combine_kernel(wts_ref, yg_ref, h2_ref, x1_ref, mod_ref, wsg_ref, wsu_ref, wsd_ref, gpost_ref, *rest):
    o_ref = rest[-1]
    h2 = h2_ref[...]
    g = jnp.dot(h2, wsg_ref[...], preferred_element_type=F32)
    u = jnp.dot(h2, wsu_ref[...], preferred_element_type=F32)
    f = jnp.dot((g * jax.nn.sigmoid(g) * u).astype(BF16), wsd_ref[...], preferred_element_type=F32)

    wts = wts_ref[...]
    los = [f[:, sl * LANES:(sl + 1) * LANES] for sl in range(_ROW_SLABS)]
    his = [f[:, _ROW_WORDS + sl * LANES:_ROW_WORDS + (sl + 1) * LANES] for sl in range(_ROW_SLABS)]
    for k in range(TOP_K):
        wk = wts[:, k:k + 1]
        for sl in range(_ROW_SLABS):
            lo, hi = _unpack_row_words(yg_ref[k, :, sl * LANES:(sl + 1) * LANES])
            los[sl] = los[sl] + wk * lo
            his[sl] = his[sl] + wk * hi
    f = jnp.concatenate(los + his, axis=1)
    g2 = mod_ref[0, 5:6, :]
    o_ref[...] = x1_ref[...] + g2 * _rms(f, gpost_ref[...])


def _combine(wts_t, yg, h2, x1, mod3, wsg, wsu, wsd, gpost, seq, first_tile, partial_out):
    t, d = x1.shape
    tm = min(TM_COMBINE, seq)
    tpb = seq // tm
    full = lambda a: pl.BlockSpec(a.shape, lambda i: (0,) * a.ndim)
    row = lambda w: pl.BlockSpec((tm, w), lambda i: (i + first_tile, 0))
    args = [wts_t, yg, h2, x1, mod3, wsg, wsu, wsd, gpost]
    in_specs = [row(TOP_K), pl.BlockSpec((TOP_K, tm, _ROW_WORDS), lambda i: (0, i, 0)), row(d), row(d),
                pl.BlockSpec((1, 6, d), lambda i: ((i + first_tile) // tpb, 0, 0)),
                full(wsg), full(wsu), full(wsd), full(gpost)]
    aliases = {}
    if partial_out is not None:
        aliases = {len(args): 0}
        args.append(partial_out)
        in_specs.append(pl.BlockSpec(memory_space=pl.ANY))
    return pl.pallas_call(
        _combine_kernel,
        grid=(yg.shape[1] // tm,),
        in_specs=in_specs,
        out_specs=row(d),
        out_shape=jax.ShapeDtypeStruct((t, d), F32),
        input_output_aliases=aliases,
        compiler_params=_params(("arbitrary",)),
        name="combine",
    )(*args)


def _pack_weights(w_in, w_uq, w_ukv):
    d = w_in.shape[0]
    half = QK_ROPE_DIM // 2
    z = lambda n, c: jnp.zeros((n, c), F32)
    o = Q_LORA_RANK + KV_LORA_RANK
    kr = w_in[:, o:o + QK_ROPE_DIM]
    kr_grp = jnp.concatenate([z(d, QK_NOPE_DIM), kr, z(d, HEAD_PAD - QK_NOPE_DIM - QK_ROPE_DIM)], axis=1)
    kr_rot = jnp.concatenate([z(d, QK_NOPE_DIM), -kr[:, half:], kr[:, :half],
                              z(d, HEAD_PAD - QK_NOPE_DIM - QK_ROPE_DIM)], axis=1)
    win_p = jnp.concatenate([w_in[:, :o], kr_grp, kr_rot, w_in[:, o + QK_ROPE_DIM:]], axis=1)

    scale = float(QK_NOPE_DIM + QK_ROPE_DIM) ** -0.5 * float(np.log2(np.e))
    r = Q_LORA_RANK
    qd = QK_NOPE_DIM + QK_ROPE_DIM
    q_grp, q_rot = [], []
    for h in range(MLA_HEADS):
        nope = w_uq[:, h * qd:h * qd + QK_NOPE_DIM]
        rope = w_uq[:, h * qd + QK_NOPE_DIM:(h + 1) * qd]
        pad = z(r, HEAD_PAD - qd)
        q_grp.append(jnp.concatenate([nope, rope, pad], axis=1))
        q_rot.append(jnp.concatenate([z(r, QK_NOPE_DIM), -rope[:, half:], rope[:, :half], pad], axis=1))
    wuq_p = jnp.concatenate(q_grp + q_rot, axis=1) * scale

    c = KV_LORA_RANK
    kd = QK_NOPE_DIM + V_HEAD_DIM
    k_grp, v_grp = [], []
    for h in range(MLA_HEADS):
        k_grp.append(jnp.concatenate([w_ukv[:, h * kd:h * kd + QK_NOPE_DIM], z(c, HEAD_PAD - QK_NOPE_DIM)], axis=1))
        v_grp.append(jnp.concatenate([w_ukv[:, h * kd + QK_NOPE_DIM:(h + 1) * kd], z(c, HEAD_PAD - V_HEAD_DIM)], axis=1))
    wukv_p = jnp.concatenate(k_grp + v_grp, axis=1)
    return win_p.astype(BF16), wuq_p.astype(BF16), wukv_p.astype(BF16)


def _rope_inputs(positions):
    inv = 1.0 / (ROPE_THETA ** (jnp.arange(0, QK_ROPE_DIM, 2, dtype=F32) / QK_ROPE_DIM))
    freq = jnp.tile(inv, LANES // inv.shape[0]).reshape(1, LANES)
    pos = jnp.repeat(positions.astype(F32).reshape(-1, _ROPE_PACK), QK_ROPE_DIM, axis=1)
    return pos, freq


def _layer(x2, c, pos, freq, batch, seq, w_ada, b_ada, g_pre_mix, w_in, g_q_lat, w_uq, g_kv_lat, w_ukv,
           w_conv, g_attn_out, g_conv_out, w_out, g_post_mix, g_pre_ffn, w_router, b_router,
           w_gate, w_up, w_down, w_sh_gate, w_sh_up, w_sh_down, g_post_ffn):
    t, d = x2.shape
    r1 = lambda a: a.reshape(1, -1)

    c_pad = jnp.zeros((SUBLANES, d), F32).at[:batch].set(c)
    mod = _ada(c_pad, w_ada, r1(b_ada))[:batch]
    mod3 = mod.reshape(batch, 6, d)

    win_p, wuq_p, wukv_p = _pack_weights(w_in, w_uq, w_ukv)
    vone = jnp.zeros((1, HEAD_PAD), F32).at[0, V_HEAD_DIM].set(1.0)
    q, k, v, yc = _mix_in(x2, mod3, r1(g_pre_mix), win_p, r1(g_q_lat), wuq_p, r1(g_kv_lat), wukv_p,
                          vone, w_conv, r1(g_conv_out), pos, freq, seq)
    attn = _attention(q, k, v, batch, seq)
    x1, h2, h2p, idx, wts, rank, cnt = _mix_out(
        attn, yc, x2, mod3, r1(g_attn_out), w_out.astype(BF16), r1(g_post_mix), r1(g_pre_ffn),
        w_router.T, b_router.reshape(-1, 1), seq)

    counts = cnt[:, 0].astype(I32)
    padded = ((counts + BM_EXPERT - 1) // BM_EXPERT) * BM_EXPERT
    pad_end = jnp.cumsum(padded)
    pad_start = pad_end - padded
    m = t * TOP_K
    nb = (m + N_EXPERTS * (BM_EXPERT - 1)) // BM_EXPERT
    nused = pad_end[-1] // BM_EXPERT
    bidx = jnp.arange(nb, dtype=I32)
    blk_exp = jnp.sum((pad_end[None, :] <= (bidx * BM_EXPERT)[:, None]).astype(I32), axis=1)
    first = ((bidx < nused) & ((bidx == 0) | (blk_exp != jnp.roll(blk_exp, 1)))).astype(I32)
    ordinal = jnp.maximum(jnp.cumsum(first) - 1, 0).astype(I32)
    seen = jnp.cumsum((counts > 0).astype(I32))
    uexp = jnp.minimum(jnp.sum((seen[None, :] <= jnp.arange(N_EXPERTS, dtype=I32)[:, None]).astype(I32), axis=1),
                       N_EXPERTS - 1).astype(I32)
    meta = jnp.stack([nused, seen[-1]]).astype(I32)

    dest = _dest(idx, rank, pad_start.astype(F32).reshape(-1, 1))
    xs = _scatter_rows_sc(h2p, dest.reshape(TOP_K, -1, SC_ROWS).transpose(1, 0, 2), nb * BM_EXPERT)
    xs = _padfill((pad_start + counts).astype(I32), (padded - counts).astype(I32), xs)
    ys = _experts(first, ordinal, uexp, meta, xs, w_gate, w_up, w_down)
    tm_c = min(TM_COMBINE, seq)
    half_rows = (t // 2) * TOP_K
    even_split = half_rows % (2 * SC_GATHER_ROWS * SC_CORES * SC_SUBCORES) == 0
    parts = 2 if t % (2 * tm_c) == 0 and even_split else 1
    tp = t // parts
    wsg, wsu, wsd = w_sh_gate.astype(BF16), w_sh_up.astype(BF16), w_sh_down.astype(BF16)
    out = None
    for part in range(parts):
        part_dest = dest[:, part * tp:(part + 1) * tp].reshape(-1)
        yg = _gather_rows_sc(ys, part_dest).reshape(TOP_K, tp, _ROW_WORDS)
        out = _combine(wts.T, yg, h2, x1, mod3, wsg, wsu, wsd, r1(g_post_ffn), seq, part * tp // tm_c, out)
    return out


def kernel(x, c, positions, w_ada, b_ada, g_pre_mix, w_in, g_q_lat, w_uq, g_kv_lat, w_ukv, w_conv, g_attn_out, g_conv_out, w_out, g_post_mix, g_pre_ffn, w_router, b_router, w_gate, w_up, w_down, w_sh_gate, w_sh_up, w_sh_down, g_post_ffn):
    batch, seq, d = x.shape
    pos, freq = _rope_inputs(positions)
    x2 = x.reshape(batch * seq, d)
    for l in range(w_ada.shape[0]):
        x2 = _layer(x2, c, pos, freq, batch, seq, w_ada[l], b_ada[l], g_pre_mix[l], w_in[l], g_q_lat[l],
                    w_uq[l], g_kv_lat[l], w_ukv[l], w_conv[l], g_attn_out[l], g_conv_out[l], w_out[l],
                    g_post_mix[l], g_pre_ffn[l], w_router[l], b_router[l], w_gate[l], w_up[l], w_down[l],
                    w_sh_gate[l], w_sh_up[l], w_sh_down[l], g_post_ffn[l])
    return x2.reshape(batch, seq, d)
```

```python
import functools

import jax
import jax.numpy as jnp
import numpy as np
from jax import lax
from jax.experimental import pallas as pl
from jax.experimental.pallas import tpu as pltpu
from jax.experimental.pallas import tpu_sc as plsc

F32 = jnp.float32
BF16 = jnp.bfloat16
I32 = jnp.int32
U32 = jnp.uint32

CHUNK = 64
MLA_HEADS = 8
QK_NOPE_DIM = 64
QK_ROPE_DIM = 32
V_HEAD_DIM = 64
Q_LORA_RANK = 384
KV_LORA_RANK = 256
ROPE_THETA = 10000.0
CONV_WIDTH = 3
N_EXPERTS = 256
TOP_K = 8
N_EXPERT_GROUPS = 8
TOPK_GROUPS = 4
EXPERT_DIM = 256
ROUTED_SCALE = 2.5
EPS = 1e-6

LANES = 128
SUBLANES = 8
HEAD_PAD = LANES
VMEM_LIMIT_BYTES = 56 * 1024 * 1024

TM_IN = 512
TQ_ATTN = 512
TM_OUT = 512
TM_DEST = 2048
BM_EXPERT = 256
TM_COMBINE = 512

NEG_INF = float("-inf")


def _rms(x, g):
    return x * lax.rsqrt(jnp.mean(x * x, axis=-1, keepdims=True) + EPS) * g


_HI_MASK = np.uint32(0xFFFF0000)
_ROW_WORDS = 512
_ROW_SLABS = _ROW_WORDS // LANES


def _pack_row_words(lo, hi):
    lo_w = lax.bitcast_convert_type(lo.astype(BF16).astype(F32), U32) >> 16
    hi_w = lax.bitcast_convert_type(hi.astype(BF16).astype(F32), U32) & _HI_MASK
    return lo_w | hi_w


def _unpack_row_words(w):
    return (lax.bitcast_convert_type(w << 16, F32), lax.bitcast_convert_type(w & _HI_MASK, F32))


def _params(sem):
    return pltpu.CompilerParams(dimension_semantics=sem, vmem_limit_bytes=VMEM_LIMIT_BYTES)


def _ada_kernel(c_ref, w_ref, b_ref, o_ref):
    c = c_ref[...]
    s = c * jax.nn.sigmoid(c)
    o_ref[...] = jnp.dot(s, w_ref[...], preferred_element_type=F32,
                         precision=lax.Precision.HIGHEST) + b_ref[...]


def _ada(c_pad, w, b):
    rows, d = c_pad.shape
    n = w.shape[1]
    tn = 3072
    return pl.pallas_call(
        _ada_kernel,
        grid=(n // tn,),
        in_specs=[pl.BlockSpec((rows, d), lambda j: (0, 0)),
                  pl.BlockSpec((d, tn), lambda j: (0, j)),
                  pl.BlockSpec((1, tn), lambda j: (0, j))],
        out_specs=pl.BlockSpec((rows, tn), lambda j: (0, j)),
        out_shape=jax.ShapeDtypeStruct((rows, n), F32),
        compiler_params=_params(("arbitrary",)),
        name="ada",
    )(c_pad, w, b)


_CQ0, _CQ1 = 0, Q_LORA_RANK
_CKV0, _CKV1 = _CQ1, _CQ1 + KV_LORA_RANK
_KR0, _KR1 = _CKV1, _CKV1 + 2 * HEAD_PAD
_CONV_DIM = 512
_GB0 = _KR1
_GC0 = _GB0 + _CONV_DIM
_XV0 = _GC0 + _CONV_DIM
_WIN_COLS = _XV0 + _CONV_DIM
_QW = MLA_HEADS * HEAD_PAD
_ROPE_PACK = LANES // QK_ROPE_DIM


def _mix_in_kernel(tiles_per_batch, x_ref, mod_ref, gpre_ref, win_ref, gq_ref, wuq_ref, gkv_ref,
                   wukv_ref, vone_ref, wconv_ref, gconv_ref, pos_ref, freq_ref,
                   q_ref, k_ref, v_ref, yc_ref, h_scr, u_scr, cos_scr, sin_scr):
    i = pl.program_id(0)
    tm = x_ref.shape[0]
    sh1 = mod_ref[0, 0:1, :]
    sc1 = mod_ref[0, 1:2, :]
    h = _rms(x_ref[...], gpre_ref[...]) * (1.0 + sc1) + sh1
    h_scr[...] = h.astype(BF16)
    ang4 = pos_ref[...] * freq_ref[...]
    cos4 = jnp.cos(ang4)
    sin4 = jnp.sin(ang4)
    lane = lax.broadcasted_iota(I32, ang4.shape, 1)
    on_rope = jnp.logical_and(lane >= QK_NOPE_DIM, lane < QK_NOPE_DIM + QK_ROPE_DIM)
    for g in range(_ROPE_PACK):
        shift = (QK_NOPE_DIM - QK_ROPE_DIM * g) % LANES
        cg = cos4 if shift == 0 else pltpu.roll(cos4, shift, axis=1)
        sg = sin4 if shift == 0 else pltpu.roll(sin4, shift, axis=1)
        cos_scr[pl.ds(g, tm // _ROPE_PACK, stride=_ROPE_PACK), :] = jnp.where(on_rope, cg, 1.0)
        sin_scr[pl.ds(g, tm // _ROPE_PACK, stride=_ROPE_PACK), :] = jnp.where(on_rope, sg, 0.0)
    cos = cos_scr[...]
    sin = sin_scr[...]

    cq = jnp.dot(h_scr[...], win_ref[:, _CQ0:_CQ1], preferred_element_type=F32)
    cqn = _rms(cq, gq_ref[...]).astype(BF16)
    qq = jnp.dot(cqn, wuq_ref[...], preferred_element_type=F32)
    for hd in range(MLA_HEADS):
        lo = hd * HEAD_PAD
        qh = qq[:, lo:lo + HEAD_PAD] * cos + qq[:, _QW + lo:_QW + lo + HEAD_PAD] * sin
        q_ref[:, lo:lo + HEAD_PAD] = qh.astype(BF16)

    ckv = jnp.dot(h_scr[...], win_ref[:, _CKV0:_CKV1], preferred_element_type=F32)
    ckvn = _rms(ckv, gkv_ref[...]).astype(BF16)
    kv = jnp.dot(ckvn, wukv_ref[...], preferred_element_type=F32)
    krr = jnp.dot(h_scr[...], win_ref[:, _KR0:_KR1], preferred_element_type=F32)
    kr = krr[:, 0:HEAD_PAD] * cos + krr[:, HEAD_PAD:2 * HEAD_PAD] * sin
    vone = vone_ref[...]
    for hd in range(MLA_HEADS):
        lo = hd * HEAD_PAD
        k_ref[:, lo:lo + HEAD_PAD] = (kv[:, lo:lo + HEAD_PAD] + kr).astype(BF16)
        v_ref[:, lo:lo + HEAD_PAD] = (kv[:, _QW + lo:_QW + lo + HEAD_PAD] + vone).astype(BF16)

    gb = jnp.dot(h_scr[...], win_ref[:, _GB0:_GC0], preferred_element_type=F32)
    gc = jnp.dot(h_scr[...], win_ref[:, _GC0:_XV0], preferred_element_type=F32)
    xv = jnp.dot(h_scr[...], win_ref[:, _XV0:_WIN_COLS], preferred_element_type=F32)
    u = gc * xv
    prev = u_scr[tm:tm + SUBLANES, :]
    first = (i % tiles_per_batch) == 0
    u_scr[0:SUBLANES, :] = jnp.where(first, jnp.zeros_like(prev), prev)
    u_scr[SUBLANES:tm + SUBLANES, :] = u
    um1 = u_scr[SUBLANES - 1:tm + SUBLANES - 1, :]
    um2 = u_scr[SUBLANES - 2:tm + SUBLANES - 2, :]
    assert wconv_ref.shape[0] == CONV_WIDTH
    conv = wconv_ref[0:1, :] * um2 + wconv_ref[1:2, :] * um1 + wconv_ref[2:3, :] * u
    yc_ref[...] = _rms(gb * conv, gconv_ref[...]).astype(BF16)


def _mix_in(x2, mod3, gpre, win_p, gq, wuq_p, gkv, wukv_p, vone, wconv, gconv, pos, freq, seq):
    t, d = x2.shape
    tm = min(TM_IN, seq)
    tpb = seq // tm
    full = lambda a: pl.BlockSpec(a.shape, lambda i: (0,) * a.ndim)
    row = lambda w: pl.BlockSpec((tm, w), lambda i: (i, 0))
    return pl.pallas_call(
        functools.partial(_mix_in_kernel, tpb),
        grid=(t // tm,),
        in_specs=[row(d),
                  pl.BlockSpec((1, 6, d), lambda i: (i // tpb, 0, 0)),
                  full(gpre), full(win_p), full(gq), full(wuq_p), full(gkv), full(wukv_p),
                  full(vone), full(wconv), full(gconv),
                  pl.BlockSpec((tm // _ROPE_PACK, LANES), lambda i: (i, 0)), full(freq)],
        out_specs=[row(_QW), row(_QW), row(_QW), row(_CONV_DIM)],
        out_shape=[jax.ShapeDtypeStruct((t, _QW), BF16), jax.ShapeDtypeStruct((t, _QW), BF16),
                   jax.ShapeDtypeStruct((t, _QW), BF16), jax.ShapeDtypeStruct((t, _CONV_DIM), BF16)],
        scratch_shapes=[pltpu.VMEM((tm, d), BF16), pltpu.VMEM((tm + SUBLANES, _CONV_DIM), F32),
                        pltpu.VMEM((tm, HEAD_PAD), F32), pltpu.VMEM((tm, HEAD_PAD), F32)],
        compiler_params=_params(("arbitrary",)),
        name="mix_in",
    )(x2, mod3, gpre, win_p, gq, wuq_p, gkv, wukv_p, vone, wconv, gconv, pos, freq)


_HEADS_PER_STEP = 2


def _attn_kernel(tq, q_ref, k_ref, v_ref, o_ref, s_scr, mrun_scr, mb_scr, acc_scr):
    tk = tq
    nq = q_ref.shape[0] // tq
    lane_groups = tk // LANES
    heads = range(_HEADS_PER_STEP)
    lanes = [slice(hh * HEAD_PAD, (hh + 1) * HEAD_PAD) for hh in heads]

    def tile_max(s):
        m = s[:, 0:LANES]
        for g in range(1, lane_groups):
            m = jnp.maximum(m, s[:, g * LANES:(g + 1) * LANES])
        return m

    def scores(hh, qi, kv):
        off = pl.multiple_of(kv * tk, tk)
        return lax.dot_general(q_ref[qi * tq:(qi + 1) * tq, lanes[hh]], k_ref[pl.ds(off, tk), lanes[hh]],
                               (((1,), (1,)), ((), ())), preferred_element_type=F32)

    rc = lax.broadcasted_iota(I32, (tq, tk), 0) // CHUNK
    cc = lax.broadcasted_iota(I32, (tq, tk), 1) // CHUNK

    def diagonal(qi):
        for hh in heads:
            s = jnp.where(cc <= rc, scores(hh, qi, qi), NEG_INF)
            s_scr[hh, qi] = s
            m_row = jnp.max(jnp.maximum(mrun_scr[hh], tile_max(s)), axis=1, keepdims=True)
            mb_scr[hh] = jnp.broadcast_to(m_row, (tq, LANES))

    mrun_scr[...] = jnp.full(mrun_scr.shape, NEG_INF, F32)
    diagonal(0)
    for qi in range(nq):
        has_next = qi + 1 < nq
        acc_scr[...] = jnp.zeros(acc_scr.shape, F32)
        if has_next:
            mrun_scr[...] = jnp.full(mrun_scr.shape, NEG_INF, F32)

        def body(kv, carry, qi=qi, has_next=has_next):
            off = pl.multiple_of(kv * tk, tk)
            for hh in heads:
                mb = mb_scr[hh]
                p = jnp.concatenate(
                    [jnp.exp2(s_scr[hh, kv, :, g * LANES:(g + 1) * LANES] - mb) for g in range(lane_groups)],
                    axis=1).astype(BF16)
                acc_scr[hh] += jnp.dot(p, v_ref[pl.ds(off, tk), lanes[hh]], preferred_element_type=F32)
                if has_next:
                    s = scores(hh, qi + 1, kv)
                    s_scr[hh, kv] = s
                    mrun_scr[hh] = jnp.maximum(mrun_scr[hh], tile_max(s))
            return carry

        lax.fori_loop(0, qi + 1, body, 0, unroll=4)
        for hh in heads:
            acc = acc_scr[hh]
            o = acc[:, 0:V_HEAD_DIM] / acc[:, V_HEAD_DIM:V_HEAD_DIM + 1]
            o_ref[qi * tq:(qi + 1) * tq, hh * V_HEAD_DIM:(hh + 1) * V_HEAD_DIM] = o.astype(BF16)
        if has_next:
            diagonal(qi + 1)


def _attention(q, k, v, batch, seq):
    t = q.shape[0]
    tq = min(TQ_ATTN, seq)
    nq = seq // tq
    hw = _HEADS_PER_STEP * HEAD_PAD
    ow = _HEADS_PER_STEP * V_HEAD_DIM
    blk = lambda w: pl.BlockSpec((seq, w), lambda b, j: (b, j))
    return pl.pallas_call(
        functools.partial(_attn_kernel, tq),
        grid=(batch, MLA_HEADS // _HEADS_PER_STEP),
        in_specs=[blk(hw), blk(hw), blk(hw)],
        out_specs=blk(ow),
        out_shape=jax.ShapeDtypeStruct((t, MLA_HEADS * V_HEAD_DIM), BF16),
        scratch_shapes=[pltpu.VMEM((_HEADS_PER_STEP, nq, tq, tq), F32),
                        pltpu.VMEM((_HEADS_PER_STEP, tq, LANES), F32),
                        pltpu.VMEM((_HEADS_PER_STEP, tq, LANES), F32),
                        pltpu.VMEM((_HEADS_PER_STEP, tq, HEAD_PAD), F32)],
        compiler_params=_params(("arbitrary", "arbitrary")),
        name="attn",
    )(q, k, v)


_GROUP_SIZE = N_EXPERTS // N_EXPERT_GROUPS
_BIG = 1.0e9


def _mix_out_kernel(attn_ref, yc_ref, x_ref, mod_ref, gattn_ref, wout_ref, gpost_ref, gpre2_ref,
                    wrt_ref, br_ref, x1_ref, h2_ref, h2p_ref, idx_ref, wts_ref, rank_ref, cnt_ref,
                    carry_scr, scores_scr, sel_scr):
    i = pl.program_id(0)
    n_tiles = pl.num_programs(0) - 1

    @pl.when(i == 0)
    def _():
        carry_scr[...] = jnp.zeros(carry_scr.shape, F32)
        scores, sel = _project_tile(attn_ref, yc_ref, x_ref, mod_ref, gattn_ref, wout_ref, gpost_ref,
                                    gpre2_ref, wrt_ref, br_ref, x1_ref, h2_ref, h2p_ref)
        scores_scr[0] = scores
        sel_scr[0] = sel

    @pl.when(jnp.logical_and(i >= 1, i < n_tiles))
    def _():
        prev_scores = scores_scr[(i - 1) % 2]
        prev_sel = sel_scr[(i - 1) % 2]
        scores, sel = _project_tile(attn_ref, yc_ref, x_ref, mod_ref, gattn_ref, wout_ref, gpost_ref,
                                    gpre2_ref, wrt_ref, br_ref, x1_ref, h2_ref, h2p_ref)
        scores_scr[i % 2] = scores
        sel_scr[i % 2] = sel
        _route_tile(prev_scores, prev_sel, idx_ref, wts_ref, rank_ref, cnt_ref, carry_scr)

    @pl.when(i == n_tiles)
    def _():
        _route_tile(scores_scr[(i - 1) % 2], sel_scr[(i - 1) % 2], idx_ref, wts_ref, rank_ref, cnt_ref,
                    carry_scr)


def _project_tile(attn_ref, yc_ref, x_ref, mod_ref, gattn_ref, wout_ref, gpost_ref, gpre2_ref,
                  wrt_ref, br_ref, x1_ref, h2_ref, h2p_ref):
    half = attn_ref.shape[1]
    an = _rms(attn_ref[...].astype(F32), gattn_ref[...]).astype(BF16)
    mix = (jnp.dot(an, wout_ref[0:half, :], preferred_element_type=F32)
           + jnp.dot(yc_ref[...], wout_ref[half:, :], preferred_element_type=F32))
    g1 = mod_ref[0, 2:3, :]
    sh2 = mod_ref[0, 3:4, :]
    sc2 = mod_ref[0, 4:5, :]
    x1 = x_ref[...] + g1 * _rms(mix, gpost_ref[...])
    x1_ref[...] = x1
    h2 = _rms(x1, gpre2_ref[...]) * (1.0 + sc2) + sh2
    h2_ref[...] = h2.astype(BF16)
    h2p_ref[...] = _pack_row_words(h2[:, 0:_ROW_WORDS], h2[:, _ROW_WORDS:])

    logits = lax.dot_general(wrt_ref[...], h2, (((1,), (1,)), ((), ())),
                             preferred_element_type=F32, precision=lax.Precision.HIGHEST)
    scores = jax.nn.sigmoid(logits)
    return scores, scores + br_ref[...]


def _route_tile(scores, sel, idx_ref, wts_ref, rank_ref, cnt_ref, carry_scr):
    tm = scores.shape[1]
    row = lax.broadcasted_iota(I32, (N_EXPERTS, tm), 0).astype(F32)

    gscore = []
    rw = lax.broadcasted_iota(I32, (_GROUP_SIZE, tm), 0).astype(F32)
    for g in range(N_EXPERT_GROUPS):
        blk = sel[g * _GROUP_SIZE:(g + 1) * _GROUP_SIZE, :]
        m1 = jnp.max(blk, axis=0, keepdims=True)
        i1 = jnp.min(jnp.where(blk == m1, rw, _BIG), axis=0, keepdims=True)
        m2 = jnp.max(jnp.where(rw == i1, NEG_INF, blk), axis=0, keepdims=True)
        gscore.append(m1 + m2)

    gkeep = [jnp.zeros((1, tm), F32) for _ in range(N_EXPERT_GROUPS)]
    for _ in range(TOPK_GROUPS):
        mg = functools.reduce(jnp.maximum, gscore)
        ig = functools.reduce(jnp.minimum, [jnp.where(gscore[g] == mg, float(g), _BIG)
                                            for g in range(N_EXPERT_GROUPS)])
        for g in range(N_EXPERT_GROUPS):
            hit = ig == float(g)
            gkeep[g] = jnp.where(hit, 1.0, gkeep[g])
            gscore[g] = jnp.where(hit, NEG_INF, gscore[g])
    n_slabs = N_EXPERTS // SUBLANES
    slabs_per_group = _GROUP_SIZE // SUBLANES
    sub = lax.broadcasted_iota(I32, (SUBLANES, tm), 0).astype(F32)
    first_rows = [jnp.where(gkeep[j // slabs_per_group] > 0.0, sel[j * SUBLANES:(j + 1) * SUBLANES, :], NEG_INF)
                  for j in range(n_slabs)]
    cur_rows = list(first_rows)
    krow = lax.broadcasted_iota(I32, (TOP_K, tm), 0)
    idx_rows = []
    idx_f = jnp.zeros((TOP_K, tm), F32)
    sc_k = jnp.zeros((TOP_K, tm), F32)
    sc_sum = jnp.zeros((1, tm), F32)
    prev = None
    for k in range(TOP_K):
        best = jnp.full((SUBLANES, tm), NEG_INF, F32)
        best_slab = jnp.zeros((SUBLANES, tm), F32)
        best_score = jnp.zeros((SUBLANES, tm), F32)
        for j in range(n_slabs):
            if prev is not None:
                cur_rows[j] = jnp.where(sub == prev - float(j * SUBLANES), NEG_INF, cur_rows[j])
            better = cur_rows[j] > best
            best = jnp.where(better, cur_rows[j], best)
            best_slab = jnp.where(better, float(j), best_slab)
            best_score = jnp.where(better, scores[j * SUBLANES:(j + 1) * SUBLANES, :], best_score)
        best_idx = best_slab * float(SUBLANES) + sub
        m = jnp.max(best, axis=0, keepdims=True)
        ik = jnp.min(jnp.where(best == m, best_idx, _BIG), axis=0, keepdims=True)
        sk = jnp.sum(jnp.where(best_idx == ik, best_score, 0.0), axis=0, keepdims=True)
        prev = ik
        idx_rows.append(ik)
        idx_f = jnp.where(krow == k, ik, idx_f)
        sc_k = jnp.where(krow == k, sk, sc_k)
        sc_sum = sc_sum + sk
    wts_ref[...] = sc_k / sc_sum * ROUTED_SCALE
    idx_ref[...] = idx_f.astype(I32)
    onehot = jnp.concatenate(
        [jnp.where(jnp.where(sub == prev - float(j * SUBLANES), NEG_INF, cur_rows[j]) != first_rows[j], 1.0, 0.0)
         for j in range(n_slabs)], axis=0)

    tri = (lax.broadcasted_iota(I32, (tm, tm), 0) < lax.broadcasted_iota(I32, (tm, tm), 1))
    excl = jnp.dot(onehot.astype(BF16), tri.astype(BF16), preferred_element_type=F32)
    rank_e = carry_scr[:, 0:1] + excl
    rank_k = jnp.zeros((TOP_K, tm), F32)
    for k in range(TOP_K):
        hit = row == idx_rows[k]
        rk = jnp.sum(jnp.where(hit, rank_e, 0.0), axis=0, keepdims=True)
        rank_k = jnp.where(krow == k, rk, rank_k)
    rank_ref[...] = rank_k.astype(I32)
    carry_scr[...] = carry_scr[...] + jnp.sum(onehot, axis=1, keepdims=True)
    cnt_ref[...] = carry_scr[...]


def _mix_out(attn, yc, x2, mod3, gattn, wout, gpost, gpre2, wrt, br, seq):
    t, d = x2.shape
    tm = min(TM_OUT, seq)
    tpb = seq // tm
    n_tiles = t // tm
    last = n_tiles - 1
    full = lambda a: pl.BlockSpec(a.shape, lambda i: (0,) * a.ndim)
    row = lambda w: pl.BlockSpec((tm, w), lambda i: (jnp.minimum(i, last), 0))
    col = pl.BlockSpec((TOP_K, tm), lambda i: (0, jnp.maximum(i - 1, 0)))
    return pl.pallas_call(
        _mix_out_kernel,
        grid=(n_tiles + 1,),
        in_specs=[row(attn.shape[1]), row(yc.shape[1]), row(d),
                  pl.BlockSpec((1, 6, d), lambda i: (jnp.minimum(i, last) // tpb, 0, 0)),
                  full(gattn), full(wout), full(gpost), full(gpre2), full(wrt), full(br)],
        out_specs=[row(d), row(d), row(_ROW_WORDS), col, col, col,
                   pl.BlockSpec((N_EXPERTS, LANES), lambda i: (0, 0))],
        out_shape=[jax.ShapeDtypeStruct((t, d), F32), jax.ShapeDtypeStruct((t, d), BF16),
                   jax.ShapeDtypeStruct((t, _ROW_WORDS), U32),
                   jax.ShapeDtypeStruct((TOP_K, t), I32), jax.ShapeDtypeStruct((TOP_K, t), F32),
                   jax.ShapeDtypeStruct((TOP_K, t), I32),
                   jax.ShapeDtypeStruct((N_EXPERTS, LANES), F32)],
        scratch_shapes=[pltpu.VMEM((N_EXPERTS, LANES), F32), pltpu.VMEM((2, N_EXPERTS, tm), F32),
                        pltpu.VMEM((2, N_EXPERTS, tm), F32)],
        compiler_params=_params(("arbitrary",)),
        name="mix_out",
    )(attn, yc, x2, mod3, gattn, wout, gpost, gpre2, wrt, br)


def _dest_kernel(idx_ref, rank_ref, pstart_ref, dest_ref):
    tm = idx_ref.shape[1]
    row = lax.broadcasted_iota(I32, (N_EXPERTS, tm), 0)
    krow = lax.broadcasted_iota(I32, (TOP_K, tm), 0)
    pstart = pstart_ref[...]
    idx = idx_ref[...]
    out = jnp.zeros((TOP_K, tm), F32)
    for k in range(TOP_K):
        hit = row == idx[k:k + 1, :]
        base = jnp.sum(jnp.where(hit, pstart, 0.0), axis=0, keepdims=True)
        out = jnp.where(krow == k, base, out)
    dest_ref[...] = out.astype(I32) + rank_ref[...]


def _dest(idx, rank, pstart):
    t = idx.shape[1]
    tm = min(TM_DEST, t)
    col = pl.BlockSpec((TOP_K, tm), lambda i: (0, i))
    return pl.pallas_call(
        _dest_kernel,
        grid=(t // tm,),
        in_specs=[col, col, pl.BlockSpec((N_EXPERTS, 1), lambda i: (0, 0))],
        out_specs=col,
        out_shape=jax.ShapeDtypeStruct((TOP_K, t), I32),
        compiler_params=_params(("arbitrary",)),
        name="dest",
    )(idx, rank, pstart)


_PAD_CHUNKS = tuple(BM_EXPERT >> s for s in range(1, BM_EXPERT.bit_length()))


SC_CORES = 2
SC_SUBCORES = 16
SC_ROWS = 128


def _sc_worker_split(n_chunks):
    workers = SC_CORES * SC_SUBCORES
    per_worker = max(1, n_chunks // workers)
    active = n_chunks // per_worker
    assert active * per_worker == n_chunks and active <= workers
    return per_worker, active


def _scatter_rows_sc(rows, idx3d, n_out):
    n, width = rows.shape
    n_chunks = n // SC_ROWS
    per_worker, active = _sc_worker_split(n_chunks)
    mesh = plsc.VectorSubcoreMesh(core_axis_name="c", subcore_axis_name="s")

    @functools.partial(
        pl.kernel, mesh=mesh, out_type=jax.ShapeDtypeStruct((n_out, width), rows.dtype),
        scratch_types=[pltpu.VMEM((TOP_K, SC_ROWS), I32), pltpu.VMEM((SC_ROWS, width), rows.dtype),
                       pltpu.SemaphoreType.DMA])
    def scatter(rows_hbm, idx_hbm, out_hbm, idx_v, rows_v, sem):
        wid = lax.axis_index("s") * SC_CORES + lax.axis_index("c")

        @pl.when(wid < active)
        def _():
            @pl.loop(0, per_worker)
            def _(c):
                chunk = wid * per_worker + c
                pltpu.sync_copy(rows_hbm.at[pl.ds(chunk * SC_ROWS, SC_ROWS)], rows_v)
                pltpu.sync_copy(idx_hbm.at[chunk], idx_v)
                copies = [pltpu.make_async_copy(rows_v, out_hbm.at[idx_v.at[k]], sem) for k in range(TOP_K)]
                for cp in copies:
                    cp.start()
                for cp in copies:
                    cp.wait()

    return scatter(rows, idx3d)


def _padfill_kernel(pad_from_ref, pad_n_ref, xs_in_hbm, xs_hbm, zero_scr, pad_sem):
    del xs_in_hbm
    zero_scr[...] = jnp.zeros(zero_scr.shape, zero_scr.dtype)

    def pad_copies(e, act):
        n = pad_n_ref[e]
        base = pad_from_ref[e]

        def single_rows(start, count):
            for j in range(SUBLANES - 1):
                @pl.when(j < count)
                def _():
                    act(pltpu.make_async_copy(zero_scr.at[pl.ds(0, 1), :],
                                              xs_hbm.at[pl.ds(start + j, 1), :], pad_sem))

        head = jnp.minimum(n, (SUBLANES - (base & (SUBLANES - 1))) & (SUBLANES - 1))
        single_rows(base, head)
        rest = n - head
        mid = base + head
        for rows in _PAD_CHUNKS:
            if rows >= SUBLANES:
                @pl.when((rest & rows) != 0)
                def _():
                    start = pl.multiple_of(mid + (rest & ~(2 * rows - 1)), SUBLANES)
                    act(pltpu.make_async_copy(zero_scr.at[pl.ds(0, rows), :],
                                              xs_hbm.at[pl.ds(start, rows), :], pad_sem))
        single_rows(mid + (rest & ~(SUBLANES - 1)), rest & (SUBLANES - 1))

    def issue_pad(e, carry):
        pad_copies(e, lambda cp: cp.start())
        return carry

    def drain_pad(e, carry):
        pad_copies(e, lambda cp: cp.wait())
        return carry

    lax.fori_loop(0, N_EXPERTS, issue_pad, 0)
    lax.fori_loop(0, N_EXPERTS, drain_pad, 0)


def _padfill(pad_from, pad_n, xs):
    return pl.pallas_call(
        _padfill_kernel,
        grid_spec=pltpu.PrefetchScalarGridSpec(
            num_scalar_prefetch=2,
            grid=(1,),
            in_specs=[pl.BlockSpec(memory_space=pl.ANY)],
            out_specs=pl.BlockSpec(memory_space=pl.ANY),
            scratch_shapes=[pltpu.VMEM((BM_EXPERT // 2, _ROW_WORDS), U32), pltpu.SemaphoreType.DMA]),
        out_shape=jax.ShapeDtypeStruct(xs.shape, xs.dtype),
        input_output_aliases={2: 0},
        compiler_params=_params(("arbitrary",)),
        name="padfill",
    )(pad_from, pad_n, xs)


_XS_SLOTS = 6
_YS_SLOTS = 2
_W_SLOTS = 5
_W_AHEAD = 3


def _expert_kernel(first_ref, ord_ref, uexp_ref, meta_ref, xs_hbm, wg_hbm, wu_hbm, wd_hbm, ys_hbm,
                   xs_buf, ys_buf, wg_buf, wu_buf, wd_buf, act_scr, xs_sem, ys_sem, w_sem):
    i = pl.program_id(0)
    nused = meta_ref[0]
    nexp = meta_ref[1]
    bm = xs_buf.shape[1]

    def xs_copy(b, slot):
        return pltpu.make_async_copy(xs_hbm.at[pl.ds(b * bm, bm), :], xs_buf.at[slot], xs_sem.at[slot])

    def ys_copy(b, slot):
        return pltpu.make_async_copy(ys_buf.at[slot], ys_hbm.at[pl.ds(b * bm, bm), :], ys_sem.at[slot])

    def w_copies(j, slot):
        e = uexp_ref[j]
        return (pltpu.make_async_copy(wg_hbm.at[e], wg_buf.at[slot], w_sem.at[slot, 0]),
                pltpu.make_async_copy(wu_hbm.at[e], wu_buf.at[slot], w_sem.at[slot, 1]),
                pltpu.make_async_copy(wd_hbm.at[e], wd_buf.at[slot], w_sem.at[slot, 2]))

    @pl.when(i == 0)
    def _():
        for s in range(_XS_SLOTS - 1):
            @pl.when(s < nused)
            def _():
                xs_copy(s, s).start()
        for s in range(_W_AHEAD):
            @pl.when(s < nexp)
            def _():
                for cp in w_copies(s, s):
                    cp.start()

    def fetch(b):
        ahead = b + _XS_SLOTS - 1

        @pl.when(ahead < nused)
        def _():
            xs_copy(ahead, ahead % _XS_SLOTS).start()

        j = ord_ref[b]

        @pl.when(first_ref[b] == 1)
        def _():
            for cp in w_copies(j, j % _W_SLOTS):
                cp.wait()
            nxt = j + _W_AHEAD

            @pl.when(nxt < nexp)
            def _():
                for cp in w_copies(nxt, nxt % _W_SLOTS):
                    cp.start()

        xs_copy(b, b % _XS_SLOTS).wait()

    def gate_up(b):
        ws = ord_ref[b] % _W_SLOTS
        lo, hi = _unpack_row_words(xs_buf[b % _XS_SLOTS])
        g = (jnp.dot(lo, wg_buf[ws, 0:_ROW_WORDS, :], preferred_element_type=F32)
             + jnp.dot(hi, wg_buf[ws, _ROW_WORDS:, :], preferred_element_type=F32))
        u = (jnp.dot(lo, wu_buf[ws, 0:_ROW_WORDS, :], preferred_element_type=F32)
             + jnp.dot(hi, wu_buf[ws, _ROW_WORDS:, :], preferred_element_type=F32))
        return g * jax.nn.sigmoid(g) * u

    def down(b, act):
        y = jnp.dot(act, wd_buf[ord_ref[b] % _W_SLOTS], preferred_element_type=F32)
        oslot = b % _YS_SLOTS
        ys_buf[oslot] = _pack_row_words(y[:, 0:_ROW_WORDS], y[:, _ROW_WORDS:])
        ys_copy(b, oslot).start()

    @pl.when(jnp.logical_and(i >= _YS_SLOTS + 1, i <= nused))
    def _():
        ys_copy(i - 1 - _YS_SLOTS, (i - 1) % _YS_SLOTS).wait()

    @pl.when(i == 0)
    def _():
        fetch(i)
        act_scr[...] = gate_up(i)

    @pl.when(jnp.logical_and(i >= 1, i < nused))
    def _():
        fetch(i)
        prev = act_scr[...]
        act_scr[...] = gate_up(i)
        down(i - 1, prev)

    @pl.when(i == nused)
    def _():
        down(i - 1, act_scr[...])
        ys_copy(i - 1, (i - 1) % _YS_SLOTS).wait()

        @pl.when(i >= 2)
        def _():
            ys_copy(i - 2, (i - 2) % _YS_SLOTS).wait()


def _experts(first, ordinal, uexp, meta, xs, w_gate, w_up, w_down):
    p = xs.shape[0]
    d = w_gate.shape[1]
    nb = p // BM_EXPERT
    anyspec = pl.BlockSpec(memory_space=pl.ANY)
    return pl.pallas_call(
        _expert_kernel,
        grid_spec=pltpu.PrefetchScalarGridSpec(
            num_scalar_prefetch=4,
            grid=(nb + 1,),
            in_specs=[anyspec, anyspec, anyspec, anyspec],
            out_specs=anyspec,
            scratch_shapes=[pltpu.VMEM((_XS_SLOTS, BM_EXPERT, _ROW_WORDS), U32),
                            pltpu.VMEM((_YS_SLOTS, BM_EXPERT, _ROW_WORDS), U32),
                            pltpu.VMEM((_W_SLOTS, d, EXPERT_DIM), F32),
                            pltpu.VMEM((_W_SLOTS, d, EXPERT_DIM), F32),
                            pltpu.VMEM((_W_SLOTS, EXPERT_DIM, d), F32),
                            pltpu.VMEM((BM_EXPERT, EXPERT_DIM), F32),
                            pltpu.SemaphoreType.DMA((_XS_SLOTS,)), pltpu.SemaphoreType.DMA((_YS_SLOTS,)),
                            pltpu.SemaphoreType.DMA((_W_SLOTS, 3))]),
        out_shape=jax.ShapeDtypeStruct((p, _ROW_WORDS), U32),
        compiler_params=_params(("arbitrary",)),
        name="experts",
    )(first, ordinal, uexp, meta, xs, w_gate, w_up, w_down)


SC_GATHER_ROWS = 64


def _gather_rows_sc(table, idx):
    n = idx.shape[0]
    width = table.shape[1]
    rows = SC_GATHER_ROWS
    per_worker, active = _sc_worker_split(n // rows)
    assert per_worker % 2 == 0 or per_worker == 1
    mesh = plsc.VectorSubcoreMesh(core_axis_name="c", subcore_axis_name="s")

    @functools.partial(
        pl.kernel, mesh=mesh, out_type=jax.ShapeDtypeStruct((n, width), table.dtype),
        scratch_types=[pltpu.VMEM((per_worker, rows), I32), pltpu.VMEM((2, rows, width), table.dtype),
                       pltpu.SemaphoreType.DMA((2,))])
    def gather(table_hbm, idx_hbm, out_hbm, idx_v, rows_v, sem):
        wid = lax.axis_index("s") * SC_CORES + lax.axis_index("c")

        def fetch(c, b):
            return pltpu.make_async_copy(table_hbm.at[idx_v.at[c]], rows_v.at[b], sem.at[b])

        @pl.when(wid < active)
        def _():
            first = wid * per_worker
            pltpu.sync_copy(idx_hbm.at[pl.ds(first, per_worker)], idx_v)
            fetch(0, 0).start()

            @pl.loop(0, per_worker, step=2)
            def _(c):
                for b in range(min(2, per_worker)):
                    cur = c + b

                    @pl.when(cur + 1 < per_worker)
                    def _():
                        fetch(cur + 1, 1 - b).start()

                    fetch(cur, b).wait()
                    pltpu.sync_copy(rows_v.at[b], out_hbm.at[pl.ds((first + cur) * rows, rows)])

    return gather(table, idx.reshape(n // rows, rows))


def _combine_kernel(wts_ref, yg_ref, h2_ref, x1_ref, mod_ref, wsg_ref, wsu_ref, wsd_ref, gpost_ref, *rest):
    o_ref = rest[-1]
    h2 = h2_ref[...]
    g = jnp.dot(h2, wsg_ref[...], preferred_element_type=F32)
    u = jnp.dot(h2, wsu_ref[...], preferred_element_type=F32)
    f = jnp.dot((g * jax.nn.sigmoid(g) * u).astype(BF16), wsd_ref[...], preferred_element_type=F32)

    wts = wts_ref[...]
    los = [f[:, sl * LANES:(sl + 1) * LANES] for sl in range(_ROW_SLABS)]
    his = [f[:, _ROW_WORDS + sl * LANES:_ROW_WORDS + (sl + 1) * LANES] for sl in range(_ROW_SLABS)]
    for k in range(TOP_K):
        wk = wts[:, k:k + 1]
        for sl in range(_ROW_SLABS):
            lo, hi = _unpack_row_words(yg_ref[k, :, sl * LANES:(sl + 1) * LANES])
            los[sl] = los[sl] + wk * lo
            his[sl] = his[sl] + wk * hi
    f = jnp.concatenate(los + his, axis=1)
    g2 = mod_ref[0, 5:6, :]
    o_ref[...] = x1_ref[...] + g2 * _rms(f, gpost_ref[...])


def _combine(wts_t, yg, h2, x1, mod3, wsg, wsu, wsd, gpost, seq, first_tile, partial_out):
    t, d = x1.shape
    tm = min(TM_COMBINE, seq)
    tpb = seq // tm
    full = lambda a: pl.BlockSpec(a.shape, lambda i: (0,) * a.ndim)
    row = lambda w: pl.BlockSpec((tm, w), lambda i: (i + first_tile, 0))
    args = [wts_t, yg, h2, x1, mod3, wsg, wsu, wsd, gpost]
    in_specs = [row(TOP_K), pl.BlockSpec((TOP_K, tm, _ROW_WORDS), lambda i: (0, i, 0)), row(d), row(d),
                pl.BlockSpec((1, 6, d), lambda i: ((i + first_tile) // tpb, 0, 0)),
                full(wsg), full(wsu), full(wsd), full(gpost)]
    aliases = {}
    if partial_out is not None:
        aliases = {len(args): 0}
        args.append(partial_out)
        in_specs.append(pl.BlockSpec(memory_space=pl.ANY))
    return pl.pallas_call(
        _combine_kernel,
        grid=(yg.shape[1] // tm,),
        in_specs=in_specs,
        out_specs=row(d),
        out_shape=jax.ShapeDtypeStruct((t, d), F32),
        input_output_aliases=aliases,
        compiler_params=_params(("arbitrary",)),
        name="combine",
    )(*args)


def _pack_weights(w_in, w_uq, w_ukv):
    d = w_in.shape[0]
    half = QK_ROPE_DIM // 2
    z = lambda n, c: jnp.zeros((n, c), F32)
    o = Q_LORA_RANK + KV_LORA_RANK
    kr = w_in[:, o:o + QK_ROPE_DIM]
    kr_grp = jnp.concatenate([z(d, QK_NOPE_DIM), kr, z(d, HEAD_PAD - QK_NOPE_DIM - QK_ROPE_DIM)], axis=1)
    kr_rot = jnp.concatenate([z(d, QK_NOPE_DIM), -kr[:, half:], kr[:, :half],
                              z(d, HEAD_PAD - QK_NOPE_DIM - QK_ROPE_DIM)], axis=1)
    win_p = jnp.concatenate([w_in[:, :o], kr_grp, kr_rot, w_in[:, o + QK_ROPE_DIM:]], axis=1)

    scale = float(QK_NOPE_DIM + QK_ROPE_DIM) ** -0.5 * float(np.log2(np.e))
    r = Q_LORA_RANK
    qd = QK_NOPE_DIM + QK_ROPE_DIM
    q_grp, q_rot = [], []
    for h in range(MLA_HEADS):
        nope = w_uq[:, h * qd:h * qd + QK_NOPE_DIM]
        rope = w_uq[:, h * qd + QK_NOPE_DIM:(h + 1) * qd]
        pad = z(r, HEAD_PAD - qd)
        q_grp.append(jnp.concatenate([nope, rope, pad], axis=1))
        q_rot.append(jnp.concatenate([z(r, QK_NOPE_DIM), -rope[:, half:], rope[:, :half], pad], axis=1))
    wuq_p = jnp.concatenate(q_grp + q_rot, axis=1) * scale

    c = KV_LORA_RANK
    kd = QK_NOPE_DIM + V_HEAD_DIM
    k_grp, v_grp = [], []
    for h in range(MLA_HEADS):
        k_grp.append(jnp.concatenate([w_ukv[:, h * kd:h * kd + QK_NOPE_DIM], z(c, HEAD_PAD - QK_NOPE_DIM)], axis=1))
        v_grp.append(jnp.concatenate([w_ukv[:, h * kd + QK_NOPE_DIM:(h + 1) * kd], z(c, HEAD_PAD - V_HEAD_DIM)], axis=1))
    wukv_p = jnp.concatenate(k_grp + v_grp, axis=1)
    return win_p.astype(BF16), wuq_p.astype(BF16), wukv_p.astype(BF16)


def _rope_inputs(positions):
    inv = 1.0 / (ROPE_THETA ** (jnp.arange(0, QK_ROPE_DIM, 2, dtype=F32) / QK_ROPE_DIM))
    freq = jnp.tile(inv, LANES // inv.shape[0]).reshape(1, LANES)
    pos = jnp.repeat(positions.astype(F32).reshape(-1, _ROPE_PACK), QK_ROPE_DIM, axis=1)
    return pos, freq


def _layer(x2, c, pos, freq, batch, seq, w_ada, b_ada, g_pre_mix, w_in, g_q_lat, w_uq, g_kv_lat, w_ukv,
           w_conv, g_attn_out, g_conv_out, w_out, g_post_mix, g_pre_ffn, w_router, b_router,
           w_gate, w_up, w_down, w_sh_gate, w_sh_up, w_sh_down, g_post_ffn):
    t, d = x2.shape
    r1 = lambda a: a.reshape(1, -1)

    c_pad = jnp.zeros((SUBLANES, d), F32).at[:batch].set(c)
    mod = _ada(c_pad, w_ada, r1(b_ada))[:batch]
    mod3 = mod.reshape(batch, 6, d)

    win_p, wuq_p, wukv_p = _pack_weights(w_in, w_uq, w_ukv)
    vone = jnp.zeros((1, HEAD_PAD), F32).at[0, V_HEAD_DIM].set(1.0)
    q, k, v, yc = _mix_in(x2, mod3, r1(g_pre_mix), win_p, r1(g_q_lat), wuq_p, r1(g_kv_lat), wukv_p,
                          vone, w_conv, r1(g_conv_out), pos, freq, seq)
    attn = _attention(q, k, v, batch, seq)
    x1, h2, h2p, idx, wts, rank, cnt = _mix_out(
        attn, yc, x2, mod3, r1(g_attn_out), w_out.astype(BF16), r1(g_post_mix), r1(g_pre_ffn),
        w_router.T, b_router.reshape(-1, 1), seq)

    counts = cnt[:, 0].astype(I32)
    padded = ((counts + BM_EXPERT - 1) // BM_EXPERT) * BM_EXPERT
    pad_end = jnp.cumsum(padded)
    pad_start = pad_end - padded
    m = t * TOP_K
    nb = (m + N_EXPERTS * (BM_EXPERT - 1)) // BM_EXPERT
    nused = pad_end[-1] // BM_EXPERT
    bidx = jnp.arange(nb, dtype=I32)
    blk_exp = jnp.sum((pad_end[None, :] <= (bidx * BM_EXPERT)[:, None]).astype(I32), axis=1)
    first = ((bidx < nused) & ((bidx == 0) | (blk_exp != jnp.roll(blk_exp, 1)))).astype(I32)
    ordinal = jnp.maximum(jnp.cumsum(first) - 1, 0).astype(I32)
    seen = jnp.cumsum((counts > 0).astype(I32))
    uexp = jnp.minimum(jnp.sum((seen[None, :] <= jnp.arange(N_EXPERTS, dtype=I32)[:, None]).astype(I32), axis=1),
                       N_EXPERTS - 1).astype(I32)
    meta = jnp.stack([nused, seen[-1]]).astype(I32)

    dest = _dest(idx, rank, pad_start.astype(F32).reshape(-1, 1))
    xs = _scatter_rows_sc(h2p, dest.reshape(TOP_K, -1, SC_ROWS).transpose(1, 0, 2), nb * BM_EXPERT)
    xs = _padfill((pad_start + counts).astype(I32), (padded - counts).astype(I32), xs)
    ys = _experts(first, ordinal, uexp, meta, xs, w_gate, w_up, w_down)
    tm_c = min(TM_COMBINE, seq)
    half_rows = (t // 2) * TOP_K
    even_split = half_rows % (2 * SC_GATHER_ROWS * SC_CORES * SC_SUBCORES) == 0
    parts = 2 if t % (2 * tm_c) == 0 and even_split else 1
    tp = t // parts
    wsg, wsu, wsd = w_sh_gate.astype(BF16), w_sh_up.astype(BF16), w_sh_down.astype(BF16)
    out = None
    for part in range(parts):
        part_dest = dest[:, part * tp:(part + 1) * tp].reshape(-1)
        yg = _gather_rows_sc(ys, part_dest).reshape(TOP_K, tp, _ROW_WORDS)
        out = _combine(wts.T, yg, h2, x1, mod3, wsg, wsu, wsd, r1(g_post_ffn), seq, part * tp // tm_c, out)
    return out


def kernel(x, c, positions, w_ada, b_ada, g_pre_mix, w_in, g_q_lat, w_uq, g_kv_lat, w_ukv, w_conv, g_attn_out, g_conv_out, w_out, g_post_mix, g_pre_ffn, w_router, b_router, w_gate, w_up, w_down, w_sh_gate, w_sh_up, w_sh_down, g_post_ffn):
    batch, seq, d = x.shape
    pos, freq = _rope_inputs(positions)
    x2 = x.reshape(batch * seq, d)
    for l in range(w_ada.shape[0]):
        x2 = _layer(x2, c, pos, freq, batch, seq, w_ada[l], b_ada[l], g_pre_mix[l], w_in[l], g_q_lat[l],
                    w_uq[l], g_kv_lat[l], w_ukv[l], w_conv[l], g_attn_out[l], g_conv_out[l], w_out[l],
                    g_post_mix[l], g_pre_ffn[l], w_router[l], b_router[l], w_gate[l], w_up[l], w_down[l],
                    w_sh_gate[l], w_sh_up[l], w_sh_down[l], g_post_ffn[l])
    return x2.reshape(batch, seq, d)
```

```python
import functools

import jax
import jax.numpy as jnp
import numpy as np
from jax import lax
from jax.experimental import pallas as pl
from jax.experimental.pallas import tpu as pltpu
from jax.experimental.pallas import tpu_sc as plsc

F32 = jnp.float32
BF16 = jnp.bfloat16
I32 = jnp.int32
U32 = jnp.uint32

CHUNK = 64
MLA_HEADS = 8
QK_NOPE_DIM = 64
QK_ROPE_DIM = 32
V_HEAD_DIM = 64
Q_LORA_RANK = 384
KV_LORA_RANK = 256
ROPE_THETA = 10000.0
CONV_WIDTH = 3
N_EXPERTS = 256
TOP_K = 8
N_EXPERT_GROUPS = 8
TOPK_GROUPS = 4
EXPERT_DIM = 256
ROUTED_SCALE = 2.5
EPS = 1e-6

LANES = 128
SUBLANES = 8
HEAD_PAD = LANES
VMEM_LIMIT_BYTES = 56 * 1024 * 1024

TM_IN = 1024
TQ_ATTN = 512
TM_OUT = 512
TM_DEST = 2048
BM_EXPERT = 256
TM_COMBINE = 512

NEG_INF = float("-inf")


def _rms(x, g):
    return x * lax.rsqrt(jnp.mean(x * x, axis=-1, keepdims=True) + EPS) * g


_HI_MASK = np.uint32(0xFFFF0000)
_ROW_WORDS = 512
_ROW_SLABS = _ROW_WORDS // LANES


def _pack_row_words(lo, hi):
    lo_w = lax.bitcast_convert_type(lo.astype(BF16).astype(F32), U32) >> 16
    hi_w = lax.bitcast_convert_type(hi.astype(BF16).astype(F32), U32) & _HI_MASK
    return lo_w | hi_w


def _unpack_row_words(w):
    return (lax.bitcast_convert_type(w << 16, F32), lax.bitcast_convert_type(w & _HI_MASK, F32))


def _params(sem):
    return pltpu.CompilerParams(dimension_semantics=sem, vmem_limit_bytes=VMEM_LIMIT_BYTES)


def _ada_kernel(c_ref, w_ref, b_ref, o_ref):
    c = c_ref[...]
    s = c * jax.nn.sigmoid(c)
    o_ref[...] = jnp.dot(s, w_ref[...], preferred_element_type=F32,
                         precision=lax.Precision.HIGHEST) + b_ref[...]


def _ada(c_pad, w, b):
    rows, d = c_pad.shape
    n = w.shape[1]
    tn = 1536
    return pl.pallas_call(
        _ada_kernel,
        grid=(n // tn,),
        in_specs=[pl.BlockSpec((rows, d), lambda j: (0, 0)),
                  pl.BlockSpec((d, tn), lambda j: (0, j)),
                  pl.BlockSpec((1, tn), lambda j: (0, j))],
        out_specs=pl.BlockSpec((rows, tn), lambda j: (0, j)),
        out_shape=jax.ShapeDtypeStruct((rows, n), F32),
        compiler_params=_params(("arbitrary",)),
        name="ada",
    )(c_pad, w, b)


_CQ0, _CQ1 = 0, Q_LORA_RANK
_CKV0, _CKV1 = _CQ1, _CQ1 + KV_LORA_RANK
_KR0, _KR1 = _CKV1, _CKV1 + 2 * HEAD_PAD
_CONV_DIM = 512
_GB0 = _KR1
_GC0 = _GB0 + _CONV_DIM
_XV0 = _GC0 + _CONV_DIM
_WIN_COLS = _XV0 + _CONV_DIM
_QW = MLA_HEADS * HEAD_PAD
_ROPE_PACK = LANES // QK_ROPE_DIM


def _mix_in_kernel(tiles_per_batch, x_ref, mod_ref, gpre_ref, win_ref, gq_ref, wuq_ref, gkv_ref,
                   wukv_ref, vone_ref, wconv_ref, gconv_ref, pos_ref, freq_ref,
                   q_ref, k_ref, v_ref, yc_ref, h_scr, u_scr, cos_scr, sin_scr):
    i = pl.program_id(0)
    tm = x_ref.shape[0]
    sh1 = mod_ref[0, 0:1, :]
    sc1 = mod_ref[0, 1:2, :]
    h = _rms(x_ref[...], gpre_ref[...]) * (1.0 + sc1) + sh1
    h_scr[...] = h.astype(BF16)
    ang4 = pos_ref[...] * freq_ref[...]
    cos4 = jnp.cos(ang4)
    sin4 = jnp.sin(ang4)
    lane = lax.broadcasted_iota(I32, ang4.shape, 1)
    on_rope = jnp.logical_and(lane >= QK_NOPE_DIM, lane < QK_NOPE_DIM + QK_ROPE_DIM)
    for g in range(_ROPE_PACK):
        shift = (QK_NOPE_DIM - QK_ROPE_DIM * g) % LANES
        cg = cos4 if shift == 0 else pltpu.roll(cos4, shift, axis=1)
        sg = sin4 if shift == 0 else pltpu.roll(sin4, shift, axis=1)
        cos_scr[pl.ds(g, tm // _ROPE_PACK, stride=_ROPE_PACK), :] = jnp.where(on_rope, cg, 1.0)
        sin_scr[pl.ds(g, tm // _ROPE_PACK, stride=_ROPE_PACK), :] = jnp.where(on_rope, sg, 0.0)
    cos = cos_scr[...]
    sin = sin_scr[...]

    cq = jnp.dot(h_scr[...], win_ref[:, _CQ0:_CQ1], preferred_element_type=F32)
    cqn = _rms(cq, gq_ref[...]).astype(BF16)
    qq = jnp.dot(cqn, wuq_ref[...], preferred_element_type=F32)
    for hd in range(MLA_HEADS):
        lo = hd * HEAD_PAD
        qh = qq[:, lo:lo + HEAD_PAD] * cos + qq[:, _QW + lo:_QW + lo + HEAD_PAD] * sin
        q_ref[:, lo:lo + HEAD_PAD] = qh.astype(BF16)

    ckv = jnp.dot(h_scr[...], win_ref[:, _CKV0:_CKV1], preferred_element_type=F32)
    ckvn = _rms(ckv, gkv_ref[...]).astype(BF16)
    kv = jnp.dot(ckvn, wukv_ref[...], preferred_element_type=F32)
    krr = jnp.dot(h_scr[...], win_ref[:, _KR0:_KR1], preferred_element_type=F32)
    kr = krr[:, 0:HEAD_PAD] * cos + krr[:, HEAD_PAD:2 * HEAD_PAD] * sin
    vone = vone_ref[...]
    for hd in range(MLA_HEADS):
        lo = hd * HEAD_PAD
        k_ref[:, lo:lo + HEAD_PAD] = (kv[:, lo:lo + HEAD_PAD] + kr).astype(BF16)
        v_ref[:, lo:lo + HEAD_PAD] = (kv[:, _QW + lo:_QW + lo + HEAD_PAD] + vone).astype(BF16)

    gb = jnp.dot(h_scr[...], win_ref[:, _GB0:_GC0], preferred_element_type=F32)
    gc = jnp.dot(h_scr[...], win_ref[:, _GC0:_XV0], preferred_element_type=F32)
    xv = jnp.dot(h_scr[...], win_ref[:, _XV0:_WIN_COLS], preferred_element_type=F32)
    u = gc * xv
    prev = u_scr[tm:tm + SUBLANES, :]
    first = (i % tiles_per_batch) == 0
    u_scr[0:SUBLANES, :] = jnp.where(first, jnp.zeros_like(prev), prev)
    u_scr[SUBLANES:tm + SUBLANES, :] = u
    um1 = u_scr[SUBLANES - 1:tm + SUBLANES - 1, :]
    um2 = u_scr[SUBLANES - 2:tm + SUBLANES - 2, :]
    assert wconv_ref.shape[0] == CONV_WIDTH
    conv = wconv_ref[0:1, :] * um2 + wconv_ref[1:2, :] * um1 + wconv_ref[2:3, :] * u
    yc_ref[...] = _rms(gb * conv, gconv_ref[...]).astype(BF16)


def _mix_in(x2, mod3, gpre, win_p, gq, wuq_p, gkv, wukv_p, vone, wconv, gconv, pos, freq, seq):
    t, d = x2.shape
    tm = min(TM_IN, seq)
    tpb = seq // tm
    full = lambda a: pl.BlockSpec(a.shape, lambda i: (0,) * a.ndim)
    row = lambda w: pl.BlockSpec((tm, w), lambda i: (i, 0))
    return pl.pallas_call(
        functools.partial(_mix_in_kernel, tpb),
        grid=(t // tm,),
        in_specs=[row(d),
                  pl.BlockSpec((1, 6, d), lambda i: (i // tpb, 0, 0)),
                  full(gpre), full(win_p), full(gq), full(wuq_p), full(gkv), full(wukv_p),
                  full(vone), full(wconv), full(gconv),
                  pl.BlockSpec((tm // _ROPE_PACK, LANES), lambda i: (i, 0)), full(freq)],
        out_specs=[row(_QW), row(_QW), row(_QW), row(_CONV_DIM)],
        out_shape=[jax.ShapeDtypeStruct((t, _QW), BF16), jax.ShapeDtypeStruct((t, _QW), BF16),
                   jax.ShapeDtypeStruct((t, _QW), BF16), jax.ShapeDtypeStruct((t, _CONV_DIM), BF16)],
        scratch_shapes=[pltpu.VMEM((tm, d), BF16), pltpu.VMEM((tm + SUBLANES, _CONV_DIM), F32),
                        pltpu.VMEM((tm, HEAD_PAD), F32), pltpu.VMEM((tm, HEAD_PAD), F32)],
        compiler_params=_params(("arbitrary",)),
        name="mix_in",
    )(x2, mod3, gpre, win_p, gq, wuq_p, gkv, wukv_p, vone, wconv, gconv, pos, freq)


_HEADS_PER_STEP = 2


def _attn_kernel(tq, q_ref, k_ref, v_ref, o_ref, s_scr, mrun_scr, mb_scr, acc_scr):
    tk = tq
    nq = q_ref.shape[0] // tq
    lane_groups = tk // LANES
    heads = range(_HEADS_PER_STEP)
    lanes = [slice(hh * HEAD_PAD, (hh + 1) * HEAD_PAD) for hh in heads]

    def tile_max(s):
        m = s[:, 0:LANES]
        for g in range(1, lane_groups):
            m = jnp.maximum(m, s[:, g * LANES:(g + 1) * LANES])
        return m

    def scores(hh, qi, kv):
        off = pl.multiple_of(kv * tk, tk)
        return lax.dot_general(q_ref[qi * tq:(qi + 1) * tq, lanes[hh]], k_ref[pl.ds(off, tk), lanes[hh]],
                               (((1,), (1,)), ((), ())), preferred_element_type=F32)

    rc = lax.broadcasted_iota(I32, (tq, tk), 0) // CHUNK
    cc = lax.broadcasted_iota(I32, (tq, tk), 1) // CHUNK

    def diagonal(qi):
        for hh in heads:
            s = jnp.where(cc <= rc, scores(hh, qi, qi), NEG_INF)
            s_scr[hh, qi] = s
            m_row = jnp.max(jnp.maximum(mrun_scr[hh], tile_max(s)), axis=1, keepdims=True)
            mb_scr[hh] = jnp.broadcast_to(m_row, (tq, LANES))

    mrun_scr[...] = jnp.full(mrun_scr.shape, NEG_INF, F32)
    diagonal(0)
    for qi in range(nq):
        has_next = qi + 1 < nq
        acc_scr[...] = jnp.zeros(acc_scr.shape, F32)
        if has_next:
            mrun_scr[...] = jnp.full(mrun_scr.shape, NEG_INF, F32)

        def body(kv, carry, qi=qi, has_next=has_next):
            off = pl.multiple_of(kv * tk, tk)
            for hh in heads:
                mb = mb_scr[hh]
                p = jnp.concatenate(
                    [jnp.exp2(s_scr[hh, kv, :, g * LANES:(g + 1) * LANES] - mb) for g in range(lane_groups)],
                    axis=1).astype(BF16)
                acc_scr[hh] += jnp.dot(p, v_ref[pl.ds(off, tk), lanes[hh]], preferred_element_type=F32)
                if has_next:
                    s = scores(hh, qi + 1, kv)
                    s_scr[hh, kv] = s
                    mrun_scr[hh] = jnp.maximum(mrun_scr[hh], tile_max(s))
            return carry

        lax.fori_loop(0, qi + 1, body, 0, unroll=4)
        for hh in heads:
            acc = acc_scr[hh]
            o = acc[:, 0:V_HEAD_DIM] / acc[:, V_HEAD_DIM:V_HEAD_DIM + 1]
            o_ref[qi * tq:(qi + 1) * tq, hh * V_HEAD_DIM:(hh + 1) * V_HEAD_DIM] = o.astype(BF16)
        if has_next:
            diagonal(qi + 1)


def _attention(q, k, v, batch, seq):
    t = q.shape[0]
    tq = min(TQ_ATTN, seq)
    nq = seq // tq
    hw = _HEADS_PER_STEP * HEAD_PAD
    ow = _HEADS_PER_STEP * V_HEAD_DIM
    blk = lambda w: pl.BlockSpec((seq, w), lambda b, j: (b, j))
    return pl.pallas_call(
        functools.partial(_attn_kernel, tq),
        grid=(batch, MLA_HEADS // _HEADS_PER_STEP),
        in_specs=[blk(hw), blk(hw), blk(hw)],
        out_specs=blk(ow),
        out_shape=jax.ShapeDtypeStruct((t, MLA_HEADS * V_HEAD_DIM), BF16),
        scratch_shapes=[pltpu.VMEM((_HEADS_PER_STEP, nq, tq, tq), F32),
                        pltpu.VMEM((_HEADS_PER_STEP, tq, LANES), F32),
                        pltpu.VMEM((_HEADS_PER_STEP, tq, LANES), F32),
                        pltpu.VMEM((_HEADS_PER_STEP, tq, HEAD_PAD), F32)],
        compiler_params=_params(("arbitrary", "arbitrary")),
        name="attn",
    )(q, k, v)


_GROUP_SIZE = N_EXPERTS // N_EXPERT_GROUPS
_BIG = 1.0e9


def _mix_out_kernel(attn_ref, yc_ref, x_ref, mod_ref, gattn_ref, wout_ref, gpost_ref, gpre2_ref,
                    wrt_ref, br_ref, x1_ref, h2_ref, h2p_ref, idx_ref, wts_ref, rank_ref, cnt_ref,
                    carry_scr, scores_scr, sel_scr):
    i = pl.program_id(0)
    n_tiles = pl.num_programs(0) - 1

    @pl.when(i == 0)
    def _():
        carry_scr[...] = jnp.zeros(carry_scr.shape, F32)
        scores, sel = _project_tile(attn_ref, yc_ref, x_ref, mod_ref, gattn_ref, wout_ref, gpost_ref,
                                    gpre2_ref, wrt_ref, br_ref, x1_ref, h2_ref, h2p_ref)
        scores_scr[0] = scores
        sel_scr[0] = sel

    @pl.when(jnp.logical_and(i >= 1, i < n_tiles))
    def _():
        prev_scores = scores_scr[(i - 1) % 2]
        prev_sel = sel_scr[(i - 1) % 2]
        scores, sel = _project_tile(attn_ref, yc_ref, x_ref, mod_ref, gattn_ref, wout_ref, gpost_ref,
                                    gpre2_ref, wrt_ref, br_ref, x1_ref, h2_ref, h2p_ref)
        scores_scr[i % 2] = scores
        sel_scr[i % 2] = sel
        _route_tile(prev_scores, prev_sel, idx_ref, wts_ref, rank_ref, cnt_ref, carry_scr)

    @pl.when(i == n_tiles)
    def _():
        _route_tile(scores_scr[(i - 1) % 2], sel_scr[(i - 1) % 2], idx_ref, wts_ref, rank_ref, cnt_ref,
                    carry_scr)


def _project_tile(attn_ref, yc_ref, x_ref, mod_ref, gattn_ref, wout_ref, gpost_ref, gpre2_ref,
                  wrt_ref, br_ref, x1_ref, h2_ref, h2p_ref):
    half = attn_ref.shape[1]
    an = _rms(attn_ref[...].astype(F32), gattn_ref[...]).astype(BF16)
    mix = (jnp.dot(an, wout_ref[0:half, :], preferred_element_type=F32)
           + jnp.dot(yc_ref[...], wout_ref[half:, :], preferred_element_type=F32))
    g1 = mod_ref[0, 2:3, :]
    sh2 = mod_ref[0, 3:4, :]
    sc2 = mod_ref[0, 4:5, :]
    x1 = x_ref[...] + g1 * _rms(mix, gpost_ref[...])
    x1_ref[...] = x1
    h2 = _rms(x1, gpre2_ref[...]) * (1.0 + sc2) + sh2
    h2_ref[...] = h2.astype(BF16)
    h2p_ref[...] = _pack_row_words(h2[:, 0:_ROW_WORDS], h2[:, _ROW_WORDS:])

    logits = lax.dot_general(wrt_ref[...], h2, (((1,), (1,)), ((), ())),
                             preferred_element_type=F32, precision=lax.Precision.HIGHEST)
    scores = jax.nn.sigmoid(logits)
    return scores, scores + br_ref[...]


def _route_tile(scores, sel, idx_ref, wts_ref, rank_ref, cnt_ref, carry_scr):
    tm = scores.shape[1]
    row = lax.broadcasted_iota(I32, (N_EXPERTS, tm), 0).astype(F32)

    gscore = []
    rw = lax.broadcasted_iota(I32, (_GROUP_SIZE, tm), 0).astype(F32)
    for g in range(N_EXPERT_GROUPS):
        blk = sel[g * _GROUP_SIZE:(g + 1) * _GROUP_SIZE, :]
        m1 = jnp.max(blk, axis=0, keepdims=True)
        i1 = jnp.min(jnp.where(blk == m1, rw, _BIG), axis=0, keepdims=True)
        m2 = jnp.max(jnp.where(rw == i1, NEG_INF, blk), axis=0, keepdims=True)
        gscore.append(m1 + m2)

    gkeep = [jnp.zeros((1, tm), F32) for _ in range(N_EXPERT_GROUPS)]
    for _ in range(TOPK_GROUPS):
        mg = functools.reduce(jnp.maximum, gscore)
        ig = functools.reduce(jnp.minimum, [jnp.where(gscore[g] == mg, float(g), _BIG)
                                            for g in range(N_EXPERT_GROUPS)])
        for g in range(N_EXPERT_GROUPS):
            hit = ig == float(g)
            gkeep[g] = jnp.where(hit, 1.0, gkeep[g])
            gscore[g] = jnp.where(hit, NEG_INF, gscore[g])
    n_slabs = N_EXPERTS // SUBLANES
    slabs_per_group = _GROUP_SIZE // SUBLANES
    sub = lax.broadcasted_iota(I32, (SUBLANES, tm), 0).astype(F32)
    first_rows = [jnp.where(gkeep[j // slabs_per_group] > 0.0, sel[j * SUBLANES:(j + 1) * SUBLANES, :], NEG_INF)
                  for j in range(n_slabs)]
    cur_rows = list(first_rows)
    krow = lax.broadcasted_iota(I32, (TOP_K, tm), 0)
    idx_rows = []
    idx_f = jnp.zeros((TOP_K, tm), F32)
    sc_k = jnp.zeros((TOP_K, tm), F32)
    sc_sum = jnp.zeros((1, tm), F32)
    prev = None
    for k in range(TOP_K):
        best = jnp.full((SUBLANES, tm), NEG_INF, F32)
        best_slab = jnp.zeros((SUBLANES, tm), F32)
        best_score = jnp.zeros((SUBLANES, tm), F32)
        for j in range(n_slabs):
            if prev is not None:
                cur_rows[j] = jnp.where(sub == prev - float(j * SUBLANES), NEG_INF, cur_rows[j])
            better = cur_rows[j] > best
            best = jnp.where(better, cur_rows[j], best)
            best_slab = jnp.where(better, float(j), best_slab)
            best_score = jnp.where(better, scores[j * SUBLANES:(j + 1) * SUBLANES, :], best_score)
        best_idx = best_slab * float(SUBLANES) + sub
        m = jnp.max(best, axis=0, keepdims=True)
        ik = jnp.min(jnp.where(best == m, best_idx, _BIG), axis=0, keepdims=True)
        sk = jnp.sum(jnp.where(best_idx == ik, best_score, 0.0), axis=0, keepdims=True)
        prev = ik
        idx_rows.append(ik)
        idx_f = jnp.where(krow == k, ik, idx_f)
        sc_k = jnp.where(krow == k, sk, sc_k)
        sc_sum = sc_sum + sk
    wts_ref[...] = sc_k / sc_sum * ROUTED_SCALE
    idx_ref[...] = idx_f.astype(I32)
    onehot = jnp.concatenate(
        [jnp.where(jnp.where(sub == prev - float(j * SUBLANES), NEG_INF, cur_rows[j]) != first_rows[j], 1.0, 0.0)
         for j in range(n_slabs)], axis=0)

    tri = (lax.broadcasted_iota(I32, (tm, tm), 0) < lax.broadcasted_iota(I32, (tm, tm), 1))
    excl = jnp.dot(onehot.astype(BF16), tri.astype(BF16), preferred_element_type=F32)
    rank_e = carry_scr[:, 0:1] + excl
    rank_k = jnp.zeros((TOP_K, tm), F32)
    for k in range(TOP_K):
        hit = row == idx_rows[k]
        rk = jnp.sum(jnp.where(hit, rank_e, 0.0), axis=0, keepdims=True)
        rank_k = jnp.where(krow == k, rk, rank_k)
    rank_ref[...] = rank_k.astype(I32)
    carry_scr[...] = carry_scr[...] + jnp.sum(onehot, axis=1, keepdims=True)
    cnt_ref[...] = carry_scr[...]


def _mix_out(attn, yc, x2, mod3, gattn, wout, gpost, gpre2, wrt, br, seq):
    t, d = x2.shape
    tm = min(TM_OUT, seq)
    tpb = seq // tm
    n_tiles = t // tm
    last = n_tiles - 1
    full = lambda a: pl.BlockSpec(a.shape, lambda i: (0,) * a.ndim)
    row = lambda w: pl.BlockSpec((tm, w), lambda i: (jnp.minimum(i, last), 0))
    col = pl.BlockSpec((TOP_K, tm), lambda i: (0, jnp.maximum(i - 1, 0)))
    return pl.pallas_call(
        _mix_out_kernel,
        grid=(n_tiles + 1,),
        in_specs=[row(attn.shape[1]), row(yc.shape[1]), row(d),
                  pl.BlockSpec((1, 6, d), lambda i: (jnp.minimum(i, last) // tpb, 0, 0)),
                  full(gattn), full(wout), full(gpost), full(gpre2), full(wrt), full(br)],
        out_specs=[row(d), row(d), row(_ROW_WORDS), col, col, col,
                   pl.BlockSpec((N_EXPERTS, LANES), lambda i: (0, 0))],
        out_shape=[jax.ShapeDtypeStruct((t, d), F32), jax.ShapeDtypeStruct((t, d), BF16),
                   jax.ShapeDtypeStruct((t, _ROW_WORDS), U32),
                   jax.ShapeDtypeStruct((TOP_K, t), I32), jax.ShapeDtypeStruct((TOP_K, t), F32),
                   jax.ShapeDtypeStruct((TOP_K, t), I32),
                   jax.ShapeDtypeStruct((N_EXPERTS, LANES), F32)],
        scratch_shapes=[pltpu.VMEM((N_EXPERTS, LANES), F32), pltpu.VMEM((2, N_EXPERTS, tm), F32),
                        pltpu.VMEM((2, N_EXPERTS, tm), F32)],
        compiler_params=_params(("arbitrary",)),
        name="mix_out",
    )(attn, yc, x2, mod3, gattn, wout, gpost, gpre2, wrt, br)


def _dest_kernel(idx_ref, rank_ref, pstart_ref, dest_ref):
    tm = idx_ref.shape[1]
    row = lax.broadcasted_iota(I32, (N_EXPERTS, tm), 0)
    krow = lax.broadcasted_iota(I32, (TOP_K, tm), 0)
    pstart = pstart_ref[...]
    idx = idx_ref[...]
    out = jnp.zeros((TOP_K, tm), F32)
    for k in range(TOP_K):
        hit = row == idx[k:k + 1, :]
        base = jnp.sum(jnp.where(hit, pstart, 0.0), axis=0, keepdims=True)
        out = jnp.where(krow == k, base, out)
    dest_ref[...] = out.astype(I32) + rank_ref[...]


def _dest(idx, rank, pstart):
    t = idx.shape[1]
    tm = min(TM_DEST, t)
    col = pl.BlockSpec((TOP_K, tm), lambda i: (0, i))
    return pl.pallas_call(
        _dest_kernel,
        grid=(t // tm,),
        in_specs=[col, col, pl.BlockSpec((N_EXPERTS, 1), lambda i: (0, 0))],
        out_specs=col,
        out_shape=jax.ShapeDtypeStruct((TOP_K, t), I32),
        compiler_params=_params(("arbitrary",)),
        name="dest",
    )(idx, rank, pstart)


_PAD_CHUNKS = tuple(BM_EXPERT >> s for s in range(1, BM_EXPERT.bit_length()))


SC_CORES = 2
SC_SUBCORES = 16
SC_ROWS = 128


def _sc_worker_split(n_chunks):
    workers = SC_CORES * SC_SUBCORES
    per_worker = max(1, n_chunks // workers)
    active = n_chunks // per_worker
    assert active * per_worker == n_chunks and active <= workers
    return per_worker, active


def _scatter_rows_sc(rows, idx3d, n_out):
    n, width = rows.shape
    n_chunks = n // SC_ROWS
    per_worker, active = _sc_worker_split(n_chunks)
    mesh = plsc.VectorSubcoreMesh(core_axis_name="c", subcore_axis_name="s")

    @functools.partial(
        pl.kernel, mesh=mesh, out_type=jax.ShapeDtypeStruct((n_out, width), rows.dtype),
        scratch_types=[pltpu.VMEM((TOP_K, SC_ROWS), I32), pltpu.VMEM((SC_ROWS, width), rows.dtype),
                       pltpu.SemaphoreType.DMA])
    def scatter(rows_hbm, idx_hbm, out_hbm, idx_v, rows_v, sem):
        wid = lax.axis_index("s") * SC_CORES + lax.axis_index("c")

        @pl.when(wid < active)
        def _():
            @pl.loop(0, per_worker)
            def _(c):
                chunk = wid * per_worker + c
                pltpu.sync_copy(rows_hbm.at[pl.ds(chunk * SC_ROWS, SC_ROWS)], rows_v)
                pltpu.sync_copy(idx_hbm.at[chunk], idx_v)
                copies = [pltpu.make_async_copy(rows_v, out_hbm.at[idx_v.at[k]], sem) for k in range(TOP_K)]
                for cp in copies:
                    cp.start()
                for cp in copies:
                    cp.wait()

    return scatter(rows, idx3d)


def _padfill_kernel(pad_from_ref, pad_n_ref, xs_in_hbm, xs_hbm, zero_scr, pad_sem):
    del xs_in_hbm
    zero_scr[...] = jnp.zeros(zero_scr.shape, zero_scr.dtype)

    def pad_copies(e, act):
        n = pad_n_ref[e]
        base = pad_from_ref[e]

        def single_rows(start, count):
            for j in range(SUBLANES - 1):
                @pl.when(j < count)
                def _():
                    act(pltpu.make_async_copy(zero_scr.at[pl.ds(0, 1), :],
                                              xs_hbm.at[pl.ds(start + j, 1), :], pad_sem))

        head = jnp.minimum(n, (SUBLANES - (base & (SUBLANES - 1))) & (SUBLANES - 1))
        single_rows(base, head)
        rest = n - head
        mid = base + head
        for rows in _PAD_CHUNKS:
            if rows >= SUBLANES:
                @pl.when((rest & rows) != 0)
                def _():
                    start = pl.multiple_of(mid + (rest & ~(2 * rows - 1)), SUBLANES)
                    act(pltpu.make_async_copy(zero_scr.at[pl.ds(0, rows), :],
                                              xs_hbm.at[pl.ds(start, rows), :], pad_sem))
        single_rows(mid + (rest & ~(SUBLANES - 1)), rest & (SUBLANES - 1))

    def issue_pad(e, carry):
        pad_copies(e, lambda cp: cp.start())
        return carry

    def drain_pad(e, carry):
        pad_copies(e, lambda cp: cp.wait())
        return carry

    lax.fori_loop(0, N_EXPERTS, issue_pad, 0)
    lax.fori_loop(0, N_EXPERTS, drain_pad, 0)


def _padfill(pad_from, pad_n, xs):
    return pl.pallas_call(
        _padfill_kernel,
        grid_spec=pltpu.PrefetchScalarGridSpec(
            num_scalar_prefetch=2,
            grid=(1,),
            in_specs=[pl.BlockSpec(memory_space=pl.ANY)],
            out_specs=pl.BlockSpec(memory_space=pl.ANY),
            scratch_shapes=[pltpu.VMEM((BM_EXPERT // 2, _ROW_WORDS), U32), pltpu.SemaphoreType.DMA]),
        out_shape=jax.ShapeDtypeStruct(xs.shape, xs.dtype),
        input_output_aliases={2: 0},
        compiler_params=_params(("arbitrary",)),
        name="padfill",
    )(pad_from, pad_n, xs)


_XS_SLOTS = 4
_YS_SLOTS = 2
_W_SLOTS = 4
_W_AHEAD = 2


def _expert_kernel(first_ref, ord_ref, uexp_ref, meta_ref, xs_hbm, wg_hbm, wu_hbm, wd_hbm, ys_hbm,
                   xs_buf, ys_buf, wg_buf, wu_buf, wd_buf, act_scr, xs_sem, ys_sem, w_sem):
    i = pl.program_id(0)
    nused = meta_ref[0]
    nexp = meta_ref[1]
    bm = xs_buf.shape[1]

    def xs_copy(b, slot):
        return pltpu.make_async_copy(xs_hbm.at[pl.ds(b * bm, bm), :], xs_buf.at[slot], xs_sem.at[slot])

    def ys_copy(b, slot):
        return pltpu.make_async_copy(ys_buf.at[slot], ys_hbm.at[pl.ds(b * bm, bm), :], ys_sem.at[slot])

    def w_copies(j, slot):
        e = uexp_ref[j]
        return (pltpu.make_async_copy(wg_hbm.at[e], wg_buf.at[slot], w_sem.at[slot, 0]),
                pltpu.make_async_copy(wu_hbm.at[e], wu_buf.at[slot], w_sem.at[slot, 1]),
                pltpu.make_async_copy(wd_hbm.at[e], wd_buf.at[slot], w_sem.at[slot, 2]))

    @pl.when(i == 0)
    def _():
        for s in range(_XS_SLOTS - 1):
            @pl.when(s < nused)
            def _():
                xs_copy(s, s).start()
        for s in range(_W_AHEAD):
            @pl.when(s < nexp)
            def _():
                for cp in w_copies(s, s):
                    cp.start()

    def fetch(b):
        ahead = b + _XS_SLOTS - 1

        @pl.when(ahead < nused)
        def _():
            xs_copy(ahead, ahead % _XS_SLOTS).start()

        j = ord_ref[b]

        @pl.when(first_ref[b] == 1)
        def _():
            for cp in w_copies(j, j % _W_SLOTS):
                cp.wait()
            nxt = j + _W_AHEAD

            @pl.when(nxt < nexp)
            def _():
                for cp in w_copies(nxt, nxt % _W_SLOTS):
                    cp.start()

        xs_copy(b, b % _XS_SLOTS).wait()

    def gate_up(b):
        ws = ord_ref[b] % _W_SLOTS
        lo, hi = _unpack_row_words(xs_buf[b % _XS_SLOTS])
        g = (jnp.dot(lo, wg_buf[ws, 0:_ROW_WORDS, :], preferred_element_type=F32)
             + jnp.dot(hi, wg_buf[ws, _ROW_WORDS:, :], preferred_element_type=F32))
        u = (jnp.dot(lo, wu_buf[ws, 0:_ROW_WORDS, :], preferred_element_type=F32)
             + jnp.dot(hi, wu_buf[ws, _ROW_WORDS:, :], preferred_element_type=F32))
        return g * jax.nn.sigmoid(g) * u

    def down(b, act):
        y = jnp.dot(act, wd_buf[ord_ref[b] % _W_SLOTS], preferred_element_type=F32)
        oslot = b % _YS_SLOTS
        ys_buf[oslot] = _pack_row_words(y[:, 0:_ROW_WORDS], y[:, _ROW_WORDS:])
        ys_copy(b, oslot).start()

    @pl.when(jnp.logical_and(i >= _YS_SLOTS + 1, i <= nused))
    def _():
        ys_copy(i - 1 - _YS_SLOTS, (i - 1) % _YS_SLOTS).wait()

    @pl.when(i == 0)
    def _():
        fetch(i)
        act_scr[...] = gate_up(i)

    @pl.when(jnp.logical_and(i >= 1, i < nused))
    def _():
        fetch(i)
        prev = act_scr[...]
        act_scr[...] = gate_up(i)
        down(i - 1, prev)

    @pl.when(i == nused)
    def _():
        down(i - 1, act_scr[...])
        ys_copy(i - 1, (i - 1) % _YS_SLOTS).wait()

        @pl.when(i >= 2)
        def _():
            ys_copy(i - 2, (i - 2) % _YS_SLOTS).wait()


def _experts(first, ordinal, uexp, meta, xs, w_gate, w_up, w_down):
    p = xs.shape[0]
    d = w_gate.shape[1]
    nb = p // BM_EXPERT
    anyspec = pl.BlockSpec(memory_space=pl.ANY)
    return pl.pallas_call(
        _expert_kernel,
        grid_spec=pltpu.PrefetchScalarGridSpec(
            num_scalar_prefetch=4,
            grid=(nb + 1,),
            in_specs=[anyspec, anyspec, anyspec, anyspec],
            out_specs=anyspec,
            scratch_shapes=[pltpu.VMEM((_XS_SLOTS, BM_EXPERT, _ROW_WORDS), U32),
                            pltpu.VMEM((_YS_SLOTS, BM_EXPERT, _ROW_WORDS), U32),
                            pltpu.VMEM((_W_SLOTS, d, EXPERT_DIM), F32),
                            pltpu.VMEM((_W_SLOTS, d, EXPERT_DIM), F32),
                            pltpu.VMEM((_W_SLOTS, EXPERT_DIM, d), F32),
                            pltpu.VMEM((BM_EXPERT, EXPERT_DIM), F32),
                            pltpu.SemaphoreType.DMA((_XS_SLOTS,)), pltpu.SemaphoreType.DMA((_YS_SLOTS,)),
                            pltpu.SemaphoreType.DMA((_W_SLOTS, 3))]),
        out_shape=jax.ShapeDtypeStruct((p, _ROW_WORDS), U32),
        compiler_params=_params(("arbitrary",)),
        name="experts",
    )(first, ordinal, uexp, meta, xs, w_gate, w_up, w_down)


SC_GATHER_ROWS = 64


def _gather_rows_sc(table, idx):
    n = idx.shape[0]
    width = table.shape[1]
    rows = SC_GATHER_ROWS
    per_worker, active = _sc_worker_split(n // rows)
    assert per_worker % 2 == 0 or per_worker == 1
    mesh = plsc.VectorSubcoreMesh(core_axis_name="c", subcore_axis_name="s")

    @functools.partial(
        pl.kernel, mesh=mesh, out_type=jax.ShapeDtypeStruct((n, width), table.dtype),
        scratch_types=[pltpu.VMEM((per_worker, rows), I32), pltpu.VMEM((2, rows, width), table.dtype),
                       pltpu.SemaphoreType.DMA((2,))])
    def gather(table_hbm, idx_hbm, out_hbm, idx_v, rows_v, sem):
        wid = lax.axis_index("s") * SC_CORES + lax.axis_index("c")

        def fetch(c, b):
            return pltpu.make_async_copy(table_hbm.at[idx_v.at[c]], rows_v.at[b], sem.at[b])

        @pl.when(wid < active)
        def _():
            first = wid * per_worker
            pltpu.sync_copy(idx_hbm.at[pl.ds(first, per_worker)], idx_v)
            fetch(0, 0).start()

            @pl.loop(0, per_worker, step=2)
            def _(c):
                for b in range(min(2, per_worker)):
                    cur = c + b

                    @pl.when(cur + 1 < per_worker)
                    def _():
                        fetch(cur + 1, 1 - b).start()

                    fetch(cur, b).wait()
                    pltpu.sync_copy(rows_v.at[b], out_hbm.at[pl.ds((first + cur) * rows, rows)])

    return gather(table, idx.reshape(n // rows, rows))


def _combine_kernel(wts_ref, yg_ref, h2_ref, x1_ref, mod_ref, wsg_ref, wsu_ref, wsd_ref, gpost_ref, *rest):
    o_ref = rest[-1]
    h2 = h2_ref[...]
    g = jnp.dot(h2, wsg_ref[...], preferred_element_type=F32)
    u = jnp.dot(h2, wsu_ref[...], preferred_element_type=F32)
    f = jnp.dot((g * jax.nn.sigmoid(g) * u).astype(BF16), wsd_ref[...], preferred_element_type=F32)

    wts = wts_ref[...]
    los = [f[:, sl * LANES:(sl + 1) * LANES] for sl in range(_ROW_SLABS)]
    his = [f[:, _ROW_WORDS + sl * LANES:_ROW_WORDS + (sl + 1) * LANES] for sl in range(_ROW_SLABS)]
    for k in range(TOP_K):
        wk = wts[:, k:k + 1]
        for sl in range(_ROW_SLABS):
            lo, hi = _unpack_row_words(yg_ref[k, :, sl * LANES:(sl + 1) * LANES])
            los[sl] = los[sl] + wk * lo
            his[sl] = his[sl] + wk * hi
    f = jnp.concatenate(los + his, axis=1)
    g2 = mod_ref[0, 5:6, :]
    o_ref[...] = x1_ref[...] + g2 * _rms(f, gpost_ref[...])


def _combine(wts_t, yg, h2, x1, mod3, wsg, wsu, wsd, gpost, seq, first_tile, partial_out):
    t, d = x1.shape
    tm = min(TM_COMBINE, seq)
    tpb = seq // tm
    full = lambda a: pl.BlockSpec(a.shape, lambda i: (0,) * a.ndim)
    row = lambda w: pl.BlockSpec((tm, w), lambda i: (i + first_tile, 0))
    args = [wts_t, yg, h2, x1, mod3, wsg, wsu, wsd, gpost]
    in_specs = [row(TOP_K), pl.BlockSpec((TOP_K, tm, _ROW_WORDS), lambda i: (0, i, 0)), row(d), row(d),
                pl.BlockSpec((1, 6, d), lambda i: ((i + first_tile) // tpb, 0, 0)),
                full(wsg), full(wsu), full(wsd), full(gpost)]
    aliases = {}
    if partial_out is not None:
        aliases = {len(args): 0}
        args.append(partial_out)
        in_specs.append(pl.BlockSpec(memory_space=pl.ANY))
    return pl.pallas_call(
        _combine_kernel,
        grid=(yg.shape[1] // tm,),
        in_specs=in_specs,
        out_specs=row(d),
        out_shape=jax.ShapeDtypeStruct((t, d), F32),
        input_output_aliases=aliases,
        compiler_params=_params(("arbitrary",)),
        name="combine",
    )(*args)


def _pack_weights(w_in, w_uq, w_ukv):
    d = w_in.shape[0]
    half = QK_ROPE_DIM // 2
    z = lambda n, c: jnp.zeros((n, c), F32)
    o = Q_LORA_RANK + KV_LORA_RANK
    kr = w_in[:, o:o + QK_ROPE_DIM]
    kr_grp = jnp.concatenate([z(d, QK_NOPE_DIM), kr, z(d, HEAD_PAD - QK_NOPE_DIM - QK_ROPE_DIM)], axis=1)
    kr_rot = jnp.concatenate([z(d, QK_NOPE_DIM), -kr[:, half:], kr[:, :half],
                              z(d, HEAD_PAD - QK_NOPE_DIM - QK_ROPE_DIM)], axis=1)
    win_p = jnp.concatenate([w_in[:, :o], kr_grp, kr_rot, w_in[:, o + QK_ROPE_DIM:]], axis=1)

    scale = float(QK_NOPE_DIM + QK_ROPE_DIM) ** -0.5 * float(np.log2(np.e))
    r = Q_LORA_RANK
    qd = QK_NOPE_DIM + QK_ROPE_DIM
    q_grp, q_rot = [], []
    for h in range(MLA_HEADS):
        nope = w_uq[:, h * qd:h * qd + QK_NOPE_DIM]
        rope = w_uq[:, h * qd + QK_NOPE_DIM:(h + 1) * qd]
        pad = z(r, HEAD_PAD - qd)
        q_grp.append(jnp.concatenate([nope, rope, pad], axis=1))
        q_rot.append(jnp.concatenate([z(r, QK_NOPE_DIM), -rope[:, half:], rope[:, :half], pad], axis=1))
    wuq_p = jnp.concatenate(q_grp + q_rot, axis=1) * scale

    c = KV_LORA_RANK
    kd = QK_NOPE_DIM + V_HEAD_DIM
    k_grp, v_grp = [], []
    for h in range(MLA_HEADS):
        k_grp.append(jnp.concatenate([w_ukv[:, h * kd:h * kd + QK_NOPE_DIM], z(c, HEAD_PAD - QK_NOPE_DIM)], axis=1))
        v_grp.append(jnp.concatenate([w_ukv[:, h * kd + QK_NOPE_DIM:(h + 1) * kd], z(c, HEAD_PAD - V_HEAD_DIM)], axis=1))
    wukv_p = jnp.concatenate(k_grp + v_grp, axis=1)
    return win_p.astype(BF16), wuq_p.astype(BF16), wukv_p.astype(BF16)


def _rope_inputs(positions):
    inv = 1.0 / (ROPE_THETA ** (jnp.arange(0, QK_ROPE_DIM, 2, dtype=F32) / QK_ROPE_DIM))
    freq = jnp.tile(inv, LANES // inv.shape[0]).reshape(1, LANES)
    pos = jnp.repeat(positions.astype(F32).reshape(-1, _ROPE_PACK), QK_ROPE_DIM, axis=1)
    return pos, freq


def _layer(x2, c, pos, freq, batch, seq, w_ada, b_ada, g_pre_mix, w_in, g_q_lat, w_uq, g_kv_lat, w_ukv,
           w_conv, g_attn_out, g_conv_out, w_out, g_post_mix, g_pre_ffn, w_router, b_router,
           w_gate, w_up, w_down, w_sh_gate, w_sh_up, w_sh_down, g_post_ffn):
    t, d = x2.shape
    r1 = lambda a: a.reshape(1, -1)

    c_pad = jnp.zeros((SUBLANES, d), F32).at[:batch].set(c)
    mod = _ada(c_pad, w_ada, r1(b_ada))[:batch]
    mod3 = mod.reshape(batch, 6, d)

    win_p, wuq_p, wukv_p = _pack_weights(w_in, w_uq, w_ukv)
    vone = jnp.zeros((1, HEAD_PAD), F32).at[0, V_HEAD_DIM].set(1.0)
    q, k, v, yc = _mix_in(x2, mod3, r1(g_pre_mix), win_p, r1(g_q_lat), wuq_p, r1(g_kv_lat), wukv_p,
                          vone, w_conv, r1(g_conv_out), pos, freq, seq)
    attn = _attention(q, k, v, batch, seq)
    x1, h2, h2p, idx, wts, rank, cnt = _mix_out(
        attn, yc, x2, mod3, r1(g_attn_out), w_out.astype(BF16), r1(g_post_mix), r1(g_pre_ffn),
        w_router.T, b_router.reshape(-1, 1), seq)

    counts = cnt[:, 0].astype(I32)
    padded = ((counts + BM_EXPERT - 1) // BM_EXPERT) * BM_EXPERT
    pad_end = jnp.cumsum(padded)
    pad_start = pad_end - padded
    m = t * TOP_K
    nb = (m + N_EXPERTS * (BM_EXPERT - 1)) // BM_EXPERT
    nused = pad_end[-1] // BM_EXPERT
    bidx = jnp.arange(nb, dtype=I32)
    blk_exp = jnp.sum((pad_end[None, :] <= (bidx * BM_EXPERT)[:, None]).astype(I32), axis=1)
    first = ((bidx < nused) & ((bidx == 0) | (blk_exp != jnp.roll(blk_exp, 1)))).astype(I32)
    ordinal = jnp.maximum(jnp.cumsum(first) - 1, 0).astype(I32)
    seen = jnp.cumsum((counts > 0).astype(I32))
    uexp = jnp.minimum(jnp.sum((seen[None, :] <= jnp.arange(N_EXPERTS, dtype=I32)[:, None]).astype(I32), axis=1),
                       N_EXPERTS - 1).astype(I32)
    meta = jnp.stack([nused, seen[-1]]).astype(I32)

    dest = _dest(idx, rank, pad_start.astype(F32).reshape(-1, 1))
    xs = _scatter_rows_sc(h2p, dest.reshape(TOP_K, -1, SC_ROWS).transpose(1, 0, 2), nb * BM_EXPERT)
    xs = _padfill((pad_start + counts).astype(I32), (padded - counts).astype(I32), xs)
    ys = _experts(first, ordinal, uexp, meta, xs, w_gate, w_up, w_down)
    tm_c = min(TM_COMBINE, seq)
    half_rows = (t // 2) * TOP_K
    even_split = half_rows % (2 * SC_GATHER_ROWS * SC_CORES * SC_SUBCORES) == 0
    parts = 2 if t % (2 * tm_c) == 0 and even_split else 1
    tp = t // parts
    wsg, wsu, wsd = w_sh_gate.astype(BF16), w_sh_up.astype(BF16), w_sh_down.astype(BF16)
    out = None
    for part in range(parts):
        part_dest = dest[:, part * tp:(part + 1) * tp].reshape(-1)
        yg = _gather_rows_sc(ys, part_dest).reshape(TOP_K, tp, _ROW_WORDS)
        out = _combine(wts.T, yg, h2, x1, mod3, wsg, wsu, wsd, r1(g_post_ffn), seq, part * tp // tm_c, out)
    return out


def kernel(x, c, positions, w_ada, b_ada, g_pre_mix, w_in, g_q_lat, w_uq, g_kv_lat, w_ukv, w_conv, g_attn_out, g_conv_out, w_out, g_post_mix, g_pre_ffn, w_router, b_router, w_gate, w_up, w_down, w_sh_gate, w_sh_up, w_sh_down, g_post_ffn):
    batch, seq, d = x.shape
    pos, freq = _rope_inputs(positions)
    x2 = x.reshape(batch * seq, d)
    for l in range(w_ada.shape[0]):
        x2 = _layer(x2, c, pos, freq, batch, seq, w_ada[l], b_ada[l], g_pre_mix[l], w_in[l], g_q_lat[l],
                    w_uq[l], g_kv_lat[l], w_ukv[l], w_conv[l], g_attn_out[l], g_conv_out[l], w_out[l],
                    g_post_mix[l], g_pre_ffn[l], w_router[l], b_router[l], w_gate[l], w_up[l], w_down[l],
                    w_sh_gate[l], w_sh_up[l], w_sh_down[l], g_post_ffn[l])
    return x2.reshape(batch, seq, d)
```

```python
import functools

import jax
import jax.numpy as jnp
import numpy as np
from jax import lax
from jax.experimental import pallas as pl
from jax.experimental.pallas import tpu as pltpu
from jax.experimental.pallas import tpu_sc as plsc

F32 = jnp.float32
BF16 = jnp.bfloat16
I32 = jnp.int32
U32 = jnp.uint32

CHUNK = 64
MLA_HEADS = 8
QK_NOPE_DIM = 64
QK_ROPE_DIM = 32
V_HEAD_DIM = 64
Q_LORA_RANK = 384
KV_LORA_RANK = 256
ROPE_THETA = 10000.0
CONV_WIDTH = 3
N_EXPERTS = 256
TOP_K = 8
N_EXPERT_GROUPS = 8
TOPK_GROUPS = 4
EXPERT_DIM = 256
ROUTED_SCALE = 2.5
EPS = 1e-6

LANES = 128
SUBLANES = 8
HEAD_PAD = LANES
VMEM_LIMIT_BYTES = 56 * 1024 * 1024

TM_IN = 1024
TQ_ATTN = 512
TM_OUT = 1024
TM_DEST = 2048
BM_EXPERT = 256
TM_COMBINE = 512

NEG_INF = float("-inf")


def _rms(x, g):
    return x * lax.rsqrt(jnp.mean(x * x, axis=-1, keepdims=True) + EPS) * g


_HI_MASK = np.uint32(0xFFFF0000)
_ROW_WORDS = 512
_ROW_SLABS = _ROW_WORDS // LANES


def _pack_row_words(lo, hi):
    lo_w = lax.bitcast_convert_type(lo.astype(BF16).astype(F32), U32) >> 16
    hi_w = lax.bitcast_convert_type(hi.astype(BF16).astype(F32), U32) & _HI_MASK
    return lo_w | hi_w


def _unpack_row_words(w):
    return (lax.bitcast_convert_type(w << 16, F32), lax.bitcast_convert_type(w & _HI_MASK, F32))


def _params(sem):
    return pltpu.CompilerParams(dimension_semantics=sem, vmem_limit_bytes=VMEM_LIMIT_BYTES)


def _ada_kernel(c_ref, w_ref, b_ref, o_ref):
    c = c_ref[...]
    s = c * jax.nn.sigmoid(c)
    o_ref[...] = jnp.dot(s, w_ref[...], preferred_element_type=F32,
                         precision=lax.Precision.HIGHEST) + b_ref[...]


def _ada(c_pad, w, b):
    rows, d = c_pad.shape
    n = w.shape[1]
    tn = 1536
    return pl.pallas_call(
        _ada_kernel,
        grid=(n // tn,),
        in_specs=[pl.BlockSpec((rows, d), lambda j: (0, 0)),
                  pl.BlockSpec((d, tn), lambda j: (0, j)),
                  pl.BlockSpec((1, tn), lambda j: (0, j))],
        out_specs=pl.BlockSpec((rows, tn), lambda j: (0, j)),
        out_shape=jax.ShapeDtypeStruct((rows, n), F32),
        compiler_params=_params(("arbitrary",)),
        name="ada",
    )(c_pad, w, b)


_CQ0, _CQ1 = 0, Q_LORA_RANK
_CKV0, _CKV1 = _CQ1, _CQ1 + KV_LORA_RANK
_KR0, _KR1 = _CKV1, _CKV1 + 2 * HEAD_PAD
_CONV_DIM = 512
_GB0 = _KR1
_GC0 = _GB0 + _CONV_DIM
_XV0 = _GC0 + _CONV_DIM
_WIN_COLS = _XV0 + _CONV_DIM
_QW = MLA_HEADS * HEAD_PAD
_ROPE_PACK = LANES // QK_ROPE_DIM


def _mix_in_kernel(tiles_per_batch, x_ref, mod_ref, gpre_ref, win_ref, gq_ref, wuq_ref, gkv_ref,
                   wukv_ref, vone_ref, wconv_ref, gconv_ref, pos_ref, freq_ref,
                   q_ref, k_ref, v_ref, yc_ref, h_scr, u_scr, cos_scr, sin_scr):
    i = pl.program_id(0)
    tm = x_ref.shape[0]
    sh1 = mod_ref[0, 0:1, :]
    sc1 = mod_ref[0, 1:2, :]
    h = _rms(x_ref[...], gpre_ref[...]) * (1.0 + sc1) + sh1
    h_scr[...] = h.astype(BF16)
    ang4 = pos_ref[...] * freq_ref[...]
    cos4 = jnp.cos(ang4)
    sin4 = jnp.sin(ang4)
    lane = lax.broadcasted_iota(I32, ang4.shape, 1)
    on_rope = jnp.logical_and(lane >= QK_NOPE_DIM, lane < QK_NOPE_DIM + QK_ROPE_DIM)
    for g in range(_ROPE_PACK):
        shift = (QK_NOPE_DIM - QK_ROPE_DIM * g) % LANES
        cg = cos4 if shift == 0 else pltpu.roll(cos4, shift, axis=1)
        sg = sin4 if shift == 0 else pltpu.roll(sin4, shift, axis=1)
        cos_scr[pl.ds(g, tm // _ROPE_PACK, stride=_ROPE_PACK), :] = jnp.where(on_rope, cg, 1.0)
        sin_scr[pl.ds(g, tm // _ROPE_PACK, stride=_ROPE_PACK), :] = jnp.where(on_rope, sg, 0.0)
    cos = cos_scr[...]
    sin = sin_scr[...]

    cq = jnp.dot(h_scr[...], win_ref[:, _CQ0:_CQ1], preferred_element_type=F32)
    cqn = _rms(cq, gq_ref[...]).astype(BF16)
    qq = jnp.dot(cqn, wuq_ref[...], preferred_element_type=F32)
    for hd in range(MLA_HEADS):
        lo = hd * HEAD_PAD
        qh = qq[:, lo:lo + HEAD_PAD] * cos + qq[:, _QW + lo:_QW + lo + HEAD_PAD] * sin
        q_ref[:, lo:lo + HEAD_PAD] = qh.astype(BF16)

    ckv = jnp.dot(h_scr[...], win_ref[:, _CKV0:_CKV1], preferred_element_type=F32)
    ckvn = _rms(ckv, gkv_ref[...]).astype(BF16)
    kv = jnp.dot(ckvn, wukv_ref[...], preferred_element_type=F32)
    krr = jnp.dot(h_scr[...], win_ref[:, _KR0:_KR1], preferred_element_type=F32)
    kr = krr[:, 0:HEAD_PAD] * cos + krr[:, HEAD_PAD:2 * HEAD_PAD] * sin
    vone = vone_ref[...]
    for hd in range(MLA_HEADS):
        lo = hd * HEAD_PAD
        k_ref[:, lo:lo + HEAD_PAD] = (kv[:, lo:lo + HEAD_PAD] + kr).astype(BF16)
        v_ref[:, lo:lo + HEAD_PAD] = (kv[:, _QW + lo:_QW + lo + HEAD_PAD] + vone).astype(BF16)

    gb = jnp.dot(h_scr[...], win_ref[:, _GB0:_GC0], preferred_element_type=F32)
    gc = jnp.dot(h_scr[...], win_ref[:, _GC0:_XV0], preferred_element_type=F32)
    xv = jnp.dot(h_scr[...], win_ref[:, _XV0:_WIN_COLS], preferred_element_type=F32)
    u = gc * xv
    prev = u_scr[tm:tm + SUBLANES, :]
    first = (i % tiles_per_batch) == 0
    u_scr[0:SUBLANES, :] = jnp.where(first, jnp.zeros_like(prev), prev)
    u_scr[SUBLANES:tm + SUBLANES, :] = u
    um1 = u_scr[SUBLANES - 1:tm + SUBLANES - 1, :]
    um2 = u_scr[SUBLANES - 2:tm + SUBLANES - 2, :]
    assert wconv_ref.shape[0] == CONV_WIDTH
    conv = wconv_ref[0:1, :] * um2 + wconv_ref[1:2, :] * um1 + wconv_ref[2:3, :] * u
    yc_ref[...] = _rms(gb * conv, gconv_ref[...]).astype(BF16)


def _mix_in(x2, mod3, gpre, win_p, gq, wuq_p, gkv, wukv_p, vone, wconv, gconv, pos, freq, seq):
    t, d = x2.shape
    tm = min(TM_IN, seq)
    tpb = seq // tm
    full = lambda a: pl.BlockSpec(a.shape, lambda i: (0,) * a.ndim)
    row = lambda w: pl.BlockSpec((tm, w), lambda i: (i, 0))
    return pl.pallas_call(
        functools.partial(_mix_in_kernel, tpb),
        grid=(t // tm,),
        in_specs=[row(d),
                  pl.BlockSpec((1, 6, d), lambda i: (i // tpb, 0, 0)),
                  full(gpre), full(win_p), full(gq), full(wuq_p), full(gkv), full(wukv_p),
                  full(vone), full(wconv), full(gconv),
                  pl.BlockSpec((tm // _ROPE_PACK, LANES), lambda i: (i, 0)), full(freq)],
        out_specs=[row(_QW), row(_QW), row(_QW), row(_CONV_DIM)],
        out_shape=[jax.ShapeDtypeStruct((t, _QW), BF16), jax.ShapeDtypeStruct((t, _QW), BF16),
                   jax.ShapeDtypeStruct((t, _QW), BF16), jax.ShapeDtypeStruct((t, _CONV_DIM), BF16)],
        scratch_shapes=[pltpu.VMEM((tm, d), BF16), pltpu.VMEM((tm + SUBLANES, _CONV_DIM), F32),
                        pltpu.VMEM((tm, HEAD_PAD), F32), pltpu.VMEM((tm, HEAD_PAD), F32)],
        compiler_params=_params(("arbitrary",)),
        name="mix_in",
    )(x2, mod3, gpre, win_p, gq, wuq_p, gkv, wukv_p, vone, wconv, gconv, pos, freq)


_HEADS_PER_STEP = 2


def _attn_kernel(tq, q_ref, k_ref, v_ref, o_ref, s_scr, mrun_scr, mb_scr, acc_scr):
    tk = tq
    nq = q_ref.shape[0] // tq
    lane_groups = tk // LANES
    heads = range(_HEADS_PER_STEP)
    lanes = [slice(hh * HEAD_PAD, (hh + 1) * HEAD_PAD) for hh in heads]

    def tile_max(s):
        m = s[:, 0:LANES]
        for g in range(1, lane_groups):
            m = jnp.maximum(m, s[:, g * LANES:(g + 1) * LANES])
        return m

    def scores(hh, qi, kv):
        off = pl.multiple_of(kv * tk, tk)
        return lax.dot_general(q_ref[qi * tq:(qi + 1) * tq, lanes[hh]], k_ref[pl.ds(off, tk), lanes[hh]],
                               (((1,), (1,)), ((), ())), preferred_element_type=F32)

    rc = lax.broadcasted_iota(I32, (tq, tk), 0) // CHUNK
    cc = lax.broadcasted_iota(I32, (tq, tk), 1) // CHUNK

    def diagonal(qi):
        for hh in heads:
            s = jnp.where(cc <= rc, scores(hh, qi, qi), NEG_INF)
            s_scr[hh, qi] = s
            m_row = jnp.max(jnp.maximum(mrun_scr[hh], tile_max(s)), axis=1, keepdims=True)
            mb_scr[hh] = jnp.broadcast_to(m_row, (tq, LANES))

    mrun_scr[...] = jnp.full(mrun_scr.shape, NEG_INF, F32)
    diagonal(0)
    for qi in range(nq):
        has_next = qi + 1 < nq
        acc_scr[...] = jnp.zeros(acc_scr.shape, F32)
        if has_next:
            mrun_scr[...] = jnp.full(mrun_scr.shape, NEG_INF, F32)

        def body(kv, carry, qi=qi, has_next=has_next):
            off = pl.multiple_of(kv * tk, tk)
            for hh in heads:
                mb = mb_scr[hh]
                p = jnp.concatenate(
                    [jnp.exp2(s_scr[hh, kv, :, g * LANES:(g + 1) * LANES] - mb) for g in range(lane_groups)],
                    axis=1).astype(BF16)
                acc_scr[hh] += jnp.dot(p, v_ref[pl.ds(off, tk), lanes[hh]], preferred_element_type=F32)
                if has_next:
                    s = scores(hh, qi + 1, kv)
                    s_scr[hh, kv] = s
                    mrun_scr[hh] = jnp.maximum(mrun_scr[hh], tile_max(s))
            return carry

        lax.fori_loop(0, qi + 1, body, 0, unroll=4)
        for hh in heads:
            acc = acc_scr[hh]
            o = acc[:, 0:V_HEAD_DIM] / acc[:, V_HEAD_DIM:V_HEAD_DIM + 1]
            o_ref[qi * tq:(qi + 1) * tq, hh * V_HEAD_DIM:(hh + 1) * V_HEAD_DIM] = o.astype(BF16)
        if has_next:
            diagonal(qi + 1)


def _attention(q, k, v, batch, seq):
    t = q.shape[0]
    tq = min(TQ_ATTN, seq)
    nq = seq // tq
    hw = _HEADS_PER_STEP * HEAD_PAD
    ow = _HEADS_PER_STEP * V_HEAD_DIM
    blk = lambda w: pl.BlockSpec((seq, w), lambda b, j: (b, j))
    return pl.pallas_call(
        functools.partial(_attn_kernel, tq),
        grid=(batch, MLA_HEADS // _HEADS_PER_STEP),
        in_specs=[blk(hw), blk(hw), blk(hw)],
        out_specs=blk(ow),
        out_shape=jax.ShapeDtypeStruct((t, MLA_HEADS * V_HEAD_DIM), BF16),
        scratch_shapes=[pltpu.VMEM((_HEADS_PER_STEP, nq, tq, tq), F32),
                        pltpu.VMEM((_HEADS_PER_STEP, tq, LANES), F32),
                        pltpu.VMEM((_HEADS_PER_STEP, tq, LANES), F32),
                        pltpu.VMEM((_HEADS_PER_STEP, tq, HEAD_PAD), F32)],
        compiler_params=_params(("arbitrary", "arbitrary")),
        name="attn",
    )(q, k, v)


_GROUP_SIZE = N_EXPERTS // N_EXPERT_GROUPS
_BIG = 1.0e9


def _mix_out_kernel(attn_ref, yc_ref, x_ref, mod_ref, gattn_ref, wout_ref, gpost_ref, gpre2_ref,
                    wrt_ref, br_ref, x1_ref, h2_ref, h2p_ref, idx_ref, wts_ref, rank_ref, cnt_ref,
                    carry_scr, scores_scr, sel_scr):
    i = pl.program_id(0)
    n_tiles = pl.num_programs(0) - 1

    @pl.when(i == 0)
    def _():
        carry_scr[...] = jnp.zeros(carry_scr.shape, F32)
        scores, sel = _project_tile(attn_ref, yc_ref, x_ref, mod_ref, gattn_ref, wout_ref, gpost_ref,
                                    gpre2_ref, wrt_ref, br_ref, x1_ref, h2_ref, h2p_ref)
        scores_scr[0] = scores
        sel_scr[0] = sel

    @pl.when(jnp.logical_and(i >= 1, i < n_tiles))
    def _():
        prev_scores = scores_scr[(i - 1) % 2]
        prev_sel = sel_scr[(i - 1) % 2]
        scores, sel = _project_tile(attn_ref, yc_ref, x_ref, mod_ref, gattn_ref, wout_ref, gpost_ref,
                                    gpre2_ref, wrt_ref, br_ref, x1_ref, h2_ref, h2p_ref)
        scores_scr[i % 2] = scores
        sel_scr[i % 2] = sel
        _route_tile(prev_scores, prev_sel, idx_ref, wts_ref, rank_ref, cnt_ref, carry_scr)

    @pl.when(i == n_tiles)
    def _():
        _route_tile(scores_scr[(i - 1) % 2], sel_scr[(i - 1) % 2], idx_ref, wts_ref, rank_ref, cnt_ref,
                    carry_scr)


def _project_tile(attn_ref, yc_ref, x_ref, mod_ref, gattn_ref, wout_ref, gpost_ref, gpre2_ref,
                  wrt_ref, br_ref, x1_ref, h2_ref, h2p_ref):
    half = attn_ref.shape[1]
    an = _rms(attn_ref[...].astype(F32), gattn_ref[...]).astype(BF16)
    mix = (jnp.dot(an, wout_ref[0:half, :], preferred_element_type=F32)
           + jnp.dot(yc_ref[...], wout_ref[half:, :], preferred_element_type=F32))
    g1 = mod_ref[0, 2:3, :]
    sh2 = mod_ref[0, 3:4, :]
    sc2 = mod_ref[0, 4:5, :]
    x1 = x_ref[...] + g1 * _rms(mix, gpost_ref[...])
    x1_ref[...] = x1
    h2 = _rms(x1, gpre2_ref[...]) * (1.0 + sc2) + sh2
    h2_ref[...] = h2.astype(BF16)
    h2p_ref[...] = _pack_row_words(h2[:, 0:_ROW_WORDS], h2[:, _ROW_WORDS:])

    logits = lax.dot_general(wrt_ref[...], h2, (((1,), (1,)), ((), ())),
                             preferred_element_type=F32, precision=lax.Precision.HIGHEST)
    scores = jax.nn.sigmoid(logits)
    return scores, scores + br_ref[...]


def _route_tile(scores, sel, idx_ref, wts_ref, rank_ref, cnt_ref, carry_scr):
    tm = scores.shape[1]
    row = lax.broadcasted_iota(I32, (N_EXPERTS, tm), 0).astype(F32)

    gscore = []
    rw = lax.broadcasted_iota(I32, (_GROUP_SIZE, tm), 0).astype(F32)
    for g in range(N_EXPERT_GROUPS):
        blk = sel[g * _GROUP_SIZE:(g + 1) * _GROUP_SIZE, :]
        m1 = jnp.max(blk, axis=0, keepdims=True)
        i1 = jnp.min(jnp.where(blk == m1, rw, _BIG), axis=0, keepdims=True)
        m2 = jnp.max(jnp.where(rw == i1, NEG_INF, blk), axis=0, keepdims=True)
        gscore.append(m1 + m2)

    gkeep = [jnp.zeros((1, tm), F32) for _ in range(N_EXPERT_GROUPS)]
    for _ in range(TOPK_GROUPS):
        mg = functools.reduce(jnp.maximum, gscore)
        ig = functools.reduce(jnp.minimum, [jnp.where(gscore[g] == mg, float(g), _BIG)
                                            for g in range(N_EXPERT_GROUPS)])
        for g in range(N_EXPERT_GROUPS):
            hit = ig == float(g)
            gkeep[g] = jnp.where(hit, 1.0, gkeep[g])
            gscore[g] = jnp.where(hit, NEG_INF, gscore[g])
    n_slabs = N_EXPERTS // SUBLANES
    slabs_per_group = _GROUP_SIZE // SUBLANES
    sub = lax.broadcasted_iota(I32, (SUBLANES, tm), 0).astype(F32)
    first_rows = [jnp.where(gkeep[j // slabs_per_group] > 0.0, sel[j * SUBLANES:(j + 1) * SUBLANES, :], NEG_INF)
                  for j in range(n_slabs)]
    cur_rows = list(first_rows)
    krow = lax.broadcasted_iota(I32, (TOP_K, tm), 0)
    idx_rows = []
    idx_f = jnp.zeros((TOP_K, tm), F32)
    sc_k = jnp.zeros((TOP_K, tm), F32)
    sc_sum = jnp.zeros((1, tm), F32)
    prev = None
    for k in range(TOP_K):
        best = jnp.full((SUBLANES, tm), NEG_INF, F32)
        best_slab = jnp.zeros((SUBLANES, tm), F32)
        best_score = jnp.zeros((SUBLANES, tm), F32)
        for j in range(n_slabs):
            if prev is not None:
                cur_rows[j] = jnp.where(sub == prev - float(j * SUBLANES), NEG_INF, cur_rows[j])
            better = cur_rows[j] > best
            best = jnp.where(better, cur_rows[j], best)
            best_slab = jnp.where(better, float(j), best_slab)
            best_score = jnp.where(better, scores[j * SUBLANES:(j + 1) * SUBLANES, :], best_score)
        best_idx = best_slab * float(SUBLANES) + sub
        m = jnp.max(best, axis=0, keepdims=True)
        ik = jnp.min(jnp.where(best == m, best_idx, _BIG), axis=0, keepdims=True)
        sk = jnp.sum(jnp.where(best_idx == ik, best_score, 0.0), axis=0, keepdims=True)
        prev = ik
        idx_rows.append(ik)
        idx_f = jnp.where(krow == k, ik, idx_f)
        sc_k = jnp.where(krow == k, sk, sc_k)
        sc_sum = sc_sum + sk
    wts_ref[...] = sc_k / sc_sum * ROUTED_SCALE
    idx_ref[...] = idx_f.astype(I32)
    onehot = jnp.concatenate(
        [jnp.where(jnp.where(sub == prev - float(j * SUBLANES), NEG_INF, cur_rows[j]) != first_rows[j], 1.0, 0.0)
         for j in range(n_slabs)], axis=0)

    tri = (lax.broadcasted_iota(I32, (tm, tm), 0) < lax.broadcasted_iota(I32, (tm, tm), 1))
    excl = jnp.dot(onehot.astype(BF16), tri.astype(BF16), preferred_element_type=F32)
    rank_e = carry_scr[:, 0:1] + excl
    rank_k = jnp.zeros((TOP_K, tm), F32)
    for k in range(TOP_K):
        hit = row == idx_rows[k]
        rk = jnp.sum(jnp.where(hit, rank_e, 0.0), axis=0, keepdims=True)
        rank_k = jnp.where(krow == k, rk, rank_k)
    rank_ref[...] = rank_k.astype(I32)
    carry_scr[...] = carry_scr[...] + jnp.sum(onehot, axis=1, keepdims=True)
    cnt_ref[...] = carry_scr[...]


def _mix_out(attn, yc, x2, mod3, gattn, wout, gpost, gpre2, wrt, br, seq):
    t, d = x2.shape
    tm = min(TM_OUT, seq)
    tpb = seq // tm
    n_tiles = t // tm
    last = n_tiles - 1
    full = lambda a: pl.BlockSpec(a.shape, lambda i: (0,) * a.ndim)
    row = lambda w: pl.BlockSpec((tm, w), lambda i: (jnp.minimum(i, last), 0))
    col = pl.BlockSpec((TOP_K, tm), lambda i: (0, jnp.maximum(i - 1, 0)))
    return pl.pallas_call(
        _mix_out_kernel,
        grid=(n_tiles + 1,),
        in_specs=[row(attn.shape[1]), row(yc.shape[1]), row(d),
                  pl.BlockSpec((1, 6, d), lambda i: (jnp.minimum(i, last) // tpb, 0, 0)),
                  full(gattn), full(wout), full(gpost), full(gpre2), full(wrt), full(br)],
        out_specs=[row(d), row(d), row(_ROW_WORDS), col, col, col,
                   pl.BlockSpec((N_EXPERTS, LANES), lambda i: (0, 0))],
        out_shape=[jax.ShapeDtypeStruct((t, d), F32), jax.ShapeDtypeStruct((t, d), BF16),
                   jax.ShapeDtypeStruct((t, _ROW_WORDS), U32),
                   jax.ShapeDtypeStruct((TOP_K, t), I32), jax.ShapeDtypeStruct((TOP_K, t), F32),
                   jax.ShapeDtypeStruct((TOP_K, t), I32),
                   jax.ShapeDtypeStruct((N_EXPERTS, LANES), F32)],
        scratch_shapes=[pltpu.VMEM((N_EXPERTS, LANES), F32), pltpu.VMEM((2, N_EXPERTS, tm), F32),
                        pltpu.VMEM((2, N_EXPERTS, tm), F32)],
        compiler_params=_params(("arbitrary",)),
        name="mix_out",
    )(attn, yc, x2, mod3, gattn, wout, gpost, gpre2, wrt, br)


def _dest_kernel(idx_ref, rank_ref, pstart_ref, dest_ref):
    tm = idx_ref.shape[1]
    row = lax.broadcasted_iota(I32, (N_EXPERTS, tm), 0)
    krow = lax.broadcasted_iota(I32, (TOP_K, tm), 0)
    pstart = pstart_ref[...]
    idx = idx_ref[...]
    out = jnp.zeros((TOP_K, tm), F32)
    for k in range(TOP_K):
        hit = row == idx[k:k + 1, :]
        base = jnp.sum(jnp.where(hit, pstart, 0.0), axis=0, keepdims=True)
        out = jnp.where(krow == k, base, out)
    dest_ref[...] = out.astype(I32) + rank_ref[...]


def _dest(idx, rank, pstart):
    t = idx.shape[1]
    tm = min(TM_DEST, t)
    col = pl.BlockSpec((TOP_K, tm), lambda i: (0, i))
    return pl.pallas_call(
        _dest_kernel,
        grid=(t // tm,),
        in_specs=[col, col, pl.BlockSpec((N_EXPERTS, 1), lambda i: (0, 0))],
        out_specs=col,
        out_shape=jax.ShapeDtypeStruct((TOP_K, t), I32),
        compiler_params=_params(("arbitrary",)),
        name="dest",
    )(idx, rank, pstart)


_PAD_CHUNKS = tuple(BM_EXPERT >> s for s in range(1, BM_EXPERT.bit_length()))


SC_CORES = 2
SC_SUBCORES = 16
SC_ROWS = 128


def _sc_worker_split(n_chunks):
    workers = SC_CORES * SC_SUBCORES
    per_worker = max(1, n_chunks // workers)
    active = n_chunks // per_worker
    assert active * per_worker == n_chunks and active <= workers
    return per_worker, active


def _scatter_rows_sc(rows, idx3d, n_out):
    n, width = rows.shape
    n_chunks = n // SC_ROWS
    per_worker, active = _sc_worker_split(n_chunks)
    mesh = plsc.VectorSubcoreMesh(core_axis_name="c", subcore_axis_name="s")

    @functools.partial(
        pl.kernel, mesh=mesh, out_type=jax.ShapeDtypeStruct((n_out, width), rows.dtype),
        scratch_types=[pltpu.VMEM((TOP_K, SC_ROWS), I32), pltpu.VMEM((SC_ROWS, width), rows.dtype),
                       pltpu.SemaphoreType.DMA])
    def scatter(rows_hbm, idx_hbm, out_hbm, idx_v, rows_v, sem):
        wid = lax.axis_index("s") * SC_CORES + lax.axis_index("c")

        @pl.when(wid < active)
        def _():
            @pl.loop(0, per_worker)
            def _(c):
                chunk = wid * per_worker + c
                pltpu.sync_copy(rows_hbm.at[pl.ds(chunk * SC_ROWS, SC_ROWS)], rows_v)
                pltpu.sync_copy(idx_hbm.at[chunk], idx_v)
                copies = [pltpu.make_async_copy(rows_v, out_hbm.at[idx_v.at[k]], sem) for k in range(TOP_K)]
                for cp in copies:
                    cp.start()
                for cp in copies:
                    cp.wait()

    return scatter(rows, idx3d)


def _padfill_kernel(pad_from_ref, pad_n_ref, xs_in_hbm, xs_hbm, zero_scr, pad_sem):
    del xs_in_hbm
    zero_scr[...] = jnp.zeros(zero_scr.shape, zero_scr.dtype)

    def pad_copies(e, act):
        n = pad_n_ref[e]
        base = pad_from_ref[e]

        def single_rows(start, count):
            for j in range(SUBLANES - 1):
                @pl.when(j < count)
                def _():
                    act(pltpu.make_async_copy(zero_scr.at[pl.ds(0, 1), :],
                                              xs_hbm.at[pl.ds(start + j, 1), :], pad_sem))

        head = jnp.minimum(n, (SUBLANES - (base & (SUBLANES - 1))) & (SUBLANES - 1))
        single_rows(base, head)
        rest = n - head
        mid = base + head
        for rows in _PAD_CHUNKS:
            if rows >= SUBLANES:
                @pl.when((rest & rows) != 0)
                def _():
                    start = pl.multiple_of(mid + (rest & ~(2 * rows - 1)), SUBLANES)
                    act(pltpu.make_async_copy(zero_scr.at[pl.ds(0, rows), :],
                                              xs_hbm.at[pl.ds(start, rows), :], pad_sem))
        single_rows(mid + (rest & ~(SUBLANES - 1)), rest & (SUBLANES - 1))

    def issue_pad(e, carry):
        pad_copies(e, lambda cp: cp.start())
        return carry

    def drain_pad(e, carry):
        pad_copies(e, lambda cp: cp.wait())
        return carry

    lax.fori_loop(0, N_EXPERTS, issue_pad, 0)
    lax.fori_loop(0, N_EXPERTS, drain_pad, 0)


def _padfill(pad_from, pad_n, xs):
    return pl.pallas_call(
        _padfill_kernel,
        grid_spec=pltpu.PrefetchScalarGridSpec(
            num_scalar_prefetch=2,
            grid=(1,),
            in_specs=[pl.BlockSpec(memory_space=pl.ANY)],
            out_specs=pl.BlockSpec(memory_space=pl.ANY),
            scratch_shapes=[pltpu.VMEM((BM_EXPERT // 2, _ROW_WORDS), U32), pltpu.SemaphoreType.DMA]),
        out_shape=jax.ShapeDtypeStruct(xs.shape, xs.dtype),
        input_output_aliases={2: 0},
        compiler_params=_params(("arbitrary",)),
        name="padfill",
    )(pad_from, pad_n, xs)


_XS_SLOTS = 4
_YS_SLOTS = 2
_W_SLOTS = 4
_W_AHEAD = 2


def _expert_kernel(first_ref, ord_ref, uexp_ref, meta_ref, xs_hbm, wg_hbm, wu_hbm, wd_hbm, ys_hbm,
                   xs_buf, ys_buf, wg_buf, wu_buf, wd_buf, act_scr, xs_sem, ys_sem, w_sem):
    i = pl.program_id(0)
    nused = meta_ref[0]
    nexp = meta_ref[1]
    bm = xs_buf.shape[1]

    def xs_copy(b, slot):
        return pltpu.make_async_copy(xs_hbm.at[pl.ds(b * bm, bm), :], xs_buf.at[slot], xs_sem.at[slot])

    def ys_copy(b, slot):
        return pltpu.make_async_copy(ys_buf.at[slot], ys_hbm.at[pl.ds(b * bm, bm), :], ys_sem.at[slot])

    def w_copies(j, slot):
        e = uexp_ref[j]
        return (pltpu.make_async_copy(wg_hbm.at[e], wg_buf.at[slot], w_sem.at[slot, 0]),
                pltpu.make_async_copy(wu_hbm.at[e], wu_buf.at[slot], w_sem.at[slot, 1]),
                pltpu.make_async_copy(wd_hbm.at[e], wd_buf.at[slot], w_sem.at[slot, 2]))

    @pl.when(i == 0)
    def _():
        for s in range(_XS_SLOTS - 1):
            @pl.when(s < nused)
            def _():
                xs_copy(s, s).start()
        for s in range(_W_AHEAD):
            @pl.when(s < nexp)
            def _():
                for cp in w_copies(s, s):
                    cp.start()

    def fetch(b):
        ahead = b + _XS_SLOTS - 1

        @pl.when(ahead < nused)
        def _():
            xs_copy(ahead, ahead % _XS_SLOTS).start()

        j = ord_ref[b]

        @pl.when(first_ref[b] == 1)
        def _():
            for cp in w_copies(j, j % _W_SLOTS):
                cp.wait()
            nxt = j + _W_AHEAD

            @pl.when(nxt < nexp)
            def _():
                for cp in w_copies(nxt, nxt % _W_SLOTS):
                    cp.start()

        xs_copy(b, b % _XS_SLOTS).wait()

    def gate_up(b):
        ws = ord_ref[b] % _W_SLOTS
        lo, hi = _unpack_row_words(xs_buf[b % _XS_SLOTS])
        g = (jnp.dot(lo, wg_buf[ws, 0:_ROW_WORDS, :], preferred_element_type=F32)
             + jnp.dot(hi, wg_buf[ws, _ROW_WORDS:, :], preferred_element_type=F32))
        u = (jnp.dot(lo, wu_buf[ws, 0:_ROW_WORDS, :], preferred_element_type=F32)
             + jnp.dot(hi, wu_buf[ws, _ROW_WORDS:, :], preferred_element_type=F32))
        return g * jax.nn.sigmoid(g) * u

    def down(b, act):
        y = jnp.dot(act, wd_buf[ord_ref[b] % _W_SLOTS], preferred_element_type=F32)
        oslot = b % _YS_SLOTS
        ys_buf[oslot] = _pack_row_words(y[:, 0:_ROW_WORDS], y[:, _ROW_WORDS:])
        ys_copy(b, oslot).start()

    @pl.when(jnp.logical_and(i >= _YS_SLOTS + 1, i <= nused))
    def _():
        ys_copy(i - 1 - _YS_SLOTS, (i - 1) % _YS_SLOTS).wait()

    @pl.when(i == 0)
    def _():
        fetch(i)
        act_scr[...] = gate_up(i)

    @pl.when(jnp.logical_and(i >= 1, i < nused))
    def _():
        fetch(i)
        prev = act_scr[...]
        act_scr[...] = gate_up(i)
        down(i - 1, prev)

    @pl.when(i == nused)
    def _():
        down(i - 1, act_scr[...])
        ys_copy(i - 1, (i - 1) % _YS_SLOTS).wait()

        @pl.when(i >= 2)
        def _():
            ys_copy(i - 2, (i - 2) % _YS_SLOTS).wait()


def _experts(first, ordinal, uexp, meta, xs, w_gate, w_up, w_down):
    p = xs.shape[0]
    d = w_gate.shape[1]
    nb = p // BM_EXPERT
    anyspec = pl.BlockSpec(memory_space=pl.ANY)
    return pl.pallas_call(
        _expert_kernel,
        grid_spec=pltpu.PrefetchScalarGridSpec(
            num_scalar_prefetch=4,
            grid=(nb + 1,),
            in_specs=[anyspec, anyspec, anyspec, anyspec],
            out_specs=anyspec,
            scratch_shapes=[pltpu.VMEM((_XS_SLOTS, BM_EXPERT, _ROW_WORDS), U32),
                            pltpu.VMEM((_YS_SLOTS, BM_EXPERT, _ROW_WORDS), U32),
                            pltpu.VMEM((_W_SLOTS, d, EXPERT_DIM), F32),
                            pltpu.VMEM((_W_SLOTS, d, EXPERT_DIM), F32),
                            pltpu.VMEM((_W_SLOTS, EXPERT_DIM, d), F32),
                            pltpu.VMEM((BM_EXPERT, EXPERT_DIM), F32),
                            pltpu.SemaphoreType.DMA((_XS_SLOTS,)), pltpu.SemaphoreType.DMA((_YS_SLOTS,)),
                            pltpu.SemaphoreType.DMA((_W_SLOTS, 3))]),
        out_shape=jax.ShapeDtypeStruct((p, _ROW_WORDS), U32),
        compiler_params=_params(("arbitrary",)),
        name="experts",
    )(first, ordinal, uexp, meta, xs, w_gate, w_up, w_down)


SC_GATHER_ROWS = 64


def _gather_rows_sc(table, idx):
    n = idx.shape[0]
    width = table.shape[1]
    rows = SC_GATHER_ROWS
    per_worker, active = _sc_worker_split(n // rows)
    assert per_worker % 2 == 0 or per_worker == 1
    mesh = plsc.VectorSubcoreMesh(core_axis_name="c", subcore_axis_name="s")

    @functools.partial(
        pl.kernel, mesh=mesh, out_type=jax.ShapeDtypeStruct((n, width), table.dtype),
        scratch_types=[pltpu.VMEM((per_worker, rows), I32), pltpu.VMEM((2, rows, width), table.dtype),
                       pltpu.SemaphoreType.DMA((2,))])
    def gather(table_hbm, idx_hbm, out_hbm, idx_v, rows_v, sem):
        wid = lax.axis_index("s") * SC_CORES + lax.axis_index("c")

        def fetch(c, b):
            return pltpu.make_async_copy(table_hbm.at[idx_v.at[c]], rows_v.at[b], sem.at[b])

        @pl.when(wid < active)
        def _():
            first = wid * per_worker
            pltpu.sync_copy(idx_hbm.at[pl.ds(first, per_worker)], idx_v)
            fetch(0, 0).start()

            @pl.loop(0, per_worker, step=2)
            def _(c):
                for b in range(min(2, per_worker)):
                    cur = c + b

                    @pl.when(cur + 1 < per_worker)
                    def _():
                        fetch(cur + 1, 1 - b).start()

                    fetch(cur, b).wait()
                    pltpu.sync_copy(rows_v.at[b], out_hbm.at[pl.ds((first + cur) * rows, rows)])

    return gather(table, idx.reshape(n // rows, rows))


def _combine_kernel(wts_ref, yg_ref, h2_ref, x1_ref, mod_ref, wsg_ref, wsu_ref, wsd_ref, gpost_ref, *rest):
    o_ref = rest[-1]
    h2 = h2_ref[...]
    g = jnp.dot(h2, wsg_ref[...], preferred_element_type=F32)
    u = jnp.dot(h2, wsu_ref[...], preferred_element_type=F32)
    f = jnp.dot((g * jax.nn.sigmoid(g) * u).astype(BF16), wsd_ref[...], preferred_element_type=F32)

    wts = wts_ref[...]
    los = [f[:, sl * LANES:(sl + 1) * LANES] for sl in range(_ROW_SLABS)]
    his = [f[:, _ROW_WORDS + sl * LANES:_ROW_WORDS + (sl + 1) * LANES] for sl in range(_ROW_SLABS)]
    for k in range(TOP_K):
        wk = wts[:, k:k + 1]
        for sl in range(_ROW_SLABS):
            lo, hi = _unpack_row_words(yg_ref[k, :, sl * LANES:(sl + 1) * LANES])
            los[sl] = los[sl] + wk * lo
            his[sl] = his[sl] + wk * hi
    f = jnp.concatenate(los + his, axis=1)
    g2 = mod_ref[0, 5:6, :]
    o_ref[...] = x1_ref[...] + g2 * _rms(f, gpost_ref[...])


def _combine(wts_t, yg, h2, x1, mod3, wsg, wsu, wsd, gpost, seq, first_tile, partial_out):
    t, d = x1.shape
    tm = min(TM_COMBINE, seq)
    tpb = seq // tm
    full = lambda a: pl.BlockSpec(a.shape, lambda i: (0,) * a.ndim)
    row = lambda w: pl.BlockSpec((tm, w), lambda i: (i + first_tile, 0))
    args = [wts_t, yg, h2, x1, mod3, wsg, wsu, wsd, gpost]
    in_specs = [row(TOP_K), pl.BlockSpec((TOP_K, tm, _ROW_WORDS), lambda i: (0, i, 0)), row(d), row(d),
                pl.BlockSpec((1, 6, d), lambda i: ((i + first_tile) // tpb, 0, 0)),
                full(wsg), full(wsu), full(wsd), full(gpost)]
    aliases = {}
    if partial_out is not None:
        aliases = {len(args): 0}
        args.append(partial_out)
        in_specs.append(pl.BlockSpec(memory_space=pl.ANY))
    return pl.pallas_call(
        _combine_kernel,
        grid=(yg.shape[1] // tm,),
        in_specs=in_specs,
        out_specs=row(d),
        out_shape=jax.ShapeDtypeStruct((t, d), F32),
        input_output_aliases=aliases,
        compiler_params=_params(("arbitrary",)),
        name="combine",
    )(*args)


def _pack_weights(w_in, w_uq, w_ukv):
    d = w_in.shape[0]
    half = QK_ROPE_DIM // 2
    z = lambda n, c: jnp.zeros((n, c), F32)
    o = Q_LORA_RANK + KV_LORA_RANK
    kr = w_in[:, o:o + QK_ROPE_DIM]
    kr_grp = jnp.concatenate([z(d, QK_NOPE_DIM), kr, z(d, HEAD_PAD - QK_NOPE_DIM - QK_ROPE_DIM)], axis=1)
    kr_rot = jnp.concatenate([z(d, QK_NOPE_DIM), -kr[:, half:], kr[:, :half],
                              z(d, HEAD_PAD - QK_NOPE_DIM - QK_ROPE_DIM)], axis=1)
    win_p = jnp.concatenate([w_in[:, :o], kr_grp, kr_rot, w_in[:, o + QK_ROPE_DIM:]], axis=1)

    scale = float(QK_NOPE_DIM + QK_ROPE_DIM) ** -0.5 * float(np.log2(np.e))
    r = Q_LORA_RANK
    qd = QK_NOPE_DIM + QK_ROPE_DIM
    q_grp, q_rot = [], []
    for h in range(MLA_HEADS):
        nope = w_uq[:, h * qd:h * qd + QK_NOPE_DIM]
        rope = w_uq[:, h * qd + QK_NOPE_DIM:(h + 1) * qd]
        pad = z(r, HEAD_PAD - qd)
        q_grp.append(jnp.concatenate([nope, rope, pad], axis=1))
        q_rot.append(jnp.concatenate([z(r, QK_NOPE_DIM), -rope[:, half:], rope[:, :half], pad], axis=1))
    wuq_p = jnp.concatenate(q_grp + q_rot, axis=1) * scale

    c = KV_LORA_RANK
    kd = QK_NOPE_DIM + V_HEAD_DIM
    k_grp, v_grp = [], []
    for h in range(MLA_HEADS):
        k_grp.append(jnp.concatenate([w_ukv[:, h * kd:h * kd + QK_NOPE_DIM], z(c, HEAD_PAD - QK_NOPE_DIM)], axis=1))
        v_grp.append(jnp.concatenate([w_ukv[:, h * kd + QK_NOPE_DIM:(h + 1) * kd], z(c, HEAD_PAD - V_HEAD_DIM)], axis=1))
    wukv_p = jnp.concatenate(k_grp + v_grp, axis=1)
    return win_p.astype(BF16), wuq_p.astype(BF16), wukv_p.astype(BF16)


def _rope_inputs(positions):
    inv = 1.0 / (ROPE_THETA ** (jnp.arange(0, QK_ROPE_DIM, 2, dtype=F32) / QK_ROPE_DIM))
    freq = jnp.tile(inv, LANES // inv.shape[0]).reshape(1, LANES)
    pos = jnp.repeat(positions.astype(F32).reshape(-1, _ROPE_PACK), QK_ROPE_DIM, axis=1)
    return pos, freq


def _layer(x2, c, pos, freq, batch, seq, w_ada, b_ada, g_pre_mix, w_in, g_q_lat, w_uq, g_kv_lat, w_ukv,
           w_conv, g_attn_out, g_conv_out, w_out, g_post_mix, g_pre_ffn, w_router, b_router,
           w_gate, w_up, w_down, w_sh_gate, w_sh_up, w_sh_down, g_post_ffn):
    t, d = x2.shape
    r1 = lambda a: a.reshape(1, -1)

    c_pad = jnp.zeros((SUBLANES, d), F32).at[:batch].set(c)
    mod = _ada(c_pad, w_ada, r1(b_ada))[:batch]
    mod3 = mod.reshape(batch, 6, d)

    win_p, wuq_p, wukv_p = _pack_weights(w_in, w_uq, w_ukv)
    vone = jnp.zeros((1, HEAD_PAD), F32).at[0, V_HEAD_DIM].set(1.0)
    q, k, v, yc = _mix_in(x2, mod3, r1(g_pre_mix), win_p, r1(g_q_lat), wuq_p, r1(g_kv_lat), wukv_p,
                          vone, w_conv, r1(g_conv_out), pos, freq, seq)
    attn = _attention(q, k, v, batch, seq)
    x1, h2, h2p, idx, wts, rank, cnt = _mix_out(
        attn, yc, x2, mod3, r1(g_attn_out), w_out.astype(BF16), r1(g_post_mix), r1(g_pre_ffn),
        w_router.T, b_router.reshape(-1, 1), seq)

    counts = cnt[:, 0].astype(I32)
    padded = ((counts + BM_EXPERT - 1) // BM_EXPERT) * BM_EXPERT
    pad_end = jnp.cumsum(padded)
    pad_start = pad_end - padded
    m = t * TOP_K
    nb = (m + N_EXPERTS * (BM_EXPERT - 1)) // BM_EXPERT
    nused = pad_end[-1] // BM_EXPERT
    bidx = jnp.arange(nb, dtype=I32)
    blk_exp = jnp.sum((pad_end[None, :] <= (bidx * BM_EXPERT)[:, None]).astype(I32), axis=1)
    first = ((bidx < nused) & ((bidx == 0) | (blk_exp != jnp.roll(blk_exp, 1)))).astype(I32)
    ordinal = jnp.maximum(jnp.cumsum(first) - 1, 0).astype(I32)
    seen = jnp.cumsum((counts > 0).astype(I32))
    uexp = jnp.minimum(jnp.sum((seen[None, :] <= jnp.arange(N_EXPERTS, dtype=I32)[:, None]).astype(I32), axis=1),
                       N_EXPERTS - 1).astype(I32)
    meta = jnp.stack([nused, seen[-1]]).astype(I32)

    dest = _dest(idx, rank, pad_start.astype(F32).reshape(-1, 1))
    xs = _scatter_rows_sc(h2p, dest.reshape(TOP_K, -1, SC_ROWS).transpose(1, 0, 2), nb * BM_EXPERT)
    xs = _padfill((pad_start + counts).astype(I32), (padded - counts).astype(I32), xs)
    ys = _experts(first, ordinal, uexp, meta, xs, w_gate, w_up, w_down)
    tm_c = min(TM_COMBINE, seq)
    half_rows = (t // 2) * TOP_K
    even_split = half_rows % (2 * SC_GATHER_ROWS * SC_CORES * SC_SUBCORES) == 0
    parts = 2 if t % (2 * tm_c) == 0 and even_split else 1
    tp = t // parts
    wsg, wsu, wsd = w_sh_gate.astype(BF16), w_sh_up.astype(BF16), w_sh_down.astype(BF16)
    out = None
    for part in range(parts):
        part_dest = dest[:, part * tp:(part + 1) * tp].reshape(-1)
        yg = _gather_rows_sc(ys, part_dest).reshape(TOP_K, tp, _ROW_WORDS)
        out = _combine(wts.T, yg, h2, x1, mod3, wsg, wsu, wsd, r1(g_post_ffn), seq, part * tp // tm_c, out)
    return out


def kernel(x, c, positions, w_ada, b_ada, g_pre_mix, w_in, g_q_lat, w_uq, g_kv_lat, w_ukv, w_conv, g_attn_out, g_conv_out, w_out, g_post_mix, g_pre_ffn, w_router, b_router, w_gate, w_up, w_down, w_sh_gate, w_sh_up, w_sh_down, g_post_ffn):
    batch, seq, d = x.shape
    pos, freq = _rope_inputs(positions)
    x2 = x.reshape(batch * seq, d)
    for l in range(w_ada.shape[0]):
        x2 = _layer(x2, c, pos, freq, batch, seq, w_ada[l], b_ada[l], g_pre_mix[l], w_in[l], g_q_lat[l],
                    w_uq[l], g_kv_lat[l], w_ukv[l], w_conv[l], g_attn_out[l], g_conv_out[l], w_out[l],
                    g_post_mix[l], g_pre_ffn[l], w_router[l], b_router[l], w_gate[l], w_up[l], w_down[l],
                    w_sh_gate[l], w_sh_up[l], w_sh_down[l], g_post_ffn[l])
    return x2.reshape(batch, seq, d)
```

```python
import functools

import jax
import jax.numpy as jnp
import numpy as np
from jax import lax
from jax.experimental import pallas as pl
from jax.experimental.pallas import tpu as pltpu
from jax.experimental.pallas import tpu_sc as plsc

F32 = jnp.float32
BF16 = jnp.bfloat16
I32 = jnp.int32
U32 = jnp.uint32

CHUNK = 64
MLA_HEADS = 8
QK_NOPE_DIM = 64
QK_ROPE_DIM = 32
V_HEAD_DIM = 64
Q_LORA_RANK = 384
KV_LORA_RANK = 256
ROPE_THETA = 10000.0
CONV_WIDTH = 3
N_EXPERTS = 256
TOP_K = 8
N_EXPERT_GROUPS = 8
TOPK_GROUPS = 4
EXPERT_DIM = 256
ROUTED_SCALE = 2.5
EPS = 1e-6

LANES = 128
SUBLANES = 8
HEAD_PAD = LANES
VMEM_LIMIT_BYTES = 56 * 1024 * 1024

TM_IN = 1024
TQ_ATTN = 512
TM_OUT = 512
TM_DEST = 2048
BM_EXPERT = 256
TM_COMBINE = 512

NEG_INF = float("-inf")


def _rms(x, g):
    return x * lax.rsqrt(jnp.mean(x * x, axis=-1, keepdims=True) + EPS) * g


_HI_MASK = np.uint32(0xFFFF0000)
_ROW_WORDS = 512
_ROW_SLABS = _ROW_WORDS // LANES


def _pack_row_words(lo, hi):
    lo_w = lax.bitcast_convert_type(lo.astype(BF16).astype(F32), U32) >> 16
    hi_w = lax.bitcast_convert_type(hi.astype(BF16).astype(F32), U32) & _HI_MASK
    return lo_w | hi_w


def _unpack_row_words(w):
    return (lax.bitcast_convert_type(w << 16, F32), lax.bitcast_convert_type(w & _HI_MASK, F32))


def _params(sem):
    return pltpu.CompilerParams(dimension_semantics=sem, vmem_limit_bytes=VMEM_LIMIT_BYTES)


def _ada_kernel(c_ref, w_ref, b_ref, o_ref):
    c = c_ref[...]
    s = c * jax.nn.sigmoid(c)
    o_ref[...] = jnp.dot(s, w_ref[...], preferred_element_type=F32,
                         precision=lax.Precision.HIGHEST) + b_ref[...]


def _ada(c_pad, w, b):
    rows, d = c_pad.shape
    n = w.shape[1]
    tn = 1536
    return pl.pallas_call(
        _ada_kernel,
        grid=(n // tn,),
        in_specs=[pl.BlockSpec((rows, d), lambda j: (0, 0)),
                  pl.BlockSpec((d, tn), lambda j: (0, j)),
                  pl.BlockSpec((1, tn), lambda j: (0, j))],
        out_specs=pl.BlockSpec((rows, tn), lambda j: (0, j)),
        out_shape=jax.ShapeDtypeStruct((rows, n), F32),
        compiler_params=_params(("arbitrary",)),
        name="ada",
    )(c_pad, w, b)


_CQ0, _CQ1 = 0, Q_LORA_RANK
_CKV0, _CKV1 = _CQ1, _CQ1 + KV_LORA_RANK
_KR0, _KR1 = _CKV1, _CKV1 + 2 * HEAD_PAD
_CONV_DIM = 512
_GB0 = _KR1
_GC0 = _GB0 + _CONV_DIM
_XV0 = _GC0 + _CONV_DIM
_WIN_COLS = _XV0 + _CONV_DIM
_QW = MLA_HEADS * HEAD_PAD
_ROPE_PACK = LANES // QK_ROPE_DIM


def _mix_in_kernel(tiles_per_batch, x_ref, mod_ref, gpre_ref, win_ref, gq_ref, wuq_ref, gkv_ref,
                   wukv_ref, vone_ref, wconv_ref, gconv_ref, pos_ref, freq_ref,
                   q_ref, k_ref, v_ref, yc_ref, h_scr, u_scr, cos_scr, sin_scr):
    i = pl.program_id(0)
    tm = x_ref.shape[0]
    sh1 = mod_ref[0, 0:1, :]
    sc1 = mod_ref[0, 1:2, :]
    h = _rms(x_ref[...], gpre_ref[...]) * (1.0 + sc1) + sh1
    h_scr[...] = h.astype(BF16)
    ang4 = pos_ref[...] * freq_ref[...]
    cos4 = jnp.cos(ang4)
    sin4 = jnp.sin(ang4)
    lane = lax.broadcasted_iota(I32, ang4.shape, 1)
    on_rope = jnp.logical_and(lane >= QK_NOPE_DIM, lane < QK_NOPE_DIM + QK_ROPE_DIM)
    for g in range(_ROPE_PACK):
        shift = (QK_NOPE_DIM - QK_ROPE_DIM * g) % LANES
        cg = cos4 if shift == 0 else pltpu.roll(cos4, shift, axis=1)
        sg = sin4 if shift == 0 else pltpu.roll(sin4, shift, axis=1)
        cos_scr[pl.ds(g, tm // _ROPE_PACK, stride=_ROPE_PACK), :] = jnp.where(on_rope, cg, 1.0)
        sin_scr[pl.ds(g, tm // _ROPE_PACK, stride=_ROPE_PACK), :] = jnp.where(on_rope, sg, 0.0)
    cos = cos_scr[...]
    sin = sin_scr[...]

    cq = jnp.dot(h_scr[...], win_ref[:, _CQ0:_CQ1], preferred_element_type=F32)
    cqn = _rms(cq, gq_ref[...]).astype(BF16)
    qq = jnp.dot(cqn, wuq_ref[...], preferred_element_type=F32)
    for hd in range(MLA_HEADS):
        lo = hd * HEAD_PAD
        qh = qq[:, lo:lo + HEAD_PAD] * cos + qq[:, _QW + lo:_QW + lo + HEAD_PAD] * sin
        q_ref[:, lo:lo + HEAD_PAD] = qh.astype(BF16)

    ckv = jnp.dot(h_scr[...], win_ref[:, _CKV0:_CKV1], preferred_element_type=F32)
    ckvn = _rms(ckv, gkv_ref[...]).astype(BF16)
    kv = jnp.dot(ckvn, wukv_ref[...], preferred_element_type=F32)
    krr = jnp.dot(h_scr[...], win_ref[:, _KR0:_KR1], preferred_element_type=F32)
    kr = krr[:, 0:HEAD_PAD] * cos + krr[:, HEAD_PAD:2 * HEAD_PAD] * sin
    vone = vone_ref[...]
    for hd in range(MLA_HEADS):
        lo = hd * HEAD_PAD
        k_ref[:, lo:lo + HEAD_PAD] = (kv[:, lo:lo + HEAD_PAD] + kr).astype(BF16)
        v_ref[:, lo:lo + HEAD_PAD] = (kv[:, _QW + lo:_QW + lo + HEAD_PAD] + vone).astype(BF16)

    gb = jnp.dot(h_scr[...], win_ref[:, _GB0:_GC0], preferred_element_type=F32)
    gc = jnp.dot(h_scr[...], win_ref[:, _GC0:_XV0], preferred_element_type=F32)
    xv = jnp.dot(h_scr[...], win_ref[:, _XV0:_WIN_COLS], preferred_element_type=F32)
    u = gc * xv
    prev = u_scr[tm:tm + SUBLANES, :]
    first = (i % tiles_per_batch) == 0
    u_scr[0:SUBLANES, :] = jnp.where(first, jnp.zeros_like(prev), prev)
    u_scr[SUBLANES:tm + SUBLANES, :] = u
    um1 = u_scr[SUBLANES - 1:tm + SUBLANES - 1, :]
    um2 = u_scr[SUBLANES - 2:tm + SUBLANES - 2, :]
    assert wconv_ref.shape[0] == CONV_WIDTH
    conv = wconv_ref[0:1, :] * um2 + wconv_ref[1:2, :] * um1 + wconv_ref[2:3, :] * u
    yc_ref[...] = _rms(gb * conv, gconv_ref[...]).astype(BF16)


def _mix_in(x2, mod3, gpre, win_p, gq, wuq_p, gkv, wukv_p, vone, wconv, gconv, pos, freq, seq):
    t, d = x2.shape
    tm = min(TM_IN, seq)
    tpb = seq // tm
    full = lambda a: pl.BlockSpec(a.shape, lambda i: (0,) * a.ndim)
    row = lambda w: pl.BlockSpec((tm, w), lambda i: (i, 0))
    return pl.pallas_call(
        functools.partial(_mix_in_kernel, tpb),
        grid=(t // tm,),
        in_specs=[row(d),
                  pl.BlockSpec((1, 6, d), lambda i: (i // tpb, 0, 0)),
                  full(gpre), full(win_p), full(gq), full(wuq_p), full(gkv), full(wukv_p),
                  full(vone), full(wconv), full(gconv),
                  pl.BlockSpec((tm // _ROPE_PACK, LANES), lambda i: (i, 0)), full(freq)],
        out_specs=[row(_QW), row(_QW), row(_QW), row(_CONV_DIM)],
        out_shape=[jax.ShapeDtypeStruct((t, _QW), BF16), jax.ShapeDtypeStruct((t, _QW), BF16),
                   jax.ShapeDtypeStruct((t, _QW), BF16), jax.ShapeDtypeStruct((t, _CONV_DIM), BF16)],
        scratch_shapes=[pltpu.VMEM((tm, d), BF16), pltpu.VMEM((tm + SUBLANES, _CONV_DIM), F32),
                        pltpu.VMEM((tm, HEAD_PAD), F32), pltpu.VMEM((tm, HEAD_PAD), F32)],
        compiler_params=_params(("arbitrary",)),
        name="mix_in",
    )(x2, mod3, gpre, win_p, gq, wuq_p, gkv, wukv_p, vone, wconv, gconv, pos, freq)


_HEADS_PER_STEP = 2


def _attn_kernel(tq, q_ref, k_ref, v_ref, o_ref, s_scr, mrun_scr, mb_scr, acc_scr):
    tk = tq
    nq = q_ref.shape[0] // tq
    lane_groups = tk // LANES
    heads = range(_HEADS_PER_STEP)
    lanes = [slice(hh * HEAD_PAD, (hh + 1) * HEAD_PAD) for hh in heads]

    def tile_max(s):
        m = s[:, 0:LANES]
        for g in range(1, lane_groups):
            m = jnp.maximum(m, s[:, g * LANES:(g + 1) * LANES])
        return m

    def scores(hh, qi, kv):
        off = pl.multiple_of(kv * tk, tk)
        return lax.dot_general(q_ref[qi * tq:(qi + 1) * tq, lanes[hh]], k_ref[pl.ds(off, tk), lanes[hh]],
                               (((1,), (1,)), ((), ())), preferred_element_type=F32)

    rc = lax.broadcasted_iota(I32, (tq, tk), 0) // CHUNK
    cc = lax.broadcasted_iota(I32, (tq, tk), 1) // CHUNK

    def diagonal(qi):
        for hh in heads:
            s = jnp.where(cc <= rc, scores(hh, qi, qi), NEG_INF)
            s_scr[hh, qi] = s
            m_row = jnp.max(jnp.maximum(mrun_scr[hh], tile_max(s)), axis=1, keepdims=True)
            mb_scr[hh] = jnp.broadcast_to(m_row, (tq, LANES))

    mrun_scr[...] = jnp.full(mrun_scr.shape, NEG_INF, F32)
    diagonal(0)
    for qi in range(nq):
        has_next = qi + 1 < nq
        acc_scr[...] = jnp.zeros(acc_scr.shape, F32)
        if has_next:
            mrun_scr[...] = jnp.full(mrun_scr.shape, NEG_INF, F32)

        def body(kv, carry, qi=qi, has_next=has_next):
            off = pl.multiple_of(kv * tk, tk)
            for hh in heads:
                mb = mb_scr[hh]
                p = jnp.concatenate(
                    [jnp.exp2(s_scr[hh, kv, :, g * LANES:(g + 1) * LANES] - mb) for g in range(lane_groups)],
                    axis=1).astype(BF16)
                acc_scr[hh] += jnp.dot(p, v_ref[pl.ds(off, tk), lanes[hh]], preferred_element_type=F32)
                if has_next:
                    s = scores(hh, qi + 1, kv)
                    s_scr[hh, kv] = s
                    mrun_scr[hh] = jnp.maximum(mrun_scr[hh], tile_max(s))
            return carry

        lax.fori_loop(0, qi + 1, body, 0, unroll=4)
        for hh in heads:
            acc = acc_scr[hh]
            o = acc[:, 0:V_HEAD_DIM] / acc[:, V_HEAD_DIM:V_HEAD_DIM + 1]
            o_ref[qi * tq:(qi + 1) * tq, hh * V_HEAD_DIM:(hh + 1) * V_HEAD_DIM] = o.astype(BF16)
        if has_next:
            diagonal(qi + 1)


def _attention(q, k, v, batch, seq):
    t = q.shape[0]
    tq = min(TQ_ATTN, seq)
    nq = seq // tq
    hw = _HEADS_PER_STEP * HEAD_PAD
    ow = _HEADS_PER_STEP * V_HEAD_DIM
    blk = lambda w: pl.BlockSpec((seq, w), lambda b, j: (b, j))
    return pl.pallas_call(
        functools.partial(_attn_kernel, tq),
        grid=(batch, MLA_HEADS // _HEADS_PER_STEP),
        in_specs=[blk(hw), blk(hw), blk(hw)],
        out_specs=blk(ow),
        out_shape=jax.ShapeDtypeStruct((t, MLA_HEADS * V_HEAD_DIM), BF16),
        scratch_shapes=[pltpu.VMEM((_HEADS_PER_STEP, nq, tq, tq), F32),
                        pltpu.VMEM((_HEADS_PER_STEP, tq, LANES), F32),
                        pltpu.VMEM((_HEADS_PER_STEP, tq, LANES), F32),
                        pltpu.VMEM((_HEADS_PER_STEP, tq, HEAD_PAD), F32)],
        compiler_params=_params(("arbitrary", "arbitrary")),
        name="attn",
    )(q, k, v)


_GROUP_SIZE = N_EXPERTS // N_EXPERT_GROUPS
_BIG = 1.0e9


def _mix_out_kernel(attn_ref, yc_ref, x_ref, mod_ref, gattn_ref, wout_ref, gpost_ref, gpre2_ref,
                    wrt_ref, br_ref, x1_ref, h2_ref, h2p_ref, idx_ref, wts_ref, rank_ref, cnt_ref,
                    carry_scr, scores_scr, sel_scr):
    i = pl.program_id(0)
    n_tiles = pl.num_programs(0) - 1

    @pl.when(i == 0)
    def _():
        carry_scr[...] = jnp.zeros(carry_scr.shape, F32)
        scores, sel = _project_tile(attn_ref, yc_ref, x_ref, mod_ref, gattn_ref, wout_ref, gpost_ref,
                                    gpre2_ref, wrt_ref, br_ref, x1_ref, h2_ref, h2p_ref)
        scores_scr[0] = scores
        sel_scr[0] = sel

    @pl.when(jnp.logical_and(i >= 1, i < n_tiles))
    def _():
        prev_scores = scores_scr[(i - 1) % 2]
        prev_sel = sel_scr[(i - 1) % 2]
        scores, sel = _project_tile(attn_ref, yc_ref, x_ref, mod_ref, gattn_ref, wout_ref, gpost_ref,
                                    gpre2_ref, wrt_ref, br_ref, x1_ref, h2_ref, h2p_ref)
        scores_scr[i % 2] = scores
        sel_scr[i % 2] = sel
        _route_tile(prev_scores, prev_sel, idx_ref, wts_ref, rank_ref, cnt_ref, carry_scr)

    @pl.when(i == n_tiles)
    def _():
        _route_tile(scores_scr[(i - 1) % 2], sel_scr[(i - 1) % 2], idx_ref, wts_ref, rank_ref, cnt_ref,
                    carry_scr)


def _project_tile(attn_ref, yc_ref, x_ref, mod_ref, gattn_ref, wout_ref, gpost_ref, gpre2_ref,
                  wrt_ref, br_ref, x1_ref, h2_ref, h2p_ref):
    half = attn_ref.shape[1]
    an = _rms(attn_ref[...].astype(F32), gattn_ref[...]).astype(BF16)
    mix = (jnp.dot(an, wout_ref[0:half, :], preferred_element_type=F32)
           + jnp.dot(yc_ref[...], wout_ref[half:, :], preferred_element_type=F32))
    g1 = mod_ref[0, 2:3, :]
    sh2 = mod_ref[0, 3:4, :]
    sc2 = mod_ref[0, 4:5, :]
    x1 = x_ref[...] + g1 * _rms(mix, gpost_ref[...])
    x1_ref[...] = x1
    h2 = _rms(x1, gpre2_ref[...]) * (1.0 + sc2) + sh2
    h2_ref[...] = h2.astype(BF16)
    h2p_ref[...] = _pack_row_words(h2[:, 0:_ROW_WORDS], h2[:, _ROW_WORDS:])

    logits = lax.dot_general(wrt_ref[...], h2, (((1,), (1,)), ((), ())),
                             preferred_element_type=F32, precision=lax.Precision.HIGHEST)
    scores = jax.nn.sigmoid(logits)
    return scores, scores + br_ref[...]


def _route_tile(scores, sel, idx_ref, wts_ref, rank_ref, cnt_ref, carry_scr):
    tm = scores.shape[1]
    row = lax.broadcasted_iota(I32, (N_EXPERTS, tm), 0).astype(F32)

    gscore = []
    rw = lax.broadcasted_iota(I32, (_GROUP_SIZE, tm), 0).astype(F32)
    for g in range(N_EXPERT_GROUPS):
        blk = sel[g * _GROUP_SIZE:(g + 1) * _GROUP_SIZE, :]
        m1 = jnp.max(blk, axis=0, keepdims=True)
        i1 = jnp.min(jnp.where(blk == m1, rw, _BIG), axis=0, keepdims=True)
        m2 = jnp.max(jnp.where(rw == i1, NEG_INF, blk), axis=0, keepdims=True)
        gscore.append(m1 + m2)

    gkeep = [jnp.zeros((1, tm), F32) for _ in range(N_EXPERT_GROUPS)]
    for _ in range(TOPK_GROUPS):
        mg = functools.reduce(jnp.maximum, gscore)
        ig = functools.reduce(jnp.minimum, [jnp.where(gscore[g] == mg, float(g), _BIG)
                                            for g in range(N_EXPERT_GROUPS)])
        for g in range(N_EXPERT_GROUPS):
            hit = ig == float(g)
            gkeep[g] = jnp.where(hit, 1.0, gkeep[g])
            gscore[g] = jnp.where(hit, NEG_INF, gscore[g])
    n_slabs = N_EXPERTS // SUBLANES
    slabs_per_group = _GROUP_SIZE // SUBLANES
    sub = lax.broadcasted_iota(I32, (SUBLANES, tm), 0).astype(F32)
    first_rows = [jnp.where(gkeep[j // slabs_per_group] > 0.0, sel[j * SUBLANES:(j + 1) * SUBLANES, :], NEG_INF)
                  for j in range(n_slabs)]
    cur_rows = list(first_rows)
    krow = lax.broadcasted_iota(I32, (TOP_K, tm), 0)
    idx_rows = []
    idx_f = jnp.zeros((TOP_K, tm), F32)
    sc_k = jnp.zeros((TOP_K, tm), F32)
    sc_sum = jnp.zeros((1, tm), F32)
    prev = None
    for k in range(TOP_K):
        best = jnp.full((SUBLANES, tm), NEG_INF, F32)
        best_slab = jnp.zeros((SUBLANES, tm), F32)
        best_score = jnp.zeros((SUBLANES, tm), F32)
        for j in range(n_slabs):
            if prev is not None:
                cur_rows[j] = jnp.where(sub == prev - float(j * SUBLANES), NEG_INF, cur_rows[j])
            better = cur_rows[j] > best
            best = jnp.where(better, cur_rows[j], best)
            best_slab = jnp.where(better, float(j), best_slab)
            best_score = jnp.where(better, scores[j * SUBLANES:(j + 1) * SUBLANES, :], best_score)
        best_idx = best_slab * float(SUBLANES) + sub
        m = jnp.max(best, axis=0, keepdims=True)
        ik = jnp.min(jnp.where(best == m, best_idx, _BIG), axis=0, keepdims=True)
        sk = jnp.sum(jnp.where(best_idx == ik, best_score, 0.0), axis=0, keepdims=True)
        prev = ik
        idx_rows.append(ik)
        idx_f = jnp.where(krow == k, ik, idx_f)
        sc_k = jnp.where(krow == k, sk, sc_k)
        sc_sum = sc_sum + sk
    wts_ref[...] = sc_k / sc_sum * ROUTED_SCALE
    idx_ref[...] = idx_f.astype(I32)
    onehot = jnp.concatenate(
        [jnp.where(jnp.where(sub == prev - float(j * SUBLANES), NEG_INF, cur_rows[j]) != first_rows[j], 1.0, 0.0)
         for j in range(n_slabs)], axis=0)

    tri = (lax.broadcasted_iota(I32, (tm, tm), 0) < lax.broadcasted_iota(I32, (tm, tm), 1))
    excl = jnp.dot(onehot.astype(BF16), tri.astype(BF16), preferred_element_type=F32)
    rank_e = carry_scr[:, 0:1] + excl
    rank_k = jnp.zeros((TOP_K, tm), F32)
    for k in range(TOP_K):
        hit = row == idx_rows[k]
        rk = jnp.sum(jnp.where(hit, rank_e, 0.0), axis=0, keepdims=True)
        rank_k = jnp.where(krow == k, rk, rank_k)
    rank_ref[...] = rank_k.astype(I32)
    carry_scr[...] = carry_scr[...] + jnp.sum(onehot, axis=1, keepdims=True)
    cnt_ref[...] = carry_scr[...]


def _mix_out(attn, yc, x2, mod3, gattn, wout, gpost, gpre2, wrt, br, seq):
    t, d = x2.shape
    tm = min(TM_OUT, seq)
    tpb = seq // tm
    n_tiles = t // tm
    last = n_tiles - 1
    full = lambda a: pl.BlockSpec(a.shape, lambda i: (0,) * a.ndim)
    row = lambda w: pl.BlockSpec((tm, w), lambda i: (jnp.minimum(i, last), 0))
    col = pl.BlockSpec((TOP_K, tm), lambda i: (0, jnp.maximum(i - 1, 0)))
    return pl.pallas_call(
        _mix_out_kernel,
        grid=(n_tiles + 1,),
        in_specs=[row(attn.shape[1]), row(yc.shape[1]), row(d),
                  pl.BlockSpec((1, 6, d), lambda i: (jnp.minimum(i, last) // tpb, 0, 0)),
                  full(gattn), full(wout), full(gpost), full(gpre2), full(wrt), full(br)],
        out_specs=[row(d), row(d), row(_ROW_WORDS), col, col, col,
                   pl.BlockSpec((N_EXPERTS, LANES), lambda i: (0, 0))],
        out_shape=[jax.ShapeDtypeStruct((t, d), F32), jax.ShapeDtypeStruct((t, d), BF16),
                   jax.ShapeDtypeStruct((t, _ROW_WORDS), U32),
                   jax.ShapeDtypeStruct((TOP_K, t), I32), jax.ShapeDtypeStruct((TOP_K, t), F32),
                   jax.ShapeDtypeStruct((TOP_K, t), I32),
                   jax.ShapeDtypeStruct((N_EXPERTS, LANES), F32)],
        scratch_shapes=[pltpu.VMEM((N_EXPERTS, LANES), F32), pltpu.VMEM((2, N_EXPERTS, tm), F32),
                        pltpu.VMEM((2, N_EXPERTS, tm), F32)],
        compiler_params=_params(("arbitrary",)),
        name="mix_out",
    )(attn, yc, x2, mod3, gattn, wout, gpost, gpre2, wrt, br)


def _dest_kernel(idx_ref, rank_ref, pstart_ref, dest_ref):
    tm = idx_ref.shape[1]
    row = lax.broadcasted_iota(I32, (N_EXPERTS, tm), 0)
    krow = lax.broadcasted_iota(I32, (TOP_K, tm), 0)
    pstart = pstart_ref[...]
    idx = idx_ref[...]
    out = jnp.zeros((TOP_K, tm), F32)
    for k in range(TOP_K):
        hit = row == idx[k:k + 1, :]
        base = jnp.sum(jnp.where(hit, pstart, 0.0), axis=0, keepdims=True)
        out = jnp.where(krow == k, base, out)
    dest_ref[...] = out.astype(I32) + rank_ref[...]


def _dest(idx, rank, pstart):
    t = idx.shape[1]
    tm = min(TM_DEST, t)
    col = pl.BlockSpec((TOP_K, tm), lambda i: (0, i))
    return pl.pallas_call(
        _dest_kernel,
        grid=(t // tm,),
        in_specs=[col, col, pl.BlockSpec((N_EXPERTS, 1), lambda i: (0, 0))],
        out_specs=col,
        out_shape=jax.ShapeDtypeStruct((TOP_K, t), I32),
        compiler_params=_params(("arbitrary",)),
        name="dest",
    )(idx, rank, pstart)


_PAD_CHUNKS = tuple(BM_EXPERT >> s for s in range(1, BM_EXPERT.bit_length()))


SC_CORES = 2
SC_SUBCORES = 16
SC_ROWS = 128


def _sc_worker_split(n_chunks):
    workers = SC_CORES * SC_SUBCORES
    per_worker = max(1, n_chunks // workers)
    active = n_chunks // per_worker
    assert active * per_worker == n_chunks and active <= workers
    return per_worker, active


def _scatter_rows_sc(rows, idx3d, n_out):
    n, width = rows.shape
    n_chunks = n // SC_ROWS
    per_worker, active = _sc_worker_split(n_chunks)
    mesh = plsc.VectorSubcoreMesh(core_axis_name="c", subcore_axis_name="s")

    @functools.partial(
        pl.kernel, mesh=mesh, out_type=jax.ShapeDtypeStruct((n_out, width), rows.dtype),
        scratch_types=[pltpu.VMEM((TOP_K, SC_ROWS), I32), pltpu.VMEM((SC_ROWS, width), rows.dtype),
                       pltpu.SemaphoreType.DMA])
    def scatter(rows_hbm, idx_hbm, out_hbm, idx_v, rows_v, sem):
        wid = lax.axis_index("s") * SC_CORES + lax.axis_index("c")

        @pl.when(wid < active)
        def _():
            @pl.loop(0, per_worker)
            def _(c):
                chunk = wid * per_worker + c
                pltpu.sync_copy(rows_hbm.at[pl.ds(chunk * SC_ROWS, SC_ROWS)], rows_v)
                pltpu.sync_copy(idx_hbm.at[chunk], idx_v)
                copies = [pltpu.make_async_copy(rows_v, out_hbm.at[idx_v.at[k]], sem) for k in range(TOP_K)]
                for cp in copies:
                    cp.start()
                for cp in copies:
                    cp.wait()

    return scatter(rows, idx3d)


def _padfill_kernel(pad_from_ref, pad_n_ref, xs_in_hbm, xs_hbm, zero_scr, pad_sem):
    del xs_in_hbm
    zero_scr[...] = jnp.zeros(zero_scr.shape, zero_scr.dtype)

    def pad_copies(e, act):
        n = pad_n_ref[e]
        base = pad_from_ref[e]

        def single_rows(start, count):
            for j in range(SUBLANES - 1):
                @pl.when(j < count)
                def _():
                    act(pltpu.make_async_copy(zero_scr.at[pl.ds(0, 1), :],
                                              xs_hbm.at[pl.ds(start + j, 1), :], pad_sem))

        head = jnp.minimum(n, (SUBLANES - (base & (SUBLANES - 1))) & (SUBLANES - 1))
        single_rows(base, head)
        rest = n - head
        mid = base + head
        for rows in _PAD_CHUNKS:
            if rows >= SUBLANES:
                @pl.when((rest & rows) != 0)
                def _():
                    start = pl.multiple_of(mid + (rest & ~(2 * rows - 1)), SUBLANES)
                    act(pltpu.make_async_copy(zero_scr.at[pl.ds(0, rows), :],
                                              xs_hbm.at[pl.ds(start, rows), :], pad_sem))
        single_rows(mid + (rest & ~(SUBLANES - 1)), rest & (SUBLANES - 1))

    def issue_pad(e, carry):
        pad_copies(e, lambda cp: cp.start())
        return carry

    def drain_pad(e, carry):
        pad_copies(e, lambda cp: cp.wait())
        return carry

    lax.fori_loop(0, N_EXPERTS, issue_pad, 0)
    lax.fori_loop(0, N_EXPERTS, drain_pad, 0)


def _padfill(pad_from, pad_n, xs):
    return pl.pallas_call(
        _padfill_kernel,
        grid_spec=pltpu.PrefetchScalarGridSpec(
            num_scalar_prefetch=2,
            grid=(1,),
            in_specs=[pl.BlockSpec(memory_space=pl.ANY)],
            out_specs=pl.BlockSpec(memory_space=pl.ANY),
            scratch_shapes=[pltpu.VMEM((BM_EXPERT // 2, _ROW_WORDS), U32), pltpu.SemaphoreType.DMA]),
        out_shape=jax.ShapeDtypeStruct(xs.shape, xs.dtype),
        input_output_aliases={2: 0},
        compiler_params=_params(("arbitrary",)),
        name="padfill",
    )(pad_from, pad_n, xs)


_XS_SLOTS = 4
_YS_SLOTS = 2
_W_SLOTS = 4
_W_AHEAD = 2


_ROW_QUANTUM = 64


def _expert_kernel(first_ref, ord_ref, quanta_ref, uexp_ref, meta_ref, xs_hbm, wg_hbm, wu_hbm, wd_hbm, ys_hbm,
                   xs_buf, ys_buf, wg_buf, wu_buf, wd_buf, act_scr, xs_sem, ys_sem, w_sem):
    i = pl.program_id(0)
    nused = meta_ref[0]
    nexp = meta_ref[1]
    bm = xs_buf.shape[1]

    def for_row_count(b, act):
        for n in range(1, bm // _ROW_QUANTUM + 1):
            @pl.when(quanta_ref[b] == n)
            def _():
                act(n * _ROW_QUANTUM)

    def xs_copy(b, slot, rows):
        return pltpu.make_async_copy(xs_hbm.at[pl.ds(b * bm, rows), :], xs_buf.at[slot, pl.ds(0, rows), :],
                                     xs_sem.at[slot])

    def ys_copy(b, slot, rows):
        return pltpu.make_async_copy(ys_buf.at[slot, pl.ds(0, rows), :], ys_hbm.at[pl.ds(b * bm, rows), :],
                                     ys_sem.at[slot])

    def xs_start(b):
        for_row_count(b, lambda rows: xs_copy(b, b % _XS_SLOTS, rows).start())

    def xs_wait(b):
        for_row_count(b, lambda rows: xs_copy(b, b % _XS_SLOTS, rows).wait())

    def ys_start(b):
        for_row_count(b, lambda rows: ys_copy(b, b % _YS_SLOTS, rows).start())

    def ys_wait(b):
        for_row_count(b, lambda rows: ys_copy(b, b % _YS_SLOTS, rows).wait())

    def w_copies(j, slot):
        e = uexp_ref[j]
        return (pltpu.make_async_copy(wg_hbm.at[e], wg_buf.at[slot], w_sem.at[slot, 0]),
                pltpu.make_async_copy(wu_hbm.at[e], wu_buf.at[slot], w_sem.at[slot, 1]),
                pltpu.make_async_copy(wd_hbm.at[e], wd_buf.at[slot], w_sem.at[slot, 2]))

    @pl.when(i == 0)
    def _():
        xs_buf[...] = jnp.zeros(xs_buf.shape, xs_buf.dtype)
        for s in range(_XS_SLOTS - 1):
            @pl.when(s < nused)
            def _():
                xs_start(s)
        for s in range(_W_AHEAD):
            @pl.when(s < nexp)
            def _():
                for cp in w_copies(s, s):
                    cp.start()

    def fetch(b):
        ahead = b + _XS_SLOTS - 1

        @pl.when(ahead < nused)
        def _():
            xs_start(ahead)

        j = ord_ref[b]

        @pl.when(first_ref[b] == 1)
        def _():
            for cp in w_copies(j, j % _W_SLOTS):
                cp.wait()
            nxt = j + _W_AHEAD

            @pl.when(nxt < nexp)
            def _():
                for cp in w_copies(nxt, nxt % _W_SLOTS):
                    cp.start()

        xs_wait(b)

    def gate_up(b):
        ws = ord_ref[b] % _W_SLOTS
        lo, hi = _unpack_row_words(xs_buf[b % _XS_SLOTS])
        g = (jnp.dot(lo, wg_buf[ws, 0:_ROW_WORDS, :], preferred_element_type=F32)
             + jnp.dot(hi, wg_buf[ws, _ROW_WORDS:, :], preferred_element_type=F32))
        u = (jnp.dot(lo, wu_buf[ws, 0:_ROW_WORDS, :], preferred_element_type=F32)
             + jnp.dot(hi, wu_buf[ws, _ROW_WORDS:, :], preferred_element_type=F32))
        return g * jax.nn.sigmoid(g) * u

    def down(b, act):
        y = jnp.dot(act, wd_buf[ord_ref[b] % _W_SLOTS], preferred_element_type=F32)
        ys_buf[b % _YS_SLOTS] = _pack_row_words(y[:, 0:_ROW_WORDS], y[:, _ROW_WORDS:])
        ys_start(b)

    @pl.when(jnp.logical_and(i >= _YS_SLOTS + 1, i <= nused))
    def _():
        ys_wait(i - 1 - _YS_SLOTS)

    @pl.when(i == 0)
    def _():
        fetch(i)
        act_scr[...] = gate_up(i)

    @pl.when(jnp.logical_and(i >= 1, i < nused))
    def _():
        fetch(i)
        prev = act_scr[...]
        act_scr[...] = gate_up(i)
        down(i - 1, prev)

    @pl.when(i == nused)
    def _():
        down(i - 1, act_scr[...])
        ys_wait(i - 1)

        @pl.when(i >= 2)
        def _():
            ys_wait(i - 2)


def _experts(first, ordinal, quanta, uexp, meta, xs, w_gate, w_up, w_down):
    p = xs.shape[0]
    d = w_gate.shape[1]
    nb = p // BM_EXPERT
    anyspec = pl.BlockSpec(memory_space=pl.ANY)
    return pl.pallas_call(
        _expert_kernel,
        grid_spec=pltpu.PrefetchScalarGridSpec(
            num_scalar_prefetch=5,
            grid=(nb + 1,),
            in_specs=[anyspec, anyspec, anyspec, anyspec],
            out_specs=anyspec,
            scratch_shapes=[pltpu.VMEM((_XS_SLOTS, BM_EXPERT, _ROW_WORDS), U32),
                            pltpu.VMEM((_YS_SLOTS, BM_EXPERT, _ROW_WORDS), U32),
                            pltpu.VMEM((_W_SLOTS, d, EXPERT_DIM), F32),
                            pltpu.VMEM((_W_SLOTS, d, EXPERT_DIM), F32),
                            pltpu.VMEM((_W_SLOTS, EXPERT_DIM, d), F32),
                            pltpu.VMEM((BM_EXPERT, EXPERT_DIM), F32),
                            pltpu.SemaphoreType.DMA((_XS_SLOTS,)), pltpu.SemaphoreType.DMA((_YS_SLOTS,)),
                            pltpu.SemaphoreType.DMA((_W_SLOTS, 3))]),
        out_shape=jax.ShapeDtypeStruct((p, _ROW_WORDS), U32),
        compiler_params=_params(("arbitrary",)),
        name="experts",
    )(first, ordinal, quanta, uexp, meta, xs, w_gate, w_up, w_down)


SC_GATHER_ROWS = 64


def _gather_rows_sc(table, idx):
    n = idx.shape[0]
    width = table.shape[1]
    rows = SC_GATHER_ROWS
    per_worker, active = _sc_worker_split(n // rows)
    assert per_worker % 2 == 0 or per_worker == 1
    mesh = plsc.VectorSubcoreMesh(core_axis_name="c", subcore_axis_name="s")

    @functools.partial(
        pl.kernel, mesh=mesh, out_type=jax.ShapeDtypeStruct((n, width), table.dtype),
        scratch_types=[pltpu.VMEM((per_worker, rows), I32), pltpu.VMEM((2, rows, width), table.dtype),
                       pltpu.SemaphoreType.DMA((2,))])
    def gather(table_hbm, idx_hbm, out_hbm, idx_v, rows_v, sem):
        wid = lax.axis_index("s") * SC_CORES + lax.axis_index("c")

        def fetch(c, b):
            return pltpu.make_async_copy(table_hbm.at[idx_v.at[c]], rows_v.at[b], sem.at[b])

        @pl.when(wid < active)
        def _():
            first = wid * per_worker
            pltpu.sync_copy(idx_hbm.at[pl.ds(first, per_worker)], idx_v)
            fetch(0, 0).start()

            @pl.loop(0, per_worker, step=2)
            def _(c):
                for b in range(min(2, per_worker)):
                    cur = c + b

                    @pl.when(cur + 1 < per_worker)
                    def _():
                        fetch(cur + 1, 1 - b).start()

                    fetch(cur, b).wait()
                    pltpu.sync_copy(rows_v.at[b], out_hbm.at[pl.ds((first + cur) * rows, rows)])

    return gather(table, idx.reshape(n // rows, rows))


def _combine_kernel(wts_ref, yg_ref, h2_ref, x1_ref, mod_ref, wsg_ref, wsu_ref, wsd_ref, gpost_ref, *rest):
    o_ref = rest[-1]
    h2 = h2_ref[...]
    g = jnp.dot(h2, wsg_ref[...], preferred_element_type=F32)
    u = jnp.dot(h2, wsu_ref[...], preferred_element_type=F32)
    f = jnp.dot((g * jax.nn.sigmoid(g) * u).astype(BF16), wsd_ref[...], preferred_element_type=F32)

    wts = wts_ref[...]
    los = [f[:, sl * LANES:(sl + 1) * LANES] for sl in range(_ROW_SLABS)]
    his = [f[:, _ROW_WORDS + sl * LANES:_ROW_WORDS + (sl + 1) * LANES] for sl in range(_ROW_SLABS)]
    for k in range(TOP_K):
        wk = wts[:, k:k + 1]
        for sl in range(_ROW_SLABS):
            lo, hi = _unpack_row_words(yg_ref[k, :, sl * LANES:(sl + 1) * LANES])
            los[sl] = los[sl] + wk * lo
            his[sl] = his[sl] + wk * hi
    f = jnp.concatenate(los + his, axis=1)
    g2 = mod_ref[0, 5:6, :]
    o_ref[...] = x1_ref[...] + g2 * _rms(f, gpost_ref[...])


def _combine(wts_t, yg, h2, x1, mod3, wsg, wsu, wsd, gpost, seq, first_tile, partial_out):
    t, d = x1.shape
    tm = min(TM_COMBINE, seq)
    tpb = seq // tm
    full = lambda a: pl.BlockSpec(a.shape, lambda i: (0,) * a.ndim)
    row = lambda w: pl.BlockSpec((tm, w), lambda i: (i + first_tile, 0))
    args = [wts_t, yg, h2, x1, mod3, wsg, wsu, wsd, gpost]
    in_specs = [row(TOP_K), pl.BlockSpec((TOP_K, tm, _ROW_WORDS), lambda i: (0, i, 0)), row(d), row(d),
                pl.BlockSpec((1, 6, d), lambda i: ((i + first_tile) // tpb, 0, 0)),
                full(wsg), full(wsu), full(wsd), full(gpost)]
    aliases = {}
    if partial_out is not None:
        aliases = {len(args): 0}
        args.append(partial_out)
        in_specs.append(pl.BlockSpec(memory_space=pl.ANY))
    return pl.pallas_call(
        _combine_kernel,
        grid=(yg.shape[1] // tm,),
        in_specs=in_specs,
        out_specs=row(d),
        out_shape=jax.ShapeDtypeStruct((t, d), F32),
        input_output_aliases=aliases,
        compiler_params=_params(("arbitrary",)),
        name="combine",
    )(*args)


def _pack_weights(w_in, w_uq, w_ukv):
    d = w_in.shape[0]
    half = QK_ROPE_DIM // 2
    z = lambda n, c: jnp.zeros((n, c), F32)
    o = Q_LORA_RANK + KV_LORA_RANK
    kr = w_in[:, o:o + QK_ROPE_DIM]
    kr_grp = jnp.concatenate([z(d, QK_NOPE_DIM), kr, z(d, HEAD_PAD - QK_NOPE_DIM - QK_ROPE_DIM)], axis=1)
    kr_rot = jnp.concatenate([z(d, QK_NOPE_DIM), -kr[:, half:], kr[:, :half],
                              z(d, HEAD_PAD - QK_NOPE_DIM - QK_ROPE_DIM)], axis=1)
    win_p = jnp.concatenate([w_in[:, :o], kr_grp, kr_rot, w_in[:, o + QK_ROPE_DIM:]], axis=1)

    scale = float(QK_NOPE_DIM + QK_ROPE_DIM) ** -0.5 * float(np.log2(np.e))
    r = Q_LORA_RANK
    qd = QK_NOPE_DIM + QK_ROPE_DIM
    q_grp, q_rot = [], []
    for h in range(MLA_HEADS):
        nope = w_uq[:, h * qd:h * qd + QK_NOPE_DIM]
        rope = w_uq[:, h * qd + QK_NOPE_DIM:(h + 1) * qd]
        pad = z(r, HEAD_PAD - qd)
        q_grp.append(jnp.concatenate([nope, rope, pad], axis=1))
        q_rot.append(jnp.concatenate([z(r, QK_NOPE_DIM), -rope[:, half:], rope[:, :half], pad], axis=1))
    wuq_p = jnp.concatenate(q_grp + q_rot, axis=1) * scale

    c = KV_LORA_RANK
    kd = QK_NOPE_DIM + V_HEAD_DIM
    k_grp, v_grp = [], []
    for h in range(MLA_HEADS):
        k_grp.append(jnp.concatenate([w_ukv[:, h * kd:h * kd + QK_NOPE_DIM], z(c, HEAD_PAD - QK_NOPE_DIM)], axis=1))
        v_grp.append(jnp.concatenate([w_ukv[:, h * kd + QK_NOPE_DIM:(h + 1) * kd], z(c, HEAD_PAD - V_HEAD_DIM)], axis=1))
    wukv_p = jnp.concatenate(k_grp + v_grp, axis=1)
    return win_p.astype(BF16), wuq_p.astype(BF16), wukv_p.astype(BF16)


def _rope_inputs(positions):
    inv = 1.0 / (ROPE_THETA ** (jnp.arange(0, QK_ROPE_DIM, 2, dtype=F32) / QK_ROPE_DIM))
    freq = jnp.tile(inv, LANES // inv.shape[0]).reshape(1, LANES)
    pos = jnp.repeat(positions.astype(F32).reshape(-1, _ROPE_PACK), QK_ROPE_DIM, axis=1)
    return pos, freq


def _layer(x2, c, pos, freq, batch, seq, w_ada, b_ada, g_pre_mix, w_in, g_q_lat, w_uq, g_kv_lat, w_ukv,
           w_conv, g_attn_out, g_conv_out, w_out, g_post_mix, g_pre_ffn, w_router, b_router,
           w_gate, w_up, w_down, w_sh_gate, w_sh_up, w_sh_down, g_post_ffn):
    t, d = x2.shape
    r1 = lambda a: a.reshape(1, -1)

    c_pad = jnp.zeros((SUBLANES, d), F32).at[:batch].set(c)
    mod = _ada(c_pad, w_ada, r1(b_ada))[:batch]
    mod3 = mod.reshape(batch, 6, d)

    win_p, wuq_p, wukv_p = _pack_weights(w_in, w_uq, w_ukv)
    vone = jnp.zeros((1, HEAD_PAD), F32).at[0, V_HEAD_DIM].set(1.0)
    q, k, v, yc = _mix_in(x2, mod3, r1(g_pre_mix), win_p, r1(g_q_lat), wuq_p, r1(g_kv_lat), wukv_p,
                          vone, w_conv, r1(g_conv_out), pos, freq, seq)
    attn = _attention(q, k, v, batch, seq)
    x1, h2, h2p, idx, wts, rank, cnt = _mix_out(
        attn, yc, x2, mod3, r1(g_attn_out), w_out.astype(BF16), r1(g_post_mix), r1(g_pre_ffn),
        w_router.T, b_router.reshape(-1, 1), seq)

    counts = cnt[:, 0].astype(I32)
    padded = ((counts + BM_EXPERT - 1) // BM_EXPERT) * BM_EXPERT
    pad_end = jnp.cumsum(padded)
    pad_start = pad_end - padded
    m = t * TOP_K
    nb = (m + N_EXPERTS * (BM_EXPERT - 1)) // BM_EXPERT
    nused = pad_end[-1] // BM_EXPERT
    bidx = jnp.arange(nb, dtype=I32)
    blk_exp = jnp.sum((pad_end[None, :] <= (bidx * BM_EXPERT)[:, None]).astype(I32), axis=1)
    first = ((bidx < nused) & ((bidx == 0) | (blk_exp != jnp.roll(blk_exp, 1)))).astype(I32)
    ordinal = jnp.maximum(jnp.cumsum(first) - 1, 0).astype(I32)
    blk_e = jnp.minimum(blk_exp, N_EXPERTS - 1)
    live = jnp.clip(counts[blk_e] - (bidx * BM_EXPERT - pad_start[blk_e]), 1, BM_EXPERT)
    quanta = ((live + _ROW_QUANTUM - 1) // _ROW_QUANTUM).astype(I32)
    seen = jnp.cumsum((counts > 0).astype(I32))
    uexp = jnp.minimum(jnp.sum((seen[None, :] <= jnp.arange(N_EXPERTS, dtype=I32)[:, None]).astype(I32), axis=1),
                       N_EXPERTS - 1).astype(I32)
    meta = jnp.stack([nused, seen[-1]]).astype(I32)

    dest = _dest(idx, rank, pad_start.astype(F32).reshape(-1, 1))
    xs = _scatter_rows_sc(h2p, dest.reshape(TOP_K, -1, SC_ROWS).transpose(1, 0, 2), nb * BM_EXPERT)
    xs = _padfill((pad_start + counts).astype(I32), ((-counts) % _ROW_QUANTUM).astype(I32), xs)
    ys = _experts(first, ordinal, quanta, uexp, meta, xs, w_gate, w_up, w_down)
    tm_c = min(TM_COMBINE, seq)
    half_rows = (t // 2) * TOP_K
    even_split = half_rows % (2 * SC_GATHER_ROWS * SC_CORES * SC_SUBCORES) == 0
    parts = 2 if t % (2 * tm_c) == 0 and even_split else 1
    tp = t // parts
    wsg, wsu, wsd = w_sh_gate.astype(BF16), w_sh_up.astype(BF16), w_sh_down.astype(BF16)
    out = None
    for part in range(parts):
        part_dest = dest[:, part * tp:(part + 1) * tp].reshape(-1)
        yg = _gather_rows_sc(ys, part_dest).reshape(TOP_K, tp, _ROW_WORDS)
        out = _combine(wts.T, yg, h2, x1, mod3, wsg, wsu, wsd, r1(g_post_ffn), seq, part * tp // tm_c, out)
    return out


def kernel(x, c, positions, w_ada, b_ada, g_pre_mix, w_in, g_q_lat, w_uq, g_kv_lat, w_ukv, w_conv, g_attn_out, g_conv_out, w_out, g_post_mix, g_pre_ffn, w_router, b_router, w_gate, w_up, w_down, w_sh_gate, w_sh_up, w_sh_down, g_post_ffn):
    batch, seq, d = x.shape
    pos, freq = _rope_inputs(positions)
    x2 = x.reshape(batch * seq, d)
    for l in range(w_ada.shape[0]):
        x2 = _layer(x2, c, pos, freq, batch, seq, w_ada[l], b_ada[l], g_pre_mix[l], w_in[l], g_q_lat[l],
                    w_uq[l], g_kv_lat[l], w_ukv[l], w_conv[l], g_attn_out[l], g_conv_out[l], w_out[l],
                    g_post_mix[l], g_pre_ffn[l], w_router[l], b_router[l], w_gate[l], w_up[l], w_down[l],
                    w_sh_gate[l], w_sh_up[l], w_sh_down[l], g_post_ffn[l])
    return x2.reshape(batch, seq, d)
```

```python
import functools

import jax
import jax.numpy as jnp
import numpy as np
from jax import lax
from jax.experimental import pallas as pl
from jax.experimental.pallas import tpu as pltpu
from jax.experimental.pallas import tpu_sc as plsc

F32 = jnp.float32
BF16 = jnp.bfloat16
I32 = jnp.int32
U32 = jnp.uint32

CHUNK = 64
MLA_HEADS = 8
QK_NOPE_DIM = 64
QK_ROPE_DIM = 32
V_HEAD_DIM = 64
Q_LORA_RANK = 384
KV_LORA_RANK = 256
ROPE_THETA = 10000.0
CONV_WIDTH = 3
N_EXPERTS = 256
TOP_K = 8
N_EXPERT_GROUPS = 8
TOPK_GROUPS = 4
EXPERT_DIM = 256
ROUTED_SCALE = 2.5
EPS = 1e-6

LANES = 128
SUBLANES = 8
HEAD_PAD = LANES
VMEM_LIMIT_BYTES = 56 * 1024 * 1024

TM_IN = 1024
TQ_ATTN = 512
TM_OUT = 512
TM_DEST = 2048
BM_EXPERT = 256
TM_COMBINE = 512

NEG_INF = float("-inf")


def _rms(x, g):
    return x * lax.rsqrt(jnp.mean(x * x, axis=-1, keepdims=True) + EPS) * g


_HI_MASK = np.uint32(0xFFFF0000)
_ROW_WORDS = 512
_ROW_SLABS = _ROW_WORDS // LANES


def _pack_row_words(lo, hi):
    lo_w = lax.bitcast_convert_type(lo.astype(BF16).astype(F32), U32) >> 16
    hi_w = lax.bitcast_convert_type(hi.astype(BF16).astype(F32), U32) & _HI_MASK
    return lo_w | hi_w


def _unpack_row_words(w):
    return (lax.bitcast_convert_type(w << 16, F32), lax.bitcast_convert_type(w & _HI_MASK, F32))


def _params(sem):
    return pltpu.CompilerParams(dimension_semantics=sem, vmem_limit_bytes=VMEM_LIMIT_BYTES)


def _ada_kernel(c_ref, w_ref, b_ref, o_ref):
    c = c_ref[...]
    s = c * jax.nn.sigmoid(c)
    o_ref[...] = jnp.dot(s, w_ref[...], preferred_element_type=F32,
                         precision=lax.Precision.HIGHEST) + b_ref[...]


def _ada(c_pad, w, b):
    rows, d = c_pad.shape
    n = w.shape[1]
    tn = 1536
    return pl.pallas_call(
        _ada_kernel,
        grid=(n // tn,),
        in_specs=[pl.BlockSpec((rows, d), lambda j: (0, 0)),
                  pl.BlockSpec((d, tn), lambda j: (0, j)),
                  pl.BlockSpec((1, tn), lambda j: (0, j))],
        out_specs=pl.BlockSpec((rows, tn), lambda j: (0, j)),
        out_shape=jax.ShapeDtypeStruct((rows, n), F32),
        compiler_params=_params(("arbitrary",)),
        name="ada",
    )(c_pad, w, b)


_CQ0, _CQ1 = 0, Q_LORA_RANK
_CKV0, _CKV1 = _CQ1, _CQ1 + KV_LORA_RANK
_KR0, _KR1 = _CKV1, _CKV1 + 2 * HEAD_PAD
_CONV_DIM = 512
_GB0 = _KR1
_GC0 = _GB0 + _CONV_DIM
_XV0 = _GC0 + _CONV_DIM
_WIN_COLS = _XV0 + _CONV_DIM
_QW = MLA_HEADS * HEAD_PAD
_ROPE_PACK = LANES // QK_ROPE_DIM


def _mix_in_kernel(tiles_per_batch, x_ref, mod_ref, gpre_ref, win_ref, gq_ref, wuq_ref, gkv_ref,
                   wukv_ref, vone_ref, wconv_ref, gconv_ref, pos_ref, freq_ref,
                   q_ref, k_ref, v_ref, yc_ref, h_scr, u_scr, cos_scr, sin_scr):
    i = pl.program_id(0)
    tm = x_ref.shape[0]
    sh1 = mod_ref[0, 0:1, :]
    sc1 = mod_ref[0, 1:2, :]
    h = _rms(x_ref[...], gpre_ref[...]) * (1.0 + sc1) + sh1
    h_scr[...] = h.astype(BF16)
    ang4 = pos_ref[...] * freq_ref[...]
    cos4 = jnp.cos(ang4)
    sin4 = jnp.sin(ang4)
    lane = lax.broadcasted_iota(I32, ang4.shape, 1)
    on_rope = jnp.logical_and(lane >= QK_NOPE_DIM, lane < QK_NOPE_DIM + QK_ROPE_DIM)
    for g in range(_ROPE_PACK):
        shift = (QK_NOPE_DIM - QK_ROPE_DIM * g) % LANES
        cg = cos4 if shift == 0 else pltpu.roll(cos4, shift, axis=1)
        sg = sin4 if shift == 0 else pltpu.roll(sin4, shift, axis=1)
        cos_scr[pl.ds(g, tm // _ROPE_PACK, stride=_ROPE_PACK), :] = jnp.where(on_rope, cg, 1.0)
        sin_scr[pl.ds(g, tm // _ROPE_PACK, stride=_ROPE_PACK), :] = jnp.where(on_rope, sg, 0.0)
    cos = cos_scr[...]
    sin = sin_scr[...]

    cq = jnp.dot(h_scr[...], win_ref[:, _CQ0:_CQ1], preferred_element_type=F32)
    cqn = _rms(cq, gq_ref[...]).astype(BF16)
    qq = jnp.dot(cqn, wuq_ref[...], preferred_element_type=F32)
    for hd in range(MLA_HEADS):
        lo = hd * HEAD_PAD
        qh = qq[:, lo:lo + HEAD_PAD] * cos + qq[:, _QW + lo:_QW + lo + HEAD_PAD] * sin
        q_ref[:, lo:lo + HEAD_PAD] = qh.astype(BF16)

    ckv = jnp.dot(h_scr[...], win_ref[:, _CKV0:_CKV1], preferred_element_type=F32)
    ckvn = _rms(ckv, gkv_ref[...]).astype(BF16)
    kv = jnp.dot(ckvn, wukv_ref[...], preferred_element_type=F32)
    krr = jnp.dot(h_scr[...], win_ref[:, _KR0:_KR1], preferred_element_type=F32)
    kr = krr[:, 0:HEAD_PAD] * cos + krr[:, HEAD_PAD:2 * HEAD_PAD] * sin
    vone = vone_ref[...]
    for hd in range(MLA_HEADS):
        lo = hd * HEAD_PAD
        k_ref[:, lo:lo + HEAD_PAD] = (kv[:, lo:lo + HEAD_PAD] + kr).astype(BF16)
        v_ref[:, lo:lo + HEAD_PAD] = (kv[:, _QW + lo:_QW + lo + HEAD_PAD] + vone).astype(BF16)

    gates = jnp.dot(h_scr[...], win_ref[:, _GB0:_WIN_COLS], preferred_element_type=F32)
    gb = gates[:, 0:_CONV_DIM]
    u = gates[:, _CONV_DIM:2 * _CONV_DIM] * gates[:, 2 * _CONV_DIM:3 * _CONV_DIM]
    prev = u_scr[tm:tm + SUBLANES, :]
    first = (i % tiles_per_batch) == 0
    u_scr[0:SUBLANES, :] = jnp.where(first, jnp.zeros_like(prev), prev)
    u_scr[SUBLANES:tm + SUBLANES, :] = u
    um1 = u_scr[SUBLANES - 1:tm + SUBLANES - 1, :]
    um2 = u_scr[SUBLANES - 2:tm + SUBLANES - 2, :]
    assert wconv_ref.shape[0] == CONV_WIDTH
    conv = wconv_ref[0:1, :] * um2 + wconv_ref[1:2, :] * um1 + wconv_ref[2:3, :] * u
    yc_ref[...] = _rms(gb * conv, gconv_ref[...]).astype(BF16)


def _mix_in(x2, mod3, gpre, win_p, gq, wuq_p, gkv, wukv_p, vone, wconv, gconv, pos, freq, seq):
    t, d = x2.shape
    tm = min(TM_IN, seq)
    tpb = seq // tm
    full = lambda a: pl.BlockSpec(a.shape, lambda i: (0,) * a.ndim)
    row = lambda w: pl.BlockSpec((tm, w), lambda i: (i, 0))
    return pl.pallas_call(
        functools.partial(_mix_in_kernel, tpb),
        grid=(t // tm,),
        in_specs=[row(d),
                  pl.BlockSpec((1, 6, d), lambda i: (i // tpb, 0, 0)),
                  full(gpre), full(win_p), full(gq), full(wuq_p), full(gkv), full(wukv_p),
                  full(vone), full(wconv), full(gconv),
                  pl.BlockSpec((tm // _ROPE_PACK, LANES), lambda i: (i, 0)), full(freq)],
        out_specs=[row(_QW), row(_QW), row(_QW), row(_CONV_DIM)],
        out_shape=[jax.ShapeDtypeStruct((t, _QW), BF16), jax.ShapeDtypeStruct((t, _QW), BF16),
                   jax.ShapeDtypeStruct((t, _QW), BF16), jax.ShapeDtypeStruct((t, _CONV_DIM), BF16)],
        scratch_shapes=[pltpu.VMEM((tm, d), BF16), pltpu.VMEM((tm + SUBLANES, _CONV_DIM), F32),
                        pltpu.VMEM((tm, HEAD_PAD), F32), pltpu.VMEM((tm, HEAD_PAD), F32)],
        compiler_params=_params(("arbitrary",)),
        name="mix_in",
    )(x2, mod3, gpre, win_p, gq, wuq_p, gkv, wukv_p, vone, wconv, gconv, pos, freq)


_HEADS_PER_STEP = 2


def _attn_kernel(tq, q_ref, k_ref, v_ref, o_ref, s_scr, mrun_scr, mb_scr, acc_scr):
    tk = tq
    nq = q_ref.shape[0] // tq
    lane_groups = tk // LANES
    heads = range(_HEADS_PER_STEP)
    lanes = [slice(hh * HEAD_PAD, (hh + 1) * HEAD_PAD) for hh in heads]

    def tile_max(s):
        m = s[:, 0:LANES]
        for g in range(1, lane_groups):
            m = jnp.maximum(m, s[:, g * LANES:(g + 1) * LANES])
        return m

    def scores(hh, qi, kv):
        off = pl.multiple_of(kv * tk, tk)
        return lax.dot_general(q_ref[qi * tq:(qi + 1) * tq, lanes[hh]], k_ref[pl.ds(off, tk), lanes[hh]],
                               (((1,), (1,)), ((), ())), preferred_element_type=F32)

    rc = lax.broadcasted_iota(I32, (tq, tk), 0) // CHUNK
    cc = lax.broadcasted_iota(I32, (tq, tk), 1) // CHUNK

    def diagonal(qi):
        for hh in heads:
            s = jnp.where(cc <= rc, scores(hh, qi, qi), NEG_INF)
            s_scr[hh, qi] = s
            m_row = jnp.max(jnp.maximum(mrun_scr[hh], tile_max(s)), axis=1, keepdims=True)
            mb_scr[hh] = jnp.broadcast_to(m_row, (tq, LANES))

    mrun_scr[...] = jnp.full(mrun_scr.shape, NEG_INF, F32)
    diagonal(0)
    for qi in range(nq):
        has_next = qi + 1 < nq
        acc_scr[...] = jnp.zeros(acc_scr.shape, F32)
        if has_next:
            mrun_scr[...] = jnp.full(mrun_scr.shape, NEG_INF, F32)

        def body(kv, carry, qi=qi, has_next=has_next):
            off = pl.multiple_of(kv * tk, tk)
            for hh in heads:
                mb = mb_scr[hh]
                p = jnp.concatenate(
                    [jnp.exp2(s_scr[hh, kv, :, g * LANES:(g + 1) * LANES] - mb) for g in range(lane_groups)],
                    axis=1).astype(BF16)
                acc_scr[hh] += jnp.dot(p, v_ref[pl.ds(off, tk), lanes[hh]], preferred_element_type=F32)
                if has_next:
                    s = scores(hh, qi + 1, kv)
                    s_scr[hh, kv] = s
                    mrun_scr[hh] = jnp.maximum(mrun_scr[hh], tile_max(s))
            return carry

        lax.fori_loop(0, qi + 1, body, 0, unroll=4)
        for hh in heads:
            acc = acc_scr[hh]
            o = acc[:, 0:V_HEAD_DIM] / acc[:, V_HEAD_DIM:V_HEAD_DIM + 1]
            o_ref[qi * tq:(qi + 1) * tq, hh * V_HEAD_DIM:(hh + 1) * V_HEAD_DIM] = o.astype(BF16)
        if has_next:
            diagonal(qi + 1)


def _attention(q, k, v, batch, seq):
    t = q.shape[0]
    tq = min(TQ_ATTN, seq)
    nq = seq // tq
    hw = _HEADS_PER_STEP * HEAD_PAD
    ow = _HEADS_PER_STEP * V_HEAD_DIM
    blk = lambda w: pl.BlockSpec((seq, w), lambda b, j: (b, j))
    return pl.pallas_call(
        functools.partial(_attn_kernel, tq),
        grid=(batch, MLA_HEADS // _HEADS_PER_STEP),
        in_specs=[blk(hw), blk(hw), blk(hw)],
        out_specs=blk(ow),
        out_shape=jax.ShapeDtypeStruct((t, MLA_HEADS * V_HEAD_DIM), BF16),
        scratch_shapes=[pltpu.VMEM((_HEADS_PER_STEP, nq, tq, tq), F32),
                        pltpu.VMEM((_HEADS_PER_STEP, tq, LANES), F32),
                        pltpu.VMEM((_HEADS_PER_STEP, tq, LANES), F32),
                        pltpu.VMEM((_HEADS_PER_STEP, tq, HEAD_PAD), F32)],
        compiler_params=_params(("arbitrary", "arbitrary")),
        name="attn",
    )(q, k, v)


_GROUP_SIZE = N_EXPERTS // N_EXPERT_GROUPS
_BIG = 1.0e9


def _mix_out_kernel(attn_ref, yc_ref, x_ref, mod_ref, gattn_ref, wout_ref, gpost_ref, gpre2_ref,
                    wrt_ref, br_ref, x1_ref, h2_ref, h2p_ref, idx_ref, wts_ref, rank_ref, cnt_ref,
                    carry_scr, scores_scr, sel_scr):
    i = pl.program_id(0)
    n_tiles = pl.num_programs(0) - 1

    @pl.when(i == 0)
    def _():
        carry_scr[...] = jnp.zeros(carry_scr.shape, F32)
        scores, sel = _project_tile(attn_ref, yc_ref, x_ref, mod_ref, gattn_ref, wout_ref, gpost_ref,
                                    gpre2_ref, wrt_ref, br_ref, x1_ref, h2_ref, h2p_ref)
        scores_scr[0] = scores
        sel_scr[0] = sel

    @pl.when(jnp.logical_and(i >= 1, i < n_tiles))
    def _():
        prev_scores = scores_scr[(i - 1) % 2]
        prev_sel = sel_scr[(i - 1) % 2]
        scores, sel = _project_tile(attn_ref, yc_ref, x_ref, mod_ref, gattn_ref, wout_ref, gpost_ref,
                                    gpre2_ref, wrt_ref, br_ref, x1_ref, h2_ref, h2p_ref)
        scores_scr[i % 2] = scores
        sel_scr[i % 2] = sel
        _route_tile(prev_scores, prev_sel, idx_ref, wts_ref, rank_ref, cnt_ref, carry_scr)

    @pl.when(i == n_tiles)
    def _():
        _route_tile(scores_scr[(i - 1) % 2], sel_scr[(i - 1) % 2], idx_ref, wts_ref, rank_ref, cnt_ref,
                    carry_scr)


def _project_tile(attn_ref, yc_ref, x_ref, mod_ref, gattn_ref, wout_ref, gpost_ref, gpre2_ref,
                  wrt_ref, br_ref, x1_ref, h2_ref, h2p_ref):
    half = attn_ref.shape[1]
    an = _rms(attn_ref[...].astype(F32), gattn_ref[...]).astype(BF16)
    mix = (jnp.dot(an, wout_ref[0:half, :], preferred_element_type=F32)
           + jnp.dot(yc_ref[...], wout_ref[half:, :], preferred_element_type=F32))
    g1 = mod_ref[0, 2:3, :]
    sh2 = mod_ref[0, 3:4, :]
    sc2 = mod_ref[0, 4:5, :]
    x1 = x_ref[...] + g1 * _rms(mix, gpost_ref[...])
    x1_ref[...] = x1
    h2 = _rms(x1, gpre2_ref[...]) * (1.0 + sc2) + sh2
    h2_ref[...] = h2.astype(BF16)
    h2p_ref[...] = _pack_row_words(h2[:, 0:_ROW_WORDS], h2[:, _ROW_WORDS:])

    logits = lax.dot_general(wrt_ref[...], h2, (((1,), (1,)), ((), ())),
                             preferred_element_type=F32, precision=lax.Precision.HIGHEST)
    scores = jax.nn.sigmoid(logits)
    return scores, scores + br_ref[...]


def _route_tile(scores, sel, idx_ref, wts_ref, rank_ref, cnt_ref, carry_scr):
    tm = scores.shape[1]
    row = lax.broadcasted_iota(I32, (N_EXPERTS, tm), 0).astype(F32)

    gscore = []
    rw = lax.broadcasted_iota(I32, (_GROUP_SIZE, tm), 0).astype(F32)
    for g in range(N_EXPERT_GROUPS):
        blk = sel[g * _GROUP_SIZE:(g + 1) * _GROUP_SIZE, :]
        m1 = jnp.max(blk, axis=0, keepdims=True)
        i1 = jnp.min(jnp.where(blk == m1, rw, _BIG), axis=0, keepdims=True)
        m2 = jnp.max(jnp.where(rw == i1, NEG_INF, blk), axis=0, keepdims=True)
        gscore.append(m1 + m2)

    gkeep = [jnp.zeros((1, tm), F32) for _ in range(N_EXPERT_GROUPS)]
    for _ in range(TOPK_GROUPS):
        mg = functools.reduce(jnp.maximum, gscore)
        ig = functools.reduce(jnp.minimum, [jnp.where(gscore[g] == mg, float(g), _BIG)
                                            for g in range(N_EXPERT_GROUPS)])
        for g in range(N_EXPERT_GROUPS):
            hit = ig == float(g)
            gkeep[g] = jnp.where(hit, 1.0, gkeep[g])
            gscore[g] = jnp.where(hit, NEG_INF, gscore[g])
    n_slabs = N_EXPERTS // SUBLANES
    slabs_per_group = _GROUP_SIZE // SUBLANES
    sub = lax.broadcasted_iota(I32, (SUBLANES, tm), 0).astype(F32)
    first_rows = [jnp.where(gkeep[j // slabs_per_group] > 0.0, sel[j * SUBLANES:(j + 1) * SUBLANES, :], NEG_INF)
                  for j in range(n_slabs)]
    cur_rows = list(first_rows)
    krow = lax.broadcasted_iota(I32, (TOP_K, tm), 0)
    idx_rows = []
    idx_f = jnp.zeros((TOP_K, tm), F32)
    sc_k = jnp.zeros((TOP_K, tm), F32)
    sc_sum = jnp.zeros((1, tm), F32)
    prev = None
    for k in range(TOP_K):
        best = jnp.full((SUBLANES, tm), NEG_INF, F32)
        best_slab = jnp.zeros((SUBLANES, tm), F32)
        best_score = jnp.zeros((SUBLANES, tm), F32)
        for j in range(n_slabs):
            if prev is not None:
                cur_rows[j] = jnp.where(sub == prev - float(j * SUBLANES), NEG_INF, cur_rows[j])
            better = cur_rows[j] > best
            best = jnp.where(better, cur_rows[j], best)
            best_slab = jnp.where(better, float(j), best_slab)
            best_score = jnp.where(better, scores[j * SUBLANES:(j + 1) * SUBLANES, :], best_score)
        best_idx = best_slab * float(SUBLANES) + sub
        m = jnp.max(best, axis=0, keepdims=True)
        ik = jnp.min(jnp.where(best == m, best_idx, _BIG), axis=0, keepdims=True)
        sk = jnp.sum(jnp.where(best_idx == ik, best_score, 0.0), axis=0, keepdims=True)
        prev = ik
        idx_rows.append(ik)
        idx_f = jnp.where(krow == k, ik, idx_f)
        sc_k = jnp.where(krow == k, sk, sc_k)
        sc_sum = sc_sum + sk
    wts_ref[...] = sc_k / sc_sum * ROUTED_SCALE
    idx_ref[...] = idx_f.astype(I32)
    onehot = jnp.concatenate(
        [jnp.where(jnp.where(sub == prev - float(j * SUBLANES), NEG_INF, cur_rows[j]) != first_rows[j], 1.0, 0.0)
         for j in range(n_slabs)], axis=0)

    tri = (lax.broadcasted_iota(I32, (tm, tm), 0) < lax.broadcasted_iota(I32, (tm, tm), 1))
    excl = jnp.dot(onehot.astype(BF16), tri.astype(BF16), preferred_element_type=F32)
    rank_e = carry_scr[:, 0:1] + excl
    rank_k = jnp.zeros((TOP_K, tm), F32)
    for k in range(TOP_K):
        hit = row == idx_rows[k]
        rk = jnp.sum(jnp.where(hit, rank_e, 0.0), axis=0, keepdims=True)
        rank_k = jnp.where(krow == k, rk, rank_k)
    rank_ref[...] = rank_k.astype(I32)
    carry_scr[...] = carry_scr[...] + jnp.sum(onehot, axis=1, keepdims=True)
    cnt_ref[...] = carry_scr[...]


def _mix_out(attn, yc, x2, mod3, gattn, wout, gpost, gpre2, wrt, br, seq):
    t, d = x2.shape
    tm = min(TM_OUT, seq)
    tpb = seq // tm
    n_tiles = t // tm
    last = n_tiles - 1
    full = lambda a: pl.BlockSpec(a.shape, lambda i: (0,) * a.ndim)
    row = lambda w: pl.BlockSpec((tm, w), lambda i: (jnp.minimum(i, last), 0))
    col = pl.BlockSpec((TOP_K, tm), lambda i: (0, jnp.maximum(i - 1, 0)))
    return pl.pallas_call(
        _mix_out_kernel,
        grid=(n_tiles + 1,),
        in_specs=[row(attn.shape[1]), row(yc.shape[1]), row(d),
                  pl.BlockSpec((1, 6, d), lambda i: (jnp.minimum(i, last) // tpb, 0, 0)),
                  full(gattn), full(wout), full(gpost), full(gpre2), full(wrt), full(br)],
        out_specs=[row(d), row(d), row(_ROW_WORDS), col, col, col,
                   pl.BlockSpec((N_EXPERTS, LANES), lambda i: (0, 0))],
        out_shape=[jax.ShapeDtypeStruct((t, d), F32), jax.ShapeDtypeStruct((t, d), BF16),
                   jax.ShapeDtypeStruct((t, _ROW_WORDS), U32),
                   jax.ShapeDtypeStruct((TOP_K, t), I32), jax.ShapeDtypeStruct((TOP_K, t), F32),
                   jax.ShapeDtypeStruct((TOP_K, t), I32),
                   jax.ShapeDtypeStruct((N_EXPERTS, LANES), F32)],
        scratch_shapes=[pltpu.VMEM((N_EXPERTS, LANES), F32), pltpu.VMEM((2, N_EXPERTS, tm), F32),
                        pltpu.VMEM((2, N_EXPERTS, tm), F32)],
        compiler_params=_params(("arbitrary",)),
        name="mix_out",
    )(attn, yc, x2, mod3, gattn, wout, gpost, gpre2, wrt, br)


def _dest_kernel(idx_ref, rank_ref, pstart_ref, dest_ref):
    tm = idx_ref.shape[1]
    row = lax.broadcasted_iota(I32, (N_EXPERTS, tm), 0)
    krow = lax.broadcasted_iota(I32, (TOP_K, tm), 0)
    pstart = pstart_ref[...]
    idx = idx_ref[...]
    out = jnp.zeros((TOP_K, tm), F32)
    for k in range(TOP_K):
        hit = row == idx[k:k + 1, :]
        base = jnp.sum(jnp.where(hit, pstart, 0.0), axis=0, keepdims=True)
        out = jnp.where(krow == k, base, out)
    dest_ref[...] = out.astype(I32) + rank_ref[...]


def _dest(idx, rank, pstart):
    t = idx.shape[1]
    tm = min(TM_DEST, t)
    col = pl.BlockSpec((TOP_K, tm), lambda i: (0, i))
    return pl.pallas_call(
        _dest_kernel,
        grid=(t // tm,),
        in_specs=[col, col, pl.BlockSpec((N_EXPERTS, 1), lambda i: (0, 0))],
        out_specs=col,
        out_shape=jax.ShapeDtypeStruct((TOP_K, t), I32),
        compiler_params=_params(("arbitrary",)),
        name="dest",
    )(idx, rank, pstart)


_PAD_CHUNKS = tuple(BM_EXPERT >> s for s in range(1, BM_EXPERT.bit_length()))


SC_CORES = 2
SC_SUBCORES = 16
SC_ROWS = 128


def _sc_worker_split(n_chunks):
    workers = SC_CORES * SC_SUBCORES
    per_worker = max(1, n_chunks // workers)
    active = n_chunks // per_worker
    assert active * per_worker == n_chunks and active <= workers
    return per_worker, active


def _scatter_rows_sc(rows, idx3d, n_out):
    n, width = rows.shape
    n_chunks = n // SC_ROWS
    per_worker, active = _sc_worker_split(n_chunks)
    mesh = plsc.VectorSubcoreMesh(core_axis_name="c", subcore_axis_name="s")

    @functools.partial(
        pl.kernel, mesh=mesh, out_type=jax.ShapeDtypeStruct((n_out, width), rows.dtype),
        scratch_types=[pltpu.VMEM((TOP_K, SC_ROWS), I32), pltpu.VMEM((SC_ROWS, width), rows.dtype),
                       pltpu.SemaphoreType.DMA])
    def scatter(rows_hbm, idx_hbm, out_hbm, idx_v, rows_v, sem):
        wid = lax.axis_index("s") * SC_CORES + lax.axis_index("c")

        @pl.when(wid < active)
        def _():
            @pl.loop(0, per_worker)
            def _(c):
                chunk = wid * per_worker + c
                pltpu.sync_copy(rows_hbm.at[pl.ds(chunk * SC_ROWS, SC_ROWS)], rows_v)
                pltpu.sync_copy(idx_hbm.at[chunk], idx_v)
                copies = [pltpu.make_async_copy(rows_v, out_hbm.at[idx_v.at[k]], sem) for k in range(TOP_K)]
                for cp in copies:
                    cp.start()
                for cp in copies:
                    cp.wait()

    return scatter(rows, idx3d)


def _padfill_kernel(pad_from_ref, pad_n_ref, xs_in_hbm, xs_hbm, zero_scr, pad_sem):
    del xs_in_hbm
    zero_scr[...] = jnp.zeros(zero_scr.shape, zero_scr.dtype)

    def pad_copies(e, act):
        n = pad_n_ref[e]
        base = pad_from_ref[e]

        def single_rows(start, count):
            for j in range(SUBLANES - 1):
                @pl.when(j < count)
                def _():
                    act(pltpu.make_async_copy(zero_scr.at[pl.ds(0, 1), :],
                                              xs_hbm.at[pl.ds(start + j, 1), :], pad_sem))

        head = jnp.minimum(n, (SUBLANES - (base & (SUBLANES - 1))) & (SUBLANES - 1))
        single_rows(base, head)
        rest = n - head
        mid = base + head
        for rows in _PAD_CHUNKS:
            if rows >= SUBLANES:
                @pl.when((rest & rows) != 0)
                def _():
                    start = pl.multiple_of(mid + (rest & ~(2 * rows - 1)), SUBLANES)
                    act(pltpu.make_async_copy(zero_scr.at[pl.ds(0, rows), :],
                                              xs_hbm.at[pl.ds(start, rows), :], pad_sem))
        single_rows(mid + (rest & ~(SUBLANES - 1)), rest & (SUBLANES - 1))

    def issue_pad(e, carry):
        pad_copies(e, lambda cp: cp.start())
        return carry

    def drain_pad(e, carry):
        pad_copies(e, lambda cp: cp.wait())
        return carry

    lax.fori_loop(0, N_EXPERTS, issue_pad, 0)
    lax.fori_loop(0, N_EXPERTS, drain_pad, 0)


def _padfill(pad_from, pad_n, xs):
    return pl.pallas_call(
        _padfill_kernel,
        grid_spec=pltpu.PrefetchScalarGridSpec(
            num_scalar_prefetch=2,
            grid=(1,),
            in_specs=[pl.BlockSpec(memory_space=pl.ANY)],
            out_specs=pl.BlockSpec(memory_space=pl.ANY),
            scratch_shapes=[pltpu.VMEM((BM_EXPERT // 2, _ROW_WORDS), U32), pltpu.SemaphoreType.DMA]),
        out_shape=jax.ShapeDtypeStruct(xs.shape, xs.dtype),
        input_output_aliases={2: 0},
        compiler_params=_params(("arbitrary",)),
        name="padfill",
    )(pad_from, pad_n, xs)


_XS_SLOTS = 4
_YS_SLOTS = 2
_W_SLOTS = 4
_W_AHEAD = 2


def _expert_kernel(first_ref, ord_ref, uexp_ref, meta_ref, xs_hbm, wg_hbm, wu_hbm, wd_hbm, ys_hbm,
                   xs_buf, ys_buf, wg_buf, wu_buf, wd_buf, act_scr, xs_sem, ys_sem, w_sem):
    i = pl.program_id(0)
    nused = meta_ref[0]
    nexp = meta_ref[1]
    bm = xs_buf.shape[1]

    def xs_copy(b, slot):
        return pltpu.make_async_copy(xs_hbm.at[pl.ds(b * bm, bm), :], xs_buf.at[slot], xs_sem.at[slot])

    def ys_copy(b, slot):
        return pltpu.make_async_copy(ys_buf.at[slot], ys_hbm.at[pl.ds(b * bm, bm), :], ys_sem.at[slot])

    def w_copies(j, slot):
        e = uexp_ref[j]
        return (pltpu.make_async_copy(wg_hbm.at[e], wg_buf.at[slot], w_sem.at[slot, 0]),
                pltpu.make_async_copy(wu_hbm.at[e], wu_buf.at[slot], w_sem.at[slot, 1]),
                pltpu.make_async_copy(wd_hbm.at[e], wd_buf.at[slot], w_sem.at[slot, 2]))

    @pl.when(i == 0)
    def _():
        for s in range(_XS_SLOTS - 1):
            @pl.when(s < nused)
            def _():
                xs_copy(s, s).start()
        for s in range(_W_AHEAD):
            @pl.when(s < nexp)
            def _():
                for cp in w_copies(s, s):
                    cp.start()

    def fetch(b):
        ahead = b + _XS_SLOTS - 1

        @pl.when(ahead < nused)
        def _():
            xs_copy(ahead, ahead % _XS_SLOTS).start()

        j = ord_ref[b]

        @pl.when(first_ref[b] == 1)
        def _():
            for cp in w_copies(j, j % _W_SLOTS):
                cp.wait()
            nxt = j + _W_AHEAD

            @pl.when(nxt < nexp)
            def _():
                for cp in w_copies(nxt, nxt % _W_SLOTS):
                    cp.start()

        xs_copy(b, b % _XS_SLOTS).wait()

    def gate_up(b):
        ws = ord_ref[b] % _W_SLOTS
        lo, hi = _unpack_row_words(xs_buf[b % _XS_SLOTS])
        g = (jnp.dot(lo, wg_buf[ws, 0:_ROW_WORDS, :], preferred_element_type=F32)
             + jnp.dot(hi, wg_buf[ws, _ROW_WORDS:, :], preferred_element_type=F32))
        u = (jnp.dot(lo, wu_buf[ws, 0:_ROW_WORDS, :], preferred_element_type=F32)
             + jnp.dot(hi, wu_buf[ws, _ROW_WORDS:, :], preferred_element_type=F32))
        return g * jax.nn.sigmoid(g) * u

    def down(b, act):
        y = jnp.dot(act, wd_buf[ord_ref[b] % _W_SLOTS], preferred_element_type=F32)
        oslot = b % _YS_SLOTS
        ys_buf[oslot] = _pack_row_words(y[:, 0:_ROW_WORDS], y[:, _ROW_WORDS:])
        ys_copy(b, oslot).start()

    @pl.when(jnp.logical_and(i >= _YS_SLOTS + 1, i <= nused))
    def _():
        ys_copy(i - 1 - _YS_SLOTS, (i - 1) % _YS_SLOTS).wait()

    @pl.when(i == 0)
    def _():
        fetch(i)
        act_scr[...] = gate_up(i)

    @pl.when(jnp.logical_and(i >= 1, i < nused))
    def _():
        fetch(i)
        prev = act_scr[...]
        act_scr[...] = gate_up(i)
        down(i - 1, prev)

    @pl.when(i == nused)
    def _():
        down(i - 1, act_scr[...])
        ys_copy(i - 1, (i - 1) % _YS_SLOTS).wait()

        @pl.when(i >= 2)
        def _():
            ys_copy(i - 2, (i - 2) % _YS_SLOTS).wait()


def _experts(first, ordinal, uexp, meta, xs, w_gate, w_up, w_down):
    p = xs.shape[0]
    d = w_gate.shape[1]
    nb = p // BM_EXPERT
    anyspec = pl.BlockSpec(memory_space=pl.ANY)
    return pl.pallas_call(
        _expert_kernel,
        grid_spec=pltpu.PrefetchScalarGridSpec(
            num_scalar_prefetch=4,
            grid=(nb + 1,),
            in_specs=[anyspec, anyspec, anyspec, anyspec],
            out_specs=anyspec,
            scratch_shapes=[pltpu.VMEM((_XS_SLOTS, BM_EXPERT, _ROW_WORDS), U32),
                            pltpu.VMEM((_YS_SLOTS, BM_EXPERT, _ROW_WORDS), U32),
                            pltpu.VMEM((_W_SLOTS, d, EXPERT_DIM), F32),
                            pltpu.VMEM((_W_SLOTS, d, EXPERT_DIM), F32),
                            pltpu.VMEM((_W_SLOTS, EXPERT_DIM, d), F32),
                            pltpu.VMEM((BM_EXPERT, EXPERT_DIM), F32),
                            pltpu.SemaphoreType.DMA((_XS_SLOTS,)), pltpu.SemaphoreType.DMA((_YS_SLOTS,)),
                            pltpu.SemaphoreType.DMA((_W_SLOTS, 3))]),
        out_shape=jax.ShapeDtypeStruct((p, _ROW_WORDS), U32),
        compiler_params=_params(("arbitrary",)),
        name="experts",
    )(first, ordinal, uexp, meta, xs, w_gate, w_up, w_down)


SC_GATHER_ROWS = 64


def _gather_rows_sc(table, idx):
    n = idx.shape[0]
    width = table.shape[1]
    rows = SC_GATHER_ROWS
    per_worker, active = _sc_worker_split(n // rows)
    assert per_worker % 2 == 0 or per_worker == 1
    mesh = plsc.VectorSubcoreMesh(core_axis_name="c", subcore_axis_name="s")

    @functools.partial(
        pl.kernel, mesh=mesh, out_type=jax.ShapeDtypeStruct((n, width), table.dtype),
        scratch_types=[pltpu.VMEM((per_worker, rows), I32), pltpu.VMEM((2, rows, width), table.dtype),
                       pltpu.SemaphoreType.DMA((2,))])
    def gather(table_hbm, idx_hbm, out_hbm, idx_v, rows_v, sem):
        wid = lax.axis_index("s") * SC_CORES + lax.axis_index("c")

        def fetch(c, b):
            return pltpu.make_async_copy(table_hbm.at[idx_v.at[c]], rows_v.at[b], sem.at[b])

        @pl.when(wid < active)
        def _():
            first = wid * per_worker
            pltpu.sync_copy(idx_hbm.at[pl.ds(first, per_worker)], idx_v)
            fetch(0, 0).start()

            @pl.loop(0, per_worker, step=2)
            def _(c):
                for b in range(min(2, per_worker)):
                    cur = c + b

                    @pl.when(cur + 1 < per_worker)
                    def _():
                        fetch(cur + 1, 1 - b).start()

                    fetch(cur, b).wait()
                    pltpu.sync_copy(rows_v.at[b], out_hbm.at[pl.ds((first + cur) * rows, rows)])

    return gather(table, idx.reshape(n // rows, rows))


def _combine_kernel(wts_ref, yg_ref, h2_ref, x1_ref, mod_ref, wsg_ref, wsu_ref, wsd_ref, gpost_ref, *rest):
    o_ref = rest[-1]
    h2 = h2_ref[...]
    g = jnp.dot(h2, wsg_ref[...], preferred_element_type=F32)
    u = jnp.dot(h2, wsu_ref[...], preferred_element_type=F32)
    f = jnp.dot((g * jax.nn.sigmoid(g) * u).astype(BF16), wsd_ref[...], preferred_element_type=F32)

    wts = wts_ref[...]
    los = [f[:, sl * LANES:(sl + 1) * LANES] for sl in range(_ROW_SLABS)]
    his = [f[:, _ROW_WORDS + sl * LANES:_ROW_WORDS + (sl + 1) * LANES] for sl in range(_ROW_SLABS)]
    for k in range(TOP_K):
        wk = wts[:, k:k + 1]
        for sl in range(_ROW_SLABS):
            lo, hi = _unpack_row_words(yg_ref[k, :, sl * LANES:(sl + 1) * LANES])
            los[sl] = los[sl] + wk * lo
            his[sl] = his[sl] + wk * hi
    f = jnp.concatenate(los + his, axis=1)
    g2 = mod_ref[0, 5:6, :]
    o_ref[...] = x1_ref[...] + g2 * _rms(f, gpost_ref[...])


def _combine(wts_t, yg, h2, x1, mod3, wsg, wsu, wsd, gpost, seq, first_tile, partial_out):
    t, d = x1.shape
    tm = min(TM_COMBINE, seq)
    tpb = seq // tm
    full = lambda a: pl.BlockSpec(a.shape, lambda i: (0,) * a.ndim)
    row = lambda w: pl.BlockSpec((tm, w), lambda i: (i + first_tile, 0))
    args = [wts_t, yg, h2, x1, mod3, wsg, wsu, wsd, gpost]
    in_specs = [row(TOP_K), pl.BlockSpec((TOP_K, tm, _ROW_WORDS), lambda i: (0, i, 0)), row(d), row(d),
                pl.BlockSpec((1, 6, d), lambda i: ((i + first_tile) // tpb, 0, 0)),
                full(wsg), full(wsu), full(wsd), full(gpost)]
    aliases = {}
    if partial_out is not None:
        aliases = {len(args): 0}
        args.append(partial_out)
        in_specs.append(pl.BlockSpec(memory_space=pl.ANY))
    return pl.pallas_call(
        _combine_kernel,
        grid=(yg.shape[1] // tm,),
        in_specs=in_specs,
        out_specs=row(d),
        out_shape=jax.ShapeDtypeStruct((t, d), F32),
        input_output_aliases=aliases,
        compiler_params=_params(("arbitrary",)),
        name="combine",
    )(*args)


def _pack_weights(w_in, w_uq, w_ukv):
    d = w_in.shape[0]
    half = QK_ROPE_DIM // 2
    z = lambda n, c: jnp.zeros((n, c), F32)
    o = Q_LORA_RANK + KV_LORA_RANK
    kr = w_in[:, o:o + QK_ROPE_DIM]
    kr_grp = jnp.concatenate([z(d, QK_NOPE_DIM), kr, z(d, HEAD_PAD - QK_NOPE_DIM - QK_ROPE_DIM)], axis=1)
    kr_rot = jnp.concatenate([z(d, QK_NOPE_DIM), -kr[:, half:], kr[:, :half],
                              z(d, HEAD_PAD - QK_NOPE_DIM - QK_ROPE_DIM)], axis=1)
    win_p = jnp.concatenate([w_in[:, :o], kr_grp, kr_rot, w_in[:, o + QK_ROPE_DIM:]], axis=1)

    scale = float(QK_NOPE_DIM + QK_ROPE_DIM) ** -0.5 * float(np.log2(np.e))
    r = Q_LORA_RANK
    qd = QK_NOPE_DIM + QK_ROPE_DIM
    q_grp, q_rot = [], []
    for h in range(MLA_HEADS):
        nope = w_uq[:, h * qd:h * qd + QK_NOPE_DIM]
        rope = w_uq[:, h * qd + QK_NOPE_DIM:(h + 1) * qd]
        pad = z(r, HEAD_PAD - qd)
        q_grp.append(jnp.concatenate([nope, rope, pad], axis=1))
        q_rot.append(jnp.concatenate([z(r, QK_NOPE_DIM), -rope[:, half:], rope[:, :half], pad], axis=1))
    wuq_p = jnp.concatenate(q_grp + q_rot, axis=1) * scale

    c = KV_LORA_RANK
    kd = QK_NOPE_DIM + V_HEAD_DIM
    k_grp, v_grp = [], []
    for h in range(MLA_HEADS):
        k_grp.append(jnp.concatenate([w_ukv[:, h * kd:h * kd + QK_NOPE_DIM], z(c, HEAD_PAD - QK_NOPE_DIM)], axis=1))
        v_grp.append(jnp.concatenate([w_ukv[:, h * kd + QK_NOPE_DIM:(h + 1) * kd], z(c, HEAD_PAD - V_HEAD_DIM)], axis=1))
    wukv_p = jnp.concatenate(k_grp + v_grp, axis=1)
    return win_p.astype(BF16), wuq_p.astype(BF16), wukv_p.astype(BF16)


def _rope_inputs(positions):
    inv = 1.0 / (ROPE_THETA ** (jnp.arange(0, QK_ROPE_DIM, 2, dtype=F32) / QK_ROPE_DIM))
    freq = jnp.tile(inv, LANES // inv.shape[0]).reshape(1, LANES)
    pos = jnp.repeat(positions.astype(F32).reshape(-1, _ROPE_PACK), QK_ROPE_DIM, axis=1)
    return pos, freq


def _layer(x2, c, pos, freq, batch, seq, w_ada, b_ada, g_pre_mix, w_in, g_q_lat, w_uq, g_kv_lat, w_ukv,
           w_conv, g_attn_out, g_conv_out, w_out, g_post_mix, g_pre_ffn, w_router, b_router,
           w_gate, w_up, w_down, w_sh_gate, w_sh_up, w_sh_down, g_post_ffn):
    t, d = x2.shape
    r1 = lambda a: a.reshape(1, -1)

    c_pad = jnp.zeros((SUBLANES, d), F32).at[:batch].set(c)
    mod = _ada(c_pad, w_ada, r1(b_ada))[:batch]
    mod3 = mod.reshape(batch, 6, d)

    win_p, wuq_p, wukv_p = _pack_weights(w_in, w_uq, w_ukv)
    vone = jnp.zeros((1, HEAD_PAD), F32).at[0, V_HEAD_DIM].set(1.0)
    q, k, v, yc = _mix_in(x2, mod3, r1(g_pre_mix), win_p, r1(g_q_lat), wuq_p, r1(g_kv_lat), wukv_p,
                          vone, w_conv, r1(g_conv_out), pos, freq, seq)
    attn = _attention(q, k, v, batch, seq)
    x1, h2, h2p, idx, wts, rank, cnt = _mix_out(
        attn, yc, x2, mod3, r1(g_attn_out), w_out.astype(BF16), r1(g_post_mix), r1(g_pre_ffn),
        w_router.T, b_router.reshape(-1, 1), seq)

    counts = cnt[:, 0].astype(I32)
    padded = ((counts + BM_EXPERT - 1) // BM_EXPERT) * BM_EXPERT
    pad_end = jnp.cumsum(padded)
    pad_start = pad_end - padded
    m = t * TOP_K
    nb = (m + N_EXPERTS * (BM_EXPERT - 1)) // BM_EXPERT
    nused = pad_end[-1] // BM_EXPERT
    bidx = jnp.arange(nb, dtype=I32)
    blk_exp = jnp.sum((pad_end[None, :] <= (bidx * BM_EXPERT)[:, None]).astype(I32), axis=1)
    first = ((bidx < nused) & ((bidx == 0) | (blk_exp != jnp.roll(blk_exp, 1)))).astype(I32)
    ordinal = jnp.maximum(jnp.cumsum(first) - 1, 0).astype(I32)
    seen = jnp.cumsum((counts > 0).astype(I32))
    uexp = jnp.minimum(jnp.sum((seen[None, :] <= jnp.arange(N_EXPERTS, dtype=I32)[:, None]).astype(I32), axis=1),
                       N_EXPERTS - 1).astype(I32)
    meta = jnp.stack([nused, seen[-1]]).astype(I32)

    dest = _dest(idx, rank, pad_start.astype(F32).reshape(-1, 1))
    xs = _scatter_rows_sc(h2p, dest.reshape(TOP_K, -1, SC_ROWS).transpose(1, 0, 2), nb * BM_EXPERT)
    xs = _padfill((pad_start + counts).astype(I32), (padded - counts).astype(I32), xs)
    ys = _experts(first, ordinal, uexp, meta, xs, w_gate, w_up, w_down)
    tm_c = min(TM_COMBINE, seq)
    half_rows = (t // 2) * TOP_K
    even_split = half_rows % (2 * SC_GATHER_ROWS * SC_CORES * SC_SUBCORES) == 0
    parts = 2 if t % (2 * tm_c) == 0 and even_split else 1
    tp = t // parts
    wsg, wsu, wsd = w_sh_gate.astype(BF16), w_sh_up.astype(BF16), w_sh_down.astype(BF16)
    out = None
    for part in range(parts):
        part_dest = dest[:, part * tp:(part + 1) * tp].reshape(-1)
        yg = _gather_rows_sc(ys, part_dest).reshape(TOP_K, tp, _ROW_WORDS)
        out = _combine(wts.T, yg, h2, x1, mod3, wsg, wsu, wsd, r1(g_post_ffn), seq, part * tp // tm_c, out)
    return out


def kernel(x, c, positions, w_ada, b_ada, g_pre_mix, w_in, g_q_lat, w_uq, g_kv_lat, w_ukv, w_conv, g_attn_out, g_conv_out, w_out, g_post_mix, g_pre_ffn, w_router, b_router, w_gate, w_up, w_down, w_sh_gate, w_sh_up, w_sh_down, g_post_ffn):
    batch, seq, d = x.shape
    pos, freq = _rope_inputs(positions)
    x2 = x.reshape(batch * seq, d)
    for l in range(w_ada.shape[0]):
        x2 = _layer(x2, c, pos, freq, batch, seq, w_ada[l], b_ada[l], g_pre_mix[l], w_in[l], g_q_lat[l],
                    w_uq[l], g_kv_lat[l], w_ukv[l], w_conv[l], g_attn_out[l], g_conv_out[l], w_out[l],
                    g_post_mix[l], g_pre_ffn[l], w_router[l], b_router[l], w_gate[l], w_up[l], w_down[l],
                    w_sh_gate[l], w_sh_up[l], w_sh_down[l], g_post_ffn[l])
    return x2.reshape(batch, seq, d)
```

```python
import functools

import jax
import jax.numpy as jnp
import numpy as np
from jax import lax
from jax.experimental import pallas as pl
from jax.experimental.pallas import tpu as pltpu
from jax.experimental.pallas import tpu_sc as plsc

F32 = jnp.float32
BF16 = jnp.bfloat16
I32 = jnp.int32
U32 = jnp.uint32

CHUNK = 64
MLA_HEADS = 8
QK_NOPE_DIM = 64
QK_ROPE_DIM = 32
V_HEAD_DIM = 64
Q_LORA_RANK = 384
KV_LORA_RANK = 256
ROPE_THETA = 10000.0
CONV_WIDTH = 3
N_EXPERTS = 256
TOP_K = 8
N_EXPERT_GROUPS = 8
TOPK_GROUPS = 4
EXPERT_DIM = 256
ROUTED_SCALE = 2.5
EPS = 1e-6

LANES = 128
SUBLANES = 8
HEAD_PAD = LANES
VMEM_LIMIT_BYTES = 56 * 1024 * 1024

TM_IN = 1024
TQ_ATTN = 512
TM_OUT = 512
TM_DEST = 2048
BM_EXPERT = 256
TM_COMBINE = 512

NEG_INF = float("-inf")


def _rms(x, g):
    return x * lax.rsqrt(jnp.mean(x * x, axis=-1, keepdims=True) + EPS) * g


_HI_MASK = np.uint32(0xFFFF0000)
_ROW_WORDS = 512
_ROW_SLABS = _ROW_WORDS // LANES


def _pack_row_words(lo, hi):
    lo_w = lax.bitcast_convert_type(lo.astype(BF16).astype(F32), U32) >> 16
    hi_w = lax.bitcast_convert_type(hi.astype(BF16).astype(F32), U32) & _HI_MASK
    return lo_w | hi_w


def _unpack_row_words(w):
    return (lax.bitcast_convert_type(w << 16, F32), lax.bitcast_convert_type(w & _HI_MASK, F32))


def _params(sem):
    return pltpu.CompilerParams(dimension_semantics=sem, vmem_limit_bytes=VMEM_LIMIT_BYTES)


def _ada_kernel(c_ref, w_ref, b_ref, o_ref):
    c = c_ref[...]
    s = c * jax.nn.sigmoid(c)
    o_ref[...] = jnp.dot(s, w_ref[...], preferred_element_type=F32,
                         precision=lax.Precision.HIGHEST) + b_ref[...]


def _ada(c_pad, w, b):
    rows, d = c_pad.shape
    n = w.shape[1]
    tn = 1536
    return pl.pallas_call(
        _ada_kernel,
        grid=(n // tn,),
        in_specs=[pl.BlockSpec((rows, d), lambda j: (0, 0)),
                  pl.BlockSpec((d, tn), lambda j: (0, j)),
                  pl.BlockSpec((1, tn), lambda j: (0, j))],
        out_specs=pl.BlockSpec((rows, tn), lambda j: (0, j)),
        out_shape=jax.ShapeDtypeStruct((rows, n), F32),
        compiler_params=_params(("arbitrary",)),
        name="ada",
    )(c_pad, w, b)


_CQ0, _CQ1 = 0, Q_LORA_RANK
_CKV0, _CKV1 = _CQ1, _CQ1 + KV_LORA_RANK
_KR0, _KR1 = _CKV1, _CKV1 + 2 * HEAD_PAD
_CONV_DIM = 512
_GB0 = _KR1
_GC0 = _GB0 + _CONV_DIM
_XV0 = _GC0 + _CONV_DIM
_WIN_COLS = _XV0 + _CONV_DIM
_QW = MLA_HEADS * HEAD_PAD
_ROPE_PACK = LANES // QK_ROPE_DIM


def _mix_in_kernel(tiles_per_batch, x_ref, mod_ref, gpre_ref, win_ref, gq_ref, wuq_ref, gkv_ref,
                   wukv_ref, vone_ref, wconv_ref, gconv_ref, pos_ref, freq_ref,
                   q_ref, k_ref, v_ref, yc_ref, h_scr, u_scr, cos_scr, sin_scr):
    i = pl.program_id(0)
    tm = x_ref.shape[0]
    sh1 = mod_ref[0, 0:1, :]
    sc1 = mod_ref[0, 1:2, :]
    h = _rms(x_ref[...], gpre_ref[...]) * (1.0 + sc1) + sh1
    h_scr[...] = h.astype(BF16)
    ang4 = pos_ref[...] * freq_ref[...]
    cos4 = jnp.cos(ang4)
    sin4 = jnp.sin(ang4)
    lane = lax.broadcasted_iota(I32, ang4.shape, 1)
    on_rope = jnp.logical_and(lane >= QK_NOPE_DIM, lane < QK_NOPE_DIM + QK_ROPE_DIM)
    for g in range(_ROPE_PACK):
        shift = (QK_NOPE_DIM - QK_ROPE_DIM * g) % LANES
        cg = cos4 if shift == 0 else pltpu.roll(cos4, shift, axis=1)
        sg = sin4 if shift == 0 else pltpu.roll(sin4, shift, axis=1)
        cos_scr[pl.ds(g, tm // _ROPE_PACK, stride=_ROPE_PACK), :] = jnp.where(on_rope, cg, 1.0)
        sin_scr[pl.ds(g, tm // _ROPE_PACK, stride=_ROPE_PACK), :] = jnp.where(on_rope, sg, 0.0)
    cos = cos_scr[...]
    sin = sin_scr[...]

    cq = jnp.dot(h_scr[...], win_ref[:, _CQ0:_CQ1], preferred_element_type=F32)
    cqn = _rms(cq, gq_ref[...]).astype(BF16)
    qq = jnp.dot(cqn, wuq_ref[...], preferred_element_type=F32)
    for hd in range(MLA_HEADS):
        lo = hd * HEAD_PAD
        qh = qq[:, lo:lo + HEAD_PAD] * cos + qq[:, _QW + lo:_QW + lo + HEAD_PAD] * sin
        q_ref[:, lo:lo + HEAD_PAD] = qh.astype(BF16)

    ckv = jnp.dot(h_scr[...], win_ref[:, _CKV0:_CKV1], preferred_element_type=F32)
    ckvn = _rms(ckv, gkv_ref[...]).astype(BF16)
    kv = jnp.dot(ckvn, wukv_ref[...], preferred_element_type=F32)
    krr = jnp.dot(h_scr[...], win_ref[:, _KR0:_KR1], preferred_element_type=F32)
    kr = krr[:, 0:HEAD_PAD] * cos + krr[:, HEAD_PAD:2 * HEAD_PAD] * sin
    vone = vone_ref[...]
    for hd in range(MLA_HEADS):
        lo = hd * HEAD_PAD
        k_ref[:, lo:lo + HEAD_PAD] = (kv[:, lo:lo + HEAD_PAD] + kr).astype(BF16)
        v_ref[:, lo:lo + HEAD_PAD] = (kv[:, _QW + lo:_QW + lo + HEAD_PAD] + vone).astype(BF16)

    gb = jnp.dot(h_scr[...], win_ref[:, _GB0:_GC0], preferred_element_type=F32)
    gc = jnp.dot(h_scr[...], win_ref[:, _GC0:_XV0], preferred_element_type=F32)
    xv = jnp.dot(h_scr[...], win_ref[:, _XV0:_WIN_COLS], preferred_element_type=F32)
    u = gc * xv
    prev = u_scr[tm:tm + SUBLANES, :]
    first = (i % tiles_per_batch) == 0
    u_scr[0:SUBLANES, :] = jnp.where(first, jnp.zeros_like(prev), prev)
    u_scr[SUBLANES:tm + SUBLANES, :] = u
    um1 = u_scr[SUBLANES - 1:tm + SUBLANES - 1, :]
    um2 = u_scr[SUBLANES - 2:tm + SUBLANES - 2, :]
    assert wconv_ref.shape[0] == CONV_WIDTH
    conv = wconv_ref[0:1, :] * um2 + wconv_ref[1:2, :] * um1 + wconv_ref[2:3, :] * u
    yc_ref[...] = _rms(gb * conv, gconv_ref[...]).astype(BF16)


def _mix_in(x2, mod3, gpre, win_p, gq, wuq_p, gkv, wukv_p, vone, wconv, gconv, pos, freq, seq):
    t, d = x2.shape
    tm = min(TM_IN, seq)
    tpb = seq // tm
    full = lambda a: pl.BlockSpec(a.shape, lambda i: (0,) * a.ndim)
    row = lambda w: pl.BlockSpec((tm, w), lambda i: (i, 0))
    return pl.pallas_call(
        functools.partial(_mix_in_kernel, tpb),
        grid=(t // tm,),
        in_specs=[row(d),
                  pl.BlockSpec((1, 6, d), lambda i: (i // tpb, 0, 0)),
                  full(gpre), full(win_p), full(gq), full(wuq_p), full(gkv), full(wukv_p),
                  full(vone), full(wconv), full(gconv),
                  pl.BlockSpec((tm // _ROPE_PACK, LANES), lambda i: (i, 0)), full(freq)],
        out_specs=[row(_QW), row(_QW), row(_QW), row(_CONV_DIM)],
        out_shape=[jax.ShapeDtypeStruct((t, _QW), BF16), jax.ShapeDtypeStruct((t, _QW), BF16),
                   jax.ShapeDtypeStruct((t, _QW), BF16), jax.ShapeDtypeStruct((t, _CONV_DIM), BF16)],
        scratch_shapes=[pltpu.VMEM((tm, d), BF16), pltpu.VMEM((tm + SUBLANES, _CONV_DIM), F32),
                        pltpu.VMEM((tm, HEAD_PAD), F32), pltpu.VMEM((tm, HEAD_PAD), F32)],
        compiler_params=_params(("arbitrary",)),
        name="mix_in",
    )(x2, mod3, gpre, win_p, gq, wuq_p, gkv, wukv_p, vone, wconv, gconv, pos, freq)


_HEADS_PER_STEP = 2


def _attn_kernel(tq, q_ref, k_ref, v_ref, o_ref, s_scr, mrun_scr, mb_scr, acc_scr):
    tk = tq
    nq = q_ref.shape[0] // tq
    lane_groups = tk // LANES
    heads = range(_HEADS_PER_STEP)
    lanes = [slice(hh * HEAD_PAD, (hh + 1) * HEAD_PAD) for hh in heads]

    def tile_max(s):
        m = s[:, 0:LANES]
        for g in range(1, lane_groups):
            m = jnp.maximum(m, s[:, g * LANES:(g + 1) * LANES])
        return m

    def scores(hh, qi, kv):
        off = pl.multiple_of(kv * tk, tk)
        return lax.dot_general(q_ref[qi * tq:(qi + 1) * tq, lanes[hh]], k_ref[pl.ds(off, tk), lanes[hh]],
                               (((1,), (1,)), ((), ())), preferred_element_type=F32)

    rc = lax.broadcasted_iota(I32, (tq, tk), 0) // CHUNK
    cc = lax.broadcasted_iota(I32, (tq, tk), 1) // CHUNK

    def diagonal(qi):
        for hh in heads:
            s = jnp.where(cc <= rc, scores(hh, qi, qi), NEG_INF)
            s_scr[hh, qi] = s
            m_row = jnp.max(jnp.maximum(mrun_scr[hh], tile_max(s)), axis=1, keepdims=True)
            mb_scr[hh] = jnp.broadcast_to(m_row, (tq, LANES))

    mrun_scr[...] = jnp.full(mrun_scr.shape, NEG_INF, F32)
    diagonal(0)
    for qi in range(nq):
        has_next = qi + 1 < nq
        acc_scr[...] = jnp.zeros(acc_scr.shape, F32)
        if has_next:
            mrun_scr[...] = jnp.full(mrun_scr.shape, NEG_INF, F32)

        def body(kv, carry, qi=qi, has_next=has_next):
            off = pl.multiple_of(kv * tk, tk)
            for hh in heads:
                mb = mb_scr[hh]
                p = jnp.concatenate(
                    [jnp.exp2(s_scr[hh, kv, :, g * LANES:(g + 1) * LANES] - mb) for g in range(lane_groups)],
                    axis=1).astype(BF16)
                acc_scr[hh] += jnp.dot(p, v_ref[pl.ds(off, tk), lanes[hh]], preferred_element_type=F32)
                if has_next:
                    s = scores(hh, qi + 1, kv)
                    s_scr[hh, kv] = s
                    mrun_scr[hh] = jnp.maximum(mrun_scr[hh], tile_max(s))
            return carry

        lax.fori_loop(0, qi + 1, body, 0, unroll=4)
        for hh in heads:
            acc = acc_scr[hh]
            o = acc[:, 0:V_HEAD_DIM] / acc[:, V_HEAD_DIM:V_HEAD_DIM + 1]
            o_ref[qi * tq:(qi + 1) * tq, hh * V_HEAD_DIM:(hh + 1) * V_HEAD_DIM] = o.astype(BF16)
        if has_next:
            diagonal(qi + 1)


def _attention(q, k, v, batch, seq):
    t = q.shape[0]
    tq = min(TQ_ATTN, seq)
    nq = seq // tq
    hw = _HEADS_PER_STEP * HEAD_PAD
    ow = _HEADS_PER_STEP * V_HEAD_DIM
    blk = lambda w: pl.BlockSpec((seq, w), lambda b, j: (b, j))
    return pl.pallas_call(
        functools.partial(_attn_kernel, tq),
        grid=(batch, MLA_HEADS // _HEADS_PER_STEP),
        in_specs=[blk(hw), blk(hw), blk(hw)],
        out_specs=blk(ow),
        out_shape=jax.ShapeDtypeStruct((t, MLA_HEADS * V_HEAD_DIM), BF16),
        scratch_shapes=[pltpu.VMEM((_HEADS_PER_STEP, nq, tq, tq), F32),
                        pltpu.VMEM((_HEADS_PER_STEP, tq, LANES), F32),
                        pltpu.VMEM((_HEADS_PER_STEP, tq, LANES), F32),
                        pltpu.VMEM((_HEADS_PER_STEP, tq, HEAD_PAD), F32)],
        compiler_params=_params(("arbitrary", "arbitrary")),
        name="attn",
    )(q, k, v)


_GROUP_SIZE = N_EXPERTS // N_EXPERT_GROUPS
_BIG = 1.0e9


def _mix_out_kernel(attn_ref, yc_ref, x_ref, mod_ref, gattn_ref, wout_ref, gpost_ref, gpre2_ref,
                    wrt_ref, br_ref, x1_ref, h2_ref, h2p_ref, idx_ref, wts_ref, rank_ref, cnt_ref,
                    carry_scr, scores_scr, sel_scr):
    i = pl.program_id(0)
    n_tiles = pl.num_programs(0) - 1

    @pl.when(i == 0)
    def _():
        carry_scr[...] = jnp.zeros(carry_scr.shape, F32)
        scores, sel = _project_tile(attn_ref, yc_ref, x_ref, mod_ref, gattn_ref, wout_ref, gpost_ref,
                                    gpre2_ref, wrt_ref, br_ref, x1_ref, h2_ref, h2p_ref)
        scores_scr[0] = scores
        sel_scr[0] = sel

    @pl.when(jnp.logical_and(i >= 1, i < n_tiles))
    def _():
        prev_scores = scores_scr[(i - 1) % 2]
        prev_sel = sel_scr[(i - 1) % 2]
        scores, sel = _project_tile(attn_ref, yc_ref, x_ref, mod_ref, gattn_ref, wout_ref, gpost_ref,
                                    gpre2_ref, wrt_ref, br_ref, x1_ref, h2_ref, h2p_ref)
        scores_scr[i % 2] = scores
        sel_scr[i % 2] = sel
        _route_tile(prev_scores, prev_sel, idx_ref, wts_ref, rank_ref, cnt_ref, carry_scr)

    @pl.when(i == n_tiles)
    def _():
        _route_tile(scores_scr[(i - 1) % 2], sel_scr[(i - 1) % 2], idx_ref, wts_ref, rank_ref, cnt_ref,
                    carry_scr)


def _project_tile(attn_ref, yc_ref, x_ref, mod_ref, gattn_ref, wout_ref, gpost_ref, gpre2_ref,
                  wrt_ref, br_ref, x1_ref, h2_ref, h2p_ref):
    half = attn_ref.shape[1]
    an = _rms(attn_ref[...].astype(F32), gattn_ref[...]).astype(BF16)
    mix = (jnp.dot(an, wout_ref[0:half, :], preferred_element_type=F32)
           + jnp.dot(yc_ref[...], wout_ref[half:, :], preferred_element_type=F32))
    g1 = mod_ref[0, 2:3, :]
    sh2 = mod_ref[0, 3:4, :]
    sc2 = mod_ref[0, 4:5, :]
    x1 = x_ref[...] + g1 * _rms(mix, gpost_ref[...])
    x1_ref[...] = x1
    h2 = _rms(x1, gpre2_ref[...]) * (1.0 + sc2) + sh2
    h2_ref[...] = h2.astype(BF16)
    h2p_ref[...] = _pack_row_words(h2[:, 0:_ROW_WORDS], h2[:, _ROW_WORDS:])

    logits = lax.dot_general(wrt_ref[...], h2, (((1,), (1,)), ((), ())),
                             preferred_element_type=F32, precision=lax.Precision.HIGHEST)
    scores = jax.nn.sigmoid(logits)
    return scores, scores + br_ref[...]


def _route_tile(scores, sel, idx_ref, wts_ref, rank_ref, cnt_ref, carry_scr):
    tm = scores.shape[1]
    row = lax.broadcasted_iota(I32, (N_EXPERTS, tm), 0).astype(F32)

    gscore = []
    rw = lax.broadcasted_iota(I32, (_GROUP_SIZE, tm), 0).astype(F32)
    for g in range(N_EXPERT_GROUPS):
        blk = sel[g * _GROUP_SIZE:(g + 1) * _GROUP_SIZE, :]
        m1 = jnp.max(blk, axis=0, keepdims=True)
        i1 = jnp.min(jnp.where(blk == m1, rw, _BIG), axis=0, keepdims=True)
        m2 = jnp.max(jnp.where(rw == i1, NEG_INF, blk), axis=0, keepdims=True)
        gscore.append(m1 + m2)

    gkeep = [jnp.zeros((1, tm), F32) for _ in range(N_EXPERT_GROUPS)]
    for _ in range(TOPK_GROUPS):
        mg = functools.reduce(jnp.maximum, gscore)
        ig = functools.reduce(jnp.minimum, [jnp.where(gscore[g] == mg, float(g), _BIG)
                                            for g in range(N_EXPERT_GROUPS)])
        for g in range(N_EXPERT_GROUPS):
            hit = ig == float(g)
            gkeep[g] = jnp.where(hit, 1.0, gkeep[g])
            gscore[g] = jnp.where(hit, NEG_INF, gscore[g])
    n_slabs = N_EXPERTS // SUBLANES
    slabs_per_group = _GROUP_SIZE // SUBLANES
    sub = lax.broadcasted_iota(I32, (SUBLANES, tm), 0).astype(F32)
    first_rows = [jnp.where(gkeep[j // slabs_per_group] > 0.0, sel[j * SUBLANES:(j + 1) * SUBLANES, :], NEG_INF)
                  for j in range(n_slabs)]
    cur_rows = list(first_rows)
    krow = lax.broadcasted_iota(I32, (TOP_K, tm), 0)
    idx_rows = []
    idx_f = jnp.zeros((TOP_K, tm), F32)
    sc_k = jnp.zeros((TOP_K, tm), F32)
    sc_sum = jnp.zeros((1, tm), F32)
    prev = None
    for k in range(TOP_K):
        best = jnp.full((SUBLANES, tm), NEG_INF, F32)
        best_slab = jnp.zeros((SUBLANES, tm), F32)
        best_score = jnp.zeros((SUBLANES, tm), F32)
        for j in range(n_slabs):
            if prev is not None:
                cur_rows[j] = jnp.where(sub == prev - float(j * SUBLANES), NEG_INF, cur_rows[j])
            better = cur_rows[j] > best
            best = jnp.where(better, cur_rows[j], best)
            best_slab = jnp.where(better, float(j), best_slab)
            best_score = jnp.where(better, scores[j * SUBLANES:(j + 1) * SUBLANES, :], best_score)
        best_idx = best_slab * float(SUBLANES) + sub
        m = jnp.max(best, axis=0, keepdims=True)
        ik = jnp.min(jnp.where(best == m, best_idx, _BIG), axis=0, keepdims=True)
        sk = jnp.sum(jnp.where(best_idx == ik, best_score, 0.0), axis=0, keepdims=True)
        prev = ik
        idx_rows.append(ik)
        idx_f = jnp.where(krow == k, ik, idx_f)
        sc_k = jnp.where(krow == k, sk, sc_k)
        sc_sum = sc_sum + sk
    wts_ref[...] = sc_k / sc_sum * ROUTED_SCALE
    idx_ref[...] = idx_f.astype(I32)
    onehot = jnp.concatenate(
        [jnp.where(jnp.where(sub == prev - float(j * SUBLANES), NEG_INF, cur_rows[j]) != first_rows[j], 1.0, 0.0)
         for j in range(n_slabs)], axis=0)

    tri = (lax.broadcasted_iota(I32, (tm, tm), 0) < lax.broadcasted_iota(I32, (tm, tm), 1))
    excl = jnp.dot(onehot.astype(BF16), tri.astype(BF16), preferred_element_type=F32)
    rank_e = carry_scr[:, 0:1] + excl
    rank_k = jnp.zeros((TOP_K, tm), F32)
    for k in range(TOP_K):
        hit = row == idx_rows[k]
        rk = jnp.sum(jnp.where(hit, rank_e, 0.0), axis=0, keepdims=True)
        rank_k = jnp.where(krow == k, rk, rank_k)
    rank_ref[...] = rank_k.astype(I32)
    carry_scr[...] = carry_scr[...] + jnp.sum(onehot, axis=1, keepdims=True)
    cnt_ref[...] = carry_scr[...]


def _mix_out(attn, yc, x2, mod3, gattn, wout, gpost, gpre2, wrt, br, seq):
    t, d = x2.shape
    tm = min(TM_OUT, seq)
    tpb = seq // tm
    n_tiles = t // tm
    last = n_tiles - 1
    full = lambda a: pl.BlockSpec(a.shape, lambda i: (0,) * a.ndim)
    row = lambda w: pl.BlockSpec((tm, w), lambda i: (jnp.minimum(i, last), 0))
    col = pl.BlockSpec((TOP_K, tm), lambda i: (0, jnp.maximum(i - 1, 0)))
    return pl.pallas_call(
        _mix_out_kernel,
        grid=(n_tiles + 1,),
        in_specs=[row(attn.shape[1]), row(yc.shape[1]), row(d),
                  pl.BlockSpec((1, 6, d), lambda i: (jnp.minimum(i, last) // tpb, 0, 0)),
                  full(gattn), full(wout), full(gpost), full(gpre2), full(wrt), full(br)],
        out_specs=[row(d), row(d), row(_ROW_WORDS), col, col, col,
                   pl.BlockSpec((N_EXPERTS, LANES), lambda i: (0, 0))],
        out_shape=[jax.ShapeDtypeStruct((t, d), F32), jax.ShapeDtypeStruct((t, d), BF16),
                   jax.ShapeDtypeStruct((t, _ROW_WORDS), U32),
                   jax.ShapeDtypeStruct((TOP_K, t), I32), jax.ShapeDtypeStruct((TOP_K, t), F32),
                   jax.ShapeDtypeStruct((TOP_K, t), I32),
                   jax.ShapeDtypeStruct((N_EXPERTS, LANES), F32)],
        scratch_shapes=[pltpu.VMEM((N_EXPERTS, LANES), F32), pltpu.VMEM((2, N_EXPERTS, tm), F32),
                        pltpu.VMEM((2, N_EXPERTS, tm), F32)],
        compiler_params=_params(("arbitrary",)),
        name="mix_out",
    )(attn, yc, x2, mod3, gattn, wout, gpost, gpre2, wrt, br)


def _dest_kernel(idx_ref, rank_ref, pstart_ref, dest_ref):
    tm = idx_ref.shape[1]
    row = lax.broadcasted_iota(I32, (N_EXPERTS, tm), 0)
    krow = lax.broadcasted_iota(I32, (TOP_K, tm), 0)
    pstart = pstart_ref[...]
    idx = idx_ref[...]
    out = jnp.zeros((TOP_K, tm), F32)
    for k in range(TOP_K):
        hit = row == idx[k:k + 1, :]
        base = jnp.sum(jnp.where(hit, pstart, 0.0), axis=0, keepdims=True)
        out = jnp.where(krow == k, base, out)
    dest_ref[...] = out.astype(I32) + rank_ref[...]


def _dest(idx, rank, pstart):
    t = idx.shape[1]
    tm = min(TM_DEST, t)
    col = pl.BlockSpec((TOP_K, tm), lambda i: (0, i))
    return pl.pallas_call(
        _dest_kernel,
        grid=(t // tm,),
        in_specs=[col, col, pl.BlockSpec((N_EXPERTS, 1), lambda i: (0, 0))],
        out_specs=col,
        out_shape=jax.ShapeDtypeStruct((TOP_K, t), I32),
        compiler_params=_params(("arbitrary",)),
        name="dest",
    )(idx, rank, pstart)


_PAD_CHUNKS = tuple(BM_EXPERT >> s for s in range(1, BM_EXPERT.bit_length()))


SC_CORES = 2
SC_SUBCORES = 16
SC_ROWS = 128


def _sc_worker_split(n_chunks):
    workers = SC_CORES * SC_SUBCORES
    per_worker = max(1, n_chunks // workers)
    active = n_chunks // per_worker
    assert active * per_worker == n_chunks and active <= workers
    return per_worker, active


def _scatter_rows_sc(rows, idx3d, n_out):
    n, width = rows.shape
    n_chunks = n // SC_ROWS
    per_worker, active = _sc_worker_split(n_chunks)
    mesh = plsc.VectorSubcoreMesh(core_axis_name="c", subcore_axis_name="s")

    @functools.partial(
        pl.kernel, mesh=mesh, out_type=jax.ShapeDtypeStruct((n_out, width), rows.dtype),
        scratch_types=[pltpu.VMEM((TOP_K, SC_ROWS), I32), pltpu.VMEM((SC_ROWS, width), rows.dtype),
                       pltpu.SemaphoreType.DMA])
    def scatter(rows_hbm, idx_hbm, out_hbm, idx_v, rows_v, sem):
        wid = lax.axis_index("s") * SC_CORES + lax.axis_index("c")

        @pl.when(wid < active)
        def _():
            @pl.loop(0, per_worker)
            def _(c):
                chunk = wid * per_worker + c
                pltpu.sync_copy(rows_hbm.at[pl.ds(chunk * SC_ROWS, SC_ROWS)], rows_v)
                pltpu.sync_copy(idx_hbm.at[chunk], idx_v)
                copies = [pltpu.make_async_copy(rows_v, out_hbm.at[idx_v.at[k]], sem) for k in range(TOP_K)]
                for cp in copies:
                    cp.start()
                for cp in copies:
                    cp.wait()

    return scatter(rows, idx3d)


def _padfill_kernel(pad_from_ref, pad_n_ref, xs_in_hbm, xs_hbm, zero_scr, pad_sem):
    del xs_in_hbm
    zero_scr[...] = jnp.zeros(zero_scr.shape, zero_scr.dtype)

    def pad_copies(e, act):
        n = pad_n_ref[e]
        base = pad_from_ref[e]

        def single_rows(start, count):
            for j in range(SUBLANES - 1):
                @pl.when(j < count)
                def _():
                    act(pltpu.make_async_copy(zero_scr.at[pl.ds(0, 1), :],
                                              xs_hbm.at[pl.ds(start + j, 1), :], pad_sem))

        head = jnp.minimum(n, (SUBLANES - (base & (SUBLANES - 1))) & (SUBLANES - 1))
        single_rows(base, head)
        rest = n - head
        mid = base + head
        for rows in _PAD_CHUNKS:
            if rows >= SUBLANES:
                @pl.when((rest & rows) != 0)
                def _():
                    start = pl.multiple_of(mid + (rest & ~(2 * rows - 1)), SUBLANES)
                    act(pltpu.make_async_copy(zero_scr.at[pl.ds(0, rows), :],
                                              xs_hbm.at[pl.ds(start, rows), :], pad_sem))
        single_rows(mid + (rest & ~(SUBLANES - 1)), rest & (SUBLANES - 1))

    def issue_pad(e, carry):
        pad_copies(e, lambda cp: cp.start())
        return carry

    def drain_pad(e, carry):
        pad_copies(e, lambda cp: cp.wait())
        return carry

    lax.fori_loop(0, N_EXPERTS, issue_pad, 0)
    lax.fori_loop(0, N_EXPERTS, drain_pad, 0)


def _padfill(pad_from, pad_n, xs):
    return pl.pallas_call(
        _padfill_kernel,
        grid_spec=pltpu.PrefetchScalarGridSpec(
            num_scalar_prefetch=2,
            grid=(1,),
            in_specs=[pl.BlockSpec(memory_space=pl.ANY)],
            out_specs=pl.BlockSpec(memory_space=pl.ANY),
            scratch_shapes=[pltpu.VMEM((BM_EXPERT // 2, _ROW_WORDS), U32), pltpu.SemaphoreType.DMA]),
        out_shape=jax.ShapeDtypeStruct(xs.shape, xs.dtype),
        input_output_aliases={2: 0},
        compiler_params=_params(("arbitrary",)),
        name="padfill",
    )(pad_from, pad_n, xs)


_XS_SLOTS = 4
_YS_SLOTS = 2
_W_SLOTS = 4
_W_AHEAD = 2


def _expert_kernel(first_ref, ord_ref, uexp_ref, meta_ref, xs_hbm, wg_hbm, wu_hbm, wd_hbm, ys_hbm,
                   xs_buf, ys_buf, wg_buf, wu_buf, wd_buf, act_scr, xs_sem, ys_sem, w_sem):
    i = pl.program_id(0)
    nused = meta_ref[0]
    nexp = meta_ref[1]
    bm = xs_buf.shape[1]

    def xs_copy(b, slot):
        return pltpu.make_async_copy(xs_hbm.at[pl.ds(b * bm, bm), :], xs_buf.at[slot], xs_sem.at[slot])

    def ys_copy(b, slot):
        return pltpu.make_async_copy(ys_buf.at[slot], ys_hbm.at[pl.ds(b * bm, bm), :], ys_sem.at[slot])

    def w_copies(j, slot):
        e = uexp_ref[j]
        return (pltpu.make_async_copy(wg_hbm.at[e], wg_buf.at[slot], w_sem.at[slot, 0]),
                pltpu.make_async_copy(wu_hbm.at[e], wu_buf.at[slot], w_sem.at[slot, 1]),
                pltpu.make_async_copy(wd_hbm.at[e], wd_buf.at[slot], w_sem.at[slot, 2]))

    @pl.when(i == 0)
    def _():
        for s in range(_XS_SLOTS - 1):
            @pl.when(s < nused)
            def _():
                xs_copy(s, s).start()
        for s in range(_W_AHEAD):
            @pl.when(s < nexp)
            def _():
                for cp in w_copies(s, s):
                    cp.start()

    def fetch(b):
        ahead = b + _XS_SLOTS - 1

        @pl.when(ahead < nused)
        def _():
            xs_copy(ahead, ahead % _XS_SLOTS).start()

        j = ord_ref[b]

        @pl.when(first_ref[b] == 1)
        def _():
            for cp in w_copies(j, j % _W_SLOTS):
                cp.wait()
            nxt = j + _W_AHEAD

            @pl.when(nxt < nexp)
            def _():
                for cp in w_copies(nxt, nxt % _W_SLOTS):
                    cp.start()

        xs_copy(b, b % _XS_SLOTS).wait()

    def gate_up(b):
        ws = ord_ref[b] % _W_SLOTS
        lo, hi = _unpack_row_words(xs_buf[b % _XS_SLOTS])
        g = (jnp.dot(lo, wg_buf[ws, 0:_ROW_WORDS, :], preferred_element_type=F32)
             + jnp.dot(hi, wg_buf[ws, _ROW_WORDS:, :], preferred_element_type=F32))
        u = (jnp.dot(lo, wu_buf[ws, 0:_ROW_WORDS, :], preferred_element_type=F32)
             + jnp.dot(hi, wu_buf[ws, _ROW_WORDS:, :], preferred_element_type=F32))
        return g * jax.nn.sigmoid(g) * u

    def down(b, act):
        y = jnp.dot(act, wd_buf[ord_ref[b] % _W_SLOTS], preferred_element_type=F32)
        oslot = b % _YS_SLOTS
        ys_buf[oslot] = _pack_row_words(y[:, 0:_ROW_WORDS], y[:, _ROW_WORDS:])
        ys_copy(b, oslot).start()

    @pl.when(jnp.logical_and(i >= _YS_SLOTS + 1, i <= nused))
    def _():
        ys_copy(i - 1 - _YS_SLOTS, (i - 1) % _YS_SLOTS).wait()

    @pl.when(i == 0)
    def _():
        fetch(i)
        act_scr[...] = gate_up(i)

    @pl.when(jnp.logical_and(i >= 1, i < nused))
    def _():
        fetch(i)
        prev = act_scr[...]
        act_scr[...] = gate_up(i)
        down(i - 1, prev)

    @pl.when(i == nused)
    def _():
        down(i - 1, act_scr[...])
        ys_copy(i - 1, (i - 1) % _YS_SLOTS).wait()

        @pl.when(i >= 2)
        def _():
            ys_copy(i - 2, (i - 2) % _YS_SLOTS).wait()


def _experts(first, ordinal, uexp, meta, xs, w_gate, w_up, w_down):
    p = xs.shape[0]
    d = w_gate.shape[1]
    nb = p // BM_EXPERT
    anyspec = pl.BlockSpec(memory_space=pl.ANY)
    return pl.pallas_call(
        _expert_kernel,
        grid_spec=pltpu.PrefetchScalarGridSpec(
            num_scalar_prefetch=4,
            grid=(nb + 1,),
            in_specs=[anyspec, anyspec, anyspec, anyspec],
            out_specs=anyspec,
            scratch_shapes=[pltpu.VMEM((_XS_SLOTS, BM_EXPERT, _ROW_WORDS), U32),
                            pltpu.VMEM((_YS_SLOTS, BM_EXPERT, _ROW_WORDS), U32),
                            pltpu.VMEM((_W_SLOTS, d, EXPERT_DIM), F32),
                            pltpu.VMEM((_W_SLOTS, d, EXPERT_DIM), F32),
                            pltpu.VMEM((_W_SLOTS, EXPERT_DIM, d), F32),
                            pltpu.VMEM((BM_EXPERT, EXPERT_DIM), F32),
                            pltpu.SemaphoreType.DMA((_XS_SLOTS,)), pltpu.SemaphoreType.DMA((_YS_SLOTS,)),
                            pltpu.SemaphoreType.DMA((_W_SLOTS, 3))]),
        out_shape=jax.ShapeDtypeStruct((p, _ROW_WORDS), U32),
        compiler_params=_params(("arbitrary",)),
        name="experts",
    )(first, ordinal, uexp, meta, xs, w_gate, w_up, w_down)


SC_GATHER_ROWS = 64


def _gather_rows_sc(table, idx):
    n = idx.shape[0]
    width = table.shape[1]
    rows = SC_GATHER_ROWS
    per_worker, active = _sc_worker_split(n // rows)
    assert per_worker % 2 == 0 or per_worker == 1
    mesh = plsc.VectorSubcoreMesh(core_axis_name="c", subcore_axis_name="s")

    @functools.partial(
        pl.kernel, mesh=mesh, out_type=jax.ShapeDtypeStruct((n, width), table.dtype),
        scratch_types=[pltpu.VMEM((per_worker, rows), I32), pltpu.VMEM((2, rows, width), table.dtype),
                       pltpu.SemaphoreType.DMA((2,))])
    def gather(table_hbm, idx_hbm, out_hbm, idx_v, rows_v, sem):
        wid = lax.axis_index("s") * SC_CORES + lax.axis_index("c")

        def fetch(c, b):
            return pltpu.make_async_copy(table_hbm.at[idx_v.at[c]], rows_v.at[b], sem.at[b])

        @pl.when(wid < active)
        def _():
            first = wid * per_worker
            pltpu.sync_copy(idx_hbm.at[pl.ds(first, per_worker)], idx_v)
            fetch(0, 0).start()

            @pl.loop(0, per_worker, step=2)
            def _(c):
                for b in range(min(2, per_worker)):
                    cur = c + b

                    @pl.when(cur + 1 < per_worker)
                    def _():
                        fetch(cur + 1, 1 - b).start()

                    fetch(cur, b).wait()
                    pltpu.sync_copy(rows_v.at[b], out_hbm.at[pl.ds((first + cur) * rows, rows)])

    return gather(table, idx.reshape(n // rows, rows))


def _combine_kernel(wts_ref, yg_ref, h2_ref, x1_ref, mod_ref, wsg_ref, wsu_ref, wsd_ref, gpost_ref, *rest):
    o_ref = rest[-1]
    h2 = h2_ref[...]
    g = jnp.dot(h2, wsg_ref[...], preferred_element_type=F32)
    u = jnp.dot(h2, wsu_ref[...], preferred_element_type=F32)
    f = jnp.dot((g * jax.nn.sigmoid(g) * u).astype(BF16), wsd_ref[...], preferred_element_type=F32)

    wts = wts_ref[...]
    los = [f[:, sl * LANES:(sl + 1) * LANES] for sl in range(_ROW_SLABS)]
    his = [f[:, _ROW_WORDS + sl * LANES:_ROW_WORDS + (sl + 1) * LANES] for sl in range(_ROW_SLABS)]
    for k in range(TOP_K):
        wk = wts[:, k:k + 1]
        for sl in range(_ROW_SLABS):
            lo, hi = _unpack_row_words(yg_ref[k, :, sl * LANES:(sl + 1) * LANES])
            los[sl] = los[sl] + wk * lo
            his[sl] = his[sl] + wk * hi
    f = jnp.concatenate(los + his, axis=1)
    g2 = mod_ref[0, 5:6, :]
    o_ref[...] = x1_ref[...] + g2 * _rms(f, gpost_ref[...])


def _combine(wts_t, yg, h2, x1, mod3, wsg, wsu, wsd, gpost, seq, first_tile, partial_out):
    t, d = x1.shape
    tm = min(TM_COMBINE, seq)
    tpb = seq // tm
    full = lambda a: pl.BlockSpec(a.shape, lambda i: (0,) * a.ndim)
    row = lambda w: pl.BlockSpec((tm, w), lambda i: (i + first_tile, 0))
    args = [wts_t, yg, h2, x1, mod3, wsg, wsu, wsd, gpost]
    in_specs = [row(TOP_K), pl.BlockSpec((TOP_K, tm, _ROW_WORDS), lambda i: (0, i, 0)), row(d), row(d),
                pl.BlockSpec((1, 6, d), lambda i: ((i + first_tile) // tpb, 0, 0)),
                full(wsg), full(wsu), full(wsd), full(gpost)]
    aliases = {}
    if partial_out is not None:
        aliases = {len(args): 0}
        args.append(partial_out)
        in_specs.append(pl.BlockSpec(memory_space=pl.ANY))
    return pl.pallas_call(
        _combine_kernel,
        grid=(yg.shape[1] // tm,),
        in_specs=in_specs,
        out_specs=row(d),
        out_shape=jax.ShapeDtypeStruct((t, d), F32),
        input_output_aliases=aliases,
        compiler_params=_params(("arbitrary",)),
        name="combine",
    )(*args)


def _pack_weights(w_in, w_uq, w_ukv):
    d = w_in.shape[0]
    half = QK_ROPE_DIM // 2
    z = lambda n, c: jnp.zeros((n, c), F32)
    o = Q_LORA_RANK + KV_LORA_RANK
    kr = w_in[:, o:o + QK_ROPE_DIM]
    kr_grp = jnp.concatenate([z(d, QK_NOPE_DIM), kr, z(d, HEAD_PAD - QK_NOPE_DIM - QK_ROPE_DIM)], axis=1)
    kr_rot = jnp.concatenate([z(d, QK_NOPE_DIM), -kr[:, half:], kr[:, :half],
                              z(d, HEAD_PAD - QK_NOPE_DIM - QK_ROPE_DIM)], axis=1)
    win_p = jnp.concatenate([w_in[:, :o].astype(BF16), kr_grp.astype(BF16), kr_rot.astype(BF16),
                             w_in[:, o + QK_ROPE_DIM:].astype(BF16)], axis=1)

    scale = float(QK_NOPE_DIM + QK_ROPE_DIM) ** -0.5 * float(np.log2(np.e))
    r = Q_LORA_RANK
    qd = QK_NOPE_DIM + QK_ROPE_DIM
    q_grp, q_rot = [], []
    for h in range(MLA_HEADS):
        nope = w_uq[:, h * qd:h * qd + QK_NOPE_DIM]
        rope = w_uq[:, h * qd + QK_NOPE_DIM:(h + 1) * qd]
        pad = z(r, HEAD_PAD - qd)
        q_grp.append(jnp.concatenate([nope, rope, pad], axis=1))
        q_rot.append(jnp.concatenate([z(r, QK_NOPE_DIM), -rope[:, half:], rope[:, :half], pad], axis=1))
    wuq_p = jnp.concatenate(q_grp + q_rot, axis=1) * scale

    c = KV_LORA_RANK
    kd = QK_NOPE_DIM + V_HEAD_DIM
    k_grp, v_grp = [], []
    for h in range(MLA_HEADS):
        k_grp.append(jnp.concatenate([w_ukv[:, h * kd:h * kd + QK_NOPE_DIM], z(c, HEAD_PAD - QK_NOPE_DIM)], axis=1))
        v_grp.append(jnp.concatenate([w_ukv[:, h * kd + QK_NOPE_DIM:(h + 1) * kd], z(c, HEAD_PAD - V_HEAD_DIM)], axis=1))
    wukv_p = jnp.concatenate(k_grp + v_grp, axis=1)
    return win_p.astype(BF16), wuq_p.astype(BF16), wukv_p.astype(BF16)


def _rope_inputs(positions):
    inv = 1.0 / (ROPE_THETA ** (jnp.arange(0, QK_ROPE_DIM, 2, dtype=F32) / QK_ROPE_DIM))
    freq = jnp.tile(inv, LANES // inv.shape[0]).reshape(1, LANES)
    pos = jnp.repeat(positions.astype(F32).reshape(-1, _ROPE_PACK), QK_ROPE_DIM, axis=1)
    return pos, freq


def _layer(x2, c, pos, freq, batch, seq, w_ada, b_ada, g_pre_mix, w_in, g_q_lat, w_uq, g_kv_lat, w_ukv,
           w_conv, g_attn_out, g_conv_out, w_out, g_post_mix, g_pre_ffn, w_router, b_router,
           w_gate, w_up, w_down, w_sh_gate, w_sh_up, w_sh_down, g_post_ffn):
    t, d = x2.shape
    r1 = lambda a: a.reshape(1, -1)

    c_pad = jnp.zeros((SUBLANES, d), F32).at[:batch].set(c)
    mod = _ada(c_pad, w_ada, r1(b_ada))[:batch]
    mod3 = mod.reshape(batch, 6, d)

    win_p, wuq_p, wukv_p = _pack_weights(w_in, w_uq, w_ukv)
    vone = jnp.zeros((1, HEAD_PAD), F32).at[0, V_HEAD_DIM].set(1.0)
    q, k, v, yc = _mix_in(x2, mod3, r1(g_pre_mix), win_p, r1(g_q_lat), wuq_p, r1(g_kv_lat), wukv_p,
                          vone, w_conv, r1(g_conv_out), pos, freq, seq)
    attn = _attention(q, k, v, batch, seq)
    x1, h2, h2p, idx, wts, rank, cnt = _mix_out(
        attn, yc, x2, mod3, r1(g_attn_out), w_out.astype(BF16), r1(g_post_mix), r1(g_pre_ffn),
        w_router.T, b_router.reshape(-1, 1), seq)

    counts = cnt[:, 0].astype(I32)
    padded = ((counts + BM_EXPERT - 1) // BM_EXPERT) * BM_EXPERT
    pad_end = jnp.cumsum(padded)
    pad_start = pad_end - padded
    m = t * TOP_K
    nb = (m + N_EXPERTS * (BM_EXPERT - 1)) // BM_EXPERT
    nused = pad_end[-1] // BM_EXPERT
    bidx = jnp.arange(nb, dtype=I32)
    blk_exp = jnp.sum((pad_end[None, :] <= (bidx * BM_EXPERT)[:, None]).astype(I32), axis=1)
    first = ((bidx < nused) & ((bidx == 0) | (blk_exp != jnp.roll(blk_exp, 1)))).astype(I32)
    ordinal = jnp.maximum(jnp.cumsum(first) - 1, 0).astype(I32)
    seen = jnp.cumsum((counts > 0).astype(I32))
    uexp = jnp.minimum(jnp.sum((seen[None, :] <= jnp.arange(N_EXPERTS, dtype=I32)[:, None]).astype(I32), axis=1),
                       N_EXPERTS - 1).astype(I32)
    meta = jnp.stack([nused, seen[-1]]).astype(I32)

    dest = _dest(idx, rank, pad_start.astype(F32).reshape(-1, 1))
    xs = _scatter_rows_sc(h2p, dest.reshape(TOP_K, -1, SC_ROWS).transpose(1, 0, 2), nb * BM_EXPERT)
    xs = _padfill((pad_start + counts).astype(I32), (padded - counts).astype(I32), xs)
    ys = _experts(first, ordinal, uexp, meta, xs, w_gate, w_up, w_down)
    tm_c = min(TM_COMBINE, seq)
    half_rows = (t // 2) * TOP_K
    even_split = half_rows % (2 * SC_GATHER_ROWS * SC_CORES * SC_SUBCORES) == 0
    parts = 2 if t % (2 * tm_c) == 0 and even_split else 1
    tp = t // parts
    wsg, wsu, wsd = w_sh_gate.astype(BF16), w_sh_up.astype(BF16), w_sh_down.astype(BF16)
    out = None
    for part in range(parts):
        part_dest = dest[:, part * tp:(part + 1) * tp].reshape(-1)
        yg = _gather_rows_sc(ys, part_dest).reshape(TOP_K, tp, _ROW_WORDS)
        out = _combine(wts.T, yg, h2, x1, mod3, wsg, wsu, wsd, r1(g_post_ffn), seq, part * tp // tm_c, out)
    return out


def kernel(x, c, positions, w_ada, b_ada, g_pre_mix, w_in, g_q_lat, w_uq, g_kv_lat, w_ukv, w_conv, g_attn_out, g_conv_out, w_out, g_post_mix, g_pre_ffn, w_router, b_router, w_gate, w_up, w_down, w_sh_gate, w_sh_up, w_sh_down, g_post_ffn):
    batch, seq, d = x.shape
    pos, freq = _rope_inputs(positions)
    x2 = x.reshape(batch * seq, d)
    for l in range(w_ada.shape[0]):
        x2 = _layer(x2, c, pos, freq, batch, seq, w_ada[l], b_ada[l], g_pre_mix[l], w_in[l], g_q_lat[l],
                    w_uq[l], g_kv_lat[l], w_ukv[l], w_conv[l], g_attn_out[l], g_conv_out[l], w_out[l],
                    g_post_mix[l], g_pre_ffn[l], w_router[l], b_router[l], w_gate[l], w_up[l], w_down[l],
                    w_sh_gate[l], w_sh_up[l], w_sh_down[l], g_post_ffn[l])
    return x2.reshape(batch, seq, d)
```

```python
import functools

import jax
import jax.numpy as jnp
import numpy as np
from jax import lax
from jax.experimental import pallas as pl
from jax.experimental.pallas import tpu as pltpu
from jax.experimental.pallas import tpu_sc as plsc

F32 = jnp.float32
BF16 = jnp.bfloat16
I32 = jnp.int32
U32 = jnp.uint32

CHUNK = 64
MLA_HEADS = 8
QK_NOPE_DIM = 64
QK_ROPE_DIM = 32
V_HEAD_DIM = 64
Q_LORA_RANK = 384
KV_LORA_RANK = 256
ROPE_THETA = 10000.0
CONV_WIDTH = 3
N_EXPERTS = 256
TOP_K = 8
N_EXPERT_GROUPS = 8
TOPK_GROUPS = 4
EXPERT_DIM = 256
ROUTED_SCALE = 2.5
EPS = 1e-6

LANES = 128
SUBLANES = 8
HEAD_PAD = LANES
VMEM_LIMIT_BYTES = 56 * 1024 * 1024

TM_IN = 1024
TQ_ATTN = 512
TM_OUT = 512
TM_DEST = 2048
BM_EXPERT = 256
TM_COMBINE = 512

NEG_INF = float("-inf")


def _rms(x, g):
    return x * lax.rsqrt(jnp.mean(x * x, axis=-1, keepdims=True) + EPS) * g


_HI_MASK = np.uint32(0xFFFF0000)
_ROW_WORDS = 512
_ROW_SLABS = _ROW_WORDS // LANES


def _pack_row_words(lo, hi):
    lo_w = lax.bitcast_convert_type(lo.astype(BF16).astype(F32), U32) >> 16
    hi_w = lax.bitcast_convert_type(hi.astype(BF16).astype(F32), U32) & _HI_MASK
    return lo_w | hi_w


def _unpack_row_words(w):
    return (lax.bitcast_convert_type(w << 16, F32), lax.bitcast_convert_type(w & _HI_MASK, F32))


def _params(sem):
    return pltpu.CompilerParams(dimension_semantics=sem, vmem_limit_bytes=VMEM_LIMIT_BYTES)


def _ada_kernel(c_ref, w_ref, b_ref, o_ref):
    c = c_ref[...]
    s = c * jax.nn.sigmoid(c)
    o_ref[...] = jnp.dot(s, w_ref[...], preferred_element_type=F32,
                         precision=lax.Precision.HIGHEST) + b_ref[...]


def _ada(c_pad, w, b):
    rows, d = c_pad.shape
    n = w.shape[1]
    tn = 1536
    return pl.pallas_call(
        _ada_kernel,
        grid=(n // tn,),
        in_specs=[pl.BlockSpec((rows, d), lambda j: (0, 0)),
                  pl.BlockSpec((d, tn), lambda j: (0, j)),
                  pl.BlockSpec((1, tn), lambda j: (0, j))],
        out_specs=pl.BlockSpec((rows, tn), lambda j: (0, j)),
        out_shape=jax.ShapeDtypeStruct((rows, n), F32),
        compiler_params=_params(("arbitrary",)),
        name="ada",
    )(c_pad, w, b)


_CQ0, _CQ1 = 0, Q_LORA_RANK
_CKV0, _CKV1 = _CQ1, _CQ1 + KV_LORA_RANK
_KR0, _KR1 = _CKV1, _CKV1 + 2 * HEAD_PAD
_CONV_DIM = 512
_GB0 = _KR1
_GC0 = _GB0 + _CONV_DIM
_XV0 = _GC0 + _CONV_DIM
_WIN_COLS = _XV0 + _CONV_DIM
_QW = MLA_HEADS * HEAD_PAD
_ROPE_PACK = LANES // QK_ROPE_DIM


def _mix_in_kernel(tiles_per_batch, x_ref, mod_ref, gpre_ref, win_ref, gq_ref, wuq_ref, gkv_ref,
                   wukv_ref, vone_ref, wconv_ref, gconv_ref, pos_ref, freq_ref,
                   q_ref, k_ref, v_ref, yc_ref, h_scr, u_scr, cos_scr, sin_scr):
    i = pl.program_id(0)
    tm = x_ref.shape[0]
    sh1 = mod_ref[0, 0:1, :]
    sc1 = mod_ref[0, 1:2, :]
    h = _rms(x_ref[...], gpre_ref[...]) * (1.0 + sc1) + sh1
    h_scr[...] = h.astype(BF16)
    ang4 = pos_ref[...] * freq_ref[...]
    cos4 = jnp.cos(ang4)
    sin4 = jnp.sin(ang4)
    lane = lax.broadcasted_iota(I32, ang4.shape, 1)
    on_rope = jnp.logical_and(lane >= QK_NOPE_DIM, lane < QK_NOPE_DIM + QK_ROPE_DIM)
    for g in range(_ROPE_PACK):
        shift = (QK_NOPE_DIM - QK_ROPE_DIM * g) % LANES
        cg = cos4 if shift == 0 else pltpu.roll(cos4, shift, axis=1)
        sg = sin4 if shift == 0 else pltpu.roll(sin4, shift, axis=1)
        cos_scr[pl.ds(g, tm // _ROPE_PACK, stride=_ROPE_PACK), :] = jnp.where(on_rope, cg, 1.0)
        sin_scr[pl.ds(g, tm // _ROPE_PACK, stride=_ROPE_PACK), :] = jnp.where(on_rope, sg, 0.0)
    cos = cos_scr[...]
    sin = sin_scr[...]

    cq = jnp.dot(h_scr[...], win_ref[:, _CQ0:_CQ1], preferred_element_type=F32)
    cqn = _rms(cq, gq_ref[...]).astype(BF16)
    pair = 2 * HEAD_PAD
    for lo in range(0, _QW, pair):
        qa = jnp.dot(cqn, wuq_ref[:, lo:lo + pair], preferred_element_type=F32)
        qr = jnp.dot(cqn, wuq_ref[:, _QW + lo:_QW + lo + pair], preferred_element_type=F32)
        for off in range(0, pair, HEAD_PAD):
            qh = qa[:, off:off + HEAD_PAD] * cos + qr[:, off:off + HEAD_PAD] * sin
            q_ref[:, lo + off:lo + off + HEAD_PAD] = qh.astype(BF16)

    ckv = jnp.dot(h_scr[...], win_ref[:, _CKV0:_CKV1], preferred_element_type=F32)
    ckvn = _rms(ckv, gkv_ref[...]).astype(BF16)
    krr = jnp.dot(h_scr[...], win_ref[:, _KR0:_KR1], preferred_element_type=F32)
    kr = krr[:, 0:HEAD_PAD] * cos + krr[:, HEAD_PAD:2 * HEAD_PAD] * sin
    vone = vone_ref[...]
    for lo in range(0, _QW, pair):
        kk = jnp.dot(ckvn, wukv_ref[:, lo:lo + pair], preferred_element_type=F32)
        vv = jnp.dot(ckvn, wukv_ref[:, _QW + lo:_QW + lo + pair], preferred_element_type=F32)
        for off in range(0, pair, HEAD_PAD):
            k_ref[:, lo + off:lo + off + HEAD_PAD] = (kk[:, off:off + HEAD_PAD] + kr).astype(BF16)
            v_ref[:, lo + off:lo + off + HEAD_PAD] = (vv[:, off:off + HEAD_PAD] + vone).astype(BF16)

    gb = jnp.dot(h_scr[...], win_ref[:, _GB0:_GC0], preferred_element_type=F32)
    gc = jnp.dot(h_scr[...], win_ref[:, _GC0:_XV0], preferred_element_type=F32)
    xv = jnp.dot(h_scr[...], win_ref[:, _XV0:_WIN_COLS], preferred_element_type=F32)
    u = gc * xv
    prev = u_scr[tm:tm + SUBLANES, :]
    first = (i % tiles_per_batch) == 0
    u_scr[0:SUBLANES, :] = jnp.where(first, jnp.zeros_like(prev), prev)
    u_scr[SUBLANES:tm + SUBLANES, :] = u
    um1 = u_scr[SUBLANES - 1:tm + SUBLANES - 1, :]
    um2 = u_scr[SUBLANES - 2:tm + SUBLANES - 2, :]
    assert wconv_ref.shape[0] == CONV_WIDTH
    conv = wconv_ref[0:1, :] * um2 + wconv_ref[1:2, :] * um1 + wconv_ref[2:3, :] * u
    yc_ref[...] = _rms(gb * conv, gconv_ref[...]).astype(BF16)


def _mix_in(x2, mod3, gpre, win_p, gq, wuq_p, gkv, wukv_p, vone, wconv, gconv, pos, freq, seq):
    t, d = x2.shape
    tm = min(TM_IN, seq)
    tpb = seq // tm
    full = lambda a: pl.BlockSpec(a.shape, lambda i: (0,) * a.ndim)
    row = lambda w: pl.BlockSpec((tm, w), lambda i: (i, 0))
    return pl.pallas_call(
        functools.partial(_mix_in_kernel, tpb),
        grid=(t // tm,),
        in_specs=[row(d),
                  pl.BlockSpec((1, 6, d), lambda i: (i // tpb, 0, 0)),
                  full(gpre), full(win_p), full(gq), full(wuq_p), full(gkv), full(wukv_p),
                  full(vone), full(wconv), full(gconv),
                  pl.BlockSpec((tm // _ROPE_PACK, LANES), lambda i: (i, 0)), full(freq)],
        out_specs=[row(_QW), row(_QW), row(_QW), row(_CONV_DIM)],
        out_shape=[jax.ShapeDtypeStruct((t, _QW), BF16), jax.ShapeDtypeStruct((t, _QW), BF16),
                   jax.ShapeDtypeStruct((t, _QW), BF16), jax.ShapeDtypeStruct((t, _CONV_DIM), BF16)],
        scratch_shapes=[pltpu.VMEM((tm, d), BF16), pltpu.VMEM((tm + SUBLANES, _CONV_DIM), F32),
                        pltpu.VMEM((tm, HEAD_PAD), F32), pltpu.VMEM((tm, HEAD_PAD), F32)],
        compiler_params=_params(("arbitrary",)),
        name="mix_in",
    )(x2, mod3, gpre, win_p, gq, wuq_p, gkv, wukv_p, vone, wconv, gconv, pos, freq)


_HEADS_PER_STEP = 2


def _attn_kernel(tq, q_ref, k_ref, v_ref, o_ref, s_scr, mrun_scr, mb_scr, acc_scr):
    tk = tq
    nq = q_ref.shape[0] // tq
    lane_groups = tk // LANES
    heads = range(_HEADS_PER_STEP)
    lanes = [slice(hh * HEAD_PAD, (hh + 1) * HEAD_PAD) for hh in heads]

    def tile_max(s):
        m = s[:, 0:LANES]
        for g in range(1, lane_groups):
            m = jnp.maximum(m, s[:, g * LANES:(g + 1) * LANES])
        return m

    def scores(hh, qi, kv):
        off = pl.multiple_of(kv * tk, tk)
        return lax.dot_general(q_ref[qi * tq:(qi + 1) * tq, lanes[hh]], k_ref[pl.ds(off, tk), lanes[hh]],
                               (((1,), (1,)), ((), ())), preferred_element_type=F32)

    rc = lax.broadcasted_iota(I32, (tq, tk), 0) // CHUNK
    cc = lax.broadcasted_iota(I32, (tq, tk), 1) // CHUNK

    def diagonal(qi):
        for hh in heads:
            s = jnp.where(cc <= rc, scores(hh, qi, qi), NEG_INF)
            s_scr[hh, qi] = s
            m_row = jnp.max(jnp.maximum(mrun_scr[hh], tile_max(s)), axis=1, keepdims=True)
            mb_scr[hh] = jnp.broadcast_to(m_row, (tq, LANES))

    mrun_scr[...] = jnp.full(mrun_scr.shape, NEG_INF, F32)
    diagonal(0)
    for qi in range(nq):
        has_next = qi + 1 < nq
        acc_scr[...] = jnp.zeros(acc_scr.shape, F32)
        if has_next:
            mrun_scr[...] = jnp.full(mrun_scr.shape, NEG_INF, F32)

        def body(kv, carry, qi=qi, has_next=has_next):
            off = pl.multiple_of(kv * tk, tk)
            for hh in heads:
                mb = mb_scr[hh]
                p = jnp.concatenate(
                    [jnp.exp2(s_scr[hh, kv, :, g * LANES:(g + 1) * LANES] - mb) for g in range(lane_groups)],
                    axis=1).astype(BF16)
                acc_scr[hh] += jnp.dot(p, v_ref[pl.ds(off, tk), lanes[hh]], preferred_element_type=F32)
                if has_next:
                    s = scores(hh, qi + 1, kv)
                    s_scr[hh, kv] = s
                    mrun_scr[hh] = jnp.maximum(mrun_scr[hh], tile_max(s))
            return carry

        lax.fori_loop(0, qi + 1, body, 0, unroll=4)
        for hh in heads:
            acc = acc_scr[hh]
            o = acc[:, 0:V_HEAD_DIM] / acc[:, V_HEAD_DIM:V_HEAD_DIM + 1]
            o_ref[qi * tq:(qi + 1) * tq, hh * V_HEAD_DIM:(hh + 1) * V_HEAD_DIM] = o.astype(BF16)
        if has_next:
            diagonal(qi + 1)


def _attention(q, k, v, batch, seq):
    t = q.shape[0]
    tq = min(TQ_ATTN, seq)
    nq = seq // tq
    hw = _HEADS_PER_STEP * HEAD_PAD
    ow = _HEADS_PER_STEP * V_HEAD_DIM
    blk = lambda w: pl.BlockSpec((seq, w), lambda b, j: (b, j))
    return pl.pallas_call(
        functools.partial(_attn_kernel, tq),
        grid=(batch, MLA_HEADS // _HEADS_PER_STEP),
        in_specs=[blk(hw), blk(hw), blk(hw)],
        out_specs=blk(ow),
        out_shape=jax.ShapeDtypeStruct((t, MLA_HEADS * V_HEAD_DIM), BF16),
        scratch_shapes=[pltpu.VMEM((_HEADS_PER_STEP, nq, tq, tq), F32),
                        pltpu.VMEM((_HEADS_PER_STEP, tq, LANES), F32),
                        pltpu.VMEM((_HEADS_PER_STEP, tq, LANES), F32),
                        pltpu.VMEM((_HEADS_PER_STEP, tq, HEAD_PAD), F32)],
        compiler_params=_params(("arbitrary", "arbitrary")),
        name="attn",
    )(q, k, v)


_GROUP_SIZE = N_EXPERTS // N_EXPERT_GROUPS
_BIG = 1.0e9


def _mix_out_kernel(attn_ref, yc_ref, x_ref, mod_ref, gattn_ref, wout_ref, gpost_ref, gpre2_ref,
                    wrt_ref, br_ref, x1_ref, h2_ref, h2p_ref, idx_ref, wts_ref, rank_ref, cnt_ref,
                    carry_scr, scores_scr, sel_scr):
    i = pl.program_id(0)
    n_tiles = pl.num_programs(0) - 1

    @pl.when(i == 0)
    def _():
        carry_scr[...] = jnp.zeros(carry_scr.shape, F32)
        scores, sel = _project_tile(attn_ref, yc_ref, x_ref, mod_ref, gattn_ref, wout_ref, gpost_ref,
                                    gpre2_ref, wrt_ref, br_ref, x1_ref, h2_ref, h2p_ref)
        scores_scr[0] = scores
        sel_scr[0] = sel

    @pl.when(jnp.logical_and(i >= 1, i < n_tiles))
    def _():
        prev_scores = scores_scr[(i - 1) % 2]
        prev_sel = sel_scr[(i - 1) % 2]
        scores, sel = _project_tile(attn_ref, yc_ref, x_ref, mod_ref, gattn_ref, wout_ref, gpost_ref,
                                    gpre2_ref, wrt_ref, br_ref, x1_ref, h2_ref, h2p_ref)
        scores_scr[i % 2] = scores
        sel_scr[i % 2] = sel
        _route_tile(prev_scores, prev_sel, idx_ref, wts_ref, rank_ref, cnt_ref, carry_scr)

    @pl.when(i == n_tiles)
    def _():
        _route_tile(scores_scr[(i - 1) % 2], sel_scr[(i - 1) % 2], idx_ref, wts_ref, rank_ref, cnt_ref,
                    carry_scr)


def _project_tile(attn_ref, yc_ref, x_ref, mod_ref, gattn_ref, wout_ref, gpost_ref, gpre2_ref,
                  wrt_ref, br_ref, x1_ref, h2_ref, h2p_ref):
    half = attn_ref.shape[1]
    an = _rms(attn_ref[...].astype(F32), gattn_ref[...]).astype(BF16)
    mix = (jnp.dot(an, wout_ref[0:half, :], preferred_element_type=F32)
           + jnp.dot(yc_ref[...], wout_ref[half:, :], preferred_element_type=F32))
    g1 = mod_ref[0, 2:3, :]
    sh2 = mod_ref[0, 3:4, :]
    sc2 = mod_ref[0, 4:5, :]
    x1 = x_ref[...] + g1 * _rms(mix, gpost_ref[...])
    x1_ref[...] = x1
    h2 = _rms(x1, gpre2_ref[...]) * (1.0 + sc2) + sh2
    h2_ref[...] = h2.astype(BF16)
    h2p_ref[...] = _pack_row_words(h2[:, 0:_ROW_WORDS], h2[:, _ROW_WORDS:])

    logits = lax.dot_general(wrt_ref[...], h2, (((1,), (1,)), ((), ())),
                             preferred_element_type=F32, precision=lax.Precision.HIGHEST)
    scores = jax.nn.sigmoid(logits)
    return scores, scores + br_ref[...]


def _route_tile(scores, sel, idx_ref, wts_ref, rank_ref, cnt_ref, carry_scr):
    tm = scores.shape[1]
    row = lax.broadcasted_iota(I32, (N_EXPERTS, tm), 0).astype(F32)

    gscore = []
    rw = lax.broadcasted_iota(I32, (_GROUP_SIZE, tm), 0).astype(F32)
    for g in range(N_EXPERT_GROUPS):
        blk = sel[g * _GROUP_SIZE:(g + 1) * _GROUP_SIZE, :]
        m1 = jnp.max(blk, axis=0, keepdims=True)
        i1 = jnp.min(jnp.where(blk == m1, rw, _BIG), axis=0, keepdims=True)
        m2 = jnp.max(jnp.where(rw == i1, NEG_INF, blk), axis=0, keepdims=True)
        gscore.append(m1 + m2)

    gkeep = [jnp.zeros((1, tm), F32) for _ in range(N_EXPERT_GROUPS)]
    for _ in range(TOPK_GROUPS):
        mg = functools.reduce(jnp.maximum, gscore)
        ig = functools.reduce(jnp.minimum, [jnp.where(gscore[g] == mg, float(g), _BIG)
                                            for g in range(N_EXPERT_GROUPS)])
        for g in range(N_EXPERT_GROUPS):
            hit = ig == float(g)
            gkeep[g] = jnp.where(hit, 1.0, gkeep[g])
            gscore[g] = jnp.where(hit, NEG_INF, gscore[g])
    n_slabs = N_EXPERTS // SUBLANES
    slabs_per_group = _GROUP_SIZE // SUBLANES
    sub = lax.broadcasted_iota(I32, (SUBLANES, tm), 0).astype(F32)
    first_rows = [jnp.where(gkeep[j // slabs_per_group] > 0.0, sel[j * SUBLANES:(j + 1) * SUBLANES, :], NEG_INF)
                  for j in range(n_slabs)]
    cur_rows = list(first_rows)
    krow = lax.broadcasted_iota(I32, (TOP_K, tm), 0)
    idx_rows = []
    idx_f = jnp.zeros((TOP_K, tm), F32)
    sc_k = jnp.zeros((TOP_K, tm), F32)
    sc_sum = jnp.zeros((1, tm), F32)
    prev = None
    for k in range(TOP_K):
        best = jnp.full((SUBLANES, tm), NEG_INF, F32)
        best_slab = jnp.zeros((SUBLANES, tm), F32)
        best_score = jnp.zeros((SUBLANES, tm), F32)
        for j in range(n_slabs):
            if prev is not None:
                cur_rows[j] = jnp.where(sub == prev - float(j * SUBLANES), NEG_INF, cur_rows[j])
            better = cur_rows[j] > best
            best = jnp.where(better, cur_rows[j], best)
            best_slab = jnp.where(better, float(j), best_slab)
            best_score = jnp.where(better, scores[j * SUBLANES:(j + 1) * SUBLANES, :], best_score)
        best_idx = best_slab * float(SUBLANES) + sub
        m = jnp.max(best, axis=0, keepdims=True)
        ik = jnp.min(jnp.where(best == m, best_idx, _BIG), axis=0, keepdims=True)
        sk = jnp.sum(jnp.where(best_idx == ik, best_score, 0.0), axis=0, keepdims=True)
        prev = ik
        idx_rows.append(ik)
        idx_f = jnp.where(krow == k, ik, idx_f)
        sc_k = jnp.where(krow == k, sk, sc_k)
        sc_sum = sc_sum + sk
    wts_ref[...] = sc_k / sc_sum * ROUTED_SCALE
    idx_ref[...] = idx_f.astype(I32)
    onehot = jnp.concatenate(
        [jnp.where(jnp.where(sub == prev - float(j * SUBLANES), NEG_INF, cur_rows[j]) != first_rows[j], 1.0, 0.0)
         for j in range(n_slabs)], axis=0)

    tri = (lax.broadcasted_iota(I32, (tm, tm), 0) < lax.broadcasted_iota(I32, (tm, tm), 1))
    excl = jnp.dot(onehot.astype(BF16), tri.astype(BF16), preferred_element_type=F32)
    rank_e = carry_scr[:, 0:1] + excl
    rank_k = jnp.zeros((TOP_K, tm), F32)
    for k in range(TOP_K):
        hit = row == idx_rows[k]
        rk = jnp.sum(jnp.where(hit, rank_e, 0.0), axis=0, keepdims=True)
        rank_k = jnp.where(krow == k, rk, rank_k)
    rank_ref[...] = rank_k.astype(I32)
    carry_scr[...] = carry_scr[...] + jnp.sum(onehot, axis=1, keepdims=True)
    cnt_ref[...] = carry_scr[...]


def _mix_out(attn, yc, x2, mod3, gattn, wout, gpost, gpre2, wrt, br, seq):
    t, d = x2.shape
    tm = min(TM_OUT, seq)
    tpb = seq // tm
    n_tiles = t // tm
    last = n_tiles - 1
    full = lambda a: pl.BlockSpec(a.shape, lambda i: (0,) * a.ndim)
    row = lambda w: pl.BlockSpec((tm, w), lambda i: (jnp.minimum(i, last), 0))
    col = pl.BlockSpec((TOP_K, tm), lambda i: (0, jnp.maximum(i - 1, 0)))
    return pl.pallas_call(
        _mix_out_kernel,
        grid=(n_tiles + 1,),
        in_specs=[row(attn.shape[1]), row(yc.shape[1]), row(d),
                  pl.BlockSpec((1, 6, d), lambda i: (jnp.minimum(i, last) // tpb, 0, 0)),
                  full(gattn), full(wout), full(gpost), full(gpre2), full(wrt), full(br)],
        out_specs=[row(d), row(d), row(_ROW_WORDS), col, col, col,
                   pl.BlockSpec((N_EXPERTS, LANES), lambda i: (0, 0))],
        out_shape=[jax.ShapeDtypeStruct((t, d), F32), jax.ShapeDtypeStruct((t, d), BF16),
                   jax.ShapeDtypeStruct((t, _ROW_WORDS), U32),
                   jax.ShapeDtypeStruct((TOP_K, t), I32), jax.ShapeDtypeStruct((TOP_K, t), F32),
                   jax.ShapeDtypeStruct((TOP_K, t), I32),
                   jax.ShapeDtypeStruct((N_EXPERTS, LANES), F32)],
        scratch_shapes=[pltpu.VMEM((N_EXPERTS, LANES), F32), pltpu.VMEM((2, N_EXPERTS, tm), F32),
                        pltpu.VMEM((2, N_EXPERTS, tm), F32)],
        compiler_params=_params(("arbitrary",)),
        name="mix_out",
    )(attn, yc, x2, mod3, gattn, wout, gpost, gpre2, wrt, br)


def _dest_kernel(idx_ref, rank_ref, pstart_ref, dest_ref):
    tm = idx_ref.shape[1]
    row = lax.broadcasted_iota(I32, (N_EXPERTS, tm), 0)
    krow = lax.broadcasted_iota(I32, (TOP_K, tm), 0)
    pstart = pstart_ref[...]
    idx = idx_ref[...]
    out = jnp.zeros((TOP_K, tm), F32)
    for k in range(TOP_K):
        hit = row == idx[k:k + 1, :]
        base = jnp.sum(jnp.where(hit, pstart, 0.0), axis=0, keepdims=True)
        out = jnp.where(krow == k, base, out)
    dest_ref[...] = out.astype(I32) + rank_ref[...]


def _dest(idx, rank, pstart):
    t = idx.shape[1]
    tm = min(TM_DEST, t)
    col = pl.BlockSpec((TOP_K, tm), lambda i: (0, i))
    return pl.pallas_call(
        _dest_kernel,
        grid=(t // tm,),
        in_specs=[col, col, pl.BlockSpec((N_EXPERTS, 1), lambda i: (0, 0))],
        out_specs=col,
        out_shape=jax.ShapeDtypeStruct((TOP_K, t), I32),
        compiler_params=_params(("arbitrary",)),
        name="dest",
    )(idx, rank, pstart)


_PAD_CHUNKS = tuple(BM_EXPERT >> s for s in range(1, BM_EXPERT.bit_length()))


SC_CORES = 2
SC_SUBCORES = 16
SC_ROWS = 128


def _sc_worker_split(n_chunks):
    workers = SC_CORES * SC_SUBCORES
    per_worker = max(1, n_chunks // workers)
    active = n_chunks // per_worker
    assert active * per_worker == n_chunks and active <= workers
    return per_worker, active


def _scatter_rows_sc(rows, idx3d, n_out):
    n, width = rows.shape
    n_chunks = n // SC_ROWS
    per_worker, active = _sc_worker_split(n_chunks)
    mesh = plsc.VectorSubcoreMesh(core_axis_name="c", subcore_axis_name="s")

    @functools.partial(
        pl.kernel, mesh=mesh, out_type=jax.ShapeDtypeStruct((n_out, width), rows.dtype),
        scratch_types=[pltpu.VMEM((TOP_K, SC_ROWS), I32), pltpu.VMEM((SC_ROWS, width), rows.dtype),
                       pltpu.SemaphoreType.DMA])
    def scatter(rows_hbm, idx_hbm, out_hbm, idx_v, rows_v, sem):
        wid = lax.axis_index("s") * SC_CORES + lax.axis_index("c")

        @pl.when(wid < active)
        def _():
            @pl.loop(0, per_worker)
            def _(c):
                chunk = wid * per_worker + c
                pltpu.sync_copy(rows_hbm.at[pl.ds(chunk * SC_ROWS, SC_ROWS)], rows_v)
                pltpu.sync_copy(idx_hbm.at[chunk], idx_v)
                copies = [pltpu.make_async_copy(rows_v, out_hbm.at[idx_v.at[k]], sem) for k in range(TOP_K)]
                for cp in copies:
                    cp.start()
                for cp in copies:
                    cp.wait()

    return scatter(rows, idx3d)


def _padfill_kernel(pad_from_ref, pad_n_ref, xs_in_hbm, xs_hbm, zero_scr, pad_sem):
    del xs_in_hbm
    zero_scr[...] = jnp.zeros(zero_scr.shape, zero_scr.dtype)

    def pad_copies(e, act):
        n = pad_n_ref[e]
        base = pad_from_ref[e]

        def single_rows(start, count):
            for j in range(SUBLANES - 1):
                @pl.when(j < count)
                def _():
                    act(pltpu.make_async_copy(zero_scr.at[pl.ds(0, 1), :],
                                              xs_hbm.at[pl.ds(start + j, 1), :], pad_sem))

        head = jnp.minimum(n, (SUBLANES - (base & (SUBLANES - 1))) & (SUBLANES - 1))
        single_rows(base, head)
        rest = n - head
        mid = base + head
        for rows in _PAD_CHUNKS:
            if rows >= SUBLANES:
                @pl.when((rest & rows) != 0)
                def _():
                    start = pl.multiple_of(mid + (rest & ~(2 * rows - 1)), SUBLANES)
                    act(pltpu.make_async_copy(zero_scr.at[pl.ds(0, rows), :],
                                              xs_hbm.at[pl.ds(start, rows), :], pad_sem))
        single_rows(mid + (rest & ~(SUBLANES - 1)), rest & (SUBLANES - 1))

    def issue_pad(e, carry):
        pad_copies(e, lambda cp: cp.start())
        return carry

    def drain_pad(e, carry):
        pad_copies(e, lambda cp: cp.wait())
        return carry

    lax.fori_loop(0, N_EXPERTS, issue_pad, 0)
    lax.fori_loop(0, N_EXPERTS, drain_pad, 0)


def _padfill(pad_from, pad_n, xs):
    return pl.pallas_call(
        _padfill_kernel,
        grid_spec=pltpu.PrefetchScalarGridSpec(
            num_scalar_prefetch=2,
            grid=(1,),
            in_specs=[pl.BlockSpec(memory_space=pl.ANY)],
            out_specs=pl.BlockSpec(memory_space=pl.ANY),
            scratch_shapes=[pltpu.VMEM((BM_EXPERT // 2, _ROW_WORDS), U32), pltpu.SemaphoreType.DMA]),
        out_shape=jax.ShapeDtypeStruct(xs.shape, xs.dtype),
        input_output_aliases={2: 0},
        compiler_params=_params(("arbitrary",)),
        name="padfill",
    )(pad_from, pad_n, xs)


_XS_SLOTS = 4
_YS_SLOTS = 2
_W_SLOTS = 4
_W_AHEAD = 2


def _expert_kernel(first_ref, ord_ref, uexp_ref, meta_ref, xs_hbm, wg_hbm, wu_hbm, wd_hbm, ys_hbm,
                   xs_buf, ys_buf, wg_buf, wu_buf, wd_buf, act_scr, xs_sem, ys_sem, w_sem):
    i = pl.program_id(0)
    nused = meta_ref[0]
    nexp = meta_ref[1]
    bm = xs_buf.shape[1]

    def xs_copy(b, slot):
        return pltpu.make_async_copy(xs_hbm.at[pl.ds(b * bm, bm), :], xs_buf.at[slot], xs_sem.at[slot])

    def ys_copy(b, slot):
        return pltpu.make_async_copy(ys_buf.at[slot], ys_hbm.at[pl.ds(b * bm, bm), :], ys_sem.at[slot])

    def w_copies(j, slot):
        e = uexp_ref[j]
        return (pltpu.make_async_copy(wg_hbm.at[e], wg_buf.at[slot], w_sem.at[slot, 0]),
                pltpu.make_async_copy(wu_hbm.at[e], wu_buf.at[slot], w_sem.at[slot, 1]),
                pltpu.make_async_copy(wd_hbm.at[e], wd_buf.at[slot], w_sem.at[slot, 2]))

    @pl.when(i == 0)
    def _():
        for s in range(_XS_SLOTS - 1):
            @pl.when(s < nused)
            def _():
                xs_copy(s, s).start()
        for s in range(_W_AHEAD):
            @pl.when(s < nexp)
            def _():
                for cp in w_copies(s, s):
                    cp.start()

    def fetch(b):
        ahead = b + _XS_SLOTS - 1

        @pl.when(ahead < nused)
        def _():
            xs_copy(ahead, ahead % _XS_SLOTS).start()

        j = ord_ref[b]

        @pl.when(first_ref[b] == 1)
        def _():
            for cp in w_copies(j, j % _W_SLOTS):
                cp.wait()
            nxt = j + _W_AHEAD

            @pl.when(nxt < nexp)
            def _():
                for cp in w_copies(nxt, nxt % _W_SLOTS):
                    cp.start()

        xs_copy(b, b % _XS_SLOTS).wait()

    def gate_up(b):
        ws = ord_ref[b] % _W_SLOTS
        lo, hi = _unpack_row_words(xs_buf[b % _XS_SLOTS])
        g = (jnp.dot(lo, wg_buf[ws, 0:_ROW_WORDS, :], preferred_element_type=F32)
             + jnp.dot(hi, wg_buf[ws, _ROW_WORDS:, :], preferred_element_type=F32))
        u = (jnp.dot(lo, wu_buf[ws, 0:_ROW_WORDS, :], preferred_element_type=F32)
             + jnp.dot(hi, wu_buf[ws, _ROW_WORDS:, :], preferred_element_type=F32))
        return g * jax.nn.sigmoid(g) * u

    def down(b, act):
        y = jnp.dot(act, wd_buf[ord_ref[b] % _W_SLOTS], preferred_element_type=F32)
        oslot = b % _YS_SLOTS
        ys_buf[oslot] = _pack_row_words(y[:, 0:_ROW_WORDS], y[:, _ROW_WORDS:])
        ys_copy(b, oslot).start()

    @pl.when(jnp.logical_and(i >= _YS_SLOTS + 1, i <= nused))
    def _():
        ys_copy(i - 1 - _YS_SLOTS, (i - 1) % _YS_SLOTS).wait()

    @pl.when(i == 0)
    def _():
        fetch(i)
        act_scr[...] = gate_up(i)

    @pl.when(jnp.logical_and(i >= 1, i < nused))
    def _():
        fetch(i)
        prev = act_scr[...]
        act_scr[...] = gate_up(i)
        down(i - 1, prev)

    @pl.when(i == nused)
    def _():
        down(i - 1, act_scr[...])
        ys_copy(i - 1, (i - 1) % _YS_SLOTS).wait()

        @pl.when(i >= 2)
        def _():
            ys_copy(i - 2, (i - 2) % _YS_SLOTS).wait()


def _experts(first, ordinal, uexp, meta, xs, w_gate, w_up, w_down):
    p = xs.shape[0]
    d = w_gate.shape[1]
    nb = p // BM_EXPERT
    anyspec = pl.BlockSpec(memory_space=pl.ANY)
    return pl.pallas_call(
        _expert_kernel,
        grid_spec=pltpu.PrefetchScalarGridSpec(
            num_scalar_prefetch=4,
            grid=(nb + 1,),
            in_specs=[anyspec, anyspec, anyspec, anyspec],
            out_specs=anyspec,
            scratch_shapes=[pltpu.VMEM((_XS_SLOTS, BM_EXPERT, _ROW_WORDS), U32),
                            pltpu.VMEM((_YS_SLOTS, BM_EXPERT, _ROW_WORDS), U32),
                            pltpu.VMEM((_W_SLOTS, d, EXPERT_DIM), F32),
                            pltpu.VMEM((_W_SLOTS, d, EXPERT_DIM), F32),
                            pltpu.VMEM((_W_SLOTS, EXPERT_DIM, d), F32),
                            pltpu.VMEM((BM_EXPERT, EXPERT_DIM), F32),
                            pltpu.SemaphoreType.DMA((_XS_SLOTS,)), pltpu.SemaphoreType.DMA((_YS_SLOTS,)),
                            pltpu.SemaphoreType.DMA((_W_SLOTS, 3))]),
        out_shape=jax.ShapeDtypeStruct((p, _ROW_WORDS), U32),
        compiler_params=_params(("arbitrary",)),
        name="experts",
    )(first, ordinal, uexp, meta, xs, w_gate, w_up, w_down)


SC_GATHER_ROWS = 64


def _gather_rows_sc(table, idx):
    n = idx.shape[0]
    width = table.shape[1]
    rows = SC_GATHER_ROWS
    per_worker, active = _sc_worker_split(n // rows)
    assert per_worker % 2 == 0 or per_worker == 1
    mesh = plsc.VectorSubcoreMesh(core_axis_name="c", subcore_axis_name="s")

    @functools.partial(
        pl.kernel, mesh=mesh, out_type=jax.ShapeDtypeStruct((n, width), table.dtype),
        scratch_types=[pltpu.VMEM((per_worker, rows), I32), pltpu.VMEM((2, rows, width), table.dtype),
                       pltpu.SemaphoreType.DMA((2,))])
    def gather(table_hbm, idx_hbm, out_hbm, idx_v, rows_v, sem):
        wid = lax.axis_index("s") * SC_CORES + lax.axis_index("c")

        def fetch(c, b):
            return pltpu.make_async_copy(table_hbm.at[idx_v.at[c]], rows_v.at[b], sem.at[b])

        @pl.when(wid < active)
        def _():
            first = wid * per_worker
            pltpu.sync_copy(idx_hbm.at[pl.ds(first, per_worker)], idx_v)
            fetch(0, 0).start()

            @pl.loop(0, per_worker, step=2)
            def _(c):
                for b in range(min(2, per_worker)):
                    cur = c + b

                    @pl.when(cur + 1 < per_worker)
                    def _():
                        fetch(cur + 1, 1 - b).start()

                    fetch(cur, b).wait()
                    pltpu.sync_copy(rows_v.at[b], out_hbm.at[pl.ds((first + cur) * rows, rows)])

    return gather(table, idx.reshape(n // rows, rows))


def _combine_kernel(wts_ref, yg_ref, h2_ref, x1_ref, mod_ref, wsg_ref, wsu_ref, wsd_ref, gpost_ref, *rest):
    o_ref = rest[-1]
    h2 = h2_ref[...]
    g = jnp.dot(h2, wsg_ref[...], preferred_element_type=F32)
    u = jnp.dot(h2, wsu_ref[...], preferred_element_type=F32)
    f = jnp.dot((g * jax.nn.sigmoid(g) * u).astype(BF16), wsd_ref[...], preferred_element_type=F32)

    wts = wts_ref[...]
    los = [f[:, sl * LANES:(sl + 1) * LANES] for sl in range(_ROW_SLABS)]
    his = [f[:, _ROW_WORDS + sl * LANES:_ROW_WORDS + (sl + 1) * LANES] for sl in range(_ROW_SLABS)]
    for k in range(TOP_K):
        wk = wts[:, k:k + 1]
        for sl in range(_ROW_SLABS):
            lo, hi = _unpack_row_words(yg_ref[k, :, sl * LANES:(sl + 1) * LANES])
            los[sl] = los[sl] + wk * lo
            his[sl] = his[sl] + wk * hi
    f = jnp.concatenate(los + his, axis=1)
    g2 = mod_ref[0, 5:6, :]
    o_ref[...] = x1_ref[...] + g2 * _rms(f, gpost_ref[...])


def _combine(wts_t, yg, h2, x1, mod3, wsg, wsu, wsd, gpost, seq, first_tile, partial_out):
    t, d = x1.shape
    tm = min(TM_COMBINE, seq)
    tpb = seq // tm
    full = lambda a: pl.BlockSpec(a.shape, lambda i: (0,) * a.ndim)
    row = lambda w: pl.BlockSpec((tm, w), lambda i: (i + first_tile, 0))
    args = [wts_t, yg, h2, x1, mod3, wsg, wsu, wsd, gpost]
    in_specs = [row(TOP_K), pl.BlockSpec((TOP_K, tm, _ROW_WORDS), lambda i: (0, i, 0)), row(d), row(d),
                pl.BlockSpec((1, 6, d), lambda i: ((i + first_tile) // tpb, 0, 0)),
                full(wsg), full(wsu), full(wsd), full(gpost)]
    aliases = {}
    if partial_out is not None:
        aliases = {len(args): 0}
        args.append(partial_out)
        in_specs.append(pl.BlockSpec(memory_space=pl.ANY))
    return pl.pallas_call(
        _combine_kernel,
        grid=(yg.shape[1] // tm,),
        in_specs=in_specs,
        out_specs=row(d),
        out_shape=jax.ShapeDtypeStruct((t, d), F32),
        input_output_aliases=aliases,
        compiler_params=_params(("arbitrary",)),
        name="combine",
    )(*args)


def _pack_weights(w_in, w_uq, w_ukv):
    d = w_in.shape[0]
    half = QK_ROPE_DIM // 2
    z = lambda n, c: jnp.zeros((n, c), F32)
    o = Q_LORA_RANK + KV_LORA_RANK
    kr = w_in[:, o:o + QK_ROPE_DIM]
    kr_grp = jnp.concatenate([z(d, QK_NOPE_DIM), kr, z(d, HEAD_PAD - QK_NOPE_DIM - QK_ROPE_DIM)], axis=1)
    kr_rot = jnp.concatenate([z(d, QK_NOPE_DIM), -kr[:, half:], kr[:, :half],
                              z(d, HEAD_PAD - QK_NOPE_DIM - QK_ROPE_DIM)], axis=1)
    win_p = jnp.concatenate([w_in[:, :o], kr_grp, kr_rot, w_in[:, o + QK_ROPE_DIM:]], axis=1)

    scale = float(QK_NOPE_DIM + QK_ROPE_DIM) ** -0.5 * float(np.log2(np.e))
    r = Q_LORA_RANK
    qd = QK_NOPE_DIM + QK_ROPE_DIM
    q_grp, q_rot = [], []
    for h in range(MLA_HEADS):
        nope = w_uq[:, h * qd:h * qd + QK_NOPE_DIM]
        rope = w_uq[:, h * qd + QK_NOPE_DIM:(h + 1) * qd]
        pad = z(r, HEAD_PAD - qd)
        q_grp.append(jnp.concatenate([nope, rope, pad], axis=1))
        q_rot.append(jnp.concatenate([z(r, QK_NOPE_DIM), -rope[:, half:], rope[:, :half], pad], axis=1))
    wuq_p = jnp.concatenate(q_grp + q_rot, axis=1) * scale

    c = KV_LORA_RANK
    kd = QK_NOPE_DIM + V_HEAD_DIM
    k_grp, v_grp = [], []
    for h in range(MLA_HEADS):
        k_grp.append(jnp.concatenate([w_ukv[:, h * kd:h * kd + QK_NOPE_DIM], z(c, HEAD_PAD - QK_NOPE_DIM)], axis=1))
        v_grp.append(jnp.concatenate([w_ukv[:, h * kd + QK_NOPE_DIM:(h + 1) * kd], z(c, HEAD_PAD - V_HEAD_DIM)], axis=1))
    wukv_p = jnp.concatenate(k_grp + v_grp, axis=1)
    return win_p.astype(BF16), wuq_p.astype(BF16), wukv_p.astype(BF16)


def _rope_inputs(positions):
    inv = 1.0 / (ROPE_THETA ** (jnp.arange(0, QK_ROPE_DIM, 2, dtype=F32) / QK_ROPE_DIM))
    freq = jnp.tile(inv, LANES // inv.shape[0]).reshape(1, LANES)
    pos = jnp.repeat(positions.astype(F32).reshape(-1, _ROPE_PACK), QK_ROPE_DIM, axis=1)
    return pos, freq


def _layer(x2, c, pos, freq, batch, seq, w_ada, b_ada, g_pre_mix, w_in, g_q_lat, w_uq, g_kv_lat, w_ukv,
           w_conv, g_attn_out, g_conv_out, w_out, g_post_mix, g_pre_ffn, w_router, b_router,
           w_gate, w_up, w_down, w_sh_gate, w_sh_up, w_sh_down, g_post_ffn):
    t, d = x2.shape
    r1 = lambda a: a.reshape(1, -1)

    c_pad = jnp.zeros((SUBLANES, d), F32).at[:batch].set(c)
    mod = _ada(c_pad, w_ada, r1(b_ada))[:batch]
    mod3 = mod.reshape(batch, 6, d)

    win_p, wuq_p, wukv_p = _pack_weights(w_in, w_uq, w_ukv)
    vone = jnp.zeros((1, HEAD_PAD), F32).at[0, V_HEAD_DIM].set(1.0)
    q, k, v, yc = _mix_in(x2, mod3, r1(g_pre_mix), win_p, r1(g_q_lat), wuq_p, r1(g_kv_lat), wukv_p,
                          vone, w_conv, r1(g_conv_out), pos, freq, seq)
    attn = _attention(q, k, v, batch, seq)
    x1, h2, h2p, idx, wts, rank, cnt = _mix_out(
        attn, yc, x2, mod3, r1(g_attn_out), w_out.astype(BF16), r1(g_post_mix), r1(g_pre_ffn),
        w_router.T, b_router.reshape(-1, 1), seq)

    counts = cnt[:, 0].astype(I32)
    padded = ((counts + BM_EXPERT - 1) // BM_EXPERT) * BM_EXPERT
    pad_end = jnp.cumsum(padded)
    pad_start = pad_end - padded
    m = t * TOP_K
    nb = (m + N_EXPERTS * (BM_EXPERT - 1)) // BM_EXPERT
    nused = pad_end[-1] // BM_EXPERT
    bidx = jnp.arange(nb, dtype=I32)
    blk_exp = jnp.sum((pad_end[None, :] <= (bidx * BM_EXPERT)[:, None]).astype(I32), axis=1)
    first = ((bidx < nused) & ((bidx == 0) | (blk_exp != jnp.roll(blk_exp, 1)))).astype(I32)
    ordinal = jnp.maximum(jnp.cumsum(first) - 1, 0).astype(I32)
    seen = jnp.cumsum((counts > 0).astype(I32))
    uexp = jnp.minimum(jnp.sum((seen[None, :] <= jnp.arange(N_EXPERTS, dtype=I32)[:, None]).astype(I32), axis=1),
                       N_EXPERTS - 1).astype(I32)
    meta = jnp.stack([nused, seen[-1]]).astype(I32)

    dest = _dest(idx, rank, pad_start.astype(F32).reshape(-1, 1))
    xs = _scatter_rows_sc(h2p, dest.reshape(TOP_K, -1, SC_ROWS).transpose(1, 0, 2), nb * BM_EXPERT)
    xs = _padfill((pad_start + counts).astype(I32), (padded - counts).astype(I32), xs)
    ys = _experts(first, ordinal, uexp, meta, xs, w_gate, w_up, w_down)
    tm_c = min(TM_COMBINE, seq)
    half_rows = (t // 2) * TOP_K
    even_split = half_rows % (2 * SC_GATHER_ROWS * SC_CORES * SC_SUBCORES) == 0
    parts = 2 if t % (2 * tm_c) == 0 and even_split else 1
    tp = t // parts
    wsg, wsu, wsd = w_sh_gate.astype(BF16), w_sh_up.astype(BF16), w_sh_down.astype(BF16)
    out = None
    for part in range(parts):
        part_dest = dest[:, part * tp:(part + 1) * tp].reshape(-1)
        yg = _gather_rows_sc(ys, part_dest).reshape(TOP_K, tp, _ROW_WORDS)
        out = _combine(wts.T, yg, h2, x1, mod3, wsg, wsu, wsd, r1(g_post_ffn), seq, part * tp // tm_c, out)
    return out


def kernel(x, c, positions, w_ada, b_ada, g_pre_mix, w_in, g_q_lat, w_uq, g_kv_lat, w_ukv, w_conv, g_attn_out, g_conv_out, w_out, g_post_mix, g_pre_ffn, w_router, b_router, w_gate, w_up, w_down, w_sh_gate, w_sh_up, w_sh_down, g_post_ffn):
    batch, seq, d = x.shape
    pos, freq = _rope_inputs(positions)
    x2 = x.reshape(batch * seq, d)
    for l in range(w_ada.shape[0]):
        x2 = _layer(x2, c, pos, freq, batch, seq, w_ada[l], b_ada[l], g_pre_mix[l], w_in[l], g_q_lat[l],
                    w_uq[l], g_kv_lat[l], w_ukv[l], w_conv[l], g_attn_out[l], g_conv_out[l], w_out[l],
                    g_post_mix[l], g_pre_ffn[l], w_router[l], b_router[l], w_gate[l], w_up[l], w_down[l],
                    w_sh_gate[l], w_sh_up[l], w_sh_down[l], g_post_ffn[l])
    return x2.reshape(batch, seq, d)
```

```python
import functools

import jax
import jax.numpy as jnp
import numpy as np
from jax import lax
from jax.experimental import pallas as pl
from jax.experimental.pallas import tpu as pltpu
from jax.experimental.pallas import tpu_sc as plsc

F32 = jnp.float32
BF16 = jnp.bfloat16
I32 = jnp.int32
U32 = jnp.uint32

CHUNK = 64
MLA_HEADS = 8
QK_NOPE_DIM = 64
QK_ROPE_DIM = 32
V_HEAD_DIM = 64
Q_LORA_RANK = 384
KV_LORA_RANK = 256
ROPE_THETA = 10000.0
CONV_WIDTH = 3
N_EXPERTS = 256
TOP_K = 8
N_EXPERT_GROUPS = 8
TOPK_GROUPS = 4
EXPERT_DIM = 256
ROUTED_SCALE = 2.5
EPS = 1e-6

LANES = 128
SUBLANES = 8
HEAD_PAD = LANES
VMEM_LIMIT_BYTES = 56 * 1024 * 1024

TM_IN = 1024
TQ_ATTN = 512
TM_OUT = 512
TM_DEST = 2048
BM_EXPERT = 256
TM_COMBINE = 512

NEG_INF = float("-inf")


def _rms(x, g):
    return x * lax.rsqrt(jnp.mean(x * x, axis=-1, keepdims=True) + EPS) * g


_HI_MASK = np.uint32(0xFFFF0000)
_ROW_WORDS = 512
_ROW_SLABS = _ROW_WORDS // LANES


def _pack_row_words(lo, hi):
    lo_w = lax.bitcast_convert_type(lo.astype(BF16).astype(F32), U32) >> 16
    hi_w = lax.bitcast_convert_type(hi.astype(BF16).astype(F32), U32) & _HI_MASK
    return lo_w | hi_w


def _unpack_row_words(w):
    return (lax.bitcast_convert_type(w << 16, F32), lax.bitcast_convert_type(w & _HI_MASK, F32))


def _params(sem):
    return pltpu.CompilerParams(dimension_semantics=sem, vmem_limit_bytes=VMEM_LIMIT_BYTES)


def _ada_kernel(c_ref, w_ref, b_ref, o_ref):
    c = c_ref[...]
    s = c * jax.nn.sigmoid(c)
    o_ref[...] = jnp.dot(s, w_ref[...], preferred_element_type=F32,
                         precision=lax.Precision.HIGHEST) + b_ref[...]


def _ada(c_pad, w, b):
    rows, d = c_pad.shape
    n = w.shape[1]
    tn = 1536
    return pl.pallas_call(
        _ada_kernel,
        grid=(n // tn,),
        in_specs=[pl.BlockSpec((rows, d), lambda j: (0, 0)),
                  pl.BlockSpec((d, tn), lambda j: (0, j)),
                  pl.BlockSpec((1, tn), lambda j: (0, j))],
        out_specs=pl.BlockSpec((rows, tn), lambda j: (0, j)),
        out_shape=jax.ShapeDtypeStruct((rows, n), F32),
        compiler_params=_params(("arbitrary",)),
        name="ada",
    )(c_pad, w, b)


_CQ0, _CQ1 = 0, Q_LORA_RANK
_CKV0, _CKV1 = _CQ1, _CQ1 + KV_LORA_RANK
_KR0, _KR1 = _CKV1, _CKV1 + 2 * HEAD_PAD
_CONV_DIM = 512
_GB0 = _KR1
_GC0 = _GB0 + _CONV_DIM
_XV0 = _GC0 + _CONV_DIM
_WIN_COLS = _XV0 + _CONV_DIM
_QW = MLA_HEADS * HEAD_PAD
_ROPE_PACK = LANES // QK_ROPE_DIM


def _mix_in_kernel(tiles_per_batch, x_ref, mod_ref, gpre_ref, win_ref, gq_ref, wuq_ref, gkv_ref,
                   wukv_ref, vone_ref, wconv_ref, gconv_ref, pos_ref, freq_ref,
                   q_ref, k_ref, v_ref, yc_ref, h_scr, u_scr, cos_scr, sin_scr):
    i = pl.program_id(0)
    tm = x_ref.shape[0]
    sh1 = mod_ref[0, 0:1, :]
    sc1 = mod_ref[0, 1:2, :]
    h = _rms(x_ref[...], gpre_ref[...]) * (1.0 + sc1) + sh1
    h_scr[...] = h.astype(BF16)
    ang4 = pos_ref[...] * freq_ref[...]
    cos4 = jnp.cos(ang4)
    sin4 = jnp.sin(ang4)
    lane = lax.broadcasted_iota(I32, ang4.shape, 1)
    on_rope = jnp.logical_and(lane >= QK_NOPE_DIM, lane < QK_NOPE_DIM + QK_ROPE_DIM)
    for g in range(_ROPE_PACK):
        shift = (QK_NOPE_DIM - QK_ROPE_DIM * g) % LANES
        cg = cos4 if shift == 0 else pltpu.roll(cos4, shift, axis=1)
        sg = sin4 if shift == 0 else pltpu.roll(sin4, shift, axis=1)
        cos_scr[pl.ds(g, tm // _ROPE_PACK, stride=_ROPE_PACK), :] = jnp.where(on_rope, cg, 1.0)
        sin_scr[pl.ds(g, tm // _ROPE_PACK, stride=_ROPE_PACK), :] = jnp.where(on_rope, sg, 0.0)
    cos = cos_scr[...]
    sin = sin_scr[...]

    cq = jnp.dot(h_scr[...], win_ref[:, _CQ0:_CQ1], preferred_element_type=F32)
    cqn = _rms(cq, gq_ref[...]).astype(BF16)
    pair = 2 * HEAD_PAD
    for lo in range(0, _QW, pair):
        qa = jnp.dot(cqn, wuq_ref[:, lo:lo + pair], preferred_element_type=F32)
        qr = jnp.dot(cqn, wuq_ref[:, _QW + lo:_QW + lo + pair], preferred_element_type=F32)
        for off in range(0, pair, HEAD_PAD):
            qh = qa[:, off:off + HEAD_PAD] * cos + qr[:, off:off + HEAD_PAD] * sin
            q_ref[:, lo + off:lo + off + HEAD_PAD] = qh.astype(BF16)

    ckv = jnp.dot(h_scr[...], win_ref[:, _CKV0:_CKV1], preferred_element_type=F32)
    ckvn = _rms(ckv, gkv_ref[...]).astype(BF16)
    krr = jnp.dot(h_scr[...], win_ref[:, _KR0:_KR1], preferred_element_type=F32)
    kr = krr[:, 0:HEAD_PAD] * cos + krr[:, HEAD_PAD:2 * HEAD_PAD] * sin
    vone = vone_ref[...]
    for lo in range(0, _QW, pair):
        kk = jnp.dot(ckvn, wukv_ref[:, lo:lo + pair], preferred_element_type=F32)
        vv = jnp.dot(ckvn, wukv_ref[:, _QW + lo:_QW + lo + pair], preferred_element_type=F32)
        for off in range(0, pair, HEAD_PAD):
            k_ref[:, lo + off:lo + off + HEAD_PAD] = (kk[:, off:off + HEAD_PAD] + kr).astype(BF16)
            v_ref[:, lo + off:lo + off + HEAD_PAD] = (vv[:, off:off + HEAD_PAD] + vone).astype(BF16)

    assert wconv_ref.shape[0] == CONV_WIDTH
    first = (i % tiles_per_batch) == 0
    chunks = []
    for c0 in range(0, _CONV_DIM, pair):
        cs = slice(c0, c0 + pair)
        gb = jnp.dot(h_scr[...], win_ref[:, _GB0 + c0:_GB0 + c0 + pair], preferred_element_type=F32)
        gc = jnp.dot(h_scr[...], win_ref[:, _GC0 + c0:_GC0 + c0 + pair], preferred_element_type=F32)
        xv = jnp.dot(h_scr[...], win_ref[:, _XV0 + c0:_XV0 + c0 + pair], preferred_element_type=F32)
        u = gc * xv
        prev = u_scr[tm:tm + SUBLANES, cs]
        u_scr[0:SUBLANES, cs] = jnp.where(first, jnp.zeros_like(prev), prev)
        u_scr[SUBLANES:tm + SUBLANES, cs] = u
        um1 = u_scr[SUBLANES - 1:tm + SUBLANES - 1, cs]
        um2 = u_scr[SUBLANES - 2:tm + SUBLANES - 2, cs]
        conv = wconv_ref[0:1, cs] * um2 + wconv_ref[1:2, cs] * um1 + wconv_ref[2:3, cs] * u
        chunks.append(gb * conv)
    ssq = sum(jnp.sum(y * y, axis=-1, keepdims=True) for y in chunks)
    inv = lax.rsqrt(ssq * (1.0 / _CONV_DIM) + EPS)
    for j, y in enumerate(chunks):
        cs = slice(j * pair, (j + 1) * pair)
        yc_ref[:, cs] = (y * inv * gconv_ref[:, cs]).astype(BF16)


def _mix_in(x2, mod3, gpre, win_p, gq, wuq_p, gkv, wukv_p, vone, wconv, gconv, pos, freq, seq):
    t, d = x2.shape
    tm = min(TM_IN, seq)
    tpb = seq // tm
    full = lambda a: pl.BlockSpec(a.shape, lambda i: (0,) * a.ndim)
    row = lambda w: pl.BlockSpec((tm, w), lambda i: (i, 0))
    return pl.pallas_call(
        functools.partial(_mix_in_kernel, tpb),
        grid=(t // tm,),
        in_specs=[row(d),
                  pl.BlockSpec((1, 6, d), lambda i: (i // tpb, 0, 0)),
                  full(gpre), full(win_p), full(gq), full(wuq_p), full(gkv), full(wukv_p),
                  full(vone), full(wconv), full(gconv),
                  pl.BlockSpec((tm // _ROPE_PACK, LANES), lambda i: (i, 0)), full(freq)],
        out_specs=[row(_QW), row(_QW), row(_QW), row(_CONV_DIM)],
        out_shape=[jax.ShapeDtypeStruct((t, _QW), BF16), jax.ShapeDtypeStruct((t, _QW), BF16),
                   jax.ShapeDtypeStruct((t, _QW), BF16), jax.ShapeDtypeStruct((t, _CONV_DIM), BF16)],
        scratch_shapes=[pltpu.VMEM((tm, d), BF16), pltpu.VMEM((tm + SUBLANES, _CONV_DIM), F32),
                        pltpu.VMEM((tm, HEAD_PAD), F32), pltpu.VMEM((tm, HEAD_PAD), F32)],
        compiler_params=_params(("arbitrary",)),
        name="mix_in",
    )(x2, mod3, gpre, win_p, gq, wuq_p, gkv, wukv_p, vone, wconv, gconv, pos, freq)


_HEADS_PER_STEP = 2


def _attn_kernel(tq, q_ref, k_ref, v_ref, o_ref, s_scr, mrun_scr, mb_scr, acc_scr):
    tk = tq
    nq = q_ref.shape[0] // tq
    lane_groups = tk // LANES
    heads = range(_HEADS_PER_STEP)
    lanes = [slice(hh * HEAD_PAD, (hh + 1) * HEAD_PAD) for hh in heads]

    def tile_max(s):
        m = s[:, 0:LANES]
        for g in range(1, lane_groups):
            m = jnp.maximum(m, s[:, g * LANES:(g + 1) * LANES])
        return m

    def scores(hh, qi, kv):
        off = pl.multiple_of(kv * tk, tk)
        return lax.dot_general(q_ref[qi * tq:(qi + 1) * tq, lanes[hh]], k_ref[pl.ds(off, tk), lanes[hh]],
                               (((1,), (1,)), ((), ())), preferred_element_type=F32)

    rc = lax.broadcasted_iota(I32, (tq, tk), 0) // CHUNK
    cc = lax.broadcasted_iota(I32, (tq, tk), 1) // CHUNK

    def diagonal(qi):
        for hh in heads:
            s = jnp.where(cc <= rc, scores(hh, qi, qi), NEG_INF)
            s_scr[hh, qi] = s
            m_row = jnp.max(jnp.maximum(mrun_scr[hh], tile_max(s)), axis=1, keepdims=True)
            mb_scr[hh] = jnp.broadcast_to(m_row, (tq, LANES))

    mrun_scr[...] = jnp.full(mrun_scr.shape, NEG_INF, F32)
    diagonal(0)
    for qi in range(nq):
        has_next = qi + 1 < nq
        acc_scr[...] = jnp.zeros(acc_scr.shape, F32)
        if has_next:
            mrun_scr[...] = jnp.full(mrun_scr.shape, NEG_INF, F32)

        def body(kv, carry, qi=qi, has_next=has_next):
            off = pl.multiple_of(kv * tk, tk)
            for hh in heads:
                mb = mb_scr[hh]
                p = jnp.concatenate(
                    [jnp.exp2(s_scr[hh, kv, :, g * LANES:(g + 1) * LANES] - mb) for g in range(lane_groups)],
                    axis=1).astype(BF16)
                acc_scr[hh] += jnp.dot(p, v_ref[pl.ds(off, tk), lanes[hh]], preferred_element_type=F32)
                if has_next:
                    s = scores(hh, qi + 1, kv)
                    s_scr[hh, kv] = s
                    mrun_scr[hh] = jnp.maximum(mrun_scr[hh], tile_max(s))
            return carry

        lax.fori_loop(0, qi + 1, body, 0, unroll=4)
        for hh in heads:
            acc = acc_scr[hh]
            o = acc[:, 0:V_HEAD_DIM] / acc[:, V_HEAD_DIM:V_HEAD_DIM + 1]
            o_ref[qi * tq:(qi + 1) * tq, hh * V_HEAD_DIM:(hh + 1) * V_HEAD_DIM] = o.astype(BF16)
        if has_next:
            diagonal(qi + 1)


def _attention(q, k, v, batch, seq):
    t = q.shape[0]
    tq = min(TQ_ATTN, seq)
    nq = seq // tq
    hw = _HEADS_PER_STEP * HEAD_PAD
    ow = _HEADS_PER_STEP * V_HEAD_DIM
    blk = lambda w: pl.BlockSpec((seq, w), lambda b, j: (b, j))
    return pl.pallas_call(
        functools.partial(_attn_kernel, tq),
        grid=(batch, MLA_HEADS // _HEADS_PER_STEP),
        in_specs=[blk(hw), blk(hw), blk(hw)],
        out_specs=blk(ow),
        out_shape=jax.ShapeDtypeStruct((t, MLA_HEADS * V_HEAD_DIM), BF16),
        scratch_shapes=[pltpu.VMEM((_HEADS_PER_STEP, nq, tq, tq), F32),
                        pltpu.VMEM((_HEADS_PER_STEP, tq, LANES), F32),
                        pltpu.VMEM((_HEADS_PER_STEP, tq, LANES), F32),
                        pltpu.VMEM((_HEADS_PER_STEP, tq, HEAD_PAD), F32)],
        compiler_params=_params(("arbitrary", "arbitrary")),
        name="attn",
    )(q, k, v)


_GROUP_SIZE = N_EXPERTS // N_EXPERT_GROUPS
_BIG = 1.0e9


def _mix_out_kernel(attn_ref, yc_ref, x_ref, mod_ref, gattn_ref, wout_ref, gpost_ref, gpre2_ref,
                    wrt_ref, br_ref, x1_ref, h2_ref, h2p_ref, idx_ref, wts_ref, rank_ref, cnt_ref,
                    carry_scr, scores_scr, sel_scr):
    i = pl.program_id(0)
    n_tiles = pl.num_programs(0) - 1

    @pl.when(i == 0)
    def _():
        carry_scr[...] = jnp.zeros(carry_scr.shape, F32)
        scores, sel = _project_tile(attn_ref, yc_ref, x_ref, mod_ref, gattn_ref, wout_ref, gpost_ref,
                                    gpre2_ref, wrt_ref, br_ref, x1_ref, h2_ref, h2p_ref)
        scores_scr[0] = scores
        sel_scr[0] = sel

    @pl.when(jnp.logical_and(i >= 1, i < n_tiles))
    def _():
        prev_scores = scores_scr[(i - 1) % 2]
        prev_sel = sel_scr[(i - 1) % 2]
        scores, sel = _project_tile(attn_ref, yc_ref, x_ref, mod_ref, gattn_ref, wout_ref, gpost_ref,
                                    gpre2_ref, wrt_ref, br_ref, x1_ref, h2_ref, h2p_ref)
        scores_scr[i % 2] = scores
        sel_scr[i % 2] = sel
        _route_tile(prev_scores, prev_sel, idx_ref, wts_ref, rank_ref, cnt_ref, carry_scr)

    @pl.when(i == n_tiles)
    def _():
        _route_tile(scores_scr[(i - 1) % 2], sel_scr[(i - 1) % 2], idx_ref, wts_ref, rank_ref, cnt_ref,
                    carry_scr)


def _project_tile(attn_ref, yc_ref, x_ref, mod_ref, gattn_ref, wout_ref, gpost_ref, gpre2_ref,
                  wrt_ref, br_ref, x1_ref, h2_ref, h2p_ref):
    half = attn_ref.shape[1]
    an = _rms(attn_ref[...].astype(F32), gattn_ref[...]).astype(BF16)
    mix = (jnp.dot(an, wout_ref[0:half, :], preferred_element_type=F32)
           + jnp.dot(yc_ref[...], wout_ref[half:, :], preferred_element_type=F32))
    g1 = mod_ref[0, 2:3, :]
    sh2 = mod_ref[0, 3:4, :]
    sc2 = mod_ref[0, 4:5, :]
    x1 = x_ref[...] + g1 * _rms(mix, gpost_ref[...])
    x1_ref[...] = x1
    h2 = _rms(x1, gpre2_ref[...]) * (1.0 + sc2) + sh2
    h2_ref[...] = h2.astype(BF16)
    h2p_ref[...] = _pack_row_words(h2[:, 0:_ROW_WORDS], h2[:, _ROW_WORDS:])

    logits = lax.dot_general(wrt_ref[...], h2, (((1,), (1,)), ((), ())),
                             preferred_element_type=F32, precision=lax.Precision.HIGHEST)
    scores = jax.nn.sigmoid(logits)
    return scores, scores + br_ref[...]


def _route_tile(scores, sel, idx_ref, wts_ref, rank_ref, cnt_ref, carry_scr):
    tm = scores.shape[1]
    row = lax.broadcasted_iota(I32, (N_EXPERTS, tm), 0).astype(F32)

    gscore = []
    rw = lax.broadcasted_iota(I32, (_GROUP_SIZE, tm), 0).astype(F32)
    for g in range(N_EXPERT_GROUPS):
        blk = sel[g * _GROUP_SIZE:(g + 1) * _GROUP_SIZE, :]
        m1 = jnp.max(blk, axis=0, keepdims=True)
        i1 = jnp.min(jnp.where(blk == m1, rw, _BIG), axis=0, keepdims=True)
        m2 = jnp.max(jnp.where(rw == i1, NEG_INF, blk), axis=0, keepdims=True)
        gscore.append(m1 + m2)

    gkeep = [jnp.zeros((1, tm), F32) for _ in range(N_EXPERT_GROUPS)]
    for _ in range(TOPK_GROUPS):
        mg = functools.reduce(jnp.maximum, gscore)
        ig = functools.reduce(jnp.minimum, [jnp.where(gscore[g] == mg, float(g), _BIG)
                                            for g in range(N_EXPERT_GROUPS)])
        for g in range(N_EXPERT_GROUPS):
            hit = ig == float(g)
            gkeep[g] = jnp.where(hit, 1.0, gkeep[g])
            gscore[g] = jnp.where(hit, NEG_INF, gscore[g])
    n_slabs = N_EXPERTS // SUBLANES
    slabs_per_group = _GROUP_SIZE // SUBLANES
    sub = lax.broadcasted_iota(I32, (SUBLANES, tm), 0).astype(F32)
    first_rows = [jnp.where(gkeep[j // slabs_per_group] > 0.0, sel[j * SUBLANES:(j + 1) * SUBLANES, :], NEG_INF)
                  for j in range(n_slabs)]
    cur_rows = list(first_rows)
    krow = lax.broadcasted_iota(I32, (TOP_K, tm), 0)
    idx_rows = []
    idx_f = jnp.zeros((TOP_K, tm), F32)
    sc_k = jnp.zeros((TOP_K, tm), F32)
    sc_sum = jnp.zeros((1, tm), F32)
    prev = None
    for k in range(TOP_K):
        best = jnp.full((SUBLANES, tm), NEG_INF, F32)
        best_slab = jnp.zeros((SUBLANES, tm), F32)
        best_score = jnp.zeros((SUBLANES, tm), F32)
        for j in range(n_slabs):
            if prev is not None:
                cur_rows[j] = jnp.where(sub == prev - float(j * SUBLANES), NEG_INF, cur_rows[j])
            better = cur_rows[j] > best
            best = jnp.where(better, cur_rows[j], best)
            best_slab = jnp.where(better, float(j), best_slab)
            best_score = jnp.where(better, scores[j * SUBLANES:(j + 1) * SUBLANES, :], best_score)
        best_idx = best_slab * float(SUBLANES) + sub
        m = jnp.max(best, axis=0, keepdims=True)
        ik = jnp.min(jnp.where(best == m, best_idx, _BIG), axis=0, keepdims=True)
        sk = jnp.sum(jnp.where(best_idx == ik, best_score, 0.0), axis=0, keepdims=True)
        prev = ik
        idx_rows.append(ik)
        idx_f = jnp.where(krow == k, ik, idx_f)
        sc_k = jnp.where(krow == k, sk, sc_k)
        sc_sum = sc_sum + sk
    wts_ref[...] = sc_k / sc_sum * ROUTED_SCALE
    idx_ref[...] = idx_f.astype(I32)
    onehot = jnp.concatenate(
        [jnp.where(jnp.where(sub == prev - float(j * SUBLANES), NEG_INF, cur_rows[j]) != first_rows[j], 1.0, 0.0)
         for j in range(n_slabs)], axis=0)

    tri = (lax.broadcasted_iota(I32, (tm, tm), 0) < lax.broadcasted_iota(I32, (tm, tm), 1))
    excl = jnp.dot(onehot.astype(BF16), tri.astype(BF16), preferred_element_type=F32)
    rank_e = carry_scr[:, 0:1] + excl
    rank_k = jnp.zeros((TOP_K, tm), F32)
    for k in range(TOP_K):
        hit = row == idx_rows[k]
        rk = jnp.sum(jnp.where(hit, rank_e, 0.0), axis=0, keepdims=True)
        rank_k = jnp.where(krow == k, rk, rank_k)
    rank_ref[...] = rank_k.astype(I32)
    carry_scr[...] = carry_scr[...] + jnp.sum(onehot, axis=1, keepdims=True)
    cnt_ref[...] = carry_scr[...]


def _mix_out(attn, yc, x2, mod3, gattn, wout, gpost, gpre2, wrt, br, seq):
    t, d = x2.shape
    tm = min(TM_OUT, seq)
    tpb = seq // tm
    n_tiles = t // tm
    last = n_tiles - 1
    full = lambda a: pl.BlockSpec(a.shape, lambda i: (0,) * a.ndim)
    row = lambda w: pl.BlockSpec((tm, w), lambda i: (jnp.minimum(i, last), 0))
    col = pl.BlockSpec((TOP_K, tm), lambda i: (0, jnp.maximum(i - 1, 0)))
    return pl.pallas_call(
        _mix_out_kernel,
        grid=(n_tiles + 1,),
        in_specs=[row(attn.shape[1]), row(yc.shape[1]), row(d),
                  pl.BlockSpec((1, 6, d), lambda i: (jnp.minimum(i, last) // tpb, 0, 0)),
                  full(gattn), full(wout), full(gpost), full(gpre2), full(wrt), full(br)],
        out_specs=[row(d), row(d), row(_ROW_WORDS), col, col, col,
                   pl.BlockSpec((N_EXPERTS, LANES), lambda i: (0, 0))],
        out_shape=[jax.ShapeDtypeStruct((t, d), F32), jax.ShapeDtypeStruct((t, d), BF16),
                   jax.ShapeDtypeStruct((t, _ROW_WORDS), U32),
                   jax.ShapeDtypeStruct((TOP_K, t), I32), jax.ShapeDtypeStruct((TOP_K, t), F32),
                   jax.ShapeDtypeStruct((TOP_K, t), I32),
                   jax.ShapeDtypeStruct((N_EXPERTS, LANES), F32)],
        scratch_shapes=[pltpu.VMEM((N_EXPERTS, LANES), F32), pltpu.VMEM((2, N_EXPERTS, tm), F32),
                        pltpu.VMEM((2, N_EXPERTS, tm), F32)],
        compiler_params=_params(("arbitrary",)),
        name="mix_out",
    )(attn, yc, x2, mod3, gattn, wout, gpost, gpre2, wrt, br)


def _dest_kernel(idx_ref, rank_ref, pstart_ref, dest_ref):
    tm = idx_ref.shape[1]
    row = lax.broadcasted_iota(I32, (N_EXPERTS, tm), 0)
    krow = lax.broadcasted_iota(I32, (TOP_K, tm), 0)
    pstart = pstart_ref[...]
    idx = idx_ref[...]
    out = jnp.zeros((TOP_K, tm), F32)
    for k in range(TOP_K):
        hit = row == idx[k:k + 1, :]
        base = jnp.sum(jnp.where(hit, pstart, 0.0), axis=0, keepdims=True)
        out = jnp.where(krow == k, base, out)
    dest_ref[...] = out.astype(I32) + rank_ref[...]


def _dest(idx, rank, pstart):
    t = idx.shape[1]
    tm = min(TM_DEST, t)
    col = pl.BlockSpec((TOP_K, tm), lambda i: (0, i))
    return pl.pallas_call(
        _dest_kernel,
        grid=(t // tm,),
        in_specs=[col, col, pl.BlockSpec((N_EXPERTS, 1), lambda i: (0, 0))],
        out_specs=col,
        out_shape=jax.ShapeDtypeStruct((TOP_K, t), I32),
        compiler_params=_params(("arbitrary",)),
        name="dest",
    )(idx, rank, pstart)


_PAD_CHUNKS = tuple(BM_EXPERT >> s for s in range(1, BM_EXPERT.bit_length()))


SC_CORES = 2
SC_SUBCORES = 16
SC_ROWS = 128


def _sc_worker_split(n_chunks):
    workers = SC_CORES * SC_SUBCORES
    per_worker = max(1, n_chunks // workers)
    active = n_chunks // per_worker
    assert active * per_worker == n_chunks and active <= workers
    return per_worker, active


def _scatter_rows_sc(rows, idx3d, n_out):
    n, width = rows.shape
    n_chunks = n // SC_ROWS
    per_worker, active = _sc_worker_split(n_chunks)
    mesh = plsc.VectorSubcoreMesh(core_axis_name="c", subcore_axis_name="s")

    @functools.partial(
        pl.kernel, mesh=mesh, out_type=jax.ShapeDtypeStruct((n_out, width), rows.dtype),
        scratch_types=[pltpu.VMEM((TOP_K, SC_ROWS), I32), pltpu.VMEM((SC_ROWS, width), rows.dtype),
                       pltpu.SemaphoreType.DMA])
    def scatter(rows_hbm, idx_hbm, out_hbm, idx_v, rows_v, sem):
        wid = lax.axis_index("s") * SC_CORES + lax.axis_index("c")

        @pl.when(wid < active)
        def _():
            @pl.loop(0, per_worker)
            def _(c):
                chunk = wid * per_worker + c
                pltpu.sync_copy(rows_hbm.at[pl.ds(chunk * SC_ROWS, SC_ROWS)], rows_v)
                pltpu.sync_copy(idx_hbm.at[chunk], idx_v)
                copies = [pltpu.make_async_copy(rows_v, out_hbm.at[idx_v.at[k]], sem) for k in range(TOP_K)]
                for cp in copies:
                    cp.start()
                for cp in copies:
                    cp.wait()

    return scatter(rows, idx3d)


def _padfill_kernel(pad_from_ref, pad_n_ref, xs_in_hbm, xs_hbm, zero_scr, pad_sem):
    del xs_in_hbm
    zero_scr[...] = jnp.zeros(zero_scr.shape, zero_scr.dtype)

    def pad_copies(e, act):
        n = pad_n_ref[e]
        base = pad_from_ref[e]

        def single_rows(start, count):
            for j in range(SUBLANES - 1):
                @pl.when(j < count)
                def _():
                    act(pltpu.make_async_copy(zero_scr.at[pl.ds(0, 1), :],
                                              xs_hbm.at[pl.ds(start + j, 1), :], pad_sem))

        head = jnp.minimum(n, (SUBLANES - (base & (SUBLANES - 1))) & (SUBLANES - 1))
        single_rows(base, head)
        rest = n - head
        mid = base + head
        for rows in _PAD_CHUNKS:
            if rows >= SUBLANES:
                @pl.when((rest & rows) != 0)
                def _():
                    start = pl.multiple_of(mid + (rest & ~(2 * rows - 1)), SUBLANES)
                    act(pltpu.make_async_copy(zero_scr.at[pl.ds(0, rows), :],
                                              xs_hbm.at[pl.ds(start, rows), :], pad_sem))
        single_rows(mid + (rest & ~(SUBLANES - 1)), rest & (SUBLANES - 1))

    def issue_pad(e, carry):
        pad_copies(e, lambda cp: cp.start())
        return carry

    def drain_pad(e, carry):
        pad_copies(e, lambda cp: cp.wait())
        return carry

    lax.fori_loop(0, N_EXPERTS, issue_pad, 0)
    lax.fori_loop(0, N_EXPERTS, drain_pad, 0)


def _padfill(pad_from, pad_n, xs):
    return pl.pallas_call(
        _padfill_kernel,
        grid_spec=pltpu.PrefetchScalarGridSpec(
            num_scalar_prefetch=2,
            grid=(1,),
            in_specs=[pl.BlockSpec(memory_space=pl.ANY)],
            out_specs=pl.BlockSpec(memory_space=pl.ANY),
            scratch_shapes=[pltpu.VMEM((BM_EXPERT // 2, _ROW_WORDS), U32), pltpu.SemaphoreType.DMA]),
        out_shape=jax.ShapeDtypeStruct(xs.shape, xs.dtype),
        input_output_aliases={2: 0},
        compiler_params=_params(("arbitrary",)),
        name="padfill",
    )(pad_from, pad_n, xs)


_XS_SLOTS = 4
_YS_SLOTS = 2
_W_SLOTS = 4
_W_AHEAD = 2


def _expert_kernel(first_ref, ord_ref, uexp_ref, meta_ref, xs_hbm, wg_hbm, wu_hbm, wd_hbm, ys_hbm,
                   xs_buf, ys_buf, wg_buf, wu_buf, wd_buf, act_scr, xs_sem, ys_sem, w_sem):
    i = pl.program_id(0)
    nused = meta_ref[0]
    nexp = meta_ref[1]
    bm = xs_buf.shape[1]

    def xs_copy(b, slot):
        return pltpu.make_async_copy(xs_hbm.at[pl.ds(b * bm, bm), :], xs_buf.at[slot], xs_sem.at[slot])

    def ys_copy(b, slot):
        return pltpu.make_async_copy(ys_buf.at[slot], ys_hbm.at[pl.ds(b * bm, bm), :], ys_sem.at[slot])

    def w_copies(j, slot):
        e = uexp_ref[j]
        return (pltpu.make_async_copy(wg_hbm.at[e], wg_buf.at[slot], w_sem.at[slot, 0]),
                pltpu.make_async_copy(wu_hbm.at[e], wu_buf.at[slot], w_sem.at[slot, 1]),
                pltpu.make_async_copy(wd_hbm.at[e], wd_buf.at[slot], w_sem.at[slot, 2]))

    @pl.when(i == 0)
    def _():
        for s in range(_XS_SLOTS - 1):
            @pl.when(s < nused)
            def _():
                xs_copy(s, s).start()
        for s in range(_W_AHEAD):
            @pl.when(s < nexp)
            def _():
                for cp in w_copies(s, s):
                    cp.start()

    def fetch(b):
        ahead = b + _XS_SLOTS - 1

        @pl.when(ahead < nused)
        def _():
            xs_copy(ahead, ahead % _XS_SLOTS).start()

        j = ord_ref[b]

        @pl.when(first_ref[b] == 1)
        def _():
            for cp in w_copies(j, j % _W_SLOTS):
                cp.wait()
            nxt = j + _W_AHEAD

            @pl.when(nxt < nexp)
            def _():
                for cp in w_copies(nxt, nxt % _W_SLOTS):
                    cp.start()

        xs_copy(b, b % _XS_SLOTS).wait()

    def gate_up(b):
        ws = ord_ref[b] % _W_SLOTS
        lo, hi = _unpack_row_words(xs_buf[b % _XS_SLOTS])
        g = (jnp.dot(lo, wg_buf[ws, 0:_ROW_WORDS, :], preferred_element_type=F32)
             + jnp.dot(hi, wg_buf[ws, _ROW_WORDS:, :], preferred_element_type=F32))
        u = (jnp.dot(lo, wu_buf[ws, 0:_ROW_WORDS, :], preferred_element_type=F32)
             + jnp.dot(hi, wu_buf[ws, _ROW_WORDS:, :], preferred_element_type=F32))
        return g * jax.nn.sigmoid(g) * u

    def down(b, act):
        y = jnp.dot(act, wd_buf[ord_ref[b] % _W_SLOTS], preferred_element_type=F32)
        oslot = b % _YS_SLOTS
        ys_buf[oslot] = _pack_row_words(y[:, 0:_ROW_WORDS], y[:, _ROW_WORDS:])
        ys_copy(b, oslot).start()

    @pl.when(jnp.logical_and(i >= _YS_SLOTS + 1, i <= nused))
    def _():
        ys_copy(i - 1 - _YS_SLOTS, (i - 1) % _YS_SLOTS).wait()

    @pl.when(i == 0)
    def _():
        fetch(i)
        act_scr[...] = gate_up(i)

    @pl.when(jnp.logical_and(i >= 1, i < nused))
    def _():
        fetch(i)
        prev = act_scr[...]
        act_scr[...] = gate_up(i)
        down(i - 1, prev)

    @pl.when(i == nused)
    def _():
        down(i - 1, act_scr[...])
        ys_copy(i - 1, (i - 1) % _YS_SLOTS).wait()

        @pl.when(i >= 2)
        def _():
            ys_copy(i - 2, (i - 2) % _YS_SLOTS).wait()


def _experts(first, ordinal, uexp, meta, xs, w_gate, w_up, w_down):
    p = xs.shape[0]
    d = w_gate.shape[1]
    nb = p // BM_EXPERT
    anyspec = pl.BlockSpec(memory_space=pl.ANY)
    return pl.pallas_call(
        _expert_kernel,
        grid_spec=pltpu.PrefetchScalarGridSpec(
            num_scalar_prefetch=4,
            grid=(nb + 1,),
            in_specs=[anyspec, anyspec, anyspec, anyspec],
            out_specs=anyspec,
            scratch_shapes=[pltpu.VMEM((_XS_SLOTS, BM_EXPERT, _ROW_WORDS), U32),
                            pltpu.VMEM((_YS_SLOTS, BM_EXPERT, _ROW_WORDS), U32),
                            pltpu.VMEM((_W_SLOTS, d, EXPERT_DIM), F32),
                            pltpu.VMEM((_W_SLOTS, d, EXPERT_DIM), F32),
                            pltpu.VMEM((_W_SLOTS, EXPERT_DIM, d), F32),
                            pltpu.VMEM((BM_EXPERT, EXPERT_DIM), F32),
                            pltpu.SemaphoreType.DMA((_XS_SLOTS,)), pltpu.SemaphoreType.DMA((_YS_SLOTS,)),
                            pltpu.SemaphoreType.DMA((_W_SLOTS, 3))]),
        out_shape=jax.ShapeDtypeStruct((p, _ROW_WORDS), U32),
        compiler_params=_params(("arbitrary",)),
        name="experts",
    )(first, ordinal, uexp, meta, xs, w_gate, w_up, w_down)


SC_GATHER_ROWS = 64


def _gather_rows_sc(table, idx):
    n = idx.shape[0]
    width = table.shape[1]
    rows = SC_GATHER_ROWS
    per_worker, active = _sc_worker_split(n // rows)
    assert per_worker % 2 == 0 or per_worker == 1
    mesh = plsc.VectorSubcoreMesh(core_axis_name="c", subcore_axis_name="s")

    @functools.partial(
        pl.kernel, mesh=mesh, out_type=jax.ShapeDtypeStruct((n, width), table.dtype),
        scratch_types=[pltpu.VMEM((per_worker, rows), I32), pltpu.VMEM((2, rows, width), table.dtype),
                       pltpu.SemaphoreType.DMA((2,))])
    def gather(table_hbm, idx_hbm, out_hbm, idx_v, rows_v, sem):
        wid = lax.axis_index("s") * SC_CORES + lax.axis_index("c")

        def fetch(c, b):
            return pltpu.make_async_copy(table_hbm.at[idx_v.at[c]], rows_v.at[b], sem.at[b])

        @pl.when(wid < active)
        def _():
            first = wid * per_worker
            pltpu.sync_copy(idx_hbm.at[pl.ds(first, per_worker)], idx_v)
            fetch(0, 0).start()

            @pl.loop(0, per_worker, step=2)
            def _(c):
                for b in range(min(2, per_worker)):
                    cur = c + b

                    @pl.when(cur + 1 < per_worker)
                    def _():
                        fetch(cur + 1, 1 - b).start()

                    fetch(cur, b).wait()
                    pltpu.sync_copy(rows_v.at[b], out_hbm.at[pl.ds((first + cur) * rows, rows)])

    return gather(table, idx.reshape(n // rows, rows))


def _combine_kernel(wts_ref, yg_ref, h2_ref, x1_ref, mod_ref, wsg_ref, wsu_ref, wsd_ref, gpost_ref, *rest):
    o_ref = rest[-1]
    h2 = h2_ref[...]
    g = jnp.dot(h2, wsg_ref[...], preferred_element_type=F32)
    u = jnp.dot(h2, wsu_ref[...], preferred_element_type=F32)
    f = jnp.dot((g * jax.nn.sigmoid(g) * u).astype(BF16), wsd_ref[...], preferred_element_type=F32)

    wts = wts_ref[...]
    los = [f[:, sl * LANES:(sl + 1) * LANES] for sl in range(_ROW_SLABS)]
    his = [f[:, _ROW_WORDS + sl * LANES:_ROW_WORDS + (sl + 1) * LANES] for sl in range(_ROW_SLABS)]
    for k in range(TOP_K):
        wk = wts[:, k:k + 1]
        for sl in range(_ROW_SLABS):
            lo, hi = _unpack_row_words(yg_ref[k, :, sl * LANES:(sl + 1) * LANES])
            los[sl] = los[sl] + wk * lo
            his[sl] = his[sl] + wk * hi
    f = jnp.concatenate(los + his, axis=1)
    g2 = mod_ref[0, 5:6, :]
    o_ref[...] = x1_ref[...] + g2 * _rms(f, gpost_ref[...])


def _combine(wts_t, yg, h2, x1, mod3, wsg, wsu, wsd, gpost, seq, first_tile, partial_out):
    t, d = x1.shape
    tm = min(TM_COMBINE, seq)
    tpb = seq // tm
    full = lambda a: pl.BlockSpec(a.shape, lambda i: (0,) * a.ndim)
    row = lambda w: pl.BlockSpec((tm, w), lambda i: (i + first_tile, 0))
    args = [wts_t, yg, h2, x1, mod3, wsg, wsu, wsd, gpost]
    in_specs = [row(TOP_K), pl.BlockSpec((TOP_K, tm, _ROW_WORDS), lambda i: (0, i, 0)), row(d), row(d),
                pl.BlockSpec((1, 6, d), lambda i: ((i + first_tile) // tpb, 0, 0)),
                full(wsg), full(wsu), full(wsd), full(gpost)]
    aliases = {}
    if partial_out is not None:
        aliases = {len(args): 0}
        args.append(partial_out)
        in_specs.append(pl.BlockSpec(memory_space=pl.ANY))
    return pl.pallas_call(
        _combine_kernel,
        grid=(yg.shape[1] // tm,),
        in_specs=in_specs,
        out_specs=row(d),
        out_shape=jax.ShapeDtypeStruct((t, d), F32),
        input_output_aliases=aliases,
        compiler_params=_params(("arbitrary",)),
        name="combine",
    )(*args)


def _pack_weights(w_in, w_uq, w_ukv):
    d = w_in.shape[0]
    half = QK_ROPE_DIM // 2
    z = lambda n, c: jnp.zeros((n, c), F32)
    o = Q_LORA_RANK + KV_LORA_RANK
    kr = w_in[:, o:o + QK_ROPE_DIM]
    kr_grp = jnp.concatenate([z(d, QK_NOPE_DIM), kr, z(d, HEAD_PAD - QK_NOPE_DIM - QK_ROPE_DIM)], axis=1)
    kr_rot = jnp.concatenate([z(d, QK_NOPE_DIM), -kr[:, half:], kr[:, :half],
                              z(d, HEAD_PAD - QK_NOPE_DIM - QK_ROPE_DIM)], axis=1)
    win_p = jnp.concatenate([w_in[:, :o], kr_grp, kr_rot, w_in[:, o + QK_ROPE_DIM:]], axis=1)

    scale = float(QK_NOPE_DIM + QK_ROPE_DIM) ** -0.5 * float(np.log2(np.e))
    r = Q_LORA_RANK
    qd = QK_NOPE_DIM + QK_ROPE_DIM
    q_grp, q_rot = [], []
    for h in range(MLA_HEADS):
        nope = w_uq[:, h * qd:h * qd + QK_NOPE_DIM]
        rope = w_uq[:, h * qd + QK_NOPE_DIM:(h + 1) * qd]
        pad = z(r, HEAD_PAD - qd)
        q_grp.append(jnp.concatenate([nope, rope, pad], axis=1))
        q_rot.append(jnp.concatenate([z(r, QK_NOPE_DIM), -rope[:, half:], rope[:, :half], pad], axis=1))
    wuq_p = jnp.concatenate(q_grp + q_rot, axis=1) * scale

    c = KV_LORA_RANK
    kd = QK_NOPE_DIM + V_HEAD_DIM
    k_grp, v_grp = [], []
    for h in range(MLA_HEADS):
        k_grp.append(jnp.concatenate([w_ukv[:, h * kd:h * kd + QK_NOPE_DIM], z(c, HEAD_PAD - QK_NOPE_DIM)], axis=1))
        v_grp.append(jnp.concatenate([w_ukv[:, h * kd + QK_NOPE_DIM:(h + 1) * kd], z(c, HEAD_PAD - V_HEAD_DIM)], axis=1))
    wukv_p = jnp.concatenate(k_grp + v_grp, axis=1)
    return win_p.astype(BF16), wuq_p.astype(BF16), wukv_p.astype(BF16)


def _rope_inputs(positions):
    inv = 1.0 / (ROPE_THETA ** (jnp.arange(0, QK_ROPE_DIM, 2, dtype=F32) / QK_ROPE_DIM))
    freq = jnp.tile(inv, LANES // inv.shape[0]).reshape(1, LANES)
    pos = jnp.repeat(positions.astype(F32).reshape(-1, _ROPE_PACK), QK_ROPE_DIM, axis=1)
    return pos, freq


def _layer(x2, c, pos, freq, batch, seq, w_ada, b_ada, g_pre_mix, w_in, g_q_lat, w_uq, g_kv_lat, w_ukv,
           w_conv, g_attn_out, g_conv_out, w_out, g_post_mix, g_pre_ffn, w_router, b_router,
           w_gate, w_up, w_down, w_sh_gate, w_sh_up, w_sh_down, g_post_ffn):
    t, d = x2.shape
    r1 = lambda a: a.reshape(1, -1)

    c_pad = jnp.zeros((SUBLANES, d), F32).at[:batch].set(c)
    mod = _ada(c_pad, w_ada, r1(b_ada))[:batch]
    mod3 = mod.reshape(batch, 6, d)

    win_p, wuq_p, wukv_p = _pack_weights(w_in, w_uq, w_ukv)
    vone = jnp.zeros((1, HEAD_PAD), F32).at[0, V_HEAD_DIM].set(1.0)
    q, k, v, yc = _mix_in(x2, mod3, r1(g_pre_mix), win_p, r1(g_q_lat), wuq_p, r1(g_kv_lat), wukv_p,
                          vone, w_conv, r1(g_conv_out), pos, freq, seq)
    attn = _attention(q, k, v, batch, seq)
    x1, h2, h2p, idx, wts, rank, cnt = _mix_out(
        attn, yc, x2, mod3, r1(g_attn_out), w_out.astype(BF16), r1(g_post_mix), r1(g_pre_ffn),
        w_router.T, b_router.reshape(-1, 1), seq)

    counts = cnt[:, 0].astype(I32)
    padded = ((counts + BM_EXPERT - 1) // BM_EXPERT) * BM_EXPERT
    pad_end = jnp.cumsum(padded)
    pad_start = pad_end - padded
    m = t * TOP_K
    nb = (m + N_EXPERTS * (BM_EXPERT - 1)) // BM_EXPERT
    nused = pad_end[-1] // BM_EXPERT
    bidx = jnp.arange(nb, dtype=I32)
    blk_exp = jnp.sum((pad_end[None, :] <= (bidx * BM_EXPERT)[:, None]).astype(I32), axis=1)
    first = ((bidx < nused) & ((bidx == 0) | (blk_exp != jnp.roll(blk_exp, 1)))).astype(I32)
    ordinal = jnp.maximum(jnp.cumsum(first) - 1, 0).astype(I32)
    seen = jnp.cumsum((counts > 0).astype(I32))
    uexp = jnp.minimum(jnp.sum((seen[None, :] <= jnp.arange(N_EXPERTS, dtype=I32)[:, None]).astype(I32), axis=1),
                       N_EXPERTS - 1).astype(I32)
    meta = jnp.stack([nused, seen[-1]]).astype(I32)

    dest = _dest(idx, rank, pad_start.astype(F32).reshape(-1, 1))
    xs = _scatter_rows_sc(h2p, dest.reshape(TOP_K, -1, SC_ROWS).transpose(1, 0, 2), nb * BM_EXPERT)
    xs = _padfill((pad_start + counts).astype(I32), (padded - counts).astype(I32), xs)
    ys = _experts(first, ordinal, uexp, meta, xs, w_gate, w_up, w_down)
    tm_c = min(TM_COMBINE, seq)
    half_rows = (t // 2) * TOP_K
    even_split = half_rows % (2 * SC_GATHER_ROWS * SC_CORES * SC_SUBCORES) == 0
    parts = 2 if t % (2 * tm_c) == 0 and even_split else 1
    tp = t // parts
    wsg, wsu, wsd = w_sh_gate.astype(BF16), w_sh_up.astype(BF16), w_sh_down.astype(BF16)
    out = None
    for part in range(parts):
        part_dest = dest[:, part * tp:(part + 1) * tp].reshape(-1)
        yg = _gather_rows_sc(ys, part_dest).reshape(TOP_K, tp, _ROW_WORDS)
        out = _combine(wts.T, yg, h2, x1, mod3, wsg, wsu, wsd, r1(g_post_ffn), seq, part * tp // tm_c, out)
    return out


def kernel(x, c, positions, w_ada, b_ada, g_pre_mix, w_in, g_q_lat, w_uq, g_kv_lat, w_ukv, w_conv, g_attn_out, g_conv_out, w_out, g_post_mix, g_pre_ffn, w_router, b_router, w_gate, w_up, w_down, w_sh_gate, w_sh_up, w_sh_down, g_post_ffn):
    batch, seq, d = x.shape
    pos, freq = _rope_inputs(positions)
    x2 = x.reshape(batch * seq, d)
    for l in range(w_ada.shape[0]):
        x2 = _layer(x2, c, pos, freq, batch, seq, w_ada[l], b_ada[l], g_pre_mix[l], w_in[l], g_q_lat[l],
                    w_uq[l], g_kv_lat[l], w_ukv[l], w_conv[l], g_attn_out[l], g_conv_out[l], w_out[l],
                    g_post_mix[l], g_pre_ffn[l], w_router[l], b_router[l], w_gate[l], w_up[l], w_down[l],
                    w_sh_gate[l], w_sh_up[l], w_sh_down[l], g_post_ffn[l])
    return x2.reshape(batch, seq, d)
```

```python
import functools

import jax
import jax.numpy as jnp
import numpy as np
from jax import lax
from jax.experimental import pallas as pl
from jax.experimental.pallas import tpu as pltpu
from jax.experimental.pallas import tpu_sc as plsc

F32 = jnp.float32
BF16 = jnp.bfloat16
I32 = jnp.int32
U32 = jnp.uint32

CHUNK = 64
MLA_HEADS = 8
QK_NOPE_DIM = 64
QK_ROPE_DIM = 32
V_HEAD_DIM = 64
Q_LORA_RANK = 384
KV_LORA_RANK = 256
ROPE_THETA = 10000.0
CONV_WIDTH = 3
N_EXPERTS = 256
TOP_K = 8
N_EXPERT_GROUPS = 8
TOPK_GROUPS = 4
EXPERT_DIM = 256
ROUTED_SCALE = 2.5
EPS = 1e-6

LANES = 128
SUBLANES = 8
HEAD_PAD = LANES
VMEM_LIMIT_BYTES = 56 * 1024 * 1024

TM_IN = 1024
TQ_ATTN = 512
TM_OUT = 512
TM_DEST = 2048
BM_EXPERT = 256
TM_COMBINE = 512

NEG_INF = float("-inf")


def _rms(x, g):
    return x * lax.rsqrt(jnp.mean(x * x, axis=-1, keepdims=True) + EPS) * g


_HI_MASK = np.uint32(0xFFFF0000)
_ROW_WORDS = 512
_ROW_SLABS = _ROW_WORDS // LANES


def _pack_row_words(lo, hi):
    lo_w = lax.bitcast_convert_type(lo.astype(BF16).astype(F32), U32) >> 16
    hi_w = lax.bitcast_convert_type(hi.astype(BF16).astype(F32), U32) & _HI_MASK
    return lo_w | hi_w


def _unpack_row_words(w):
    return (lax.bitcast_convert_type(w << 16, F32), lax.bitcast_convert_type(w & _HI_MASK, F32))


def _params(sem):
    return pltpu.CompilerParams(dimension_semantics=sem, vmem_limit_bytes=VMEM_LIMIT_BYTES)


def _ada_kernel(c_ref, w_ref, b_ref, o_ref):
    c = c_ref[...]
    s = c * jax.nn.sigmoid(c)
    o_ref[...] = jnp.dot(s, w_ref[...], preferred_element_type=F32,
                         precision=lax.Precision.HIGHEST) + b_ref[...]


def _ada(c_pad, w, b):
    rows, d = c_pad.shape
    n = w.shape[1]
    tn = 1536
    return pl.pallas_call(
        _ada_kernel,
        grid=(n // tn,),
        in_specs=[pl.BlockSpec((rows, d), lambda j: (0, 0)),
                  pl.BlockSpec((d, tn), lambda j: (0, j)),
                  pl.BlockSpec((1, tn), lambda j: (0, j))],
        out_specs=pl.BlockSpec((rows, tn), lambda j: (0, j)),
        out_shape=jax.ShapeDtypeStruct((rows, n), F32),
        compiler_params=_params(("arbitrary",)),
        name="ada",
    )(c_pad, w, b)


_CQ0, _CQ1 = 0, Q_LORA_RANK
_CKV0, _CKV1 = _CQ1, _CQ1 + KV_LORA_RANK
_KR0, _KR1 = _CKV1, _CKV1 + 2 * HEAD_PAD
_CONV_DIM = 512
_GB0 = _KR1
_GC0 = _GB0 + _CONV_DIM
_XV0 = _GC0 + _CONV_DIM
_WIN_COLS = _XV0 + _CONV_DIM
_QW = MLA_HEADS * HEAD_PAD
_ROPE_PACK = LANES // QK_ROPE_DIM


def _mix_in_kernel(tiles_per_batch, x_ref, mod_ref, gpre_ref, win_ref, gq_ref, wuq_ref, gkv_ref,
                   wukv_ref, vone_ref, wconv_ref, gconv_ref, pos_ref, freq_ref,
                   q_ref, k_ref, v_ref, yc_ref, h_scr, u_scr, cos_scr, sin_scr):
    i = pl.program_id(0)
    tm = x_ref.shape[0]
    sh1 = mod_ref[0, 0:1, :]
    sc1 = mod_ref[0, 1:2, :]
    h = _rms(x_ref[...], gpre_ref[...]) * (1.0 + sc1) + sh1
    h_scr[...] = h.astype(BF16)
    ang4 = pos_ref[...] * freq_ref[...]
    cos4 = jnp.cos(ang4)
    sin4 = jnp.sin(ang4)
    lane = lax.broadcasted_iota(I32, ang4.shape, 1)
    on_rope = jnp.logical_and(lane >= QK_NOPE_DIM, lane < QK_NOPE_DIM + QK_ROPE_DIM)
    for g in range(_ROPE_PACK):
        shift = (QK_NOPE_DIM - QK_ROPE_DIM * g) % LANES
        cg = cos4 if shift == 0 else pltpu.roll(cos4, shift, axis=1)
        sg = sin4 if shift == 0 else pltpu.roll(sin4, shift, axis=1)
        cos_scr[pl.ds(g, tm // _ROPE_PACK, stride=_ROPE_PACK), :] = jnp.where(on_rope, cg, 1.0)
        sin_scr[pl.ds(g, tm // _ROPE_PACK, stride=_ROPE_PACK), :] = jnp.where(on_rope, sg, 0.0)
    cos = cos_scr[...]
    sin = sin_scr[...]

    cq = jnp.dot(h_scr[...], win_ref[:, _CQ0:_CQ1], preferred_element_type=F32)
    cqn = _rms(cq, gq_ref[...]).astype(BF16)
    pair = 2 * HEAD_PAD
    for lo in range(0, _QW, pair):
        qa = jnp.dot(cqn, wuq_ref[:, lo:lo + pair], preferred_element_type=F32)
        qr = jnp.dot(cqn, wuq_ref[:, _QW + lo:_QW + lo + pair], preferred_element_type=F32)
        for off in range(0, pair, HEAD_PAD):
            qh = qa[:, off:off + HEAD_PAD] * cos + qr[:, off:off + HEAD_PAD] * sin
            q_ref[:, lo + off:lo + off + HEAD_PAD] = qh.astype(BF16)

    ckv = jnp.dot(h_scr[...], win_ref[:, _CKV0:_CKV1], preferred_element_type=F32)
    ckvn = _rms(ckv, gkv_ref[...]).astype(BF16)
    krr = jnp.dot(h_scr[...], win_ref[:, _KR0:_KR1], preferred_element_type=F32)
    kr = krr[:, 0:HEAD_PAD] * cos + krr[:, HEAD_PAD:2 * HEAD_PAD] * sin
    vone = vone_ref[...]
    for lo in range(0, _QW, pair):
        kk = jnp.dot(ckvn, wukv_ref[:, lo:lo + pair], preferred_element_type=F32)
        vv = jnp.dot(ckvn, wukv_ref[:, _QW + lo:_QW + lo + pair], preferred_element_type=F32)
        for off in range(0, pair, HEAD_PAD):
            k_ref[:, lo + off:lo + off + HEAD_PAD] = (kk[:, off:off + HEAD_PAD] + kr).astype(BF16)
            v_ref[:, lo + off:lo + off + HEAD_PAD] = (vv[:, off:off + HEAD_PAD] + vone).astype(BF16)

    assert wconv_ref.shape[0] == CONV_WIDTH
    first = (i % tiles_per_batch) == 0
    chunks = []
    for c0 in range(0, _CONV_DIM, pair):
        cs = slice(c0, c0 + pair)
        gb = jnp.dot(h_scr[...], win_ref[:, _GB0 + c0:_GB0 + c0 + pair], preferred_element_type=F32)
        gc = jnp.dot(h_scr[...], win_ref[:, _GC0 + c0:_GC0 + c0 + pair], preferred_element_type=F32)
        xv = jnp.dot(h_scr[...], win_ref[:, _XV0 + c0:_XV0 + c0 + pair], preferred_element_type=F32)
        u = gc * xv
        prev = u_scr[tm:tm + SUBLANES, cs]
        u_scr[0:SUBLANES, cs] = jnp.where(first, jnp.zeros_like(prev), prev)
        u_scr[SUBLANES:tm + SUBLANES, cs] = u
        um1 = u_scr[SUBLANES - 1:tm + SUBLANES - 1, cs]
        um2 = u_scr[SUBLANES - 2:tm + SUBLANES - 2, cs]
        conv = wconv_ref[0:1, cs] * um2 + wconv_ref[1:2, cs] * um1 + wconv_ref[2:3, cs] * u
        chunks.append(gb * conv)
    ssq = sum(jnp.sum(y * y, axis=-1, keepdims=True) for y in chunks)
    inv = lax.rsqrt(ssq * (1.0 / _CONV_DIM) + EPS)
    for j, y in enumerate(chunks):
        cs = slice(j * pair, (j + 1) * pair)
        yc_ref[:, cs] = (y * inv * gconv_ref[:, cs]).astype(BF16)


def _mix_in(x2, mod3, gpre, win_p, gq, wuq_p, gkv, wukv_p, vone, wconv, gconv, pos, freq, seq):
    t, d = x2.shape
    tm = min(TM_IN, seq)
    tpb = seq // tm
    full = lambda a: pl.BlockSpec(a.shape, lambda i: (0,) * a.ndim)
    row = lambda w: pl.BlockSpec((tm, w), lambda i: (i, 0))
    return pl.pallas_call(
        functools.partial(_mix_in_kernel, tpb),
        grid=(t // tm,),
        in_specs=[row(d),
                  pl.BlockSpec((1, 6, d), lambda i: (i // tpb, 0, 0)),
                  full(gpre), full(win_p), full(gq), full(wuq_p), full(gkv), full(wukv_p),
                  full(vone), full(wconv), full(gconv),
                  pl.BlockSpec((tm // _ROPE_PACK, LANES), lambda i: (i, 0)), full(freq)],
        out_specs=[row(_QW), row(_QW), row(_QW), row(_CONV_DIM)],
        out_shape=[jax.ShapeDtypeStruct((t, _QW), BF16), jax.ShapeDtypeStruct((t, _QW), BF16),
                   jax.ShapeDtypeStruct((t, _QW), BF16), jax.ShapeDtypeStruct((t, _CONV_DIM), BF16)],
        scratch_shapes=[pltpu.VMEM((tm, d), BF16), pltpu.VMEM((tm + SUBLANES, _CONV_DIM), F32),
                        pltpu.VMEM((tm, HEAD_PAD), F32), pltpu.VMEM((tm, HEAD_PAD), F32)],
        compiler_params=_params(("arbitrary",)),
        name="mix_in",
    )(x2, mod3, gpre, win_p, gq, wuq_p, gkv, wukv_p, vone, wconv, gconv, pos, freq)


_HEADS_PER_STEP = 2


def _attn_kernel(tq, q_ref, k_ref, v_ref, o_ref, s_scr, mrun_scr, mb_scr, acc_scr):
    tk = tq
    nq = q_ref.shape[0] // tq
    lane_groups = tk // LANES
    heads = range(_HEADS_PER_STEP)
    lanes = [slice(hh * HEAD_PAD, (hh + 1) * HEAD_PAD) for hh in heads]

    def tile_max(s):
        m = s[:, 0:LANES]
        for g in range(1, lane_groups):
            m = jnp.maximum(m, s[:, g * LANES:(g + 1) * LANES])
        return m

    def scores(hh, qi, kv):
        off = pl.multiple_of(kv * tk, tk)
        return lax.dot_general(q_ref[qi * tq:(qi + 1) * tq, lanes[hh]], k_ref[pl.ds(off, tk), lanes[hh]],
                               (((1,), (1,)), ((), ())), preferred_element_type=F32)

    rc = lax.broadcasted_iota(I32, (tq, tk), 0) // CHUNK
    cc = lax.broadcasted_iota(I32, (tq, tk), 1) // CHUNK

    def diagonal(qi):
        for hh in heads:
            s = jnp.where(cc <= rc, scores(hh, qi, qi), NEG_INF)
            s_scr[hh, qi] = s
            m_row = jnp.max(jnp.maximum(mrun_scr[hh], tile_max(s)), axis=1, keepdims=True)
            mb_scr[hh] = jnp.broadcast_to(m_row, (tq, LANES))

    mrun_scr[...] = jnp.full(mrun_scr.shape, NEG_INF, F32)
    diagonal(0)
    for qi in range(nq):
        has_next = qi + 1 < nq
        acc_scr[...] = jnp.zeros(acc_scr.shape, F32)
        if has_next:
            mrun_scr[...] = jnp.full(mrun_scr.shape, NEG_INF, F32)

        def body(kv, carry, qi=qi, has_next=has_next):
            off = pl.multiple_of(kv * tk, tk)
            for hh in heads:
                mb = mb_scr[hh]
                p = jnp.concatenate(
                    [jnp.exp2(s_scr[hh, kv, :, g * LANES:(g + 1) * LANES] - mb) for g in range(lane_groups)],
                    axis=1).astype(BF16)
                acc_scr[hh] += jnp.dot(p, v_ref[pl.ds(off, tk), lanes[hh]], preferred_element_type=F32)
                if has_next:
                    s = scores(hh, qi + 1, kv)
                    s_scr[hh, kv] = s
                    mrun_scr[hh] = jnp.maximum(mrun_scr[hh], tile_max(s))
            return carry

        lax.fori_loop(0, qi + 1, body, 0, unroll=4)
        for hh in heads:
            acc = acc_scr[hh]
            o = acc[:, 0:V_HEAD_DIM] / acc[:, V_HEAD_DIM:V_HEAD_DIM + 1]
            o_ref[qi * tq:(qi + 1) * tq, hh * V_HEAD_DIM:(hh + 1) * V_HEAD_DIM] = o.astype(BF16)
        if has_next:
            diagonal(qi + 1)


def _attention(q, k, v, batch, seq):
    t = q.shape[0]
    tq = min(TQ_ATTN, seq)
    nq = seq // tq
    hw = _HEADS_PER_STEP * HEAD_PAD
    ow = _HEADS_PER_STEP * V_HEAD_DIM
    blk = lambda w: pl.BlockSpec((seq, w), lambda b, j: (b, j))
    return pl.pallas_call(
        functools.partial(_attn_kernel, tq),
        grid=(batch, MLA_HEADS // _HEADS_PER_STEP),
        in_specs=[blk(hw), blk(hw), blk(hw)],
        out_specs=blk(ow),
        out_shape=jax.ShapeDtypeStruct((t, MLA_HEADS * V_HEAD_DIM), BF16),
        scratch_shapes=[pltpu.VMEM((_HEADS_PER_STEP, nq, tq, tq), F32),
                        pltpu.VMEM((_HEADS_PER_STEP, tq, LANES), F32),
                        pltpu.VMEM((_HEADS_PER_STEP, tq, LANES), F32),
                        pltpu.VMEM((_HEADS_PER_STEP, tq, HEAD_PAD), F32)],
        compiler_params=_params(("arbitrary", "arbitrary")),
        name="attn",
    )(q, k, v)


_GROUP_SIZE = N_EXPERTS // N_EXPERT_GROUPS
_BIG = 1.0e9


def _mix_out_kernel(attn_ref, yc_ref, x_ref, mod_ref, gattn_ref, wout_ref, gpost_ref, gpre2_ref,
                    wrt_ref, br_ref, x1_ref, h2_ref, h2p_ref, idx_ref, wts_ref, rank_ref, cnt_ref,
                    carry_scr, scores_scr, sel_scr):
    i = pl.program_id(0)
    n_tiles = pl.num_programs(0) - 1

    @pl.when(i == 0)
    def _():
        carry_scr[...] = jnp.zeros(carry_scr.shape, F32)
        scores, sel = _project_tile(attn_ref, yc_ref, x_ref, mod_ref, gattn_ref, wout_ref, gpost_ref,
                                    gpre2_ref, wrt_ref, br_ref, x1_ref, h2_ref, h2p_ref)
        scores_scr[0] = scores
        sel_scr[0] = sel

    @pl.when(jnp.logical_and(i >= 1, i < n_tiles))
    def _():
        prev_scores = scores_scr[(i - 1) % 2]
        prev_sel = sel_scr[(i - 1) % 2]
        scores, sel = _project_tile(attn_ref, yc_ref, x_ref, mod_ref, gattn_ref, wout_ref, gpost_ref,
                                    gpre2_ref, wrt_ref, br_ref, x1_ref, h2_ref, h2p_ref)
        scores_scr[i % 2] = scores
        sel_scr[i % 2] = sel
        _route_tile(prev_scores, prev_sel, idx_ref, wts_ref, rank_ref, cnt_ref, carry_scr)

    @pl.when(i == n_tiles)
    def _():
        _route_tile(scores_scr[(i - 1) % 2], sel_scr[(i - 1) % 2], idx_ref, wts_ref, rank_ref, cnt_ref,
                    carry_scr)


def _project_tile(attn_ref, yc_ref, x_ref, mod_ref, gattn_ref, wout_ref, gpost_ref, gpre2_ref,
                  wrt_ref, br_ref, x1_ref, h2_ref, h2p_ref):
    half = attn_ref.shape[1]
    an = _rms(attn_ref[...].astype(F32), gattn_ref[...]).astype(BF16)
    d = x_ref.shape[1]
    step = 2 * LANES
    yc = yc_ref[...]
    mixes = [jnp.dot(an, wout_ref[0:half, c0:c0 + step], preferred_element_type=F32)
             + jnp.dot(yc, wout_ref[half:, c0:c0 + step], preferred_element_type=F32)
             for c0 in range(0, d, step)]
    ssq = sum(jnp.sum(m * m, axis=-1, keepdims=True) for m in mixes)
    inv = lax.rsqrt(ssq * (1.0 / d) + EPS)
    sh2 = mod_ref[0, 3:4, :]
    sc2 = mod_ref[0, 4:5, :]
    x1 = jnp.concatenate(
        [x_ref[:, j * step:(j + 1) * step]
         + mod_ref[0, 2:3, j * step:(j + 1) * step] * (m * inv * gpost_ref[:, j * step:(j + 1) * step])
         for j, m in enumerate(mixes)], axis=1)
    x1_ref[...] = x1
    h2 = _rms(x1, gpre2_ref[...]) * (1.0 + sc2) + sh2
    h2_ref[...] = h2.astype(BF16)
    h2p_ref[...] = _pack_row_words(h2[:, 0:_ROW_WORDS], h2[:, _ROW_WORDS:])

    logits = lax.dot_general(wrt_ref[...], h2, (((1,), (1,)), ((), ())),
                             preferred_element_type=F32, precision=lax.Precision.HIGHEST)
    scores = jax.nn.sigmoid(logits)
    return scores, scores + br_ref[...]


def _route_tile(scores, sel, idx_ref, wts_ref, rank_ref, cnt_ref, carry_scr):
    tm = scores.shape[1]
    row = lax.broadcasted_iota(I32, (N_EXPERTS, tm), 0).astype(F32)

    gscore = []
    rw = lax.broadcasted_iota(I32, (_GROUP_SIZE, tm), 0).astype(F32)
    for g in range(N_EXPERT_GROUPS):
        blk = sel[g * _GROUP_SIZE:(g + 1) * _GROUP_SIZE, :]
        m1 = jnp.max(blk, axis=0, keepdims=True)
        i1 = jnp.min(jnp.where(blk == m1, rw, _BIG), axis=0, keepdims=True)
        m2 = jnp.max(jnp.where(rw == i1, NEG_INF, blk), axis=0, keepdims=True)
        gscore.append(m1 + m2)

    gkeep = [jnp.zeros((1, tm), F32) for _ in range(N_EXPERT_GROUPS)]
    for _ in range(TOPK_GROUPS):
        mg = functools.reduce(jnp.maximum, gscore)
        ig = functools.reduce(jnp.minimum, [jnp.where(gscore[g] == mg, float(g), _BIG)
                                            for g in range(N_EXPERT_GROUPS)])
        for g in range(N_EXPERT_GROUPS):
            hit = ig == float(g)
            gkeep[g] = jnp.where(hit, 1.0, gkeep[g])
            gscore[g] = jnp.where(hit, NEG_INF, gscore[g])
    n_slabs = N_EXPERTS // SUBLANES
    slabs_per_group = _GROUP_SIZE // SUBLANES
    sub = lax.broadcasted_iota(I32, (SUBLANES, tm), 0).astype(F32)
    first_rows = [jnp.where(gkeep[j // slabs_per_group] > 0.0, sel[j * SUBLANES:(j + 1) * SUBLANES, :], NEG_INF)
                  for j in range(n_slabs)]
    cur_rows = list(first_rows)
    krow = lax.broadcasted_iota(I32, (TOP_K, tm), 0)
    idx_rows = []
    idx_f = jnp.zeros((TOP_K, tm), F32)
    sc_k = jnp.zeros((TOP_K, tm), F32)
    sc_sum = jnp.zeros((1, tm), F32)
    prev = None
    for k in range(TOP_K):
        best = jnp.full((SUBLANES, tm), NEG_INF, F32)
        best_slab = jnp.zeros((SUBLANES, tm), F32)
        best_score = jnp.zeros((SUBLANES, tm), F32)
        for j in range(n_slabs):
            if prev is not None:
                cur_rows[j] = jnp.where(sub == prev - float(j * SUBLANES), NEG_INF, cur_rows[j])
            better = cur_rows[j] > best
            best = jnp.where(better, cur_rows[j], best)
            best_slab = jnp.where(better, float(j), best_slab)
            best_score = jnp.where(better, scores[j * SUBLANES:(j + 1) * SUBLANES, :], best_score)
        best_idx = best_slab * float(SUBLANES) + sub
        m = jnp.max(best, axis=0, keepdims=True)
        ik = jnp.min(jnp.where(best == m, best_idx, _BIG), axis=0, keepdims=True)
        sk = jnp.sum(jnp.where(best_idx == ik, best_score, 0.0), axis=0, keepdims=True)
        prev = ik
        idx_rows.append(ik)
        idx_f = jnp.where(krow == k, ik, idx_f)
        sc_k = jnp.where(krow == k, sk, sc_k)
        sc_sum = sc_sum + sk
    wts_ref[...] = sc_k / sc_sum * ROUTED_SCALE
    idx_ref[...] = idx_f.astype(I32)
    onehot = jnp.concatenate(
        [jnp.where(jnp.where(sub == prev - float(j * SUBLANES), NEG_INF, cur_rows[j]) != first_rows[j], 1.0, 0.0)
         for j in range(n_slabs)], axis=0)

    tri = (lax.broadcasted_iota(I32, (tm, tm), 0) < lax.broadcasted_iota(I32, (tm, tm), 1))
    excl = jnp.dot(onehot.astype(BF16), tri.astype(BF16), preferred_element_type=F32)
    rank_e = carry_scr[:, 0:1] + excl
    rank_k = jnp.zeros((TOP_K, tm), F32)
    for k in range(TOP_K):
        hit = row == idx_rows[k]
        rk = jnp.sum(jnp.where(hit, rank_e, 0.0), axis=0, keepdims=True)
        rank_k = jnp.where(krow == k, rk, rank_k)
    rank_ref[...] = rank_k.astype(I32)
    carry_scr[...] = carry_scr[...] + jnp.sum(onehot, axis=1, keepdims=True)
    cnt_ref[...] = carry_scr[...]


def _mix_out(attn, yc, x2, mod3, gattn, wout, gpost, gpre2, wrt, br, seq):
    t, d = x2.shape
    tm = min(TM_OUT, seq)
    tpb = seq // tm
    n_tiles = t // tm
    last = n_tiles - 1
    full = lambda a: pl.BlockSpec(a.shape, lambda i: (0,) * a.ndim)
    row = lambda w: pl.BlockSpec((tm, w), lambda i: (jnp.minimum(i, last), 0))
    col = pl.BlockSpec((TOP_K, tm), lambda i: (0, jnp.maximum(i - 1, 0)))
    return pl.pallas_call(
        _mix_out_kernel,
        grid=(n_tiles + 1,),
        in_specs=[row(attn.shape[1]), row(yc.shape[1]), row(d),
                  pl.BlockSpec((1, 6, d), lambda i: (jnp.minimum(i, last) // tpb, 0, 0)),
                  full(gattn), full(wout), full(gpost), full(gpre2), full(wrt), full(br)],
        out_specs=[row(d), row(d), row(_ROW_WORDS), col, col, col,
                   pl.BlockSpec((N_EXPERTS, LANES), lambda i: (0, 0))],
        out_shape=[jax.ShapeDtypeStruct((t, d), F32), jax.ShapeDtypeStruct((t, d), BF16),
                   jax.ShapeDtypeStruct((t, _ROW_WORDS), U32),
                   jax.ShapeDtypeStruct((TOP_K, t), I32), jax.ShapeDtypeStruct((TOP_K, t), F32),
                   jax.ShapeDtypeStruct((TOP_K, t), I32),
                   jax.ShapeDtypeStruct((N_EXPERTS, LANES), F32)],
        scratch_shapes=[pltpu.VMEM((N_EXPERTS, LANES), F32), pltpu.VMEM((2, N_EXPERTS, tm), F32),
                        pltpu.VMEM((2, N_EXPERTS, tm), F32)],
        compiler_params=_params(("arbitrary",)),
        name="mix_out",
    )(attn, yc, x2, mod3, gattn, wout, gpost, gpre2, wrt, br)


def _dest_kernel(idx_ref, rank_ref, pstart_ref, dest_ref):
    tm = idx_ref.shape[1]
    row = lax.broadcasted_iota(I32, (N_EXPERTS, tm), 0)
    krow = lax.broadcasted_iota(I32, (TOP_K, tm), 0)
    pstart = pstart_ref[...]
    idx = idx_ref[...]
    out = jnp.zeros((TOP_K, tm), F32)
    for k in range(TOP_K):
        hit = row == idx[k:k + 1, :]
        base = jnp.sum(jnp.where(hit, pstart, 0.0), axis=0, keepdims=True)
        out = jnp.where(krow == k, base, out)
    dest_ref[...] = out.astype(I32) + rank_ref[...]


def _dest(idx, rank, pstart):
    t = idx.shape[1]
    tm = min(TM_DEST, t)
    col = pl.BlockSpec((TOP_K, tm), lambda i: (0, i))
    return pl.pallas_call(
        _dest_kernel,
        grid=(t // tm,),
        in_specs=[col, col, pl.BlockSpec((N_EXPERTS, 1), lambda i: (0, 0))],
        out_specs=col,
        out_shape=jax.ShapeDtypeStruct((TOP_K, t), I32),
        compiler_params=_params(("arbitrary",)),
        name="dest",
    )(idx, rank, pstart)


_PAD_CHUNKS = tuple(BM_EXPERT >> s for s in range(1, BM_EXPERT.bit_length()))


SC_CORES = 2
SC_SUBCORES = 16
SC_ROWS = 128


def _sc_worker_split(n_chunks):
    workers = SC_CORES * SC_SUBCORES
    per_worker = max(1, n_chunks // workers)
    active = n_chunks // per_worker
    assert active * per_worker == n_chunks and active <= workers
    return per_worker, active


def _scatter_rows_sc(rows, idx3d, n_out):
    n, width = rows.shape
    n_chunks = n // SC_ROWS
    per_worker, active = _sc_worker_split(n_chunks)
    mesh = plsc.VectorSubcoreMesh(core_axis_name="c", subcore_axis_name="s")

    @functools.partial(
        pl.kernel, mesh=mesh, out_type=jax.ShapeDtypeStruct((n_out, width), rows.dtype),
        scratch_types=[pltpu.VMEM((TOP_K, SC_ROWS), I32), pltpu.VMEM((SC_ROWS, width), rows.dtype),
                       pltpu.SemaphoreType.DMA])
    def scatter(rows_hbm, idx_hbm, out_hbm, idx_v, rows_v, sem):
        wid = lax.axis_index("s") * SC_CORES + lax.axis_index("c")

        @pl.when(wid < active)
        def _():
            @pl.loop(0, per_worker)
            def _(c):
                chunk = wid * per_worker + c
                pltpu.sync_copy(rows_hbm.at[pl.ds(chunk * SC_ROWS, SC_ROWS)], rows_v)
                pltpu.sync_copy(idx_hbm.at[chunk], idx_v)
                copies = [pltpu.make_async_copy(rows_v, out_hbm.at[idx_v.at[k]], sem) for k in range(TOP_K)]
                for cp in copies:
                    cp.start()
                for cp in copies:
                    cp.wait()

    return scatter(rows, idx3d)


def _padfill_kernel(pad_from_ref, pad_n_ref, xs_in_hbm, xs_hbm, zero_scr, pad_sem):
    del xs_in_hbm
    zero_scr[...] = jnp.zeros(zero_scr.shape, zero_scr.dtype)

    def pad_copies(e, act):
        n = pad_n_ref[e]
        base = pad_from_ref[e]

        def single_rows(start, count):
            for j in range(SUBLANES - 1):
                @pl.when(j < count)
                def _():
                    act(pltpu.make_async_copy(zero_scr.at[pl.ds(0, 1), :],
                                              xs_hbm.at[pl.ds(start + j, 1), :], pad_sem))

        head = jnp.minimum(n, (SUBLANES - (base & (SUBLANES - 1))) & (SUBLANES - 1))
        single_rows(base, head)
        rest = n - head
        mid = base + head
        for rows in _PAD_CHUNKS:
            if rows >= SUBLANES:
                @pl.when((rest & rows) != 0)
                def _():
                    start = pl.multiple_of(mid + (rest & ~(2 * rows - 1)), SUBLANES)
                    act(pltpu.make_async_copy(zero_scr.at[pl.ds(0, rows), :],
                                              xs_hbm.at[pl.ds(start, rows), :], pad_sem))
        single_rows(mid + (rest & ~(SUBLANES - 1)), rest & (SUBLANES - 1))

    def issue_pad(e, carry):
        pad_copies(e, lambda cp: cp.start())
        return carry

    def drain_pad(e, carry):
        pad_copies(e, lambda cp: cp.wait())
        return carry

    lax.fori_loop(0, N_EXPERTS, issue_pad, 0)
    lax.fori_loop(0, N_EXPERTS, drain_pad, 0)


def _padfill(pad_from, pad_n, xs):
    return pl.pallas_call(
        _padfill_kernel,
        grid_spec=pltpu.PrefetchScalarGridSpec(
            num_scalar_prefetch=2,
            grid=(1,),
            in_specs=[pl.BlockSpec(memory_space=pl.ANY)],
            out_specs=pl.BlockSpec(memory_space=pl.ANY),
            scratch_shapes=[pltpu.VMEM((BM_EXPERT // 2, _ROW_WORDS), U32), pltpu.SemaphoreType.DMA]),
        out_shape=jax.ShapeDtypeStruct(xs.shape, xs.dtype),
        input_output_aliases={2: 0},
        compiler_params=_params(("arbitrary",)),
        name="padfill",
    )(pad_from, pad_n, xs)


_XS_SLOTS = 4
_YS_SLOTS = 2
_W_SLOTS = 4
_W_AHEAD = 2


def _expert_kernel(first_ref, ord_ref, uexp_ref, meta_ref, xs_hbm, wg_hbm, wu_hbm, wd_hbm, ys_hbm,
                   xs_buf, ys_buf, wg_buf, wu_buf, wd_buf, act_scr, xs_sem, ys_sem, w_sem):
    i = pl.program_id(0)
    nused = meta_ref[0]
    nexp = meta_ref[1]
    bm = xs_buf.shape[1]

    def xs_copy(b, slot):
        return pltpu.make_async_copy(xs_hbm.at[pl.ds(b * bm, bm), :], xs_buf.at[slot], xs_sem.at[slot])

    def ys_copy(b, slot):
        return pltpu.make_async_copy(ys_buf.at[slot], ys_hbm.at[pl.ds(b * bm, bm), :], ys_sem.at[slot])

    def w_copies(j, slot):
        e = uexp_ref[j]
        return (pltpu.make_async_copy(wg_hbm.at[e], wg_buf.at[slot], w_sem.at[slot, 0]),
                pltpu.make_async_copy(wu_hbm.at[e], wu_buf.at[slot], w_sem.at[slot, 1]),
                pltpu.make_async_copy(wd_hbm.at[e], wd_buf.at[slot], w_sem.at[slot, 2]))

    @pl.when(i == 0)
    def _():
        for s in range(_XS_SLOTS - 1):
            @pl.when(s < nused)
            def _():
                xs_copy(s, s).start()
        for s in range(_W_AHEAD):
            @pl.when(s < nexp)
            def _():
                for cp in w_copies(s, s):
                    cp.start()

    def fetch(b):
        ahead = b + _XS_SLOTS - 1

        @pl.when(ahead < nused)
        def _():
            xs_copy(ahead, ahead % _XS_SLOTS).start()

        j = ord_ref[b]

        @pl.when(first_ref[b] == 1)
        def _():
            for cp in w_copies(j, j % _W_SLOTS):
                cp.wait()
            nxt = j + _W_AHEAD

            @pl.when(nxt < nexp)
            def _():
                for cp in w_copies(nxt, nxt % _W_SLOTS):
                    cp.start()

        xs_copy(b, b % _XS_SLOTS).wait()

    def gate_up(b):
        ws = ord_ref[b] % _W_SLOTS
        lo, hi = _unpack_row_words(xs_buf[b % _XS_SLOTS])
        g = (jnp.dot(lo, wg_buf[ws, 0:_ROW_WORDS, :], preferred_element_type=F32)
             + jnp.dot(hi, wg_buf[ws, _ROW_WORDS:, :], preferred_element_type=F32))
        u = (jnp.dot(lo, wu_buf[ws, 0:_ROW_WORDS, :], preferred_element_type=F32)
             + jnp.dot(hi, wu_buf[ws, _ROW_WORDS:, :], preferred_element_type=F32))
        return g * jax.nn.sigmoid(g) * u

    def down(b, act):
        y = jnp.dot(act, wd_buf[ord_ref[b] % _W_SLOTS], preferred_element_type=F32)
        oslot = b % _YS_SLOTS
        ys_buf[oslot] = _pack_row_words(y[:, 0:_ROW_WORDS], y[:, _ROW_WORDS:])
        ys_copy(b, oslot).start()

    @pl.when(jnp.logical_and(i >= _YS_SLOTS + 1, i <= nused))
    def _():
        ys_copy(i - 1 - _YS_SLOTS, (i - 1) % _YS_SLOTS).wait()

    @pl.when(i == 0)
    def _():
        fetch(i)
        act_scr[...] = gate_up(i)

    @pl.when(jnp.logical_and(i >= 1, i < nused))
    def _():
        fetch(i)
        prev = act_scr[...]
        act_scr[...] = gate_up(i)
        down(i - 1, prev)

    @pl.when(i == nused)
    def _():
        down(i - 1, act_scr[...])
        ys_copy(i - 1, (i - 1) % _YS_SLOTS).wait()

        @pl.when(i >= 2)
        def _():
            ys_copy(i - 2, (i - 2) % _YS_SLOTS).wait()


def _experts(first, ordinal, uexp, meta, xs, w_gate, w_up, w_down):
    p = xs.shape[0]
    d = w_gate.shape[1]
    nb = p // BM_EXPERT
    anyspec = pl.BlockSpec(memory_space=pl.ANY)
    return pl.pallas_call(
        _expert_kernel,
        grid_spec=pltpu.PrefetchScalarGridSpec(
            num_scalar_prefetch=4,
            grid=(nb + 1,),
            in_specs=[anyspec, anyspec, anyspec, anyspec],
            out_specs=anyspec,
            scratch_shapes=[pltpu.VMEM((_XS_SLOTS, BM_EXPERT, _ROW_WORDS), U32),
                            pltpu.VMEM((_YS_SLOTS, BM_EXPERT, _ROW_WORDS), U32),
                            pltpu.VMEM((_W_SLOTS, d, EXPERT_DIM), F32),
                            pltpu.VMEM((_W_SLOTS, d, EXPERT_DIM), F32),
                            pltpu.VMEM((_W_SLOTS, EXPERT_DIM, d), F32),
                            pltpu.VMEM((BM_EXPERT, EXPERT_DIM), F32),
                            pltpu.SemaphoreType.DMA((_XS_SLOTS,)), pltpu.SemaphoreType.DMA((_YS_SLOTS,)),
                            pltpu.SemaphoreType.DMA((_W_SLOTS, 3))]),
        out_shape=jax.ShapeDtypeStruct((p, _ROW_WORDS), U32),
        compiler_params=_params(("arbitrary",)),
        name="experts",
    )(first, ordinal, uexp, meta, xs, w_gate, w_up, w_down)


SC_GATHER_ROWS = 64


def _gather_rows_sc(table, idx):
    n = idx.shape[0]
    width = table.shape[1]
    rows = SC_GATHER_ROWS
    per_worker, active = _sc_worker_split(n // rows)
    assert per_worker % 2 == 0 or per_worker == 1
    mesh = plsc.VectorSubcoreMesh(core_axis_name="c", subcore_axis_name="s")

    @functools.partial(
        pl.kernel, mesh=mesh, out_type=jax.ShapeDtypeStruct((n, width), table.dtype),
        scratch_types=[pltpu.VMEM((per_worker, rows), I32), pltpu.VMEM((2, rows, width), table.dtype),
                       pltpu.SemaphoreType.DMA((2,))])
    def gather(table_hbm, idx_hbm, out_hbm, idx_v, rows_v, sem):
        wid = lax.axis_index("s") * SC_CORES + lax.axis_index("c")

        def fetch(c, b):
            return pltpu.make_async_copy(table_hbm.at[idx_v.at[c]], rows_v.at[b], sem.at[b])

        @pl.when(wid < active)
        def _():
            first = wid * per_worker
            pltpu.sync_copy(idx_hbm.at[pl.ds(first, per_worker)], idx_v)
            fetch(0, 0).start()

            @pl.loop(0, per_worker, step=2)
            def _(c):
                for b in range(min(2, per_worker)):
                    cur = c + b

                    @pl.when(cur + 1 < per_worker)
                    def _():
                        fetch(cur + 1, 1 - b).start()

                    fetch(cur, b).wait()
                    pltpu.sync_copy(rows_v.at[b], out_hbm.at[pl.ds((first + cur) * rows, rows)])

    return gather(table, idx.reshape(n // rows, rows))


def _combine_kernel(wts_ref, yg_ref, h2_ref, x1_ref, mod_ref, wsg_ref, wsu_ref, wsd_ref, gpost_ref, *rest):
    o_ref = rest[-1]
    h2 = h2_ref[...]
    g = jnp.dot(h2, wsg_ref[...], preferred_element_type=F32)
    u = jnp.dot(h2, wsu_ref[...], preferred_element_type=F32)
    f = jnp.dot((g * jax.nn.sigmoid(g) * u).astype(BF16), wsd_ref[...], preferred_element_type=F32)

    wts = wts_ref[...]
    los = [f[:, sl * LANES:(sl + 1) * LANES] for sl in range(_ROW_SLABS)]
    his = [f[:, _ROW_WORDS + sl * LANES:_ROW_WORDS + (sl + 1) * LANES] for sl in range(_ROW_SLABS)]
    for k in range(TOP_K):
        wk = wts[:, k:k + 1]
        for sl in range(_ROW_SLABS):
            lo, hi = _unpack_row_words(yg_ref[k, :, sl * LANES:(sl + 1) * LANES])
            los[sl] = los[sl] + wk * lo
            his[sl] = his[sl] + wk * hi
    f = jnp.concatenate(los + his, axis=1)
    g2 = mod_ref[0, 5:6, :]
    o_ref[...] = x1_ref[...] + g2 * _rms(f, gpost_ref[...])


def _combine(wts_t, yg, h2, x1, mod3, wsg, wsu, wsd, gpost, seq, first_tile, partial_out):
    t, d = x1.shape
    tm = min(TM_COMBINE, seq)
    tpb = seq // tm
    full = lambda a: pl.BlockSpec(a.shape, lambda i: (0,) * a.ndim)
    row = lambda w: pl.BlockSpec((tm, w), lambda i: (i + first_tile, 0))
    args = [wts_t, yg, h2, x1, mod3, wsg, wsu, wsd, gpost]
    in_specs = [row(TOP_K), pl.BlockSpec((TOP_K, tm, _ROW_WORDS), lambda i: (0, i, 0)), row(d), row(d),
                pl.BlockSpec((1, 6, d), lambda i: ((i + first_tile) // tpb, 0, 0)),
                full(wsg), full(wsu), full(wsd), full(gpost)]
    aliases = {}
    if partial_out is not None:
        aliases = {len(args): 0}
        args.append(partial_out)
        in_specs.append(pl.BlockSpec(memory_space=pl.ANY))
    return pl.pallas_call(
        _combine_kernel,
        grid=(yg.shape[1] // tm,),
        in_specs=in_specs,
        out_specs=row(d),
        out_shape=jax.ShapeDtypeStruct((t, d), F32),
        input_output_aliases=aliases,
        compiler_params=_params(("arbitrary",)),
        name="combine",
    )(*args)


def _pack_weights(w_in, w_uq, w_ukv):
    d = w_in.shape[0]
    half = QK_ROPE_DIM // 2
    z = lambda n, c: jnp.zeros((n, c), F32)
    o = Q_LORA_RANK + KV_LORA_RANK
    kr = w_in[:, o:o + QK_ROPE_DIM]
    kr_grp = jnp.concatenate([z(d, QK_NOPE_DIM), kr, z(d, HEAD_PAD - QK_NOPE_DIM - QK_ROPE_DIM)], axis=1)
    kr_rot = jnp.concatenate([z(d, QK_NOPE_DIM), -kr[:, half:], kr[:, :half],
                              z(d, HEAD_PAD - QK_NOPE_DIM - QK_ROPE_DIM)], axis=1)
    win_p = jnp.concatenate([w_in[:, :o], kr_grp, kr_rot, w_in[:, o + QK_ROPE_DIM:]], axis=1)

    scale = float(QK_NOPE_DIM + QK_ROPE_DIM) ** -0.5 * float(np.log2(np.e))
    r = Q_LORA_RANK
    qd = QK_NOPE_DIM + QK_ROPE_DIM
    q_grp, q_rot = [], []
    for h in range(MLA_HEADS):
        nope = w_uq[:, h * qd:h * qd + QK_NOPE_DIM]
        rope = w_uq[:, h * qd + QK_NOPE_DIM:(h + 1) * qd]
        pad = z(r, HEAD_PAD - qd)
        q_grp.append(jnp.concatenate([nope, rope, pad], axis=1))
        q_rot.append(jnp.concatenate([z(r, QK_NOPE_DIM), -rope[:, half:], rope[:, :half], pad], axis=1))
    wuq_p = jnp.concatenate(q_grp + q_rot, axis=1) * scale

    c = KV_LORA_RANK
    kd = QK_NOPE_DIM + V_HEAD_DIM
    k_grp, v_grp = [], []
    for h in range(MLA_HEADS):
        k_grp.append(jnp.concatenate([w_ukv[:, h * kd:h * kd + QK_NOPE_DIM], z(c, HEAD_PAD - QK_NOPE_DIM)], axis=1))
        v_grp.append(jnp.concatenate([w_ukv[:, h * kd + QK_NOPE_DIM:(h + 1) * kd], z(c, HEAD_PAD - V_HEAD_DIM)], axis=1))
    wukv_p = jnp.concatenate(k_grp + v_grp, axis=1)
    return win_p.astype(BF16), wuq_p.astype(BF16), wukv_p.astype(BF16)


def _rope_inputs(positions):
    inv = 1.0 / (ROPE_THETA ** (jnp.arange(0, QK_ROPE_DIM, 2, dtype=F32) / QK_ROPE_DIM))
    freq = jnp.tile(inv, LANES // inv.shape[0]).reshape(1, LANES)
    pos = jnp.repeat(positions.astype(F32).reshape(-1, _ROPE_PACK), QK_ROPE_DIM, axis=1)
    return pos, freq


def _layer(x2, c, pos, freq, batch, seq, w_ada, b_ada, g_pre_mix, w_in, g_q_lat, w_uq, g_kv_lat, w_ukv,
           w_conv, g_attn_out, g_conv_out, w_out, g_post_mix, g_pre_ffn, w_router, b_router,
           w_gate, w_up, w_down, w_sh_gate, w_sh_up, w_sh_down, g_post_ffn):
    t, d = x2.shape
    r1 = lambda a: a.reshape(1, -1)

    c_pad = jnp.zeros((SUBLANES, d), F32).at[:batch].set(c)
    mod = _ada(c_pad, w_ada, r1(b_ada))[:batch]
    mod3 = mod.reshape(batch, 6, d)

    win_p, wuq_p, wukv_p = _pack_weights(w_in, w_uq, w_ukv)
    vone = jnp.zeros((1, HEAD_PAD), F32).at[0, V_HEAD_DIM].set(1.0)
    q, k, v, yc = _mix_in(x2, mod3, r1(g_pre_mix), win_p, r1(g_q_lat), wuq_p, r1(g_kv_lat), wukv_p,
                          vone, w_conv, r1(g_conv_out), pos, freq, seq)
    attn = _attention(q, k, v, batch, seq)
    x1, h2, h2p, idx, wts, rank, cnt = _mix_out(
        attn, yc, x2, mod3, r1(g_attn_out), w_out.astype(BF16), r1(g_post_mix), r1(g_pre_ffn),
        w_router.T, b_router.reshape(-1, 1), seq)

    counts = cnt[:, 0].astype(I32)
    padded = ((counts + BM_EXPERT - 1) // BM_EXPERT) * BM_EXPERT
    pad_end = jnp.cumsum(padded)
    pad_start = pad_end - padded
    m = t * TOP_K
    nb = (m + N_EXPERTS * (BM_EXPERT - 1)) // BM_EXPERT
    nused = pad_end[-1] // BM_EXPERT
    bidx = jnp.arange(nb, dtype=I32)
    blk_exp = jnp.sum((pad_end[None, :] <= (bidx * BM_EXPERT)[:, None]).astype(I32), axis=1)
    first = ((bidx < nused) & ((bidx == 0) | (blk_exp != jnp.roll(blk_exp, 1)))).astype(I32)
    ordinal = jnp.maximum(jnp.cumsum(first) - 1, 0).astype(I32)
    seen = jnp.cumsum((counts > 0).astype(I32))
    uexp = jnp.minimum(jnp.sum((seen[None, :] <= jnp.arange(N_EXPERTS, dtype=I32)[:, None]).astype(I32), axis=1),
                       N_EXPERTS - 1).astype(I32)
    meta = jnp.stack([nused, seen[-1]]).astype(I32)

    dest = _dest(idx, rank, pad_start.astype(F32).reshape(-1, 1))
    xs = _scatter_rows_sc(h2p, dest.reshape(TOP_K, -1, SC_ROWS).transpose(1, 0, 2), nb * BM_EXPERT)
    xs = _padfill((pad_start + counts).astype(I32), (padded - counts).astype(I32), xs)
    ys = _experts(first, ordinal, uexp, meta, xs, w_gate, w_up, w_down)
    tm_c = min(TM_COMBINE, seq)
    half_rows = (t // 2) * TOP_K
    even_split = half_rows % (2 * SC_GATHER_ROWS * SC_CORES * SC_SUBCORES) == 0
    parts = 2 if t % (2 * tm_c) == 0 and even_split else 1
    tp = t // parts
    wsg, wsu, wsd = w_sh_gate.astype(BF16), w_sh_up.astype(BF16), w_sh_down.astype(BF16)
    out = None
    for part in range(parts):
        part_dest = dest[:, part * tp:(part + 1) * tp].reshape(-1)
        yg = _gather_rows_sc(ys, part_dest).reshape(TOP_K, tp, _ROW_WORDS)
        out = _combine(wts.T, yg, h2, x1, mod3, wsg, wsu, wsd, r1(g_post_ffn), seq, part * tp // tm_c, out)
    return out


def kernel(x, c, positions, w_ada, b_ada, g_pre_mix, w_in, g_q_lat, w_uq, g_kv_lat, w_ukv, w_conv, g_attn_out, g_conv_out, w_out, g_post_mix, g_pre_ffn, w_router, b_router, w_gate, w_up, w_down, w_sh_gate, w_sh_up, w_sh_down, g_post_ffn):
    batch, seq, d = x.shape
    pos, freq = _rope_inputs(positions)
    x2 = x.reshape(batch * seq, d)
    for l in range(w_ada.shape[0]):
        x2 = _layer(x2, c, pos, freq, batch, seq, w_ada[l], b_ada[l], g_pre_mix[l], w_in[l], g_q_lat[l],
                    w_uq[l], g_kv_lat[l], w_ukv[l], w_conv[l], g_attn_out[l], g_conv_out[l], w_out[l],
                    g_post_mix[l], g_pre_ffn[l], w_router[l], b_router[l], w_gate[l], w_up[l], w_down[l],
                    w_sh_gate[l], w_sh_up[l], w_sh_down[l], g_post_ffn[l])
    return x2.reshape(batch, seq, d)
```
